```python
import jax, jax.numpy as jnp
from jax import lax
import numpy as np

D_MODEL = 1024
BATCH = 8
SEQ = 8192
DEPTH = 1

SGU_WIDTH = D_MODEL
SGU_GROUPS = 8
SGU_GROUP_DIM = SGU_WIDTH // SGU_GROUPS
CHUNK = 128
HEAD_DIM = 64
N_Q_HEADS = D_MODEL // HEAD_DIM
N_KV_HEADS = N_Q_HEADS // 4
Q_PER_KV = N_Q_HEADS // N_KV_HEADS
WINDOW = 128
BLOCK = 128
ROPE_DIM = HEAD_DIM // 4
ROPE_THETA = 500000.0
ATTN_WIDTH = N_Q_HEADS * HEAD_DIM
KV_WIDTH = N_KV_HEADS * HEAD_DIM
D_FF = 4 * D_MODEL
EPS = 1e-6

IN_SPLITS = (SGU_WIDTH, SGU_WIDTH, ATTN_WIDTH, KV_WIDTH, KV_WIDTH, D_MODEL, D_MODEL)
IN_WIDTH = sum(IN_SPLITS)

kernel_name = "hybrid_gated_sgu_swa_sink_block"


def rms_norm(x, g):
    xf = x.astype(jnp.float32)
    y = xf * lax.rsqrt(jnp.mean(xf * xf, axis=-1, keepdims=True) + EPS)
    return (y * g.astype(jnp.float32)).astype(x.dtype)


def layer_norm(x, g, b):
    xf = x.astype(jnp.float32)
    mu = jnp.mean(xf, axis=-1, keepdims=True)
    var = jnp.mean(jnp.square(xf - mu), axis=-1, keepdims=True)
    y = (xf - mu) * lax.rsqrt(var + EPS)
    return (y * g.astype(jnp.float32) + b.astype(jnp.float32)).astype(x.dtype)


def partial_rotary(x, positions):
    inv_freq = ROPE_THETA ** (-jnp.arange(0, ROPE_DIM, 2, dtype=jnp.float32) / ROPE_DIM)
    ang = positions.astype(jnp.float32)[..., None] * inv_freq
    cos = jnp.cos(ang)[:, :, None, :]
    sin = jnp.sin(ang)[:, :, None, :]
    xr = x[..., :ROPE_DIM].astype(jnp.float32)
    x1, x2 = xr[..., : ROPE_DIM // 2], xr[..., ROPE_DIM // 2:]
    rot = jnp.concatenate([x1 * cos - x2 * sin, x2 * cos + x1 * sin], axis=-1).astype(x.dtype)
    return jnp.concatenate([rot, x[..., ROPE_DIM:]], axis=-1)


def chunked_spatial_gating(u, v, ln_g, ln_b, w_s, b_s):
    B, S, _ = v.shape
    n_chunks = S // CHUNK
    vn = layer_norm(v, ln_g, ln_b)
    vc = vn.reshape(B, n_chunks, CHUNK, SGU_GROUPS, SGU_GROUP_DIM)
    causal = jnp.tril(jnp.ones((CHUNK, CHUNK), dtype=bool))
    w = jnp.where(causal[None], w_s, jnp.zeros_like(w_s))
    mixed = jnp.einsum('gts,bcsge->bctge', w, vc) + b_s.T[None, None, :, :, None]
    return u * mixed.reshape(B, S, SGU_WIDTH)


def sliding_window_sink_attention(q, k, v, sinks):
    B, S = q.shape[0], q.shape[1]
    n_blocks = S // BLOCK
    qb = q.reshape(B, n_blocks, BLOCK, N_KV_HEADS, Q_PER_KV, HEAD_DIM)
    kb = k.reshape(B, n_blocks, BLOCK, N_KV_HEADS, HEAD_DIM)
    vb = v.reshape(B, n_blocks, BLOCK, N_KV_HEADS, HEAD_DIM)
    pad = ((0, 0), (1, 0), (0, 0), (0, 0), (0, 0))
    k_band = jnp.concatenate([jnp.pad(kb, pad)[:, :-1], kb], axis=2)
    v_band = jnp.concatenate([jnp.pad(vb, pad)[:, :-1], vb], axis=2)
    scores = jnp.einsum('bnqgrd,bnkgd->bngrqk', qb, k_band).astype(jnp.float32) * (HEAD_DIM ** -0.5)
    blk = jnp.arange(n_blocks)[:, None]
    qpos = blk * BLOCK + jnp.arange(BLOCK)[None, :]
    kpos = (blk - 1) * BLOCK + jnp.arange(2 * BLOCK)[None, :]
    diff = qpos[:, :, None] - kpos[:, None, :]
    allowed = (diff >= 0) & (diff < WINDOW) & (kpos[:, None, :] >= 0)
    scores = jnp.where(allowed[None, :, None, None], scores, jnp.float32(-1e30))
    sink = sinks.astype(jnp.float32).reshape(N_KV_HEADS, Q_PER_KV)[None, None, :, :, None, None]
    m = jnp.maximum(jnp.max(scores, axis=-1, keepdims=True), sink)
    p = jnp.exp(scores - m)
    denom = jnp.sum(p, axis=-1, keepdims=True) + jnp.exp(sink - m)
    p = (p / denom).astype(v.dtype)
    out = jnp.einsum('bngrqk,bnkgd->bnqgrd', p, v_band)
    return out.reshape(B, S, ATTN_WIDTH)


def hybrid_layer(x, positions, w_in, ln_v_gain, ln_v_bias, w_spatial, b_spatial, sinks,
                 w_a, w_b, w_o, norm_mix_pre, norm_mix_post,
                 w_ff_in, w_ff_out, norm_ff_pre, norm_ff_post):
    B, S, _ = x.shape
    h = rms_norm(x, norm_mix_pre)
    proj = h @ w_in
    offs = np.cumsum(IN_SPLITS)[:-1].tolist()
    u, v_sgu, q, k, v_att, gate_a, gate_b = jnp.split(proj, offs, axis=-1)
    a = chunked_spatial_gating(jax.nn.gelu(u), jax.nn.gelu(v_sgu), ln_v_gain, ln_v_bias, w_spatial, b_spatial)
    q = partial_rotary(q.reshape(B, S, N_Q_HEADS, HEAD_DIM), positions)
    k = partial_rotary(k.reshape(B, S, N_KV_HEADS, HEAD_DIM), positions)
    v_att = v_att.reshape(B, S, N_KV_HEADS, HEAD_DIM)
    att = sliding_window_sink_attention(q, k, v_att, sinks)
    merged = jax.nn.sigmoid(gate_a) * (a @ w_a) + jax.nn.sigmoid(gate_b) * (att @ w_b)
    x = x + rms_norm(merged @ w_o, norm_mix_post)
    hf = rms_norm(x, norm_ff_pre)
    ff = jnp.square(jax.nn.relu(hf @ w_ff_in)) @ w_ff_out
    return x + rms_norm(ff, norm_ff_post)


def _fwd_setup_inputs(seed: int = 0) -> dict:
    key = jax.random.key(seed)
    ks = jax.random.split(key, 20)
    f32 = jnp.float32
    def nrm(k, shape, scale):
        return jax.random.normal(k, shape, dtype=f32) * scale
    def gain(k, shape):
        return 1.0 + 0.05 * jax.random.normal(k, shape, dtype=f32)
    x = jax.random.normal(ks[0], (BATCH, SEQ, D_MODEL), dtype=f32)
    offset = jax.random.randint(ks[1], (BATCH, 1), 0, 4096, dtype=jnp.int32)
    positions = offset + jnp.arange(SEQ, dtype=jnp.int32)[None, :]
    return {
        "x": x,
        "positions": positions,
        "w_in": nrm(ks[2], (DEPTH, D_MODEL, IN_WIDTH), D_MODEL ** -0.5),
        "ln_v_gain": gain(ks[3], (DEPTH, SGU_WIDTH)),
        "ln_v_bias": nrm(ks[4], (DEPTH, SGU_WIDTH), 0.02),
        "w_spatial": nrm(ks[5], (DEPTH, SGU_GROUPS, CHUNK, CHUNK), CHUNK ** -0.5),
        "b_spatial": gain(ks[6], (DEPTH, SGU_GROUPS, CHUNK)),
        "sinks": nrm(ks[7], (DEPTH, N_Q_HEADS), 0.5),
        "w_a": nrm(ks[8], (DEPTH, SGU_WIDTH, D_MODEL), SGU_WIDTH ** -0.5),
        "w_b": nrm(ks[9], (DEPTH, ATTN_WIDTH, D_MODEL), ATTN_WIDTH ** -0.5),
        "w_o": nrm(ks[10], (DEPTH, D_MODEL, D_MODEL), D_MODEL ** -0.5),
        "norm_mix_pre": gain(ks[11], (DEPTH, D_MODEL)),
        "norm_mix_post": gain(ks[12], (DEPTH, D_MODEL)),
        "w_ff_in": nrm(ks[13], (DEPTH, D_MODEL, D_FF), D_MODEL ** -0.5),
        "w_ff_out": nrm(ks[14], (DEPTH, D_FF, D_MODEL), D_FF ** -0.5),
        "norm_ff_pre": gain(ks[15], (DEPTH, D_MODEL)),
        "norm_ff_post": gain(ks[16], (DEPTH, D_MODEL)),
    }


def _fwd_reference(x, positions, w_in, ln_v_gain, ln_v_bias, w_spatial, b_spatial, sinks,
              w_a, w_b, w_o, norm_mix_pre, norm_mix_post,
              w_ff_in, w_ff_out, norm_ff_pre, norm_ff_post):
    for l in range(DEPTH):
        x = hybrid_layer(x, positions, w_in[l], ln_v_gain[l], ln_v_bias[l], w_spatial[l], b_spatial[l],
                         sinks[l], w_a[l], w_b[l], w_o[l], norm_mix_pre[l], norm_mix_post[l],
                         w_ff_in[l], w_ff_out[l], norm_ff_pre[l], norm_ff_post[l])
    return x


import jax as _jax
import jax.numpy as _jnp

TWIN_FORMAT = 'train_step'
FWD_PARAMS = ['x', 'positions', 'w_in', 'ln_v_gain', 'ln_v_bias', 'w_spatial', 'b_spatial', 'sinks', 'w_a', 'w_b', 'w_o', 'norm_mix_pre', 'norm_mix_post', 'w_ff_in', 'w_ff_out', 'norm_ff_pre', 'norm_ff_post']
TWIN_WEIGHTS = ['w_in', 'ln_v_gain', 'ln_v_bias', 'w_spatial', 'b_spatial', 'sinks', 'w_a', 'w_b', 'w_o', 'norm_mix_pre', 'norm_mix_post', 'w_ff_in', 'w_ff_out', 'norm_ff_pre', 'norm_ff_post']
TWIN_DIFF_INPUT = 'x'
TWIN_INPUTS = ['x', 'positions', 'w_in', 'ln_v_gain', 'ln_v_bias', 'w_spatial', 'b_spatial', 'sinks', 'w_a', 'w_b', 'w_o', 'norm_mix_pre', 'norm_mix_post', 'w_ff_in', 'w_ff_out', 'norm_ff_pre', 'norm_ff_post', 'loss_target', 'm_w_in', 'm_ln_v_gain', 'm_ln_v_bias', 'm_w_spatial', 'm_b_spatial', 'm_sinks', 'm_w_a', 'm_w_b', 'm_w_o', 'm_norm_mix_pre', 'm_norm_mix_post', 'm_w_ff_in', 'm_w_ff_out', 'm_norm_ff_pre', 'm_norm_ff_post', 'v_w_in', 'v_ln_v_gain', 'v_ln_v_bias', 'v_w_spatial', 'v_b_spatial', 'v_sinks', 'v_w_a', 'v_w_b', 'v_w_o', 'v_norm_mix_pre', 'v_norm_mix_post', 'v_w_ff_in', 'v_w_ff_out', 'v_norm_ff_pre', 'v_norm_ff_post']
TWIN_OUTPUTS = ['loss', 'grad_x', 'grad_w_in', 'grad_ln_v_gain', 'grad_ln_v_bias', 'grad_w_spatial', 'grad_b_spatial', 'grad_sinks', 'grad_w_a', 'grad_w_b', 'grad_w_o', 'grad_norm_mix_pre', 'grad_norm_mix_post', 'grad_w_ff_in', 'grad_w_ff_out', 'grad_norm_ff_pre', 'grad_norm_ff_post', 'delta_w_in', 'delta_ln_v_gain', 'delta_ln_v_bias', 'delta_w_spatial', 'delta_b_spatial', 'delta_sinks', 'delta_w_a', 'delta_w_b', 'delta_w_o', 'delta_norm_mix_pre', 'delta_norm_mix_post', 'delta_w_ff_in', 'delta_w_ff_out', 'delta_norm_ff_pre', 'delta_norm_ff_post', 'new_m_w_in', 'new_m_ln_v_gain', 'new_m_ln_v_bias', 'new_m_w_spatial', 'new_m_b_spatial', 'new_m_sinks', 'new_m_w_a', 'new_m_w_b', 'new_m_w_o', 'new_m_norm_mix_pre', 'new_m_norm_mix_post', 'new_m_w_ff_in', 'new_m_w_ff_out', 'new_m_norm_ff_pre', 'new_m_norm_ff_post', 'new_v_w_in', 'new_v_ln_v_gain', 'new_v_ln_v_bias', 'new_v_w_spatial', 'new_v_b_spatial', 'new_v_sinks', 'new_v_w_a', 'new_v_w_b', 'new_v_w_o', 'new_v_norm_mix_pre', 'new_v_norm_mix_post', 'new_v_w_ff_in', 'new_v_w_ff_out', 'new_v_norm_ff_pre', 'new_v_norm_ff_post']
TWIN_LEAF_KINDS = {'loss': 'loss', 'grad_x': 'grad_x', 'grad_w_in': 'grad_w', 'grad_ln_v_gain': 'grad_w', 'grad_ln_v_bias': 'grad_w', 'grad_w_spatial': 'grad_w', 'grad_b_spatial': 'grad_w', 'grad_sinks': 'grad_w', 'grad_w_a': 'grad_w', 'grad_w_b': 'grad_w', 'grad_w_o': 'grad_w', 'grad_norm_mix_pre': 'grad_w', 'grad_norm_mix_post': 'grad_w', 'grad_w_ff_in': 'grad_w', 'grad_w_ff_out': 'grad_w', 'grad_norm_ff_pre': 'grad_w', 'grad_norm_ff_post': 'grad_w', 'delta_w_in': 'delta_w', 'delta_ln_v_gain': 'delta_w', 'delta_ln_v_bias': 'delta_w', 'delta_w_spatial': 'delta_w', 'delta_b_spatial': 'delta_w', 'delta_sinks': 'delta_w', 'delta_w_a': 'delta_w', 'delta_w_b': 'delta_w', 'delta_w_o': 'delta_w', 'delta_norm_mix_pre': 'delta_w', 'delta_norm_mix_post': 'delta_w', 'delta_w_ff_in': 'delta_w', 'delta_w_ff_out': 'delta_w', 'delta_norm_ff_pre': 'delta_w', 'delta_norm_ff_post': 'delta_w', 'new_m_w_in': 'new_m', 'new_m_ln_v_gain': 'new_m', 'new_m_ln_v_bias': 'new_m', 'new_m_w_spatial': 'new_m', 'new_m_b_spatial': 'new_m', 'new_m_sinks': 'new_m', 'new_m_w_a': 'new_m', 'new_m_w_b': 'new_m', 'new_m_w_o': 'new_m', 'new_m_norm_mix_pre': 'new_m', 'new_m_norm_mix_post': 'new_m', 'new_m_w_ff_in': 'new_m', 'new_m_w_ff_out': 'new_m', 'new_m_norm_ff_pre': 'new_m', 'new_m_norm_ff_post': 'new_m', 'new_v_w_in': 'new_v', 'new_v_ln_v_gain': 'new_v', 'new_v_ln_v_bias': 'new_v', 'new_v_w_spatial': 'new_v', 'new_v_b_spatial': 'new_v', 'new_v_sinks': 'new_v', 'new_v_w_a': 'new_v', 'new_v_w_b': 'new_v', 'new_v_w_o': 'new_v', 'new_v_norm_mix_pre': 'new_v', 'new_v_norm_mix_post': 'new_v', 'new_v_w_ff_in': 'new_v', 'new_v_w_ff_out': 'new_v', 'new_v_norm_ff_pre': 'new_v', 'new_v_norm_ff_post': 'new_v'}


def _forward(args):
    return _fwd_reference(*[args[k] for k in FWD_PARAMS])


def _output_shape():
    def fwd():
        inp = _fwd_setup_inputs(0)
        return _fwd_reference(*[inp[k] for k in FWD_PARAMS])
    out = _jax.eval_shape(fwd)
    return out.shape, out.dtype

N_MICROBATCH = 1
ADAM_LR = 0.001
ADAM_B1 = 0.9
ADAM_B2 = 0.999
ADAM_EPS = 1e-08
ADAM_WD = 0.01
ADAM_STEP = 10
PER_EXAMPLE_BATCH_AXIS = {'x': 0, 'positions': 0, 'loss_target': 0}
SHARED_INPUTS = []
_WEIGHT_DTYPES = {'w_in': _jnp.float32, 'ln_v_gain': _jnp.float32, 'ln_v_bias': _jnp.float32, 'w_spatial': _jnp.float32, 'b_spatial': _jnp.float32, 'sinks': _jnp.float32, 'w_a': _jnp.float32, 'w_b': _jnp.float32, 'w_o': _jnp.float32, 'norm_mix_pre': _jnp.float32, 'norm_mix_post': _jnp.float32, 'w_ff_in': _jnp.float32, 'w_ff_out': _jnp.float32, 'norm_ff_pre': _jnp.float32, 'norm_ff_post': _jnp.float32}
MOMENT_SCALE = {'w_in': 5.060009e-01, 'ln_v_gain': 4.331684e-01, 'ln_v_bias': 5.292043e-01, 'w_spatial': 3.136731e-01, 'b_spatial': 5.449728e-01, 'sinks': 1.089692e-01, 'w_a': 9.398909e+00, 'w_b': 2.235494e-01, 'w_o': 9.706372e+00, 'norm_mix_pre': 1.246123e+00, 'norm_mix_post': 6.539125e+01, 'w_ff_in': 1.825947e+00, 'w_ff_out': 9.318762e+00, 'norm_ff_pre': 3.613616e+00, 'norm_ff_post': 6.712530e+01}


def _to_microbatches(a, axis):
    t = _jnp.moveaxis(a, axis, 0)
    t = t.reshape((N_MICROBATCH, t.shape[0] // N_MICROBATCH) + t.shape[1:])
    return _jnp.moveaxis(t, 1, axis + 1)


def setup_inputs(seed: int = 0) -> dict:
    inp = _fwd_setup_inputs(seed)
    key = _jax.random.fold_in(_jax.random.key(seed), 7919)
    shape, _ = _output_shape()
    out = dict(inp)
    out["loss_target"] = _jax.random.normal(_jax.random.fold_in(key, 0), shape, _jnp.float32)
    for i, name in enumerate(TWIN_WEIGHTS):
        w = inp[name].astype(_jnp.float32)
        if MOMENT_SCALE is None:
            s = _jnp.sqrt(_jnp.mean(_jnp.square(w)) + 1e-30)
        else:
            s = MOMENT_SCALE[name]
        km, kv = _jax.random.split(_jax.random.fold_in(key, i + 1))
        out[name] = w
        out["m_" + name] = s * _jax.random.normal(km, w.shape, _jnp.float32)
        out["v_" + name] = (s * s) * _jax.random.uniform(kv, w.shape, _jnp.float32, 0.5, 1.5)
    if N_MICROBATCH > 1:
        for name, axis in PER_EXAMPLE_BATCH_AXIS.items():
            out[name] = _to_microbatches(out[name], axis)
    return {'x': out['x'], 'positions': out['positions'], 'w_in': out['w_in'], 'ln_v_gain': out['ln_v_gain'], 'ln_v_bias': out['ln_v_bias'], 'w_spatial': out['w_spatial'], 'b_spatial': out['b_spatial'], 'sinks': out['sinks'], 'w_a': out['w_a'], 'w_b': out['w_b'], 'w_o': out['w_o'], 'norm_mix_pre': out['norm_mix_pre'], 'norm_mix_post': out['norm_mix_post'], 'w_ff_in': out['w_ff_in'], 'w_ff_out': out['w_ff_out'], 'norm_ff_pre': out['norm_ff_pre'], 'norm_ff_post': out['norm_ff_post'], 'loss_target': out['loss_target'], 'm_w_in': out['m_w_in'], 'm_ln_v_gain': out['m_ln_v_gain'], 'm_ln_v_bias': out['m_ln_v_bias'], 'm_w_spatial': out['m_w_spatial'], 'm_b_spatial': out['m_b_spatial'], 'm_sinks': out['m_sinks'], 'm_w_a': out['m_w_a'], 'm_w_b': out['m_w_b'], 'm_w_o': out['m_w_o'], 'm_norm_mix_pre': out['m_norm_mix_pre'], 'm_norm_mix_post': out['m_norm_mix_post'], 'm_w_ff_in': out['m_w_ff_in'], 'm_w_ff_out': out['m_w_ff_out'], 'm_norm_ff_pre': out['m_norm_ff_pre'], 'm_norm_ff_post': out['m_norm_ff_post'], 'v_w_in': out['v_w_in'], 'v_ln_v_gain': out['v_ln_v_gain'], 'v_ln_v_bias': out['v_ln_v_bias'], 'v_w_spatial': out['v_w_spatial'], 'v_b_spatial': out['v_b_spatial'], 'v_sinks': out['v_sinks'], 'v_w_a': out['v_w_a'], 'v_w_b': out['v_w_b'], 'v_w_o': out['v_w_o'], 'v_norm_mix_pre': out['v_norm_mix_pre'], 'v_norm_mix_post': out['v_norm_mix_post'], 'v_w_ff_in': out['v_w_ff_in'], 'v_w_ff_out': out['v_w_ff_out'], 'v_norm_ff_pre': out['v_norm_ff_pre'], 'v_norm_ff_post': out['v_norm_ff_post']}


def _loss(weights, diff, rest, loss_target):
    with _jax.named_scope("forward"):
        args = {**rest, TWIN_DIFF_INPUT: diff, **{k: w.astype(_WEIGHT_DTYPES[k]) for k, w in weights.items()}}
        y = _forward(args)
    with _jax.named_scope("loss_head"):
        err = _jnp.square(y.astype(_jnp.float32) - loss_target)
        return 0.5 * _jnp.sum(_jnp.mean(err, axis=-1)) if err.ndim else 0.5 * err


def _adamw(w, g, m, v):
    m = ADAM_B1 * m + (1.0 - ADAM_B1) * g
    v = ADAM_B2 * v + (1.0 - ADAM_B2) * _jnp.square(g)
    m_hat = m / (1.0 - ADAM_B1 ** ADAM_STEP)
    v_hat = v / (1.0 - ADAM_B2 ** ADAM_STEP)
    delta = -ADAM_LR * (m_hat / (_jnp.sqrt(v_hat) + ADAM_EPS) + ADAM_WD * w)
    return delta, m, v


def reference(x, positions, w_in, ln_v_gain, ln_v_bias, w_spatial, b_spatial, sinks, w_a, w_b, w_o, norm_mix_pre, norm_mix_post, w_ff_in, w_ff_out, norm_ff_pre, norm_ff_post, loss_target, m_w_in, m_ln_v_gain, m_ln_v_bias, m_w_spatial, m_b_spatial, m_sinks, m_w_a, m_w_b, m_w_o, m_norm_mix_pre, m_norm_mix_post, m_w_ff_in, m_w_ff_out, m_norm_ff_pre, m_norm_ff_post, v_w_in, v_ln_v_gain, v_ln_v_bias, v_w_spatial, v_b_spatial, v_sinks, v_w_a, v_w_b, v_w_o, v_norm_mix_pre, v_norm_mix_post, v_w_ff_in, v_w_ff_out, v_norm_ff_pre, v_norm_ff_post):
    given = dict(x=x, positions=positions, w_in=w_in, ln_v_gain=ln_v_gain, ln_v_bias=ln_v_bias, w_spatial=w_spatial, b_spatial=b_spatial, sinks=sinks, w_a=w_a, w_b=w_b, w_o=w_o, norm_mix_pre=norm_mix_pre, norm_mix_post=norm_mix_post, w_ff_in=w_ff_in, w_ff_out=w_ff_out, norm_ff_pre=norm_ff_pre, norm_ff_post=norm_ff_post, loss_target=loss_target, m_w_in=m_w_in, m_ln_v_gain=m_ln_v_gain, m_ln_v_bias=m_ln_v_bias, m_w_spatial=m_w_spatial, m_b_spatial=m_b_spatial, m_sinks=m_sinks, m_w_a=m_w_a, m_w_b=m_w_b, m_w_o=m_w_o, m_norm_mix_pre=m_norm_mix_pre, m_norm_mix_post=m_norm_mix_post, m_w_ff_in=m_w_ff_in, m_w_ff_out=m_w_ff_out, m_norm_ff_pre=m_norm_ff_pre, m_norm_ff_post=m_norm_ff_post, v_w_in=v_w_in, v_ln_v_gain=v_ln_v_gain, v_ln_v_bias=v_ln_v_bias, v_w_spatial=v_w_spatial, v_b_spatial=v_b_spatial, v_sinks=v_sinks, v_w_a=v_w_a, v_w_b=v_w_b, v_w_o=v_w_o, v_norm_mix_pre=v_norm_mix_pre, v_norm_mix_post=v_norm_mix_post, v_w_ff_in=v_w_ff_in, v_w_ff_out=v_w_ff_out, v_norm_ff_pre=v_norm_ff_pre, v_norm_ff_post=v_norm_ff_post)
    weights = {n: given[n] for n in TWIN_WEIGHTS}
    shared = {n: given[n] for n in SHARED_INPUTS}
    per_example = {n: given[n] for n in ['x', 'positions']}
    grad_fn = _jax.value_and_grad(_loss, argnums=(0, 1))

    def one_microbatch(ex, loss_target):
        ex = dict(ex)
        diff = ex.pop(TWIN_DIFF_INPUT)
        return grad_fn(weights, diff, {**shared, **ex}, loss_target)

    if N_MICROBATCH == 1:
        loss, (grad_w, grad_x) = one_microbatch(per_example, given["loss_target"])
    else:
        def body(carry, xs):
            loss_sum, grad_sum = carry
            l_k, (gw_k, gx_k) = one_microbatch(xs[0], xs[1])
            with _jax.named_scope("update"):
                return (loss_sum + l_k, _jax.tree.map(_jnp.add, grad_sum, gw_k)), gx_k

        init = (_jnp.zeros((), _jnp.float32), _jax.tree.map(_jnp.zeros_like, weights))
        (loss, grad_w), grad_x = _jax.lax.scan(body, init, (per_example, given["loss_target"]))
    with _jax.named_scope("update"):
        delta_w, new_m, new_v = {}, {}, {}
        for n in TWIN_WEIGHTS:
            delta_w[n], new_m[n], new_v[n] = _adamw(weights[n], grad_w[n], given["m_" + n], given["v_" + n])
    return (loss, grad_x, *[grad_w[n] for n in TWIN_WEIGHTS], *[delta_w[n] for n in TWIN_WEIGHTS],
            *[new_m[n] for n in TWIN_WEIGHTS], *[new_v[n] for n in TWIN_WEIGHTS])
```

```python
import functools

import jax
import jax.numpy as jnp
from jax import lax
from jax.experimental import pallas as pl
from jax.experimental.pallas import tpu as pltpu

F32 = jnp.float32
BF16 = jnp.bfloat16

D = 1024
CH = 128
NG = 8
HD = 64
NQ = 16
NKV = 4
KVW = NKV * HD
DFF = 4 * D
EPS = 1e-6
IN_W = 5632
SEG = (0, 1024, 2048, 3072, 3328, 3584, 4608, 5632)
ROPE_HALF = 8
Q_SCALE = HD ** -0.5

LR, B1, B2, AEPS, WD, STEP = 0.001, 0.9, 0.999, 1e-08, 0.01, 10

VMEM_LIMIT = 56 * 1024 * 1024
MESH = pl.DeviceIdType.MESH

_GELU_C0 = 0.7978845608028654
_GELU_C1 = 0.044715


def _cparams(sem=None):
    kw = dict(vmem_limit_bytes=VMEM_LIMIT)
    if sem is not None:
        kw["dimension_semantics"] = sem
    return pltpu.CompilerParams(**kw)


def _resident(shape):
    nd = len(shape)
    return pl.BlockSpec(shape, lambda *_: (0,) * nd, pipeline_mode=pl.Buffered(1))


def _const(shape):
    nd = len(shape)
    return pl.BlockSpec(shape, lambda *_: (0,) * nd)


def _rows(tm, w):
    return pl.BlockSpec((tm, w), lambda i: (i, 0))


def _gelu(x):
    t = jnp.tanh(_GELU_C0 * (x + _GELU_C1 * (x * x * x)))
    return 0.5 * x * (1.0 + t), t


def _gelu_grad(x, t):
    return 0.5 * (1.0 + t) + 0.5 * x * (1.0 - t * t) * (_GELU_C0 * (1.0 + 3.0 * _GELU_C1 * x * x))


def _sigmoid(x):
    return 1.0 / (1.0 + jnp.exp(-x))


def _rms_hat(x):
    r = lax.rsqrt(jnp.mean(x * x, axis=-1, keepdims=True) + EPS)
    return x * r, r


def _rms_bwd(xhat, r, g, dout):
    dg = jnp.sum(dout * xhat, axis=0, keepdims=True)
    dy = dout * g
    dx = r * (dy - xhat * jnp.mean(dy * xhat, axis=-1, keepdims=True))
    return dx, dg


def _dot(a, b):
    return jnp.dot(a, b, preferred_element_type=F32)


def _dot_nt(a, b):
    return lax.dot_general(a, b, (((1,), (1,)), ((), ())), preferred_element_type=F32)


def _dot_tn(a, b):
    return lax.dot_general(a, b, (((0,), (0,)), ((), ())), preferred_element_type=F32)


def _rope(blk, c, s1, s2):
    return blk * c + pltpu.roll(blk, CH - ROPE_HALF, 1) * s1 + pltpu.roll(blk, ROPE_HALF, 1) * s2


def _rope_t(blk, c, s1, s2):
    return blk * c + pltpu.roll(blk * s1, ROPE_HALF, 1) + pltpu.roll(blk * s2, CH - ROPE_HALF, 1)


def _inproj(x, g1, w_in, rc, rs1, rs2, tm):
    T = x.shape[0]

    def body(x_ref, g_ref, w_ref, c_ref, s1_ref, s2_ref,
             h_ref, u_ref, v_ref, q_ref, k_ref, va_ref, ga_ref, gb_ref):
        xhat, _ = _rms_hat(x_ref[...])
        h = (xhat * g_ref[...]).astype(BF16)
        h_ref[...] = h
        u_ref[...] = _dot(h, w_ref[:, SEG[0]:SEG[1]])
        v_ref[...] = _dot(h, w_ref[:, SEG[1]:SEG[2]])
        c, s1, s2 = c_ref[...], s1_ref[...], s2_ref[...]
        q = _dot(h, w_ref[:, SEG[2]:SEG[3]])
        for p in range(D // CH):
            blk = _rope(q[:, CH * p:CH * (p + 1)], c, s1, s2) * Q_SCALE
            q_ref[:, CH * p:CH * (p + 1)] = blk.astype(BF16)
        k = _dot(h, w_ref[:, SEG[3]:SEG[4]])
        for p in range(KVW // CH):
            k_ref[:, CH * p:CH * (p + 1)] = _rope(k[:, CH * p:CH * (p + 1)], c, s1, s2).astype(BF16)
        va_ref[...] = _dot(h, w_ref[:, SEG[4]:SEG[5]]).astype(BF16)
        ga_ref[...] = _dot(h, w_ref[:, SEG[5]:SEG[6]])
        gb_ref[...] = _dot(h, w_ref[:, SEG[6]:SEG[7]])

    sd = jax.ShapeDtypeStruct
    return pl.pallas_call(
        body, name="inproj_fwd", grid=(T // tm,),
        in_specs=[_rows(tm, D), _const((1, D)), _resident((D, IN_W)), _rows(tm, CH), _rows(tm, CH), _rows(tm, CH)],
        out_specs=[_rows(tm, D), _rows(tm, D), _rows(tm, D), _rows(tm, D), _rows(tm, KVW), _rows(tm, KVW),
                   _rows(tm, D), _rows(tm, D)],
        out_shape=[sd((T, D), BF16), sd((T, D), F32), sd((T, D), F32), sd((T, D), BF16), sd((T, KVW), BF16),
                   sd((T, KVW), BF16), sd((T, D), F32), sd((T, D), F32)],
        compiler_params=_cparams(("parallel",)),
    )(x, g1, w_in, rc, rs1, rs2)


def _sgu_common(u, vs, lng, lnb, ws_ref, bfull):
    nc = u.shape[0] // CH
    ug, tu = _gelu(u)
    vg, tv = _gelu(vs)
    mu = jnp.mean(vg, axis=-1, keepdims=True)
    xc = vg - mu
    rstd = lax.rsqrt(jnp.mean(xc * xc, axis=-1, keepdims=True) + EPS)
    vhat = xc * rstd
    vnb = (vhat * lng + lnb).astype(BF16)
    tri = lax.broadcasted_iota(jnp.int32, (CH, CH), 0) >= lax.broadcasted_iota(jnp.int32, (CH, CH), 1)
    wts, rhss, mixed = [], [], []
    for g in range(NG):
        wt = jnp.where(tri, ws_ref[g], 0.0).astype(BF16)
        rhs = jnp.concatenate([vnb[CH * c:CH * (c + 1), CH * g:CH * (g + 1)] for c in range(nc)], axis=1)
        mix = _dot(wt, rhs)
        wts.append(wt)
        rhss.append(rhs)
        mixed.append([mix[:, CH * c:CH * (c + 1)] + bfull[:, CH * g:CH * (g + 1)] for c in range(nc)])
    return nc, ug, tu, tv, rstd, vhat, tri, wts, rhss, mixed


def _sgu_fwd(u, vs, lng, lnb, ws, bfull, tm):
    T = u.shape[0]

    def body(u_ref, v_ref, lng_ref, lnb_ref, ws_ref, bf_ref, a_ref):
        nc, ug, _, _, _, _, _, _, _, mixed = _sgu_common(
            u_ref[...], v_ref[...], lng_ref[...], lnb_ref[...], ws_ref, bf_ref[...])
        for g in range(NG):
            for c in range(nc):
                a_ref[CH * c:CH * (c + 1), CH * g:CH * (g + 1)] = (
                    ug[CH * c:CH * (c + 1), CH * g:CH * (g + 1)] * mixed[g][c]).astype(BF16)

    return pl.pallas_call(
        body, name="sgu_fwd", grid=(T // tm,),
        in_specs=[_rows(tm, D), _rows(tm, D), _const((1, D)), _const((1, D)), _const((NG, CH, CH)), _const((CH, D))],
        out_specs=_rows(tm, D), out_shape=jax.ShapeDtypeStruct((T, D), BF16),
        compiler_params=_cparams(("parallel",)),
    )(u, vs, lng, lnb, ws, bfull)


def _sgu_bwd(u, vs, da, lng, lnb, ws, bfull, tm):
    T = u.shape[0]
    nsteps = T // tm

    def body(u_ref, v_ref, da_ref, lng_ref, lnb_ref, ws_ref, bf_ref,
             du_ref, dv_ref, dws_ref, dbs_ref, dlg_ref, dlb_ref, db_ref):
        i = pl.program_id(0)
        u, vs, da, lng = u_ref[...], v_ref[...], da_ref[...], lng_ref[...]
        nc, ug, tu, tv, rstd, vhat, tri, wts, rhss, mixed = _sgu_common(u, vs, lng, lnb_ref[...], ws_ref, bf_ref[...])

        @pl.when(i == 0)
        def _():
            dws_ref[...] = jnp.zeros_like(dws_ref)
            db_ref[...] = jnp.zeros_like(db_ref)
            dlg_ref[...] = jnp.zeros_like(dlg_ref)
            dlb_ref[...] = jnp.zeros_like(dlb_ref)

        gu = _gelu_grad(u, tu)
        dvn_cols = []
        for g in range(NG):
            dmix = []
            for c in range(nc):
                sl = (slice(CH * c, CH * (c + 1)), slice(CH * g, CH * (g + 1)))
                da_b = da[sl]
                du_ref[sl] = (da_b * mixed[g][c] * gu[sl]).astype(BF16)
                dmix.append(da_b * ug[sl])
            db_ref[:, CH * g:CH * (g + 1)] += functools.reduce(lambda a, b: a + b, dmix)
            dm = jnp.concatenate(dmix, axis=1).astype(BF16)
            dws_ref[g] += _dot_nt(dm, rhss[g])
            dvn_cols.append(_dot_tn(wts[g], dm))
        dvn = jnp.concatenate(
            [jnp.concatenate([dvn_cols[g][:, CH * c:CH * (c + 1)] for g in range(NG)], axis=1) for c in range(nc)],
            axis=0)
        dlg_ref[...] += jnp.sum(dvn * vhat, axis=0, keepdims=True)
        dlb_ref[...] += jnp.sum(dvn, axis=0, keepdims=True)
        dvh = dvn * lng
        dvg = rstd * (dvh - jnp.mean(dvh, axis=-1, keepdims=True)
                      - vhat * jnp.mean(dvh * vhat, axis=-1, keepdims=True))
        dv_ref[...] = (dvg * _gelu_grad(vs, tv)).astype(BF16)

        @pl.when(i == nsteps - 1)
        def _():
            for g in range(NG):
                dws_ref[g] = jnp.where(tri, dws_ref[g], 0.0)
                dbs_ref[g:g + 1, :] = jnp.sum(db_ref[:, CH * g:CH * (g + 1)].T, axis=0, keepdims=True)

    sd = jax.ShapeDtypeStruct
    return pl.pallas_call(
        body, name="sgu_bwd", grid=(nsteps,),
        in_specs=[_rows(tm, D), _rows(tm, D), _rows(tm, D), _const((1, D)), _const((1, D)), _const((NG, CH, CH)),
                  _const((CH, D))],
        out_specs=[_rows(tm, D), _rows(tm, D), _const((NG, CH, CH)), _const((NG, CH)), _const((1, D)), _const((1, D))],
        out_shape=[sd((T, D), BF16), sd((T, D), BF16), sd((NG, CH, CH), F32), sd((NG, CH), F32), sd((1, D), F32),
                   sd((1, D), F32)],
        scratch_shapes=[pltpu.VMEM((CH, D), F32)],
        compiler_params=_cparams(("arbitrary",)),
    )(u, vs, da, lng, lnb, ws, bfull)


def _pair_layout(prev, cur, grp):
    j, half = grp // 2, grp % 2
    blk = jnp.concatenate([prev[:, CH * j:CH * (j + 1)], cur[:, CH * j:CH * (j + 1)]], axis=0).astype(F32)
    lo = lax.broadcasted_iota(jnp.int32, blk.shape, 1) < HD
    rolled = pltpu.roll(blk, HD, 1)
    even = jnp.where(lo, blk if half == 0 else rolled, 0.0)
    odd = jnp.where(lo, 0.0, rolled if half == 0 else blk)
    return jnp.concatenate([even, odd], axis=0).astype(BF16)


def _unpair(acc, grp):
    half = grp % 2
    lo = lax.broadcasted_iota(jnp.int32, (2 * CH, CH), 1) < HD
    ev, od = acc[:2 * CH], acc[2 * CH:]
    if half == 0:
        return jnp.where(lo, ev + pltpu.roll(od, HD, 1), 0.0)
    return jnp.where(lo, 0.0, pltpu.roll(ev, HD, 1) + od)


def _attn_mask(n):
    qi = lax.broadcasted_iota(jnp.int32, (CH, 2 * CH), 0)
    kc = lax.broadcasted_iota(jnp.int32, (CH, 2 * CH), 1)
    ok = (kc > qi) & (kc <= qi + CH) & ((kc >= CH) | (n > 0))
    return jnp.concatenate([ok, ok], axis=1)


def _softmax_sink(s, sink):
    m = jnp.maximum(jnp.max(s, axis=-1, keepdims=True), sink)
    p = jnp.exp(s - m)
    ps = jnp.exp(sink - m)
    inv = 1.0 / (jnp.sum(p, axis=-1, keepdims=True) + ps)
    return p * inv, ps * inv


def _attn_fwd(q, k, va, sinks):
    T = q.shape[0]
    nb = T // CH

    def body(sk_ref, q_ref, kp_ref, kc_ref, vp_ref, vc_ref, o_ref):
        n = pl.program_id(0)
        mask = _attn_mask(n)
        kp, kc, vp, vc = kp_ref[...], kc_ref[...], vp_ref[...], vc_ref[...]
        for grp in range(NKV):
            kk = _pair_layout(kp, kc, grp)
            vv = _pair_layout(vp, vc, grp)
            for pp in range(2):
                p = 2 * grp + pp
                s = jnp.where(mask, _dot_nt(q_ref[:, CH * p:CH * (p + 1)], kk), -1e30)
                pe, _ = _softmax_sink(s[:, :2 * CH], sk_ref[2 * p])
                po, _ = _softmax_sink(s[:, 2 * CH:], sk_ref[2 * p + 1])
                pr = jnp.concatenate([pe, po], axis=1).astype(BF16)
                o_ref[:, CH * p:CH * (p + 1)] = _dot(pr, vv).astype(BF16)

    prev = lambda n: (jnp.maximum(n - 1, 0), 0)
    cur = lambda n: (n, 0)
    return pl.pallas_call(
        body, name="attn_fwd", grid=(nb,),
        in_specs=[pl.BlockSpec(memory_space=pltpu.SMEM), pl.BlockSpec((CH, D), cur),
                  pl.BlockSpec((CH, KVW), prev), pl.BlockSpec((CH, KVW), cur),
                  pl.BlockSpec((CH, KVW), prev), pl.BlockSpec((CH, KVW), cur)],
        out_specs=pl.BlockSpec((CH, D), cur), out_shape=jax.ShapeDtypeStruct((T, D), BF16),
        compiler_params=_cparams(("parallel",)),
    )(sinks, q, k, k, va, va)


def _attn_bwd(q, k, va, datt, sinks, rc, rs1, rs2):
    T = q.shape[0]
    nb = T // CH

    def body(sk_ref, q_ref, kp_ref, kc_ref, vp_ref, vc_ref, do_ref, cq_ref, s1q_ref, s2q_ref, ck_ref, s1k_ref, s2k_ref,
             dq_ref, dk_ref, dv_ref, dsk_ref, kcar, vcar):
        n = pl.program_id(0)

        @pl.when(n == 0)
        def _():
            kcar[...] = jnp.zeros_like(kcar)
            vcar[...] = jnp.zeros_like(vcar)
            dsk_ref[...] = jnp.zeros_like(dsk_ref)

        def flush(kprev, vprev):
            ck, s1k, s2k = ck_ref[...], s1k_ref[...], s2k_ref[...]
            for j in range(KVW // CH):
                sl = slice(CH * j, CH * (j + 1))
                dk_ref[:, sl] = _rope_t(kcar[:, sl] + kprev[:, sl], ck, s1k, s2k).astype(BF16)
                dv_ref[:, sl] = (vcar[:, sl] + vprev[:, sl]).astype(BF16)

        @pl.when(n < nb)
        def _():
            mask = _attn_mask(n)
            kp, kc, vp, vc = kp_ref[...], kc_ref[...], vp_ref[...], vc_ref[...]
            cq, s1q, s2q = cq_ref[...], s1q_ref[...], s2q_ref[...]
            lane = lax.broadcasted_iota(jnp.int32, (1, CH), 1)
            dsk = jnp.zeros((1, CH), F32)
            dk_cols, dv_cols = [], []
            for j in range(KVW // CH):
                dkj = jnp.zeros((2 * CH, CH), F32)
                dvj = jnp.zeros((2 * CH, CH), F32)
                for grp in (2 * j, 2 * j + 1):
                    kk = _pair_layout(kp, kc, grp)
                    vv = _pair_layout(vp, vc, grp)
                    dkk = jnp.zeros((4 * CH, CH), F32)
                    dvv = jnp.zeros((4 * CH, CH), F32)
                    for pp in range(2):
                        p = 2 * grp + pp
                        qp = q_ref[:, CH * p:CH * (p + 1)]
                        dob = do_ref[:, CH * p:CH * (p + 1)].astype(BF16)
                        s = jnp.where(mask, _dot_nt(qp, kk), -1e30)
                        dp = _dot_nt(dob, vv)
                        ds_parts, p_parts = [], []
                        for par in range(2):
                            sl = slice(2 * CH * par, 2 * CH * (par + 1))
                            pr, psink = _softmax_sink(s[:, sl], sk_ref[2 * p + par])
                            delta = jnp.sum(pr * dp[:, sl], axis=-1, keepdims=True)
                            ds_parts.append(pr * (dp[:, sl] - delta))
                            p_parts.append(pr)
                            tot = -jnp.sum(psink * delta, axis=0, keepdims=True)
                            dsk = dsk + jnp.where(lane == 2 * p + par, tot, 0.0)
                        ds = jnp.concatenate(ds_parts, axis=1).astype(BF16)
                        pb = jnp.concatenate(p_parts, axis=1).astype(BF16)
                        dq_ref[:, CH * p:CH * (p + 1)] = (_rope_t(_dot(ds, kk), cq, s1q, s2q) * Q_SCALE).astype(BF16)
                        dkk = dkk + _dot_tn(ds, qp)
                        dvv = dvv + _dot_tn(pb, dob)
                    dkj = dkj + _unpair(dkk, grp)
                    dvj = dvj + _unpair(dvv, grp)
                dk_cols.append(dkj)
                dv_cols.append(dvj)
            dkf = jnp.concatenate(dk_cols, axis=1)
            dvf = jnp.concatenate(dv_cols, axis=1)
            dsk_ref[...] += dsk
            flush(dkf[:CH], dvf[:CH])
            kcar[...] = dkf[CH:]
            vcar[...] = dvf[CH:]

        @pl.when(n == nb)
        def _():
            z = jnp.zeros((CH, KVW), F32)
            flush(z, z)

    last = nb - 1
    cur = lambda n: (jnp.minimum(n, last), 0)
    prev = lambda n: (jnp.clip(n - 1, 0, last), 0)
    sd = jax.ShapeDtypeStruct
    return pl.pallas_call(
        body, name="attn_bwd", grid=(nb + 1,),
        in_specs=[pl.BlockSpec(memory_space=pltpu.SMEM), pl.BlockSpec((CH, D), cur),
                  pl.BlockSpec((CH, KVW), prev), pl.BlockSpec((CH, KVW), cur),
                  pl.BlockSpec((CH, KVW), prev), pl.BlockSpec((CH, KVW), cur),
                  pl.BlockSpec((CH, D), cur),
                  pl.BlockSpec((CH, CH), cur), pl.BlockSpec((CH, CH), cur), pl.BlockSpec((CH, CH), cur),
                  pl.BlockSpec((CH, CH), prev), pl.BlockSpec((CH, CH), prev), pl.BlockSpec((CH, CH), prev)],
        out_specs=[pl.BlockSpec((CH, D), cur), pl.BlockSpec((CH, KVW), prev), pl.BlockSpec((CH, KVW), prev),
                   _const((1, CH))],
        out_shape=[sd((T, D), BF16), sd((T, KVW), BF16), sd((T, KVW), BF16), sd((1, CH), F32)],
        scratch_shapes=[pltpu.VMEM((CH, KVW), F32), pltpu.VMEM((CH, KVW), F32)],
        compiler_params=_cparams(("arbitrary",)),
    )(sinks, q, k, k, va, va, datt, rc, rs1, rs2, rc, rs1, rs2)


def _merge_fwd(a, att, ga, gb, x, w_a, w_b, w_o, g2, tm):
    T = x.shape[0]

    def body(a_ref, att_ref, ga_ref, gb_ref, x_ref, wa_ref, wb_ref, wo_ref, g_ref,
             pa_ref, pb_ref, mg_ref, mix_ref, x1_ref):
        pa = _dot(a_ref[...], wa_ref[...])
        pb = _dot(att_ref[...], wb_ref[...])
        pa_ref[...] = pa
        pb_ref[...] = pb
        merged = (_sigmoid(ga_ref[...]) * pa + _sigmoid(gb_ref[...]) * pb).astype(BF16)
        mg_ref[...] = merged
        mix = _dot(merged, wo_ref[...])
        mix_ref[...] = mix
        mhat, _ = _rms_hat(mix)
        x1_ref[...] = x_ref[...] + mhat * g_ref[...]

    sd = jax.ShapeDtypeStruct
    return pl.pallas_call(
        body, name="merge_fwd", grid=(T // tm,),
        in_specs=[_rows(tm, D)] * 5 + [_resident((D, D))] * 3 + [_const((1, D))],
        out_specs=[_rows(tm, D)] * 5,
        out_shape=[sd((T, D), F32), sd((T, D), F32), sd((T, D), BF16), sd((T, D), F32), sd((T, D), F32)],
        compiler_params=_cparams(("parallel",)),
    )(a, att, ga, gb, x, w_a, w_b, w_o, g2)


def _merge_bwd(dx1, mix, ga, gb, pa, pb, w_a, w_b, w_o, g2, tm):
    T = dx1.shape[0]

    def body(dx1_ref, mix_ref, ga_ref, gb_ref, pa_ref, pb_ref, wa_ref, wb_ref, wo_ref, g_ref,
             dmix_ref, dao_ref, dbo_ref, dga_ref, dgb_ref, da_ref, datt_ref, dg_ref):
        @pl.when(pl.program_id(0) == 0)
        def _():
            dg_ref[...] = jnp.zeros_like(dg_ref)

        mhat, r = _rms_hat(mix_ref[...])
        dmix, dg = _rms_bwd(mhat, r, g_ref[...], dx1_ref[...])
        dg_ref[...] += dg
        dmix = dmix.astype(BF16)
        dmix_ref[...] = dmix
        dmerged = _dot_nt(dmix, wo_ref[...])
        sa = _sigmoid(ga_ref[...])
        sb = _sigmoid(gb_ref[...])
        dao = (dmerged * sa).astype(BF16)
        dbo = (dmerged * sb).astype(BF16)
        dao_ref[...] = dao
        dbo_ref[...] = dbo
        dga_ref[...] = (dmerged * pa_ref[...] * (sa * (1.0 - sa))).astype(BF16)
        dgb_ref[...] = (dmerged * pb_ref[...] * (sb * (1.0 - sb))).astype(BF16)
        da_ref[...] = _dot_nt(dao, wa_ref[...])
        datt_ref[...] = _dot_nt(dbo, wb_ref[...])

    sd = jax.ShapeDtypeStruct
    return pl.pallas_call(
        body, name="merge_bwd", grid=(T // tm,),
        in_specs=[_rows(tm, D)] * 6 + [_resident((D, D))] * 3 + [_const((1, D))],
        out_specs=[_rows(tm, D)] * 7 + [_const((1, D))],
        out_shape=[sd((T, D), BF16)] * 5 + [sd((T, D), F32)] * 2 + [sd((1, D), F32)],
        compiler_params=_cparams(("arbitrary",)),
    )(dx1, mix, ga, gb, pa, pb, w_a, w_b, w_o, g2)


def _ffn(x1, target, w1, w2, g3, g4, tm):
    T = x1.shape[0]

    def body(x_ref, t_ref, w1_ref, w2_ref, g3_ref, g4_ref,
             hf_ref, f2_ref, dff_ref, df1_ref, dx_ref, ls_ref, dg3_ref, dg4_ref):
        @pl.when(pl.program_id(0) == 0)
        def _():
            ls_ref[...] = jnp.zeros_like(ls_ref)
            dg3_ref[...] = jnp.zeros_like(dg3_ref)
            dg4_ref[...] = jnp.zeros_like(dg4_ref)

        x = x_ref[...]
        g3, g4 = g3_ref[...], g4_ref[...]
        xhat, r3 = _rms_hat(x)
        hf = (xhat * g3).astype(BF16)
        hf_ref[...] = hf
        rl = jnp.maximum(_dot(hf, w1_ref[...]), 0.0)
        f2 = (rl * rl).astype(BF16)
        f2_ref[...] = f2
        fhat, r4 = _rms_hat(_dot(f2, w2_ref[...]))
        err = x + fhat * g4 - t_ref[...]
        ls_ref[...] += jnp.sum(err * err, axis=0, keepdims=True)
        dy = err * (1.0 / D)
        dff, dg4 = _rms_bwd(fhat, r4, g4, dy)
        dg4_ref[...] += dg4
        dff = dff.astype(BF16)
        dff_ref[...] = dff
        df1 = (_dot_nt(dff, w2_ref[...]) * (2.0 * rl)).astype(BF16)
        df1_ref[...] = df1
        dxn, dg3 = _rms_bwd(xhat, r3, g3, _dot_nt(df1, w1_ref[...]))
        dg3_ref[...] += dg3
        dx_ref[...] = dy + dxn

    sd = jax.ShapeDtypeStruct
    return pl.pallas_call(
        body, name="ffn_fwd_bwd", grid=(T // tm,),
        in_specs=[_rows(tm, D), _rows(tm, D), _resident((D, DFF)), _resident((DFF, D)), _const((1, D)), _const((1, D))],
        out_specs=[_rows(tm, D), _rows(tm, DFF), _rows(tm, D), _rows(tm, DFF), _rows(tm, D), _const((1, D)),
                   _const((1, D)), _const((1, D))],
        out_shape=[sd((T, D), BF16), sd((T, DFF), BF16), sd((T, D), BF16), sd((T, DFF), BF16), sd((T, D), F32),
                   sd((1, D), F32), sd((1, D), F32), sd((1, D), F32)],
        compiler_params=_cparams(("arbitrary",)),
    )(x1, target, w1, w2, g3, g4)


def _inproj_bwd(parts, x, dx1, g1, w_in, tm):
    T = x.shape[0]
    widths = [p.shape[1] for p in parts]
    offs = [sum(widths[:i]) for i in range(len(widths) + 1)]
    assert offs[-1] == IN_W

    def body(*refs):
        n = len(parts)
        prefs = refs[:n]
        x_ref, dx1_ref, g_ref, w_ref, dx_ref, dp_ref, dg_ref = refs[n:]

        @pl.when(pl.program_id(0) == 0)
        def _():
            dg_ref[...] = jnp.zeros_like(dg_ref)

        dh = None
        for i in range(n):
            blk = prefs[i][...]
            dp_ref[:, offs[i]:offs[i + 1]] = blk
            t = _dot_nt(blk, w_ref[:, offs[i]:offs[i + 1]])
            dh = t if dh is None else dh + t
        xhat, r = _rms_hat(x_ref[...])
        dxn, dg = _rms_bwd(xhat, r, g_ref[...], dh)
        dg_ref[...] += dg
        dx_ref[...] = dx1_ref[...] + dxn

    sd = jax.ShapeDtypeStruct
    return pl.pallas_call(
        body, name="inproj_bwd", grid=(T // tm,),
        in_specs=[_rows(tm, w) for w in widths] + [_rows(tm, D), _rows(tm, D), _const((1, D)), _resident((D, IN_W))],
        out_specs=[_rows(tm, D), _rows(tm, IN_W), _const((1, D))],
        out_shape=[sd((T, D), F32), sd((T, IN_W), BF16), sd((1, D), F32)],
        compiler_params=_cparams(("arbitrary",)),
    )(*parts, x, dx1, g1, w_in)


def _wgrad(a, g, tn, tm, name):
    T, K = a.shape
    N = g.shape[1]

    def body(a_ref, g_ref, o_ref):
        @pl.when(pl.program_id(1) == 0)
        def _():
            o_ref[...] = jnp.zeros_like(o_ref)

        o_ref[...] += _dot_tn(a_ref[...], g_ref[...])

    return pl.pallas_call(
        body, name=name, grid=(N // tn, T // tm),
        in_specs=[pl.BlockSpec((tm, K), lambda j, t: (t, 0)), pl.BlockSpec((tm, tn), lambda j, t: (t, j))],
        out_specs=pl.BlockSpec((K, tn), lambda j, t: (0, j)),
        out_shape=jax.ShapeDtypeStruct((K, N), F32),
        compiler_params=_cparams(("parallel", "arbitrary")),
    )(a, g)


def _adamw(w, g, m, v, tr, name):
    R, C = w.shape
    bc1 = 1.0 / (1.0 - B1 ** STEP)
    bc2 = 1.0 / (1.0 - B2 ** STEP)

    def body(w_ref, g_ref, m_ref, v_ref, d_ref, nm_ref, nv_ref):
        g = g_ref[...]
        m = B1 * m_ref[...] + (1.0 - B1) * g
        v = B2 * v_ref[...] + (1.0 - B2) * (g * g)
        nm_ref[...] = m
        nv_ref[...] = v
        d_ref[...] = -LR * ((m * bc1) / (jnp.sqrt(v * bc2) + AEPS) + WD * w_ref[...])

    spec = pl.BlockSpec((tr, C), lambda i: (i, 0))
    return pl.pallas_call(
        body, name=name, grid=(R // tr,), in_specs=[spec] * 4, out_specs=[spec] * 3,
        out_shape=[jax.ShapeDtypeStruct((R, C), F32)] * 3,
        compiler_params=_cparams(("parallel",)),
    )(w, g, m, v)


BIG = (("col", (D, IN_W)), ("row", (D, D)), ("row", (D, D)), ("row", (D, D)), ("col", (D, DFF)), ("row", (DFF, D)))
NBIG = len(BIG)
ANY = pl.BlockSpec(memory_space=pl.ANY)


def _shard_shape(kind, shape):
    R, C = shape
    return (R, C // 4) if kind == "col" else (R // 4, C)


def _half_shape(kind, shape):
    R, C = shape
    return (R // 2, C) if kind == "col" else (R, C // 2)


def _piece_shape(kind, shape):
    R, C = shape
    return (R // 2, C // 4) if kind == "col" else (R // 4, C // 2)


def _own_region(ref, kind, shape, s):
    R, C = shape
    return ref.at[:, pl.ds(s * (C // 4), C // 4)] if kind == "col" else ref.at[pl.ds(s * (R // 4), R // 4), :]


def _ag_region(ref, kind, shape, s, hc):
    R, C = shape
    if kind == "col":
        return ref.at[pl.ds(hc * (R // 2), R // 2), pl.ds(s * (C // 4), C // 4)]
    return ref.at[pl.ds(s * (R // 4) + hc * (R // 8), R // 8), :]


def _ag_shard_half(ref, kind, shape, hc):
    R, C = shape
    return ref.at[pl.ds(hc * (R // 2), R // 2), :] if kind == "col" else ref.at[pl.ds(hc * (R // 8), R // 8), :]


def _grad_half(ref, kind, shape, hc):
    R, C = shape
    return ref.at[pl.ds(hc * (R // 2), R // 2), :] if kind == "col" else ref.at[:, pl.ds(hc * (C // 2), C // 2)]


def _half_piece(ref, kind, shape, s):
    R, C = shape
    return ref.at[:, pl.ds(s * (C // 4), C // 4)] if kind == "col" else ref.at[pl.ds(s * (R // 4), R // 4), :]


def _place():
    x, y, c = lax.axis_index("x"), lax.axis_index("y"), lax.axis_index("c")
    chips = [(1 - x, y), (x, 1 - y), (1 - x, 1 - y)]
    return x, y, c, chips


def _rcopy(src, dst, ssem, rsem, dev):
    return pltpu.make_async_remote_copy(src_ref=src, dst_ref=dst, send_sem=ssem, recv_sem=rsem,
                                        device_id=dev, device_id_type=MESH)


def _all_gather(shards):
    def body(*refs):
        sh, full = refs[:NBIG], refs[NBIG:2 * NBIG]
        ssem, rsem, lsem = refs[2 * NBIG:]
        x, y, c, chips = _place()
        me_s, sib = 2 * x + y, (x, y, 1 - c)
        local, sends = [], []
        for w, (kind, shape) in enumerate(BIG):
            cp = pltpu.make_async_copy(sh[w], _own_region(full[w], kind, shape, me_s), lsem.at[w])
            cp.start()
            local.append(cp)
        for w, (kind, shape) in enumerate(BIG):
            for j, (cx, cy) in enumerate(chips):
                cp = _rcopy(_ag_shard_half(sh[w], kind, shape, c), _ag_region(full[w], kind, shape, me_s, c),
                            ssem.at[3 * w + j], rsem.at[3 * w + j], (cx, cy, c))
                cp.start()
                sends.append(cp)
        for w, (kind, shape) in enumerate(BIG):
            for j, (cx, cy) in enumerate(chips):
                reg = _ag_region(full[w], kind, shape, 2 * cx + cy, c)
                _rcopy(reg, reg, ssem.at[3 * w + j], rsem.at[3 * w + j], sib).wait_recv()
                k = 3 * NBIG + 3 * w + j
                cp = _rcopy(reg, reg, ssem.at[k], rsem.at[k], sib)
                cp.start()
                sends.append(cp)
        for w, (kind, shape) in enumerate(BIG):
            for j, (cx, cy) in enumerate(chips):
                reg = _ag_region(full[w], kind, shape, 2 * cx + cy, 1 - c)
                k = 3 * NBIG + 3 * w + j
                _rcopy(reg, reg, ssem.at[k], rsem.at[k], sib).wait_recv()
        for cp in sends:
            cp.wait_send()
        for cp in local:
            cp.wait()

    return pl.pallas_call(
        body, name="weights_all_gather", in_specs=[ANY] * NBIG, out_specs=[ANY] * NBIG,
        out_shape=[jax.ShapeDtypeStruct(shape, BF16) for _, shape in BIG],
        scratch_shapes=[pltpu.SemaphoreType.DMA((6 * NBIG,)), pltpu.SemaphoreType.DMA((6 * NBIG,)),
                        pltpu.SemaphoreType.DMA((NBIG,))],
    )(*shards)


def _rs_sibling(grads):
    def body(*refs):
        g, mine, got = refs[:NBIG], refs[NBIG:2 * NBIG], refs[2 * NBIG:3 * NBIG]
        ssem, rsem, lsem = refs[3 * NBIG:]
        x, y, c, _ = _place()
        sib = (x, y, 1 - c)
        cps = []
        for w, (kind, shape) in enumerate(BIG):
            lc = pltpu.make_async_copy(_grad_half(g[w], kind, shape, c), mine[w], lsem.at[w])
            lc.start()
            rc = _rcopy(_grad_half(g[w], kind, shape, 1 - c), got[w], ssem.at[w], rsem.at[w], sib)
            rc.start()
            cps.append((lc, rc))
        for lc, rc in cps:
            rc.wait_recv()
            rc.wait_send()
            lc.wait()

    halves = [jax.ShapeDtypeStruct(_half_shape(k, s), F32) for k, s in BIG]
    return pl.pallas_call(
        body, name="grads_to_sibling", in_specs=[ANY] * NBIG, out_specs=[ANY] * (2 * NBIG), out_shape=halves + halves,
        scratch_shapes=[pltpu.SemaphoreType.DMA((NBIG,)), pltpu.SemaphoreType.DMA((NBIG,)), pltpu.SemaphoreType.DMA((NBIG,))],
    )(*grads)


def _rs_chips(sums, sums_bf):
    def body(*refs):
        s32, s16 = refs[:NBIG], refs[NBIG:2 * NBIG]
        mine, got = refs[2 * NBIG:3 * NBIG], refs[3 * NBIG:4 * NBIG]
        ssem, rsem, lsem = refs[4 * NBIG:]
        x, y, c, chips = _place()
        me_s = 2 * x + y
        cps = []
        for w, (kind, shape) in enumerate(BIG):
            lc = pltpu.make_async_copy(_half_piece(s32[w], kind, shape, me_s), mine[w], lsem.at[w])
            lc.start()
            cps.append(lc)
        sends = []
        for w, (kind, shape) in enumerate(BIG):
            for j, (cx, cy) in enumerate(chips):
                cp = _rcopy(_half_piece(s16[w], kind, shape, 2 * cx + cy), got[w].at[j],
                            ssem.at[3 * w + j], rsem.at[3 * w + j], (cx, cy, c))
                cp.start()
                sends.append(cp)
        for cp in sends:
            cp.wait_recv()
        for cp in sends:
            cp.wait_send()
        for lc in cps:
            lc.wait()

    sd = jax.ShapeDtypeStruct
    return pl.pallas_call(
        body, name="grads_to_chips", in_specs=[ANY] * (2 * NBIG), out_specs=[ANY] * (2 * NBIG),
        out_shape=[sd(_piece_shape(k, s), F32) for k, s in BIG] + [sd((3,) + _piece_shape(k, s), BF16) for k, s in BIG],
        scratch_shapes=[pltpu.SemaphoreType.DMA((3 * NBIG,)), pltpu.SemaphoreType.DMA((3 * NBIG,)),
                        pltpu.SemaphoreType.DMA((NBIG,))],
    )(*sums, *sums_bf)


def _rs_share(pieces):
    def body(*refs):
        r, g = refs[:NBIG], refs[NBIG:2 * NBIG]
        ssem, rsem, lsem = refs[2 * NBIG:]
        x, y, c, _ = _place()
        sib = (x, y, 1 - c)
        cps = []
        for w, (kind, shape) in enumerate(BIG):
            sshape = _shard_shape(kind, shape)
            dst = (g[w].at[pl.ds(c * (sshape[0] // 2), sshape[0] // 2), :] if kind == "col"
                   else g[w].at[:, pl.ds(c * (sshape[1] // 2), sshape[1] // 2)])
            lc = pltpu.make_async_copy(r[w], dst, lsem.at[w])
            lc.start()
            rc = _rcopy(r[w], dst, ssem.at[w], rsem.at[w], sib)
            rc.start()
            cps.append((lc, rc))
        for lc, rc in cps:
            rc.wait_recv()
            rc.wait_send()
            lc.wait()

    return pl.pallas_call(
        body, name="grads_share", in_specs=[ANY] * NBIG, out_specs=[ANY] * NBIG,
        out_shape=[jax.ShapeDtypeStruct(_shard_shape(k, s), F32) for k, s in BIG],
        scratch_shapes=[pltpu.SemaphoreType.DMA((NBIG,)), pltpu.SemaphoreType.DMA((NBIG,)), pltpu.SemaphoreType.DMA((NBIG,))],
    )(*pieces)


def _add_halves(a, b, name):
    R, C = a.shape
    tr = 256

    def body(a_ref, b_ref, s_ref, sb_ref):
        s = a_ref[...] + b_ref[...]
        s_ref[...] = s
        sb_ref[...] = s.astype(BF16)

    spec = pl.BlockSpec((tr, C), lambda i: (i, 0))
    return pl.pallas_call(
        body, name=name, grid=(R // tr,), in_specs=[spec, spec], out_specs=[spec, spec],
        out_shape=[jax.ShapeDtypeStruct((R, C), F32), jax.ShapeDtypeStruct((R, C), BF16)],
        compiler_params=_cparams(("parallel",)),
    )(a, b)


def _add_pieces(mine, got, name):
    R, C = mine.shape
    tr = 256

    def body(m_ref, g_ref, o_ref):
        acc = m_ref[...]
        for j in range(3):
            acc = acc + g_ref[j].astype(F32)
        o_ref[...] = acc

    spec = pl.BlockSpec((tr, C), lambda i: (i, 0))
    return pl.pallas_call(
        body, name=name, grid=(R // tr,), in_specs=[spec, pl.BlockSpec((3, tr, C), lambda i: (0, i, 0))], out_specs=spec,
        out_shape=jax.ShapeDtypeStruct((R, C), F32), compiler_params=_cparams(("parallel",)),
    )(mine, got)


SMALL_ROWS = 1024 + 8 * 8


def _small_all_reduce(p):
    def body(p_ref, o_ref, slots, ssem, rsem):
        x, y, c = lax.axis_index("x"), lax.axis_index("y"), lax.axis_index("c")
        me = 4 * x + 2 * y + c
        slots[me] = p_ref[...]
        cps = []
        for r in range(1, 8):
            bx, by, bc = (r >> 2) & 1, (r >> 1) & 1, r & 1
            tgt = (1 - x if bx else x, 1 - y if by else y, 1 - c if bc else c)
            cp = _rcopy(p_ref, slots.at[me], ssem.at[r - 1], rsem.at[r - 1], tgt)
            cp.start()
            cps.append((cp, 4 * tgt[0] + 2 * tgt[1] + tgt[2]))
        for r, (cp, src) in enumerate(cps):
            _rcopy(p_ref, slots.at[src], ssem.at[r], rsem.at[r], (x, y, c)).wait_recv()
        acc = slots[0]
        for d in range(1, 8):
            acc = acc + slots[d]
        o_ref[...] = acc
        for cp, _ in cps:
            cp.wait_send()

    vm = pl.BlockSpec(memory_space=pltpu.VMEM)
    return pl.pallas_call(
        body, name="small_all_reduce", in_specs=[vm], out_specs=vm,
        out_shape=jax.ShapeDtypeStruct((SMALL_ROWS, CH), F32),
        scratch_shapes=[pltpu.VMEM((8, SMALL_ROWS, CH), F32), pltpu.SemaphoreType.DMA((7,)), pltpu.SemaphoreType.DMA((7,))],
    )(p)


def _rope_tables(positions):
    inv_freq = 500000.0 ** (-jnp.arange(0, 2 * ROPE_HALF, 2, dtype=F32) / (2 * ROPE_HALF))
    ang = positions.astype(F32)[:, None] * inv_freq
    cos, sin = jnp.cos(ang), jnp.sin(ang)
    T = positions.shape[0]
    one = jnp.ones((T, HD - 2 * ROPE_HALF), F32)
    zero = jnp.zeros((T, HD - 2 * ROPE_HALF), F32)
    z8 = jnp.zeros((T, ROPE_HALF), F32)
    c = jnp.concatenate([cos, cos, one], axis=1)
    s1 = jnp.concatenate([-sin, z8, zero], axis=1)
    s2 = jnp.concatenate([z8, sin, zero], axis=1)
    return tuple(jnp.concatenate([t, t], axis=1) for t in (c, s1, s2))


def _local_step(x, positions, w_in, w_a, w_b, w_o, w_ff_in, w_ff_out, ln_g, ln_b, w_sp, b_sp, sinks,
                g1, g2, g3, g4, target):
    rc, rs1, rs2 = _rope_tables(positions)
    bfull = jnp.repeat(b_sp.T, CH, axis=1)
    h, u, vs, q, k, va, ga, gb = _inproj(x, g1, w_in, rc, rs1, rs2, tm=256)
    a = _sgu_fwd(u, vs, ln_g, ln_b, w_sp, bfull, tm=512)
    att = _attn_fwd(q, k, va, sinks)
    pa, pb, merged, mix, x1 = _merge_fwd(a, att, ga, gb, x, w_a, w_b, w_o, g2, tm=256)
    hf, f2, dff, df1, dx1, lsum, dg3, dg4 = _ffn(x1, target, w_ff_in, w_ff_out, g3, g4, tm=256)
    wtm = min(1024, x.shape[0])
    dw_ff_out = _wgrad(f2, dff, tn=512, tm=512, name="wgrad_ff_out")
    dw_ff_in = _wgrad(hf, df1, tn=1024, tm=wtm, name="wgrad_ff_in")
    dmix, dao, dbo, dga, dgb, da, datt, dg2 = _merge_bwd(dx1, mix, ga, gb, pa, pb, w_a, w_b, w_o, g2, tm=256)
    dw_o = _wgrad(merged, dmix, tn=1024, tm=wtm, name="wgrad_o")
    dw_a = _wgrad(a, dao, tn=1024, tm=wtm, name="wgrad_a")
    dw_b = _wgrad(att, dbo, tn=1024, tm=wtm, name="wgrad_b")
    du, dvs, dws, dbs, dlg, dlb = _sgu_bwd(u, vs, da, ln_g, ln_b, w_sp, bfull, tm=512)
    dq, dk, dva, dsk = _attn_bwd(q, k, va, datt, sinks, rc, rs1, rs2)
    dx, dproj, dg1 = _inproj_bwd([du, dvs, dq, dk, dva, dga, dgb], x, dx1, g1, w_in, tm=256)
    dw_in = _wgrad(h, dproj, tn=IN_W // 4, tm=wtm, name="wgrad_in")
    big = (dw_in, dw_a, dw_b, dw_o, dw_ff_in, dw_ff_out)
    small = dict(ln_v_gain=dlg, ln_v_bias=dlb, w_spatial=dws, b_spatial=dbs, sinks=dsk[:, :NQ],
                 norm_mix_pre=dg1, norm_mix_post=dg2, norm_ff_pre=dg3, norm_ff_post=dg4)
    return lsum, dx, big, small


BIG_NAMES = ("w_in", "w_a", "w_b", "w_o", "w_ff_in", "w_ff_out")
SMALL_NAMES = ("w_spatial", "ln_v_gain", "ln_v_bias", "b_spatial", "sinks", "norm_mix_pre", "norm_mix_post",
               "norm_ff_pre", "norm_ff_post")
WEIGHT_ORDER = ("w_in", "ln_v_gain", "ln_v_bias", "w_spatial", "b_spatial", "sinks", "w_a", "w_b", "w_o",
                "norm_mix_pre", "norm_mix_post", "w_ff_in", "w_ff_out", "norm_ff_pre", "norm_ff_post")


def _pack_small(d):
    parts = []
    for n in SMALL_NAMES:
        flat = d[n].reshape(-1)
        pad = (-flat.shape[0]) % (8 * CH)
        parts.append(jnp.pad(flat, (0, pad)).reshape(-1, CH))
    return jnp.concatenate(parts, axis=0)


def _unpack_small(p, like):
    out, row = {}, 0
    for n in SMALL_NAMES:
        size = like[n].size
        rows = -(-size // (8 * CH)) * 8
        out[n] = p[row:row + rows].reshape(-1)[:size].reshape(like[n].shape)
        row += rows
    return out


def kernel(x, positions, w_in, ln_v_gain, ln_v_bias, w_spatial, b_spatial, sinks, w_a, w_b, w_o, norm_mix_pre, norm_mix_post, w_ff_in, w_ff_out, norm_ff_pre, norm_ff_post, loss_target, m_w_in, m_ln_v_gain, m_ln_v_bias, m_w_spatial, m_b_spatial, m_sinks, m_w_a, m_w_b, m_w_o, m_norm_mix_pre, m_norm_mix_post, m_w_ff_in, m_w_ff_out, m_norm_ff_pre, m_norm_ff_post, v_w_in, v_ln_v_gain, v_ln_v_bias, v_w_spatial, v_b_spatial, v_sinks, v_w_a, v_w_b, v_w_o, v_norm_mix_pre, v_norm_mix_post, v_w_ff_in, v_w_ff_out, v_norm_ff_pre, v_norm_ff_post):
    w = dict(w_in=w_in, ln_v_gain=ln_v_gain, ln_v_bias=ln_v_bias, w_spatial=w_spatial, b_spatial=b_spatial, sinks=sinks,
             w_a=w_a, w_b=w_b, w_o=w_o, norm_mix_pre=norm_mix_pre, norm_mix_post=norm_mix_post, w_ff_in=w_ff_in,
             w_ff_out=w_ff_out, norm_ff_pre=norm_ff_pre, norm_ff_post=norm_ff_post)
    m = dict(w_in=m_w_in, ln_v_gain=m_ln_v_gain, ln_v_bias=m_ln_v_bias, w_spatial=m_w_spatial, b_spatial=m_b_spatial,
             sinks=m_sinks, w_a=m_w_a, w_b=m_w_b, w_o=m_w_o, norm_mix_pre=m_norm_mix_pre, norm_mix_post=m_norm_mix_post,
             w_ff_in=m_w_ff_in, w_ff_out=m_w_ff_out, norm_ff_pre=m_norm_ff_pre, norm_ff_post=m_norm_ff_post)
    v = dict(w_in=v_w_in, ln_v_gain=v_ln_v_gain, ln_v_bias=v_ln_v_bias, w_spatial=v_w_spatial, b_spatial=v_b_spatial,
             sinks=v_sinks, w_a=v_w_a, w_b=v_w_b, w_o=v_w_o, norm_mix_pre=v_norm_mix_pre, norm_mix_post=v_norm_mix_post,
             w_ff_in=v_w_ff_in, w_ff_out=v_w_ff_out, norm_ff_pre=v_norm_ff_pre, norm_ff_post=v_norm_ff_post)

    whole = _all_gather([w[n][0].astype(BF16) for n in BIG_NAMES])
    lsum, dx, big, small = _local_step(
        x[0], positions[0], *whole, ln_v_gain, ln_v_bias, w_spatial[0], b_spatial[0], sinks[0],
        norm_mix_pre, norm_mix_post, norm_ff_pre, norm_ff_post, loss_target[0])
    loss = lax.psum(0.5 * jnp.sum(lsum) / D, ("x", "y", "c"))

    halves = _rs_sibling(big)
    mine, got = halves[:NBIG], halves[NBIG:]
    sums = [_add_halves(mine[i], got[i], name="grad_add_sibling_" + BIG_NAMES[i]) for i in range(NBIG)]
    pieces = _rs_chips([s[0] for s in sums], [s[1] for s in sums])
    reduced = [_add_pieces(pieces[i], pieces[NBIG + i], name="grad_add_chips_" + BIG_NAMES[i]) for i in range(NBIG)]
    shard_grads = _rs_share(reduced)

    grad, delta, new_m, new_v = {}, {}, {}, {}
    for i, n in enumerate(BIG_NAMES):
        g = shard_grads[i]
        d_, m_, v_ = _adamw(w[n][0], g, m[n][0], v[n][0], tr=256, name="adamw_" + n)
        grad[n], delta[n], new_m[n], new_v[n] = g[None], d_[None], m_[None], v_[None]

    gs = _small_all_reduce(_pack_small(small))
    ds, ms, vs = _adamw(_pack_small(w), gs, _pack_small(m), _pack_small(v), tr=SMALL_ROWS // 4, name="adamw_small")
    for packed, dst in ((gs, grad), (ds, delta), (ms, new_m), (vs, new_v)):
        dst.update(_unpack_small(packed, w))

    outs = [loss, dx[None]]
    for group in (grad, delta, new_m, new_v):
        outs.extend(group[n] for n in WEIGHT_ORDER)
    return tuple(outs)
```

```python
import functools

import jax
import jax.numpy as jnp
from jax import lax
from jax.experimental import pallas as pl
from jax.experimental.pallas import tpu as pltpu

F32 = jnp.float32
BF16 = jnp.bfloat16

D = 1024
CH = 128
NG = 8
HD = 64
NQ = 16
NKV = 4
KVW = NKV * HD
DFF = 4 * D
EPS = 1e-6
IN_W = 5632
SEG = (0, 1024, 2048, 3072, 3328, 3584, 4608, 5632)
ROPE_HALF = 8
Q_SCALE = HD ** -0.5

LR, B1, B2, AEPS, WD, STEP = 0.001, 0.9, 0.999, 1e-08, 0.01, 10

VMEM_LIMIT = 56 * 1024 * 1024
MESH = pl.DeviceIdType.MESH

_GELU_C0 = 0.7978845608028654
_GELU_C1 = 0.044715


def _cparams(sem=None):
    kw = dict(vmem_limit_bytes=VMEM_LIMIT)
    if sem is not None:
        kw["dimension_semantics"] = sem
    return pltpu.CompilerParams(**kw)


def _resident(shape):
    nd = len(shape)
    return pl.BlockSpec(shape, lambda *_: (0,) * nd, pipeline_mode=pl.Buffered(1))


def _const(shape):
    nd = len(shape)
    return pl.BlockSpec(shape, lambda *_: (0,) * nd)


def _rows(tm, w):
    return pl.BlockSpec((tm, w), lambda i: (i, 0))


def _gelu(x):
    t = jnp.tanh(_GELU_C0 * (x + _GELU_C1 * (x * x * x)))
    return 0.5 * x * (1.0 + t), t


def _gelu_grad(x, t):
    return 0.5 * (1.0 + t) + 0.5 * x * (1.0 - t * t) * (_GELU_C0 * (1.0 + 3.0 * _GELU_C1 * x * x))


def _sigmoid(x):
    return 1.0 / (1.0 + jnp.exp(-x))


def _rms_hat(x):
    r = lax.rsqrt(jnp.mean(x * x, axis=-1, keepdims=True) + EPS)
    return x * r, r


def _rms_bwd(xhat, r, g, dout):
    dg = jnp.sum(dout * xhat, axis=0, keepdims=True)
    dy = dout * g
    dx = r * (dy - xhat * jnp.mean(dy * xhat, axis=-1, keepdims=True))
    return dx, dg


def _dot(a, b):
    return jnp.dot(a, b, preferred_element_type=F32)


def _dot_nt(a, b):
    return lax.dot_general(a, b, (((1,), (1,)), ((), ())), preferred_element_type=F32)


def _dot_tn(a, b):
    return lax.dot_general(a, b, (((0,), (0,)), ((), ())), preferred_element_type=F32)


def _rope(blk, c, s1, s2):
    return blk * c + pltpu.roll(blk, CH - ROPE_HALF, 1) * s1 + pltpu.roll(blk, ROPE_HALF, 1) * s2


def _rope_t(blk, c, s1, s2):
    return blk * c + pltpu.roll(blk * s1, ROPE_HALF, 1) + pltpu.roll(blk * s2, CH - ROPE_HALF, 1)


def _inproj(x, g1, w_in, rc, rs1, rs2, tm):
    T = x.shape[0]

    def body(x_ref, g_ref, w_ref, c_ref, s1_ref, s2_ref,
             h_ref, u_ref, v_ref, q_ref, k_ref, va_ref, ga_ref, gb_ref):
        xhat, _ = _rms_hat(x_ref[...])
        h = (xhat * g_ref[...]).astype(BF16)
        h_ref[...] = h
        u_ref[...] = _dot(h, w_ref[:, SEG[0]:SEG[1]])
        v_ref[...] = _dot(h, w_ref[:, SEG[1]:SEG[2]])
        c, s1, s2 = c_ref[...], s1_ref[...], s2_ref[...]
        q = _dot(h, w_ref[:, SEG[2]:SEG[3]])
        for p in range(D // CH):
            blk = _rope(q[:, CH * p:CH * (p + 1)], c, s1, s2) * Q_SCALE
            q_ref[:, CH * p:CH * (p + 1)] = blk.astype(BF16)
        k = _dot(h, w_ref[:, SEG[3]:SEG[4]])
        for p in range(KVW // CH):
            k_ref[:, CH * p:CH * (p + 1)] = _rope(k[:, CH * p:CH * (p + 1)], c, s1, s2).astype(BF16)
        va_ref[...] = _dot(h, w_ref[:, SEG[4]:SEG[5]]).astype(BF16)
        ga_ref[...] = _dot(h, w_ref[:, SEG[5]:SEG[6]])
        gb_ref[...] = _dot(h, w_ref[:, SEG[6]:SEG[7]])

    sd = jax.ShapeDtypeStruct
    return pl.pallas_call(
        body, name="inproj_fwd", grid=(T // tm,),
        in_specs=[_rows(tm, D), _const((1, D)), _resident((D, IN_W)), _rows(tm, CH), _rows(tm, CH), _rows(tm, CH)],
        out_specs=[_rows(tm, D), _rows(tm, D), _rows(tm, D), _rows(tm, D), _rows(tm, KVW), _rows(tm, KVW),
                   _rows(tm, D), _rows(tm, D)],
        out_shape=[sd((T, D), BF16), sd((T, D), F32), sd((T, D), F32), sd((T, D), BF16), sd((T, KVW), BF16),
                   sd((T, KVW), BF16), sd((T, D), F32), sd((T, D), F32)],
        compiler_params=_cparams(("parallel",)),
    )(x, g1, w_in, rc, rs1, rs2)


def _sgu_common(u, vs, lng, lnb, ws_ref, bfull):
    nc = u.shape[0] // CH
    ug, tu = _gelu(u)
    vg, tv = _gelu(vs)
    mu = jnp.mean(vg, axis=-1, keepdims=True)
    xc = vg - mu
    rstd = lax.rsqrt(jnp.mean(xc * xc, axis=-1, keepdims=True) + EPS)
    vhat = xc * rstd
    vnb = (vhat * lng + lnb).astype(BF16)
    tri = lax.broadcasted_iota(jnp.int32, (CH, CH), 0) >= lax.broadcasted_iota(jnp.int32, (CH, CH), 1)
    wts, rhss, mixed = [], [], []
    for g in range(NG):
        wt = jnp.where(tri, ws_ref[g], 0.0).astype(BF16)
        rhs = jnp.concatenate([vnb[CH * c:CH * (c + 1), CH * g:CH * (g + 1)] for c in range(nc)], axis=1)
        mix = _dot(wt, rhs)
        wts.append(wt)
        rhss.append(rhs)
        mixed.append([mix[:, CH * c:CH * (c + 1)] + bfull[:, CH * g:CH * (g + 1)] for c in range(nc)])
    return nc, ug, tu, tv, rstd, vhat, tri, wts, rhss, mixed


def _sgu_fwd(u, vs, lng, lnb, ws, bfull, tm):
    T = u.shape[0]

    def body(u_ref, v_ref, lng_ref, lnb_ref, ws_ref, bf_ref, a_ref):
        nc, ug, _, _, _, _, _, _, _, mixed = _sgu_common(
            u_ref[...], v_ref[...], lng_ref[...], lnb_ref[...], ws_ref, bf_ref[...])
        for g in range(NG):
            for c in range(nc):
                a_ref[CH * c:CH * (c + 1), CH * g:CH * (g + 1)] = (
                    ug[CH * c:CH * (c + 1), CH * g:CH * (g + 1)] * mixed[g][c]).astype(BF16)

    return pl.pallas_call(
        body, name="sgu_fwd", grid=(T // tm,),
        in_specs=[_rows(tm, D), _rows(tm, D), _const((1, D)), _const((1, D)), _const((NG, CH, CH)), _const((CH, D))],
        out_specs=_rows(tm, D), out_shape=jax.ShapeDtypeStruct((T, D), BF16),
        compiler_params=_cparams(("parallel",)),
    )(u, vs, lng, lnb, ws, bfull)


def _sgu_bwd(u, vs, da, lng, lnb, ws, bfull, tm):
    T = u.shape[0]
    nsteps = T // tm

    def body(u_ref, v_ref, da_ref, lng_ref, lnb_ref, ws_ref, bf_ref,
             du_ref, dv_ref, dws_ref, dbs_ref, dlg_ref, dlb_ref, db_ref):
        i = pl.program_id(0)
        u, vs, da, lng = u_ref[...], v_ref[...], da_ref[...], lng_ref[...]
        nc, ug, tu, tv, rstd, vhat, tri, wts, rhss, mixed = _sgu_common(u, vs, lng, lnb_ref[...], ws_ref, bf_ref[...])

        @pl.when(i == 0)
        def _():
            dws_ref[...] = jnp.zeros_like(dws_ref)
            db_ref[...] = jnp.zeros_like(db_ref)
            dlg_ref[...] = jnp.zeros_like(dlg_ref)
            dlb_ref[...] = jnp.zeros_like(dlb_ref)

        gu = _gelu_grad(u, tu)
        dvn_cols = []
        for g in range(NG):
            dmix = []
            for c in range(nc):
                sl = (slice(CH * c, CH * (c + 1)), slice(CH * g, CH * (g + 1)))
                da_b = da[sl]
                du_ref[sl] = (da_b * mixed[g][c] * gu[sl]).astype(BF16)
                dmix.append(da_b * ug[sl])
            db_ref[:, CH * g:CH * (g + 1)] += functools.reduce(lambda a, b: a + b, dmix)
            dm = jnp.concatenate(dmix, axis=1).astype(BF16)
            dws_ref[g] += _dot_nt(dm, rhss[g])
            dvn_cols.append(_dot_tn(wts[g], dm))
        dvn = jnp.concatenate(
            [jnp.concatenate([dvn_cols[g][:, CH * c:CH * (c + 1)] for g in range(NG)], axis=1) for c in range(nc)],
            axis=0)
        dlg_ref[...] += jnp.sum(dvn * vhat, axis=0, keepdims=True)
        dlb_ref[...] += jnp.sum(dvn, axis=0, keepdims=True)
        dvh = dvn * lng
        dvg = rstd * (dvh - jnp.mean(dvh, axis=-1, keepdims=True)
                      - vhat * jnp.mean(dvh * vhat, axis=-1, keepdims=True))
        dv_ref[...] = (dvg * _gelu_grad(vs, tv)).astype(BF16)

        @pl.when(i == nsteps - 1)
        def _():
            for g in range(NG):
                dws_ref[g] = jnp.where(tri, dws_ref[g], 0.0)
                dbs_ref[g:g + 1, :] = jnp.sum(db_ref[:, CH * g:CH * (g + 1)].T, axis=0, keepdims=True)

    sd = jax.ShapeDtypeStruct
    return pl.pallas_call(
        body, name="sgu_bwd", grid=(nsteps,),
        in_specs=[_rows(tm, D), _rows(tm, D), _rows(tm, D), _const((1, D)), _const((1, D)), _const((NG, CH, CH)),
                  _const((CH, D))],
        out_specs=[_rows(tm, D), _rows(tm, D), _const((NG, CH, CH)), _const((NG, CH)), _const((1, D)), _const((1, D))],
        out_shape=[sd((T, D), BF16), sd((T, D), BF16), sd((NG, CH, CH), F32), sd((NG, CH), F32), sd((1, D), F32),
                   sd((1, D), F32)],
        scratch_shapes=[pltpu.VMEM((CH, D), F32)],
        compiler_params=_cparams(("arbitrary",)),
    )(u, vs, da, lng, lnb, ws, bfull)


def _pair_layout(prev, cur, grp):
    j, half = grp // 2, grp % 2
    blk = jnp.concatenate([prev[:, CH * j:CH * (j + 1)], cur[:, CH * j:CH * (j + 1)]], axis=0).astype(F32)
    lo = lax.broadcasted_iota(jnp.int32, blk.shape, 1) < HD
    rolled = pltpu.roll(blk, HD, 1)
    even = jnp.where(lo, blk if half == 0 else rolled, 0.0)
    odd = jnp.where(lo, 0.0, rolled if half == 0 else blk)
    return jnp.concatenate([even, odd], axis=0).astype(BF16)


def _unpair(acc, grp):
    half = grp % 2
    lo = lax.broadcasted_iota(jnp.int32, (2 * CH, CH), 1) < HD
    ev, od = acc[:2 * CH], acc[2 * CH:]
    if half == 0:
        return jnp.where(lo, ev + pltpu.roll(od, HD, 1), 0.0)
    return jnp.where(lo, 0.0, pltpu.roll(ev, HD, 1) + od)


def _attn_mask(n):
    qi = lax.broadcasted_iota(jnp.int32, (CH, 2 * CH), 0)
    kc = lax.broadcasted_iota(jnp.int32, (CH, 2 * CH), 1)
    ok = (kc > qi) & (kc <= qi + CH) & ((kc >= CH) | (n > 0))
    return jnp.concatenate([ok, ok], axis=1)


def _softmax_sink(s, sink):
    m = jnp.maximum(jnp.max(s, axis=-1, keepdims=True), sink)
    p = jnp.exp(s - m)
    ps = jnp.exp(sink - m)
    inv = 1.0 / (jnp.sum(p, axis=-1, keepdims=True) + ps)
    return p * inv, ps * inv


def _attn_fwd(q, k, va, sinks):
    T = q.shape[0]
    nb = T // CH

    def body(sk_ref, q_ref, kp_ref, kc_ref, vp_ref, vc_ref, o_ref):
        n = pl.program_id(0)
        mask = _attn_mask(n)
        kp, kc, vp, vc = kp_ref[...], kc_ref[...], vp_ref[...], vc_ref[...]
        for grp in range(NKV):
            kk = _pair_layout(kp, kc, grp)
            vv = _pair_layout(vp, vc, grp)
            for pp in range(2):
                p = 2 * grp + pp
                s = jnp.where(mask, _dot_nt(q_ref[:, CH * p:CH * (p + 1)], kk), -1e30)
                pe, _ = _softmax_sink(s[:, :2 * CH], sk_ref[2 * p])
                po, _ = _softmax_sink(s[:, 2 * CH:], sk_ref[2 * p + 1])
                pr = jnp.concatenate([pe, po], axis=1).astype(BF16)
                o_ref[:, CH * p:CH * (p + 1)] = _dot(pr, vv).astype(BF16)

    prev = lambda n: (jnp.maximum(n - 1, 0), 0)
    cur = lambda n: (n, 0)
    return pl.pallas_call(
        body, name="attn_fwd", grid=(nb,),
        in_specs=[pl.BlockSpec(memory_space=pltpu.SMEM), pl.BlockSpec((CH, D), cur),
                  pl.BlockSpec((CH, KVW), prev), pl.BlockSpec((CH, KVW), cur),
                  pl.BlockSpec((CH, KVW), prev), pl.BlockSpec((CH, KVW), cur)],
        out_specs=pl.BlockSpec((CH, D), cur), out_shape=jax.ShapeDtypeStruct((T, D), BF16),
        compiler_params=_cparams(("parallel",)),
    )(sinks, q, k, k, va, va)


def _attn_bwd(q, k, va, datt, sinks, rc, rs1, rs2):
    T = q.shape[0]
    nb = T // CH

    def body(sk_ref, q_ref, kp_ref, kc_ref, vp_ref, vc_ref, do_ref, cq_ref, s1q_ref, s2q_ref, ck_ref, s1k_ref, s2k_ref,
             dq_ref, dk_ref, dv_ref, dsk_ref, kcar, vcar):
        n = pl.program_id(0)

        @pl.when(n == 0)
        def _():
            kcar[...] = jnp.zeros_like(kcar)
            vcar[...] = jnp.zeros_like(vcar)
            dsk_ref[...] = jnp.zeros_like(dsk_ref)

        def flush(kprev, vprev):
            ck, s1k, s2k = ck_ref[...], s1k_ref[...], s2k_ref[...]
            for j in range(KVW // CH):
                sl = slice(CH * j, CH * (j + 1))
                dk_ref[:, sl] = _rope_t(kcar[:, sl] + kprev[:, sl], ck, s1k, s2k).astype(BF16)
                dv_ref[:, sl] = (vcar[:, sl] + vprev[:, sl]).astype(BF16)

        @pl.when(n < nb)
        def _():
            mask = _attn_mask(n)
            kp, kc, vp, vc = kp_ref[...], kc_ref[...], vp_ref[...], vc_ref[...]
            cq, s1q, s2q = cq_ref[...], s1q_ref[...], s2q_ref[...]
            lane = lax.broadcasted_iota(jnp.int32, (1, CH), 1)
            dsk = jnp.zeros((1, CH), F32)
            dk_cols, dv_cols = [], []
            for j in range(KVW // CH):
                dkj = jnp.zeros((2 * CH, CH), F32)
                dvj = jnp.zeros((2 * CH, CH), F32)
                for grp in (2 * j, 2 * j + 1):
                    kk = _pair_layout(kp, kc, grp)
                    vv = _pair_layout(vp, vc, grp)
                    dkk = jnp.zeros((4 * CH, CH), F32)
                    dvv = jnp.zeros((4 * CH, CH), F32)
                    for pp in range(2):
                        p = 2 * grp + pp
                        qp = q_ref[:, CH * p:CH * (p + 1)]
                        dob = do_ref[:, CH * p:CH * (p + 1)].astype(BF16)
                        s = jnp.where(mask, _dot_nt(qp, kk), -1e30)
                        dp = _dot_nt(dob, vv)
                        ds_parts, p_parts = [], []
                        for par in range(2):
                            sl = slice(2 * CH * par, 2 * CH * (par + 1))
                            pr, psink = _softmax_sink(s[:, sl], sk_ref[2 * p + par])
                            delta = jnp.sum(pr * dp[:, sl], axis=-1, keepdims=True)
                            ds_parts.append(pr * (dp[:, sl] - delta))
                            p_parts.append(pr)
                            tot = -jnp.sum(psink * delta, axis=0, keepdims=True)
                            dsk = dsk + jnp.where(lane == 2 * p + par, tot, 0.0)
                        ds = jnp.concatenate(ds_parts, axis=1).astype(BF16)
                        pb = jnp.concatenate(p_parts, axis=1).astype(BF16)
                        dq_ref[:, CH * p:CH * (p + 1)] = (_rope_t(_dot(ds, kk), cq, s1q, s2q) * Q_SCALE).astype(BF16)
                        dkk = dkk + _dot_tn(ds, qp)
                        dvv = dvv + _dot_tn(pb, dob)
                    dkj = dkj + _unpair(dkk, grp)
                    dvj = dvj + _unpair(dvv, grp)
                dk_cols.append(dkj)
                dv_cols.append(dvj)
            dkf = jnp.concatenate(dk_cols, axis=1)
            dvf = jnp.concatenate(dv_cols, axis=1)
            dsk_ref[...] += dsk
            flush(dkf[:CH], dvf[:CH])
            kcar[...] = dkf[CH:]
            vcar[...] = dvf[CH:]

        @pl.when(n == nb)
        def _():
            z = jnp.zeros((CH, KVW), F32)
            flush(z, z)

    last = nb - 1
    cur = lambda n: (jnp.minimum(n, last), 0)
    prev = lambda n: (jnp.clip(n - 1, 0, last), 0)
    sd = jax.ShapeDtypeStruct
    return pl.pallas_call(
        body, name="attn_bwd", grid=(nb + 1,),
        in_specs=[pl.BlockSpec(memory_space=pltpu.SMEM), pl.BlockSpec((CH, D), cur),
                  pl.BlockSpec((CH, KVW), prev), pl.BlockSpec((CH, KVW), cur),
                  pl.BlockSpec((CH, KVW), prev), pl.BlockSpec((CH, KVW), cur),
                  pl.BlockSpec((CH, D), cur),
                  pl.BlockSpec((CH, CH), cur), pl.BlockSpec((CH, CH), cur), pl.BlockSpec((CH, CH), cur),
                  pl.BlockSpec((CH, CH), prev), pl.BlockSpec((CH, CH), prev), pl.BlockSpec((CH, CH), prev)],
        out_specs=[pl.BlockSpec((CH, D), cur), pl.BlockSpec((CH, KVW), prev), pl.BlockSpec((CH, KVW), prev),
                   _const((1, CH))],
        out_shape=[sd((T, D), BF16), sd((T, KVW), BF16), sd((T, KVW), BF16), sd((1, CH), F32)],
        scratch_shapes=[pltpu.VMEM((CH, KVW), F32), pltpu.VMEM((CH, KVW), F32)],
        compiler_params=_cparams(("arbitrary",)),
    )(sinks, q, k, k, va, va, datt, rc, rs1, rs2, rc, rs1, rs2)


def _merge_fwd(a, att, ga, gb, x, w_a, w_b, w_o, g2, tm):
    T = x.shape[0]

    def body(a_ref, att_ref, ga_ref, gb_ref, x_ref, wa_ref, wb_ref, wo_ref, g_ref,
             pa_ref, pb_ref, mg_ref, mix_ref, x1_ref):
        pa = _dot(a_ref[...], wa_ref[...])
        pb = _dot(att_ref[...], wb_ref[...])
        pa_ref[...] = pa
        pb_ref[...] = pb
        merged = (_sigmoid(ga_ref[...]) * pa + _sigmoid(gb_ref[...]) * pb).astype(BF16)
        mg_ref[...] = merged
        mix = _dot(merged, wo_ref[...])
        mix_ref[...] = mix
        mhat, _ = _rms_hat(mix)
        x1_ref[...] = x_ref[...] + mhat * g_ref[...]

    sd = jax.ShapeDtypeStruct
    return pl.pallas_call(
        body, name="merge_fwd", grid=(T // tm,),
        in_specs=[_rows(tm, D)] * 5 + [_resident((D, D))] * 3 + [_const((1, D))],
        out_specs=[_rows(tm, D)] * 5,
        out_shape=[sd((T, D), F32), sd((T, D), F32), sd((T, D), BF16), sd((T, D), F32), sd((T, D), F32)],
        compiler_params=_cparams(("parallel",)),
    )(a, att, ga, gb, x, w_a, w_b, w_o, g2)


def _merge_bwd(dx1, mix, ga, gb, pa, pb, w_a, w_b, w_o, g2, tm):
    T = dx1.shape[0]

    def body(dx1_ref, mix_ref, ga_ref, gb_ref, pa_ref, pb_ref, wa_ref, wb_ref, wo_ref, g_ref,
             dmix_ref, dao_ref, dbo_ref, dga_ref, dgb_ref, da_ref, datt_ref, dg_ref):
        @pl.when(pl.program_id(0) == 0)
        def _():
            dg_ref[...] = jnp.zeros_like(dg_ref)

        mhat, r = _rms_hat(mix_ref[...])
        dmix, dg = _rms_bwd(mhat, r, g_ref[...], dx1_ref[...])
        dg_ref[...] += dg
        dmix = dmix.astype(BF16)
        dmix_ref[...] = dmix
        dmerged = _dot_nt(dmix, wo_ref[...])
        sa = _sigmoid(ga_ref[...])
        sb = _sigmoid(gb_ref[...])
        dao = (dmerged * sa).astype(BF16)
        dbo = (dmerged * sb).astype(BF16)
        dao_ref[...] = dao
        dbo_ref[...] = dbo
        dga_ref[...] = (dmerged * pa_ref[...] * (sa * (1.0 - sa))).astype(BF16)
        dgb_ref[...] = (dmerged * pb_ref[...] * (sb * (1.0 - sb))).astype(BF16)
        da_ref[...] = _dot_nt(dao, wa_ref[...])
        datt_ref[...] = _dot_nt(dbo, wb_ref[...])

    sd = jax.ShapeDtypeStruct
    return pl.pallas_call(
        body, name="merge_bwd", grid=(T // tm,),
        in_specs=[_rows(tm, D)] * 6 + [_resident((D, D))] * 3 + [_const((1, D))],
        out_specs=[_rows(tm, D)] * 7 + [_const((1, D))],
        out_shape=[sd((T, D), BF16)] * 5 + [sd((T, D), F32)] * 2 + [sd((1, D), F32)],
        compiler_params=_cparams(("arbitrary",)),
    )(dx1, mix, ga, gb, pa, pb, w_a, w_b, w_o, g2)


def _ffn(x1, target, w1, w2, g3, g4, tm):
    T = x1.shape[0]

    def body(x_ref, t_ref, w1_ref, w2_ref, g3_ref, g4_ref,
             hf_ref, f2_ref, dff_ref, df1_ref, dx_ref, ls_ref, dg3_ref, dg4_ref):
        @pl.when(pl.program_id(0) == 0)
        def _():
            ls_ref[...] = jnp.zeros_like(ls_ref)
            dg3_ref[...] = jnp.zeros_like(dg3_ref)
            dg4_ref[...] = jnp.zeros_like(dg4_ref)

        x = x_ref[...]
        g3, g4 = g3_ref[...], g4_ref[...]
        xhat, r3 = _rms_hat(x)
        hf = (xhat * g3).astype(BF16)
        hf_ref[...] = hf
        rl = jnp.maximum(_dot(hf, w1_ref[...]), 0.0)
        f2 = (rl * rl).astype(BF16)
        f2_ref[...] = f2
        fhat, r4 = _rms_hat(_dot(f2, w2_ref[...]))
        err = x + fhat * g4 - t_ref[...]
        ls_ref[...] += jnp.sum(err * err, axis=0, keepdims=True)
        dy = err * (1.0 / D)
        dff, dg4 = _rms_bwd(fhat, r4, g4, dy)
        dg4_ref[...] += dg4
        dff = dff.astype(BF16)
        dff_ref[...] = dff
        df1 = (_dot_nt(dff, w2_ref[...]) * (2.0 * rl)).astype(BF16)
        df1_ref[...] = df1
        dxn, dg3 = _rms_bwd(xhat, r3, g3, _dot_nt(df1, w1_ref[...]))
        dg3_ref[...] += dg3
        dx_ref[...] = dy + dxn

    sd = jax.ShapeDtypeStruct
    return pl.pallas_call(
        body, name="ffn_fwd_bwd", grid=(T // tm,),
        in_specs=[_rows(tm, D), _rows(tm, D), _resident((D, DFF)), _resident((DFF, D)), _const((1, D)), _const((1, D))],
        out_specs=[_rows(tm, D), _rows(tm, DFF), _rows(tm, D), _rows(tm, DFF), _rows(tm, D), _const((1, D)),
                   _const((1, D)), _const((1, D))],
        out_shape=[sd((T, D), BF16), sd((T, DFF), BF16), sd((T, D), BF16), sd((T, DFF), BF16), sd((T, D), F32),
                   sd((1, D), F32), sd((1, D), F32), sd((1, D), F32)],
        compiler_params=_cparams(("arbitrary",)),
    )(x1, target, w1, w2, g3, g4)


def _inproj_bwd(parts, x, dx1, g1, w_in, tm):
    T = x.shape[0]
    widths = [p.shape[1] for p in parts]
    offs = [sum(widths[:i]) for i in range(len(widths) + 1)]
    assert offs[-1] == IN_W

    def body(*refs):
        n = len(parts)
        prefs = refs[:n]
        x_ref, dx1_ref, g_ref, w_ref, dx_ref, dp_ref, dg_ref = refs[n:]

        @pl.when(pl.program_id(0) == 0)
        def _():
            dg_ref[...] = jnp.zeros_like(dg_ref)

        dh = None
        for i in range(n):
            blk = prefs[i][...]
            dp_ref[:, offs[i]:offs[i + 1]] = blk
            t = _dot_nt(blk, w_ref[:, offs[i]:offs[i + 1]])
            dh = t if dh is None else dh + t
        xhat, r = _rms_hat(x_ref[...])
        dxn, dg = _rms_bwd(xhat, r, g_ref[...], dh)
        dg_ref[...] += dg
        dx_ref[...] = dx1_ref[...] + dxn

    sd = jax.ShapeDtypeStruct
    return pl.pallas_call(
        body, name="inproj_bwd", grid=(T // tm,),
        in_specs=[_rows(tm, w) for w in widths] + [_rows(tm, D), _rows(tm, D), _const((1, D)), _resident((D, IN_W))],
        out_specs=[_rows(tm, D), _rows(tm, IN_W), _const((1, D))],
        out_shape=[sd((T, D), F32), sd((T, IN_W), BF16), sd((1, D), F32)],
        compiler_params=_cparams(("arbitrary",)),
    )(*parts, x, dx1, g1, w_in)


def _wgrad(a, g, tn, tm, name):
    T, K = a.shape
    N = g.shape[1]

    def body(a_ref, g_ref, o_ref):
        @pl.when(pl.program_id(1) == 0)
        def _():
            o_ref[...] = jnp.zeros_like(o_ref)

        o_ref[...] += _dot_tn(a_ref[...], g_ref[...])

    return pl.pallas_call(
        body, name=name, grid=(N // tn, T // tm),
        in_specs=[pl.BlockSpec((tm, K), lambda j, t: (t, 0)), pl.BlockSpec((tm, tn), lambda j, t: (t, j))],
        out_specs=pl.BlockSpec((K, tn), lambda j, t: (0, j)),
        out_shape=jax.ShapeDtypeStruct((K, N), F32),
        compiler_params=_cparams(("parallel", "arbitrary")),
    )(a, g)


def _adamw(w, g, m, v, tr, name):
    R, C = w.shape
    bc1 = 1.0 / (1.0 - B1 ** STEP)
    bc2 = 1.0 / (1.0 - B2 ** STEP)

    def body(w_ref, g_ref, m_ref, v_ref, d_ref, nm_ref, nv_ref):
        g = g_ref[...]
        m = B1 * m_ref[...] + (1.0 - B1) * g
        v = B2 * v_ref[...] + (1.0 - B2) * (g * g)
        nm_ref[...] = m
        nv_ref[...] = v
        d_ref[...] = -LR * ((m * bc1) / (jnp.sqrt(v * bc2) + AEPS) + WD * w_ref[...])

    spec = pl.BlockSpec((tr, C), lambda i: (i, 0))
    return pl.pallas_call(
        body, name=name, grid=(R // tr,), in_specs=[spec] * 4, out_specs=[spec] * 3,
        out_shape=[jax.ShapeDtypeStruct((R, C), F32)] * 3,
        compiler_params=_cparams(("parallel",)),
    )(w, g, m, v)


BIG = (("col", (D, IN_W)), ("row", (D, D)), ("row", (D, D)), ("row", (D, D)), ("col", (D, DFF)), ("row", (DFF, D)))
NBIG = len(BIG)
ANY = pl.BlockSpec(memory_space=pl.ANY)


def _shard_shape(kind, shape):
    R, C = shape
    return (R, C // 4) if kind == "col" else (R // 4, C)


def _half_shape(kind, shape):
    R, C = shape
    return (R // 2, C) if kind == "col" else (R, C // 2)


def _piece_shape(kind, shape):
    R, C = shape
    return (R // 2, C // 4) if kind == "col" else (R // 4, C // 2)


def _own_region(ref, kind, shape, s):
    R, C = shape
    return ref.at[:, pl.ds(s * (C // 4), C // 4)] if kind == "col" else ref.at[pl.ds(s * (R // 4), R // 4), :]


def _ag_region(ref, kind, shape, s, hc):
    R, C = shape
    if kind == "col":
        return ref.at[pl.ds(hc * (R // 2), R // 2), pl.ds(s * (C // 4), C // 4)]
    return ref.at[pl.ds(s * (R // 4) + hc * (R // 8), R // 8), :]


def _ag_shard_half(ref, kind, shape, hc):
    R, C = shape
    return ref.at[pl.ds(hc * (R // 2), R // 2), :] if kind == "col" else ref.at[pl.ds(hc * (R // 8), R // 8), :]


def _grad_half(ref, kind, shape, hc):
    R, C = shape
    return ref.at[pl.ds(hc * (R // 2), R // 2), :] if kind == "col" else ref.at[:, pl.ds(hc * (C // 2), C // 2)]


def _half_piece(ref, kind, shape, s):
    R, C = shape
    return ref.at[:, pl.ds(s * (C // 4), C // 4)] if kind == "col" else ref.at[pl.ds(s * (R // 4), R // 4), :]


def _place():
    x, y, c = lax.axis_index("x"), lax.axis_index("y"), lax.axis_index("c")
    chips = [(1 - x, y), (x, 1 - y), (1 - x, 1 - y)]
    return x, y, c, chips


def _rcopy(src, dst, ssem, rsem, dev):
    return pltpu.make_async_remote_copy(src_ref=src, dst_ref=dst, send_sem=ssem, recv_sem=rsem,
                                        device_id=dev, device_id_type=MESH)


def _all_gather(shards):
    def body(*refs):
        sh, full = refs[:NBIG], refs[NBIG:2 * NBIG]
        ssem, rsem, lsem, osem = refs[2 * NBIG:2 * NBIG + 4]
        bounce = refs[2 * NBIG + 4:]
        x, y, c, chips = _place()
        me_s, sib = 2 * x + y, (x, y, 1 - c)
        local, sends = [], []
        loads = [pltpu.make_async_copy(sh[w], bounce[w], lsem.at[w]) for w in range(NBIG)]
        for cp in loads:
            cp.start()
        for w, (kind, shape) in enumerate(BIG):
            for j, (cx, cy) in enumerate(chips):
                cp = _rcopy(_ag_shard_half(sh[w], kind, shape, c), _ag_region(full[w], kind, shape, me_s, c),
                            ssem.at[3 * w + j], rsem.at[3 * w + j], (cx, cy, c))
                cp.start()
                sends.append(cp)
        for w, (kind, shape) in enumerate(BIG):
            loads[w].wait()
            cp = pltpu.make_async_copy(bounce[w], _own_region(full[w], kind, shape, me_s), osem.at[w])
            cp.start()
            local.append(cp)
        for w, (kind, shape) in enumerate(BIG):
            for j, (cx, cy) in enumerate(chips):
                reg = _ag_region(full[w], kind, shape, 2 * cx + cy, c)
                _rcopy(reg, reg, ssem.at[3 * w + j], rsem.at[3 * w + j], sib).wait_recv()
                k = 3 * NBIG + 3 * w + j
                cp = _rcopy(reg, reg, ssem.at[k], rsem.at[k], sib)
                cp.start()
                sends.append(cp)
        for w, (kind, shape) in enumerate(BIG):
            for j, (cx, cy) in enumerate(chips):
                reg = _ag_region(full[w], kind, shape, 2 * cx + cy, 1 - c)
                k = 3 * NBIG + 3 * w + j
                _rcopy(reg, reg, ssem.at[k], rsem.at[k], sib).wait_recv()
        for cp in sends:
            cp.wait_send()
        for cp in local:
            cp.wait()

    return pl.pallas_call(
        body, name="weights_all_gather", in_specs=[ANY] * NBIG, out_specs=[ANY] * NBIG,
        out_shape=[jax.ShapeDtypeStruct(shape, BF16) for _, shape in BIG],
        scratch_shapes=[pltpu.SemaphoreType.DMA((6 * NBIG,)), pltpu.SemaphoreType.DMA((6 * NBIG,)),
                        pltpu.SemaphoreType.DMA((NBIG,)), pltpu.SemaphoreType.DMA((NBIG,))]
        + [pltpu.VMEM(_shard_shape(k, s), BF16) for k, s in BIG],
        compiler_params=pltpu.CompilerParams(vmem_limit_bytes=VMEM_LIMIT),
    )(*shards)


def _rs_sibling(grads):
    def body(*refs):
        g, got = refs[:NBIG], refs[NBIG:2 * NBIG]
        ssem, rsem = refs[2 * NBIG:]
        x, y, c, _ = _place()
        sib = (x, y, 1 - c)
        cps = []
        for w, (kind, shape) in enumerate(BIG):
            rc = _rcopy(_grad_half(g[w], kind, shape, 1 - c), got[w], ssem.at[w], rsem.at[w], sib)
            rc.start()
            cps.append(rc)
        for rc in cps:
            rc.wait_recv()
        for rc in cps:
            rc.wait_send()

    return pl.pallas_call(
        body, name="grads_to_sibling", in_specs=[ANY] * NBIG, out_specs=[ANY] * NBIG,
        out_shape=[jax.ShapeDtypeStruct(_half_shape(k, s), F32) for k, s in BIG],
        scratch_shapes=[pltpu.SemaphoreType.DMA((NBIG,)), pltpu.SemaphoreType.DMA((NBIG,))],
    )(*grads)


def _rs_chips(sums_bf):
    def body(*refs):
        s16, got = refs[:NBIG], refs[NBIG:2 * NBIG]
        ssem, rsem = refs[2 * NBIG:]
        x, y, c, chips = _place()
        sends = []
        for w, (kind, shape) in enumerate(BIG):
            for j, (cx, cy) in enumerate(chips):
                cp = _rcopy(_half_piece(s16[w], kind, shape, 2 * cx + cy), got[w].at[j],
                            ssem.at[3 * w + j], rsem.at[3 * w + j], (cx, cy, c))
                cp.start()
                sends.append(cp)
        for cp in sends:
            cp.wait_recv()
        for cp in sends:
            cp.wait_send()

    return pl.pallas_call(
        body, name="grads_to_chips", in_specs=[ANY] * NBIG, out_specs=[ANY] * NBIG,
        out_shape=[jax.ShapeDtypeStruct((3,) + _piece_shape(k, s), BF16) for k, s in BIG],
        scratch_shapes=[pltpu.SemaphoreType.DMA((3 * NBIG,)), pltpu.SemaphoreType.DMA((3 * NBIG,))],
    )(*sums_bf)


def _shard_half(ref, kind, shape, hc):
    sr, sc = _shard_shape(kind, shape)
    return ref.at[pl.ds(hc * (sr // 2), sr // 2), :] if kind == "col" else ref.at[:, pl.ds(hc * (sc // 2), sc // 2)]


def _rs_share(shard_grads):
    def body(*refs):
        g = refs[NBIG:2 * NBIG]
        ssem, rsem = refs[2 * NBIG:]
        x, y, c, _ = _place()
        sib = (x, y, 1 - c)
        cps = []
        for w, (kind, shape) in enumerate(BIG):
            part = _shard_half(g[w], kind, shape, c)
            rc = _rcopy(part, part, ssem.at[w], rsem.at[w], sib)
            rc.start()
            cps.append(rc)
        for w, (kind, shape) in enumerate(BIG):
            part = _shard_half(g[w], kind, shape, 1 - c)
            _rcopy(part, part, ssem.at[w], rsem.at[w], sib).wait_recv()
        for rc in cps:
            rc.wait_send()

    return pl.pallas_call(
        body, name="grads_share", in_specs=[ANY] * NBIG, out_specs=[ANY] * NBIG,
        out_shape=[jax.ShapeDtypeStruct(_shard_shape(k, s), F32) for k, s in BIG],
        input_output_aliases={w: w for w in range(NBIG)},
        scratch_shapes=[pltpu.SemaphoreType.DMA((NBIG,)), pltpu.SemaphoreType.DMA((NBIG,))],
    )(*shard_grads)


ADD_ROWS = 256


def _add_halves(place, g, got, kind, name):
    R, C = g.shape
    hr, hcols = _half_shape(kind, (R, C))
    steps = hr // ADD_ROWS

    def body(p_ref, g_ref, b_ref, s_ref, sb_ref):
        s = g_ref[...] + b_ref[...]
        s_ref[...] = s
        sb_ref[...] = s.astype(BF16)

    if kind == "col":
        g_spec = pl.BlockSpec((ADD_ROWS, C), lambda i, p: (p[0] * steps + i, 0))
    else:
        g_spec = pl.BlockSpec((ADD_ROWS, hcols), lambda i, p: (i, p[0]))
    spec = pl.BlockSpec((ADD_ROWS, hcols), lambda i, p: (i, 0))
    return pl.pallas_call(
        body, name=name,
        grid_spec=pltpu.PrefetchScalarGridSpec(num_scalar_prefetch=1, grid=(steps,), in_specs=[g_spec, spec],
                                               out_specs=[spec, spec]),
        out_shape=[jax.ShapeDtypeStruct((hr, hcols), F32), jax.ShapeDtypeStruct((hr, hcols), BF16)],
        compiler_params=_cparams(("parallel",)),
    )(place, g, got)


def _add_pieces(place, half, got, kind, shape, name):
    pr, pc = _piece_shape(kind, shape)
    steps = pr // ADD_ROWS

    def body(p_ref, m_ref, g_ref, o_ref):
        acc = m_ref[...]
        for j in range(3):
            acc = acc + g_ref[j].astype(F32)
        o_ref[...] = acc

    if kind == "col":
        m_spec = pl.BlockSpec((ADD_ROWS, pc), lambda i, p: (i, p[1]))
        o_spec = pl.BlockSpec((ADD_ROWS, pc), lambda i, p: (p[0] * steps + i, 0))
    else:
        m_spec = pl.BlockSpec((ADD_ROWS, pc), lambda i, p: (p[1] * steps + i, 0))
        o_spec = pl.BlockSpec((ADD_ROWS, pc), lambda i, p: (i, p[0]))
    return pl.pallas_call(
        body, name=name,
        grid_spec=pltpu.PrefetchScalarGridSpec(
            num_scalar_prefetch=1, grid=(steps,),
            in_specs=[m_spec, pl.BlockSpec((3, ADD_ROWS, pc), lambda i, p: (0, i, 0))], out_specs=o_spec),
        out_shape=jax.ShapeDtypeStruct(_shard_shape(kind, shape), F32),
        compiler_params=_cparams(("parallel",)),
    )(place, half, got)


SMALL_ROWS = 1024 + 8 * 8


def _small_all_reduce(p):
    def body(p_ref, o_ref, slots, ssem, rsem):
        x, y, c = lax.axis_index("x"), lax.axis_index("y"), lax.axis_index("c")
        me = 4 * x + 2 * y + c
        slots[me] = p_ref[...]
        cps = []
        for r in range(1, 8):
            bx, by, bc = (r >> 2) & 1, (r >> 1) & 1, r & 1
            tgt = (1 - x if bx else x, 1 - y if by else y, 1 - c if bc else c)
            cp = _rcopy(p_ref, slots.at[me], ssem.at[r - 1], rsem.at[r - 1], tgt)
            cp.start()
            cps.append((cp, 4 * tgt[0] + 2 * tgt[1] + tgt[2]))
        for r, (cp, src) in enumerate(cps):
            _rcopy(p_ref, slots.at[src], ssem.at[r], rsem.at[r], (x, y, c)).wait_recv()
        acc = slots[0]
        for d in range(1, 8):
            acc = acc + slots[d]
        o_ref[...] = acc
        for cp, _ in cps:
            cp.wait_send()

    vm = pl.BlockSpec(memory_space=pltpu.VMEM)
    return pl.pallas_call(
        body, name="small_all_reduce", in_specs=[vm], out_specs=vm,
        out_shape=jax.ShapeDtypeStruct((SMALL_ROWS, CH), F32),
        scratch_shapes=[pltpu.VMEM((8, SMALL_ROWS, CH), F32), pltpu.SemaphoreType.DMA((7,)), pltpu.SemaphoreType.DMA((7,))],
    )(p)


def _rope_tables(positions):
    inv_freq = 500000.0 ** (-jnp.arange(0, 2 * ROPE_HALF, 2, dtype=F32) / (2 * ROPE_HALF))
    head = jnp.concatenate([inv_freq, inv_freq, jnp.zeros((HD - 2 * ROPE_HALF,), F32)])
    lane_freq = jnp.concatenate([head, head])
    ang = positions.astype(F32)[:, None] * lane_freq[None, :]
    cos, sin = jnp.cos(ang), jnp.sin(ang)
    first = (jnp.arange(CH) % HD) < ROPE_HALF
    return cos, jnp.where(first[None, :], -sin, 0.0), jnp.where(first[None, :], 0.0, sin)


def _local_step(x, positions, w_in, w_a, w_b, w_o, w_ff_in, w_ff_out, ln_g, ln_b, w_sp, b_sp, sinks,
                g1, g2, g3, g4, target):
    rc, rs1, rs2 = _rope_tables(positions)
    bfull = jnp.repeat(b_sp.T, CH, axis=1)
    h, u, vs, q, k, va, ga, gb = _inproj(x, g1, w_in, rc, rs1, rs2, tm=256)
    a = _sgu_fwd(u, vs, ln_g, ln_b, w_sp, bfull, tm=512)
    att = _attn_fwd(q, k, va, sinks)
    pa, pb, merged, mix, x1 = _merge_fwd(a, att, ga, gb, x, w_a, w_b, w_o, g2, tm=256)
    hf, f2, dff, df1, dx1, lsum, dg3, dg4 = _ffn(x1, target, w_ff_in, w_ff_out, g3, g4, tm=256)
    wtm = min(1024, x.shape[0])
    dw_ff_out = _wgrad(f2, dff, tn=512, tm=512, name="wgrad_ff_out")
    dw_ff_in = _wgrad(hf, df1, tn=1024, tm=wtm, name="wgrad_ff_in")
    dmix, dao, dbo, dga, dgb, da, datt, dg2 = _merge_bwd(dx1, mix, ga, gb, pa, pb, w_a, w_b, w_o, g2, tm=256)
    dw_o = _wgrad(merged, dmix, tn=1024, tm=wtm, name="wgrad_o")
    dw_a = _wgrad(a, dao, tn=1024, tm=wtm, name="wgrad_a")
    dw_b = _wgrad(att, dbo, tn=1024, tm=wtm, name="wgrad_b")
    du, dvs, dws, dbs, dlg, dlb = _sgu_bwd(u, vs, da, ln_g, ln_b, w_sp, bfull, tm=512)
    dq, dk, dva, dsk = _attn_bwd(q, k, va, datt, sinks, rc, rs1, rs2)
    dx, dproj, dg1 = _inproj_bwd([du, dvs, dq, dk, dva, dga, dgb], x, dx1, g1, w_in, tm=256)
    dw_in = _wgrad(h, dproj, tn=IN_W // 4, tm=wtm, name="wgrad_in")
    big = (dw_in, dw_a, dw_b, dw_o, dw_ff_in, dw_ff_out)
    small = dict(ln_v_gain=dlg, ln_v_bias=dlb, w_spatial=dws, b_spatial=dbs, sinks=dsk[:, :NQ],
                 norm_mix_pre=dg1, norm_mix_post=dg2, norm_ff_pre=dg3, norm_ff_post=dg4)
    return lsum, dx, big, small


BIG_NAMES = ("w_in", "w_a", "w_b", "w_o", "w_ff_in", "w_ff_out")
SMALL_NAMES = ("w_spatial", "ln_v_gain", "ln_v_bias", "b_spatial", "sinks", "norm_mix_pre", "norm_mix_post",
               "norm_ff_pre", "norm_ff_post")
WEIGHT_ORDER = ("w_in", "ln_v_gain", "ln_v_bias", "w_spatial", "b_spatial", "sinks", "w_a", "w_b", "w_o",
                "norm_mix_pre", "norm_mix_post", "w_ff_in", "w_ff_out", "norm_ff_pre", "norm_ff_post")


def _pack_small(d):
    parts = []
    for n in SMALL_NAMES:
        flat = d[n].reshape(-1)
        pad = (-flat.shape[0]) % (8 * CH)
        parts.append(jnp.pad(flat, (0, pad)).reshape(-1, CH))
    return jnp.concatenate(parts, axis=0)


def _unpack_small(p, like):
    out, row = {}, 0
    for n in SMALL_NAMES:
        size = like[n].size
        rows = -(-size // (8 * CH)) * 8
        out[n] = p[row:row + rows].reshape(-1)[:size].reshape(like[n].shape)
        row += rows
    return out


def kernel(x, positions, w_in, ln_v_gain, ln_v_bias, w_spatial, b_spatial, sinks, w_a, w_b, w_o, norm_mix_pre, norm_mix_post, w_ff_in, w_ff_out, norm_ff_pre, norm_ff_post, loss_target, m_w_in, m_ln_v_gain, m_ln_v_bias, m_w_spatial, m_b_spatial, m_sinks, m_w_a, m_w_b, m_w_o, m_norm_mix_pre, m_norm_mix_post, m_w_ff_in, m_w_ff_out, m_norm_ff_pre, m_norm_ff_post, v_w_in, v_ln_v_gain, v_ln_v_bias, v_w_spatial, v_b_spatial, v_sinks, v_w_a, v_w_b, v_w_o, v_norm_mix_pre, v_norm_mix_post, v_w_ff_in, v_w_ff_out, v_norm_ff_pre, v_norm_ff_post):
    w = dict(w_in=w_in, ln_v_gain=ln_v_gain, ln_v_bias=ln_v_bias, w_spatial=w_spatial, b_spatial=b_spatial, sinks=sinks,
             w_a=w_a, w_b=w_b, w_o=w_o, norm_mix_pre=norm_mix_pre, norm_mix_post=norm_mix_post, w_ff_in=w_ff_in,
             w_ff_out=w_ff_out, norm_ff_pre=norm_ff_pre, norm_ff_post=norm_ff_post)
    m = dict(w_in=m_w_in, ln_v_gain=m_ln_v_gain, ln_v_bias=m_ln_v_bias, w_spatial=m_w_spatial, b_spatial=m_b_spatial,
             sinks=m_sinks, w_a=m_w_a, w_b=m_w_b, w_o=m_w_o, norm_mix_pre=m_norm_mix_pre, norm_mix_post=m_norm_mix_post,
             w_ff_in=m_w_ff_in, w_ff_out=m_w_ff_out, norm_ff_pre=m_norm_ff_pre, norm_ff_post=m_norm_ff_post)
    v = dict(w_in=v_w_in, ln_v_gain=v_ln_v_gain, ln_v_bias=v_ln_v_bias, w_spatial=v_w_spatial, b_spatial=v_b_spatial,
             sinks=v_sinks, w_a=v_w_a, w_b=v_w_b, w_o=v_w_o, norm_mix_pre=v_norm_mix_pre, norm_mix_post=v_norm_mix_post,
             w_ff_in=v_w_ff_in, w_ff_out=v_w_ff_out, norm_ff_pre=v_norm_ff_pre, norm_ff_post=v_norm_ff_post)

    whole = _all_gather([w[n][0].astype(BF16) for n in BIG_NAMES])
    lsum, dx, big, small = _local_step(
        x[0], positions[0], *whole, ln_v_gain, ln_v_bias, w_spatial[0], b_spatial[0], sinks[0],
        norm_mix_pre, norm_mix_post, norm_ff_pre, norm_ff_post, loss_target[0])
    loss = lax.psum(0.5 * jnp.sum(lsum) / D, ("x", "y", "c"))

    place = jnp.stack([lax.axis_index("c"), 2 * lax.axis_index("x") + lax.axis_index("y")]).astype(jnp.int32)
    got = _rs_sibling(big)
    sums = [_add_halves(place, big[i], got[i], BIG[i][0], name="grad_add_sibling_" + BIG_NAMES[i]) for i in range(NBIG)]
    pieces = _rs_chips([s[1] for s in sums])
    partial = [_add_pieces(place, sums[i][0], pieces[i], *BIG[i], name="grad_add_chips_" + BIG_NAMES[i])
               for i in range(NBIG)]
    shard_grads = _rs_share(partial)

    grad, delta, new_m, new_v = {}, {}, {}, {}
    for i, n in enumerate(BIG_NAMES):
        g = shard_grads[i]
        d_, m_, v_ = _adamw(w[n][0], g, m[n][0], v[n][0], tr=256, name="adamw_" + n)
        grad[n], delta[n], new_m[n], new_v[n] = g[None], d_[None], m_[None], v_[None]

    gs = _small_all_reduce(_pack_small(small))
    ds, ms, vs = _adamw(_pack_small(w), gs, _pack_small(m), _pack_small(v), tr=SMALL_ROWS // 4, name="adamw_small")
    for packed, dst in ((gs, grad), (ds, delta), (ms, new_m), (vs, new_v)):
        dst.update(_unpack_small(packed, w))

    outs = [loss, dx[None]]
    for group in (grad, delta, new_m, new_v):
        outs.extend(group[n] for n in WEIGHT_ORDER)
    return tuple(outs)
```

```python
import functools

import jax
import jax.numpy as jnp
from jax import lax
from jax.experimental import pallas as pl
from jax.experimental.pallas import tpu as pltpu

F32 = jnp.float32
BF16 = jnp.bfloat16

D = 1024
CH = 128
NG = 8
HD = 64
NQ = 16
NKV = 4
KVW = NKV * HD
DFF = 4 * D
EPS = 1e-6
IN_W = 5632
SEG = (0, 1024, 2048, 3072, 3328, 3584, 4608, 5632)
ROPE_HALF = 8
Q_SCALE = HD ** -0.5

LR, B1, B2, AEPS, WD, STEP = 0.001, 0.9, 0.999, 1e-08, 0.01, 10

VMEM_LIMIT = 56 * 1024 * 1024
MESH = pl.DeviceIdType.MESH

_GELU_C0 = 0.7978845608028654
_GELU_C1 = 0.044715


def _cparams(sem=None):
    kw = dict(vmem_limit_bytes=VMEM_LIMIT)
    if sem is not None:
        kw["dimension_semantics"] = sem
    return pltpu.CompilerParams(**kw)


def _resident(shape):
    nd = len(shape)
    return pl.BlockSpec(shape, lambda *_: (0,) * nd, pipeline_mode=pl.Buffered(1))


def _const(shape):
    nd = len(shape)
    return pl.BlockSpec(shape, lambda *_: (0,) * nd)


def _rows(tm, w):
    return pl.BlockSpec((tm, w), lambda i: (i, 0))


def _gelu(x):
    t = jnp.tanh(_GELU_C0 * (x + _GELU_C1 * (x * x * x)))
    return 0.5 * x * (1.0 + t), t


def _gelu_grad(x, t):
    return 0.5 * (1.0 + t) + 0.5 * x * (1.0 - t * t) * (_GELU_C0 * (1.0 + 3.0 * _GELU_C1 * x * x))


def _sigmoid(x):
    return 1.0 / (1.0 + jnp.exp(-x))


def _rms_hat(x):
    r = lax.rsqrt(jnp.mean(x * x, axis=-1, keepdims=True) + EPS)
    return x * r, r


def _rms_bwd(xhat, r, g, dout):
    dg = jnp.sum(dout * xhat, axis=0, keepdims=True)
    dy = dout * g
    dx = r * (dy - xhat * jnp.mean(dy * xhat, axis=-1, keepdims=True))
    return dx, dg


def _dot(a, b):
    return jnp.dot(a, b, preferred_element_type=F32)


def _dot_nt(a, b):
    return lax.dot_general(a, b, (((1,), (1,)), ((), ())), preferred_element_type=F32)


def _dot_tn(a, b):
    return lax.dot_general(a, b, (((0,), (0,)), ((), ())), preferred_element_type=F32)


def _rope(blk, c, s1, s2):
    return blk * c + pltpu.roll(blk, CH - ROPE_HALF, 1) * s1 + pltpu.roll(blk, ROPE_HALF, 1) * s2


def _rope_t(blk, c, s1, s2):
    return blk * c + pltpu.roll(blk * s1, ROPE_HALF, 1) + pltpu.roll(blk * s2, CH - ROPE_HALF, 1)


def _inproj(x, g1, w_in, rc, rs1, rs2, tm):
    T = x.shape[0]

    def body(x_ref, g_ref, w_ref, c_ref, s1_ref, s2_ref,
             h_ref, u_ref, v_ref, q_ref, k_ref, va_ref, ga_ref, gb_ref):
        xhat, _ = _rms_hat(x_ref[...])
        h = (xhat * g_ref[...]).astype(BF16)
        h_ref[...] = h
        u_ref[...] = _dot(h, w_ref[:, SEG[0]:SEG[1]])
        v_ref[...] = _dot(h, w_ref[:, SEG[1]:SEG[2]])
        c, s1, s2 = c_ref[...], s1_ref[...], s2_ref[...]
        q = _dot(h, w_ref[:, SEG[2]:SEG[3]])
        for p in range(D // CH):
            blk = _rope(q[:, CH * p:CH * (p + 1)], c, s1, s2) * Q_SCALE
            q_ref[:, CH * p:CH * (p + 1)] = blk.astype(BF16)
        k = _dot(h, w_ref[:, SEG[3]:SEG[4]])
        for p in range(KVW // CH):
            k_ref[:, CH * p:CH * (p + 1)] = _rope(k[:, CH * p:CH * (p + 1)], c, s1, s2).astype(BF16)
        va_ref[...] = _dot(h, w_ref[:, SEG[4]:SEG[5]]).astype(BF16)
        ga_ref[...] = _dot(h, w_ref[:, SEG[5]:SEG[6]])
        gb_ref[...] = _dot(h, w_ref[:, SEG[6]:SEG[7]])

    sd = jax.ShapeDtypeStruct
    return pl.pallas_call(
        body, name="inproj_fwd", grid=(T // tm,),
        in_specs=[_rows(tm, D), _const((1, D)), _resident((D, IN_W)), _rows(tm, CH), _rows(tm, CH), _rows(tm, CH)],
        out_specs=[_rows(tm, D), _rows(tm, D), _rows(tm, D), _rows(tm, D), _rows(tm, KVW), _rows(tm, KVW),
                   _rows(tm, D), _rows(tm, D)],
        out_shape=[sd((T, D), BF16), sd((T, D), F32), sd((T, D), F32), sd((T, D), BF16), sd((T, KVW), BF16),
                   sd((T, KVW), BF16), sd((T, D), F32), sd((T, D), F32)],
        compiler_params=_cparams(("parallel",)),
    )(x, g1, w_in, rc, rs1, rs2)


def _sgu_common(u, vs, lng, lnb, ws_ref, bfull):
    nc = u.shape[0] // CH
    ug, tu = _gelu(u)
    vg, tv = _gelu(vs)
    mu = jnp.mean(vg, axis=-1, keepdims=True)
    xc = vg - mu
    rstd = lax.rsqrt(jnp.mean(xc * xc, axis=-1, keepdims=True) + EPS)
    vhat = xc * rstd
    vnb = (vhat * lng + lnb).astype(BF16)
    tri = lax.broadcasted_iota(jnp.int32, (CH, CH), 0) >= lax.broadcasted_iota(jnp.int32, (CH, CH), 1)
    wts, rhss, mixed = [], [], []
    for g in range(NG):
        wt = jnp.where(tri, ws_ref[g], 0.0).astype(BF16)
        rhs = jnp.concatenate([vnb[CH * c:CH * (c + 1), CH * g:CH * (g + 1)] for c in range(nc)], axis=1)
        mix = _dot(wt, rhs)
        wts.append(wt)
        rhss.append(rhs)
        mixed.append([mix[:, CH * c:CH * (c + 1)] + bfull[:, CH * g:CH * (g + 1)] for c in range(nc)])
    return nc, ug, tu, tv, rstd, vhat, tri, wts, rhss, mixed


def _sgu_fwd(u, vs, lng, lnb, ws, bfull, tm):
    T = u.shape[0]

    def body(u_ref, v_ref, lng_ref, lnb_ref, ws_ref, bf_ref, a_ref):
        nc, ug, _, _, _, _, _, _, _, mixed = _sgu_common(
            u_ref[...], v_ref[...], lng_ref[...], lnb_ref[...], ws_ref, bf_ref[...])
        mixed_all = jnp.concatenate(
            [jnp.concatenate([mixed[g][c] for g in range(NG)], axis=1) for c in range(nc)], axis=0)
        a_ref[...] = (ug * mixed_all).astype(BF16)

    return pl.pallas_call(
        body, name="sgu_fwd", grid=(T // tm,),
        in_specs=[_rows(tm, D), _rows(tm, D), _const((1, D)), _const((1, D)), _const((NG, CH, CH)), _const((CH, D))],
        out_specs=_rows(tm, D), out_shape=jax.ShapeDtypeStruct((T, D), BF16),
        compiler_params=_cparams(("parallel",)),
    )(u, vs, lng, lnb, ws, bfull)


def _sgu_bwd(u, vs, da, lng, lnb, ws, bfull, tm):
    T = u.shape[0]
    nsteps = T // tm

    def body(u_ref, v_ref, da_ref, lng_ref, lnb_ref, ws_ref, bf_ref,
             du_ref, dv_ref, dws_ref, dbs_ref, dlg_ref, dlb_ref, db_ref):
        i = pl.program_id(0)
        u, vs, da, lng = u_ref[...], v_ref[...], da_ref[...], lng_ref[...]
        nc, ug, tu, tv, rstd, vhat, tri, wts, rhss, mixed = _sgu_common(u, vs, lng, lnb_ref[...], ws_ref, bf_ref[...])

        @pl.when(i == 0)
        def _():
            dws_ref[...] = jnp.zeros_like(dws_ref)
            db_ref[...] = jnp.zeros_like(db_ref)
            dlg_ref[...] = jnp.zeros_like(dlg_ref)
            dlb_ref[...] = jnp.zeros_like(dlb_ref)

        mixed_all = jnp.concatenate(
            [jnp.concatenate([mixed[g][c] for g in range(NG)], axis=1) for c in range(nc)], axis=0)
        du_ref[...] = (da * mixed_all * _gelu_grad(u, tu)).astype(BF16)
        dmixed = da * ug
        dvn_cols = []
        for g in range(NG):
            dmix = [dmixed[CH * c:CH * (c + 1), CH * g:CH * (g + 1)] for c in range(nc)]
            db_ref[:, CH * g:CH * (g + 1)] += functools.reduce(lambda a, b: a + b, dmix)
            dm = jnp.concatenate(dmix, axis=1).astype(BF16)
            dws_ref[g] += _dot_nt(dm, rhss[g])
            dvn_cols.append(_dot_tn(wts[g], dm))
        dvn = jnp.concatenate(
            [jnp.concatenate([dvn_cols[g][:, CH * c:CH * (c + 1)] for g in range(NG)], axis=1) for c in range(nc)],
            axis=0)
        dlg_ref[...] += jnp.sum(dvn * vhat, axis=0, keepdims=True)
        dlb_ref[...] += jnp.sum(dvn, axis=0, keepdims=True)
        dvh = dvn * lng
        dvg = rstd * (dvh - jnp.mean(dvh, axis=-1, keepdims=True)
                      - vhat * jnp.mean(dvh * vhat, axis=-1, keepdims=True))
        dv_ref[...] = (dvg * _gelu_grad(vs, tv)).astype(BF16)

        @pl.when(i == nsteps - 1)
        def _():
            for g in range(NG):
                dws_ref[g] = jnp.where(tri, dws_ref[g], 0.0)
                dbs_ref[g:g + 1, :] = jnp.sum(db_ref[:, CH * g:CH * (g + 1)].T, axis=0, keepdims=True)

    sd = jax.ShapeDtypeStruct
    return pl.pallas_call(
        body, name="sgu_bwd", grid=(nsteps,),
        in_specs=[_rows(tm, D), _rows(tm, D), _rows(tm, D), _const((1, D)), _const((1, D)), _const((NG, CH, CH)),
                  _const((CH, D))],
        out_specs=[_rows(tm, D), _rows(tm, D), _const((NG, CH, CH)), _const((NG, CH)), _const((1, D)), _const((1, D))],
        out_shape=[sd((T, D), BF16), sd((T, D), BF16), sd((NG, CH, CH), F32), sd((NG, CH), F32), sd((1, D), F32),
                   sd((1, D), F32)],
        scratch_shapes=[pltpu.VMEM((CH, D), F32)],
        compiler_params=_cparams(("arbitrary",)),
    )(u, vs, da, lng, lnb, ws, bfull)


def _pair_layout(prev, cur, grp):
    j, half = grp // 2, grp % 2
    blk = jnp.concatenate([prev[:, CH * j:CH * (j + 1)], cur[:, CH * j:CH * (j + 1)]], axis=0).astype(F32)
    lo = lax.broadcasted_iota(jnp.int32, blk.shape, 1) < HD
    rolled = pltpu.roll(blk, HD, 1)
    even = jnp.where(lo, blk if half == 0 else rolled, 0.0)
    odd = jnp.where(lo, 0.0, rolled if half == 0 else blk)
    return jnp.concatenate([even, odd], axis=0).astype(BF16)


def _unpair(acc, grp):
    half = grp % 2
    lo = lax.broadcasted_iota(jnp.int32, (2 * CH, CH), 1) < HD
    ev, od = acc[:2 * CH], acc[2 * CH:]
    if half == 0:
        return jnp.where(lo, ev + pltpu.roll(od, HD, 1), 0.0)
    return jnp.where(lo, 0.0, pltpu.roll(ev, HD, 1) + od)


def _attn_mask(n):
    qi = lax.broadcasted_iota(jnp.int32, (CH, 2 * CH), 0)
    kc = lax.broadcasted_iota(jnp.int32, (CH, 2 * CH), 1)
    ok = (kc > qi) & (kc <= qi + CH) & ((kc >= CH) | (n > 0))
    return jnp.concatenate([ok, ok], axis=1)


def _softmax_sink(s, sink):
    m = jnp.maximum(jnp.max(s, axis=-1, keepdims=True), sink)
    p = jnp.exp(s - m)
    ps = jnp.exp(sink - m)
    inv = 1.0 / (jnp.sum(p, axis=-1, keepdims=True) + ps)
    return p * inv, ps * inv


def _attn_fwd(q, k, va, sinks):
    T = q.shape[0]
    nb = T // CH

    def body(sk_ref, q_ref, kp_ref, kc_ref, vp_ref, vc_ref, o_ref):
        n = pl.program_id(0)
        mask = _attn_mask(n)
        kp, kc, vp, vc = kp_ref[...], kc_ref[...], vp_ref[...], vc_ref[...]
        kks = [_pair_layout(kp, kc, grp) for grp in range(NKV)]
        vvs = [_pair_layout(vp, vc, grp) for grp in range(NKV)]
        npairs = D // CH

        def scores(p):
            return _dot_nt(q_ref[:, CH * p:CH * (p + 1)], kks[p // 2])

        ahead = 3
        outs, probs = [], []
        pending = [scores(p) for p in range(ahead)]
        for p in range(npairs):
            s = jnp.where(mask, pending.pop(0), -1e30)
            if p + ahead < npairs:
                pending.append(scores(p + ahead))
            pe, _ = _softmax_sink(s[:, :2 * CH], sk_ref[2 * p])
            po, _ = _softmax_sink(s[:, 2 * CH:], sk_ref[2 * p + 1])
            probs.append(jnp.concatenate([pe, po], axis=1).astype(BF16))
            if p >= 1:
                outs.append(_dot(probs[p - 1], vvs[(p - 1) // 2]))
        outs.append(_dot(probs[-1], vvs[-1]))
        o_ref[...] = jnp.concatenate(outs, axis=1).astype(BF16)

    prev = lambda n: (jnp.maximum(n - 1, 0), 0)
    cur = lambda n: (n, 0)
    return pl.pallas_call(
        body, name="attn_fwd", grid=(nb,),
        in_specs=[pl.BlockSpec(memory_space=pltpu.SMEM), pl.BlockSpec((CH, D), cur),
                  pl.BlockSpec((CH, KVW), prev), pl.BlockSpec((CH, KVW), cur),
                  pl.BlockSpec((CH, KVW), prev), pl.BlockSpec((CH, KVW), cur)],
        out_specs=pl.BlockSpec((CH, D), cur), out_shape=jax.ShapeDtypeStruct((T, D), BF16),
        compiler_params=_cparams(("parallel",)),
    )(sinks, q, k, k, va, va)


def _attn_bwd(q, k, va, datt, sinks, rc, rs1, rs2):
    T = q.shape[0]
    nb = T // CH

    def body(sk_ref, q_ref, kp_ref, kc_ref, vp_ref, vc_ref, do_ref, cq_ref, s1q_ref, s2q_ref, ck_ref, s1k_ref, s2k_ref,
             dq_ref, dk_ref, dv_ref, dsk_ref, kcar, vcar):
        n = pl.program_id(0)

        @pl.when(n == 0)
        def _():
            kcar[...] = jnp.zeros_like(kcar)
            vcar[...] = jnp.zeros_like(vcar)
            dsk_ref[...] = jnp.zeros_like(dsk_ref)

        def flush(kprev, vprev):
            ck, s1k, s2k = ck_ref[...], s1k_ref[...], s2k_ref[...]
            for j in range(KVW // CH):
                sl = slice(CH * j, CH * (j + 1))
                dk_ref[:, sl] = _rope_t(kcar[:, sl] + kprev[:, sl], ck, s1k, s2k).astype(BF16)
                dv_ref[:, sl] = (vcar[:, sl] + vprev[:, sl]).astype(BF16)

        @pl.when(n < nb)
        def _():
            mask = _attn_mask(n)
            kp, kc, vp, vc = kp_ref[...], kc_ref[...], vp_ref[...], vc_ref[...]
            cq, s1q, s2q = cq_ref[...], s1q_ref[...], s2q_ref[...]
            lane = lax.broadcasted_iota(jnp.int32, (1, CH), 1)
            dsk = jnp.zeros((1, CH), F32)
            npairs = D // CH
            kks = [_pair_layout(kp, kc, grp) for grp in range(NKV)]
            vvs = [_pair_layout(vp, vc, grp) for grp in range(NKV)]
            qs = [q_ref[:, CH * p:CH * (p + 1)] for p in range(npairs)]
            dos = [do_ref[:, CH * p:CH * (p + 1)].astype(BF16) for p in range(npairs)]

            def first(p):
                return _dot_nt(qs[p], kks[p // 2]), _dot_nt(dos[p], vvs[p // 2])

            def last(p, ds, pb):
                return (_rope_t(_dot(ds, kks[p // 2]), cq, s1q, s2q) * Q_SCALE, _dot_tn(ds, qs[p]), _dot_tn(pb, dos[p]))

            ahead = 2
            pending = [first(p) for p in range(ahead)]
            mids, ends = [], []
            for p in range(npairs):
                s, dp = pending.pop(0)
                s = jnp.where(mask, s, -1e30)
                if p + ahead < npairs:
                    pending.append(first(p + ahead))
                ds_parts, p_parts = [], []
                for par in range(2):
                    sl = slice(2 * CH * par, 2 * CH * (par + 1))
                    pr, psink = _softmax_sink(s[:, sl], sk_ref[2 * p + par])
                    delta = jnp.sum(pr * dp[:, sl], axis=-1, keepdims=True)
                    ds_parts.append(pr * (dp[:, sl] - delta))
                    p_parts.append(pr)
                    tot = -jnp.sum(psink * delta, axis=0, keepdims=True)
                    dsk = dsk + jnp.where(lane == 2 * p + par, tot, 0.0)
                mids.append((jnp.concatenate(ds_parts, axis=1).astype(BF16), jnp.concatenate(p_parts, axis=1).astype(BF16)))
                if p >= 1:
                    ends.append(last(p - 1, *mids[p - 1]))
            ends.append(last(npairs - 1, *mids[-1]))
            dq_cols = [e[0] for e in ends]
            dk_cols, dv_cols = [], []
            for j in range(KVW // CH):
                dkj, dvj = None, None
                for grp in (2 * j, 2 * j + 1):
                    ck_ = _unpair(ends[2 * grp][1] + ends[2 * grp + 1][1], grp)
                    cv_ = _unpair(ends[2 * grp][2] + ends[2 * grp + 1][2], grp)
                    dkj = ck_ if dkj is None else dkj + ck_
                    dvj = cv_ if dvj is None else dvj + cv_
                dk_cols.append(dkj)
                dv_cols.append(dvj)
            dkf = jnp.concatenate(dk_cols, axis=1)
            dvf = jnp.concatenate(dv_cols, axis=1)
            dq_ref[...] = jnp.concatenate(dq_cols, axis=1).astype(BF16)
            dsk_ref[...] += dsk
            flush(dkf[:CH], dvf[:CH])
            kcar[...] = dkf[CH:]
            vcar[...] = dvf[CH:]

        @pl.when(n == nb)
        def _():
            z = jnp.zeros((CH, KVW), F32)
            flush(z, z)

    last = nb - 1
    cur = lambda n: (jnp.minimum(n, last), 0)
    prev = lambda n: (jnp.clip(n - 1, 0, last), 0)
    sd = jax.ShapeDtypeStruct
    return pl.pallas_call(
        body, name="attn_bwd", grid=(nb + 1,),
        in_specs=[pl.BlockSpec(memory_space=pltpu.SMEM), pl.BlockSpec((CH, D), cur),
                  pl.BlockSpec((CH, KVW), prev), pl.BlockSpec((CH, KVW), cur),
                  pl.BlockSpec((CH, KVW), prev), pl.BlockSpec((CH, KVW), cur),
                  pl.BlockSpec((CH, D), cur),
                  pl.BlockSpec((CH, CH), cur), pl.BlockSpec((CH, CH), cur), pl.BlockSpec((CH, CH), cur),
                  pl.BlockSpec((CH, CH), prev), pl.BlockSpec((CH, CH), prev), pl.BlockSpec((CH, CH), prev)],
        out_specs=[pl.BlockSpec((CH, D), cur), pl.BlockSpec((CH, KVW), prev), pl.BlockSpec((CH, KVW), prev),
                   _const((1, CH))],
        out_shape=[sd((T, D), BF16), sd((T, KVW), BF16), sd((T, KVW), BF16), sd((1, CH), F32)],
        scratch_shapes=[pltpu.VMEM((CH, KVW), F32), pltpu.VMEM((CH, KVW), F32)],
        compiler_params=_cparams(("arbitrary",)),
    )(sinks, q, k, k, va, va, datt, rc, rs1, rs2, rc, rs1, rs2)


def _merge_fwd(a, att, ga, gb, x, w_a, w_b, w_o, g2, tm):
    T = x.shape[0]

    def body(a_ref, att_ref, ga_ref, gb_ref, x_ref, wa_ref, wb_ref, wo_ref, g_ref,
             pa_ref, pb_ref, mg_ref, mix_ref, x1_ref):
        pa = _dot(a_ref[...], wa_ref[...])
        pb = _dot(att_ref[...], wb_ref[...])
        pa_ref[...] = pa
        pb_ref[...] = pb
        merged = (_sigmoid(ga_ref[...]) * pa + _sigmoid(gb_ref[...]) * pb).astype(BF16)
        mg_ref[...] = merged
        mix = _dot(merged, wo_ref[...])
        mix_ref[...] = mix
        mhat, _ = _rms_hat(mix)
        x1_ref[...] = x_ref[...] + mhat * g_ref[...]

    sd = jax.ShapeDtypeStruct
    return pl.pallas_call(
        body, name="merge_fwd", grid=(T // tm,),
        in_specs=[_rows(tm, D)] * 5 + [_resident((D, D))] * 3 + [_const((1, D))],
        out_specs=[_rows(tm, D)] * 5,
        out_shape=[sd((T, D), F32), sd((T, D), F32), sd((T, D), BF16), sd((T, D), F32), sd((T, D), F32)],
        compiler_params=_cparams(("parallel",)),
    )(a, att, ga, gb, x, w_a, w_b, w_o, g2)


def _merge_bwd(dx1, mix, ga, gb, pa, pb, w_a, w_b, w_o, g2, tm):
    T = dx1.shape[0]

    def body(dx1_ref, mix_ref, ga_ref, gb_ref, pa_ref, pb_ref, wa_ref, wb_ref, wo_ref, g_ref,
             dmix_ref, dao_ref, dbo_ref, dga_ref, dgb_ref, da_ref, datt_ref, dg_ref):
        @pl.when(pl.program_id(0) == 0)
        def _():
            dg_ref[...] = jnp.zeros_like(dg_ref)

        mhat, r = _rms_hat(mix_ref[...])
        dmix, dg = _rms_bwd(mhat, r, g_ref[...], dx1_ref[...])
        dg_ref[...] += dg
        dmix = dmix.astype(BF16)
        dmix_ref[...] = dmix
        dmerged = _dot_nt(dmix, wo_ref[...])
        sa = _sigmoid(ga_ref[...])
        sb = _sigmoid(gb_ref[...])
        dao = (dmerged * sa).astype(BF16)
        dbo = (dmerged * sb).astype(BF16)
        dao_ref[...] = dao
        dbo_ref[...] = dbo
        dga_ref[...] = (dmerged * pa_ref[...] * (sa * (1.0 - sa))).astype(BF16)
        dgb_ref[...] = (dmerged * pb_ref[...] * (sb * (1.0 - sb))).astype(BF16)
        da_ref[...] = _dot_nt(dao, wa_ref[...])
        datt_ref[...] = _dot_nt(dbo, wb_ref[...])

    sd = jax.ShapeDtypeStruct
    return pl.pallas_call(
        body, name="merge_bwd", grid=(T // tm,),
        in_specs=[_rows(tm, D)] * 6 + [_resident((D, D))] * 3 + [_const((1, D))],
        out_specs=[_rows(tm, D)] * 7 + [_const((1, D))],
        out_shape=[sd((T, D), BF16)] * 5 + [sd((T, D), F32)] * 2 + [sd((1, D), F32)],
        compiler_params=_cparams(("arbitrary",)),
    )(dx1, mix, ga, gb, pa, pb, w_a, w_b, w_o, g2)


def _ffn(x1, target, w1, w2, g3, g4, tm):
    T = x1.shape[0]

    def body(x_ref, t_ref, w1_ref, w2_ref, g3_ref, g4_ref,
             hf_ref, f2_ref, dff_ref, df1_ref, dx_ref, ls_ref, dg3_ref, dg4_ref):
        @pl.when(pl.program_id(0) == 0)
        def _():
            ls_ref[...] = jnp.zeros_like(ls_ref)
            dg3_ref[...] = jnp.zeros_like(dg3_ref)
            dg4_ref[...] = jnp.zeros_like(dg4_ref)

        x = x_ref[...]
        g3, g4 = g3_ref[...], g4_ref[...]
        xhat, r3 = _rms_hat(x)
        hf = (xhat * g3).astype(BF16)
        hf_ref[...] = hf
        rl = jnp.maximum(_dot(hf, w1_ref[...]), 0.0)
        f2 = (rl * rl).astype(BF16)
        f2_ref[...] = f2
        fhat, r4 = _rms_hat(_dot(f2, w2_ref[...]))
        err = x + fhat * g4 - t_ref[...]
        ls_ref[...] += jnp.sum(err * err, axis=0, keepdims=True)
        dy = err * (1.0 / D)
        dff, dg4 = _rms_bwd(fhat, r4, g4, dy)
        dg4_ref[...] += dg4
        dff = dff.astype(BF16)
        dff_ref[...] = dff
        df1 = (_dot_nt(dff, w2_ref[...]) * (2.0 * rl)).astype(BF16)
        df1_ref[...] = df1
        dxn, dg3 = _rms_bwd(xhat, r3, g3, _dot_nt(df1, w1_ref[...]))
        dg3_ref[...] += dg3
        dx_ref[...] = dy + dxn

    sd = jax.ShapeDtypeStruct
    return pl.pallas_call(
        body, name="ffn_fwd_bwd", grid=(T // tm,),
        in_specs=[_rows(tm, D), _rows(tm, D), _resident((D, DFF)), _resident((DFF, D)), _const((1, D)), _const((1, D))],
        out_specs=[_rows(tm, D), _rows(tm, DFF), _rows(tm, D), _rows(tm, DFF), _rows(tm, D), _const((1, D)),
                   _const((1, D)), _const((1, D))],
        out_shape=[sd((T, D), BF16), sd((T, DFF), BF16), sd((T, D), BF16), sd((T, DFF), BF16), sd((T, D), F32),
                   sd((1, D), F32), sd((1, D), F32), sd((1, D), F32)],
        compiler_params=_cparams(("arbitrary",)),
    )(x1, target, w1, w2, g3, g4)


def _inproj_bwd(parts, x, dx1, g1, w_in, tm):
    T = x.shape[0]
    widths = [p.shape[1] for p in parts]
    offs = [sum(widths[:i]) for i in range(len(widths) + 1)]
    assert offs[-1] == IN_W

    def body(*refs):
        n = len(parts)
        prefs = refs[:n]
        x_ref, dx1_ref, g_ref, w_ref, dx_ref, dp_ref, dg_ref = refs[n:]

        @pl.when(pl.program_id(0) == 0)
        def _():
            dg_ref[...] = jnp.zeros_like(dg_ref)

        dh = None
        for i in range(n):
            blk = prefs[i][...]
            dp_ref[:, offs[i]:offs[i + 1]] = blk
            t = _dot_nt(blk, w_ref[:, offs[i]:offs[i + 1]])
            dh = t if dh is None else dh + t
        xhat, r = _rms_hat(x_ref[...])
        dxn, dg = _rms_bwd(xhat, r, g_ref[...], dh)
        dg_ref[...] += dg
        dx_ref[...] = dx1_ref[...] + dxn

    sd = jax.ShapeDtypeStruct
    return pl.pallas_call(
        body, name="inproj_bwd", grid=(T // tm,),
        in_specs=[_rows(tm, w) for w in widths] + [_rows(tm, D), _rows(tm, D), _const((1, D)), _resident((D, IN_W))],
        out_specs=[_rows(tm, D), _rows(tm, IN_W), _const((1, D))],
        out_shape=[sd((T, D), F32), sd((T, IN_W), BF16), sd((1, D), F32)],
        compiler_params=_cparams(("arbitrary",)),
    )(*parts, x, dx1, g1, w_in)


def _wgrad(a, g, tn, tm, name):
    T, K = a.shape
    N = g.shape[1]

    def body(a_ref, g_ref, o_ref):
        @pl.when(pl.program_id(1) == 0)
        def _():
            o_ref[...] = jnp.zeros_like(o_ref)

        o_ref[...] += _dot_tn(a_ref[...], g_ref[...])

    return pl.pallas_call(
        body, name=name, grid=(N // tn, T // tm),
        in_specs=[pl.BlockSpec((tm, K), lambda j, t: (t, 0)), pl.BlockSpec((tm, tn), lambda j, t: (t, j))],
        out_specs=pl.BlockSpec((K, tn), lambda j, t: (0, j)),
        out_shape=jax.ShapeDtypeStruct((K, N), F32),
        compiler_params=_cparams(("parallel", "arbitrary")),
    )(a, g)


def _adamw(w, g, m, v, tr, name):
    R, C = w.shape
    bc1 = 1.0 / (1.0 - B1 ** STEP)
    bc2 = 1.0 / (1.0 - B2 ** STEP)

    def body(w_ref, g_ref, m_ref, v_ref, d_ref, nm_ref, nv_ref):
        g = g_ref[...]
        m = B1 * m_ref[...] + (1.0 - B1) * g
        v = B2 * v_ref[...] + (1.0 - B2) * (g * g)
        nm_ref[...] = m
        nv_ref[...] = v
        d_ref[...] = -LR * ((m * bc1) / (jnp.sqrt(v * bc2) + AEPS) + WD * w_ref[...])

    spec = pl.BlockSpec((tr, C), lambda i: (i, 0))
    return pl.pallas_call(
        body, name=name, grid=(R // tr,), in_specs=[spec] * 4, out_specs=[spec] * 3,
        out_shape=[jax.ShapeDtypeStruct((R, C), F32)] * 3,
        compiler_params=_cparams(("parallel",)),
    )(w, g, m, v)


BIG = (("col", (D, IN_W)), ("row", (D, D)), ("row", (D, D)), ("row", (D, D)), ("col", (D, DFF)), ("row", (DFF, D)))
NBIG = len(BIG)
ANY = pl.BlockSpec(memory_space=pl.ANY)


def _shard_shape(kind, shape):
    R, C = shape
    return (R, C // 4) if kind == "col" else (R // 4, C)


def _half_shape(kind, shape):
    R, C = shape
    return (R // 2, C) if kind == "col" else (R, C // 2)


def _piece_shape(kind, shape):
    R, C = shape
    return (R // 2, C // 4) if kind == "col" else (R // 4, C // 2)


def _own_region(ref, kind, shape, s):
    R, C = shape
    return ref.at[:, pl.ds(s * (C // 4), C // 4)] if kind == "col" else ref.at[pl.ds(s * (R // 4), R // 4), :]


def _ag_region(ref, kind, shape, s, hc):
    R, C = shape
    if kind == "col":
        return ref.at[pl.ds(hc * (R // 2), R // 2), pl.ds(s * (C // 4), C // 4)]
    return ref.at[pl.ds(s * (R // 4) + hc * (R // 8), R // 8), :]


def _ag_shard_half(ref, kind, shape, hc):
    R, C = shape
    return ref.at[pl.ds(hc * (R // 2), R // 2), :] if kind == "col" else ref.at[pl.ds(hc * (R // 8), R // 8), :]


def _grad_half(ref, kind, shape, hc):
    R, C = shape
    return ref.at[pl.ds(hc * (R // 2), R // 2), :] if kind == "col" else ref.at[:, pl.ds(hc * (C // 2), C // 2)]


def _half_piece(ref, kind, shape, s):
    R, C = shape
    return ref.at[:, pl.ds(s * (C // 4), C // 4)] if kind == "col" else ref.at[pl.ds(s * (R // 4), R // 4), :]


def _place():
    x, y, c = lax.axis_index("x"), lax.axis_index("y"), lax.axis_index("c")
    chips = [(1 - x, y), (x, 1 - y), (1 - x, 1 - y)]
    return x, y, c, chips


def _rcopy(src, dst, ssem, rsem, dev):
    return pltpu.make_async_remote_copy(src_ref=src, dst_ref=dst, send_sem=ssem, recv_sem=rsem,
                                        device_id=dev, device_id_type=MESH)


def _all_gather(shards):
    def body(*refs):
        sh, full = refs[:NBIG], refs[NBIG:2 * NBIG]
        ssem, rsem, lsem, osem = refs[2 * NBIG:2 * NBIG + 4]
        bounce = refs[2 * NBIG + 4:]
        x, y, c, chips = _place()
        me_s, sib = 2 * x + y, (x, y, 1 - c)
        local, sends = [], []
        loads = [pltpu.make_async_copy(sh[w], bounce[w], lsem.at[w]) for w in range(NBIG)]
        for cp in loads:
            cp.start()
        for w, (kind, shape) in enumerate(BIG):
            for j, (cx, cy) in enumerate(chips):
                cp = _rcopy(_ag_shard_half(sh[w], kind, shape, c), _ag_region(full[w], kind, shape, me_s, c),
                            ssem.at[3 * w + j], rsem.at[3 * w + j], (cx, cy, c))
                cp.start()
                sends.append(cp)
        for w, (kind, shape) in enumerate(BIG):
            loads[w].wait()
            cp = pltpu.make_async_copy(bounce[w], _own_region(full[w], kind, shape, me_s), osem.at[w])
            cp.start()
            local.append(cp)
        for w, (kind, shape) in enumerate(BIG):
            for j, (cx, cy) in enumerate(chips):
                reg = _ag_region(full[w], kind, shape, 2 * cx + cy, c)
                _rcopy(reg, reg, ssem.at[3 * w + j], rsem.at[3 * w + j], sib).wait_recv()
                k = 3 * NBIG + 3 * w + j
                cp = _rcopy(reg, reg, ssem.at[k], rsem.at[k], sib)
                cp.start()
                sends.append(cp)
        for w, (kind, shape) in enumerate(BIG):
            for j, (cx, cy) in enumerate(chips):
                reg = _ag_region(full[w], kind, shape, 2 * cx + cy, 1 - c)
                k = 3 * NBIG + 3 * w + j
                _rcopy(reg, reg, ssem.at[k], rsem.at[k], sib).wait_recv()
        for cp in sends:
            cp.wait_send()
        for cp in local:
            cp.wait()

    return pl.pallas_call(
        body, name="weights_all_gather", in_specs=[ANY] * NBIG, out_specs=[ANY] * NBIG,
        out_shape=[jax.ShapeDtypeStruct(shape, BF16) for _, shape in BIG],
        scratch_shapes=[pltpu.SemaphoreType.DMA((6 * NBIG,)), pltpu.SemaphoreType.DMA((6 * NBIG,)),
                        pltpu.SemaphoreType.DMA((NBIG,)), pltpu.SemaphoreType.DMA((NBIG,))]
        + [pltpu.VMEM(_shard_shape(k, s), BF16) for k, s in BIG],
        compiler_params=pltpu.CompilerParams(vmem_limit_bytes=VMEM_LIMIT),
    )(*shards)


def _rs_sibling(grads):
    def body(*refs):
        g, got = refs[:NBIG], refs[NBIG:2 * NBIG]
        ssem, rsem = refs[2 * NBIG:]
        x, y, c, _ = _place()
        sib = (x, y, 1 - c)
        cps = []
        for w, (kind, shape) in enumerate(BIG):
            rc = _rcopy(_grad_half(g[w], kind, shape, 1 - c), got[w], ssem.at[w], rsem.at[w], sib)
            rc.start()
            cps.append(rc)
        for rc in cps:
            rc.wait_recv()
        for rc in cps:
            rc.wait_send()

    return pl.pallas_call(
        body, name="grads_to_sibling", in_specs=[ANY] * NBIG, out_specs=[ANY] * NBIG,
        out_shape=[jax.ShapeDtypeStruct(_half_shape(k, s), F32) for k, s in BIG],
        scratch_shapes=[pltpu.SemaphoreType.DMA((NBIG,)), pltpu.SemaphoreType.DMA((NBIG,))],
    )(*grads)


def _rs_chips(sums_bf):
    def body(*refs):
        s16, got = refs[:NBIG], refs[NBIG:2 * NBIG]
        ssem, rsem = refs[2 * NBIG:]
        x, y, c, chips = _place()
        sends = []
        for w, (kind, shape) in enumerate(BIG):
            for j, (cx, cy) in enumerate(chips):
                cp = _rcopy(_half_piece(s16[w], kind, shape, 2 * cx + cy), got[w].at[j],
                            ssem.at[3 * w + j], rsem.at[3 * w + j], (cx, cy, c))
                cp.start()
                sends.append(cp)
        for cp in sends:
            cp.wait_recv()
        for cp in sends:
            cp.wait_send()

    return pl.pallas_call(
        body, name="grads_to_chips", in_specs=[ANY] * NBIG, out_specs=[ANY] * NBIG,
        out_shape=[jax.ShapeDtypeStruct((3,) + _piece_shape(k, s), BF16) for k, s in BIG],
        scratch_shapes=[pltpu.SemaphoreType.DMA((3 * NBIG,)), pltpu.SemaphoreType.DMA((3 * NBIG,))],
    )(*sums_bf)


def _shard_half(ref, kind, shape, hc):
    sr, sc = _shard_shape(kind, shape)
    return ref.at[pl.ds(hc * (sr // 2), sr // 2), :] if kind == "col" else ref.at[:, pl.ds(hc * (sc // 2), sc // 2)]


def _rs_share(shard_grads):
    def body(*refs):
        g = refs[NBIG:2 * NBIG]
        ssem, rsem = refs[2 * NBIG:]
        x, y, c, _ = _place()
        sib = (x, y, 1 - c)
        cps = []
        for w, (kind, shape) in enumerate(BIG):
            part = _shard_half(g[w], kind, shape, c)
            rc = _rcopy(part, part, ssem.at[w], rsem.at[w], sib)
            rc.start()
            cps.append(rc)
        for w, (kind, shape) in enumerate(BIG):
            part = _shard_half(g[w], kind, shape, 1 - c)
            _rcopy(part, part, ssem.at[w], rsem.at[w], sib).wait_recv()
        for rc in cps:
            rc.wait_send()

    return pl.pallas_call(
        body, name="grads_share", in_specs=[ANY] * NBIG, out_specs=[ANY] * NBIG,
        out_shape=[jax.ShapeDtypeStruct(_shard_shape(k, s), F32) for k, s in BIG],
        input_output_aliases={w: w for w in range(NBIG)},
        scratch_shapes=[pltpu.SemaphoreType.DMA((NBIG,)), pltpu.SemaphoreType.DMA((NBIG,))],
    )(*shard_grads)


ADD_ROWS = 256


def _add_halves(place, g, got, kind, name):
    R, C = g.shape
    hr, hcols = _half_shape(kind, (R, C))
    steps = hr // ADD_ROWS

    def body(p_ref, g_ref, b_ref, s_ref, sb_ref):
        s = g_ref[...] + b_ref[...]
        s_ref[...] = s
        sb_ref[...] = s.astype(BF16)

    if kind == "col":
        g_spec = pl.BlockSpec((ADD_ROWS, C), lambda i, p: (p[0] * steps + i, 0))
    else:
        g_spec = pl.BlockSpec((ADD_ROWS, hcols), lambda i, p: (i, p[0]))
    spec = pl.BlockSpec((ADD_ROWS, hcols), lambda i, p: (i, 0))
    return pl.pallas_call(
        body, name=name,
        grid_spec=pltpu.PrefetchScalarGridSpec(num_scalar_prefetch=1, grid=(steps,), in_specs=[g_spec, spec],
                                               out_specs=[spec, spec]),
        out_shape=[jax.ShapeDtypeStruct((hr, hcols), F32), jax.ShapeDtypeStruct((hr, hcols), BF16)],
        compiler_params=_cparams(("parallel",)),
    )(place, g, got)


def _add_pieces(place, half, got, kind, shape, name):
    pr, pc = _piece_shape(kind, shape)
    steps = pr // ADD_ROWS

    def body(p_ref, m_ref, g_ref, o_ref):
        acc = m_ref[...]
        for j in range(3):
            acc = acc + g_ref[j].astype(F32)
        o_ref[...] = acc

    if kind == "col":
        m_spec = pl.BlockSpec((ADD_ROWS, pc), lambda i, p: (i, p[1]))
        o_spec = pl.BlockSpec((ADD_ROWS, pc), lambda i, p: (p[0] * steps + i, 0))
    else:
        m_spec = pl.BlockSpec((ADD_ROWS, pc), lambda i, p: (p[1] * steps + i, 0))
        o_spec = pl.BlockSpec((ADD_ROWS, pc), lambda i, p: (i, p[0]))
    return pl.pallas_call(
        body, name=name,
        grid_spec=pltpu.PrefetchScalarGridSpec(
            num_scalar_prefetch=1, grid=(steps,),
            in_specs=[m_spec, pl.BlockSpec((3, ADD_ROWS, pc), lambda i, p: (0, i, 0))], out_specs=o_spec),
        out_shape=jax.ShapeDtypeStruct(_shard_shape(kind, shape), F32),
        compiler_params=_cparams(("parallel",)),
    )(place, half, got)


SMALL_ROWS = 1024 + 8 * 8


def _small_all_reduce(p):
    def body(p_ref, o_ref, slots, ssem, rsem):
        x, y, c = lax.axis_index("x"), lax.axis_index("y"), lax.axis_index("c")
        me = 4 * x + 2 * y + c
        slots[me] = p_ref[...]
        cps = []
        for r in range(1, 8):
            bx, by, bc = (r >> 2) & 1, (r >> 1) & 1, r & 1
            tgt = (1 - x if bx else x, 1 - y if by else y, 1 - c if bc else c)
            cp = _rcopy(p_ref, slots.at[me], ssem.at[r - 1], rsem.at[r - 1], tgt)
            cp.start()
            cps.append((cp, 4 * tgt[0] + 2 * tgt[1] + tgt[2]))
        for r, (cp, src) in enumerate(cps):
            _rcopy(p_ref, slots.at[src], ssem.at[r], rsem.at[r], (x, y, c)).wait_recv()
        acc = slots[0]
        for d in range(1, 8):
            acc = acc + slots[d]
        o_ref[...] = acc
        for cp, _ in cps:
            cp.wait_send()

    vm = pl.BlockSpec(memory_space=pltpu.VMEM)
    return pl.pallas_call(
        body, name="small_all_reduce", in_specs=[vm], out_specs=vm,
        out_shape=jax.ShapeDtypeStruct((SMALL_ROWS, CH), F32),
        scratch_shapes=[pltpu.VMEM((8, SMALL_ROWS, CH), F32), pltpu.SemaphoreType.DMA((7,)), pltpu.SemaphoreType.DMA((7,))],
    )(p)


def _rope_tables(positions):
    inv_freq = 500000.0 ** (-jnp.arange(0, 2 * ROPE_HALF, 2, dtype=F32) / (2 * ROPE_HALF))
    head = jnp.concatenate([inv_freq, inv_freq, jnp.zeros((HD - 2 * ROPE_HALF,), F32)])
    lane_freq = jnp.concatenate([head, head])
    ang = positions.astype(F32)[:, None] * lane_freq[None, :]
    cos, sin = jnp.cos(ang), jnp.sin(ang)
    first = (jnp.arange(CH) % HD) < ROPE_HALF
    return cos, jnp.where(first[None, :], -sin, 0.0), jnp.where(first[None, :], 0.0, sin)


def _local_step(x, positions, w_in, w_a, w_b, w_o, w_ff_in, w_ff_out, ln_g, ln_b, w_sp, b_sp, sinks,
                g1, g2, g3, g4, target):
    rc, rs1, rs2 = _rope_tables(positions)
    bfull = jnp.repeat(b_sp.T, CH, axis=1)
    h, u, vs, q, k, va, ga, gb = _inproj(x, g1, w_in, rc, rs1, rs2, tm=256)
    a = _sgu_fwd(u, vs, ln_g, ln_b, w_sp, bfull, tm=512)
    att = _attn_fwd(q, k, va, sinks)
    pa, pb, merged, mix, x1 = _merge_fwd(a, att, ga, gb, x, w_a, w_b, w_o, g2, tm=256)
    hf, f2, dff, df1, dx1, lsum, dg3, dg4 = _ffn(x1, target, w_ff_in, w_ff_out, g3, g4, tm=256)
    wtm = min(1024, x.shape[0])
    dw_ff_out = _wgrad(f2, dff, tn=512, tm=512, name="wgrad_ff_out")
    dw_ff_in = _wgrad(hf, df1, tn=1024, tm=wtm, name="wgrad_ff_in")
    dmix, dao, dbo, dga, dgb, da, datt, dg2 = _merge_bwd(dx1, mix, ga, gb, pa, pb, w_a, w_b, w_o, g2, tm=256)
    dw_o = _wgrad(merged, dmix, tn=1024, tm=wtm, name="wgrad_o")
    dw_a = _wgrad(a, dao, tn=1024, tm=wtm, name="wgrad_a")
    dw_b = _wgrad(att, dbo, tn=1024, tm=wtm, name="wgrad_b")
    du, dvs, dws, dbs, dlg, dlb = _sgu_bwd(u, vs, da, ln_g, ln_b, w_sp, bfull, tm=512)
    dq, dk, dva, dsk = _attn_bwd(q, k, va, datt, sinks, rc, rs1, rs2)
    dx, dproj, dg1 = _inproj_bwd([du, dvs, dq, dk, dva, dga, dgb], x, dx1, g1, w_in, tm=256)
    dw_in = _wgrad(h, dproj, tn=IN_W // 4, tm=wtm, name="wgrad_in")
    big = (dw_in, dw_a, dw_b, dw_o, dw_ff_in, dw_ff_out)
    small = dict(ln_v_gain=dlg, ln_v_bias=dlb, w_spatial=dws, b_spatial=dbs, sinks=dsk[:, :NQ],
                 norm_mix_pre=dg1, norm_mix_post=dg2, norm_ff_pre=dg3, norm_ff_post=dg4)
    return lsum, dx, big, small


BIG_NAMES = ("w_in", "w_a", "w_b", "w_o", "w_ff_in", "w_ff_out")
SMALL_NAMES = ("w_spatial", "ln_v_gain", "ln_v_bias", "b_spatial", "sinks", "norm_mix_pre", "norm_mix_post",
               "norm_ff_pre", "norm_ff_post")
WEIGHT_ORDER = ("w_in", "ln_v_gain", "ln_v_bias", "w_spatial", "b_spatial", "sinks", "w_a", "w_b", "w_o",
                "norm_mix_pre", "norm_mix_post", "w_ff_in", "w_ff_out", "norm_ff_pre", "norm_ff_post")


def _pack_small(d):
    parts = []
    for n in SMALL_NAMES:
        flat = d[n].reshape(-1)
        pad = (-flat.shape[0]) % (8 * CH)
        parts.append(jnp.pad(flat, (0, pad)).reshape(-1, CH))
    return jnp.concatenate(parts, axis=0)


def _unpack_small(p, like):
    out, row = {}, 0
    for n in SMALL_NAMES:
        size = like[n].size
        rows = -(-size // (8 * CH)) * 8
        out[n] = p[row:row + rows].reshape(-1)[:size].reshape(like[n].shape)
        row += rows
    return out


def kernel(x, positions, w_in, ln_v_gain, ln_v_bias, w_spatial, b_spatial, sinks, w_a, w_b, w_o, norm_mix_pre, norm_mix_post, w_ff_in, w_ff_out, norm_ff_pre, norm_ff_post, loss_target, m_w_in, m_ln_v_gain, m_ln_v_bias, m_w_spatial, m_b_spatial, m_sinks, m_w_a, m_w_b, m_w_o, m_norm_mix_pre, m_norm_mix_post, m_w_ff_in, m_w_ff_out, m_norm_ff_pre, m_norm_ff_post, v_w_in, v_ln_v_gain, v_ln_v_bias, v_w_spatial, v_b_spatial, v_sinks, v_w_a, v_w_b, v_w_o, v_norm_mix_pre, v_norm_mix_post, v_w_ff_in, v_w_ff_out, v_norm_ff_pre, v_norm_ff_post):
    w = dict(w_in=w_in, ln_v_gain=ln_v_gain, ln_v_bias=ln_v_bias, w_spatial=w_spatial, b_spatial=b_spatial, sinks=sinks,
             w_a=w_a, w_b=w_b, w_o=w_o, norm_mix_pre=norm_mix_pre, norm_mix_post=norm_mix_post, w_ff_in=w_ff_in,
             w_ff_out=w_ff_out, norm_ff_pre=norm_ff_pre, norm_ff_post=norm_ff_post)
    m = dict(w_in=m_w_in, ln_v_gain=m_ln_v_gain, ln_v_bias=m_ln_v_bias, w_spatial=m_w_spatial, b_spatial=m_b_spatial,
             sinks=m_sinks, w_a=m_w_a, w_b=m_w_b, w_o=m_w_o, norm_mix_pre=m_norm_mix_pre, norm_mix_post=m_norm_mix_post,
             w_ff_in=m_w_ff_in, w_ff_out=m_w_ff_out, norm_ff_pre=m_norm_ff_pre, norm_ff_post=m_norm_ff_post)
    v = dict(w_in=v_w_in, ln_v_gain=v_ln_v_gain, ln_v_bias=v_ln_v_bias, w_spatial=v_w_spatial, b_spatial=v_b_spatial,
             sinks=v_sinks, w_a=v_w_a, w_b=v_w_b, w_o=v_w_o, norm_mix_pre=v_norm_mix_pre, norm_mix_post=v_norm_mix_post,
             w_ff_in=v_w_ff_in, w_ff_out=v_w_ff_out, norm_ff_pre=v_norm_ff_pre, norm_ff_post=v_norm_ff_post)

    whole = _all_gather([w[n][0].astype(BF16) for n in BIG_NAMES])
    lsum, dx, big, small = _local_step(
        x[0], positions[0], *whole, ln_v_gain, ln_v_bias, w_spatial[0], b_spatial[0], sinks[0],
        norm_mix_pre, norm_mix_post, norm_ff_pre, norm_ff_post, loss_target[0])
    loss = lax.psum(0.5 * jnp.sum(lsum) / D, ("x", "y", "c"))

    place = jnp.stack([lax.axis_index("c"), 2 * lax.axis_index("x") + lax.axis_index("y")]).astype(jnp.int32)
    got = _rs_sibling(big)
    sums = [_add_halves(place, big[i], got[i], BIG[i][0], name="grad_add_sibling_" + BIG_NAMES[i]) for i in range(NBIG)]
    pieces = _rs_chips([s[1] for s in sums])
    partial = [_add_pieces(place, sums[i][0], pieces[i], *BIG[i], name="grad_add_chips_" + BIG_NAMES[i])
               for i in range(NBIG)]
    shard_grads = _rs_share(partial)

    grad, delta, new_m, new_v = {}, {}, {}, {}
    for i, n in enumerate(BIG_NAMES):
        g = shard_grads[i]
        d_, m_, v_ = _adamw(w[n][0], g, m[n][0], v[n][0], tr=256, name="adamw_" + n)
        grad[n], delta[n], new_m[n], new_v[n] = g[None], d_[None], m_[None], v_[None]

    gs = _small_all_reduce(_pack_small(small))
    ds, ms, vs = _adamw(_pack_small(w), gs, _pack_small(m), _pack_small(v), tr=SMALL_ROWS // 4, name="adamw_small")
    for packed, dst in ((gs, grad), (ds, delta), (ms, new_m), (vs, new_v)):
        dst.update(_unpack_small(packed, w))

    outs = [loss, dx[None]]
    for group in (grad, delta, new_m, new_v):
        outs.extend(group[n] for n in WEIGHT_ORDER)
    return tuple(outs)
```

```python
import functools

import jax
import jax.numpy as jnp
from jax import lax
from jax.experimental import pallas as pl
from jax.experimental.pallas import tpu as pltpu

F32 = jnp.float32
BF16 = jnp.bfloat16

D = 1024
CH = 128
NG = 8
HD = 64
NQ = 16
NKV = 4
KVW = NKV * HD
DFF = 4 * D
EPS = 1e-6
IN_W = 5632
SEG = (0, 1024, 2048, 3072, 3328, 3584, 4608, 5632)
ROPE_HALF = 8
Q_SCALE = HD ** -0.5

LR, B1, B2, AEPS, WD, STEP = 0.001, 0.9, 0.999, 1e-08, 0.01, 10

VMEM_LIMIT = 56 * 1024 * 1024
MESH = pl.DeviceIdType.MESH

_GELU_C0 = 0.7978845608028654
_GELU_C1 = 0.044715


def _cparams(sem=None):
    kw = dict(vmem_limit_bytes=VMEM_LIMIT)
    if sem is not None:
        kw["dimension_semantics"] = sem
    return pltpu.CompilerParams(**kw)


def _resident(shape):
    nd = len(shape)
    return pl.BlockSpec(shape, lambda *_: (0,) * nd, pipeline_mode=pl.Buffered(1))


def _const(shape):
    nd = len(shape)
    return pl.BlockSpec(shape, lambda *_: (0,) * nd)


def _rows(tm, w):
    return pl.BlockSpec((tm, w), lambda i: (i, 0))


class _Exchange:
    def __init__(self, ins, outs, aliases, scratch, start, finish):
        self.ins, self.outs, self.aliases, self.scratch = list(ins), list(outs), dict(aliases), list(scratch)
        self.start, self.finish = start, finish


def _call(body, args, *, name, grid, in_specs, out_specs, out_shape, scratch_shapes=(), sem=None, comm=None):
    single = not isinstance(out_shape, (list, tuple))
    out_shape = [out_shape] if single else list(out_shape)
    out_specs = [out_specs] if single else list(out_specs)
    if comm is None:
        res = pl.pallas_call(body, name=name, grid=grid, in_specs=list(in_specs), out_specs=out_specs,
                             out_shape=out_shape, scratch_shapes=list(scratch_shapes),
                             compiler_params=_cparams(sem))(*args)
        return (res[0] if single else res), []
    n_in, n_out, n_scr = len(args), len(out_shape), len(scratch_shapes)
    nci, nco = len(comm.ins), len(comm.outs)
    steps = 1
    for g in grid:
        steps *= g

    def hosted(*refs):
        a, ci = refs[:n_in], refs[n_in:n_in + nci]
        o, co = refs[n_in + nci:n_in + nci + n_out], refs[n_in + nci + n_out:n_in + nci + n_out + nco]
        rest = refs[n_in + nci + n_out + nco:]
        scr, cs = rest[:n_scr], rest[n_scr:]
        step = pl.program_id(0)
        for d in range(1, len(grid)):
            step = step * grid[d] + pl.program_id(d)

        @pl.when(step == 0)
        def _():
            comm.start(ci, co, cs)

        body(*a, *o, *scr)

        @pl.when(step == steps - 1)
        def _():
            comm.finish(ci, co, cs)

    res = pl.pallas_call(
        hosted, name=name, grid=grid, in_specs=list(in_specs) + [ANY] * nci, out_specs=out_specs + [ANY] * nco,
        out_shape=out_shape + comm.outs, scratch_shapes=list(scratch_shapes) + comm.scratch,
        input_output_aliases={n_in + i: n_out + j for i, j in comm.aliases.items()},
        compiler_params=_cparams(("arbitrary",) * len(grid)),
    )(*args, *comm.ins)
    own = res[:n_out]
    return (own[0] if single else own), list(res[n_out:])


def _run(comm, name):
    nci = len(comm.ins)

    def body(*refs):
        ci, co, cs = refs[:nci], refs[nci:nci + len(comm.outs)], refs[nci + len(comm.outs):]
        comm.start(ci, co, cs)
        comm.finish(ci, co, cs)

    return pl.pallas_call(
        body, name=name, in_specs=[ANY] * nci, out_specs=[ANY] * len(comm.outs), out_shape=comm.outs,
        scratch_shapes=comm.scratch, input_output_aliases=comm.aliases,
        compiler_params=pltpu.CompilerParams(vmem_limit_bytes=VMEM_LIMIT),
    )(*comm.ins)


def _gelu(x):
    t = jnp.tanh(_GELU_C0 * (x + _GELU_C1 * (x * x * x)))
    return 0.5 * x * (1.0 + t), t


def _gelu_grad(x, t):
    return 0.5 * (1.0 + t) + 0.5 * x * (1.0 - t * t) * (_GELU_C0 * (1.0 + 3.0 * _GELU_C1 * x * x))


def _sigmoid(x):
    return 1.0 / (1.0 + jnp.exp(-x))


def _rms_hat(x):
    r = lax.rsqrt(jnp.mean(x * x, axis=-1, keepdims=True) + EPS)
    return x * r, r


def _rms_bwd(xhat, r, g, dout):
    dg = jnp.sum(dout * xhat, axis=0, keepdims=True)
    dy = dout * g
    dx = r * (dy - xhat * jnp.mean(dy * xhat, axis=-1, keepdims=True))
    return dx, dg


def _dot(a, b):
    return jnp.dot(a, b, preferred_element_type=F32)


def _dot_nt(a, b):
    return lax.dot_general(a, b, (((1,), (1,)), ((), ())), preferred_element_type=F32)


def _dot_tn(a, b):
    return lax.dot_general(a, b, (((0,), (0,)), ((), ())), preferred_element_type=F32)


def _rope(blk, c, s1, s2):
    return blk * c + pltpu.roll(blk, CH - ROPE_HALF, 1) * s1 + pltpu.roll(blk, ROPE_HALF, 1) * s2


def _rope_t(blk, c, s1, s2):
    return blk * c + pltpu.roll(blk * s1, ROPE_HALF, 1) + pltpu.roll(blk * s2, CH - ROPE_HALF, 1)


def _inproj(x, g1, w_in, rc, rs1, rs2, tm, comm=None):
    T = x.shape[0]

    def body(x_ref, g_ref, w_ref, c_ref, s1_ref, s2_ref,
             h_ref, u_ref, v_ref, q_ref, k_ref, va_ref, ga_ref, gb_ref):
        xhat, _ = _rms_hat(x_ref[...])
        h = (xhat * g_ref[...]).astype(BF16)
        h_ref[...] = h
        u_ref[...] = _dot(h, w_ref[:, SEG[0]:SEG[1]])
        v_ref[...] = _dot(h, w_ref[:, SEG[1]:SEG[2]])
        c, s1, s2 = c_ref[...], s1_ref[...], s2_ref[...]
        q = _dot(h, w_ref[:, SEG[2]:SEG[3]])
        for p in range(D // CH):
            blk = _rope(q[:, CH * p:CH * (p + 1)], c, s1, s2) * Q_SCALE
            q_ref[:, CH * p:CH * (p + 1)] = blk.astype(BF16)
        k = _dot(h, w_ref[:, SEG[3]:SEG[4]])
        for p in range(KVW // CH):
            k_ref[:, CH * p:CH * (p + 1)] = _rope(k[:, CH * p:CH * (p + 1)], c, s1, s2).astype(BF16)
        va_ref[...] = _dot(h, w_ref[:, SEG[4]:SEG[5]]).astype(BF16)
        ga_ref[...] = _dot(h, w_ref[:, SEG[5]:SEG[6]])
        gb_ref[...] = _dot(h, w_ref[:, SEG[6]:SEG[7]])

    sd = jax.ShapeDtypeStruct
    return _call(
        body, (x, g1, w_in, rc, rs1, rs2), name="inproj_fwd", grid=(T // tm,),
        in_specs=[_rows(tm, D), _const((1, D)), _resident((D, IN_W)), _rows(tm, CH), _rows(tm, CH), _rows(tm, CH)],
        out_specs=[_rows(tm, D), _rows(tm, D), _rows(tm, D), _rows(tm, D), _rows(tm, KVW), _rows(tm, KVW),
                   _rows(tm, D), _rows(tm, D)],
        out_shape=[sd((T, D), BF16), sd((T, D), F32), sd((T, D), F32), sd((T, D), BF16), sd((T, KVW), BF16),
                   sd((T, KVW), BF16), sd((T, D), F32), sd((T, D), F32)],
        sem=("parallel",), comm=comm)


def _sgu_common(u, vs, lng, lnb, ws_ref, bfull):
    nc = u.shape[0] // CH
    ug, tu = _gelu(u)
    vg, tv = _gelu(vs)
    mu = jnp.mean(vg, axis=-1, keepdims=True)
    xc = vg - mu
    rstd = lax.rsqrt(jnp.mean(xc * xc, axis=-1, keepdims=True) + EPS)
    vhat = xc * rstd
    vnb = (vhat * lng + lnb).astype(BF16)
    tri = lax.broadcasted_iota(jnp.int32, (CH, CH), 0) >= lax.broadcasted_iota(jnp.int32, (CH, CH), 1)
    wts, rhss, mixed = [], [], []
    for g in range(NG):
        wt = jnp.where(tri, ws_ref[g], 0.0).astype(BF16)
        rhs = jnp.concatenate([vnb[CH * c:CH * (c + 1), CH * g:CH * (g + 1)] for c in range(nc)], axis=1)
        mix = _dot(wt, rhs)
        wts.append(wt)
        rhss.append(rhs)
        mixed.append([mix[:, CH * c:CH * (c + 1)] + bfull[:, CH * g:CH * (g + 1)] for c in range(nc)])
    return nc, ug, tu, tv, rstd, vhat, tri, wts, rhss, mixed


def _sgu_fwd(u, vs, lng, lnb, ws, bfull, tm, comm=None):
    T = u.shape[0]

    def body(u_ref, v_ref, lng_ref, lnb_ref, ws_ref, bf_ref, a_ref):
        nc, ug, _, _, _, _, _, _, _, mixed = _sgu_common(
            u_ref[...], v_ref[...], lng_ref[...], lnb_ref[...], ws_ref, bf_ref[...])
        mixed_all = jnp.concatenate(
            [jnp.concatenate([mixed[g][c] for g in range(NG)], axis=1) for c in range(nc)], axis=0)
        a_ref[...] = (ug * mixed_all).astype(BF16)

    return _call(
        body, (u, vs, lng, lnb, ws, bfull), name="sgu_fwd", grid=(T // tm,),
        in_specs=[_rows(tm, D), _rows(tm, D), _const((1, D)), _const((1, D)), _const((NG, CH, CH)), _const((CH, D))],
        out_specs=_rows(tm, D), out_shape=jax.ShapeDtypeStruct((T, D), BF16), sem=("parallel",), comm=comm)


def _sgu_bwd(u, vs, da, lng, lnb, ws, bfull, tm, comm=None):
    T = u.shape[0]
    nsteps = T // tm

    def body(u_ref, v_ref, da_ref, lng_ref, lnb_ref, ws_ref, bf_ref,
             du_ref, dv_ref, dws_ref, dbs_ref, dlg_ref, dlb_ref, db_ref):
        i = pl.program_id(0)
        u, vs, da, lng = u_ref[...], v_ref[...], da_ref[...], lng_ref[...]
        nc, ug, tu, tv, rstd, vhat, tri, wts, rhss, mixed = _sgu_common(u, vs, lng, lnb_ref[...], ws_ref, bf_ref[...])

        @pl.when(i == 0)
        def _():
            dws_ref[...] = jnp.zeros_like(dws_ref)
            db_ref[...] = jnp.zeros_like(db_ref)
            dlg_ref[...] = jnp.zeros_like(dlg_ref)
            dlb_ref[...] = jnp.zeros_like(dlb_ref)

        mixed_all = jnp.concatenate(
            [jnp.concatenate([mixed[g][c] for g in range(NG)], axis=1) for c in range(nc)], axis=0)
        du_ref[...] = (da * mixed_all * _gelu_grad(u, tu)).astype(BF16)
        dmixed = da * ug
        dvn_cols = []
        for g in range(NG):
            dmix = [dmixed[CH * c:CH * (c + 1), CH * g:CH * (g + 1)] for c in range(nc)]
            db_ref[:, CH * g:CH * (g + 1)] += functools.reduce(lambda a, b: a + b, dmix)
            dm = jnp.concatenate(dmix, axis=1).astype(BF16)
            dws_ref[g] += _dot_nt(dm, rhss[g])
            dvn_cols.append(_dot_tn(wts[g], dm))
        dvn = jnp.concatenate(
            [jnp.concatenate([dvn_cols[g][:, CH * c:CH * (c + 1)] for g in range(NG)], axis=1) for c in range(nc)],
            axis=0)
        dlg_ref[...] += jnp.sum(dvn * vhat, axis=0, keepdims=True)
        dlb_ref[...] += jnp.sum(dvn, axis=0, keepdims=True)
        dvh = dvn * lng
        dvg = rstd * (dvh - jnp.mean(dvh, axis=-1, keepdims=True)
                      - vhat * jnp.mean(dvh * vhat, axis=-1, keepdims=True))
        dv_ref[...] = (dvg * _gelu_grad(vs, tv)).astype(BF16)

        @pl.when(i == nsteps - 1)
        def _():
            for g in range(NG):
                dws_ref[g] = jnp.where(tri, dws_ref[g], 0.0)
                dbs_ref[g:g + 1, :] = jnp.sum(db_ref[:, CH * g:CH * (g + 1)].T, axis=0, keepdims=True)

    sd = jax.ShapeDtypeStruct
    return _call(
        body, (u, vs, da, lng, lnb, ws, bfull), name="sgu_bwd", grid=(nsteps,),
        in_specs=[_rows(tm, D), _rows(tm, D), _rows(tm, D), _const((1, D)), _const((1, D)), _const((NG, CH, CH)),
                  _const((CH, D))],
        out_specs=[_rows(tm, D), _rows(tm, D), _const((NG, CH, CH)), _const((NG, CH)), _const((1, D)), _const((1, D))],
        out_shape=[sd((T, D), BF16), sd((T, D), BF16), sd((NG, CH, CH), F32), sd((NG, CH), F32), sd((1, D), F32),
                   sd((1, D), F32)],
        scratch_shapes=[pltpu.VMEM((CH, D), F32)], sem=("arbitrary",), comm=comm)


def _pair_layout(prev, cur, grp):
    j, half = grp // 2, grp % 2
    blk = jnp.concatenate([prev[:, CH * j:CH * (j + 1)], cur[:, CH * j:CH * (j + 1)]], axis=0).astype(F32)
    lo = lax.broadcasted_iota(jnp.int32, blk.shape, 1) < HD
    rolled = pltpu.roll(blk, HD, 1)
    even = jnp.where(lo, blk if half == 0 else rolled, 0.0)
    odd = jnp.where(lo, 0.0, rolled if half == 0 else blk)
    return jnp.concatenate([even, odd], axis=0).astype(BF16)


def _unpair(acc, grp):
    half = grp % 2
    lo = lax.broadcasted_iota(jnp.int32, (2 * CH, CH), 1) < HD
    ev, od = acc[:2 * CH], acc[2 * CH:]
    if half == 0:
        return jnp.where(lo, ev + pltpu.roll(od, HD, 1), 0.0)
    return jnp.where(lo, 0.0, pltpu.roll(ev, HD, 1) + od)


def _attn_mask(n):
    qi = lax.broadcasted_iota(jnp.int32, (CH, 2 * CH), 0)
    kc = lax.broadcasted_iota(jnp.int32, (CH, 2 * CH), 1)
    ok = (kc > qi) & (kc <= qi + CH) & ((kc >= CH) | (n > 0))
    return jnp.concatenate([ok, ok], axis=1)


def _softmax_sink(s, sink):
    m = jnp.maximum(jnp.max(s, axis=-1, keepdims=True), sink)
    p = jnp.exp(s - m)
    ps = jnp.exp(sink - m)
    inv = 1.0 / (jnp.sum(p, axis=-1, keepdims=True) + ps)
    return p * inv, ps * inv


def _attn_fwd(q, k, va, sinks):
    T = q.shape[0]
    nb = T // CH

    def body(sk_ref, q_ref, kp_ref, kc_ref, vp_ref, vc_ref, o_ref):
        n = pl.program_id(0)
        mask = _attn_mask(n)
        kp, kc, vp, vc = kp_ref[...], kc_ref[...], vp_ref[...], vc_ref[...]
        kks = [_pair_layout(kp, kc, grp) for grp in range(NKV)]
        vvs = [_pair_layout(vp, vc, grp) for grp in range(NKV)]
        npairs = D // CH

        def scores(p):
            return _dot_nt(q_ref[:, CH * p:CH * (p + 1)], kks[p // 2])

        ahead = 3
        outs, probs = [], []
        pending = [scores(p) for p in range(ahead)]
        for p in range(npairs):
            s = jnp.where(mask, pending.pop(0), -1e30)
            if p + ahead < npairs:
                pending.append(scores(p + ahead))
            pe, _ = _softmax_sink(s[:, :2 * CH], sk_ref[2 * p])
            po, _ = _softmax_sink(s[:, 2 * CH:], sk_ref[2 * p + 1])
            probs.append(jnp.concatenate([pe, po], axis=1).astype(BF16))
            if p >= 1:
                outs.append(_dot(probs[p - 1], vvs[(p - 1) // 2]))
        outs.append(_dot(probs[-1], vvs[-1]))
        o_ref[...] = jnp.concatenate(outs, axis=1).astype(BF16)

    prev = lambda n: (jnp.maximum(n - 1, 0), 0)
    cur = lambda n: (n, 0)
    return pl.pallas_call(
        body, name="attn_fwd", grid=(nb,),
        in_specs=[pl.BlockSpec(memory_space=pltpu.SMEM), pl.BlockSpec((CH, D), cur),
                  pl.BlockSpec((CH, KVW), prev), pl.BlockSpec((CH, KVW), cur),
                  pl.BlockSpec((CH, KVW), prev), pl.BlockSpec((CH, KVW), cur)],
        out_specs=pl.BlockSpec((CH, D), cur), out_shape=jax.ShapeDtypeStruct((T, D), BF16),
        compiler_params=_cparams(("parallel",)),
    )(sinks, q, k, k, va, va)


def _attn_bwd(q, k, va, datt, sinks, rc, rs1, rs2, comm=None):
    T = q.shape[0]
    nb = T // CH

    def body(sk_ref, q_ref, kp_ref, kc_ref, vp_ref, vc_ref, do_ref, cq_ref, s1q_ref, s2q_ref, ck_ref, s1k_ref, s2k_ref,
             dq_ref, dk_ref, dv_ref, dsk_ref, kcar, vcar):
        n = pl.program_id(0)

        @pl.when(n == 0)
        def _():
            kcar[...] = jnp.zeros_like(kcar)
            vcar[...] = jnp.zeros_like(vcar)
            dsk_ref[...] = jnp.zeros_like(dsk_ref)

        def flush(kprev, vprev):
            ck, s1k, s2k = ck_ref[...], s1k_ref[...], s2k_ref[...]
            for j in range(KVW // CH):
                sl = slice(CH * j, CH * (j + 1))
                dk_ref[:, sl] = _rope_t(kcar[:, sl] + kprev[:, sl], ck, s1k, s2k).astype(BF16)
                dv_ref[:, sl] = (vcar[:, sl] + vprev[:, sl]).astype(BF16)

        @pl.when(n < nb)
        def _():
            mask = _attn_mask(n)
            kp, kc, vp, vc = kp_ref[...], kc_ref[...], vp_ref[...], vc_ref[...]
            cq, s1q, s2q = cq_ref[...], s1q_ref[...], s2q_ref[...]
            lane = lax.broadcasted_iota(jnp.int32, (1, CH), 1)
            dsk = jnp.zeros((1, CH), F32)
            npairs = D // CH
            kks = [_pair_layout(kp, kc, grp) for grp in range(NKV)]
            vvs = [_pair_layout(vp, vc, grp) for grp in range(NKV)]
            qs = [q_ref[:, CH * p:CH * (p + 1)] for p in range(npairs)]
            dos = [do_ref[:, CH * p:CH * (p + 1)].astype(BF16) for p in range(npairs)]

            def first(p):
                return _dot_nt(qs[p], kks[p // 2]), _dot_nt(dos[p], vvs[p // 2])

            def last(p, ds, pb):
                return (_rope_t(_dot(ds, kks[p // 2]), cq, s1q, s2q) * Q_SCALE, _dot_tn(ds, qs[p]), _dot_tn(pb, dos[p]))

            ahead = 2
            pending = [first(p) for p in range(ahead)]
            mids, ends = [], []
            for p in range(npairs):
                s, dp = pending.pop(0)
                s = jnp.where(mask, s, -1e30)
                if p + ahead < npairs:
                    pending.append(first(p + ahead))
                ds_parts, p_parts = [], []
                for par in range(2):
                    sl = slice(2 * CH * par, 2 * CH * (par + 1))
                    pr, psink = _softmax_sink(s[:, sl], sk_ref[2 * p + par])
                    delta = jnp.sum(pr * dp[:, sl], axis=-1, keepdims=True)
                    ds_parts.append(pr * (dp[:, sl] - delta))
                    p_parts.append(pr)
                    tot = -jnp.sum(psink * delta, axis=0, keepdims=True)
                    dsk = dsk + jnp.where(lane == 2 * p + par, tot, 0.0)
                mids.append((jnp.concatenate(ds_parts, axis=1).astype(BF16), jnp.concatenate(p_parts, axis=1).astype(BF16)))
                if p >= 1:
                    ends.append(last(p - 1, *mids[p - 1]))
            ends.append(last(npairs - 1, *mids[-1]))
            dq_cols = [e[0] for e in ends]
            dk_cols, dv_cols = [], []
            for j in range(KVW // CH):
                dkj, dvj = None, None
                for grp in (2 * j, 2 * j + 1):
                    ck_ = _unpair(ends[2 * grp][1] + ends[2 * grp + 1][1], grp)
                    cv_ = _unpair(ends[2 * grp][2] + ends[2 * grp + 1][2], grp)
                    dkj = ck_ if dkj is None else dkj + ck_
                    dvj = cv_ if dvj is None else dvj + cv_
                dk_cols.append(dkj)
                dv_cols.append(dvj)
            dkf = jnp.concatenate(dk_cols, axis=1)
            dvf = jnp.concatenate(dv_cols, axis=1)
            dq_ref[...] = jnp.concatenate(dq_cols, axis=1).astype(BF16)
            dsk_ref[...] += dsk
            flush(dkf[:CH], dvf[:CH])
            kcar[...] = dkf[CH:]
            vcar[...] = dvf[CH:]

        @pl.when(n == nb)
        def _():
            z = jnp.zeros((CH, KVW), F32)
            flush(z, z)

    last = nb - 1
    cur = lambda n: (jnp.minimum(n, last), 0)
    prev = lambda n: (jnp.clip(n - 1, 0, last), 0)
    sd = jax.ShapeDtypeStruct
    return _call(
        body, (sinks, q, k, k, va, va, datt, rc, rs1, rs2, rc, rs1, rs2), name="attn_bwd", grid=(nb + 1,),
        in_specs=[pl.BlockSpec(memory_space=pltpu.SMEM), pl.BlockSpec((CH, D), cur),
                  pl.BlockSpec((CH, KVW), prev), pl.BlockSpec((CH, KVW), cur),
                  pl.BlockSpec((CH, KVW), prev), pl.BlockSpec((CH, KVW), cur),
                  pl.BlockSpec((CH, D), cur),
                  pl.BlockSpec((CH, CH), cur), pl.BlockSpec((CH, CH), cur), pl.BlockSpec((CH, CH), cur),
                  pl.BlockSpec((CH, CH), prev), pl.BlockSpec((CH, CH), prev), pl.BlockSpec((CH, CH), prev)],
        out_specs=[pl.BlockSpec((CH, D), cur), pl.BlockSpec((CH, KVW), prev), pl.BlockSpec((CH, KVW), prev),
                   _const((1, CH))],
        out_shape=[sd((T, D), BF16), sd((T, KVW), BF16), sd((T, KVW), BF16), sd((1, CH), F32)],
        scratch_shapes=[pltpu.VMEM((CH, KVW), F32), pltpu.VMEM((CH, KVW), F32)], sem=("arbitrary",), comm=comm)


def _merge_fwd(a, att, ga, gb, x, w_a, w_b, w_o, g2, tm):
    T = x.shape[0]

    def body(a_ref, att_ref, ga_ref, gb_ref, x_ref, wa_ref, wb_ref, wo_ref, g_ref,
             pa_ref, pb_ref, mg_ref, mix_ref, x1_ref):
        pa = _dot(a_ref[...], wa_ref[...])
        pb = _dot(att_ref[...], wb_ref[...])
        pa_ref[...] = pa
        pb_ref[...] = pb
        merged = (_sigmoid(ga_ref[...]) * pa + _sigmoid(gb_ref[...]) * pb).astype(BF16)
        mg_ref[...] = merged
        mix = _dot(merged, wo_ref[...])
        mix_ref[...] = mix
        mhat, _ = _rms_hat(mix)
        x1_ref[...] = x_ref[...] + mhat * g_ref[...]

    sd = jax.ShapeDtypeStruct
    return pl.pallas_call(
        body, name="merge_fwd", grid=(T // tm,),
        in_specs=[_rows(tm, D)] * 5 + [_resident((D, D))] * 3 + [_const((1, D))],
        out_specs=[_rows(tm, D)] * 5,
        out_shape=[sd((T, D), F32), sd((T, D), F32), sd((T, D), BF16), sd((T, D), F32), sd((T, D), F32)],
        compiler_params=_cparams(("parallel",)),
    )(a, att, ga, gb, x, w_a, w_b, w_o, g2)


def _merge_bwd(dx1, mix, ga, gb, pa, pb, w_a, w_b, w_o, g2, tm):
    T = dx1.shape[0]

    def body(dx1_ref, mix_ref, ga_ref, gb_ref, pa_ref, pb_ref, wa_ref, wb_ref, wo_ref, g_ref,
             dmix_ref, dao_ref, dbo_ref, dga_ref, dgb_ref, da_ref, datt_ref, dg_ref):
        @pl.when(pl.program_id(0) == 0)
        def _():
            dg_ref[...] = jnp.zeros_like(dg_ref)

        mhat, r = _rms_hat(mix_ref[...])
        dmix, dg = _rms_bwd(mhat, r, g_ref[...], dx1_ref[...])
        dg_ref[...] += dg
        dmix = dmix.astype(BF16)
        dmix_ref[...] = dmix
        dmerged = _dot_nt(dmix, wo_ref[...])
        sa = _sigmoid(ga_ref[...])
        sb = _sigmoid(gb_ref[...])
        dao = (dmerged * sa).astype(BF16)
        dbo = (dmerged * sb).astype(BF16)
        dao_ref[...] = dao
        dbo_ref[...] = dbo
        dga_ref[...] = (dmerged * pa_ref[...] * (sa * (1.0 - sa))).astype(BF16)
        dgb_ref[...] = (dmerged * pb_ref[...] * (sb * (1.0 - sb))).astype(BF16)
        da_ref[...] = _dot_nt(dao, wa_ref[...])
        datt_ref[...] = _dot_nt(dbo, wb_ref[...])

    sd = jax.ShapeDtypeStruct
    return pl.pallas_call(
        body, name="merge_bwd", grid=(T // tm,),
        in_specs=[_rows(tm, D)] * 6 + [_resident((D, D))] * 3 + [_const((1, D))],
        out_specs=[_rows(tm, D)] * 7 + [_const((1, D))],
        out_shape=[sd((T, D), BF16)] * 5 + [sd((T, D), F32)] * 2 + [sd((1, D), F32)],
        compiler_params=_cparams(("arbitrary",)),
    )(dx1, mix, ga, gb, pa, pb, w_a, w_b, w_o, g2)


def _ffn(x1, target, w1, w2, g3, g4, tm):
    T = x1.shape[0]

    def body(x_ref, t_ref, w1_ref, w2_ref, g3_ref, g4_ref,
             hf_ref, f2_ref, dff_ref, df1_ref, dx_ref, ls_ref, dg3_ref, dg4_ref):
        @pl.when(pl.program_id(0) == 0)
        def _():
            ls_ref[...] = jnp.zeros_like(ls_ref)
            dg3_ref[...] = jnp.zeros_like(dg3_ref)
            dg4_ref[...] = jnp.zeros_like(dg4_ref)

        x = x_ref[...]
        g3, g4 = g3_ref[...], g4_ref[...]
        xhat, r3 = _rms_hat(x)
        hf = (xhat * g3).astype(BF16)
        hf_ref[...] = hf
        rl = jnp.maximum(_dot(hf, w1_ref[...]), 0.0)
        f2 = (rl * rl).astype(BF16)
        f2_ref[...] = f2
        fhat, r4 = _rms_hat(_dot(f2, w2_ref[...]))
        err = x + fhat * g4 - t_ref[...]
        ls_ref[...] += jnp.sum(err * err, axis=0, keepdims=True)
        dy = err * (1.0 / D)
        dff, dg4 = _rms_bwd(fhat, r4, g4, dy)
        dg4_ref[...] += dg4
        dff = dff.astype(BF16)
        dff_ref[...] = dff
        df1 = (_dot_nt(dff, w2_ref[...]) * (2.0 * rl)).astype(BF16)
        df1_ref[...] = df1
        dxn, dg3 = _rms_bwd(xhat, r3, g3, _dot_nt(df1, w1_ref[...]))
        dg3_ref[...] += dg3
        dx_ref[...] = dy + dxn

    sd = jax.ShapeDtypeStruct
    return pl.pallas_call(
        body, name="ffn_fwd_bwd", grid=(T // tm,),
        in_specs=[_rows(tm, D), _rows(tm, D), _resident((D, DFF)), _resident((DFF, D)), _const((1, D)), _const((1, D))],
        out_specs=[_rows(tm, D), _rows(tm, DFF), _rows(tm, D), _rows(tm, DFF), _rows(tm, D), _const((1, D)),
                   _const((1, D)), _const((1, D))],
        out_shape=[sd((T, D), BF16), sd((T, DFF), BF16), sd((T, D), BF16), sd((T, DFF), BF16), sd((T, D), F32),
                   sd((1, D), F32), sd((1, D), F32), sd((1, D), F32)],
        compiler_params=_cparams(("arbitrary",)),
    )(x1, target, w1, w2, g3, g4)


def _inproj_bwd(parts, x, dx1, g1, w_in, tm, comm=None):
    T = x.shape[0]
    widths = [p.shape[1] for p in parts]
    offs = [sum(widths[:i]) for i in range(len(widths) + 1)]
    assert offs[-1] == IN_W

    def body(*refs):
        n = len(parts)
        prefs = refs[:n]
        x_ref, dx1_ref, g_ref, w_ref, dx_ref, dp_ref, dg_ref = refs[n:]

        @pl.when(pl.program_id(0) == 0)
        def _():
            dg_ref[...] = jnp.zeros_like(dg_ref)

        dh = None
        for i in range(n):
            blk = prefs[i][...]
            dp_ref[:, offs[i]:offs[i + 1]] = blk
            t = _dot_nt(blk, w_ref[:, offs[i]:offs[i + 1]])
            dh = t if dh is None else dh + t
        xhat, r = _rms_hat(x_ref[...])
        dxn, dg = _rms_bwd(xhat, r, g_ref[...], dh)
        dg_ref[...] += dg
        dx_ref[...] = dx1_ref[...] + dxn

    sd = jax.ShapeDtypeStruct
    return _call(
        body, (*parts, x, dx1, g1, w_in), name="inproj_bwd", grid=(T // tm,),
        in_specs=[_rows(tm, w) for w in widths] + [_rows(tm, D), _rows(tm, D), _const((1, D)), _resident((D, IN_W))],
        out_specs=[_rows(tm, D), _rows(tm, IN_W), _const((1, D))],
        out_shape=[sd((T, D), F32), sd((T, IN_W), BF16), sd((1, D), F32)], sem=("arbitrary",), comm=comm)


def _wgrad(a, g, tn, tm, name, comm=None):
    T, K = a.shape
    N = g.shape[1]

    def body(a_ref, g_ref, o_ref):
        @pl.when(pl.program_id(1) == 0)
        def _():
            o_ref[...] = jnp.zeros_like(o_ref)

        o_ref[...] += _dot_tn(a_ref[...], g_ref[...])

    return _call(
        body, (a, g), name=name, grid=(N // tn, T // tm),
        in_specs=[pl.BlockSpec((tm, K), lambda j, t: (t, 0)), pl.BlockSpec((tm, tn), lambda j, t: (t, j))],
        out_specs=pl.BlockSpec((K, tn), lambda j, t: (0, j)),
        out_shape=jax.ShapeDtypeStruct((K, N), F32), sem=("parallel", "arbitrary"), comm=comm)


def _adamw(w, g, m, v, tr, name):
    R, C = w.shape
    bc1 = 1.0 / (1.0 - B1 ** STEP)
    bc2 = 1.0 / (1.0 - B2 ** STEP)

    def body(w_ref, g_ref, m_ref, v_ref, d_ref, nm_ref, nv_ref):
        g = g_ref[...]
        m = B1 * m_ref[...] + (1.0 - B1) * g
        v = B2 * v_ref[...] + (1.0 - B2) * (g * g)
        nm_ref[...] = m
        nv_ref[...] = v
        d_ref[...] = -LR * ((m * bc1) / (jnp.sqrt(v * bc2) + AEPS) + WD * w_ref[...])

    spec = pl.BlockSpec((tr, C), lambda i: (i, 0))
    return pl.pallas_call(
        body, name=name, grid=(R // tr,), in_specs=[spec] * 4, out_specs=[spec] * 3,
        out_shape=[jax.ShapeDtypeStruct((R, C), F32)] * 3,
        compiler_params=_cparams(("parallel",)),
    )(w, g, m, v)


BIG = (("col", (D, IN_W)), ("row", (D, D)), ("row", (D, D)), ("row", (D, D)), ("col", (D, DFF)), ("row", (DFF, D)))
NBIG = len(BIG)
ANY = pl.BlockSpec(memory_space=pl.ANY)


def _shard_shape(kind, shape):
    R, C = shape
    return (R, C // 4) if kind == "col" else (R // 4, C)


def _half_shape(kind, shape):
    R, C = shape
    return (R // 2, C) if kind == "col" else (R, C // 2)


def _piece_shape(kind, shape):
    R, C = shape
    return (R // 2, C // 4) if kind == "col" else (R // 4, C // 2)


def _own_region(ref, kind, shape, s):
    R, C = shape
    return ref.at[:, pl.ds(s * (C // 4), C // 4)] if kind == "col" else ref.at[pl.ds(s * (R // 4), R // 4), :]


def _ag_region(ref, kind, shape, s, hc):
    R, C = shape
    if kind == "col":
        return ref.at[pl.ds(hc * (R // 2), R // 2), pl.ds(s * (C // 4), C // 4)]
    return ref.at[pl.ds(s * (R // 4) + hc * (R // 8), R // 8), :]


def _ag_shard_half(ref, kind, shape, hc):
    R, C = shape
    return ref.at[pl.ds(hc * (R // 2), R // 2), :] if kind == "col" else ref.at[pl.ds(hc * (R // 8), R // 8), :]


def _grad_half(ref, kind, shape, hc):
    R, C = shape
    return ref.at[pl.ds(hc * (R // 2), R // 2), :] if kind == "col" else ref.at[:, pl.ds(hc * (C // 2), C // 2)]


def _half_piece(ref, kind, shape, s):
    R, C = shape
    return ref.at[:, pl.ds(s * (C // 4), C // 4)] if kind == "col" else ref.at[pl.ds(s * (R // 4), R // 4), :]


def _place():
    x, y, c = lax.axis_index("x"), lax.axis_index("y"), lax.axis_index("c")
    chips = [(1 - x, y), (x, 1 - y), (1 - x, 1 - y)]
    return x, y, c, chips


def _rcopy(src, dst, ssem, rsem, dev):
    return pltpu.make_async_remote_copy(src_ref=src, dst_ref=dst, send_sem=ssem, recv_sem=rsem,
                                        device_id=dev, device_id_type=MESH)


def _dma_sems(n):
    return pltpu.SemaphoreType.DMA((n,))


def _x_gather_ici(shards, ws):
    n = len(ws)
    specs = [BIG[w] for w in ws]

    def place():
        x, y, c, chips = _place()
        return c, chips, 2 * x + y

    def sends(sh, full, sc):
        c, chips, me_s = place()
        return [_rcopy(_ag_shard_half(sh[i], kind, shape, c), _ag_region(full[i], kind, shape, me_s, c),
                       sc[0].at[3 * i + j], sc[1].at[3 * i + j], (cx, cy, c))
                for i, (kind, shape) in enumerate(specs) for j, (cx, cy) in enumerate(chips)]

    def start(sh, full, sc):
        for i in range(n):
            pltpu.make_async_copy(sh[i], sc[4 + i], sc[2].at[i]).start()
        for cp in sends(sh, full, sc):
            cp.start()

    def finish(sh, full, sc):
        c, chips, me_s = place()
        stores = []
        for i, (kind, shape) in enumerate(specs):
            pltpu.make_async_copy(sh[i], sc[4 + i], sc[2].at[i]).wait()
            st = pltpu.make_async_copy(sc[4 + i], _own_region(full[i], kind, shape, me_s), sc[3].at[i])
            st.start()
            stores.append(st)
        for i, (kind, shape) in enumerate(specs):
            for j, (cx, cy) in enumerate(chips):
                reg = _ag_region(full[i], kind, shape, 2 * cx + cy, c)
                _rcopy(reg, reg, sc[0].at[3 * i + j], sc[1].at[3 * i + j], (cx, cy, c)).wait_recv()
        for cp in sends(sh, full, sc):
            cp.wait_send()
        for st in stores:
            st.wait()

    return _Exchange(
        shards, [jax.ShapeDtypeStruct(shape, BF16) for _, shape in specs], {},
        [_dma_sems(3 * n), _dma_sems(3 * n), _dma_sems(n), _dma_sems(n)]
        + [pltpu.VMEM(_shard_shape(k, s), BF16) for k, s in specs], start, finish)


def _x_gather_d2d(wholes, ws):
    specs = [BIG[w] for w in ws]
    n = len(ws)

    def copies(full, sc, mine):
        x, y, c, chips = _place()
        hc = c if mine else 1 - c
        return [_rcopy(reg, reg, sc[0].at[3 * i + j], sc[1].at[3 * i + j], (x, y, 1 - c))
                for i, (kind, shape) in enumerate(specs) for j, (cx, cy) in enumerate(chips)
                for reg in [_ag_region(full[i], kind, shape, 2 * cx + cy, hc)]]

    def start(_, full, sc):
        for cp in copies(full, sc, True):
            cp.start()

    def finish(_, full, sc):
        for cp in copies(full, sc, False):
            cp.wait_recv()
        for cp in copies(full, sc, True):
            cp.wait_send()

    return _Exchange(wholes, [jax.ShapeDtypeStruct(shape, BF16) for _, shape in specs], {i: i for i in range(n)},
                     [_dma_sems(3 * n), _dma_sems(3 * n)], start, finish)


def _x_grads_sibling(grads, ws):
    specs = [BIG[w] for w in ws]
    n = len(ws)

    def copies(g, got, sc):
        x, y, c, _ = _place()
        return [_rcopy(_grad_half(g[i], kind, shape, 1 - c), got[i], sc[0].at[i], sc[1].at[i], (x, y, 1 - c))
                for i, (kind, shape) in enumerate(specs)]

    def start(g, got, sc):
        for cp in copies(g, got, sc):
            cp.start()

    def finish(g, got, sc):
        for cp in copies(g, got, sc):
            cp.wait_recv()
        for cp in copies(g, got, sc):
            cp.wait_send()

    return _Exchange(grads, [jax.ShapeDtypeStruct(_half_shape(k, s), F32) for k, s in specs], {},
                     [_dma_sems(n), _dma_sems(n)], start, finish)


def _x_grads_chips(sums_bf, ws):
    specs = [BIG[w] for w in ws]
    n = len(ws)

    def copies(s16, got, sc):
        x, y, c, chips = _place()
        return [_rcopy(_half_piece(s16[i], kind, shape, 2 * cx + cy), got[i].at[j],
                       sc[0].at[3 * i + j], sc[1].at[3 * i + j], (cx, cy, c))
                for i, (kind, shape) in enumerate(specs) for j, (cx, cy) in enumerate(chips)]

    def start(s16, got, sc):
        for cp in copies(s16, got, sc):
            cp.start()

    def finish(s16, got, sc):
        for cp in copies(s16, got, sc):
            cp.wait_recv()
        for cp in copies(s16, got, sc):
            cp.wait_send()

    return _Exchange(sums_bf, [jax.ShapeDtypeStruct((3,) + _piece_shape(k, s), BF16) for k, s in specs], {},
                     [_dma_sems(3 * n), _dma_sems(3 * n)], start, finish)


def _shard_half(ref, kind, shape, hc):
    sr, sc = _shard_shape(kind, shape)
    return ref.at[pl.ds(hc * (sr // 2), sr // 2), :] if kind == "col" else ref.at[:, pl.ds(hc * (sc // 2), sc // 2)]


def _x_grads_share(shard_grads, ws):
    specs = [BIG[w] for w in ws]
    n = len(ws)

    def copies(g, sc, mine):
        x, y, c, _ = _place()
        hc = c if mine else 1 - c
        return [_rcopy(part, part, sc[0].at[i], sc[1].at[i], (x, y, 1 - c))
                for i, (kind, shape) in enumerate(specs) for part in [_shard_half(g[i], kind, shape, hc)]]

    def start(_, g, sc):
        for cp in copies(g, sc, True):
            cp.start()

    def finish(_, g, sc):
        for cp in copies(g, sc, False):
            cp.wait_recv()
        for cp in copies(g, sc, True):
            cp.wait_send()

    return _Exchange(shard_grads, [jax.ShapeDtypeStruct(_shard_shape(k, s), F32) for k, s in specs],
                     {i: i for i in range(n)}, [_dma_sems(n), _dma_sems(n)], start, finish)


ADD_ROWS = 256


def _add_halves(place, g, got, kind, name):
    R, C = g.shape
    hr, hcols = _half_shape(kind, (R, C))
    steps = hr // ADD_ROWS

    def body(p_ref, g_ref, b_ref, s_ref, sb_ref):
        s = g_ref[...] + b_ref[...]
        s_ref[...] = s
        sb_ref[...] = s.astype(BF16)

    if kind == "col":
        g_spec = pl.BlockSpec((ADD_ROWS, C), lambda i, p: (p[0] * steps + i, 0))
    else:
        g_spec = pl.BlockSpec((ADD_ROWS, hcols), lambda i, p: (i, p[0]))
    spec = pl.BlockSpec((ADD_ROWS, hcols), lambda i, p: (i, 0))
    return pl.pallas_call(
        body, name=name,
        grid_spec=pltpu.PrefetchScalarGridSpec(num_scalar_prefetch=1, grid=(steps,), in_specs=[g_spec, spec],
                                               out_specs=[spec, spec]),
        out_shape=[jax.ShapeDtypeStruct((hr, hcols), F32), jax.ShapeDtypeStruct((hr, hcols), BF16)],
        compiler_params=_cparams(("parallel",)),
    )(place, g, got)


def _add_pieces(place, half, got, kind, shape, name):
    pr, pc = _piece_shape(kind, shape)
    steps = pr // ADD_ROWS

    def body(p_ref, m_ref, g_ref, o_ref):
        acc = m_ref[...]
        for j in range(3):
            acc = acc + g_ref[j].astype(F32)
        o_ref[...] = acc

    if kind == "col":
        m_spec = pl.BlockSpec((ADD_ROWS, pc), lambda i, p: (i, p[1]))
        o_spec = pl.BlockSpec((ADD_ROWS, pc), lambda i, p: (p[0] * steps + i, 0))
    else:
        m_spec = pl.BlockSpec((ADD_ROWS, pc), lambda i, p: (p[1] * steps + i, 0))
        o_spec = pl.BlockSpec((ADD_ROWS, pc), lambda i, p: (i, p[0]))
    return pl.pallas_call(
        body, name=name,
        grid_spec=pltpu.PrefetchScalarGridSpec(
            num_scalar_prefetch=1, grid=(steps,),
            in_specs=[m_spec, pl.BlockSpec((3, ADD_ROWS, pc), lambda i, p: (0, i, 0))], out_specs=o_spec),
        out_shape=jax.ShapeDtypeStruct(_shard_shape(kind, shape), F32),
        compiler_params=_cparams(("parallel",)),
    )(place, half, got)


SMALL_ROWS = 1024 + 8 * 8


def _x_small_all_reduce(p):
    def parts(p_ref, sc):
        slots, ssem, rsem = sc[0], sc[2], sc[3]
        x, y, c = lax.axis_index("x"), lax.axis_index("y"), lax.axis_index("c")
        me = 4 * x + 2 * y + c
        out = []
        for r in range(1, 8):
            bx, by, bc = (r >> 2) & 1, (r >> 1) & 1, r & 1
            tgt = (1 - x if bx else x, 1 - y if by else y, 1 - c if bc else c)
            send = _rcopy(p_ref, slots.at[me], ssem.at[r - 1], rsem.at[r - 1], tgt)
            src = 4 * tgt[0] + 2 * tgt[1] + tgt[2]
            recv = _rcopy(p_ref, slots.at[src], ssem.at[r - 1], rsem.at[r - 1], tgt)
            out.append((send, recv))
        return me, out

    def start(ins, outs, sc):
        me, cps = parts(ins[0], sc)
        pltpu.make_async_copy(ins[0], sc[0].at[me], sc[4].at[0]).start()
        for send, _ in cps:
            send.start()

    def finish(ins, outs, sc):
        me, cps = parts(ins[0], sc)
        pltpu.make_async_copy(ins[0], sc[0].at[me], sc[4].at[0]).wait()
        for _, recv in cps:
            recv.wait_recv()
        acc = sc[0][0]
        for d in range(1, 8):
            acc = acc + sc[0][d]
        sc[1][...] = acc
        back = pltpu.make_async_copy(sc[1], outs[0], sc[4].at[1])
        back.start()
        for send, _ in cps:
            send.wait_send()
        back.wait()

    return _Exchange([p], [jax.ShapeDtypeStruct((SMALL_ROWS, CH), F32)], {},
                     [pltpu.VMEM((8, SMALL_ROWS, CH), F32), pltpu.VMEM((SMALL_ROWS, CH), F32), _dma_sems(7), _dma_sems(7),
                      _dma_sems(2)], start, finish)


def _rope_tables(positions):
    inv_freq = 500000.0 ** (-jnp.arange(0, 2 * ROPE_HALF, 2, dtype=F32) / (2 * ROPE_HALF))
    head = jnp.concatenate([inv_freq, inv_freq, jnp.zeros((HD - 2 * ROPE_HALF,), F32)])
    lane_freq = jnp.concatenate([head, head])
    ang = positions.astype(F32)[:, None] * lane_freq[None, :]
    cos, sin = jnp.cos(ang), jnp.sin(ang)
    first = (jnp.arange(CH) % HD) < ROPE_HALF
    return cos, jnp.where(first[None, :], -sin, 0.0), jnp.where(first[None, :], 0.0, sin)


BIG_NAMES = ("w_in", "w_a", "w_b", "w_o", "w_ff_in", "w_ff_out")
SMALL_NAMES = ("w_spatial", "ln_v_gain", "ln_v_bias", "b_spatial", "sinks", "norm_mix_pre", "norm_mix_post",
               "norm_ff_pre", "norm_ff_post")
WEIGHT_ORDER = ("w_in", "ln_v_gain", "ln_v_bias", "w_spatial", "b_spatial", "sinks", "w_a", "w_b", "w_o",
                "norm_mix_pre", "norm_mix_post", "w_ff_in", "w_ff_out", "norm_ff_pre", "norm_ff_post")


def _pack_small(d):
    parts = []
    for n in SMALL_NAMES:
        flat = d[n].reshape(-1)
        pad = (-flat.shape[0]) % (8 * CH)
        parts.append(jnp.pad(flat, (0, pad)).reshape(-1, CH))
    return jnp.concatenate(parts, axis=0)


def _unpack_small(p, like):
    out, row = {}, 0
    for n in SMALL_NAMES:
        size = like[n].size
        rows = -(-size // (8 * CH)) * 8
        out[n] = p[row:row + rows].reshape(-1)[:size].reshape(like[n].shape)
        row += rows
    return out


def kernel(x, positions, w_in, ln_v_gain, ln_v_bias, w_spatial, b_spatial, sinks, w_a, w_b, w_o, norm_mix_pre, norm_mix_post, w_ff_in, w_ff_out, norm_ff_pre, norm_ff_post, loss_target, m_w_in, m_ln_v_gain, m_ln_v_bias, m_w_spatial, m_b_spatial, m_sinks, m_w_a, m_w_b, m_w_o, m_norm_mix_pre, m_norm_mix_post, m_w_ff_in, m_w_ff_out, m_norm_ff_pre, m_norm_ff_post, v_w_in, v_ln_v_gain, v_ln_v_bias, v_w_spatial, v_b_spatial, v_sinks, v_w_a, v_w_b, v_w_o, v_norm_mix_pre, v_norm_mix_post, v_w_ff_in, v_w_ff_out, v_norm_ff_pre, v_norm_ff_post):
    w = dict(w_in=w_in, ln_v_gain=ln_v_gain, ln_v_bias=ln_v_bias, w_spatial=w_spatial, b_spatial=b_spatial, sinks=sinks,
             w_a=w_a, w_b=w_b, w_o=w_o, norm_mix_pre=norm_mix_pre, norm_mix_post=norm_mix_post, w_ff_in=w_ff_in,
             w_ff_out=w_ff_out, norm_ff_pre=norm_ff_pre, norm_ff_post=norm_ff_post)
    m = dict(w_in=m_w_in, ln_v_gain=m_ln_v_gain, ln_v_bias=m_ln_v_bias, w_spatial=m_w_spatial, b_spatial=m_b_spatial,
             sinks=m_sinks, w_a=m_w_a, w_b=m_w_b, w_o=m_w_o, norm_mix_pre=m_norm_mix_pre, norm_mix_post=m_norm_mix_post,
             w_ff_in=m_w_ff_in, w_ff_out=m_w_ff_out, norm_ff_pre=m_norm_ff_pre, norm_ff_post=m_norm_ff_post)
    v = dict(w_in=v_w_in, ln_v_gain=v_ln_v_gain, ln_v_bias=v_ln_v_bias, w_spatial=v_w_spatial, b_spatial=v_b_spatial,
             sinks=v_sinks, w_a=v_w_a, w_b=v_w_b, w_o=v_w_o, norm_mix_pre=v_norm_mix_pre, norm_mix_post=v_norm_mix_post,
             w_ff_in=v_w_ff_in, w_ff_out=v_w_ff_out, norm_ff_pre=v_norm_ff_pre, norm_ff_post=v_norm_ff_post)

    FIRST, REST = (0,), tuple(range(1, NBIG))
    shards = [w[n][0].astype(BF16) for n in BIG_NAMES]
    place = jnp.stack([lax.axis_index("c"), 2 * lax.axis_index("x") + lax.axis_index("y")]).astype(jnp.int32)
    xs, target = x[0], loss_target[0]
    wtm = min(1024, xs.shape[0])
    g1, g2, g3, g4 = norm_mix_pre, norm_mix_post, norm_ff_pre, norm_ff_post
    w_sp, snk = w_spatial[0], sinks[0]
    rc, rs1, rs2 = _rope_tables(positions[0])
    bfull = jnp.repeat(b_spatial[0].T, CH, axis=1)

    def reduce_tail(ws, grads, got):
        sums = [_add_halves(place, grads[i], got[i], BIG[k][0], name="grad_add_sibling_" + BIG_NAMES[k])
                for i, k in enumerate(ws)]
        return sums, _x_grads_chips([s[1] for s in sums], ws)

    def reduce_end(ws, sums, pieces):
        return [_add_pieces(place, sums[i][0], pieces[i], *BIG[k], name="grad_add_chips_" + BIG_NAMES[k])
                for i, k in enumerate(ws)]

    w_in_b = _run(_x_gather_d2d(_run(_x_gather_ici(shards[:1], FIRST), "gather_w_in_ici"), FIRST), "gather_w_in_d2d")[0]
    (h, u, vs, q, k, va, ga, gb), rest = _inproj(xs, g1, w_in_b, rc, rs1, rs2, tm=256, comm=_x_gather_ici(shards[1:], REST))
    a, rest = _sgu_fwd(u, vs, ln_v_gain, ln_v_bias, w_sp, bfull, tm=512, comm=_x_gather_d2d(rest, REST))
    w_a_b, w_b_b, w_o_b, w_ff_in_b, w_ff_out_b = rest
    att = _attn_fwd(q, k, va, snk)
    pa, pb, merged, mix, x1 = _merge_fwd(a, att, ga, gb, xs, w_a_b, w_b_b, w_o_b, g2, tm=256)
    hf, f2, dff, df1, dx1, lsum, dg3, dg4 = _ffn(x1, target, w_ff_in_b, w_ff_out_b, g3, g4, tm=256)
    loss = lax.psum(0.5 * jnp.sum(lsum) / D, ("x", "y", "c"))

    dw_ff_out, _ = _wgrad(f2, dff, tn=512, tm=512, name="wgrad_ff_out")
    dw_ff_in, _ = _wgrad(hf, df1, tn=1024, tm=wtm, name="wgrad_ff_in")
    dmix, dao, dbo, dga, dgb, da, datt, dg2 = _merge_bwd(dx1, mix, ga, gb, pa, pb, w_a_b, w_b_b, w_o_b, g2, tm=256)
    dw_o, _ = _wgrad(merged, dmix, tn=1024, tm=wtm, name="wgrad_o")
    dw_a, _ = _wgrad(a, dao, tn=1024, tm=wtm, name="wgrad_a")
    dw_b, _ = _wgrad(att, dbo, tn=1024, tm=wtm, name="wgrad_b")
    grads_rest = [dw_a, dw_b, dw_o, dw_ff_in, dw_ff_out]
    (du, dvs, dws, dbs, dlg, dlb), got_rest = _sgu_bwd(
        u, vs, da, ln_v_gain, ln_v_bias, w_sp, bfull, tm=512, comm=_x_grads_sibling(grads_rest, REST))
    sums_rest, to_chips = reduce_tail(REST, grads_rest, got_rest)
    (dq, dk, dva, dsk), pieces_rest = _attn_bwd(q, k, va, datt, snk, rc, rs1, rs2, comm=to_chips)
    partial_rest = reduce_end(REST, sums_rest, pieces_rest)
    (dx, dproj, dg1), shard_rest = _inproj_bwd([du, dvs, dq, dk, dva, dga, dgb], xs, dx1, g1, w_in_b, tm=256,
                                               comm=_x_grads_share(partial_rest, REST))
    small = dict(ln_v_gain=dlg, ln_v_bias=dlb, w_spatial=dws, b_spatial=dbs, sinks=dsk[:, :NQ],
                 norm_mix_pre=dg1, norm_mix_post=dg2, norm_ff_pre=dg3, norm_ff_post=dg4)
    dw_in, (gs,) = _wgrad(h, dproj, tn=IN_W // 4, tm=wtm, name="wgrad_in", comm=_x_small_all_reduce(_pack_small(small)))
    got_in = _run(_x_grads_sibling([dw_in], FIRST), "grads_in_to_sibling")
    sums_in, to_chips = reduce_tail(FIRST, [dw_in], got_in)
    partial_in = reduce_end(FIRST, sums_in, _run(to_chips, "grads_in_to_chips"))
    shard_grads = list(_run(_x_grads_share(partial_in, FIRST), "grads_in_share")) + list(shard_rest)

    grad, delta, new_m, new_v = {}, {}, {}, {}
    for i, n in enumerate(BIG_NAMES):
        g = shard_grads[i]
        d_, m_, v_ = _adamw(w[n][0], g, m[n][0], v[n][0], tr=256, name="adamw_" + n)
        grad[n], delta[n], new_m[n], new_v[n] = g[None], d_[None], m_[None], v_[None]

    ds, ms, vs = _adamw(_pack_small(w), gs, _pack_small(m), _pack_small(v), tr=SMALL_ROWS // 4, name="adamw_small")
    for packed, dst in ((gs, grad), (ds, delta), (ms, new_m), (vs, new_v)):
        dst.update(_unpack_small(packed, w))

    outs = [loss, dx[None]]
    for group in (grad, delta, new_m, new_v):
        outs.extend(group[n] for n in WEIGHT_ORDER)
    return tuple(outs)
```

```python
import functools

import jax
import jax.numpy as jnp
from jax import lax
from jax.experimental import pallas as pl
from jax.experimental.pallas import tpu as pltpu

F32 = jnp.float32
BF16 = jnp.bfloat16

D = 1024
CH = 128
NG = 8
HD = 64
NQ = 16
NKV = 4
KVW = NKV * HD
DFF = 4 * D
EPS = 1e-6
IN_W = 5632
SEG = (0, 1024, 2048, 3072, 3328, 3584, 4608, 5632)
ROPE_HALF = 8
Q_SCALE = HD ** -0.5

LR, B1, B2, AEPS, WD, STEP = 0.001, 0.9, 0.999, 1e-08, 0.01, 10

VMEM_LIMIT = 56 * 1024 * 1024
MESH = pl.DeviceIdType.MESH

_GELU_C0 = 0.7978845608028654
_GELU_C1 = 0.044715


def _cparams(sem=None):
    kw = dict(vmem_limit_bytes=VMEM_LIMIT)
    if sem is not None:
        kw["dimension_semantics"] = sem
    return pltpu.CompilerParams(**kw)


def _resident(shape):
    nd = len(shape)
    return pl.BlockSpec(shape, lambda *_: (0,) * nd, pipeline_mode=pl.Buffered(1))


def _const(shape):
    nd = len(shape)
    return pl.BlockSpec(shape, lambda *_: (0,) * nd)


def _rows(tm, w):
    return pl.BlockSpec((tm, w), lambda i: (i, 0))


class _Exchange:
    def __init__(self, ins, outs, aliases, scratch, start, finish):
        self.ins, self.outs, self.aliases, self.scratch = list(ins), list(outs), dict(aliases), list(scratch)
        self.start, self.finish = start, finish


def _both(a, b):
    na, ma, sa = len(a.ins), len(a.outs), len(a.scratch)

    def start(ci, co, cs):
        a.start(ci[:na], co[:ma], cs[:sa])
        b.start(ci[na:], co[ma:], cs[sa:])

    def finish(ci, co, cs):
        a.finish(ci[:na], co[:ma], cs[:sa])
        b.finish(ci[na:], co[ma:], cs[sa:])

    aliases = {**a.aliases, **{na + i: ma + j for i, j in b.aliases.items()}}
    return _Exchange(a.ins + b.ins, a.outs + b.outs, aliases, a.scratch + b.scratch, start, finish)


def _call(body, args, *, name, grid, in_specs, out_specs, out_shape, scratch_shapes=(), sem=None, comm=None):
    single = not isinstance(out_shape, (list, tuple))
    out_shape = [out_shape] if single else list(out_shape)
    out_specs = [out_specs] if single else list(out_specs)
    if comm is None:
        res = pl.pallas_call(body, name=name, grid=grid, in_specs=list(in_specs), out_specs=out_specs,
                             out_shape=out_shape, scratch_shapes=list(scratch_shapes),
                             compiler_params=_cparams(sem))(*args)
        return (res[0] if single else res), []
    n_in, n_out, n_scr = len(args), len(out_shape), len(scratch_shapes)
    nci, nco = len(comm.ins), len(comm.outs)
    steps = 1
    for g in grid:
        steps *= g

    def hosted(*refs):
        a, ci = refs[:n_in], refs[n_in:n_in + nci]
        o, co = refs[n_in + nci:n_in + nci + n_out], refs[n_in + nci + n_out:n_in + nci + n_out + nco]
        rest = refs[n_in + nci + n_out + nco:]
        scr, cs = rest[:n_scr], rest[n_scr:]
        step = pl.program_id(0)
        for d in range(1, len(grid)):
            step = step * grid[d] + pl.program_id(d)

        @pl.when(step == 0)
        def _():
            comm.start(ci, co, cs)

        body(*a, *o, *scr)

        @pl.when(step == steps - 1)
        def _():
            comm.finish(ci, co, cs)

    res = pl.pallas_call(
        hosted, name=name, grid=grid, in_specs=list(in_specs) + [ANY] * nci, out_specs=out_specs + [ANY] * nco,
        out_shape=out_shape + comm.outs, scratch_shapes=list(scratch_shapes) + comm.scratch,
        input_output_aliases={n_in + i: n_out + j for i, j in comm.aliases.items()},
        compiler_params=_cparams(("arbitrary",) * len(grid)),
    )(*args, *comm.ins)
    own = res[:n_out]
    return (own[0] if single else own), list(res[n_out:])


def _run(comm, name):
    nci = len(comm.ins)

    def body(*refs):
        ci, co, cs = refs[:nci], refs[nci:nci + len(comm.outs)], refs[nci + len(comm.outs):]
        comm.start(ci, co, cs)
        comm.finish(ci, co, cs)

    return pl.pallas_call(
        body, name=name, in_specs=[ANY] * nci, out_specs=[ANY] * len(comm.outs), out_shape=comm.outs,
        scratch_shapes=comm.scratch, input_output_aliases=comm.aliases,
        compiler_params=pltpu.CompilerParams(vmem_limit_bytes=VMEM_LIMIT),
    )(*comm.ins)


def _gelu(x):
    t = jnp.tanh(_GELU_C0 * (x + _GELU_C1 * (x * x * x)))
    return 0.5 * x * (1.0 + t), t


def _gelu_grad(x, t):
    return 0.5 * (1.0 + t) + 0.5 * x * (1.0 - t * t) * (_GELU_C0 * (1.0 + 3.0 * _GELU_C1 * x * x))


def _sigmoid(x):
    return 1.0 / (1.0 + jnp.exp(-x))


def _rms_hat(x):
    r = lax.rsqrt(jnp.mean(x * x, axis=-1, keepdims=True) + EPS)
    return x * r, r


def _rms_bwd(xhat, r, g, dout):
    dg = jnp.sum(dout * xhat, axis=0, keepdims=True)
    dy = dout * g
    dx = r * (dy - xhat * jnp.mean(dy * xhat, axis=-1, keepdims=True))
    return dx, dg


def _dot(a, b):
    return jnp.dot(a, b, preferred_element_type=F32)


def _dot_nt(a, b):
    return lax.dot_general(a, b, (((1,), (1,)), ((), ())), preferred_element_type=F32)


def _dot_tn(a, b):
    return lax.dot_general(a, b, (((0,), (0,)), ((), ())), preferred_element_type=F32)


def _rope(blk, c, s1, s2):
    return blk * c + pltpu.roll(blk, CH - ROPE_HALF, 1) * s1 + pltpu.roll(blk, ROPE_HALF, 1) * s2


def _rope_t(blk, c, s1, s2):
    return blk * c + pltpu.roll(blk * s1, ROPE_HALF, 1) + pltpu.roll(blk * s2, CH - ROPE_HALF, 1)


def _inproj(x, g1, w_in, rc, rs1, rs2, tm, comm=None):
    T = x.shape[0]

    def body(x_ref, g_ref, w_ref, c_ref, s1_ref, s2_ref,
             h_ref, u_ref, v_ref, q_ref, k_ref, va_ref, ga_ref, gb_ref):
        xhat, _ = _rms_hat(x_ref[...])
        h = (xhat * g_ref[...]).astype(BF16)
        h_ref[...] = h
        u_ref[...] = _dot(h, w_ref[:, SEG[0]:SEG[1]])
        v_ref[...] = _dot(h, w_ref[:, SEG[1]:SEG[2]])
        c, s1, s2 = c_ref[...], s1_ref[...], s2_ref[...]
        q = _dot(h, w_ref[:, SEG[2]:SEG[3]])
        for p in range(D // CH):
            blk = _rope(q[:, CH * p:CH * (p + 1)], c, s1, s2) * Q_SCALE
            q_ref[:, CH * p:CH * (p + 1)] = blk.astype(BF16)
        k = _dot(h, w_ref[:, SEG[3]:SEG[4]])
        for p in range(KVW // CH):
            k_ref[:, CH * p:CH * (p + 1)] = _rope(k[:, CH * p:CH * (p + 1)], c, s1, s2).astype(BF16)
        va_ref[...] = _dot(h, w_ref[:, SEG[4]:SEG[5]]).astype(BF16)
        ga_ref[...] = _dot(h, w_ref[:, SEG[5]:SEG[6]])
        gb_ref[...] = _dot(h, w_ref[:, SEG[6]:SEG[7]])

    sd = jax.ShapeDtypeStruct
    return _call(
        body, (x, g1, w_in, rc, rs1, rs2), name="inproj_fwd", grid=(T // tm,),
        in_specs=[_rows(tm, D), _const((1, D)), _resident((D, IN_W)), _rows(tm, CH), _rows(tm, CH), _rows(tm, CH)],
        out_specs=[_rows(tm, D), _rows(tm, D), _rows(tm, D), _rows(tm, D), _rows(tm, KVW), _rows(tm, KVW),
                   _rows(tm, D), _rows(tm, D)],
        out_shape=[sd((T, D), BF16), sd((T, D), F32), sd((T, D), F32), sd((T, D), BF16), sd((T, KVW), BF16),
                   sd((T, KVW), BF16), sd((T, D), F32), sd((T, D), F32)],
        sem=("parallel",), comm=comm)


def _sgu_common(u, vs, lng, lnb, ws_ref, bfull):
    nc = u.shape[0] // CH
    ug, tu = _gelu(u)
    vg, tv = _gelu(vs)
    mu = jnp.mean(vg, axis=-1, keepdims=True)
    xc = vg - mu
    rstd = lax.rsqrt(jnp.mean(xc * xc, axis=-1, keepdims=True) + EPS)
    vhat = xc * rstd
    vnb = (vhat * lng + lnb).astype(BF16)
    tri = lax.broadcasted_iota(jnp.int32, (CH, CH), 0) >= lax.broadcasted_iota(jnp.int32, (CH, CH), 1)
    wts, rhss, mixed = [], [], []
    for g in range(NG):
        wt = jnp.where(tri, ws_ref[g], 0.0).astype(BF16)
        rhs = jnp.concatenate([vnb[CH * c:CH * (c + 1), CH * g:CH * (g + 1)] for c in range(nc)], axis=1)
        mix = _dot(wt, rhs)
        wts.append(wt)
        rhss.append(rhs)
        mixed.append([mix[:, CH * c:CH * (c + 1)] + bfull[:, CH * g:CH * (g + 1)] for c in range(nc)])
    return nc, ug, tu, tv, rstd, vhat, tri, wts, rhss, mixed


def _sgu_fwd(u, vs, lng, lnb, ws, bfull, tm, comm=None):
    T = u.shape[0]

    def body(u_ref, v_ref, lng_ref, lnb_ref, ws_ref, bf_ref, a_ref):
        nc, ug, _, _, _, _, _, _, _, mixed = _sgu_common(
            u_ref[...], v_ref[...], lng_ref[...], lnb_ref[...], ws_ref, bf_ref[...])
        mixed_all = jnp.concatenate(
            [jnp.concatenate([mixed[g][c] for g in range(NG)], axis=1) for c in range(nc)], axis=0)
        a_ref[...] = (ug * mixed_all).astype(BF16)

    return _call(
        body, (u, vs, lng, lnb, ws, bfull), name="sgu_fwd", grid=(T // tm,),
        in_specs=[_rows(tm, D), _rows(tm, D), _const((1, D)), _const((1, D)), _const((NG, CH, CH)), _const((CH, D))],
        out_specs=_rows(tm, D), out_shape=jax.ShapeDtypeStruct((T, D), BF16), sem=("parallel",), comm=comm)


def _sgu_bwd(u, vs, da, lng, lnb, ws, bfull, tm, comm=None):
    T = u.shape[0]
    nsteps = T // tm

    def body(u_ref, v_ref, da_ref, lng_ref, lnb_ref, ws_ref, bf_ref,
             du_ref, dv_ref, dws_ref, dbs_ref, dlg_ref, dlb_ref, db_ref):
        i = pl.program_id(0)
        u, vs, da, lng = u_ref[...], v_ref[...], da_ref[...], lng_ref[...]
        nc, ug, tu, tv, rstd, vhat, tri, wts, rhss, mixed = _sgu_common(u, vs, lng, lnb_ref[...], ws_ref, bf_ref[...])

        @pl.when(i == 0)
        def _():
            dws_ref[...] = jnp.zeros_like(dws_ref)
            db_ref[...] = jnp.zeros_like(db_ref)
            dlg_ref[...] = jnp.zeros_like(dlg_ref)
            dlb_ref[...] = jnp.zeros_like(dlb_ref)

        mixed_all = jnp.concatenate(
            [jnp.concatenate([mixed[g][c] for g in range(NG)], axis=1) for c in range(nc)], axis=0)
        du_ref[...] = (da * mixed_all * _gelu_grad(u, tu)).astype(BF16)
        dmixed = da * ug
        dvn_cols = []
        for g in range(NG):
            dmix = [dmixed[CH * c:CH * (c + 1), CH * g:CH * (g + 1)] for c in range(nc)]
            db_ref[:, CH * g:CH * (g + 1)] += functools.reduce(lambda a, b: a + b, dmix)
            dm = jnp.concatenate(dmix, axis=1).astype(BF16)
            dws_ref[g] += _dot_nt(dm, rhss[g])
            dvn_cols.append(_dot_tn(wts[g], dm))
        dvn = jnp.concatenate(
            [jnp.concatenate([dvn_cols[g][:, CH * c:CH * (c + 1)] for g in range(NG)], axis=1) for c in range(nc)],
            axis=0)
        dlg_ref[...] += jnp.sum(dvn * vhat, axis=0, keepdims=True)
        dlb_ref[...] += jnp.sum(dvn, axis=0, keepdims=True)
        dvh = dvn * lng
        dvg = rstd * (dvh - jnp.mean(dvh, axis=-1, keepdims=True)
                      - vhat * jnp.mean(dvh * vhat, axis=-1, keepdims=True))
        dv_ref[...] = (dvg * _gelu_grad(vs, tv)).astype(BF16)

        @pl.when(i == nsteps - 1)
        def _():
            for g in range(NG):
                dws_ref[g] = jnp.where(tri, dws_ref[g], 0.0)
                dbs_ref[g:g + 1, :] = jnp.sum(db_ref[:, CH * g:CH * (g + 1)].T, axis=0, keepdims=True)

    sd = jax.ShapeDtypeStruct
    return _call(
        body, (u, vs, da, lng, lnb, ws, bfull), name="sgu_bwd", grid=(nsteps,),
        in_specs=[_rows(tm, D), _rows(tm, D), _rows(tm, D), _const((1, D)), _const((1, D)), _const((NG, CH, CH)),
                  _const((CH, D))],
        out_specs=[_rows(tm, D), _rows(tm, D), _const((NG, CH, CH)), _const((NG, CH)), _const((1, D)), _const((1, D))],
        out_shape=[sd((T, D), BF16), sd((T, D), BF16), sd((NG, CH, CH), F32), sd((NG, CH), F32), sd((1, D), F32),
                   sd((1, D), F32)],
        scratch_shapes=[pltpu.VMEM((CH, D), F32)], sem=("arbitrary",), comm=comm)


def _pair_layout(prev, cur, grp):
    j, half = grp // 2, grp % 2
    blk = jnp.concatenate([prev[:, CH * j:CH * (j + 1)], cur[:, CH * j:CH * (j + 1)]], axis=0).astype(F32)
    lo = lax.broadcasted_iota(jnp.int32, blk.shape, 1) < HD
    rolled = pltpu.roll(blk, HD, 1)
    even = jnp.where(lo, blk if half == 0 else rolled, 0.0)
    odd = jnp.where(lo, 0.0, rolled if half == 0 else blk)
    return jnp.concatenate([even, odd], axis=0).astype(BF16)


def _attn_mask(n):
    qi = lax.broadcasted_iota(jnp.int32, (CH, 2 * CH), 0)
    kc = lax.broadcasted_iota(jnp.int32, (CH, 2 * CH), 1)
    ok = (kc > qi) & (kc <= qi + CH) & ((kc >= CH) | (n > 0))
    return jnp.concatenate([ok, ok], axis=1)


def _softmax_sink(s, sink):
    m = jnp.maximum(jnp.max(s, axis=-1, keepdims=True), sink)
    p = jnp.exp(s - m)
    ps = jnp.exp(sink - m)
    inv = 1.0 / (jnp.sum(p, axis=-1, keepdims=True) + ps)
    return p * inv, ps * inv


def _attn_fwd(q, k, va, sinks):
    T = q.shape[0]
    nb = T // CH

    def body(sk_ref, q_ref, kp_ref, kc_ref, vp_ref, vc_ref, o_ref):
        n = pl.program_id(0)
        mask = _attn_mask(n)
        kp, kc, vp, vc = kp_ref[...], kc_ref[...], vp_ref[...], vc_ref[...]
        kks = [_pair_layout(kp, kc, grp) for grp in range(NKV)]
        vvs = [_pair_layout(vp, vc, grp) for grp in range(NKV)]
        npairs = D // CH

        def scores(p):
            return _dot_nt(q_ref[:, CH * p:CH * (p + 1)], kks[p // 2])

        ahead = 3
        outs, probs = [], []
        pending = [scores(p) for p in range(ahead)]
        for p in range(npairs):
            s = jnp.where(mask, pending.pop(0), -1e30)
            if p + ahead < npairs:
                pending.append(scores(p + ahead))
            pe, _ = _softmax_sink(s[:, :2 * CH], sk_ref[2 * p])
            po, _ = _softmax_sink(s[:, 2 * CH:], sk_ref[2 * p + 1])
            probs.append(jnp.concatenate([pe, po], axis=1).astype(BF16))
            if p >= 1:
                outs.append(_dot(probs[p - 1], vvs[(p - 1) // 2]))
        outs.append(_dot(probs[-1], vvs[-1]))
        o_ref[...] = jnp.concatenate(outs, axis=1).astype(BF16)

    prev = lambda n: (jnp.maximum(n - 1, 0), 0)
    cur = lambda n: (n, 0)
    return pl.pallas_call(
        body, name="attn_fwd", grid=(nb,),
        in_specs=[pl.BlockSpec(memory_space=pltpu.SMEM), pl.BlockSpec((CH, D), cur),
                  pl.BlockSpec((CH, KVW), prev), pl.BlockSpec((CH, KVW), cur),
                  pl.BlockSpec((CH, KVW), prev), pl.BlockSpec((CH, KVW), cur)],
        out_specs=pl.BlockSpec((CH, D), cur), out_shape=jax.ShapeDtypeStruct((T, D), BF16),
        compiler_params=_cparams(("parallel",)),
    )(sinks, q, k, k, va, va)


def _attn_bwd(q, k, va, datt, sinks, rc, rs1, rs2, comm=None):
    T = q.shape[0]
    nb = T // CH

    def body(sk_ref, q_ref, kp_ref, kc_ref, vp_ref, vc_ref, do_ref, cq_ref, s1q_ref, s2q_ref, ck_ref, s1k_ref, s2k_ref,
             dq_ref, dk_ref, dv_ref, dsk_ref, kcar, vcar):
        n = pl.program_id(0)

        @pl.when(n == 0)
        def _():
            kcar[...] = jnp.zeros_like(kcar)
            vcar[...] = jnp.zeros_like(vcar)
            dsk_ref[...] = jnp.zeros_like(dsk_ref)

        def flush(kprev, vprev):
            ck, s1k, s2k = ck_ref[...], s1k_ref[...], s2k_ref[...]
            for j in range(KVW // CH):
                sl = slice(CH * j, CH * (j + 1))
                dk_ref[:, sl] = _rope_t(kcar[:, sl] + kprev[:, sl], ck, s1k, s2k).astype(BF16)
                dv_ref[:, sl] = (vcar[:, sl] + vprev[:, sl]).astype(BF16)

        @pl.when(n < nb)
        def _():
            mask = _attn_mask(n)
            kp, kc, vp, vc = kp_ref[...], kc_ref[...], vp_ref[...], vc_ref[...]
            cq, s1q, s2q = cq_ref[...], s1q_ref[...], s2q_ref[...]
            lane = lax.broadcasted_iota(jnp.int32, (1, CH), 1)
            dsk = jnp.zeros((1, CH), F32)
            npairs = D // CH
            kks = [_pair_layout(kp, kc, grp) for grp in range(NKV)]
            vvs = [_pair_layout(vp, vc, grp) for grp in range(NKV)]
            qs = [q_ref[:, CH * p:CH * (p + 1)] for p in range(npairs)]
            dos = [do_ref[:, CH * p:CH * (p + 1)].astype(BF16) for p in range(npairs)]

            def first(p):
                return _dot_nt(qs[p], kks[p // 2]), _dot_nt(dos[p], vvs[p // 2])

            def last(p, ds, pb):
                return (_rope_t(_dot(ds, kks[p // 2]), cq, s1q, s2q) * Q_SCALE, _dot_tn(qs[p], ds), _dot_tn(dos[p], pb))

            ahead = 2
            pending = [first(p) for p in range(ahead)]
            mids, ends = [], []
            for p in range(npairs):
                s, dp = pending.pop(0)
                s = jnp.where(mask, s, -1e30)
                if p + ahead < npairs:
                    pending.append(first(p + ahead))
                ds_parts, p_parts = [], []
                for par in range(2):
                    sl = slice(2 * CH * par, 2 * CH * (par + 1))
                    pr, psink = _softmax_sink(s[:, sl], sk_ref[2 * p + par])
                    delta = jnp.sum(pr * dp[:, sl], axis=-1, keepdims=True)
                    ds_parts.append(pr * (dp[:, sl] - delta))
                    p_parts.append(pr)
                    tot = -jnp.sum(psink * delta, axis=0, keepdims=True)
                    dsk = dsk + jnp.where(lane == 2 * p + par, tot, 0.0)
                mids.append((jnp.concatenate(ds_parts, axis=1).astype(BF16), jnp.concatenate(p_parts, axis=1).astype(BF16)))
                if p >= 1:
                    ends.append(last(p - 1, *mids[p - 1]))
            ends.append(last(npairs - 1, *mids[-1]))
            dq_cols = [e[0] for e in ends]
            def fold(i):
                rows = []
                for grp in range(NKV):
                    acc = ends[2 * grp][i] + ends[2 * grp + 1][i]
                    rows.append(acc[:HD, :2 * CH] + acc[HD:, 2 * CH:])
                return jnp.concatenate(rows, axis=0).T

            dkf, dvf = fold(1), fold(2)
            dq_ref[...] = jnp.concatenate(dq_cols, axis=1).astype(BF16)
            dsk_ref[...] += dsk
            flush(dkf[:CH], dvf[:CH])
            kcar[...] = dkf[CH:]
            vcar[...] = dvf[CH:]

        @pl.when(n == nb)
        def _():
            z = jnp.zeros((CH, KVW), F32)
            flush(z, z)

    last = nb - 1
    cur = lambda n: (jnp.minimum(n, last), 0)
    prev = lambda n: (jnp.clip(n - 1, 0, last), 0)
    sd = jax.ShapeDtypeStruct
    return _call(
        body, (sinks, q, k, k, va, va, datt, rc, rs1, rs2, rc, rs1, rs2), name="attn_bwd", grid=(nb + 1,),
        in_specs=[pl.BlockSpec(memory_space=pltpu.SMEM), pl.BlockSpec((CH, D), cur),
                  pl.BlockSpec((CH, KVW), prev), pl.BlockSpec((CH, KVW), cur),
                  pl.BlockSpec((CH, KVW), prev), pl.BlockSpec((CH, KVW), cur),
                  pl.BlockSpec((CH, D), cur),
                  pl.BlockSpec((CH, CH), cur), pl.BlockSpec((CH, CH), cur), pl.BlockSpec((CH, CH), cur),
                  pl.BlockSpec((CH, CH), prev), pl.BlockSpec((CH, CH), prev), pl.BlockSpec((CH, CH), prev)],
        out_specs=[pl.BlockSpec((CH, D), cur), pl.BlockSpec((CH, KVW), prev), pl.BlockSpec((CH, KVW), prev),
                   _const((1, CH))],
        out_shape=[sd((T, D), BF16), sd((T, KVW), BF16), sd((T, KVW), BF16), sd((1, CH), F32)],
        scratch_shapes=[pltpu.VMEM((CH, KVW), F32), pltpu.VMEM((CH, KVW), F32)], sem=("arbitrary",), comm=comm)


def _merge_fwd(a, att, ga, gb, x, w_a, w_b, w_o, g2, tm):
    T = x.shape[0]

    def body(a_ref, att_ref, ga_ref, gb_ref, x_ref, wa_ref, wb_ref, wo_ref, g_ref,
             pa_ref, pb_ref, mg_ref, mix_ref, x1_ref):
        pa = _dot(a_ref[...], wa_ref[...])
        pb = _dot(att_ref[...], wb_ref[...])
        pa_ref[...] = pa
        pb_ref[...] = pb
        merged = (_sigmoid(ga_ref[...]) * pa + _sigmoid(gb_ref[...]) * pb).astype(BF16)
        mg_ref[...] = merged
        mix = _dot(merged, wo_ref[...])
        mix_ref[...] = mix
        mhat, _ = _rms_hat(mix)
        x1_ref[...] = x_ref[...] + mhat * g_ref[...]

    sd = jax.ShapeDtypeStruct
    return pl.pallas_call(
        body, name="merge_fwd", grid=(T // tm,),
        in_specs=[_rows(tm, D)] * 5 + [_resident((D, D))] * 3 + [_const((1, D))],
        out_specs=[_rows(tm, D)] * 5,
        out_shape=[sd((T, D), F32), sd((T, D), F32), sd((T, D), BF16), sd((T, D), F32), sd((T, D), F32)],
        compiler_params=_cparams(("parallel",)),
    )(a, att, ga, gb, x, w_a, w_b, w_o, g2)


def _merge_bwd(dx1, mix, ga, gb, pa, pb, w_a, w_b, w_o, g2, tm):
    T = dx1.shape[0]

    def body(dx1_ref, mix_ref, ga_ref, gb_ref, pa_ref, pb_ref, wa_ref, wb_ref, wo_ref, g_ref,
             dmix_ref, dao_ref, dbo_ref, dga_ref, dgb_ref, da_ref, datt_ref, dg_ref):
        @pl.when(pl.program_id(0) == 0)
        def _():
            dg_ref[...] = jnp.zeros_like(dg_ref)

        mhat, r = _rms_hat(mix_ref[...])
        dmix, dg = _rms_bwd(mhat, r, g_ref[...], dx1_ref[...])
        dg_ref[...] += dg
        dmix = dmix.astype(BF16)
        dmix_ref[...] = dmix
        dmerged = _dot_nt(dmix, wo_ref[...])
        sa = _sigmoid(ga_ref[...])
        sb = _sigmoid(gb_ref[...])
        dao = (dmerged * sa).astype(BF16)
        dbo = (dmerged * sb).astype(BF16)
        dao_ref[...] = dao
        dbo_ref[...] = dbo
        dga_ref[...] = (dmerged * pa_ref[...] * (sa * (1.0 - sa))).astype(BF16)
        dgb_ref[...] = (dmerged * pb_ref[...] * (sb * (1.0 - sb))).astype(BF16)
        da_ref[...] = _dot_nt(dao, wa_ref[...])
        datt_ref[...] = _dot_nt(dbo, wb_ref[...]).astype(BF16)

    sd = jax.ShapeDtypeStruct
    return pl.pallas_call(
        body, name="merge_bwd", grid=(T // tm,),
        in_specs=[_rows(tm, D)] * 6 + [_resident((D, D))] * 3 + [_const((1, D))],
        out_specs=[_rows(tm, D)] * 7 + [_const((1, D))],
        out_shape=[sd((T, D), BF16)] * 5 + [sd((T, D), F32), sd((T, D), BF16), sd((1, D), F32)],
        compiler_params=_cparams(("arbitrary",)),
    )(dx1, mix, ga, gb, pa, pb, w_a, w_b, w_o, g2)


def _ffn(x1, target, w1, w2, g3, g4, tm):
    T = x1.shape[0]

    def body(x_ref, t_ref, w1_ref, w2_ref, g3_ref, g4_ref,
             hf_ref, f2_ref, dff_ref, df1_ref, dx_ref, ls_ref, dg3_ref, dg4_ref):
        @pl.when(pl.program_id(0) == 0)
        def _():
            ls_ref[...] = jnp.zeros_like(ls_ref)
            dg3_ref[...] = jnp.zeros_like(dg3_ref)
            dg4_ref[...] = jnp.zeros_like(dg4_ref)

        x = x_ref[...]
        g3, g4 = g3_ref[...], g4_ref[...]
        xhat, r3 = _rms_hat(x)
        hf = (xhat * g3).astype(BF16)
        hf_ref[...] = hf
        rl = jnp.maximum(_dot(hf, w1_ref[...]), 0.0)
        f2 = (rl * rl).astype(BF16)
        f2_ref[...] = f2
        fhat, r4 = _rms_hat(_dot(f2, w2_ref[...]))
        err = x + fhat * g4 - t_ref[...]
        ls_ref[...] += jnp.sum(err * err, axis=0, keepdims=True)
        dy = err * (1.0 / D)
        dff, dg4 = _rms_bwd(fhat, r4, g4, dy)
        dg4_ref[...] += dg4
        dff = dff.astype(BF16)
        dff_ref[...] = dff
        df1 = (_dot_nt(dff, w2_ref[...]) * (2.0 * rl)).astype(BF16)
        df1_ref[...] = df1
        dxn, dg3 = _rms_bwd(xhat, r3, g3, _dot_nt(df1, w1_ref[...]))
        dg3_ref[...] += dg3
        dx_ref[...] = dy + dxn

    sd = jax.ShapeDtypeStruct
    return pl.pallas_call(
        body, name="ffn_fwd_bwd", grid=(T // tm,),
        in_specs=[_rows(tm, D), _rows(tm, D), _resident((D, DFF)), _resident((DFF, D)), _const((1, D)), _const((1, D))],
        out_specs=[_rows(tm, D), _rows(tm, DFF), _rows(tm, D), _rows(tm, DFF), _rows(tm, D), _const((1, D)),
                   _const((1, D)), _const((1, D))],
        out_shape=[sd((T, D), BF16), sd((T, DFF), BF16), sd((T, D), BF16), sd((T, DFF), BF16), sd((T, D), F32),
                   sd((1, D), F32), sd((1, D), F32), sd((1, D), F32)],
        compiler_params=_cparams(("arbitrary",)),
    )(x1, target, w1, w2, g3, g4)


def _inproj_bwd(parts, x, dx1, g1, w_in, tm, comm=None):
    T = x.shape[0]
    widths = [p.shape[1] for p in parts]
    offs = [sum(widths[:i]) for i in range(len(widths) + 1)]
    assert offs[-1] == IN_W

    def body(*refs):
        n = len(parts)
        prefs = refs[:n]
        x_ref, dx1_ref, g_ref, w_ref, dx_ref, dp_ref, dg_ref = refs[n:]

        @pl.when(pl.program_id(0) == 0)
        def _():
            dg_ref[...] = jnp.zeros_like(dg_ref)

        dh = None
        for i in range(n):
            blk = prefs[i][...]
            dp_ref[:, offs[i]:offs[i + 1]] = blk
            t = _dot_nt(blk, w_ref[:, offs[i]:offs[i + 1]])
            dh = t if dh is None else dh + t
        xhat, r = _rms_hat(x_ref[...])
        dxn, dg = _rms_bwd(xhat, r, g_ref[...], dh)
        dg_ref[...] += dg
        dx_ref[...] = dx1_ref[...] + dxn

    sd = jax.ShapeDtypeStruct
    return _call(
        body, (*parts, x, dx1, g1, w_in), name="inproj_bwd", grid=(T // tm,),
        in_specs=[_rows(tm, w) for w in widths] + [_rows(tm, D), _rows(tm, D), _const((1, D)), _resident((D, IN_W))],
        out_specs=[_rows(tm, D), _rows(tm, IN_W), _const((1, D))],
        out_shape=[sd((T, D), F32), sd((T, IN_W), BF16), sd((1, D), F32)], sem=("arbitrary",), comm=comm)


def _wgrad(a, g, tn, tm, name, comm=None):
    T, K = a.shape
    N = g.shape[1]

    def body(a_ref, g_ref, o_ref):
        @pl.when(pl.program_id(1) == 0)
        def _():
            o_ref[...] = jnp.zeros_like(o_ref)

        o_ref[...] += _dot_tn(a_ref[...], g_ref[...])

    return _call(
        body, (a, g), name=name, grid=(N // tn, T // tm),
        in_specs=[pl.BlockSpec((tm, K), lambda j, t: (t, 0)), pl.BlockSpec((tm, tn), lambda j, t: (t, j))],
        out_specs=pl.BlockSpec((K, tn), lambda j, t: (0, j)),
        out_shape=jax.ShapeDtypeStruct((K, N), F32), sem=("parallel", "arbitrary"), comm=comm)


def _adamw(w, g, m, v, tr, name):
    R, C = w.shape
    bc1 = 1.0 / (1.0 - B1 ** STEP)
    bc2 = 1.0 / (1.0 - B2 ** STEP)

    def body(w_ref, g_ref, m_ref, v_ref, d_ref, nm_ref, nv_ref):
        g = g_ref[...]
        m = B1 * m_ref[...] + (1.0 - B1) * g
        v = B2 * v_ref[...] + (1.0 - B2) * (g * g)
        nm_ref[...] = m
        nv_ref[...] = v
        d_ref[...] = -LR * ((m * bc1) / (jnp.sqrt(v * bc2) + AEPS) + WD * w_ref[...])

    spec = pl.BlockSpec((tr, C), lambda i: (i, 0))
    return pl.pallas_call(
        body, name=name, grid=(R // tr,), in_specs=[spec] * 4, out_specs=[spec] * 3,
        out_shape=[jax.ShapeDtypeStruct((R, C), F32)] * 3,
        compiler_params=_cparams(("parallel",)),
    )(w, g, m, v)


BIG = (("col", (D, IN_W)), ("row", (D, D)), ("row", (D, D)), ("row", (D, D)), ("col", (D, DFF)), ("row", (DFF, D)))
NBIG = len(BIG)
ANY = pl.BlockSpec(memory_space=pl.ANY)


def _shard_shape(kind, shape):
    R, C = shape
    return (R, C // 4) if kind == "col" else (R // 4, C)


def _half_shape(kind, shape):
    R, C = shape
    return (R // 2, C) if kind == "col" else (R, C // 2)


def _piece_shape(kind, shape):
    R, C = shape
    return (R // 2, C // 4) if kind == "col" else (R // 4, C // 2)


def _own_region(ref, kind, shape, s):
    R, C = shape
    return ref.at[:, pl.ds(s * (C // 4), C // 4)] if kind == "col" else ref.at[pl.ds(s * (R // 4), R // 4), :]


def _ag_region(ref, kind, shape, s, hc):
    R, C = shape
    if kind == "col":
        return ref.at[pl.ds(hc * (R // 2), R // 2), pl.ds(s * (C // 4), C // 4)]
    return ref.at[pl.ds(s * (R // 4) + hc * (R // 8), R // 8), :]


def _ag_shard_half(ref, kind, shape, hc):
    R, C = shape
    return ref.at[pl.ds(hc * (R // 2), R // 2), :] if kind == "col" else ref.at[pl.ds(hc * (R // 8), R // 8), :]


def _grad_half(ref, kind, shape, hc):
    R, C = shape
    return ref.at[pl.ds(hc * (R // 2), R // 2), :] if kind == "col" else ref.at[:, pl.ds(hc * (C // 2), C // 2)]


def _half_piece(ref, kind, shape, s):
    R, C = shape
    return ref.at[:, pl.ds(s * (C // 4), C // 4)] if kind == "col" else ref.at[pl.ds(s * (R // 4), R // 4), :]


def _place():
    x, y, c = lax.axis_index("x"), lax.axis_index("y"), lax.axis_index("c")
    chips = [(1 - x, y), (x, 1 - y), (1 - x, 1 - y)]
    return x, y, c, chips


def _rcopy(src, dst, ssem, rsem, dev):
    return pltpu.make_async_remote_copy(src_ref=src, dst_ref=dst, send_sem=ssem, recv_sem=rsem,
                                        device_id=dev, device_id_type=MESH)


def _dma_sems(n):
    return pltpu.SemaphoreType.DMA((n,))


def _x_gather_ici(shards, ws):
    n = len(ws)
    specs = [BIG[w] for w in ws]

    def place():
        x, y, c, chips = _place()
        return c, chips, 2 * x + y

    def sends(sh, full, sc):
        c, chips, me_s = place()
        return [_rcopy(_ag_shard_half(sh[i], kind, shape, c), _ag_region(full[i], kind, shape, me_s, c),
                       sc[0].at[3 * i + j], sc[1].at[3 * i + j], (cx, cy, c))
                for i, (kind, shape) in enumerate(specs) for j, (cx, cy) in enumerate(chips)]

    def start(sh, full, sc):
        for i in range(n):
            pltpu.make_async_copy(sh[i], sc[4 + i], sc[2].at[i]).start()
        for cp in sends(sh, full, sc):
            cp.start()

    def finish(sh, full, sc):
        c, chips, me_s = place()
        stores = []
        for i, (kind, shape) in enumerate(specs):
            pltpu.make_async_copy(sh[i], sc[4 + i], sc[2].at[i]).wait()
            st = pltpu.make_async_copy(sc[4 + i], _own_region(full[i], kind, shape, me_s), sc[3].at[i])
            st.start()
            stores.append(st)
        for i, (kind, shape) in enumerate(specs):
            for j, (cx, cy) in enumerate(chips):
                reg = _ag_region(full[i], kind, shape, 2 * cx + cy, c)
                _rcopy(reg, reg, sc[0].at[3 * i + j], sc[1].at[3 * i + j], (cx, cy, c)).wait_recv()
        for cp in sends(sh, full, sc):
            cp.wait_send()
        for st in stores:
            st.wait()

    return _Exchange(
        shards, [jax.ShapeDtypeStruct(shape, BF16) for _, shape in specs], {},
        [_dma_sems(3 * n), _dma_sems(3 * n), _dma_sems(n), _dma_sems(n)]
        + [pltpu.VMEM(_shard_shape(k, s), BF16) for k, s in specs], start, finish)


def _x_gather_d2d(wholes, ws):
    specs = [BIG[w] for w in ws]
    n = len(ws)

    def copies(full, sc, mine):
        x, y, c, chips = _place()
        hc = c if mine else 1 - c
        return [_rcopy(reg, reg, sc[0].at[3 * i + j], sc[1].at[3 * i + j], (x, y, 1 - c))
                for i, (kind, shape) in enumerate(specs) for j, (cx, cy) in enumerate(chips)
                for reg in [_ag_region(full[i], kind, shape, 2 * cx + cy, hc)]]

    def start(_, full, sc):
        for cp in copies(full, sc, True):
            cp.start()

    def finish(_, full, sc):
        for cp in copies(full, sc, False):
            cp.wait_recv()
        for cp in copies(full, sc, True):
            cp.wait_send()

    return _Exchange(wholes, [jax.ShapeDtypeStruct(shape, BF16) for _, shape in specs], {i: i for i in range(n)},
                     [_dma_sems(3 * n), _dma_sems(3 * n)], start, finish)


def _x_grads_sibling(grads, ws):
    specs = [BIG[w] for w in ws]
    n = len(ws)

    def copies(g, got, sc):
        x, y, c, _ = _place()
        return [_rcopy(_grad_half(g[i], kind, shape, 1 - c), got[i], sc[0].at[i], sc[1].at[i], (x, y, 1 - c))
                for i, (kind, shape) in enumerate(specs)]

    def start(g, got, sc):
        for cp in copies(g, got, sc):
            cp.start()

    def finish(g, got, sc):
        for cp in copies(g, got, sc):
            cp.wait_recv()
        for cp in copies(g, got, sc):
            cp.wait_send()

    return _Exchange(grads, [jax.ShapeDtypeStruct(_half_shape(k, s), F32) for k, s in specs], {},
                     [_dma_sems(n), _dma_sems(n)], start, finish)


def _x_grads_chips(sums_bf, ws):
    specs = [BIG[w] for w in ws]
    n = len(ws)

    def copies(s16, got, sc):
        x, y, c, chips = _place()
        return [_rcopy(_half_piece(s16[i], kind, shape, 2 * cx + cy), got[i].at[j],
                       sc[0].at[3 * i + j], sc[1].at[3 * i + j], (cx, cy, c))
                for i, (kind, shape) in enumerate(specs) for j, (cx, cy) in enumerate(chips)]

    def start(s16, got, sc):
        for cp in copies(s16, got, sc):
            cp.start()

    def finish(s16, got, sc):
        for cp in copies(s16, got, sc):
            cp.wait_recv()
        for cp in copies(s16, got, sc):
            cp.wait_send()

    return _Exchange(sums_bf, [jax.ShapeDtypeStruct((3,) + _piece_shape(k, s), BF16) for k, s in specs], {},
                     [_dma_sems(3 * n), _dma_sems(3 * n)], start, finish)


def _shard_half(ref, kind, shape, hc):
    sr, sc = _shard_shape(kind, shape)
    return ref.at[pl.ds(hc * (sr // 2), sr // 2), :] if kind == "col" else ref.at[:, pl.ds(hc * (sc // 2), sc // 2)]


def _x_grads_share(shard_grads, ws):
    specs = [BIG[w] for w in ws]
    n = len(ws)

    def copies(g, sc, mine):
        x, y, c, _ = _place()
        hc = c if mine else 1 - c
        return [_rcopy(part, part, sc[0].at[i], sc[1].at[i], (x, y, 1 - c))
                for i, (kind, shape) in enumerate(specs) for part in [_shard_half(g[i], kind, shape, hc)]]

    def start(_, g, sc):
        for cp in copies(g, sc, True):
            cp.start()

    def finish(_, g, sc):
        for cp in copies(g, sc, False):
            cp.wait_recv()
        for cp in copies(g, sc, True):
            cp.wait_send()

    return _Exchange(shard_grads, [jax.ShapeDtypeStruct(_shard_shape(k, s), F32) for k, s in specs],
                     {i: i for i in range(n)}, [_dma_sems(n), _dma_sems(n)], start, finish)


ADD_ROWS = 256


def _add_halves(place, g, got, kind, name):
    R, C = g.shape
    hr, hcols = _half_shape(kind, (R, C))
    steps = hr // ADD_ROWS

    def body(p_ref, g_ref, b_ref, s_ref, sb_ref):
        s = g_ref[...] + b_ref[...]
        s_ref[...] = s
        sb_ref[...] = s.astype(BF16)

    if kind == "col":
        g_spec = pl.BlockSpec((ADD_ROWS, C), lambda i, p: (p[0] * steps + i, 0))
    else:
        g_spec = pl.BlockSpec((ADD_ROWS, hcols), lambda i, p: (i, p[0]))
    spec = pl.BlockSpec((ADD_ROWS, hcols), lambda i, p: (i, 0))
    return pl.pallas_call(
        body, name=name,
        grid_spec=pltpu.PrefetchScalarGridSpec(num_scalar_prefetch=1, grid=(steps,), in_specs=[g_spec, spec],
                                               out_specs=[spec, spec]),
        out_shape=[jax.ShapeDtypeStruct((hr, hcols), F32), jax.ShapeDtypeStruct((hr, hcols), BF16)],
        compiler_params=_cparams(("parallel",)),
    )(place, g, got)


def _add_pieces(place, half, got, kind, shape, name):
    pr, pc = _piece_shape(kind, shape)
    steps = pr // ADD_ROWS

    def body(p_ref, m_ref, g_ref, o_ref):
        acc = m_ref[...]
        for j in range(3):
            acc = acc + g_ref[j].astype(F32)
        o_ref[...] = acc

    if kind == "col":
        m_spec = pl.BlockSpec((ADD_ROWS, pc), lambda i, p: (i, p[1]))
        o_spec = pl.BlockSpec((ADD_ROWS, pc), lambda i, p: (p[0] * steps + i, 0))
    else:
        m_spec = pl.BlockSpec((ADD_ROWS, pc), lambda i, p: (p[1] * steps + i, 0))
        o_spec = pl.BlockSpec((ADD_ROWS, pc), lambda i, p: (i, p[0]))
    return pl.pallas_call(
        body, name=name,
        grid_spec=pltpu.PrefetchScalarGridSpec(
            num_scalar_prefetch=1, grid=(steps,),
            in_specs=[m_spec, pl.BlockSpec((3, ADD_ROWS, pc), lambda i, p: (0, i, 0))], out_specs=o_spec),
        out_shape=jax.ShapeDtypeStruct(_shard_shape(kind, shape), F32),
        compiler_params=_cparams(("parallel",)),
    )(place, half, got)


SMALL_ROWS = 1024 + 8 * 8


def _x_small_all_reduce(p):
    def parts(p_ref, sc):
        slots, ssem, rsem = sc[0], sc[2], sc[3]
        x, y, c = lax.axis_index("x"), lax.axis_index("y"), lax.axis_index("c")
        me = 4 * x + 2 * y + c
        out = []
        for r in range(1, 8):
            bx, by, bc = (r >> 2) & 1, (r >> 1) & 1, r & 1
            tgt = (1 - x if bx else x, 1 - y if by else y, 1 - c if bc else c)
            send = _rcopy(p_ref, slots.at[me], ssem.at[r - 1], rsem.at[r - 1], tgt)
            src = 4 * tgt[0] + 2 * tgt[1] + tgt[2]
            recv = _rcopy(p_ref, slots.at[src], ssem.at[r - 1], rsem.at[r - 1], tgt)
            out.append((send, recv))
        return me, out

    def start(ins, outs, sc):
        me, cps = parts(ins[0], sc)
        pltpu.make_async_copy(ins[0], sc[0].at[me], sc[4].at[0]).start()
        for send, _ in cps:
            send.start()

    def finish(ins, outs, sc):
        me, cps = parts(ins[0], sc)
        pltpu.make_async_copy(ins[0], sc[0].at[me], sc[4].at[0]).wait()
        for _, recv in cps:
            recv.wait_recv()
        acc = sc[0][0]
        for d in range(1, 8):
            acc = acc + sc[0][d]
        sc[1][...] = acc
        back = pltpu.make_async_copy(sc[1], outs[0], sc[4].at[1])
        back.start()
        for send, _ in cps:
            send.wait_send()
        back.wait()

    return _Exchange([p], [jax.ShapeDtypeStruct((SMALL_ROWS, CH), F32)], {},
                     [pltpu.VMEM((8, SMALL_ROWS, CH), F32), pltpu.VMEM((SMALL_ROWS, CH), F32), _dma_sems(7), _dma_sems(7),
                      _dma_sems(2)], start, finish)


def _rope_tables(positions):
    inv_freq = 500000.0 ** (-jnp.arange(0, 2 * ROPE_HALF, 2, dtype=F32) / (2 * ROPE_HALF))
    head = jnp.concatenate([inv_freq, inv_freq, jnp.zeros((HD - 2 * ROPE_HALF,), F32)])
    lane_freq = jnp.concatenate([head, head])
    ang = positions.astype(F32)[:, None] * lane_freq[None, :]
    cos, sin = jnp.cos(ang), jnp.sin(ang)
    first = (jnp.arange(CH) % HD) < ROPE_HALF
    return cos, jnp.where(first[None, :], -sin, 0.0), jnp.where(first[None, :], 0.0, sin)


BIG_NAMES = ("w_in", "w_a", "w_b", "w_o", "w_ff_in", "w_ff_out")
SMALL_NAMES = ("w_spatial", "ln_v_gain", "ln_v_bias", "b_spatial", "sinks", "norm_mix_pre", "norm_mix_post",
               "norm_ff_pre", "norm_ff_post")
WEIGHT_ORDER = ("w_in", "ln_v_gain", "ln_v_bias", "w_spatial", "b_spatial", "sinks", "w_a", "w_b", "w_o",
                "norm_mix_pre", "norm_mix_post", "w_ff_in", "w_ff_out", "norm_ff_pre", "norm_ff_post")


def _pack_small(d):
    parts = []
    for n in SMALL_NAMES:
        flat = d[n].reshape(-1)
        pad = (-flat.shape[0]) % (8 * CH)
        parts.append(jnp.pad(flat, (0, pad)).reshape(-1, CH))
    return jnp.concatenate(parts, axis=0)


def _unpack_small(p, like):
    out, row = {}, 0
    for n in SMALL_NAMES:
        size = like[n].size
        rows = -(-size // (8 * CH)) * 8
        out[n] = p[row:row + rows].reshape(-1)[:size].reshape(like[n].shape)
        row += rows
    return out


def kernel(x, positions, w_in, ln_v_gain, ln_v_bias, w_spatial, b_spatial, sinks, w_a, w_b, w_o, norm_mix_pre, norm_mix_post, w_ff_in, w_ff_out, norm_ff_pre, norm_ff_post, loss_target, m_w_in, m_ln_v_gain, m_ln_v_bias, m_w_spatial, m_b_spatial, m_sinks, m_w_a, m_w_b, m_w_o, m_norm_mix_pre, m_norm_mix_post, m_w_ff_in, m_w_ff_out, m_norm_ff_pre, m_norm_ff_post, v_w_in, v_ln_v_gain, v_ln_v_bias, v_w_spatial, v_b_spatial, v_sinks, v_w_a, v_w_b, v_w_o, v_norm_mix_pre, v_norm_mix_post, v_w_ff_in, v_w_ff_out, v_norm_ff_pre, v_norm_ff_post):
    w = dict(w_in=w_in, ln_v_gain=ln_v_gain, ln_v_bias=ln_v_bias, w_spatial=w_spatial, b_spatial=b_spatial, sinks=sinks,
             w_a=w_a, w_b=w_b, w_o=w_o, norm_mix_pre=norm_mix_pre, norm_mix_post=norm_mix_post, w_ff_in=w_ff_in,
             w_ff_out=w_ff_out, norm_ff_pre=norm_ff_pre, norm_ff_post=norm_ff_post)
    m = dict(w_in=m_w_in, ln_v_gain=m_ln_v_gain, ln_v_bias=m_ln_v_bias, w_spatial=m_w_spatial, b_spatial=m_b_spatial,
             sinks=m_sinks, w_a=m_w_a, w_b=m_w_b, w_o=m_w_o, norm_mix_pre=m_norm_mix_pre, norm_mix_post=m_norm_mix_post,
             w_ff_in=m_w_ff_in, w_ff_out=m_w_ff_out, norm_ff_pre=m_norm_ff_pre, norm_ff_post=m_norm_ff_post)
    v = dict(w_in=v_w_in, ln_v_gain=v_ln_v_gain, ln_v_bias=v_ln_v_bias, w_spatial=v_w_spatial, b_spatial=v_b_spatial,
             sinks=v_sinks, w_a=v_w_a, w_b=v_w_b, w_o=v_w_o, norm_mix_pre=v_norm_mix_pre, norm_mix_post=v_norm_mix_post,
             w_ff_in=v_w_ff_in, w_ff_out=v_w_ff_out, norm_ff_pre=v_norm_ff_pre, norm_ff_post=v_norm_ff_post)

    FIRST, REST = (0,), tuple(range(1, NBIG))
    shards = [w[n][0].astype(BF16) for n in BIG_NAMES]
    place = jnp.stack([lax.axis_index("c"), 2 * lax.axis_index("x") + lax.axis_index("y")]).astype(jnp.int32)
    xs, target = x[0], loss_target[0]
    wtm = min(1024, xs.shape[0])
    g1, g2, g3, g4 = norm_mix_pre, norm_mix_post, norm_ff_pre, norm_ff_post
    w_sp, snk = w_spatial[0], sinks[0]
    rc, rs1, rs2 = _rope_tables(positions[0])
    bfull = jnp.repeat(b_spatial[0].T, CH, axis=1)

    def reduce_tail(ws, grads, got):
        sums = [_add_halves(place, grads[i], got[i], BIG[k][0], name="grad_add_sibling_" + BIG_NAMES[k])
                for i, k in enumerate(ws)]
        return sums, _x_grads_chips([s[1] for s in sums], ws)

    def reduce_end(ws, sums, pieces):
        return [_add_pieces(place, sums[i][0], pieces[i], *BIG[k], name="grad_add_chips_" + BIG_NAMES[k])
                for i, k in enumerate(ws)]

    w_in_b = _run(_x_gather_d2d(_run(_x_gather_ici(shards[:1], FIRST), "gather_w_in_ici"), FIRST), "gather_w_in_d2d")[0]
    (h, u, vs, q, k, va, ga, gb), rest = _inproj(xs, g1, w_in_b, rc, rs1, rs2, tm=256, comm=_x_gather_ici(shards[1:], REST))
    a, rest = _sgu_fwd(u, vs, ln_v_gain, ln_v_bias, w_sp, bfull, tm=512, comm=_x_gather_d2d(rest, REST))
    w_a_b, w_b_b, w_o_b, w_ff_in_b, w_ff_out_b = rest
    att = _attn_fwd(q, k, va, snk)
    pa, pb, merged, mix, x1 = _merge_fwd(a, att, ga, gb, xs, w_a_b, w_b_b, w_o_b, g2, tm=256)
    hf, f2, dff, df1, dx1, lsum, dg3, dg4 = _ffn(x1, target, w_ff_in_b, w_ff_out_b, g3, g4, tm=256)
    loss = lax.psum(0.5 * jnp.sum(lsum) / D, ("x", "y", "c"))

    dw_ff_out, _ = _wgrad(f2, dff, tn=512, tm=512, name="wgrad_ff_out")
    dw_ff_in, _ = _wgrad(hf, df1, tn=1024, tm=wtm, name="wgrad_ff_in")
    dmix, dao, dbo, dga, dgb, da, datt, dg2 = _merge_bwd(dx1, mix, ga, gb, pa, pb, w_a_b, w_b_b, w_o_b, g2, tm=256)
    dw_o, _ = _wgrad(merged, dmix, tn=1024, tm=wtm, name="wgrad_o")
    dw_a, _ = _wgrad(a, dao, tn=1024, tm=wtm, name="wgrad_a")
    dw_b, _ = _wgrad(att, dbo, tn=1024, tm=wtm, name="wgrad_b")
    grads_rest = [dw_a, dw_b, dw_o, dw_ff_in, dw_ff_out]
    (du, dvs, dws, dbs, dlg, dlb), got_rest = _sgu_bwd(
        u, vs, da, ln_v_gain, ln_v_bias, w_sp, bfull, tm=512, comm=_x_grads_sibling(grads_rest, REST))
    sums_rest, to_chips = reduce_tail(REST, grads_rest, got_rest)
    (dq, dk, dva, dsk), pieces_rest = _attn_bwd(q, k, va, datt, snk, rc, rs1, rs2, comm=to_chips)
    partial_rest = reduce_end(REST, sums_rest, pieces_rest)
    (dx, dproj, dg1), _ = _inproj_bwd([du, dvs, dq, dk, dva, dga, dgb], xs, dx1, g1, w_in_b, tm=256)
    small = dict(ln_v_gain=dlg, ln_v_bias=dlb, w_spatial=dws, b_spatial=dbs, sinks=dsk[:, :NQ],
                 norm_mix_pre=dg1, norm_mix_post=dg2, norm_ff_pre=dg3, norm_ff_post=dg4)
    dw_in, (gs, *shard_rest) = _wgrad(
        h, dproj, tn=IN_W // 4, tm=wtm, name="wgrad_in",
        comm=_both(_x_small_all_reduce(_pack_small(small)), _x_grads_share(partial_rest, REST)))
    got_in = _run(_x_grads_sibling([dw_in], FIRST), "grads_in_to_sibling")
    sums_in, to_chips = reduce_tail(FIRST, [dw_in], got_in)
    partial_in = reduce_end(FIRST, sums_in, _run(to_chips, "grads_in_to_chips"))
    shard_grads = list(_run(_x_grads_share(partial_in, FIRST), "grads_in_share")) + list(shard_rest)

    grad, delta, new_m, new_v = {}, {}, {}, {}
    for i, n in enumerate(BIG_NAMES):
        g = shard_grads[i]
        d_, m_, v_ = _adamw(w[n][0], g, m[n][0], v[n][0], tr=256, name="adamw_" + n)
        grad[n], delta[n], new_m[n], new_v[n] = g[None], d_[None], m_[None], v_[None]

    ds, ms, vs = _adamw(_pack_small(w), gs, _pack_small(m), _pack_small(v), tr=SMALL_ROWS // 4, name="adamw_small")
    for packed, dst in ((gs, grad), (ds, delta), (ms, new_m), (vs, new_v)):
        dst.update(_unpack_small(packed, w))

    outs = [loss, dx[None]]
    for group in (grad, delta, new_m, new_v):
        outs.extend(group[n] for n in WEIGHT_ORDER)
    return tuple(outs)
```

```python
import functools

import jax
import jax.numpy as jnp
from jax import lax
from jax.experimental import pallas as pl
from jax.experimental.pallas import tpu as pltpu

F32 = jnp.float32
BF16 = jnp.bfloat16

D = 1024
CH = 128
NG = 8
HD = 64
NQ = 16
NKV = 4
KVW = NKV * HD
DFF = 4 * D
EPS = 1e-6
IN_W = 5632
SEG = (0, 1024, 2048, 3072, 3328, 3584, 4608, 5632)
ROPE_HALF = 8
Q_SCALE = HD ** -0.5

LR, B1, B2, AEPS, WD, STEP = 0.001, 0.9, 0.999, 1e-08, 0.01, 10

VMEM_LIMIT = 60 * 1024 * 1024
MESH = pl.DeviceIdType.MESH

_GELU_C0 = 0.7978845608028654
_GELU_C1 = 0.044715


def _cparams(sem=None):
    kw = dict(vmem_limit_bytes=VMEM_LIMIT)
    if sem is not None:
        kw["dimension_semantics"] = sem
    return pltpu.CompilerParams(**kw)


def _resident(shape):
    nd = len(shape)
    return pl.BlockSpec(shape, lambda *_: (0,) * nd, pipeline_mode=pl.Buffered(1))


def _const(shape):
    nd = len(shape)
    return pl.BlockSpec(shape, lambda *_: (0,) * nd)


def _rows(tm, w):
    return pl.BlockSpec((tm, w), lambda i: (i, 0))


class _Exchange:
    def __init__(self, ins, outs, aliases, scratch, start, finish):
        self.ins, self.outs, self.aliases, self.scratch = list(ins), list(outs), dict(aliases), list(scratch)
        self.start, self.finish = start, finish


def _both(a, b):
    na, ma, sa = len(a.ins), len(a.outs), len(a.scratch)

    def start(ci, co, cs):
        a.start(ci[:na], co[:ma], cs[:sa])
        b.start(ci[na:], co[ma:], cs[sa:])

    def finish(ci, co, cs):
        a.finish(ci[:na], co[:ma], cs[:sa])
        b.finish(ci[na:], co[ma:], cs[sa:])

    aliases = {**a.aliases, **{na + i: ma + j for i, j in b.aliases.items()}}
    return _Exchange(a.ins + b.ins, a.outs + b.outs, aliases, a.scratch + b.scratch, start, finish)


def _call(body, args, *, name, grid, in_specs, out_specs, out_shape, scratch_shapes=(), sem=None, comm=None):
    single = not isinstance(out_shape, (list, tuple))
    out_shape = [out_shape] if single else list(out_shape)
    out_specs = [out_specs] if single else list(out_specs)
    if comm is None:
        res = pl.pallas_call(body, name=name, grid=grid, in_specs=list(in_specs), out_specs=out_specs,
                             out_shape=out_shape, scratch_shapes=list(scratch_shapes),
                             compiler_params=_cparams(sem))(*args)
        return (res[0] if single else res), []
    n_in, n_out, n_scr = len(args), len(out_shape), len(scratch_shapes)
    nci, nco = len(comm.ins), len(comm.outs)
    steps = 1
    for g in grid:
        steps *= g

    def hosted(*refs):
        a, ci = refs[:n_in], refs[n_in:n_in + nci]
        o, co = refs[n_in + nci:n_in + nci + n_out], refs[n_in + nci + n_out:n_in + nci + n_out + nco]
        rest = refs[n_in + nci + n_out + nco:]
        scr, cs = rest[:n_scr], rest[n_scr:]
        step = pl.program_id(0)
        for d in range(1, len(grid)):
            step = step * grid[d] + pl.program_id(d)

        @pl.when(step == 0)
        def _():
            comm.start(ci, co, cs)

        body(*a, *o, *scr)

        @pl.when(step == steps - 1)
        def _():
            comm.finish(ci, co, cs)

    res = pl.pallas_call(
        hosted, name=name, grid=grid, in_specs=list(in_specs) + [ANY] * nci, out_specs=out_specs + [ANY] * nco,
        out_shape=out_shape + comm.outs, scratch_shapes=list(scratch_shapes) + comm.scratch,
        input_output_aliases={n_in + i: n_out + j for i, j in comm.aliases.items()},
        compiler_params=_cparams(("arbitrary",) * len(grid)),
    )(*args, *comm.ins)
    own = res[:n_out]
    return (own[0] if single else own), list(res[n_out:])


def _run(comm, name):
    nci = len(comm.ins)

    def body(*refs):
        ci, co, cs = refs[:nci], refs[nci:nci + len(comm.outs)], refs[nci + len(comm.outs):]
        comm.start(ci, co, cs)
        comm.finish(ci, co, cs)

    return pl.pallas_call(
        body, name=name, in_specs=[ANY] * nci, out_specs=[ANY] * len(comm.outs), out_shape=comm.outs,
        scratch_shapes=comm.scratch, input_output_aliases=comm.aliases,
        compiler_params=pltpu.CompilerParams(vmem_limit_bytes=VMEM_LIMIT),
    )(*comm.ins)


def _gelu(x):
    x2 = x * x
    t = jnp.tanh(x * (_GELU_C0 + (_GELU_C0 * _GELU_C1) * x2))
    hx = 0.5 * x
    return hx + hx * t, (t, x2, hx)


def _gelu_grad(parts):
    t, x2, hx = parts
    return (0.5 + 0.5 * t) + hx * (1.0 - t * t) * (_GELU_C0 + (3.0 * _GELU_C0 * _GELU_C1) * x2)


def _sigmoid(x):
    return 1.0 / (1.0 + jnp.exp(-x))


def _rms_hat(x):
    r = lax.rsqrt(jnp.mean(x * x, axis=-1, keepdims=True) + EPS)
    return x * r, r


def _rms_bwd(xhat, r, g, dout):
    dg = jnp.sum(dout * xhat, axis=0, keepdims=True)
    dy = dout * g
    dx = r * (dy - xhat * jnp.mean(dy * xhat, axis=-1, keepdims=True))
    return dx, dg


def _dot(a, b):
    return jnp.dot(a, b, preferred_element_type=F32)


def _dot_nt(a, b):
    return lax.dot_general(a, b, (((1,), (1,)), ((), ())), preferred_element_type=F32)


def _dot_tn(a, b):
    return lax.dot_general(a, b, (((0,), (0,)), ((), ())), preferred_element_type=F32)


def _rope(blk, c, s1, s2):
    return blk * c + pltpu.roll(blk, CH - ROPE_HALF, 1) * s1 + pltpu.roll(blk, ROPE_HALF, 1) * s2


def _rope_t(blk, c, s1, s2):
    return blk * c + pltpu.roll(blk * s1, ROPE_HALF, 1) + pltpu.roll(blk * s2, CH - ROPE_HALF, 1)


def _inproj(x, g1, w_in, rc, rs1, rs2, tm, comm=None):
    T = x.shape[0]

    def body(x_ref, g_ref, w_ref, c_ref, s1_ref, s2_ref,
             h_ref, u_ref, v_ref, q_ref, k_ref, va_ref, ga_ref, gb_ref):
        xhat, _ = _rms_hat(x_ref[...])
        h = (xhat * g_ref[...]).astype(BF16)
        h_ref[...] = h
        u_ref[...] = _dot(h, w_ref[:, SEG[0]:SEG[1]])
        v_ref[...] = _dot(h, w_ref[:, SEG[1]:SEG[2]])
        c, s1, s2 = c_ref[...], s1_ref[...], s2_ref[...]
        q = _dot(h, w_ref[:, SEG[2]:SEG[3]])
        for p in range(D // CH):
            blk = _rope(q[:, CH * p:CH * (p + 1)], c, s1, s2) * Q_SCALE
            q_ref[:, CH * p:CH * (p + 1)] = blk.astype(BF16)
        k = _dot(h, w_ref[:, SEG[3]:SEG[4]])
        for p in range(KVW // CH):
            k_ref[:, CH * p:CH * (p + 1)] = _rope(k[:, CH * p:CH * (p + 1)], c, s1, s2).astype(BF16)
        va_ref[...] = _dot(h, w_ref[:, SEG[4]:SEG[5]]).astype(BF16)
        ga_ref[...] = _dot(h, w_ref[:, SEG[5]:SEG[6]]).astype(BF16)
        gb_ref[...] = _dot(h, w_ref[:, SEG[6]:SEG[7]]).astype(BF16)

    sd = jax.ShapeDtypeStruct
    return _call(
        body, (x, g1, w_in, rc, rs1, rs2), name="inproj_fwd", grid=(T // tm,),
        in_specs=[_rows(tm, D), _const((1, D)), _resident((D, IN_W)), _rows(tm, CH), _rows(tm, CH), _rows(tm, CH)],
        out_specs=[_rows(tm, D), _rows(tm, D), _rows(tm, D), _rows(tm, D), _rows(tm, KVW), _rows(tm, KVW),
                   _rows(tm, D), _rows(tm, D)],
        out_shape=[sd((T, D), BF16), sd((T, D), F32), sd((T, D), F32), sd((T, D), BF16), sd((T, KVW), BF16),
                   sd((T, KVW), BF16), sd((T, D), BF16), sd((T, D), BF16)],
        sem=("parallel",), comm=comm)


def _sgu_common(u, vs, lng, lnb, ws_ref, bfull):
    nc = u.shape[0] // CH
    ug, tu = _gelu(u)
    vg, tv = _gelu(vs)
    mu = jnp.mean(vg, axis=-1, keepdims=True)
    xc = vg - mu
    rstd = lax.rsqrt(jnp.mean(xc * xc, axis=-1, keepdims=True) + EPS)
    vhat = xc * rstd
    vnb = (vhat * lng + lnb).astype(BF16)
    tri = lax.broadcasted_iota(jnp.int32, (CH, CH), 0) >= lax.broadcasted_iota(jnp.int32, (CH, CH), 1)
    wts, rhss, mixed = [], [], []
    for g in range(NG):
        wt = jnp.where(tri, ws_ref[g], 0.0).astype(BF16)
        rhs = jnp.concatenate([vnb[CH * c:CH * (c + 1), CH * g:CH * (g + 1)] for c in range(nc)], axis=1)
        mix = _dot(wt, rhs)
        wts.append(wt)
        rhss.append(rhs)
        mixed.append([mix[:, CH * c:CH * (c + 1)] + bfull[:, CH * g:CH * (g + 1)] for c in range(nc)])
    return nc, ug, tu, tv, rstd, vhat, tri, wts, rhss, mixed


def _sgu_fwd(u, vs, lng, lnb, ws, bfull, tm, comm=None):
    T = u.shape[0]

    def body(u_ref, v_ref, lng_ref, lnb_ref, ws_ref, bf_ref, a_ref):
        nc, ug, _, _, _, _, _, _, _, mixed = _sgu_common(
            u_ref[...], v_ref[...], lng_ref[...], lnb_ref[...], ws_ref, bf_ref[...])
        mixed_all = jnp.concatenate(
            [jnp.concatenate([mixed[g][c] for g in range(NG)], axis=1) for c in range(nc)], axis=0)
        a_ref[...] = (ug * mixed_all).astype(BF16)

    return _call(
        body, (u, vs, lng, lnb, ws, bfull), name="sgu_fwd", grid=(T // tm,),
        in_specs=[_rows(tm, D), _rows(tm, D), _const((1, D)), _const((1, D)), _const((NG, CH, CH)), _const((CH, D))],
        out_specs=_rows(tm, D), out_shape=jax.ShapeDtypeStruct((T, D), BF16), sem=("parallel",), comm=comm)


def _sgu_bwd(u, vs, da, lng, lnb, ws, bfull, tm, comm=None):
    T = u.shape[0]
    nsteps = T // tm

    def body(u_ref, v_ref, da_ref, lng_ref, lnb_ref, ws_ref, bf_ref,
             du_ref, dv_ref, dws_ref, dbs_ref, dlg_ref, dlb_ref, db_ref):
        i = pl.program_id(0)
        u, vs, da, lng = u_ref[...], v_ref[...], da_ref[...], lng_ref[...]
        nc, ug, tu, tv, rstd, vhat, tri, wts, rhss, mixed = _sgu_common(u, vs, lng, lnb_ref[...], ws_ref, bf_ref[...])

        @pl.when(i == 0)
        def _():
            dws_ref[...] = jnp.zeros_like(dws_ref)
            db_ref[...] = jnp.zeros_like(db_ref)
            dlg_ref[...] = jnp.zeros_like(dlg_ref)
            dlb_ref[...] = jnp.zeros_like(dlb_ref)

        mixed_all = jnp.concatenate(
            [jnp.concatenate([mixed[g][c] for g in range(NG)], axis=1) for c in range(nc)], axis=0)
        du_ref[...] = (da * mixed_all * _gelu_grad(tu)).astype(BF16)
        dmixed = da * ug
        dvn_cols = []
        for g in range(NG):
            dmix = [dmixed[CH * c:CH * (c + 1), CH * g:CH * (g + 1)] for c in range(nc)]
            db_ref[:, CH * g:CH * (g + 1)] += functools.reduce(lambda a, b: a + b, dmix)
            dm = jnp.concatenate(dmix, axis=1).astype(BF16)
            dws_ref[g] += _dot_nt(dm, rhss[g])
            dvn_cols.append(_dot_tn(wts[g], dm))
        dvn = jnp.concatenate(
            [jnp.concatenate([dvn_cols[g][:, CH * c:CH * (c + 1)] for g in range(NG)], axis=1) for c in range(nc)],
            axis=0)
        dlg_ref[...] += jnp.sum(dvn * vhat, axis=0, keepdims=True)
        dlb_ref[...] += jnp.sum(dvn, axis=0, keepdims=True)
        dvh = dvn * lng
        dvg = rstd * (dvh - jnp.mean(dvh, axis=-1, keepdims=True)
                      - vhat * jnp.mean(dvh * vhat, axis=-1, keepdims=True))
        dv_ref[...] = (dvg * _gelu_grad(tv)).astype(BF16)

        @pl.when(i == nsteps - 1)
        def _():
            for g in range(NG):
                dws_ref[g] = jnp.where(tri, dws_ref[g], 0.0)
                dbs_ref[g:g + 1, :] = jnp.sum(db_ref[:, CH * g:CH * (g + 1)].T, axis=0, keepdims=True)

    sd = jax.ShapeDtypeStruct
    return _call(
        body, (u, vs, da, lng, lnb, ws, bfull), name="sgu_bwd", grid=(nsteps,),
        in_specs=[_rows(tm, D), _rows(tm, D), _rows(tm, D), _const((1, D)), _const((1, D)), _const((NG, CH, CH)),
                  _const((CH, D))],
        out_specs=[_rows(tm, D), _rows(tm, D), _const((NG, CH, CH)), _const((NG, CH)), _const((1, D)), _const((1, D))],
        out_shape=[sd((T, D), BF16), sd((T, D), BF16), sd((NG, CH, CH), F32), sd((NG, CH), F32), sd((1, D), F32),
                   sd((1, D), F32)],
        scratch_shapes=[pltpu.VMEM((CH, D), F32)], sem=("arbitrary",), comm=comm)


def _pair_layout(prev, cur, grp):
    j, half = grp // 2, grp % 2
    blk = jnp.concatenate([prev[:, CH * j:CH * (j + 1)], cur[:, CH * j:CH * (j + 1)]], axis=0).astype(F32)
    lo = lax.broadcasted_iota(jnp.int32, blk.shape, 1) < HD
    rolled = pltpu.roll(blk, HD, 1)
    even = jnp.where(lo, blk if half == 0 else rolled, 0.0)
    odd = jnp.where(lo, 0.0, rolled if half == 0 else blk)
    return jnp.concatenate([even, odd], axis=0).astype(BF16)


def _attn_mask(n):
    qi = lax.broadcasted_iota(jnp.int32, (CH, 2 * CH), 0)
    kc = lax.broadcasted_iota(jnp.int32, (CH, 2 * CH), 1)
    ok = (kc > qi) & (kc <= qi + CH) & ((kc >= CH) | (n > 0))
    return jnp.concatenate([ok, ok], axis=1)


def _softmax_sink(s, sink):
    m = jnp.maximum(jnp.max(s, axis=-1, keepdims=True), sink)
    p = jnp.exp(s - m)
    ps = jnp.exp(sink - m)
    inv = 1.0 / (jnp.sum(p, axis=-1, keepdims=True) + ps)
    return p * inv, ps * inv


def _attn_fwd(q, k, va, sinks):
    T = q.shape[0]
    nb = T // CH

    def body(sk_ref, q_ref, kp_ref, kc_ref, vp_ref, vc_ref, o_ref):
        n = pl.program_id(0)
        mask = _attn_mask(n)
        kp, kc, vp, vc = kp_ref[...], kc_ref[...], vp_ref[...], vc_ref[...]
        kks = [_pair_layout(kp, kc, grp) for grp in range(NKV)]
        vvs = [_pair_layout(vp, vc, grp) for grp in range(NKV)]
        npairs = D // CH

        def scores(p):
            return _dot_nt(q_ref[:, CH * p:CH * (p + 1)], kks[p // 2])

        ahead = 3
        outs, probs = [], []
        pending = [scores(p) for p in range(ahead)]
        for p in range(npairs):
            s = jnp.where(mask, pending.pop(0), -1e30)
            if p + ahead < npairs:
                pending.append(scores(p + ahead))
            pe, _ = _softmax_sink(s[:, :2 * CH], sk_ref[2 * p])
            po, _ = _softmax_sink(s[:, 2 * CH:], sk_ref[2 * p + 1])
            probs.append(jnp.concatenate([pe, po], axis=1).astype(BF16))
            if p >= 1:
                outs.append(_dot(probs[p - 1], vvs[(p - 1) // 2]))
        outs.append(_dot(probs[-1], vvs[-1]))
        o_ref[...] = jnp.concatenate(outs, axis=1).astype(BF16)

    prev = lambda n: (jnp.maximum(n - 1, 0), 0)
    cur = lambda n: (n, 0)
    return pl.pallas_call(
        body, name="attn_fwd", grid=(nb,),
        in_specs=[pl.BlockSpec(memory_space=pltpu.SMEM), pl.BlockSpec((CH, D), cur),
                  pl.BlockSpec((CH, KVW), prev), pl.BlockSpec((CH, KVW), cur),
                  pl.BlockSpec((CH, KVW), prev), pl.BlockSpec((CH, KVW), cur)],
        out_specs=pl.BlockSpec((CH, D), cur), out_shape=jax.ShapeDtypeStruct((T, D), BF16),
        compiler_params=_cparams(("parallel",)),
    )(sinks, q, k, k, va, va)


def _attn_bwd(q, k, va, datt, sinks, rc, rs1, rs2, comm=None):
    T = q.shape[0]
    nb = T // CH

    def body(sk_ref, q_ref, kp_ref, kc_ref, vp_ref, vc_ref, do_ref, cq_ref, s1q_ref, s2q_ref, ck_ref, s1k_ref, s2k_ref,
             dq_ref, dk_ref, dv_ref, dsk_ref, kcar, vcar):
        n = pl.program_id(0)

        @pl.when(n == 0)
        def _():
            kcar[...] = jnp.zeros_like(kcar)
            vcar[...] = jnp.zeros_like(vcar)
            dsk_ref[...] = jnp.zeros_like(dsk_ref)

        def flush(kprev, vprev):
            ck, s1k, s2k = ck_ref[...], s1k_ref[...], s2k_ref[...]
            for j in range(KVW // CH):
                sl = slice(CH * j, CH * (j + 1))
                dk_ref[:, sl] = _rope_t(kcar[:, sl] + kprev[:, sl], ck, s1k, s2k).astype(BF16)
                dv_ref[:, sl] = (vcar[:, sl] + vprev[:, sl]).astype(BF16)

        @pl.when(n < nb)
        def _():
            mask = _attn_mask(n)
            kp, kc, vp, vc = kp_ref[...], kc_ref[...], vp_ref[...], vc_ref[...]
            cq, s1q, s2q = cq_ref[...], s1q_ref[...], s2q_ref[...]
            lane = lax.broadcasted_iota(jnp.int32, (1, CH), 1)
            dsk = jnp.zeros((1, CH), F32)
            npairs = D // CH
            kks = [_pair_layout(kp, kc, grp) for grp in range(NKV)]
            vvs = [_pair_layout(vp, vc, grp) for grp in range(NKV)]
            qs = [q_ref[:, CH * p:CH * (p + 1)] for p in range(npairs)]
            dos = [do_ref[:, CH * p:CH * (p + 1)].astype(BF16) for p in range(npairs)]

            def first(p):
                return _dot_nt(qs[p], kks[p // 2]), _dot_nt(dos[p], vvs[p // 2])

            def last(p, ds, pb):
                return (_rope_t(_dot(ds, kks[p // 2]), cq, s1q, s2q) * Q_SCALE, _dot_tn(qs[p], ds), _dot_tn(dos[p], pb))

            ahead = 2
            pending = [first(p) for p in range(ahead)]
            mids, ends = [], []
            for p in range(npairs):
                s, dp = pending.pop(0)
                s = jnp.where(mask, s, -1e30)
                if p + ahead < npairs:
                    pending.append(first(p + ahead))
                ds_parts, p_parts = [], []
                for par in range(2):
                    sl = slice(2 * CH * par, 2 * CH * (par + 1))
                    pr, psink = _softmax_sink(s[:, sl], sk_ref[2 * p + par])
                    delta = jnp.sum(pr * dp[:, sl], axis=-1, keepdims=True)
                    ds_parts.append(pr * (dp[:, sl] - delta))
                    p_parts.append(pr)
                    tot = -jnp.sum(psink * delta, axis=0, keepdims=True)
                    dsk = dsk + jnp.where(lane == 2 * p + par, tot, 0.0)
                mids.append((jnp.concatenate(ds_parts, axis=1).astype(BF16), jnp.concatenate(p_parts, axis=1).astype(BF16)))
                if p >= 1:
                    ends.append(last(p - 1, *mids[p - 1]))
            ends.append(last(npairs - 1, *mids[-1]))
            dq_cols = [e[0] for e in ends]
            def fold(i):
                rows = []
                for grp in range(NKV):
                    acc = ends[2 * grp][i] + ends[2 * grp + 1][i]
                    rows.append(acc[:HD, :2 * CH] + acc[HD:, 2 * CH:])
                return jnp.concatenate(rows, axis=0).T

            dkf, dvf = fold(1), fold(2)
            dq_ref[...] = jnp.concatenate(dq_cols, axis=1).astype(BF16)
            dsk_ref[...] += dsk
            flush(dkf[:CH], dvf[:CH])
            kcar[...] = dkf[CH:]
            vcar[...] = dvf[CH:]

        @pl.when(n == nb)
        def _():
            z = jnp.zeros((CH, KVW), F32)
            flush(z, z)

    last = nb - 1
    cur = lambda n: (jnp.minimum(n, last), 0)
    prev = lambda n: (jnp.clip(n - 1, 0, last), 0)
    sd = jax.ShapeDtypeStruct
    return _call(
        body, (sinks, q, k, k, va, va, datt, rc, rs1, rs2, rc, rs1, rs2), name="attn_bwd", grid=(nb + 1,),
        in_specs=[pl.BlockSpec(memory_space=pltpu.SMEM), pl.BlockSpec((CH, D), cur),
                  pl.BlockSpec((CH, KVW), prev), pl.BlockSpec((CH, KVW), cur),
                  pl.BlockSpec((CH, KVW), prev), pl.BlockSpec((CH, KVW), cur),
                  pl.BlockSpec((CH, D), cur),
                  pl.BlockSpec((CH, CH), cur), pl.BlockSpec((CH, CH), cur), pl.BlockSpec((CH, CH), cur),
                  pl.BlockSpec((CH, CH), prev), pl.BlockSpec((CH, CH), prev), pl.BlockSpec((CH, CH), prev)],
        out_specs=[pl.BlockSpec((CH, D), cur), pl.BlockSpec((CH, KVW), prev), pl.BlockSpec((CH, KVW), prev),
                   _const((1, CH))],
        out_shape=[sd((T, D), BF16), sd((T, KVW), BF16), sd((T, KVW), BF16), sd((1, CH), F32)],
        scratch_shapes=[pltpu.VMEM((CH, KVW), F32), pltpu.VMEM((CH, KVW), F32)], sem=("arbitrary",), comm=comm)


def _merge_fwd(a, att, ga, gb, x, w_a, w_b, w_o, g2, tm):
    T = x.shape[0]

    def body(a_ref, att_ref, ga_ref, gb_ref, x_ref, wa_ref, wb_ref, wo_ref, g_ref,
             pa_ref, pb_ref, mg_ref, mix_ref, x1_ref):
        pa = _dot(a_ref[...], wa_ref[...])
        pb = _dot(att_ref[...], wb_ref[...])
        pa_ref[...] = pa.astype(BF16)
        pb_ref[...] = pb.astype(BF16)
        merged = (_sigmoid(ga_ref[...].astype(F32)) * pa + _sigmoid(gb_ref[...].astype(F32)) * pb).astype(BF16)
        mg_ref[...] = merged
        mix = _dot(merged, wo_ref[...])
        mix_ref[...] = mix
        mhat, _ = _rms_hat(mix)
        x1_ref[...] = x_ref[...] + mhat * g_ref[...]

    sd = jax.ShapeDtypeStruct
    return pl.pallas_call(
        body, name="merge_fwd", grid=(T // tm,),
        in_specs=[_rows(tm, D)] * 5 + [_resident((D, D))] * 3 + [_const((1, D))],
        out_specs=[_rows(tm, D)] * 5,
        out_shape=[sd((T, D), BF16), sd((T, D), BF16), sd((T, D), BF16), sd((T, D), F32), sd((T, D), F32)],
        compiler_params=_cparams(("parallel",)),
    )(a, att, ga, gb, x, w_a, w_b, w_o, g2)


def _merge_bwd(dx1, mix, ga, gb, pa, pb, w_a, w_b, w_o, g2, tm):
    T = dx1.shape[0]

    def body(dx1_ref, mix_ref, ga_ref, gb_ref, pa_ref, pb_ref, wa_ref, wb_ref, wo_ref, g_ref,
             dmix_ref, dao_ref, dbo_ref, dga_ref, dgb_ref, da_ref, datt_ref, dg_ref):
        @pl.when(pl.program_id(0) == 0)
        def _():
            dg_ref[...] = jnp.zeros_like(dg_ref)

        mhat, r = _rms_hat(mix_ref[...])
        dmix, dg = _rms_bwd(mhat, r, g_ref[...], dx1_ref[...])
        dg_ref[...] += dg
        dmix = dmix.astype(BF16)
        dmix_ref[...] = dmix
        dmerged = _dot_nt(dmix, wo_ref[...])
        sa = _sigmoid(ga_ref[...].astype(F32))
        sb = _sigmoid(gb_ref[...].astype(F32))
        dao = (dmerged * sa).astype(BF16)
        dbo = (dmerged * sb).astype(BF16)
        dao_ref[...] = dao
        dbo_ref[...] = dbo
        dga_ref[...] = (dmerged * pa_ref[...].astype(F32) * (sa * (1.0 - sa))).astype(BF16)
        dgb_ref[...] = (dmerged * pb_ref[...].astype(F32) * (sb * (1.0 - sb))).astype(BF16)
        da_ref[...] = _dot_nt(dao, wa_ref[...])
        datt_ref[...] = _dot_nt(dbo, wb_ref[...]).astype(BF16)

    sd = jax.ShapeDtypeStruct
    return pl.pallas_call(
        body, name="merge_bwd", grid=(T // tm,),
        in_specs=[_rows(tm, D)] * 6 + [_resident((D, D))] * 3 + [_const((1, D))],
        out_specs=[_rows(tm, D)] * 7 + [_const((1, D))],
        out_shape=[sd((T, D), BF16)] * 5 + [sd((T, D), F32), sd((T, D), BF16), sd((1, D), F32)],
        compiler_params=_cparams(("arbitrary",)),
    )(dx1, mix, ga, gb, pa, pb, w_a, w_b, w_o, g2)


def _ffn(x1, target, w1, w2, g3, g4, tm):
    T = x1.shape[0]

    def body(x_ref, t_ref, w1_ref, w2_ref, g3_ref, g4_ref,
             hf_ref, f2_ref, dff_ref, df1_ref, dx_ref, ls_ref, dg3_ref, dg4_ref):
        @pl.when(pl.program_id(0) == 0)
        def _():
            ls_ref[...] = jnp.zeros_like(ls_ref)
            dg3_ref[...] = jnp.zeros_like(dg3_ref)
            dg4_ref[...] = jnp.zeros_like(dg4_ref)

        x = x_ref[...]
        g3, g4 = g3_ref[...], g4_ref[...]
        xhat, r3 = _rms_hat(x)
        hf = (xhat * g3).astype(BF16)
        hf_ref[...] = hf
        rl = jnp.maximum(_dot(hf, w1_ref[...]), 0.0)
        f2 = (rl * rl).astype(BF16)
        f2_ref[...] = f2
        fhat, r4 = _rms_hat(_dot(f2, w2_ref[...]))
        err = x + fhat * g4 - t_ref[...]
        ls_ref[...] += jnp.sum(err * err, axis=0, keepdims=True)
        dy = err * (1.0 / D)
        dff, dg4 = _rms_bwd(fhat, r4, g4, dy)
        dg4_ref[...] += dg4
        dff = dff.astype(BF16)
        dff_ref[...] = dff
        df1 = (_dot_nt(dff, w2_ref[...]) * (2.0 * rl)).astype(BF16)
        df1_ref[...] = df1
        dxn, dg3 = _rms_bwd(xhat, r3, g3, _dot_nt(df1, w1_ref[...]))
        dg3_ref[...] += dg3
        dx_ref[...] = dy + dxn

    sd = jax.ShapeDtypeStruct
    return pl.pallas_call(
        body, name="ffn_fwd_bwd", grid=(T // tm,),
        in_specs=[_rows(tm, D), _rows(tm, D), _resident((D, DFF)), _resident((DFF, D)), _const((1, D)), _const((1, D))],
        out_specs=[_rows(tm, D), _rows(tm, DFF), _rows(tm, D), _rows(tm, DFF), _rows(tm, D), _const((1, D)),
                   _const((1, D)), _const((1, D))],
        out_shape=[sd((T, D), BF16), sd((T, DFF), BF16), sd((T, D), BF16), sd((T, DFF), BF16), sd((T, D), F32),
                   sd((1, D), F32), sd((1, D), F32), sd((1, D), F32)],
        compiler_params=_cparams(("arbitrary",)),
    )(x1, target, w1, w2, g3, g4)


def _inproj_bwd(parts, x, dx1, g1, w_in, tm, comm=None):
    T = x.shape[0]
    widths = [p.shape[1] for p in parts]
    offs = [sum(widths[:i]) for i in range(len(widths) + 1)]
    assert offs[-1] == IN_W

    def body(*refs):
        n = len(parts)
        prefs = refs[:n]
        x_ref, dx1_ref, g_ref, w_ref, dx_ref, dp_ref, dg_ref = refs[n:]

        @pl.when(pl.program_id(0) == 0)
        def _():
            dg_ref[...] = jnp.zeros_like(dg_ref)

        dh = None
        for i in range(n):
            blk = prefs[i][...]
            dp_ref[:, offs[i]:offs[i + 1]] = blk
            t = _dot_nt(blk, w_ref[:, offs[i]:offs[i + 1]])
            dh = t if dh is None else dh + t
        xhat, r = _rms_hat(x_ref[...])
        dxn, dg = _rms_bwd(xhat, r, g_ref[...], dh)
        dg_ref[...] += dg
        dx_ref[...] = dx1_ref[...] + dxn

    sd = jax.ShapeDtypeStruct
    return _call(
        body, (*parts, x, dx1, g1, w_in), name="inproj_bwd", grid=(T // tm,),
        in_specs=[_rows(tm, w) for w in widths] + [_rows(tm, D), _rows(tm, D), _const((1, D)), _resident((D, IN_W))],
        out_specs=[_rows(tm, D), _rows(tm, IN_W), _const((1, D))],
        out_shape=[sd((T, D), F32), sd((T, IN_W), BF16), sd((1, D), F32)], sem=("arbitrary",), comm=comm)


def _wgrad(a, g, tn, tm, name, comm=None):
    T, K = a.shape
    N = g.shape[1]

    def body(a_ref, g_ref, o_ref):
        @pl.when(pl.program_id(1) == 0)
        def _():
            o_ref[...] = jnp.zeros_like(o_ref)

        o_ref[...] += _dot_tn(a_ref[...], g_ref[...])

    return _call(
        body, (a, g), name=name, grid=(N // tn, T // tm),
        in_specs=[pl.BlockSpec((tm, K), lambda j, t: (t, 0)), pl.BlockSpec((tm, tn), lambda j, t: (t, j))],
        out_specs=pl.BlockSpec((K, tn), lambda j, t: (0, j)),
        out_shape=jax.ShapeDtypeStruct((K, N), F32), sem=("parallel", "arbitrary"), comm=comm)


def _adamw(w, g, m, v, tr, name):
    R, C = w.shape
    bc1 = 1.0 / (1.0 - B1 ** STEP)
    bc2 = 1.0 / (1.0 - B2 ** STEP)

    def body(w_ref, g_ref, m_ref, v_ref, d_ref, nm_ref, nv_ref):
        g = g_ref[...]
        m = B1 * m_ref[...] + (1.0 - B1) * g
        v = B2 * v_ref[...] + (1.0 - B2) * (g * g)
        nm_ref[...] = m
        nv_ref[...] = v
        d_ref[...] = -LR * ((m * bc1) / (jnp.sqrt(v * bc2) + AEPS) + WD * w_ref[...])

    spec = pl.BlockSpec((tr, C), lambda i: (i, 0))
    return pl.pallas_call(
        body, name=name, grid=(R // tr,), in_specs=[spec] * 4, out_specs=[spec] * 3,
        out_shape=[jax.ShapeDtypeStruct((R, C), F32)] * 3,
        compiler_params=_cparams(("parallel",)),
    )(w, g, m, v)


BIG = (("col", (D, IN_W)), ("row", (D, D)), ("row", (D, D)), ("row", (D, D)), ("col", (D, DFF)), ("row", (DFF, D)))
NBIG = len(BIG)
ANY = pl.BlockSpec(memory_space=pl.ANY)


def _shard_shape(kind, shape):
    R, C = shape
    return (R, C // 4) if kind == "col" else (R // 4, C)


def _half_shape(kind, shape):
    R, C = shape
    return (R // 2, C) if kind == "col" else (R, C // 2)


def _piece_shape(kind, shape):
    R, C = shape
    return (R // 2, C // 4) if kind == "col" else (R // 4, C // 2)


def _own_region(ref, kind, shape, s):
    R, C = shape
    return ref.at[:, pl.ds(s * (C // 4), C // 4)] if kind == "col" else ref.at[pl.ds(s * (R // 4), R // 4), :]


def _ag_region(ref, kind, shape, s, hc):
    R, C = shape
    if kind == "col":
        return ref.at[pl.ds(hc * (R // 2), R // 2), pl.ds(s * (C // 4), C // 4)]
    return ref.at[pl.ds(s * (R // 4) + hc * (R // 8), R // 8), :]


def _ag_shard_half(ref, kind, shape, hc):
    R, C = shape
    return ref.at[pl.ds(hc * (R // 2), R // 2), :] if kind == "col" else ref.at[pl.ds(hc * (R // 8), R // 8), :]


def _grad_half(ref, kind, shape, hc):
    R, C = shape
    return ref.at[pl.ds(hc * (R // 2), R // 2), :] if kind == "col" else ref.at[:, pl.ds(hc * (C // 2), C // 2)]


def _half_piece(ref, kind, shape, s):
    R, C = shape
    return ref.at[:, pl.ds(s * (C // 4), C // 4)] if kind == "col" else ref.at[pl.ds(s * (R // 4), R // 4), :]


def _place():
    x, y, c = lax.axis_index("x"), lax.axis_index("y"), lax.axis_index("c")
    chips = [(1 - x, y), (x, 1 - y), (1 - x, 1 - y)]
    return x, y, c, chips


def _rcopy(src, dst, ssem, rsem, dev):
    return pltpu.make_async_remote_copy(src_ref=src, dst_ref=dst, send_sem=ssem, recv_sem=rsem,
                                        device_id=dev, device_id_type=MESH)


def _dma_sems(n):
    return pltpu.SemaphoreType.DMA((n,))


def _x_gather_ici(shards, ws):
    n = len(ws)
    specs = [BIG[w] for w in ws]

    def place():
        x, y, c, chips = _place()
        return c, chips, 2 * x + y

    def sends(sh, full, sc):
        c, chips, me_s = place()
        return [_rcopy(_ag_shard_half(sh[i], kind, shape, c), _ag_region(full[i], kind, shape, me_s, c),
                       sc[0].at[3 * i + j], sc[1].at[3 * i + j], (cx, cy, c))
                for i, (kind, shape) in enumerate(specs) for j, (cx, cy) in enumerate(chips)]

    def start(sh, full, sc):
        for i in range(n):
            pltpu.make_async_copy(sh[i], sc[4 + i], sc[2].at[i]).start()
        for cp in sends(sh, full, sc):
            cp.start()

    def finish(sh, full, sc):
        c, chips, me_s = place()
        stores = []
        for i, (kind, shape) in enumerate(specs):
            pltpu.make_async_copy(sh[i], sc[4 + i], sc[2].at[i]).wait()
            st = pltpu.make_async_copy(sc[4 + i], _own_region(full[i], kind, shape, me_s), sc[3].at[i])
            st.start()
            stores.append(st)
        for i, (kind, shape) in enumerate(specs):
            for j, (cx, cy) in enumerate(chips):
                reg = _ag_region(full[i], kind, shape, 2 * cx + cy, c)
                _rcopy(reg, reg, sc[0].at[3 * i + j], sc[1].at[3 * i + j], (cx, cy, c)).wait_recv()
        for cp in sends(sh, full, sc):
            cp.wait_send()
        for st in stores:
            st.wait()

    return _Exchange(
        shards, [jax.ShapeDtypeStruct(shape, BF16) for _, shape in specs], {},
        [_dma_sems(3 * n), _dma_sems(3 * n), _dma_sems(n), _dma_sems(n)]
        + [pltpu.VMEM(_shard_shape(k, s), BF16) for k, s in specs], start, finish)


def _x_gather_d2d(wholes, ws):
    specs = [BIG[w] for w in ws]
    n = len(ws)

    def copies(full, sc, mine):
        x, y, c, chips = _place()
        hc = c if mine else 1 - c
        return [_rcopy(reg, reg, sc[0].at[3 * i + j], sc[1].at[3 * i + j], (x, y, 1 - c))
                for i, (kind, shape) in enumerate(specs) for j, (cx, cy) in enumerate(chips)
                for reg in [_ag_region(full[i], kind, shape, 2 * cx + cy, hc)]]

    def start(_, full, sc):
        for cp in copies(full, sc, True):
            cp.start()

    def finish(_, full, sc):
        for cp in copies(full, sc, False):
            cp.wait_recv()
        for cp in copies(full, sc, True):
            cp.wait_send()

    return _Exchange(wholes, [jax.ShapeDtypeStruct(shape, BF16) for _, shape in specs], {i: i for i in range(n)},
                     [_dma_sems(3 * n), _dma_sems(3 * n)], start, finish)


def _x_grads_sibling(grads, ws):
    specs = [BIG[w] for w in ws]
    n = len(ws)

    def copies(g, got, sc):
        x, y, c, _ = _place()
        return [_rcopy(_grad_half(g[i], kind, shape, 1 - c), got[i], sc[0].at[i], sc[1].at[i], (x, y, 1 - c))
                for i, (kind, shape) in enumerate(specs)]

    def start(g, got, sc):
        for cp in copies(g, got, sc):
            cp.start()

    def finish(g, got, sc):
        for cp in copies(g, got, sc):
            cp.wait_recv()
        for cp in copies(g, got, sc):
            cp.wait_send()

    return _Exchange(grads, [jax.ShapeDtypeStruct(_half_shape(k, s), F32) for k, s in specs], {},
                     [_dma_sems(n), _dma_sems(n)], start, finish)


def _x_grads_chips(sums_bf, ws):
    specs = [BIG[w] for w in ws]
    n = len(ws)

    def copies(s16, got, sc):
        x, y, c, chips = _place()
        return [_rcopy(_half_piece(s16[i], kind, shape, 2 * cx + cy), got[i].at[j],
                       sc[0].at[3 * i + j], sc[1].at[3 * i + j], (cx, cy, c))
                for i, (kind, shape) in enumerate(specs) for j, (cx, cy) in enumerate(chips)]

    def start(s16, got, sc):
        for cp in copies(s16, got, sc):
            cp.start()

    def finish(s16, got, sc):
        for cp in copies(s16, got, sc):
            cp.wait_recv()
        for cp in copies(s16, got, sc):
            cp.wait_send()

    return _Exchange(sums_bf, [jax.ShapeDtypeStruct((3,) + _piece_shape(k, s), BF16) for k, s in specs], {},
                     [_dma_sems(3 * n), _dma_sems(3 * n)], start, finish)


def _shard_half(ref, kind, shape, hc):
    sr, sc = _shard_shape(kind, shape)
    return ref.at[pl.ds(hc * (sr // 2), sr // 2), :] if kind == "col" else ref.at[:, pl.ds(hc * (sc // 2), sc // 2)]


def _x_grads_share(shard_grads, ws):
    specs = [BIG[w] for w in ws]
    n = len(ws)

    def copies(g, sc, mine):
        x, y, c, _ = _place()
        hc = c if mine else 1 - c
        return [_rcopy(part, part, sc[0].at[i], sc[1].at[i], (x, y, 1 - c))
                for i, (kind, shape) in enumerate(specs) for part in [_shard_half(g[i], kind, shape, hc)]]

    def start(_, g, sc):
        for cp in copies(g, sc, True):
            cp.start()

    def finish(_, g, sc):
        for cp in copies(g, sc, False):
            cp.wait_recv()
        for cp in copies(g, sc, True):
            cp.wait_send()

    return _Exchange(shard_grads, [jax.ShapeDtypeStruct(_shard_shape(k, s), F32) for k, s in specs],
                     {i: i for i in range(n)}, [_dma_sems(n), _dma_sems(n)], start, finish)


ADD_ROWS = 256


def _add_halves(place, g, got, kind, name):
    R, C = g.shape
    hr, hcols = _half_shape(kind, (R, C))
    steps = hr // ADD_ROWS

    def body(p_ref, g_ref, b_ref, s_ref, sb_ref):
        s = g_ref[...] + b_ref[...]
        s_ref[...] = s
        sb_ref[...] = s.astype(BF16)

    if kind == "col":
        g_spec = pl.BlockSpec((ADD_ROWS, C), lambda i, p: (p[0] * steps + i, 0))
    else:
        g_spec = pl.BlockSpec((ADD_ROWS, hcols), lambda i, p: (i, p[0]))
    spec = pl.BlockSpec((ADD_ROWS, hcols), lambda i, p: (i, 0))
    return pl.pallas_call(
        body, name=name,
        grid_spec=pltpu.PrefetchScalarGridSpec(num_scalar_prefetch=1, grid=(steps,), in_specs=[g_spec, spec],
                                               out_specs=[spec, spec]),
        out_shape=[jax.ShapeDtypeStruct((hr, hcols), F32), jax.ShapeDtypeStruct((hr, hcols), BF16)],
        compiler_params=_cparams(("parallel",)),
    )(place, g, got)


def _add_pieces(place, half, got, kind, shape, name):
    pr, pc = _piece_shape(kind, shape)
    steps = pr // ADD_ROWS

    def body(p_ref, m_ref, g_ref, o_ref):
        acc = m_ref[...]
        for j in range(3):
            acc = acc + g_ref[j].astype(F32)
        o_ref[...] = acc

    if kind == "col":
        m_spec = pl.BlockSpec((ADD_ROWS, pc), lambda i, p: (i, p[1]))
        o_spec = pl.BlockSpec((ADD_ROWS, pc), lambda i, p: (p[0] * steps + i, 0))
    else:
        m_spec = pl.BlockSpec((ADD_ROWS, pc), lambda i, p: (p[1] * steps + i, 0))
        o_spec = pl.BlockSpec((ADD_ROWS, pc), lambda i, p: (i, p[0]))
    return pl.pallas_call(
        body, name=name,
        grid_spec=pltpu.PrefetchScalarGridSpec(
            num_scalar_prefetch=1, grid=(steps,),
            in_specs=[m_spec, pl.BlockSpec((3, ADD_ROWS, pc), lambda i, p: (0, i, 0))], out_specs=o_spec),
        out_shape=jax.ShapeDtypeStruct(_shard_shape(kind, shape), F32),
        compiler_params=_cparams(("parallel",)),
    )(place, half, got)


SMALL_ROWS = 1024 + 8 * 8


def _x_small_all_reduce(p):
    def parts(p_ref, sc):
        slots, ssem, rsem = sc[0], sc[2], sc[3]
        x, y, c = lax.axis_index("x"), lax.axis_index("y"), lax.axis_index("c")
        me = 4 * x + 2 * y + c
        out = []
        for r in range(1, 8):
            bx, by, bc = (r >> 2) & 1, (r >> 1) & 1, r & 1
            tgt = (1 - x if bx else x, 1 - y if by else y, 1 - c if bc else c)
            send = _rcopy(p_ref, slots.at[me], ssem.at[r - 1], rsem.at[r - 1], tgt)
            src = 4 * tgt[0] + 2 * tgt[1] + tgt[2]
            recv = _rcopy(p_ref, slots.at[src], ssem.at[r - 1], rsem.at[r - 1], tgt)
            out.append((send, recv))
        return me, out

    def start(ins, outs, sc):
        me, cps = parts(ins[0], sc)
        pltpu.make_async_copy(ins[0], sc[0].at[me], sc[4].at[0]).start()
        for send, _ in cps:
            send.start()

    def finish(ins, outs, sc):
        me, cps = parts(ins[0], sc)
        pltpu.make_async_copy(ins[0], sc[0].at[me], sc[4].at[0]).wait()
        for _, recv in cps:
            recv.wait_recv()
        acc = sc[0][0]
        for d in range(1, 8):
            acc = acc + sc[0][d]
        sc[1][...] = acc
        back = pltpu.make_async_copy(sc[1], outs[0], sc[4].at[1])
        back.start()
        for send, _ in cps:
            send.wait_send()
        back.wait()

    return _Exchange([p], [jax.ShapeDtypeStruct((SMALL_ROWS, CH), F32)], {},
                     [pltpu.VMEM((8, SMALL_ROWS, CH), F32), pltpu.VMEM((SMALL_ROWS, CH), F32), _dma_sems(7), _dma_sems(7),
                      _dma_sems(2)], start, finish)


def _rope_tables(positions):
    inv_freq = 500000.0 ** (-jnp.arange(0, 2 * ROPE_HALF, 2, dtype=F32) / (2 * ROPE_HALF))
    head = jnp.concatenate([inv_freq, inv_freq, jnp.zeros((HD - 2 * ROPE_HALF,), F32)])
    lane_freq = jnp.concatenate([head, head])
    ang = positions.astype(F32)[:, None] * lane_freq[None, :]
    cos, sin = jnp.cos(ang), jnp.sin(ang)
    first = (jnp.arange(CH) % HD) < ROPE_HALF
    return cos, jnp.where(first[None, :], -sin, 0.0), jnp.where(first[None, :], 0.0, sin)


BIG_NAMES = ("w_in", "w_a", "w_b", "w_o", "w_ff_in", "w_ff_out")
SMALL_NAMES = ("w_spatial", "ln_v_gain", "ln_v_bias", "b_spatial", "sinks", "norm_mix_pre", "norm_mix_post",
               "norm_ff_pre", "norm_ff_post")
WEIGHT_ORDER = ("w_in", "ln_v_gain", "ln_v_bias", "w_spatial", "b_spatial", "sinks", "w_a", "w_b", "w_o",
                "norm_mix_pre", "norm_mix_post", "w_ff_in", "w_ff_out", "norm_ff_pre", "norm_ff_post")


def _pack_small(d):
    parts = []
    for n in SMALL_NAMES:
        flat = d[n].reshape(-1)
        pad = (-flat.shape[0]) % (8 * CH)
        parts.append(jnp.pad(flat, (0, pad)).reshape(-1, CH))
    return jnp.concatenate(parts, axis=0)


def _unpack_small(p, like):
    out, row = {}, 0
    for n in SMALL_NAMES:
        size = like[n].size
        rows = -(-size // (8 * CH)) * 8
        out[n] = p[row:row + rows].reshape(-1)[:size].reshape(like[n].shape)
        row += rows
    return out


def kernel(x, positions, w_in, ln_v_gain, ln_v_bias, w_spatial, b_spatial, sinks, w_a, w_b, w_o, norm_mix_pre, norm_mix_post, w_ff_in, w_ff_out, norm_ff_pre, norm_ff_post, loss_target, m_w_in, m_ln_v_gain, m_ln_v_bias, m_w_spatial, m_b_spatial, m_sinks, m_w_a, m_w_b, m_w_o, m_norm_mix_pre, m_norm_mix_post, m_w_ff_in, m_w_ff_out, m_norm_ff_pre, m_norm_ff_post, v_w_in, v_ln_v_gain, v_ln_v_bias, v_w_spatial, v_b_spatial, v_sinks, v_w_a, v_w_b, v_w_o, v_norm_mix_pre, v_norm_mix_post, v_w_ff_in, v_w_ff_out, v_norm_ff_pre, v_norm_ff_post):
    w = dict(w_in=w_in, ln_v_gain=ln_v_gain, ln_v_bias=ln_v_bias, w_spatial=w_spatial, b_spatial=b_spatial, sinks=sinks,
             w_a=w_a, w_b=w_b, w_o=w_o, norm_mix_pre=norm_mix_pre, norm_mix_post=norm_mix_post, w_ff_in=w_ff_in,
             w_ff_out=w_ff_out, norm_ff_pre=norm_ff_pre, norm_ff_post=norm_ff_post)
    m = dict(w_in=m_w_in, ln_v_gain=m_ln_v_gain, ln_v_bias=m_ln_v_bias, w_spatial=m_w_spatial, b_spatial=m_b_spatial,
             sinks=m_sinks, w_a=m_w_a, w_b=m_w_b, w_o=m_w_o, norm_mix_pre=m_norm_mix_pre, norm_mix_post=m_norm_mix_post,
             w_ff_in=m_w_ff_in, w_ff_out=m_w_ff_out, norm_ff_pre=m_norm_ff_pre, norm_ff_post=m_norm_ff_post)
    v = dict(w_in=v_w_in, ln_v_gain=v_ln_v_gain, ln_v_bias=v_ln_v_bias, w_spatial=v_w_spatial, b_spatial=v_b_spatial,
             sinks=v_sinks, w_a=v_w_a, w_b=v_w_b, w_o=v_w_o, norm_mix_pre=v_norm_mix_pre, norm_mix_post=v_norm_mix_post,
             w_ff_in=v_w_ff_in, w_ff_out=v_w_ff_out, norm_ff_pre=v_norm_ff_pre, norm_ff_post=v_norm_ff_post)

    FIRST, REST = (0,), tuple(range(1, NBIG))
    shards = [w[n][0].astype(BF16) for n in BIG_NAMES]
    place = jnp.stack([lax.axis_index("c"), 2 * lax.axis_index("x") + lax.axis_index("y")]).astype(jnp.int32)
    xs, target = x[0], loss_target[0]
    T = xs.shape[0]
    wtm, wtm2 = min(1024, T), min(2048, T)
    g1, g2, g3, g4 = norm_mix_pre, norm_mix_post, norm_ff_pre, norm_ff_post
    w_sp, snk = w_spatial[0], sinks[0]
    rc, rs1, rs2 = _rope_tables(positions[0])
    bfull = jnp.repeat(b_spatial[0].T, CH, axis=1)

    def reduce_tail(ws, grads, got):
        sums = [_add_halves(place, grads[i], got[i], BIG[k][0], name="grad_add_sibling_" + BIG_NAMES[k])
                for i, k in enumerate(ws)]
        return sums, _x_grads_chips([s[1] for s in sums], ws)

    def reduce_end(ws, sums, pieces):
        return [_add_pieces(place, sums[i][0], pieces[i], *BIG[k], name="grad_add_chips_" + BIG_NAMES[k])
                for i, k in enumerate(ws)]

    w_in_b = _run(_x_gather_d2d(_run(_x_gather_ici(shards[:1], FIRST), "gather_w_in_ici"), FIRST), "gather_w_in_d2d")[0]
    (h, u, vs, q, k, va, ga, gb), rest = _inproj(xs, g1, w_in_b, rc, rs1, rs2, tm=512, comm=_x_gather_ici(shards[1:], REST))
    a, rest = _sgu_fwd(u, vs, ln_v_gain, ln_v_bias, w_sp, bfull, tm=512, comm=_x_gather_d2d(rest, REST))
    w_a_b, w_b_b, w_o_b, w_ff_in_b, w_ff_out_b = rest
    att = _attn_fwd(q, k, va, snk)
    pa, pb, merged, mix, x1 = _merge_fwd(a, att, ga, gb, xs, w_a_b, w_b_b, w_o_b, g2, tm=512)
    hf, f2, dff, df1, dx1, lsum, dg3, dg4 = _ffn(x1, target, w_ff_in_b, w_ff_out_b, g3, g4, tm=256)
    loss = lax.psum(0.5 * jnp.sum(lsum) / D, ("x", "y", "c"))

    dw_ff_out, _ = _wgrad(f2, dff, tn=1024, tm=512, name="wgrad_ff_out")
    dw_ff_in, _ = _wgrad(hf, df1, tn=2048, tm=wtm2, name="wgrad_ff_in")
    dmix, dao, dbo, dga, dgb, da, datt, dg2 = _merge_bwd(dx1, mix, ga, gb, pa, pb, w_a_b, w_b_b, w_o_b, g2, tm=512)
    dw_o, _ = _wgrad(merged, dmix, tn=1024, tm=wtm2, name="wgrad_o")
    dw_a, _ = _wgrad(a, dao, tn=1024, tm=wtm2, name="wgrad_a")
    dw_b, _ = _wgrad(att, dbo, tn=1024, tm=wtm2, name="wgrad_b")
    grads_rest = [dw_a, dw_b, dw_o, dw_ff_in, dw_ff_out]
    (du, dvs, dws, dbs, dlg, dlb), got_rest = _sgu_bwd(
        u, vs, da, ln_v_gain, ln_v_bias, w_sp, bfull, tm=512, comm=_x_grads_sibling(grads_rest, REST))
    sums_rest, to_chips = reduce_tail(REST, grads_rest, got_rest)
    (dq, dk, dva, dsk), pieces_rest = _attn_bwd(q, k, va, datt, snk, rc, rs1, rs2, comm=to_chips)
    partial_rest = reduce_end(REST, sums_rest, pieces_rest)
    (dx, dproj, dg1), _ = _inproj_bwd([du, dvs, dq, dk, dva, dga, dgb], xs, dx1, g1, w_in_b, tm=512)
    small = dict(ln_v_gain=dlg, ln_v_bias=dlb, w_spatial=dws, b_spatial=dbs, sinks=dsk[:, :NQ],
                 norm_mix_pre=dg1, norm_mix_post=dg2, norm_ff_pre=dg3, norm_ff_post=dg4)
    dw_in, (gs, *shard_rest) = _wgrad(
        h, dproj, tn=IN_W // 2, tm=wtm, name="wgrad_in",
        comm=_both(_x_small_all_reduce(_pack_small(small)), _x_grads_share(partial_rest, REST)))
    got_in = _run(_x_grads_sibling([dw_in], FIRST), "grads_in_to_sibling")
    sums_in, to_chips = reduce_tail(FIRST, [dw_in], got_in)
    partial_in = reduce_end(FIRST, sums_in, _run(to_chips, "grads_in_to_chips"))
    shard_grads = list(_run(_x_grads_share(partial_in, FIRST), "grads_in_share")) + list(shard_rest)

    grad, delta, new_m, new_v = {}, {}, {}, {}
    for i, n in enumerate(BIG_NAMES):
        g = shard_grads[i]
        d_, m_, v_ = _adamw(w[n][0], g, m[n][0], v[n][0], tr=256, name="adamw_" + n)
        grad[n], delta[n], new_m[n], new_v[n] = g[None], d_[None], m_[None], v_[None]

    ds, ms, vs = _adamw(_pack_small(w), gs, _pack_small(m), _pack_small(v), tr=SMALL_ROWS // 4, name="adamw_small")
    for packed, dst in ((gs, grad), (ds, delta), (ms, new_m), (vs, new_v)):
        dst.update(_unpack_small(packed, w))

    outs = [loss, dx[None]]
    for group in (grad, delta, new_m, new_v):
        outs.extend(group[n] for n in WEIGHT_ORDER)
    return tuple(outs)
```

```python
import functools

import jax
import jax.numpy as jnp
from jax import lax
from jax.experimental import pallas as pl
from jax.experimental.pallas import tpu as pltpu

F32 = jnp.float32
BF16 = jnp.bfloat16

D = 1024
CH = 128
NG = 8
HD = 64
NQ = 16
NKV = 4
KVW = NKV * HD
DFF = 4 * D
EPS = 1e-6
IN_W = 5632
SEG = (0, 1024, 2048, 3072, 3328, 3584, 4608, 5632)
ROPE_HALF = 8
Q_SCALE = HD ** -0.5

LR, B1, B2, AEPS, WD, STEP = 0.001, 0.9, 0.999, 1e-08, 0.01, 10

VMEM_LIMIT = 60 * 1024 * 1024
MESH = pl.DeviceIdType.MESH

_GELU_C0 = 0.7978845608028654
_GELU_C1 = 0.044715


def _cparams(sem=None):
    kw = dict(vmem_limit_bytes=VMEM_LIMIT)
    if sem is not None:
        kw["dimension_semantics"] = sem
    return pltpu.CompilerParams(**kw)


def _resident(shape):
    nd = len(shape)
    return pl.BlockSpec(shape, lambda *_: (0,) * nd, pipeline_mode=pl.Buffered(1))


def _const(shape):
    nd = len(shape)
    return pl.BlockSpec(shape, lambda *_: (0,) * nd)


def _rows(tm, w):
    return pl.BlockSpec((tm, w), lambda i: (i, 0))


class _Exchange:
    def __init__(self, ins, outs, aliases, scratch, start, finish):
        self.ins, self.outs, self.aliases, self.scratch = list(ins), list(outs), dict(aliases), list(scratch)
        self.start, self.finish = start, finish


def _both(a, b):
    na, ma, sa = len(a.ins), len(a.outs), len(a.scratch)

    def start(ci, co, cs):
        a.start(ci[:na], co[:ma], cs[:sa])
        b.start(ci[na:], co[ma:], cs[sa:])

    def finish(ci, co, cs):
        a.finish(ci[:na], co[:ma], cs[:sa])
        b.finish(ci[na:], co[ma:], cs[sa:])

    aliases = {**a.aliases, **{na + i: ma + j for i, j in b.aliases.items()}}
    return _Exchange(a.ins + b.ins, a.outs + b.outs, aliases, a.scratch + b.scratch, start, finish)


def _call(body, args, *, name, grid, in_specs, out_specs, out_shape, scratch_shapes=(), sem=None, comm=None):
    single = not isinstance(out_shape, (list, tuple))
    out_shape = [out_shape] if single else list(out_shape)
    out_specs = [out_specs] if single else list(out_specs)
    if comm is None:
        res = pl.pallas_call(body, name=name, grid=grid, in_specs=list(in_specs), out_specs=out_specs,
                             out_shape=out_shape, scratch_shapes=list(scratch_shapes),
                             compiler_params=_cparams(sem))(*args)
        return (res[0] if single else res), []
    n_in, n_out, n_scr = len(args), len(out_shape), len(scratch_shapes)
    nci, nco = len(comm.ins), len(comm.outs)
    steps = 1
    for g in grid:
        steps *= g

    def hosted(*refs):
        a, ci = refs[:n_in], refs[n_in:n_in + nci]
        o, co = refs[n_in + nci:n_in + nci + n_out], refs[n_in + nci + n_out:n_in + nci + n_out + nco]
        rest = refs[n_in + nci + n_out + nco:]
        scr, cs = rest[:n_scr], rest[n_scr:]
        step = pl.program_id(0)
        for d in range(1, len(grid)):
            step = step * grid[d] + pl.program_id(d)

        @pl.when(step == 0)
        def _():
            comm.start(ci, co, cs)

        body(*a, *o, *scr)

        @pl.when(step == steps - 1)
        def _():
            comm.finish(ci, co, cs)

    res = pl.pallas_call(
        hosted, name=name, grid=grid, in_specs=list(in_specs) + [ANY] * nci, out_specs=out_specs + [ANY] * nco,
        out_shape=out_shape + comm.outs, scratch_shapes=list(scratch_shapes) + comm.scratch,
        input_output_aliases={n_in + i: n_out + j for i, j in comm.aliases.items()},
        compiler_params=_cparams(("arbitrary",) * len(grid)),
    )(*args, *comm.ins)
    own = res[:n_out]
    return (own[0] if single else own), list(res[n_out:])


def _run(comm, name):
    nci = len(comm.ins)

    def body(*refs):
        ci, co, cs = refs[:nci], refs[nci:nci + len(comm.outs)], refs[nci + len(comm.outs):]
        comm.start(ci, co, cs)
        comm.finish(ci, co, cs)

    return pl.pallas_call(
        body, name=name, in_specs=[ANY] * nci, out_specs=[ANY] * len(comm.outs), out_shape=comm.outs,
        scratch_shapes=comm.scratch, input_output_aliases=comm.aliases,
        compiler_params=pltpu.CompilerParams(vmem_limit_bytes=VMEM_LIMIT),
    )(*comm.ins)


def _gelu(x):
    x2 = x * x
    t = jnp.tanh(x * (_GELU_C0 + (_GELU_C0 * _GELU_C1) * x2))
    hx = 0.5 * x
    return hx + hx * t, (t, x2, hx)


def _gelu_grad(parts):
    t, x2, hx = parts
    return (0.5 + 0.5 * t) + hx * (1.0 - t * t) * (_GELU_C0 + (3.0 * _GELU_C0 * _GELU_C1) * x2)


def _sigmoid(x):
    return 1.0 / (1.0 + jnp.exp(-x))


def _rms_hat(x):
    r = lax.rsqrt(jnp.mean(x * x, axis=-1, keepdims=True) + EPS)
    return x * r, r


def _rms_bwd(xhat, r, g, dout):
    dg = jnp.sum(dout * xhat, axis=0, keepdims=True)
    dy = dout * g
    dx = r * (dy - xhat * jnp.mean(dy * xhat, axis=-1, keepdims=True))
    return dx, dg


def _dot(a, b):
    return jnp.dot(a, b, preferred_element_type=F32)


def _dot_nt(a, b):
    return lax.dot_general(a, b, (((1,), (1,)), ((), ())), preferred_element_type=F32)


def _dot_tn(a, b):
    return lax.dot_general(a, b, (((0,), (0,)), ((), ())), preferred_element_type=F32)


def _rope(blk, c, s1, s2):
    return blk * c + pltpu.roll(blk, CH - ROPE_HALF, 1) * s1 + pltpu.roll(blk, ROPE_HALF, 1) * s2


def _rope_t(blk, c, s1, s2):
    return blk * c + pltpu.roll(blk * s1, ROPE_HALF, 1) + pltpu.roll(blk * s2, CH - ROPE_HALF, 1)


def _inproj(x, g1, w_in, rc, rs1, rs2, tm, comm=None):
    T = x.shape[0]

    def body(x_ref, g_ref, w_ref, c_ref, s1_ref, s2_ref,
             h_ref, u_ref, v_ref, q_ref, k_ref, va_ref, ga_ref, gb_ref):
        xhat, _ = _rms_hat(x_ref[...])
        h = (xhat * g_ref[...]).astype(BF16)
        h_ref[...] = h
        u_ref[...] = _dot(h, w_ref[:, SEG[0]:SEG[1]])
        v_ref[...] = _dot(h, w_ref[:, SEG[1]:SEG[2]])
        c, s1, s2 = c_ref[...], s1_ref[...], s2_ref[...]
        q = _dot(h, w_ref[:, SEG[2]:SEG[3]])
        for p in range(D // CH):
            blk = _rope(q[:, CH * p:CH * (p + 1)], c, s1, s2) * Q_SCALE
            q_ref[:, CH * p:CH * (p + 1)] = blk.astype(BF16)
        k = _dot(h, w_ref[:, SEG[3]:SEG[4]])
        for p in range(KVW // CH):
            k_ref[:, CH * p:CH * (p + 1)] = _rope(k[:, CH * p:CH * (p + 1)], c, s1, s2).astype(BF16)
        va_ref[...] = _dot(h, w_ref[:, SEG[4]:SEG[5]]).astype(BF16)
        ga_ref[...] = _dot(h, w_ref[:, SEG[5]:SEG[6]]).astype(BF16)
        gb_ref[...] = _dot(h, w_ref[:, SEG[6]:SEG[7]]).astype(BF16)

    sd = jax.ShapeDtypeStruct
    return _call(
        body, (x, g1, w_in, rc, rs1, rs2), name="inproj_fwd", grid=(T // tm,),
        in_specs=[_rows(tm, D), _const((1, D)), _resident((D, IN_W)), _rows(tm, CH), _rows(tm, CH), _rows(tm, CH)],
        out_specs=[_rows(tm, D), _rows(tm, D), _rows(tm, D), _rows(tm, D), _rows(tm, KVW), _rows(tm, KVW),
                   _rows(tm, D), _rows(tm, D)],
        out_shape=[sd((T, D), BF16), sd((T, D), F32), sd((T, D), F32), sd((T, D), BF16), sd((T, KVW), BF16),
                   sd((T, KVW), BF16), sd((T, D), BF16), sd((T, D), BF16)],
        sem=("parallel",), comm=comm)


def _sgu_common(u, vs, lng, lnb, ws_ref, bfull):
    nc = u.shape[0] // CH
    ug, tu = _gelu(u)
    vg, tv = _gelu(vs)
    mu = jnp.mean(vg, axis=-1, keepdims=True)
    xc = vg - mu
    rstd = lax.rsqrt(jnp.mean(xc * xc, axis=-1, keepdims=True) + EPS)
    vhat = xc * rstd
    vnb = (vhat * lng + lnb).astype(BF16)
    tri = lax.broadcasted_iota(jnp.int32, (CH, CH), 0) >= lax.broadcasted_iota(jnp.int32, (CH, CH), 1)
    wts, rhss, mixed = [], [], []
    for g in range(NG):
        wt = jnp.where(tri, ws_ref[g], 0.0).astype(BF16)
        rhs = jnp.concatenate([vnb[CH * c:CH * (c + 1), CH * g:CH * (g + 1)] for c in range(nc)], axis=1)
        mix = _dot(wt, rhs)
        wts.append(wt)
        rhss.append(rhs)
        mixed.append([mix[:, CH * c:CH * (c + 1)] + bfull[:, CH * g:CH * (g + 1)] for c in range(nc)])
    return nc, ug, tu, tv, rstd, vhat, tri, wts, rhss, mixed


def _sgu_fwd(u, vs, lng, lnb, ws, bfull, tm, comm=None):
    T = u.shape[0]

    def body(u_ref, v_ref, lng_ref, lnb_ref, ws_ref, bf_ref, a_ref):
        nc, ug, _, _, _, _, _, _, _, mixed = _sgu_common(
            u_ref[...], v_ref[...], lng_ref[...], lnb_ref[...], ws_ref, bf_ref[...])
        mixed_all = jnp.concatenate(
            [jnp.concatenate([mixed[g][c] for g in range(NG)], axis=1) for c in range(nc)], axis=0)
        a_ref[...] = (ug * mixed_all).astype(BF16)

    return _call(
        body, (u, vs, lng, lnb, ws, bfull), name="sgu_fwd", grid=(T // tm,),
        in_specs=[_rows(tm, D), _rows(tm, D), _const((1, D)), _const((1, D)), _const((NG, CH, CH)), _const((CH, D))],
        out_specs=_rows(tm, D), out_shape=jax.ShapeDtypeStruct((T, D), BF16), sem=("parallel",), comm=comm)


def _sgu_bwd(u, vs, da, lng, lnb, ws, bfull, tm, comm=None):
    T = u.shape[0]
    nsteps = T // tm

    def body(u_ref, v_ref, da_ref, lng_ref, lnb_ref, ws_ref, bf_ref,
             du_ref, dv_ref, dws_ref, dbs_ref, dlg_ref, dlb_ref, db_ref):
        i = pl.program_id(0)
        u, vs, da, lng = u_ref[...], v_ref[...], da_ref[...], lng_ref[...]
        nc, ug, tu, tv, rstd, vhat, tri, wts, rhss, mixed = _sgu_common(u, vs, lng, lnb_ref[...], ws_ref, bf_ref[...])

        @pl.when(i == 0)
        def _():
            dws_ref[...] = jnp.zeros_like(dws_ref)
            db_ref[...] = jnp.zeros_like(db_ref)
            dlg_ref[...] = jnp.zeros_like(dlg_ref)
            dlb_ref[...] = jnp.zeros_like(dlb_ref)

        mixed_all = jnp.concatenate(
            [jnp.concatenate([mixed[g][c] for g in range(NG)], axis=1) for c in range(nc)], axis=0)
        du_ref[...] = (da * mixed_all * _gelu_grad(tu)).astype(BF16)
        dmixed = da * ug
        dvn_cols = []
        for g in range(NG):
            dmix = [dmixed[CH * c:CH * (c + 1), CH * g:CH * (g + 1)] for c in range(nc)]
            db_ref[:, CH * g:CH * (g + 1)] += functools.reduce(lambda a, b: a + b, dmix)
            dm = jnp.concatenate(dmix, axis=1).astype(BF16)
            dws_ref[g] += _dot_nt(dm, rhss[g])
            dvn_cols.append(_dot_tn(wts[g], dm))
        dvn = jnp.concatenate(
            [jnp.concatenate([dvn_cols[g][:, CH * c:CH * (c + 1)] for g in range(NG)], axis=1) for c in range(nc)],
            axis=0)
        dlg_ref[...] += jnp.sum(dvn * vhat, axis=0, keepdims=True)
        dlb_ref[...] += jnp.sum(dvn, axis=0, keepdims=True)
        dvh = dvn * lng
        dvg = rstd * (dvh - jnp.mean(dvh, axis=-1, keepdims=True)
                      - vhat * jnp.mean(dvh * vhat, axis=-1, keepdims=True))
        dv_ref[...] = (dvg * _gelu_grad(tv)).astype(BF16)

        @pl.when(i == nsteps - 1)
        def _():
            for g in range(NG):
                dws_ref[g] = jnp.where(tri, dws_ref[g], 0.0)
                dbs_ref[g:g + 1, :] = jnp.sum(db_ref[:, CH * g:CH * (g + 1)].T, axis=0, keepdims=True)

    sd = jax.ShapeDtypeStruct
    return _call(
        body, (u, vs, da, lng, lnb, ws, bfull), name="sgu_bwd", grid=(nsteps,),
        in_specs=[_rows(tm, D), _rows(tm, D), _rows(tm, D), _const((1, D)), _const((1, D)), _const((NG, CH, CH)),
                  _const((CH, D))],
        out_specs=[_rows(tm, D), _rows(tm, D), _const((NG, CH, CH)), _const((NG, CH)), _const((1, D)), _const((1, D))],
        out_shape=[sd((T, D), BF16), sd((T, D), BF16), sd((NG, CH, CH), F32), sd((NG, CH), F32), sd((1, D), F32),
                   sd((1, D), F32)],
        scratch_shapes=[pltpu.VMEM((CH, D), F32)], sem=("arbitrary",), comm=comm)


def _pair_layout(prev, cur, grp):
    j, half = grp // 2, grp % 2
    blk = jnp.concatenate([prev[:, CH * j:CH * (j + 1)], cur[:, CH * j:CH * (j + 1)]], axis=0).astype(F32)
    lo = lax.broadcasted_iota(jnp.int32, blk.shape, 1) < HD
    rolled = pltpu.roll(blk, HD, 1)
    even = jnp.where(lo, blk if half == 0 else rolled, 0.0)
    odd = jnp.where(lo, 0.0, rolled if half == 0 else blk)
    return jnp.concatenate([even, odd], axis=0).astype(BF16)


def _attn_mask(n):
    qi = lax.broadcasted_iota(jnp.int32, (CH, 2 * CH), 0)
    kc = lax.broadcasted_iota(jnp.int32, (CH, 2 * CH), 1)
    ok = (kc > qi) & (kc <= qi + CH) & ((kc >= CH) | (n > 0))
    return jnp.concatenate([ok, ok], axis=1)


def _softmax_sink(s, sink):
    m = jnp.maximum(jnp.max(s, axis=-1, keepdims=True), sink)
    p = jnp.exp(s - m)
    ps = jnp.exp(sink - m)
    inv = 1.0 / (jnp.sum(p, axis=-1, keepdims=True) + ps)
    return p * inv, ps * inv


def _attn_fwd(q, k, va, sinks, comm=None):
    T = q.shape[0]
    nb = T // CH

    def body(sk_ref, q_ref, kp_ref, kc_ref, vp_ref, vc_ref, o_ref):
        n = pl.program_id(0)
        mask = _attn_mask(n)
        kp, kc, vp, vc = kp_ref[...], kc_ref[...], vp_ref[...], vc_ref[...]
        kks = [_pair_layout(kp, kc, grp) for grp in range(NKV)]
        vvs = [_pair_layout(vp, vc, grp) for grp in range(NKV)]
        npairs = D // CH

        def scores(p):
            return _dot_nt(q_ref[:, CH * p:CH * (p + 1)], kks[p // 2])

        ahead = 3
        outs, probs = [], []
        pending = [scores(p) for p in range(ahead)]
        for p in range(npairs):
            s = jnp.where(mask, pending.pop(0), -1e30)
            if p + ahead < npairs:
                pending.append(scores(p + ahead))
            pe, _ = _softmax_sink(s[:, :2 * CH], sk_ref[2 * p])
            po, _ = _softmax_sink(s[:, 2 * CH:], sk_ref[2 * p + 1])
            probs.append(jnp.concatenate([pe, po], axis=1).astype(BF16))
            if p >= 1:
                outs.append(_dot(probs[p - 1], vvs[(p - 1) // 2]))
        outs.append(_dot(probs[-1], vvs[-1]))
        o_ref[...] = jnp.concatenate(outs, axis=1).astype(BF16)

    prev = lambda n: (jnp.maximum(n - 1, 0), 0)
    cur = lambda n: (n, 0)
    return _call(
        body, (sinks, q, k, k, va, va), name="attn_fwd", grid=(nb,),
        in_specs=[pl.BlockSpec(memory_space=pltpu.SMEM), pl.BlockSpec((CH, D), cur),
                  pl.BlockSpec((CH, KVW), prev), pl.BlockSpec((CH, KVW), cur),
                  pl.BlockSpec((CH, KVW), prev), pl.BlockSpec((CH, KVW), cur)],
        out_specs=pl.BlockSpec((CH, D), cur), out_shape=jax.ShapeDtypeStruct((T, D), BF16),
        sem=("parallel",), comm=comm)


def _attn_bwd(q, k, va, datt, sinks, rc, rs1, rs2, comm=None):
    T = q.shape[0]
    nb = T // CH

    def body(sk_ref, q_ref, kp_ref, kc_ref, vp_ref, vc_ref, do_ref, cq_ref, s1q_ref, s2q_ref, ck_ref, s1k_ref, s2k_ref,
             dq_ref, dk_ref, dv_ref, dsk_ref, kcar, vcar):
        n = pl.program_id(0)

        @pl.when(n == 0)
        def _():
            kcar[...] = jnp.zeros_like(kcar)
            vcar[...] = jnp.zeros_like(vcar)
            dsk_ref[...] = jnp.zeros_like(dsk_ref)

        def flush(kprev, vprev):
            ck, s1k, s2k = ck_ref[...], s1k_ref[...], s2k_ref[...]
            for j in range(KVW // CH):
                sl = slice(CH * j, CH * (j + 1))
                dk_ref[:, sl] = _rope_t(kcar[:, sl] + kprev[:, sl], ck, s1k, s2k).astype(BF16)
                dv_ref[:, sl] = (vcar[:, sl] + vprev[:, sl]).astype(BF16)

        @pl.when(n < nb)
        def _():
            mask = _attn_mask(n)
            kp, kc, vp, vc = kp_ref[...], kc_ref[...], vp_ref[...], vc_ref[...]
            cq, s1q, s2q = cq_ref[...], s1q_ref[...], s2q_ref[...]
            lane = lax.broadcasted_iota(jnp.int32, (1, CH), 1)
            dsk = jnp.zeros((1, CH), F32)
            npairs = D // CH
            kks = [_pair_layout(kp, kc, grp) for grp in range(NKV)]
            vvs = [_pair_layout(vp, vc, grp) for grp in range(NKV)]
            qs = [q_ref[:, CH * p:CH * (p + 1)] for p in range(npairs)]
            dos = [do_ref[:, CH * p:CH * (p + 1)].astype(BF16) for p in range(npairs)]

            def first(p):
                return _dot_nt(qs[p], kks[p // 2]), _dot_nt(dos[p], vvs[p // 2])

            def last(p, ds, pb):
                return (_rope_t(_dot(ds, kks[p // 2]), cq, s1q, s2q) * Q_SCALE, _dot_tn(qs[p], ds), _dot_tn(dos[p], pb))

            ahead = 2
            pending = [first(p) for p in range(ahead)]
            mids, ends = [], []
            for p in range(npairs):
                s, dp = pending.pop(0)
                s = jnp.where(mask, s, -1e30)
                if p + ahead < npairs:
                    pending.append(first(p + ahead))
                ds_parts, p_parts = [], []
                for par in range(2):
                    sl = slice(2 * CH * par, 2 * CH * (par + 1))
                    pr, psink = _softmax_sink(s[:, sl], sk_ref[2 * p + par])
                    delta = jnp.sum(pr * dp[:, sl], axis=-1, keepdims=True)
                    ds_parts.append(pr * (dp[:, sl] - delta))
                    p_parts.append(pr)
                    tot = -jnp.sum(psink * delta, axis=0, keepdims=True)
                    dsk = dsk + jnp.where(lane == 2 * p + par, tot, 0.0)
                mids.append((jnp.concatenate(ds_parts, axis=1).astype(BF16), jnp.concatenate(p_parts, axis=1).astype(BF16)))
                if p >= 1:
                    ends.append(last(p - 1, *mids[p - 1]))
            ends.append(last(npairs - 1, *mids[-1]))
            dq_cols = [e[0] for e in ends]
            def fold(i):
                rows = []
                for grp in range(NKV):
                    acc = ends[2 * grp][i] + ends[2 * grp + 1][i]
                    rows.append(acc[:HD, :2 * CH] + acc[HD:, 2 * CH:])
                return jnp.concatenate(rows, axis=0).T

            dkf, dvf = fold(1), fold(2)
            dq_ref[...] = jnp.concatenate(dq_cols, axis=1).astype(BF16)
            dsk_ref[...] += dsk
            flush(dkf[:CH], dvf[:CH])
            kcar[...] = dkf[CH:]
            vcar[...] = dvf[CH:]

        @pl.when(n == nb)
        def _():
            z = jnp.zeros((CH, KVW), F32)
            flush(z, z)

    last = nb - 1
    cur = lambda n: (jnp.minimum(n, last), 0)
    prev = lambda n: (jnp.clip(n - 1, 0, last), 0)
    sd = jax.ShapeDtypeStruct
    return _call(
        body, (sinks, q, k, k, va, va, datt, rc, rs1, rs2, rc, rs1, rs2), name="attn_bwd", grid=(nb + 1,),
        in_specs=[pl.BlockSpec(memory_space=pltpu.SMEM), pl.BlockSpec((CH, D), cur),
                  pl.BlockSpec((CH, KVW), prev), pl.BlockSpec((CH, KVW), cur),
                  pl.BlockSpec((CH, KVW), prev), pl.BlockSpec((CH, KVW), cur),
                  pl.BlockSpec((CH, D), cur),
                  pl.BlockSpec((CH, CH), cur), pl.BlockSpec((CH, CH), cur), pl.BlockSpec((CH, CH), cur),
                  pl.BlockSpec((CH, CH), prev), pl.BlockSpec((CH, CH), prev), pl.BlockSpec((CH, CH), prev)],
        out_specs=[pl.BlockSpec((CH, D), cur), pl.BlockSpec((CH, KVW), prev), pl.BlockSpec((CH, KVW), prev),
                   _const((1, CH))],
        out_shape=[sd((T, D), BF16), sd((T, KVW), BF16), sd((T, KVW), BF16), sd((1, CH), F32)],
        scratch_shapes=[pltpu.VMEM((CH, KVW), F32), pltpu.VMEM((CH, KVW), F32)], sem=("arbitrary",), comm=comm)


def _merge_fwd(a, att, ga, gb, x, w_a, w_b, w_o, g2, tm):
    T = x.shape[0]

    def body(a_ref, att_ref, ga_ref, gb_ref, x_ref, wa_ref, wb_ref, wo_ref, g_ref,
             pa_ref, pb_ref, mg_ref, mix_ref, x1_ref):
        pa = _dot(a_ref[...], wa_ref[...])
        pb = _dot(att_ref[...], wb_ref[...])
        pa_ref[...] = pa.astype(BF16)
        pb_ref[...] = pb.astype(BF16)
        merged = (_sigmoid(ga_ref[...].astype(F32)) * pa + _sigmoid(gb_ref[...].astype(F32)) * pb).astype(BF16)
        mg_ref[...] = merged
        mix = _dot(merged, wo_ref[...])
        mix_ref[...] = mix
        mhat, _ = _rms_hat(mix)
        x1_ref[...] = x_ref[...] + mhat * g_ref[...]

    sd = jax.ShapeDtypeStruct
    return pl.pallas_call(
        body, name="merge_fwd", grid=(T // tm,),
        in_specs=[_rows(tm, D)] * 5 + [_resident((D, D))] * 3 + [_const((1, D))],
        out_specs=[_rows(tm, D)] * 5,
        out_shape=[sd((T, D), BF16), sd((T, D), BF16), sd((T, D), BF16), sd((T, D), F32), sd((T, D), F32)],
        compiler_params=_cparams(("parallel",)),
    )(a, att, ga, gb, x, w_a, w_b, w_o, g2)


def _merge_bwd(dx1, mix, ga, gb, pa, pb, w_a, w_b, w_o, g2, tm):
    T = dx1.shape[0]

    def body(dx1_ref, mix_ref, ga_ref, gb_ref, pa_ref, pb_ref, wa_ref, wb_ref, wo_ref, g_ref,
             dmix_ref, dao_ref, dbo_ref, dga_ref, dgb_ref, da_ref, datt_ref, dg_ref):
        @pl.when(pl.program_id(0) == 0)
        def _():
            dg_ref[...] = jnp.zeros_like(dg_ref)

        mhat, r = _rms_hat(mix_ref[...])
        dmix, dg = _rms_bwd(mhat, r, g_ref[...], dx1_ref[...])
        dg_ref[...] += dg
        dmix = dmix.astype(BF16)
        dmix_ref[...] = dmix
        dmerged = _dot_nt(dmix, wo_ref[...])
        sa = _sigmoid(ga_ref[...].astype(F32))
        sb = _sigmoid(gb_ref[...].astype(F32))
        dao = (dmerged * sa).astype(BF16)
        dbo = (dmerged * sb).astype(BF16)
        dao_ref[...] = dao
        dbo_ref[...] = dbo
        dga_ref[...] = (dmerged * pa_ref[...].astype(F32) * (sa * (1.0 - sa))).astype(BF16)
        dgb_ref[...] = (dmerged * pb_ref[...].astype(F32) * (sb * (1.0 - sb))).astype(BF16)
        da_ref[...] = _dot_nt(dao, wa_ref[...])
        datt_ref[...] = _dot_nt(dbo, wb_ref[...]).astype(BF16)

    sd = jax.ShapeDtypeStruct
    return pl.pallas_call(
        body, name="merge_bwd", grid=(T // tm,),
        in_specs=[_rows(tm, D)] * 6 + [_resident((D, D))] * 3 + [_const((1, D))],
        out_specs=[_rows(tm, D)] * 7 + [_const((1, D))],
        out_shape=[sd((T, D), BF16)] * 5 + [sd((T, D), F32), sd((T, D), BF16), sd((1, D), F32)],
        compiler_params=_cparams(("arbitrary",)),
    )(dx1, mix, ga, gb, pa, pb, w_a, w_b, w_o, g2)


def _ffn(x1, target, w1, w2, g3, g4, tm):
    T = x1.shape[0]

    def body(x_ref, t_ref, w1_ref, w2_ref, g3_ref, g4_ref,
             hf_ref, f2_ref, dff_ref, df1_ref, dx_ref, ls_ref, dg3_ref, dg4_ref):
        @pl.when(pl.program_id(0) == 0)
        def _():
            ls_ref[...] = jnp.zeros_like(ls_ref)
            dg3_ref[...] = jnp.zeros_like(dg3_ref)
            dg4_ref[...] = jnp.zeros_like(dg4_ref)

        x = x_ref[...]
        g3, g4 = g3_ref[...], g4_ref[...]
        xhat, r3 = _rms_hat(x)
        hf = (xhat * g3).astype(BF16)
        hf_ref[...] = hf
        rl = jnp.maximum(_dot(hf, w1_ref[...]), 0.0)
        f2 = (rl * rl).astype(BF16)
        f2_ref[...] = f2
        fhat, r4 = _rms_hat(_dot(f2, w2_ref[...]))
        err = x + fhat * g4 - t_ref[...]
        ls_ref[...] += jnp.sum(err * err, axis=0, keepdims=True)
        dy = err * (1.0 / D)
        dff, dg4 = _rms_bwd(fhat, r4, g4, dy)
        dg4_ref[...] += dg4
        dff = dff.astype(BF16)
        dff_ref[...] = dff
        df1 = (_dot_nt(dff, w2_ref[...]) * (2.0 * rl)).astype(BF16)
        df1_ref[...] = df1
        dxn, dg3 = _rms_bwd(xhat, r3, g3, _dot_nt(df1, w1_ref[...]))
        dg3_ref[...] += dg3
        dx_ref[...] = dy + dxn

    sd = jax.ShapeDtypeStruct
    return pl.pallas_call(
        body, name="ffn_fwd_bwd", grid=(T // tm,),
        in_specs=[_rows(tm, D), _rows(tm, D), _resident((D, DFF)), _resident((DFF, D)), _const((1, D)), _const((1, D))],
        out_specs=[_rows(tm, D), _rows(tm, DFF), _rows(tm, D), _rows(tm, DFF), _rows(tm, D), _const((1, D)),
                   _const((1, D)), _const((1, D))],
        out_shape=[sd((T, D), BF16), sd((T, DFF), BF16), sd((T, D), BF16), sd((T, DFF), BF16), sd((T, D), F32),
                   sd((1, D), F32), sd((1, D), F32), sd((1, D), F32)],
        compiler_params=_cparams(("arbitrary",)),
    )(x1, target, w1, w2, g3, g4)


def _inproj_bwd(parts, x, dx1, g1, w_in, tm, comm=None):
    T = x.shape[0]
    widths = [p.shape[1] for p in parts]
    offs = [sum(widths[:i]) for i in range(len(widths) + 1)]
    assert offs[-1] == IN_W

    def body(*refs):
        n = len(parts)
        prefs = refs[:n]
        x_ref, dx1_ref, g_ref, w_ref, dx_ref, dp_ref, dg_ref = refs[n:]

        @pl.when(pl.program_id(0) == 0)
        def _():
            dg_ref[...] = jnp.zeros_like(dg_ref)

        dh = None
        for i in range(n):
            blk = prefs[i][...]
            dp_ref[:, offs[i]:offs[i + 1]] = blk
            t = _dot_nt(blk, w_ref[:, offs[i]:offs[i + 1]])
            dh = t if dh is None else dh + t
        xhat, r = _rms_hat(x_ref[...])
        dxn, dg = _rms_bwd(xhat, r, g_ref[...], dh)
        dg_ref[...] += dg
        dx_ref[...] = dx1_ref[...] + dxn

    sd = jax.ShapeDtypeStruct
    return _call(
        body, (*parts, x, dx1, g1, w_in), name="inproj_bwd", grid=(T // tm,),
        in_specs=[_rows(tm, w) for w in widths] + [_rows(tm, D), _rows(tm, D), _const((1, D)), _resident((D, IN_W))],
        out_specs=[_rows(tm, D), _rows(tm, IN_W), _const((1, D))],
        out_shape=[sd((T, D), F32), sd((T, IN_W), BF16), sd((1, D), F32)], sem=("arbitrary",), comm=comm)


def _wgrad(a, g, tn, tm, name, comm=None):
    T, K = a.shape
    N = g.shape[1]

    def body(a_ref, g_ref, o_ref):
        @pl.when(pl.program_id(1) == 0)
        def _():
            o_ref[...] = jnp.zeros_like(o_ref)

        o_ref[...] += _dot_tn(a_ref[...], g_ref[...])

    return _call(
        body, (a, g), name=name, grid=(N // tn, T // tm),
        in_specs=[pl.BlockSpec((tm, K), lambda j, t: (t, 0)), pl.BlockSpec((tm, tn), lambda j, t: (t, j))],
        out_specs=pl.BlockSpec((K, tn), lambda j, t: (0, j)),
        out_shape=jax.ShapeDtypeStruct((K, N), F32), sem=("parallel", "arbitrary"), comm=comm)


def _adamw(w, g, m, v, tr, name):
    R, C = w.shape
    bc1 = 1.0 / (1.0 - B1 ** STEP)
    bc2 = 1.0 / (1.0 - B2 ** STEP)

    def body(w_ref, g_ref, m_ref, v_ref, d_ref, nm_ref, nv_ref):
        g = g_ref[...]
        m = B1 * m_ref[...] + (1.0 - B1) * g
        v = B2 * v_ref[...] + (1.0 - B2) * (g * g)
        nm_ref[...] = m
        nv_ref[...] = v
        d_ref[...] = -LR * ((m * bc1) / (jnp.sqrt(v * bc2) + AEPS) + WD * w_ref[...])

    spec = pl.BlockSpec((tr, C), lambda i: (i, 0))
    return pl.pallas_call(
        body, name=name, grid=(R // tr,), in_specs=[spec] * 4, out_specs=[spec] * 3,
        out_shape=[jax.ShapeDtypeStruct((R, C), F32)] * 3,
        compiler_params=_cparams(("parallel",)),
    )(w, g, m, v)


BIG = (("col", (D, IN_W)), ("row", (D, D)), ("row", (D, D)), ("row", (D, D)), ("col", (D, DFF)), ("row", (DFF, D)))
NBIG = len(BIG)
ANY = pl.BlockSpec(memory_space=pl.ANY)


def _shard_shape(kind, shape):
    R, C = shape
    return (R, C // 4) if kind == "col" else (R // 4, C)


def _half_shape(kind, shape):
    R, C = shape
    return (R // 2, C) if kind == "col" else (R, C // 2)


def _piece_shape(kind, shape):
    R, C = shape
    return (R // 2, C // 4) if kind == "col" else (R // 4, C // 2)


def _own_region(ref, kind, shape, s):
    R, C = shape
    return ref.at[:, pl.ds(s * (C // 4), C // 4)] if kind == "col" else ref.at[pl.ds(s * (R // 4), R // 4), :]


def _ag_region(ref, kind, shape, s, hc):
    R, C = shape
    if kind == "col":
        return ref.at[pl.ds(hc * (R // 2), R // 2), pl.ds(s * (C // 4), C // 4)]
    return ref.at[pl.ds(s * (R // 4) + hc * (R // 8), R // 8), :]


def _ag_shard_half(ref, kind, shape, hc):
    R, C = shape
    return ref.at[pl.ds(hc * (R // 2), R // 2), :] if kind == "col" else ref.at[pl.ds(hc * (R // 8), R // 8), :]


def _grad_half(ref, kind, shape, hc):
    R, C = shape
    return ref.at[pl.ds(hc * (R // 2), R // 2), :] if kind == "col" else ref.at[:, pl.ds(hc * (C // 2), C // 2)]


def _half_piece(ref, kind, shape, s):
    R, C = shape
    return ref.at[:, pl.ds(s * (C // 4), C // 4)] if kind == "col" else ref.at[pl.ds(s * (R // 4), R // 4), :]


def _place():
    x, y, c = lax.axis_index("x"), lax.axis_index("y"), lax.axis_index("c")
    chips = [(1 - x, y), (x, 1 - y), (1 - x, 1 - y)]
    return x, y, c, chips


def _rcopy(src, dst, ssem, rsem, dev):
    return pltpu.make_async_remote_copy(src_ref=src, dst_ref=dst, send_sem=ssem, recv_sem=rsem,
                                        device_id=dev, device_id_type=MESH)


def _dma_sems(n):
    return pltpu.SemaphoreType.DMA((n,))


def _x_gather_ici(shards, ws):
    n = len(ws)
    specs = [BIG[w] for w in ws]

    def place():
        x, y, c, chips = _place()
        return c, chips, 2 * x + y

    def sends(sh, full, sc):
        c, chips, me_s = place()
        return [_rcopy(_ag_shard_half(sh[i], kind, shape, c), _ag_region(full[i], kind, shape, me_s, c),
                       sc[0].at[3 * i + j], sc[1].at[3 * i + j], (cx, cy, c))
                for i, (kind, shape) in enumerate(specs) for j, (cx, cy) in enumerate(chips)]

    def start(sh, full, sc):
        for i in range(n):
            pltpu.make_async_copy(sh[i], sc[4 + i], sc[2].at[i]).start()
        for cp in sends(sh, full, sc):
            cp.start()

    def finish(sh, full, sc):
        c, chips, me_s = place()
        stores = []
        for i, (kind, shape) in enumerate(specs):
            pltpu.make_async_copy(sh[i], sc[4 + i], sc[2].at[i]).wait()
            st = pltpu.make_async_copy(sc[4 + i], _own_region(full[i], kind, shape, me_s), sc[3].at[i])
            st.start()
            stores.append(st)
        for i, (kind, shape) in enumerate(specs):
            for j, (cx, cy) in enumerate(chips):
                reg = _ag_region(full[i], kind, shape, 2 * cx + cy, c)
                _rcopy(reg, reg, sc[0].at[3 * i + j], sc[1].at[3 * i + j], (cx, cy, c)).wait_recv()
        for cp in sends(sh, full, sc):
            cp.wait_send()
        for st in stores:
            st.wait()

    return _Exchange(
        shards, [jax.ShapeDtypeStruct(shape, BF16) for _, shape in specs], {},
        [_dma_sems(3 * n), _dma_sems(3 * n), _dma_sems(n), _dma_sems(n)]
        + [pltpu.VMEM(_shard_shape(k, s), BF16) for k, s in specs], start, finish)


def _x_gather_d2d(wholes, ws):
    specs = [BIG[w] for w in ws]
    n = len(ws)

    def copies(full, sc, mine):
        x, y, c, chips = _place()
        hc = c if mine else 1 - c
        return [_rcopy(reg, reg, sc[0].at[3 * i + j], sc[1].at[3 * i + j], (x, y, 1 - c))
                for i, (kind, shape) in enumerate(specs) for j, (cx, cy) in enumerate(chips)
                for reg in [_ag_region(full[i], kind, shape, 2 * cx + cy, hc)]]

    def start(_, full, sc):
        for cp in copies(full, sc, True):
            cp.start()

    def finish(_, full, sc):
        for cp in copies(full, sc, False):
            cp.wait_recv()
        for cp in copies(full, sc, True):
            cp.wait_send()

    return _Exchange(wholes, [jax.ShapeDtypeStruct(shape, BF16) for _, shape in specs], {i: i for i in range(n)},
                     [_dma_sems(3 * n), _dma_sems(3 * n)], start, finish)


def _x_grads_sibling(grads, ws):
    specs = [BIG[w] for w in ws]
    n = len(ws)

    def copies(g, got, sc):
        x, y, c, _ = _place()
        return [_rcopy(_grad_half(g[i], kind, shape, 1 - c), got[i], sc[0].at[i], sc[1].at[i], (x, y, 1 - c))
                for i, (kind, shape) in enumerate(specs)]

    def start(g, got, sc):
        for cp in copies(g, got, sc):
            cp.start()

    def finish(g, got, sc):
        for cp in copies(g, got, sc):
            cp.wait_recv()
        for cp in copies(g, got, sc):
            cp.wait_send()

    return _Exchange(grads, [jax.ShapeDtypeStruct(_half_shape(k, s), F32) for k, s in specs], {},
                     [_dma_sems(n), _dma_sems(n)], start, finish)


def _x_grads_chips(sums_bf, ws):
    specs = [BIG[w] for w in ws]
    n = len(ws)

    def copies(s16, got, sc):
        x, y, c, chips = _place()
        return [_rcopy(_half_piece(s16[i], kind, shape, 2 * cx + cy), got[i].at[j],
                       sc[0].at[3 * i + j], sc[1].at[3 * i + j], (cx, cy, c))
                for i, (kind, shape) in enumerate(specs) for j, (cx, cy) in enumerate(chips)]

    def start(s16, got, sc):
        for cp in copies(s16, got, sc):
            cp.start()

    def finish(s16, got, sc):
        for cp in copies(s16, got, sc):
            cp.wait_recv()
        for cp in copies(s16, got, sc):
            cp.wait_send()

    return _Exchange(sums_bf, [jax.ShapeDtypeStruct((3,) + _piece_shape(k, s), BF16) for k, s in specs], {},
                     [_dma_sems(3 * n), _dma_sems(3 * n)], start, finish)


def _shard_half(ref, kind, shape, hc):
    sr, sc = _shard_shape(kind, shape)
    return ref.at[pl.ds(hc * (sr // 2), sr // 2), :] if kind == "col" else ref.at[:, pl.ds(hc * (sc // 2), sc // 2)]


def _x_grads_share(shard_grads, ws):
    specs = [BIG[w] for w in ws]
    n = len(ws)

    def copies(g, sc, mine):
        x, y, c, _ = _place()
        hc = c if mine else 1 - c
        return [_rcopy(part, part, sc[0].at[i], sc[1].at[i], (x, y, 1 - c))
                for i, (kind, shape) in enumerate(specs) for part in [_shard_half(g[i], kind, shape, hc)]]

    def start(_, g, sc):
        for cp in copies(g, sc, True):
            cp.start()

    def finish(_, g, sc):
        for cp in copies(g, sc, False):
            cp.wait_recv()
        for cp in copies(g, sc, True):
            cp.wait_send()

    return _Exchange(shard_grads, [jax.ShapeDtypeStruct(_shard_shape(k, s), F32) for k, s in specs],
                     {i: i for i in range(n)}, [_dma_sems(n), _dma_sems(n)], start, finish)


ADD_ROWS = 256


def _add_halves(place, g, got, kind, name):
    R, C = g.shape
    hr, hcols = _half_shape(kind, (R, C))
    steps = hr // ADD_ROWS

    def body(p_ref, g_ref, b_ref, s_ref, sb_ref):
        s = g_ref[...] + b_ref[...]
        s_ref[...] = s
        sb_ref[...] = s.astype(BF16)

    if kind == "col":
        g_spec = pl.BlockSpec((ADD_ROWS, C), lambda i, p: (p[0] * steps + i, 0))
    else:
        g_spec = pl.BlockSpec((ADD_ROWS, hcols), lambda i, p: (i, p[0]))
    spec = pl.BlockSpec((ADD_ROWS, hcols), lambda i, p: (i, 0))
    return pl.pallas_call(
        body, name=name,
        grid_spec=pltpu.PrefetchScalarGridSpec(num_scalar_prefetch=1, grid=(steps,), in_specs=[g_spec, spec],
                                               out_specs=[spec, spec]),
        out_shape=[jax.ShapeDtypeStruct((hr, hcols), F32), jax.ShapeDtypeStruct((hr, hcols), BF16)],
        compiler_params=_cparams(("parallel",)),
    )(place, g, got)


def _add_pieces(place, half, got, kind, shape, name):
    pr, pc = _piece_shape(kind, shape)
    steps = pr // ADD_ROWS

    def body(p_ref, m_ref, g_ref, o_ref):
        acc = m_ref[...]
        for j in range(3):
            acc = acc + g_ref[j].astype(F32)
        o_ref[...] = acc

    if kind == "col":
        m_spec = pl.BlockSpec((ADD_ROWS, pc), lambda i, p: (i, p[1]))
        o_spec = pl.BlockSpec((ADD_ROWS, pc), lambda i, p: (p[0] * steps + i, 0))
    else:
        m_spec = pl.BlockSpec((ADD_ROWS, pc), lambda i, p: (p[1] * steps + i, 0))
        o_spec = pl.BlockSpec((ADD_ROWS, pc), lambda i, p: (i, p[0]))
    return pl.pallas_call(
        body, name=name,
        grid_spec=pltpu.PrefetchScalarGridSpec(
            num_scalar_prefetch=1, grid=(steps,),
            in_specs=[m_spec, pl.BlockSpec((3, ADD_ROWS, pc), lambda i, p: (0, i, 0))], out_specs=o_spec),
        out_shape=jax.ShapeDtypeStruct(_shard_shape(kind, shape), F32),
        compiler_params=_cparams(("parallel",)),
    )(place, half, got)


SMALL_ROWS = 1024 + 8 * 8


def _x_small_all_reduce(p):
    def parts(p_ref, sc):
        slots, ssem, rsem = sc[0], sc[2], sc[3]
        x, y, c = lax.axis_index("x"), lax.axis_index("y"), lax.axis_index("c")
        me = 4 * x + 2 * y + c
        out = []
        for r in range(1, 8):
            bx, by, bc = (r >> 2) & 1, (r >> 1) & 1, r & 1
            tgt = (1 - x if bx else x, 1 - y if by else y, 1 - c if bc else c)
            send = _rcopy(p_ref, slots.at[me], ssem.at[r - 1], rsem.at[r - 1], tgt)
            src = 4 * tgt[0] + 2 * tgt[1] + tgt[2]
            recv = _rcopy(p_ref, slots.at[src], ssem.at[r - 1], rsem.at[r - 1], tgt)
            out.append((send, recv))
        return me, out

    def start(ins, outs, sc):
        me, cps = parts(ins[0], sc)
        pltpu.make_async_copy(ins[0], sc[0].at[me], sc[4].at[0]).start()
        for send, _ in cps:
            send.start()

    def finish(ins, outs, sc):
        me, cps = parts(ins[0], sc)
        pltpu.make_async_copy(ins[0], sc[0].at[me], sc[4].at[0]).wait()
        for _, recv in cps:
            recv.wait_recv()
        acc = sc[0][0]
        for d in range(1, 8):
            acc = acc + sc[0][d]
        sc[1][...] = acc
        back = pltpu.make_async_copy(sc[1], outs[0], sc[4].at[1])
        back.start()
        for send, _ in cps:
            send.wait_send()
        back.wait()

    return _Exchange([p], [jax.ShapeDtypeStruct((SMALL_ROWS, CH), F32)], {},
                     [pltpu.VMEM((8, SMALL_ROWS, CH), F32), pltpu.VMEM((SMALL_ROWS, CH), F32), _dma_sems(7), _dma_sems(7),
                      _dma_sems(2)], start, finish)


def _rope_tables(positions, comm=None):
    T = positions.shape[0]
    inv_freq = 500000.0 ** (-jnp.arange(0, 2 * ROPE_HALF, 2, dtype=F32) / (2 * ROPE_HALF))
    head = jnp.concatenate([inv_freq, inv_freq, jnp.zeros((HD - 2 * ROPE_HALF,), F32)])
    lane_freq = jnp.concatenate([head, head])[None, :]
    pos = jnp.broadcast_to(positions.astype(F32)[:, None], (T, CH))
    tm = min(1024, T)

    def body(p_ref, f_ref, c_ref, s1_ref, s2_ref):
        ang = p_ref[...] * f_ref[...]
        sin = jnp.sin(ang)
        first = (lax.broadcasted_iota(jnp.int32, ang.shape, 1) % HD) < ROPE_HALF
        c_ref[...] = jnp.cos(ang)
        s1_ref[...] = jnp.where(first, -sin, 0.0)
        s2_ref[...] = jnp.where(first, 0.0, sin)

    return _call(body, (pos, lane_freq), name="rope_tables", grid=(T // tm,),
                 in_specs=[_rows(tm, CH), _const((1, CH))], out_specs=[_rows(tm, CH)] * 3,
                 out_shape=[jax.ShapeDtypeStruct((T, CH), F32)] * 3, sem=("parallel",), comm=comm)


BIG_NAMES = ("w_in", "w_a", "w_b", "w_o", "w_ff_in", "w_ff_out")
SMALL_NAMES = ("w_spatial", "ln_v_gain", "ln_v_bias", "b_spatial", "sinks", "norm_mix_pre", "norm_mix_post",
               "norm_ff_pre", "norm_ff_post")
WEIGHT_ORDER = ("w_in", "ln_v_gain", "ln_v_bias", "w_spatial", "b_spatial", "sinks", "w_a", "w_b", "w_o",
                "norm_mix_pre", "norm_mix_post", "w_ff_in", "w_ff_out", "norm_ff_pre", "norm_ff_post")


def _pack_small(d):
    parts = []
    for n in SMALL_NAMES:
        flat = d[n].reshape(-1)
        pad = (-flat.shape[0]) % (8 * CH)
        parts.append(jnp.pad(flat, (0, pad)).reshape(-1, CH))
    return jnp.concatenate(parts, axis=0)


def _unpack_small(p, like):
    out, row = {}, 0
    for n in SMALL_NAMES:
        size = like[n].size
        rows = -(-size // (8 * CH)) * 8
        out[n] = p[row:row + rows].reshape(-1)[:size].reshape(like[n].shape)
        row += rows
    return out


def kernel(x, positions, w_in, ln_v_gain, ln_v_bias, w_spatial, b_spatial, sinks, w_a, w_b, w_o, norm_mix_pre, norm_mix_post, w_ff_in, w_ff_out, norm_ff_pre, norm_ff_post, loss_target, m_w_in, m_ln_v_gain, m_ln_v_bias, m_w_spatial, m_b_spatial, m_sinks, m_w_a, m_w_b, m_w_o, m_norm_mix_pre, m_norm_mix_post, m_w_ff_in, m_w_ff_out, m_norm_ff_pre, m_norm_ff_post, v_w_in, v_ln_v_gain, v_ln_v_bias, v_w_spatial, v_b_spatial, v_sinks, v_w_a, v_w_b, v_w_o, v_norm_mix_pre, v_norm_mix_post, v_w_ff_in, v_w_ff_out, v_norm_ff_pre, v_norm_ff_post):
    w = dict(w_in=w_in, ln_v_gain=ln_v_gain, ln_v_bias=ln_v_bias, w_spatial=w_spatial, b_spatial=b_spatial, sinks=sinks,
             w_a=w_a, w_b=w_b, w_o=w_o, norm_mix_pre=norm_mix_pre, norm_mix_post=norm_mix_post, w_ff_in=w_ff_in,
             w_ff_out=w_ff_out, norm_ff_pre=norm_ff_pre, norm_ff_post=norm_ff_post)
    m = dict(w_in=m_w_in, ln_v_gain=m_ln_v_gain, ln_v_bias=m_ln_v_bias, w_spatial=m_w_spatial, b_spatial=m_b_spatial,
             sinks=m_sinks, w_a=m_w_a, w_b=m_w_b, w_o=m_w_o, norm_mix_pre=m_norm_mix_pre, norm_mix_post=m_norm_mix_post,
             w_ff_in=m_w_ff_in, w_ff_out=m_w_ff_out, norm_ff_pre=m_norm_ff_pre, norm_ff_post=m_norm_ff_post)
    v = dict(w_in=v_w_in, ln_v_gain=v_ln_v_gain, ln_v_bias=v_ln_v_bias, w_spatial=v_w_spatial, b_spatial=v_b_spatial,
             sinks=v_sinks, w_a=v_w_a, w_b=v_w_b, w_o=v_w_o, norm_mix_pre=v_norm_mix_pre, norm_mix_post=v_norm_mix_post,
             w_ff_in=v_w_ff_in, w_ff_out=v_w_ff_out, norm_ff_pre=v_norm_ff_pre, norm_ff_post=v_norm_ff_post)

    FIRST, REST = (0,), tuple(range(1, NBIG))
    shards = [w[n][0].astype(BF16) for n in BIG_NAMES]
    place = jnp.stack([lax.axis_index("c"), 2 * lax.axis_index("x") + lax.axis_index("y")]).astype(jnp.int32)
    xs, target = x[0], loss_target[0]
    T = xs.shape[0]
    wtm, wtm2 = min(1024, T), min(2048, T)
    g1, g2, g3, g4 = norm_mix_pre, norm_mix_post, norm_ff_pre, norm_ff_post
    w_sp, snk = w_spatial[0], sinks[0]
    MIX, FF = (1, 2, 3), (4, 5)
    bfull = jnp.repeat(b_spatial[0].T, CH, axis=1)

    def reduce_tail(ws, grads, got):
        sums = [_add_halves(place, grads[i], got[i], BIG[k][0], name="grad_add_sibling_" + BIG_NAMES[k])
                for i, k in enumerate(ws)]
        return sums, _x_grads_chips([s[1] for s in sums], ws)

    def reduce_end(ws, sums, pieces):
        return [_add_pieces(place, sums[i][0], pieces[i], *BIG[k], name="grad_add_chips_" + BIG_NAMES[k])
                for i, k in enumerate(ws)]

    (rc, rs1, rs2), w_in_part = _rope_tables(positions[0], comm=_x_gather_ici(shards[:1], FIRST))
    w_in_b = _run(_x_gather_d2d(w_in_part, FIRST), "gather_w_in_d2d")[0]
    (h, u, vs, q, k, va, ga, gb), ff_part = _inproj(xs, g1, w_in_b, rc, rs1, rs2, tm=512, comm=_x_gather_ici(shards[4:], FF))
    a, mix_part = _sgu_fwd(u, vs, ln_v_gain, ln_v_bias, w_sp, bfull, tm=512, comm=_x_gather_ici(shards[1:4], MIX))
    att, rest = _attn_fwd(q, k, va, snk, comm=_both(_x_gather_d2d(mix_part, MIX), _x_gather_d2d(ff_part, FF)))
    w_a_b, w_b_b, w_o_b, w_ff_in_b, w_ff_out_b = rest
    pa, pb, merged, mix, x1 = _merge_fwd(a, att, ga, gb, xs, w_a_b, w_b_b, w_o_b, g2, tm=512)
    hf, f2, dff, df1, dx1, lsum, dg3, dg4 = _ffn(x1, target, w_ff_in_b, w_ff_out_b, g3, g4, tm=256)
    loss = lax.psum(0.5 * jnp.sum(lsum) / D, ("x", "y", "c"))

    dw_ff_out, _ = _wgrad(f2, dff, tn=1024, tm=512, name="wgrad_ff_out")
    dw_ff_in, _ = _wgrad(hf, df1, tn=2048, tm=wtm2, name="wgrad_ff_in")
    dmix, dao, dbo, dga, dgb, da, datt, dg2 = _merge_bwd(dx1, mix, ga, gb, pa, pb, w_a_b, w_b_b, w_o_b, g2, tm=512)
    dw_o, _ = _wgrad(merged, dmix, tn=1024, tm=wtm2, name="wgrad_o")
    dw_a, _ = _wgrad(a, dao, tn=1024, tm=wtm2, name="wgrad_a")
    dw_b, _ = _wgrad(att, dbo, tn=1024, tm=wtm2, name="wgrad_b")
    grads_rest = [dw_a, dw_b, dw_o, dw_ff_in, dw_ff_out]
    (du, dvs, dws, dbs, dlg, dlb), got_rest = _sgu_bwd(
        u, vs, da, ln_v_gain, ln_v_bias, w_sp, bfull, tm=512, comm=_x_grads_sibling(grads_rest, REST))
    sums_rest, to_chips = reduce_tail(REST, grads_rest, got_rest)
    (dq, dk, dva, dsk), pieces_rest = _attn_bwd(q, k, va, datt, snk, rc, rs1, rs2, comm=to_chips)
    partial_rest = reduce_end(REST, sums_rest, pieces_rest)
    (dx, dproj, dg1), _ = _inproj_bwd([du, dvs, dq, dk, dva, dga, dgb], xs, dx1, g1, w_in_b, tm=512)
    small = dict(ln_v_gain=dlg, ln_v_bias=dlb, w_spatial=dws, b_spatial=dbs, sinks=dsk[:, :NQ],
                 norm_mix_pre=dg1, norm_mix_post=dg2, norm_ff_pre=dg3, norm_ff_post=dg4)
    dw_in, (gs, *shard_rest) = _wgrad(
        h, dproj, tn=IN_W // 2, tm=wtm, name="wgrad_in",
        comm=_both(_x_small_all_reduce(_pack_small(small)), _x_grads_share(partial_rest, REST)))
    got_in = _run(_x_grads_sibling([dw_in], FIRST), "grads_in_to_sibling")
    sums_in, to_chips = reduce_tail(FIRST, [dw_in], got_in)
    partial_in = reduce_end(FIRST, sums_in, _run(to_chips, "grads_in_to_chips"))
    shard_grads = list(_run(_x_grads_share(partial_in, FIRST), "grads_in_share")) + list(shard_rest)

    grad, delta, new_m, new_v = {}, {}, {}, {}
    for i, n in enumerate(BIG_NAMES):
        g = shard_grads[i]
        d_, m_, v_ = _adamw(w[n][0], g, m[n][0], v[n][0], tr=256, name="adamw_" + n)
        grad[n], delta[n], new_m[n], new_v[n] = g[None], d_[None], m_[None], v_[None]

    ds, ms, vs = _adamw(_pack_small(w), gs, _pack_small(m), _pack_small(v), tr=SMALL_ROWS // 4, name="adamw_small")
    for packed, dst in ((gs, grad), (ds, delta), (ms, new_m), (vs, new_v)):
        dst.update(_unpack_small(packed, w))

    outs = [loss, dx[None]]
    for group in (grad, delta, new_m, new_v):
        outs.extend(group[n] for n in WEIGHT_ORDER)
    return tuple(outs)
```

```python
import functools

import jax
import jax.numpy as jnp
from jax import lax
from jax.experimental import pallas as pl
from jax.experimental.pallas import tpu as pltpu

F32 = jnp.float32
BF16 = jnp.bfloat16

D = 1024
CH = 128
NG = 8
HD = 64
NQ = 16
NKV = 4
KVW = NKV * HD
DFF = 4 * D
EPS = 1e-6
IN_W = 5632
SEG = (0, 1024, 2048, 3072, 3328, 3584, 4608, 5632)
ROPE_HALF = 8
Q_SCALE = HD ** -0.5

LR, B1, B2, AEPS, WD, STEP = 0.001, 0.9, 0.999, 1e-08, 0.01, 10

VMEM_LIMIT = 60 * 1024 * 1024
MESH = pl.DeviceIdType.MESH

_GELU_C0 = 0.7978845608028654
_GELU_C1 = 0.044715


def _cparams(sem=None):
    kw = dict(vmem_limit_bytes=VMEM_LIMIT)
    if sem is not None:
        kw["dimension_semantics"] = sem
    return pltpu.CompilerParams(**kw)


def _resident(shape):
    nd = len(shape)
    return pl.BlockSpec(shape, lambda *_: (0,) * nd, pipeline_mode=pl.Buffered(1))


def _const(shape):
    nd = len(shape)
    return pl.BlockSpec(shape, lambda *_: (0,) * nd)


def _rows(tm, w):
    return pl.BlockSpec((tm, w), lambda i: (i, 0))


class _Exchange:
    def __init__(self, ins, outs, aliases, scratch, start, finish):
        self.ins, self.outs, self.aliases, self.scratch = list(ins), list(outs), dict(aliases), list(scratch)
        self.start, self.finish = start, finish


def _both(a, b):
    na, ma, sa = len(a.ins), len(a.outs), len(a.scratch)

    def start(ci, co, cs):
        a.start(ci[:na], co[:ma], cs[:sa])
        b.start(ci[na:], co[ma:], cs[sa:])

    def finish(ci, co, cs):
        a.finish(ci[:na], co[:ma], cs[:sa])
        b.finish(ci[na:], co[ma:], cs[sa:])

    aliases = {**a.aliases, **{na + i: ma + j for i, j in b.aliases.items()}}
    return _Exchange(a.ins + b.ins, a.outs + b.outs, aliases, a.scratch + b.scratch, start, finish)


def _call(body, args, *, name, grid, in_specs, out_specs, out_shape, scratch_shapes=(), sem=None, comm=None):
    single = not isinstance(out_shape, (list, tuple))
    out_shape = [out_shape] if single else list(out_shape)
    out_specs = [out_specs] if single else list(out_specs)
    if comm is None:
        res = pl.pallas_call(body, name=name, grid=grid, in_specs=list(in_specs), out_specs=out_specs,
                             out_shape=out_shape, scratch_shapes=list(scratch_shapes),
                             compiler_params=_cparams(sem))(*args)
        return (res[0] if single else res), []
    n_in, n_out, n_scr = len(args), len(out_shape), len(scratch_shapes)
    nci, nco = len(comm.ins), len(comm.outs)
    steps = 1
    for g in grid:
        steps *= g

    def hosted(*refs):
        a, ci = refs[:n_in], refs[n_in:n_in + nci]
        o, co = refs[n_in + nci:n_in + nci + n_out], refs[n_in + nci + n_out:n_in + nci + n_out + nco]
        rest = refs[n_in + nci + n_out + nco:]
        scr, cs = rest[:n_scr], rest[n_scr:]
        step = pl.program_id(0)
        for d in range(1, len(grid)):
            step = step * grid[d] + pl.program_id(d)

        @pl.when(step == 0)
        def _():
            comm.start(ci, co, cs)

        body(*a, *o, *scr)

        @pl.when(step == steps - 1)
        def _():
            comm.finish(ci, co, cs)

    res = pl.pallas_call(
        hosted, name=name, grid=grid, in_specs=list(in_specs) + [ANY] * nci, out_specs=out_specs + [ANY] * nco,
        out_shape=out_shape + comm.outs, scratch_shapes=list(scratch_shapes) + comm.scratch,
        input_output_aliases={n_in + i: n_out + j for i, j in comm.aliases.items()},
        compiler_params=_cparams(("arbitrary",) * len(grid)),
    )(*args, *comm.ins)
    own = res[:n_out]
    return (own[0] if single else own), list(res[n_out:])


def _run(comm, name):
    nci = len(comm.ins)

    def body(*refs):
        ci, co, cs = refs[:nci], refs[nci:nci + len(comm.outs)], refs[nci + len(comm.outs):]
        comm.start(ci, co, cs)
        comm.finish(ci, co, cs)

    return pl.pallas_call(
        body, name=name, in_specs=[ANY] * nci, out_specs=[ANY] * len(comm.outs), out_shape=comm.outs,
        scratch_shapes=comm.scratch, input_output_aliases=comm.aliases,
        compiler_params=pltpu.CompilerParams(vmem_limit_bytes=VMEM_LIMIT),
    )(*comm.ins)


def _gelu(x):
    x2 = x * x
    t = jnp.tanh(x * (_GELU_C0 + (_GELU_C0 * _GELU_C1) * x2))
    hx = 0.5 * x
    return hx + hx * t, (t, x2, hx)


def _gelu_grad(parts):
    t, x2, hx = parts
    return (0.5 + 0.5 * t) + hx * (1.0 - t * t) * (_GELU_C0 + (3.0 * _GELU_C0 * _GELU_C1) * x2)


def _sigmoid(x):
    return 1.0 / (1.0 + jnp.exp(-x))


def _rms_hat(x):
    r = lax.rsqrt(jnp.mean(x * x, axis=-1, keepdims=True) + EPS)
    return x * r, r


def _rms_bwd(xhat, r, g, dout):
    dg = jnp.sum(dout * xhat, axis=0, keepdims=True)
    dy = dout * g
    dx = r * (dy - xhat * jnp.mean(dy * xhat, axis=-1, keepdims=True))
    return dx, dg


def _dot(a, b):
    return jnp.dot(a, b, preferred_element_type=F32)


def _dot_nt(a, b):
    return lax.dot_general(a, b, (((1,), (1,)), ((), ())), preferred_element_type=F32)


def _dot_tn(a, b):
    return lax.dot_general(a, b, (((0,), (0,)), ((), ())), preferred_element_type=F32)


def _rope(blk, c, s1, s2):
    return blk * c + pltpu.roll(blk, CH - ROPE_HALF, 1) * s1 + pltpu.roll(blk, ROPE_HALF, 1) * s2


def _rope_t(blk, c, s1, s2):
    return blk * c + pltpu.roll(blk * s1, ROPE_HALF, 1) + pltpu.roll(blk * s2, CH - ROPE_HALF, 1)


def _inproj(x, g1, w_in, rc, rs1, rs2, tm, comm=None):
    T = x.shape[0]

    def body(x_ref, g_ref, w_ref, c_ref, s1_ref, s2_ref,
             h_ref, u_ref, v_ref, q_ref, k_ref, va_ref, ga_ref, gb_ref):
        xhat, _ = _rms_hat(x_ref[...])
        h = (xhat * g_ref[...]).astype(BF16)
        h_ref[...] = h
        u_ref[...] = _dot(h, w_ref[:, SEG[0]:SEG[1]])
        v_ref[...] = _dot(h, w_ref[:, SEG[1]:SEG[2]])
        c, s1, s2 = c_ref[...], s1_ref[...], s2_ref[...]
        q = _dot(h, w_ref[:, SEG[2]:SEG[3]])
        for p in range(D // CH):
            blk = _rope(q[:, CH * p:CH * (p + 1)], c, s1, s2) * Q_SCALE
            q_ref[:, CH * p:CH * (p + 1)] = blk.astype(BF16)
        k = _dot(h, w_ref[:, SEG[3]:SEG[4]])
        for p in range(KVW // CH):
            k_ref[:, CH * p:CH * (p + 1)] = _rope(k[:, CH * p:CH * (p + 1)], c, s1, s2).astype(BF16)
        va_ref[...] = _dot(h, w_ref[:, SEG[4]:SEG[5]]).astype(BF16)
        ga_ref[...] = _dot(h, w_ref[:, SEG[5]:SEG[6]]).astype(BF16)
        gb_ref[...] = _dot(h, w_ref[:, SEG[6]:SEG[7]]).astype(BF16)

    sd = jax.ShapeDtypeStruct
    return _call(
        body, (x, g1, w_in, rc, rs1, rs2), name="inproj_fwd", grid=(T // tm,),
        in_specs=[_rows(tm, D), _const((1, D)), _resident((D, IN_W)), _rows(tm, CH), _rows(tm, CH), _rows(tm, CH)],
        out_specs=[_rows(tm, D), _rows(tm, D), _rows(tm, D), _rows(tm, D), _rows(tm, KVW), _rows(tm, KVW),
                   _rows(tm, D), _rows(tm, D)],
        out_shape=[sd((T, D), BF16), sd((T, D), F32), sd((T, D), F32), sd((T, D), BF16), sd((T, KVW), BF16),
                   sd((T, KVW), BF16), sd((T, D), BF16), sd((T, D), BF16)],
        sem=("parallel",), comm=comm)


def _sgu_common(u, vs, lng, lnb, ws_ref, bfull):
    nc = u.shape[0] // CH
    ug, tu = _gelu(u)
    vg, tv = _gelu(vs)
    mu = jnp.mean(vg, axis=-1, keepdims=True)
    xc = vg - mu
    rstd = lax.rsqrt(jnp.mean(xc * xc, axis=-1, keepdims=True) + EPS)
    vhat = xc * rstd
    vnb = (vhat * lng + lnb).astype(BF16)
    tri = lax.broadcasted_iota(jnp.int32, (CH, CH), 0) >= lax.broadcasted_iota(jnp.int32, (CH, CH), 1)
    wts, rhss, mixed = [], [], []
    for g in range(NG):
        wt = jnp.where(tri, ws_ref[g], 0.0).astype(BF16)
        rhs = jnp.concatenate([vnb[CH * c:CH * (c + 1), CH * g:CH * (g + 1)] for c in range(nc)], axis=1)
        mix = _dot(wt, rhs)
        wts.append(wt)
        rhss.append(rhs)
        mixed.append([mix[:, CH * c:CH * (c + 1)] + bfull[:, CH * g:CH * (g + 1)] for c in range(nc)])
    return nc, ug, tu, tv, rstd, vhat, tri, wts, rhss, mixed


def _sgu_fwd(u, vs, lng, lnb, ws, bfull, tm, comm=None):
    T = u.shape[0]

    def body(u_ref, v_ref, lng_ref, lnb_ref, ws_ref, bf_ref, a_ref):
        nc, ug, _, _, _, _, _, _, _, mixed = _sgu_common(
            u_ref[...], v_ref[...], lng_ref[...], lnb_ref[...], ws_ref, bf_ref[...])
        mixed_all = jnp.concatenate(
            [jnp.concatenate([mixed[g][c] for g in range(NG)], axis=1) for c in range(nc)], axis=0)
        a_ref[...] = (ug * mixed_all).astype(BF16)

    return _call(
        body, (u, vs, lng, lnb, ws, bfull), name="sgu_fwd", grid=(T // tm,),
        in_specs=[_rows(tm, D), _rows(tm, D), _const((1, D)), _const((1, D)), _const((NG, CH, CH)), _const((CH, D))],
        out_specs=_rows(tm, D), out_shape=jax.ShapeDtypeStruct((T, D), BF16), sem=("parallel",), comm=comm)


def _sgu_bwd(u, vs, da, lng, lnb, ws, bfull, tm, comm=None):
    T = u.shape[0]
    nsteps = T // tm

    def body(u_ref, v_ref, da_ref, lng_ref, lnb_ref, ws_ref, bf_ref,
             du_ref, dv_ref, dws_ref, dbs_ref, dlg_ref, dlb_ref, db_ref):
        i = pl.program_id(0)
        u, vs, da, lng = u_ref[...], v_ref[...], da_ref[...], lng_ref[...]
        nc, ug, tu, tv, rstd, vhat, tri, wts, rhss, mixed = _sgu_common(u, vs, lng, lnb_ref[...], ws_ref, bf_ref[...])

        @pl.when(i == 0)
        def _():
            dws_ref[...] = jnp.zeros_like(dws_ref)
            db_ref[...] = jnp.zeros_like(db_ref)
            dlg_ref[...] = jnp.zeros_like(dlg_ref)
            dlb_ref[...] = jnp.zeros_like(dlb_ref)

        mixed_all = jnp.concatenate(
            [jnp.concatenate([mixed[g][c] for g in range(NG)], axis=1) for c in range(nc)], axis=0)
        du_ref[...] = (da * mixed_all * _gelu_grad(tu)).astype(BF16)
        dmixed = da * ug
        dvn_cols = []
        for g in range(NG):
            dmix = [dmixed[CH * c:CH * (c + 1), CH * g:CH * (g + 1)] for c in range(nc)]
            db_ref[:, CH * g:CH * (g + 1)] += functools.reduce(lambda a, b: a + b, dmix)
            dm = jnp.concatenate(dmix, axis=1).astype(BF16)
            dws_ref[g] += _dot_nt(dm, rhss[g])
            dvn_cols.append(_dot_tn(wts[g], dm))
        dvn = jnp.concatenate(
            [jnp.concatenate([dvn_cols[g][:, CH * c:CH * (c + 1)] for g in range(NG)], axis=1) for c in range(nc)],
            axis=0)
        dlg_ref[...] += jnp.sum(dvn * vhat, axis=0, keepdims=True)
        dlb_ref[...] += jnp.sum(dvn, axis=0, keepdims=True)
        dvh = dvn * lng
        dvg = rstd * (dvh - jnp.mean(dvh, axis=-1, keepdims=True)
                      - vhat * jnp.mean(dvh * vhat, axis=-1, keepdims=True))
        dv_ref[...] = (dvg * _gelu_grad(tv)).astype(BF16)

        @pl.when(i == nsteps - 1)
        def _():
            for g in range(NG):
                dws_ref[g] = jnp.where(tri, dws_ref[g], 0.0)
                dbs_ref[g:g + 1, :] = jnp.sum(db_ref[:, CH * g:CH * (g + 1)].T, axis=0, keepdims=True)

    sd = jax.ShapeDtypeStruct
    return _call(
        body, (u, vs, da, lng, lnb, ws, bfull), name="sgu_bwd", grid=(nsteps,),
        in_specs=[_rows(tm, D), _rows(tm, D), _rows(tm, D), _const((1, D)), _const((1, D)), _const((NG, CH, CH)),
                  _const((CH, D))],
        out_specs=[_rows(tm, D), _rows(tm, D), _const((NG, CH, CH)), _const((NG, CH)), _const((1, D)), _const((1, D))],
        out_shape=[sd((T, D), BF16), sd((T, D), BF16), sd((NG, CH, CH), F32), sd((NG, CH), F32), sd((1, D), F32),
                   sd((1, D), F32)],
        scratch_shapes=[pltpu.VMEM((CH, D), F32)], sem=("arbitrary",), comm=comm)


def _pair_layout(prev, cur, grp):
    j, half = grp // 2, grp % 2
    blk = jnp.concatenate([prev[:, CH * j:CH * (j + 1)], cur[:, CH * j:CH * (j + 1)]], axis=0).astype(F32)
    lo = lax.broadcasted_iota(jnp.int32, blk.shape, 1) < HD
    rolled = pltpu.roll(blk, HD, 1)
    even = jnp.where(lo, blk if half == 0 else rolled, 0.0)
    odd = jnp.where(lo, 0.0, rolled if half == 0 else blk)
    return jnp.concatenate([even, odd], axis=0).astype(BF16)


def _attn_mask(n):
    qi = lax.broadcasted_iota(jnp.int32, (CH, 2 * CH), 0)
    kc = lax.broadcasted_iota(jnp.int32, (CH, 2 * CH), 1)
    ok = (kc > qi) & (kc <= qi + CH) & ((kc >= CH) | (n > 0))
    return jnp.concatenate([ok, ok], axis=1)


def _softmax_sink(s, sink):
    m = jnp.maximum(jnp.max(s, axis=-1, keepdims=True), sink)
    p = jnp.exp(s - m)
    ps = jnp.exp(sink - m)
    inv = 1.0 / (jnp.sum(p, axis=-1, keepdims=True) + ps)
    return p * inv, ps * inv


def _attn_fwd(q, k, va, sinks, comm=None):
    T = q.shape[0]
    nb = T // CH

    def body(sk_ref, q_ref, kp_ref, kc_ref, vp_ref, vc_ref, o_ref):
        n = pl.program_id(0)
        mask = _attn_mask(n)
        kp, kc, vp, vc = kp_ref[...], kc_ref[...], vp_ref[...], vc_ref[...]
        kks = [_pair_layout(kp, kc, grp) for grp in range(NKV)]
        vvs = [_pair_layout(vp, vc, grp) for grp in range(NKV)]
        npairs = D // CH

        def scores(p):
            return _dot_nt(q_ref[:, CH * p:CH * (p + 1)], kks[p // 2])

        ahead = 3
        outs, probs = [], []
        pending = [scores(p) for p in range(ahead)]
        even_lanes = lax.broadcasted_iota(jnp.int32, (CH, CH), 1) < HD

        def unnormalised(s, sink):
            m = jnp.maximum(jnp.max(s, axis=-1, keepdims=True), sink)
            p = jnp.exp(s - m)
            return p, 1.0 / (jnp.sum(p, axis=-1, keepdims=True) + jnp.exp(sink - m))

        def value_product(p):
            pr, ie, io = probs[p]
            return _dot(pr, vvs[p // 2]) * jnp.where(even_lanes, ie, io)

        for p in range(npairs):
            s = jnp.where(mask, pending.pop(0), -1e30)
            if p + ahead < npairs:
                pending.append(scores(p + ahead))
            pe, ie = unnormalised(s[:, :2 * CH], sk_ref[2 * p])
            po, io = unnormalised(s[:, 2 * CH:], sk_ref[2 * p + 1])
            probs.append((jnp.concatenate([pe, po], axis=1).astype(BF16), ie, io))
            if p >= 1:
                outs.append(value_product(p - 1))
        outs.append(value_product(npairs - 1))
        o_ref[...] = jnp.concatenate(outs, axis=1).astype(BF16)

    prev = lambda n: (jnp.maximum(n - 1, 0), 0)
    cur = lambda n: (n, 0)
    return _call(
        body, (sinks, q, k, k, va, va), name="attn_fwd", grid=(nb,),
        in_specs=[pl.BlockSpec(memory_space=pltpu.SMEM), pl.BlockSpec((CH, D), cur),
                  pl.BlockSpec((CH, KVW), prev), pl.BlockSpec((CH, KVW), cur),
                  pl.BlockSpec((CH, KVW), prev), pl.BlockSpec((CH, KVW), cur)],
        out_specs=pl.BlockSpec((CH, D), cur), out_shape=jax.ShapeDtypeStruct((T, D), BF16),
        sem=("parallel",), comm=comm)


def _attn_bwd(q, k, va, datt, sinks, rc, rs1, rs2, comm=None):
    T = q.shape[0]
    nb = T // CH

    def body(sk_ref, q_ref, kp_ref, kc_ref, vp_ref, vc_ref, do_ref, cq_ref, s1q_ref, s2q_ref, ck_ref, s1k_ref, s2k_ref,
             dq_ref, dk_ref, dv_ref, dsk_ref, kcar, vcar):
        n = pl.program_id(0)

        @pl.when(n == 0)
        def _():
            kcar[...] = jnp.zeros_like(kcar)
            vcar[...] = jnp.zeros_like(vcar)
            dsk_ref[...] = jnp.zeros_like(dsk_ref)

        def flush(kprev, vprev):
            ck, s1k, s2k = ck_ref[...], s1k_ref[...], s2k_ref[...]
            for j in range(KVW // CH):
                sl = slice(CH * j, CH * (j + 1))
                dk_ref[:, sl] = _rope_t(kcar[:, sl] + kprev[:, sl], ck, s1k, s2k).astype(BF16)
                dv_ref[:, sl] = (vcar[:, sl] + vprev[:, sl]).astype(BF16)

        @pl.when(n < nb)
        def _():
            mask = _attn_mask(n)
            kp, kc, vp, vc = kp_ref[...], kc_ref[...], vp_ref[...], vc_ref[...]
            cq, s1q, s2q = cq_ref[...], s1q_ref[...], s2q_ref[...]
            lane = lax.broadcasted_iota(jnp.int32, (1, CH), 1)
            dsk = jnp.zeros((1, CH), F32)
            npairs = D // CH
            kks = [_pair_layout(kp, kc, grp) for grp in range(NKV)]
            vvs = [_pair_layout(vp, vc, grp) for grp in range(NKV)]
            qs = [q_ref[:, CH * p:CH * (p + 1)] for p in range(npairs)]
            dos = [do_ref[:, CH * p:CH * (p + 1)].astype(BF16) for p in range(npairs)]

            def first(p):
                return _dot_nt(qs[p], kks[p // 2]), _dot_nt(dos[p], vvs[p // 2])

            def last(p, ds, pb):
                return (_rope_t(_dot(ds, kks[p // 2]), cq, s1q, s2q) * Q_SCALE, _dot_tn(qs[p], ds), _dot_tn(dos[p], pb))

            ahead = 2
            pending = [first(p) for p in range(ahead)]
            mids, ends = [], []
            for p in range(npairs):
                s, dp = pending.pop(0)
                s = jnp.where(mask, s, -1e30)
                if p + ahead < npairs:
                    pending.append(first(p + ahead))
                ds_parts, p_parts = [], []
                for par in range(2):
                    sl = slice(2 * CH * par, 2 * CH * (par + 1))
                    pr, psink = _softmax_sink(s[:, sl], sk_ref[2 * p + par])
                    delta = jnp.sum(pr * dp[:, sl], axis=-1, keepdims=True)
                    ds_parts.append(pr * (dp[:, sl] - delta))
                    p_parts.append(pr)
                    tot = -jnp.sum(psink * delta, axis=0, keepdims=True)
                    dsk = dsk + jnp.where(lane == 2 * p + par, tot, 0.0)
                mids.append((jnp.concatenate(ds_parts, axis=1).astype(BF16), jnp.concatenate(p_parts, axis=1).astype(BF16)))
                if p >= 1:
                    ends.append(last(p - 1, *mids[p - 1]))
            ends.append(last(npairs - 1, *mids[-1]))
            dq_cols = [e[0] for e in ends]
            def fold(i):
                rows = []
                for grp in range(NKV):
                    acc = ends[2 * grp][i] + ends[2 * grp + 1][i]
                    rows.append(acc[:HD, :2 * CH] + acc[HD:, 2 * CH:])
                return jnp.concatenate(rows, axis=0).T

            dkf, dvf = fold(1), fold(2)
            dq_ref[...] = jnp.concatenate(dq_cols, axis=1).astype(BF16)
            dsk_ref[...] += dsk
            flush(dkf[:CH], dvf[:CH])
            kcar[...] = dkf[CH:]
            vcar[...] = dvf[CH:]

        @pl.when(n == nb)
        def _():
            z = jnp.zeros((CH, KVW), F32)
            flush(z, z)

    last = nb - 1
    cur = lambda n: (jnp.minimum(n, last), 0)
    prev = lambda n: (jnp.clip(n - 1, 0, last), 0)
    sd = jax.ShapeDtypeStruct
    return _call(
        body, (sinks, q, k, k, va, va, datt, rc, rs1, rs2, rc, rs1, rs2), name="attn_bwd", grid=(nb + 1,),
        in_specs=[pl.BlockSpec(memory_space=pltpu.SMEM), pl.BlockSpec((CH, D), cur),
                  pl.BlockSpec((CH, KVW), prev), pl.BlockSpec((CH, KVW), cur),
                  pl.BlockSpec((CH, KVW), prev), pl.BlockSpec((CH, KVW), cur),
                  pl.BlockSpec((CH, D), cur),
                  pl.BlockSpec((CH, CH), cur), pl.BlockSpec((CH, CH), cur), pl.BlockSpec((CH, CH), cur),
                  pl.BlockSpec((CH, CH), prev), pl.BlockSpec((CH, CH), prev), pl.BlockSpec((CH, CH), prev)],
        out_specs=[pl.BlockSpec((CH, D), cur), pl.BlockSpec((CH, KVW), prev), pl.BlockSpec((CH, KVW), prev),
                   _const((1, CH))],
        out_shape=[sd((T, D), BF16), sd((T, KVW), BF16), sd((T, KVW), BF16), sd((1, CH), F32)],
        scratch_shapes=[pltpu.VMEM((CH, KVW), F32), pltpu.VMEM((CH, KVW), F32)], sem=("arbitrary",), comm=comm)


def _merge_fwd(a, att, ga, gb, x, w_a, w_b, w_o, g2, tm):
    T = x.shape[0]

    def body(a_ref, att_ref, ga_ref, gb_ref, x_ref, wa_ref, wb_ref, wo_ref, g_ref,
             pa_ref, pb_ref, mg_ref, mix_ref, x1_ref):
        pa = _dot(a_ref[...], wa_ref[...])
        pb = _dot(att_ref[...], wb_ref[...])
        pa_ref[...] = pa.astype(BF16)
        pb_ref[...] = pb.astype(BF16)
        merged = (_sigmoid(ga_ref[...].astype(F32)) * pa + _sigmoid(gb_ref[...].astype(F32)) * pb).astype(BF16)
        mg_ref[...] = merged
        mix = _dot(merged, wo_ref[...])
        mix_ref[...] = mix
        mhat, _ = _rms_hat(mix)
        x1_ref[...] = x_ref[...] + mhat * g_ref[...]

    sd = jax.ShapeDtypeStruct
    return pl.pallas_call(
        body, name="merge_fwd", grid=(T // tm,),
        in_specs=[_rows(tm, D)] * 5 + [_resident((D, D))] * 3 + [_const((1, D))],
        out_specs=[_rows(tm, D)] * 5,
        out_shape=[sd((T, D), BF16), sd((T, D), BF16), sd((T, D), BF16), sd((T, D), F32), sd((T, D), F32)],
        compiler_params=_cparams(("parallel",)),
    )(a, att, ga, gb, x, w_a, w_b, w_o, g2)


def _merge_bwd(dx1, mix, ga, gb, pa, pb, a, att, merged, w_a, w_b, w_o, g2, tm, comm=None):
    T = dx1.shape[0]
    nsteps = T // tm

    def body(dx1_ref, mix_ref, ga_ref, gb_ref, pa_ref, pb_ref, a_ref, att_ref, mg_ref, wa_ref, wb_ref, wo_ref, g_ref,
             dga_ref, dgb_ref, da_ref, datt_ref, dg_ref, dwa_ref, dwb_ref, dwo_ref, acc, sem):
        i = pl.program_id(0)

        @pl.when(i == 0)
        def _():
            dg_ref[...] = jnp.zeros_like(dg_ref)
            acc[...] = jnp.zeros_like(acc)

        mhat, r = _rms_hat(mix_ref[...])
        dmix, dg = _rms_bwd(mhat, r, g_ref[...], dx1_ref[...])
        dg_ref[...] += dg
        dmix = dmix.astype(BF16)
        dmerged = _dot_nt(dmix, wo_ref[...])
        sa = _sigmoid(ga_ref[...].astype(F32))
        sb = _sigmoid(gb_ref[...].astype(F32))
        dao = (dmerged * sa).astype(BF16)
        dbo = (dmerged * sb).astype(BF16)
        dga_ref[...] = (dmerged * pa_ref[...].astype(F32) * (sa * (1.0 - sa))).astype(BF16)
        dgb_ref[...] = (dmerged * pb_ref[...].astype(F32) * (sb * (1.0 - sb))).astype(BF16)
        da_ref[...] = _dot_nt(dao, wa_ref[...])
        datt_ref[...] = _dot_nt(dbo, wb_ref[...]).astype(BF16)
        acc[0] += _dot_tn(a_ref[...], dao)
        acc[1] += _dot_tn(att_ref[...], dbo)
        acc[2] += _dot_tn(mg_ref[...], dmix)

        @pl.when(i == nsteps - 1)
        def _():
            outs = [pltpu.make_async_copy(acc.at[j], ref, sem.at[j]) for j, ref in enumerate((dwa_ref, dwb_ref, dwo_ref))]
            for cp in outs:
                cp.start()
            for cp in outs:
                cp.wait()

    sd = jax.ShapeDtypeStruct
    return _call(
        body, (dx1, mix, ga, gb, pa, pb, a, att, merged, w_a, w_b, w_o, g2), name="merge_bwd", grid=(nsteps,),
        in_specs=[_rows(tm, D)] * 9 + [_resident((D, D))] * 3 + [_const((1, D))],
        out_specs=[_rows(tm, D)] * 4 + [_const((1, D))] + [ANY] * 3,
        out_shape=[sd((T, D), BF16), sd((T, D), BF16), sd((T, D), F32), sd((T, D), BF16), sd((1, D), F32)]
        + [sd((D, D), F32)] * 3,
        scratch_shapes=[pltpu.VMEM((3, D, D), F32), _dma_sems(3)], sem=("arbitrary",), comm=comm)


def _ffn(x1, target, w1, w2, g3, g4, tm):
    T = x1.shape[0]

    def body(x_ref, t_ref, w1_ref, w2_ref, g3_ref, g4_ref,
             hf_ref, f2_ref, dff_ref, df1_ref, dx_ref, ls_ref, dg3_ref, dg4_ref):
        @pl.when(pl.program_id(0) == 0)
        def _():
            ls_ref[...] = jnp.zeros_like(ls_ref)
            dg3_ref[...] = jnp.zeros_like(dg3_ref)
            dg4_ref[...] = jnp.zeros_like(dg4_ref)

        x = x_ref[...]
        g3, g4 = g3_ref[...], g4_ref[...]
        xhat, r3 = _rms_hat(x)
        hf = (xhat * g3).astype(BF16)
        hf_ref[...] = hf
        rl = jnp.maximum(_dot(hf, w1_ref[...]), 0.0)
        f2 = (rl * rl).astype(BF16)
        f2_ref[...] = f2
        fhat, r4 = _rms_hat(_dot(f2, w2_ref[...]))
        err = x + fhat * g4 - t_ref[...]
        ls_ref[...] += jnp.sum(err * err, axis=0, keepdims=True)
        dy = err * (1.0 / D)
        dff, dg4 = _rms_bwd(fhat, r4, g4, dy)
        dg4_ref[...] += dg4
        dff = dff.astype(BF16)
        dff_ref[...] = dff
        df1 = (_dot_nt(dff, w2_ref[...]) * (2.0 * rl)).astype(BF16)
        df1_ref[...] = df1
        dxn, dg3 = _rms_bwd(xhat, r3, g3, _dot_nt(df1, w1_ref[...]))
        dg3_ref[...] += dg3
        dx_ref[...] = dy + dxn

    sd = jax.ShapeDtypeStruct
    return pl.pallas_call(
        body, name="ffn_fwd_bwd", grid=(T // tm,),
        in_specs=[_rows(tm, D), _rows(tm, D), _resident((D, DFF)), _resident((DFF, D)), _const((1, D)), _const((1, D))],
        out_specs=[_rows(tm, D), _rows(tm, DFF), _rows(tm, D), _rows(tm, DFF), _rows(tm, D), _const((1, D)),
                   _const((1, D)), _const((1, D))],
        out_shape=[sd((T, D), BF16), sd((T, DFF), BF16), sd((T, D), BF16), sd((T, DFF), BF16), sd((T, D), F32),
                   sd((1, D), F32), sd((1, D), F32), sd((1, D), F32)],
        compiler_params=_cparams(("arbitrary",)),
    )(x1, target, w1, w2, g3, g4)


def _inproj_bwd(parts, x, dx1, g1, w_in, tm, comm=None):
    T = x.shape[0]
    widths = [p.shape[1] for p in parts]
    offs = [sum(widths[:i]) for i in range(len(widths) + 1)]
    assert offs[-1] == IN_W

    def body(*refs):
        n = len(parts)
        prefs = refs[:n]
        x_ref, dx1_ref, g_ref, w_ref, dx_ref, dp_ref, dg_ref = refs[n:]

        @pl.when(pl.program_id(0) == 0)
        def _():
            dg_ref[...] = jnp.zeros_like(dg_ref)

        dh = None
        for i in range(n):
            blk = prefs[i][...]
            dp_ref[:, offs[i]:offs[i + 1]] = blk
            t = _dot_nt(blk, w_ref[:, offs[i]:offs[i + 1]])
            dh = t if dh is None else dh + t
        xhat, r = _rms_hat(x_ref[...])
        dxn, dg = _rms_bwd(xhat, r, g_ref[...], dh)
        dg_ref[...] += dg
        dx_ref[...] = dx1_ref[...] + dxn

    sd = jax.ShapeDtypeStruct
    return _call(
        body, (*parts, x, dx1, g1, w_in), name="inproj_bwd", grid=(T // tm,),
        in_specs=[_rows(tm, w) for w in widths] + [_rows(tm, D), _rows(tm, D), _const((1, D)), _resident((D, IN_W))],
        out_specs=[_rows(tm, D), _rows(tm, IN_W), _const((1, D))],
        out_shape=[sd((T, D), F32), sd((T, IN_W), BF16), sd((1, D), F32)], sem=("arbitrary",), comm=comm)


def _wgrad(a, g, tn, tm, name, comm=None):
    T, K = a.shape
    N = g.shape[1]

    def body(a_ref, g_ref, o_ref):
        @pl.when(pl.program_id(1) == 0)
        def _():
            o_ref[...] = jnp.zeros_like(o_ref)

        o_ref[...] += _dot_tn(a_ref[...], g_ref[...])

    return _call(
        body, (a, g), name=name, grid=(N // tn, T // tm),
        in_specs=[pl.BlockSpec((tm, K), lambda j, t: (t, 0)), pl.BlockSpec((tm, tn), lambda j, t: (t, j))],
        out_specs=pl.BlockSpec((K, tn), lambda j, t: (0, j)),
        out_shape=jax.ShapeDtypeStruct((K, N), F32), sem=("parallel", "arbitrary"), comm=comm)


def _adamw(w, g, m, v, tr, name):
    R, C = w.shape
    bc1 = 1.0 / (1.0 - B1 ** STEP)
    bc2 = 1.0 / (1.0 - B2 ** STEP)

    def body(w_ref, g_ref, m_ref, v_ref, d_ref, nm_ref, nv_ref):
        g = g_ref[...]
        m = B1 * m_ref[...] + (1.0 - B1) * g
        v = B2 * v_ref[...] + (1.0 - B2) * (g * g)
        nm_ref[...] = m
        nv_ref[...] = v
        d_ref[...] = -LR * ((m * bc1) / (jnp.sqrt(v * bc2) + AEPS) + WD * w_ref[...])

    spec = pl.BlockSpec((tr, C), lambda i: (i, 0))
    return pl.pallas_call(
        body, name=name, grid=(R // tr,), in_specs=[spec] * 4, out_specs=[spec] * 3,
        out_shape=[jax.ShapeDtypeStruct((R, C), F32)] * 3,
        compiler_params=_cparams(("parallel",)),
    )(w, g, m, v)


BIG = (("col", (D, IN_W)), ("row", (D, D)), ("row", (D, D)), ("row", (D, D)), ("col", (D, DFF)), ("row", (DFF, D)))
NBIG = len(BIG)
ANY = pl.BlockSpec(memory_space=pl.ANY)


def _shard_shape(kind, shape):
    R, C = shape
    return (R, C // 4) if kind == "col" else (R // 4, C)


def _half_shape(kind, shape):
    R, C = shape
    return (R // 2, C) if kind == "col" else (R, C // 2)


def _piece_shape(kind, shape):
    R, C = shape
    return (R // 2, C // 4) if kind == "col" else (R // 4, C // 2)


def _own_region(ref, kind, shape, s):
    R, C = shape
    return ref.at[:, pl.ds(s * (C // 4), C // 4)] if kind == "col" else ref.at[pl.ds(s * (R // 4), R // 4), :]


def _ag_region(ref, kind, shape, s, hc):
    R, C = shape
    if kind == "col":
        return ref.at[pl.ds(hc * (R // 2), R // 2), pl.ds(s * (C // 4), C // 4)]
    return ref.at[pl.ds(s * (R // 4) + hc * (R // 8), R // 8), :]


def _ag_shard_half(ref, kind, shape, hc):
    R, C = shape
    return ref.at[pl.ds(hc * (R // 2), R // 2), :] if kind == "col" else ref.at[pl.ds(hc * (R // 8), R // 8), :]


def _grad_half(ref, kind, shape, hc):
    R, C = shape
    return ref.at[pl.ds(hc * (R // 2), R // 2), :] if kind == "col" else ref.at[:, pl.ds(hc * (C // 2), C // 2)]


def _half_piece(ref, kind, shape, s):
    R, C = shape
    return ref.at[:, pl.ds(s * (C // 4), C // 4)] if kind == "col" else ref.at[pl.ds(s * (R // 4), R // 4), :]


def _place():
    x, y, c = lax.axis_index("x"), lax.axis_index("y"), lax.axis_index("c")
    chips = [(1 - x, y), (x, 1 - y), (1 - x, 1 - y)]
    return x, y, c, chips


def _rcopy(src, dst, ssem, rsem, dev):
    return pltpu.make_async_remote_copy(src_ref=src, dst_ref=dst, send_sem=ssem, recv_sem=rsem,
                                        device_id=dev, device_id_type=MESH)


def _dma_sems(n):
    return pltpu.SemaphoreType.DMA((n,))


def _x_gather_ici(shards, ws):
    n = len(ws)
    specs = [BIG[w] for w in ws]

    def place():
        x, y, c, chips = _place()
        return c, chips, 2 * x + y

    def sends(sh, full, sc):
        c, chips, me_s = place()
        return [_rcopy(_ag_shard_half(sh[i], kind, shape, c), _ag_region(full[i], kind, shape, me_s, c),
                       sc[0].at[3 * i + j], sc[1].at[3 * i + j], (cx, cy, c))
                for i, (kind, shape) in enumerate(specs) for j, (cx, cy) in enumerate(chips)]

    def start(sh, full, sc):
        for i in range(n):
            pltpu.make_async_copy(sh[i], sc[4 + i], sc[2].at[i]).start()
        for cp in sends(sh, full, sc):
            cp.start()

    def finish(sh, full, sc):
        c, chips, me_s = place()
        stores = []
        for i, (kind, shape) in enumerate(specs):
            pltpu.make_async_copy(sh[i], sc[4 + i], sc[2].at[i]).wait()
            st = pltpu.make_async_copy(sc[4 + i], _own_region(full[i], kind, shape, me_s), sc[3].at[i])
            st.start()
            stores.append(st)
        for i, (kind, shape) in enumerate(specs):
            for j, (cx, cy) in enumerate(chips):
                reg = _ag_region(full[i], kind, shape, 2 * cx + cy, c)
                _rcopy(reg, reg, sc[0].at[3 * i + j], sc[1].at[3 * i + j], (cx, cy, c)).wait_recv()
        for cp in sends(sh, full, sc):
            cp.wait_send()
        for st in stores:
            st.wait()

    return _Exchange(
        shards, [jax.ShapeDtypeStruct(shape, BF16) for _, shape in specs], {},
        [_dma_sems(3 * n), _dma_sems(3 * n), _dma_sems(n), _dma_sems(n)]
        + [pltpu.VMEM(_shard_shape(k, s), BF16) for k, s in specs], start, finish)


def _x_gather_d2d(wholes, ws):
    specs = [BIG[w] for w in ws]
    n = len(ws)

    def copies(full, sc, mine):
        x, y, c, chips = _place()
        hc = c if mine else 1 - c
        return [_rcopy(reg, reg, sc[0].at[3 * i + j], sc[1].at[3 * i + j], (x, y, 1 - c))
                for i, (kind, shape) in enumerate(specs) for j, (cx, cy) in enumerate(chips)
                for reg in [_ag_region(full[i], kind, shape, 2 * cx + cy, hc)]]

    def start(_, full, sc):
        for cp in copies(full, sc, True):
            cp.start()

    def finish(_, full, sc):
        for cp in copies(full, sc, False):
            cp.wait_recv()
        for cp in copies(full, sc, True):
            cp.wait_send()

    return _Exchange(wholes, [jax.ShapeDtypeStruct(shape, BF16) for _, shape in specs], {i: i for i in range(n)},
                     [_dma_sems(3 * n), _dma_sems(3 * n)], start, finish)


def _x_grads_sibling(grads, ws):
    specs = [BIG[w] for w in ws]
    n = len(ws)

    def copies(g, got, sc):
        x, y, c, _ = _place()
        return [_rcopy(_grad_half(g[i], kind, shape, 1 - c), got[i], sc[0].at[i], sc[1].at[i], (x, y, 1 - c))
                for i, (kind, shape) in enumerate(specs)]

    def start(g, got, sc):
        for cp in copies(g, got, sc):
            cp.start()

    def finish(g, got, sc):
        for cp in copies(g, got, sc):
            cp.wait_recv()
        for cp in copies(g, got, sc):
            cp.wait_send()

    return _Exchange(grads, [jax.ShapeDtypeStruct(_half_shape(k, s), F32) for k, s in specs], {},
                     [_dma_sems(n), _dma_sems(n)], start, finish)


def _x_to_sibling(arr):
    def copy(ins, outs, sc):
        x, y, c, _ = _place()
        return _rcopy(ins[0], outs[0], sc[0].at[0], sc[1].at[0], (x, y, 1 - c))

    def finish(ins, outs, sc):
        copy(ins, outs, sc).wait_recv()
        copy(ins, outs, sc).wait_send()

    return _Exchange([arr], [jax.ShapeDtypeStruct(arr.shape, arr.dtype)], {}, [_dma_sems(1), _dma_sems(1)],
                     lambda ins, outs, sc: copy(ins, outs, sc).start(), finish)


def _x_grads_chips(sums_bf, ws):
    specs = [BIG[w] for w in ws]
    n = len(ws)

    def copies(s16, got, sc):
        x, y, c, chips = _place()
        return [_rcopy(_half_piece(s16[i], kind, shape, 2 * cx + cy), got[i].at[j],
                       sc[0].at[3 * i + j], sc[1].at[3 * i + j], (cx, cy, c))
                for i, (kind, shape) in enumerate(specs) for j, (cx, cy) in enumerate(chips)]

    def start(s16, got, sc):
        for cp in copies(s16, got, sc):
            cp.start()

    def finish(s16, got, sc):
        for cp in copies(s16, got, sc):
            cp.wait_recv()
        for cp in copies(s16, got, sc):
            cp.wait_send()

    return _Exchange(sums_bf, [jax.ShapeDtypeStruct((3,) + _piece_shape(k, s), BF16) for k, s in specs], {},
                     [_dma_sems(3 * n), _dma_sems(3 * n)], start, finish)


def _shard_half(ref, kind, shape, hc):
    sr, sc = _shard_shape(kind, shape)
    return ref.at[pl.ds(hc * (sr // 2), sr // 2), :] if kind == "col" else ref.at[:, pl.ds(hc * (sc // 2), sc // 2)]


def _x_grads_share(shard_grads, ws):
    specs = [BIG[w] for w in ws]
    n = len(ws)

    def copies(g, sc, mine):
        x, y, c, _ = _place()
        hc = c if mine else 1 - c
        return [_rcopy(part, part, sc[0].at[i], sc[1].at[i], (x, y, 1 - c))
                for i, (kind, shape) in enumerate(specs) for part in [_shard_half(g[i], kind, shape, hc)]]

    def start(_, g, sc):
        for cp in copies(g, sc, True):
            cp.start()

    def finish(_, g, sc):
        for cp in copies(g, sc, False):
            cp.wait_recv()
        for cp in copies(g, sc, True):
            cp.wait_send()

    return _Exchange(shard_grads, [jax.ShapeDtypeStruct(_shard_shape(k, s), F32) for k, s in specs],
                     {i: i for i in range(n)}, [_dma_sems(n), _dma_sems(n)], start, finish)


ADD_ROWS = 256


def _add_halves(place, g, got, kind, name):
    R, C = g.shape
    hr, hcols = _half_shape(kind, (R, C))
    steps = hr // ADD_ROWS

    def body(p_ref, g_ref, b_ref, s_ref, sb_ref):
        s = g_ref[...] + b_ref[...]
        s_ref[...] = s
        sb_ref[...] = s.astype(BF16)

    if kind == "col":
        g_spec = pl.BlockSpec((ADD_ROWS, C), lambda i, p: (p[0] * steps + i, 0))
    else:
        g_spec = pl.BlockSpec((ADD_ROWS, hcols), lambda i, p: (i, p[0]))
    spec = pl.BlockSpec((ADD_ROWS, hcols), lambda i, p: (i, 0))
    return pl.pallas_call(
        body, name=name,
        grid_spec=pltpu.PrefetchScalarGridSpec(num_scalar_prefetch=1, grid=(steps,), in_specs=[g_spec, spec],
                                               out_specs=[spec, spec]),
        out_shape=[jax.ShapeDtypeStruct((hr, hcols), F32), jax.ShapeDtypeStruct((hr, hcols), BF16)],
        compiler_params=_cparams(("parallel",)),
    )(place, g, got)


def _add_pair(a, b, name):
    R, C = a.shape

    def body(a_ref, b_ref, s_ref, sb_ref):
        s = a_ref[...] + b_ref[...]
        s_ref[...] = s
        sb_ref[...] = s.astype(BF16)

    spec = pl.BlockSpec((ADD_ROWS, C), lambda i: (i, 0))
    return pl.pallas_call(
        body, name=name, grid=(R // ADD_ROWS,), in_specs=[spec, spec], out_specs=[spec, spec],
        out_shape=[jax.ShapeDtypeStruct((R, C), F32), jax.ShapeDtypeStruct((R, C), BF16)],
        compiler_params=_cparams(("parallel",)),
    )(a, b)


def _add_pieces(place, half, got, kind, shape, name):
    pr, pc = _piece_shape(kind, shape)
    steps = pr // ADD_ROWS

    def body(p_ref, m_ref, g_ref, o_ref):
        acc = m_ref[...]
        for j in range(3):
            acc = acc + g_ref[j].astype(F32)
        o_ref[...] = acc

    if kind == "col":
        m_spec = pl.BlockSpec((ADD_ROWS, pc), lambda i, p: (i, p[1]))
        o_spec = pl.BlockSpec((ADD_ROWS, pc), lambda i, p: (p[0] * steps + i, 0))
    else:
        m_spec = pl.BlockSpec((ADD_ROWS, pc), lambda i, p: (p[1] * steps + i, 0))
        o_spec = pl.BlockSpec((ADD_ROWS, pc), lambda i, p: (i, p[0]))
    return pl.pallas_call(
        body, name=name,
        grid_spec=pltpu.PrefetchScalarGridSpec(
            num_scalar_prefetch=1, grid=(steps,),
            in_specs=[m_spec, pl.BlockSpec((3, ADD_ROWS, pc), lambda i, p: (0, i, 0))], out_specs=o_spec),
        out_shape=jax.ShapeDtypeStruct(_shard_shape(kind, shape), F32),
        compiler_params=_cparams(("parallel",)),
    )(place, half, got)


SMALL_ROWS = 1024 + 8 * 8


def _x_small_all_reduce(p):
    def parts(p_ref, sc):
        slots, ssem, rsem = sc[0], sc[2], sc[3]
        x, y, c = lax.axis_index("x"), lax.axis_index("y"), lax.axis_index("c")
        me = 4 * x + 2 * y + c
        out = []
        for r in range(1, 8):
            bx, by, bc = (r >> 2) & 1, (r >> 1) & 1, r & 1
            tgt = (1 - x if bx else x, 1 - y if by else y, 1 - c if bc else c)
            send = _rcopy(p_ref, slots.at[me], ssem.at[r - 1], rsem.at[r - 1], tgt)
            src = 4 * tgt[0] + 2 * tgt[1] + tgt[2]
            recv = _rcopy(p_ref, slots.at[src], ssem.at[r - 1], rsem.at[r - 1], tgt)
            out.append((send, recv))
        return me, out

    def start(ins, outs, sc):
        me, cps = parts(ins[0], sc)
        pltpu.make_async_copy(ins[0], sc[0].at[me], sc[4].at[0]).start()
        for send, _ in cps:
            send.start()

    def finish(ins, outs, sc):
        me, cps = parts(ins[0], sc)
        pltpu.make_async_copy(ins[0], sc[0].at[me], sc[4].at[0]).wait()
        for _, recv in cps:
            recv.wait_recv()
        acc = sc[0][0]
        for d in range(1, 8):
            acc = acc + sc[0][d]
        sc[1][...] = acc
        back = pltpu.make_async_copy(sc[1], outs[0], sc[4].at[1])
        back.start()
        for send, _ in cps:
            send.wait_send()
        back.wait()

    return _Exchange([p], [jax.ShapeDtypeStruct((SMALL_ROWS, CH), F32)], {},
                     [pltpu.VMEM((8, SMALL_ROWS, CH), F32), pltpu.VMEM((SMALL_ROWS, CH), F32), _dma_sems(7), _dma_sems(7),
                      _dma_sems(2)], start, finish)


def _rope_tables(positions, comm=None):
    T = positions.shape[0]
    inv_freq = 500000.0 ** (-jnp.arange(0, 2 * ROPE_HALF, 2, dtype=F32) / (2 * ROPE_HALF))
    head = jnp.concatenate([inv_freq, inv_freq, jnp.zeros((HD - 2 * ROPE_HALF,), F32)])
    lane_freq = jnp.concatenate([head, head])[None, :]
    pos = jnp.broadcast_to(positions.astype(F32)[:, None], (T, CH))
    tm = min(1024, T)

    def body(p_ref, f_ref, c_ref, s1_ref, s2_ref):
        ang = p_ref[...] * f_ref[...]
        sin = jnp.sin(ang)
        first = (lax.broadcasted_iota(jnp.int32, ang.shape, 1) % HD) < ROPE_HALF
        c_ref[...] = jnp.cos(ang)
        s1_ref[...] = jnp.where(first, -sin, 0.0)
        s2_ref[...] = jnp.where(first, 0.0, sin)

    return _call(body, (pos, lane_freq), name="rope_tables", grid=(T // tm,),
                 in_specs=[_rows(tm, CH), _const((1, CH))], out_specs=[_rows(tm, CH)] * 3,
                 out_shape=[jax.ShapeDtypeStruct((T, CH), F32)] * 3, sem=("parallel",), comm=comm)


BIG_NAMES = ("w_in", "w_a", "w_b", "w_o", "w_ff_in", "w_ff_out")
SMALL_NAMES = ("w_spatial", "ln_v_gain", "ln_v_bias", "b_spatial", "sinks", "norm_mix_pre", "norm_mix_post",
               "norm_ff_pre", "norm_ff_post")
WEIGHT_ORDER = ("w_in", "ln_v_gain", "ln_v_bias", "w_spatial", "b_spatial", "sinks", "w_a", "w_b", "w_o",
                "norm_mix_pre", "norm_mix_post", "w_ff_in", "w_ff_out", "norm_ff_pre", "norm_ff_post")


def _pack_small(d):
    parts = []
    for n in SMALL_NAMES:
        flat = d[n].reshape(-1)
        pad = (-flat.shape[0]) % (8 * CH)
        parts.append(jnp.pad(flat, (0, pad)).reshape(-1, CH))
    return jnp.concatenate(parts, axis=0)


def _unpack_small(p, like):
    out, row = {}, 0
    for n in SMALL_NAMES:
        size = like[n].size
        rows = -(-size // (8 * CH)) * 8
        out[n] = p[row:row + rows].reshape(-1)[:size].reshape(like[n].shape)
        row += rows
    return out


def kernel(x, positions, w_in, ln_v_gain, ln_v_bias, w_spatial, b_spatial, sinks, w_a, w_b, w_o, norm_mix_pre, norm_mix_post, w_ff_in, w_ff_out, norm_ff_pre, norm_ff_post, loss_target, m_w_in, m_ln_v_gain, m_ln_v_bias, m_w_spatial, m_b_spatial, m_sinks, m_w_a, m_w_b, m_w_o, m_norm_mix_pre, m_norm_mix_post, m_w_ff_in, m_w_ff_out, m_norm_ff_pre, m_norm_ff_post, v_w_in, v_ln_v_gain, v_ln_v_bias, v_w_spatial, v_b_spatial, v_sinks, v_w_a, v_w_b, v_w_o, v_norm_mix_pre, v_norm_mix_post, v_w_ff_in, v_w_ff_out, v_norm_ff_pre, v_norm_ff_post):
    w = dict(w_in=w_in, ln_v_gain=ln_v_gain, ln_v_bias=ln_v_bias, w_spatial=w_spatial, b_spatial=b_spatial, sinks=sinks,
             w_a=w_a, w_b=w_b, w_o=w_o, norm_mix_pre=norm_mix_pre, norm_mix_post=norm_mix_post, w_ff_in=w_ff_in,
             w_ff_out=w_ff_out, norm_ff_pre=norm_ff_pre, norm_ff_post=norm_ff_post)
    m = dict(w_in=m_w_in, ln_v_gain=m_ln_v_gain, ln_v_bias=m_ln_v_bias, w_spatial=m_w_spatial, b_spatial=m_b_spatial,
             sinks=m_sinks, w_a=m_w_a, w_b=m_w_b, w_o=m_w_o, norm_mix_pre=m_norm_mix_pre, norm_mix_post=m_norm_mix_post,
             w_ff_in=m_w_ff_in, w_ff_out=m_w_ff_out, norm_ff_pre=m_norm_ff_pre, norm_ff_post=m_norm_ff_post)
    v = dict(w_in=v_w_in, ln_v_gain=v_ln_v_gain, ln_v_bias=v_ln_v_bias, w_spatial=v_w_spatial, b_spatial=v_b_spatial,
             sinks=v_sinks, w_a=v_w_a, w_b=v_w_b, w_o=v_w_o, norm_mix_pre=v_norm_mix_pre, norm_mix_post=v_norm_mix_post,
             w_ff_in=v_w_ff_in, w_ff_out=v_w_ff_out, norm_ff_pre=v_norm_ff_pre, norm_ff_post=v_norm_ff_post)

    FIRST, REST = (0,), tuple(range(1, NBIG))
    shards = [w[n][0].astype(BF16) for n in BIG_NAMES]
    place = jnp.stack([lax.axis_index("c"), 2 * lax.axis_index("x") + lax.axis_index("y")]).astype(jnp.int32)
    xs, target = x[0], loss_target[0]
    T = xs.shape[0]
    wtm, wtm2 = min(1024, T), min(2048, T)
    g1, g2, g3, g4 = norm_mix_pre, norm_mix_post, norm_ff_pre, norm_ff_post
    w_sp, snk = w_spatial[0], sinks[0]
    MIX, FF = (1, 2, 3), (4, 5)
    bfull = jnp.repeat(b_spatial[0].T, CH, axis=1)

    def reduce_tail(ws, grads, got):
        sums = [_add_halves(place, grads[i], got[i], BIG[k][0], name="grad_add_sibling_" + BIG_NAMES[k])
                for i, k in enumerate(ws)]
        return sums, _x_grads_chips([s[1] for s in sums], ws)

    def reduce_end(ws, sums, pieces):
        return [_add_pieces(place, sums[i][0], pieces[i], *BIG[k], name="grad_add_chips_" + BIG_NAMES[k])
                for i, k in enumerate(ws)]

    (rc, rs1, rs2), w_in_part = _rope_tables(positions[0], comm=_x_gather_ici(shards[:1], FIRST))
    w_in_b = _run(_x_gather_d2d(w_in_part, FIRST), "gather_w_in_d2d")[0]
    (h, u, vs, q, k, va, ga, gb), ff_part = _inproj(xs, g1, w_in_b, rc, rs1, rs2, tm=512, comm=_x_gather_ici(shards[4:], FF))
    a, mix_part = _sgu_fwd(u, vs, ln_v_gain, ln_v_bias, w_sp, bfull, tm=512, comm=_x_gather_ici(shards[1:4], MIX))
    att, rest = _attn_fwd(q, k, va, snk, comm=_both(_x_gather_d2d(mix_part, MIX), _x_gather_d2d(ff_part, FF)))
    w_a_b, w_b_b, w_o_b, w_ff_in_b, w_ff_out_b = rest
    pa, pb, merged, mix, x1 = _merge_fwd(a, att, ga, gb, xs, w_a_b, w_b_b, w_o_b, g2, tm=512)
    hf, f2, dff, df1, dx1, lsum, dg3, dg4 = _ffn(x1, target, w_ff_in_b, w_ff_out_b, g3, g4, tm=256)
    loss = lax.psum(0.5 * jnp.sum(lsum) / D, ("x", "y", "c"))

    dw_ff_out, _ = _wgrad(f2, dff, tn=1024, tm=512, name="wgrad_ff_out")
    dw_ff_in, _ = _wgrad(hf, df1, tn=2048, tm=wtm2, name="wgrad_ff_in")
    (dga, dgb, da, datt, dg2, dw_a, dw_b, dw_o), _ = _merge_bwd(
        dx1, mix, ga, gb, pa, pb, a, att, merged, w_a_b, w_b_b, w_o_b, g2, tm=256)
    grads_rest = [dw_a, dw_b, dw_o, dw_ff_in, dw_ff_out]
    (du, dvs, dws, dbs, dlg, dlb), got_rest = _sgu_bwd(
        u, vs, da, ln_v_gain, ln_v_bias, w_sp, bfull, tm=512, comm=_x_grads_sibling(grads_rest, REST))
    sums_rest, to_chips = reduce_tail(REST, grads_rest, got_rest)
    (dq, dk, dva, dsk), pieces_rest = _attn_bwd(q, k, va, datt, snk, rc, rs1, rs2, comm=to_chips)
    partial_rest = reduce_end(REST, sums_rest, pieces_rest)
    (dx, dproj, dg1), _ = _inproj_bwd([du, dvs, dq, dk, dva, dga, dgb], xs, dx1, g1, w_in_b, tm=512)
    small = dict(ln_v_gain=dlg, ln_v_bias=dlb, w_spatial=dws, b_spatial=dbs, sinks=dsk[:, :NQ],
                 norm_mix_pre=dg1, norm_mix_post=dg2, norm_ff_pre=dg3, norm_ff_post=dg4)
    c = lax.axis_index("c")
    h_send = lax.dynamic_slice_in_dim(h, (1 - c) * (D // 2), D // 2, axis=1)
    h_keep = lax.dynamic_slice_in_dim(h, c * (D // 2), D // 2, axis=1)
    dw_in_send, (gs, *shard_rest) = _wgrad(
        h_send, dproj, tn=IN_W // 2, tm=wtm, name="wgrad_in_sibling_half",
        comm=_both(_x_small_all_reduce(_pack_small(small)), _x_grads_share(partial_rest, REST)))
    dw_in_keep, (got_in,) = _wgrad(h_keep, dproj, tn=IN_W // 2, tm=wtm, name="wgrad_in_own_half",
                                   comm=_x_to_sibling(dw_in_send))
    sums_in = [_add_pair(dw_in_keep, got_in, name="grad_add_sibling_w_in")]
    to_chips = _x_grads_chips([sums_in[0][1]], FIRST)
    partial_in = reduce_end(FIRST, sums_in, _run(to_chips, "grads_in_to_chips"))
    shard_grads = list(_run(_x_grads_share(partial_in, FIRST), "grads_in_share")) + list(shard_rest)

    grad, delta, new_m, new_v = {}, {}, {}, {}
    for i, n in enumerate(BIG_NAMES):
        g = shard_grads[i]
        d_, m_, v_ = _adamw(w[n][0], g, m[n][0], v[n][0], tr=256, name="adamw_" + n)
        grad[n], delta[n], new_m[n], new_v[n] = g[None], d_[None], m_[None], v_[None]

    ds, ms, vs = _adamw(_pack_small(w), gs, _pack_small(m), _pack_small(v), tr=SMALL_ROWS // 4, name="adamw_small")
    for packed, dst in ((gs, grad), (ds, delta), (ms, new_m), (vs, new_v)):
        dst.update(_unpack_small(packed, w))

    outs = [loss, dx[None]]
    for group in (grad, delta, new_m, new_v):
        outs.extend(group[n] for n in WEIGHT_ORDER)
    return tuple(outs)
```

```python
import functools

import jax
import jax.numpy as jnp
from jax import lax
from jax.experimental import pallas as pl
from jax.experimental.pallas import tpu as pltpu

F32 = jnp.float32
BF16 = jnp.bfloat16

D = 1024
CH = 128
NG = 8
HD = 64
NQ = 16
NKV = 4
KVW = NKV * HD
DFF = 4 * D
EPS = 1e-6
IN_W = 5632
SEG = (0, 1024, 2048, 3072, 3328, 3584, 4608, 5632)
ROPE_HALF = 8
Q_SCALE = HD ** -0.5

LR, B1, B2, AEPS, WD, STEP = 0.001, 0.9, 0.999, 1e-08, 0.01, 10

VMEM_LIMIT = 60 * 1024 * 1024
MESH = pl.DeviceIdType.MESH

_GELU_C0 = 0.7978845608028654
_GELU_C1 = 0.044715


def _cparams(sem=None):
    kw = dict(vmem_limit_bytes=VMEM_LIMIT)
    if sem is not None:
        kw["dimension_semantics"] = sem
    return pltpu.CompilerParams(**kw)


def _resident(shape):
    nd = len(shape)
    return pl.BlockSpec(shape, lambda *_: (0,) * nd, pipeline_mode=pl.Buffered(1))


def _const(shape):
    nd = len(shape)
    return pl.BlockSpec(shape, lambda *_: (0,) * nd)


def _rows(tm, w):
    return pl.BlockSpec((tm, w), lambda i: (i, 0))


class _Exchange:
    def __init__(self, ins, outs, aliases, scratch, start, finish):
        self.ins, self.outs, self.aliases, self.scratch = list(ins), list(outs), dict(aliases), list(scratch)
        self.start, self.finish = start, finish


def _both(a, b):
    na, ma, sa = len(a.ins), len(a.outs), len(a.scratch)

    def start(ci, co, cs):
        a.start(ci[:na], co[:ma], cs[:sa])
        b.start(ci[na:], co[ma:], cs[sa:])

    def finish(ci, co, cs):
        a.finish(ci[:na], co[:ma], cs[:sa])
        b.finish(ci[na:], co[ma:], cs[sa:])

    aliases = {**a.aliases, **{na + i: ma + j for i, j in b.aliases.items()}}
    return _Exchange(a.ins + b.ins, a.outs + b.outs, aliases, a.scratch + b.scratch, start, finish)


def _call(body, args, *, name, grid, in_specs, out_specs, out_shape, scratch_shapes=(), sem=None, comm=None):
    single = not isinstance(out_shape, (list, tuple))
    out_shape = [out_shape] if single else list(out_shape)
    out_specs = [out_specs] if single else list(out_specs)
    if comm is None:
        res = pl.pallas_call(body, name=name, grid=grid, in_specs=list(in_specs), out_specs=out_specs,
                             out_shape=out_shape, scratch_shapes=list(scratch_shapes),
                             compiler_params=_cparams(sem))(*args)
        return (res[0] if single else res), []
    n_in, n_out, n_scr = len(args), len(out_shape), len(scratch_shapes)
    nci, nco = len(comm.ins), len(comm.outs)
    steps = 1
    for g in grid:
        steps *= g

    def hosted(*refs):
        a, ci = refs[:n_in], refs[n_in:n_in + nci]
        o, co = refs[n_in + nci:n_in + nci + n_out], refs[n_in + nci + n_out:n_in + nci + n_out + nco]
        rest = refs[n_in + nci + n_out + nco:]
        scr, cs = rest[:n_scr], rest[n_scr:]
        step = pl.program_id(0)
        for d in range(1, len(grid)):
            step = step * grid[d] + pl.program_id(d)

        @pl.when(step == 0)
        def _():
            comm.start(ci, co, cs)

        body(*a, *o, *scr)

        @pl.when(step == steps - 1)
        def _():
            comm.finish(ci, co, cs)

    res = pl.pallas_call(
        hosted, name=name, grid=grid, in_specs=list(in_specs) + [ANY] * nci, out_specs=out_specs + [ANY] * nco,
        out_shape=out_shape + comm.outs, scratch_shapes=list(scratch_shapes) + comm.scratch,
        input_output_aliases={n_in + i: n_out + j for i, j in comm.aliases.items()},
        compiler_params=_cparams(("arbitrary",) * len(grid)),
    )(*args, *comm.ins)
    own = res[:n_out]
    return (own[0] if single else own), list(res[n_out:])


def _run(comm, name):
    nci = len(comm.ins)

    def body(*refs):
        ci, co, cs = refs[:nci], refs[nci:nci + len(comm.outs)], refs[nci + len(comm.outs):]
        comm.start(ci, co, cs)
        comm.finish(ci, co, cs)

    return pl.pallas_call(
        body, name=name, in_specs=[ANY] * nci, out_specs=[ANY] * len(comm.outs), out_shape=comm.outs,
        scratch_shapes=comm.scratch, input_output_aliases=comm.aliases,
        compiler_params=pltpu.CompilerParams(vmem_limit_bytes=VMEM_LIMIT),
    )(*comm.ins)


def _gelu(x):
    x2 = x * x
    t = jnp.tanh(x * (_GELU_C0 + (_GELU_C0 * _GELU_C1) * x2))
    hx = 0.5 * x
    return hx + hx * t, (t, x2, hx)


def _gelu_grad(parts):
    t, x2, hx = parts
    return (0.5 + 0.5 * t) + hx * (1.0 - t * t) * (_GELU_C0 + (3.0 * _GELU_C0 * _GELU_C1) * x2)


def _sigmoid(x):
    return 1.0 / (1.0 + jnp.exp(-x))


def _rms_hat(x):
    r = lax.rsqrt(jnp.mean(x * x, axis=-1, keepdims=True) + EPS)
    return x * r, r


def _rms_bwd(xhat, r, g, dout):
    dg = jnp.sum(dout * xhat, axis=0, keepdims=True)
    dy = dout * g
    dx = r * (dy - xhat * jnp.mean(dy * xhat, axis=-1, keepdims=True))
    return dx, dg


def _dot(a, b):
    return jnp.dot(a, b, preferred_element_type=F32)


def _dot_nt(a, b):
    return lax.dot_general(a, b, (((1,), (1,)), ((), ())), preferred_element_type=F32)


def _dot_tn(a, b):
    return lax.dot_general(a, b, (((0,), (0,)), ((), ())), preferred_element_type=F32)


def _rope(blk, c, s1, s2):
    return blk * c + pltpu.roll(blk, CH - ROPE_HALF, 1) * s1 + pltpu.roll(blk, ROPE_HALF, 1) * s2


def _rope_t(blk, c, s1, s2):
    return blk * c + pltpu.roll(blk * s1, ROPE_HALF, 1) + pltpu.roll(blk * s2, CH - ROPE_HALF, 1)


def _inproj(x, g1, w_in, rc, rs1, rs2, tm, comm=None):
    T = x.shape[0]

    def body(x_ref, g_ref, w_ref, c_ref, s1_ref, s2_ref,
             h_ref, u_ref, v_ref, q_ref, k_ref, va_ref, ga_ref, gb_ref):
        xhat, _ = _rms_hat(x_ref[...])
        h = (xhat * g_ref[...]).astype(BF16)
        h_ref[...] = h
        u_ref[...] = _dot(h, w_ref[:, SEG[0]:SEG[1]])
        v_ref[...] = _dot(h, w_ref[:, SEG[1]:SEG[2]])
        c, s1, s2 = c_ref[...], s1_ref[...], s2_ref[...]
        q = _dot(h, w_ref[:, SEG[2]:SEG[3]])
        for p in range(D // CH):
            blk = _rope(q[:, CH * p:CH * (p + 1)], c, s1, s2) * Q_SCALE
            q_ref[:, CH * p:CH * (p + 1)] = blk.astype(BF16)
        k = _dot(h, w_ref[:, SEG[3]:SEG[4]])
        for p in range(KVW // CH):
            k_ref[:, CH * p:CH * (p + 1)] = _rope(k[:, CH * p:CH * (p + 1)], c, s1, s2).astype(BF16)
        va_ref[...] = _dot(h, w_ref[:, SEG[4]:SEG[5]]).astype(BF16)
        ga_ref[...] = _dot(h, w_ref[:, SEG[5]:SEG[6]]).astype(BF16)
        gb_ref[...] = _dot(h, w_ref[:, SEG[6]:SEG[7]]).astype(BF16)

    sd = jax.ShapeDtypeStruct
    return _call(
        body, (x, g1, w_in, rc, rs1, rs2), name="inproj_fwd", grid=(T // tm,),
        in_specs=[_rows(tm, D), _const((1, D)), _resident((D, IN_W)), _rows(tm, CH), _rows(tm, CH), _rows(tm, CH)],
        out_specs=[_rows(tm, D), _rows(tm, D), _rows(tm, D), _rows(tm, D), _rows(tm, KVW), _rows(tm, KVW),
                   _rows(tm, D), _rows(tm, D)],
        out_shape=[sd((T, D), BF16), sd((T, D), F32), sd((T, D), F32), sd((T, D), BF16), sd((T, KVW), BF16),
                   sd((T, KVW), BF16), sd((T, D), BF16), sd((T, D), BF16)],
        sem=("parallel",), comm=comm)


def _sgu_common(u, vs, lng, lnb, ws_ref, bfull):
    nc = u.shape[0] // CH
    ug, tu = _gelu(u)
    vg, tv = _gelu(vs)
    mu = jnp.mean(vg, axis=-1, keepdims=True)
    xc = vg - mu
    rstd = lax.rsqrt(jnp.mean(xc * xc, axis=-1, keepdims=True) + EPS)
    vhat = xc * rstd
    vnb = (vhat * lng + lnb).astype(BF16)
    tri = lax.broadcasted_iota(jnp.int32, (CH, CH), 0) >= lax.broadcasted_iota(jnp.int32, (CH, CH), 1)
    wts, rhss, mixed = [], [], []
    for g in range(NG):
        wt = jnp.where(tri, ws_ref[g], 0.0).astype(BF16)
        rhs = jnp.concatenate([vnb[CH * c:CH * (c + 1), CH * g:CH * (g + 1)] for c in range(nc)], axis=1)
        mix = _dot(wt, rhs)
        wts.append(wt)
        rhss.append(rhs)
        mixed.append([mix[:, CH * c:CH * (c + 1)] + bfull[:, CH * g:CH * (g + 1)] for c in range(nc)])
    return nc, ug, tu, tv, rstd, vhat, tri, wts, rhss, mixed


def _sgu_fwd(u, vs, lng, lnb, ws, bfull, tm, comm=None):
    T = u.shape[0]

    def body(u_ref, v_ref, lng_ref, lnb_ref, ws_ref, bf_ref, a_ref):
        nc, ug, _, _, _, _, _, _, _, mixed = _sgu_common(
            u_ref[...], v_ref[...], lng_ref[...], lnb_ref[...], ws_ref, bf_ref[...])
        mixed_all = jnp.concatenate(
            [jnp.concatenate([mixed[g][c] for g in range(NG)], axis=1) for c in range(nc)], axis=0)
        a_ref[...] = (ug * mixed_all).astype(BF16)

    return _call(
        body, (u, vs, lng, lnb, ws, bfull), name="sgu_fwd", grid=(T // tm,),
        in_specs=[_rows(tm, D), _rows(tm, D), _const((1, D)), _const((1, D)), _const((NG, CH, CH)), _const((CH, D))],
        out_specs=_rows(tm, D), out_shape=jax.ShapeDtypeStruct((T, D), BF16), sem=("parallel",), comm=comm)


def _sgu_bwd(u, vs, da, lng, lnb, ws, bfull, tm, comm=None):
    T = u.shape[0]
    nsteps = T // tm

    def body(u_ref, v_ref, da_ref, lng_ref, lnb_ref, ws_ref, bf_ref,
             du_ref, dv_ref, dws_ref, dbs_ref, dlg_ref, dlb_ref, db_ref):
        i = pl.program_id(0)
        u, vs, da, lng = u_ref[...], v_ref[...], da_ref[...], lng_ref[...]
        nc, ug, tu, tv, rstd, vhat, tri, wts, rhss, mixed = _sgu_common(u, vs, lng, lnb_ref[...], ws_ref, bf_ref[...])

        @pl.when(i == 0)
        def _():
            dws_ref[...] = jnp.zeros_like(dws_ref)
            db_ref[...] = jnp.zeros_like(db_ref)
            dlg_ref[...] = jnp.zeros_like(dlg_ref)
            dlb_ref[...] = jnp.zeros_like(dlb_ref)

        mixed_all = jnp.concatenate(
            [jnp.concatenate([mixed[g][c] for g in range(NG)], axis=1) for c in range(nc)], axis=0)
        du_ref[...] = (da * mixed_all * _gelu_grad(tu)).astype(BF16)
        dmixed = da * ug
        dvn_cols = []
        for g in range(NG):
            dmix = [dmixed[CH * c:CH * (c + 1), CH * g:CH * (g + 1)] for c in range(nc)]
            db_ref[:, CH * g:CH * (g + 1)] += functools.reduce(lambda a, b: a + b, dmix)
            dm = jnp.concatenate(dmix, axis=1).astype(BF16)
            dws_ref[g] += _dot_nt(dm, rhss[g])
            dvn_cols.append(_dot_tn(wts[g], dm))
        dvn = jnp.concatenate(
            [jnp.concatenate([dvn_cols[g][:, CH * c:CH * (c + 1)] for g in range(NG)], axis=1) for c in range(nc)],
            axis=0)
        dlg_ref[...] += jnp.sum(dvn * vhat, axis=0, keepdims=True)
        dlb_ref[...] += jnp.sum(dvn, axis=0, keepdims=True)
        dvh = dvn * lng
        dvg = rstd * (dvh - jnp.mean(dvh, axis=-1, keepdims=True)
                      - vhat * jnp.mean(dvh * vhat, axis=-1, keepdims=True))
        dv_ref[...] = (dvg * _gelu_grad(tv)).astype(BF16)

        @pl.when(i == nsteps - 1)
        def _():
            for g in range(NG):
                dws_ref[g] = jnp.where(tri, dws_ref[g], 0.0)
                dbs_ref[g:g + 1, :] = jnp.sum(db_ref[:, CH * g:CH * (g + 1)].T, axis=0, keepdims=True)

    sd = jax.ShapeDtypeStruct
    return _call(
        body, (u, vs, da, lng, lnb, ws, bfull), name="sgu_bwd", grid=(nsteps,),
        in_specs=[_rows(tm, D), _rows(tm, D), _rows(tm, D), _const((1, D)), _const((1, D)), _const((NG, CH, CH)),
                  _const((CH, D))],
        out_specs=[_rows(tm, D), _rows(tm, D), _const((NG, CH, CH)), _const((NG, CH)), _const((1, D)), _const((1, D))],
        out_shape=[sd((T, D), BF16), sd((T, D), BF16), sd((NG, CH, CH), F32), sd((NG, CH), F32), sd((1, D), F32),
                   sd((1, D), F32)],
        scratch_shapes=[pltpu.VMEM((CH, D), F32)], sem=("arbitrary",), comm=comm)


def _pair_layout(prev, cur, grp):
    j, half = grp // 2, grp % 2
    blk = jnp.concatenate([prev[:, CH * j:CH * (j + 1)], cur[:, CH * j:CH * (j + 1)]], axis=0).astype(F32)
    lo = lax.broadcasted_iota(jnp.int32, blk.shape, 1) < HD
    rolled = pltpu.roll(blk, HD, 1)
    even = jnp.where(lo, blk if half == 0 else rolled, 0.0)
    odd = jnp.where(lo, 0.0, rolled if half == 0 else blk)
    return jnp.concatenate([even, odd], axis=0).astype(BF16)


def _attn_mask(n):
    qi = lax.broadcasted_iota(jnp.int32, (CH, 2 * CH), 0)
    kc = lax.broadcasted_iota(jnp.int32, (CH, 2 * CH), 1)
    ok = (kc > qi) & (kc <= qi + CH) & ((kc >= CH) | (n > 0))
    return jnp.concatenate([ok, ok], axis=1)


def _softmax_sink(s, sink):
    m = jnp.maximum(jnp.max(s, axis=-1, keepdims=True), sink)
    p = jnp.exp(s - m)
    ps = jnp.exp(sink - m)
    inv = 1.0 / (jnp.sum(p, axis=-1, keepdims=True) + ps)
    return p * inv, ps * inv


def _attn_fwd(q, k, va, sinks, comm=None):
    T = q.shape[0]
    nb = T // CH

    def body(sk_ref, q_ref, kp_ref, kc_ref, vp_ref, vc_ref, o_ref):
        n = pl.program_id(0)
        mask = _attn_mask(n)
        kp, kc, vp, vc = kp_ref[...], kc_ref[...], vp_ref[...], vc_ref[...]
        kks = [_pair_layout(kp, kc, grp) for grp in range(NKV)]
        vvs = [_pair_layout(vp, vc, grp) for grp in range(NKV)]
        npairs = D // CH

        def scores(p):
            return _dot_nt(q_ref[:, CH * p:CH * (p + 1)], kks[p // 2])

        ahead = 3
        outs, probs = [], []
        pending = [scores(p) for p in range(ahead)]
        even_lanes = lax.broadcasted_iota(jnp.int32, (CH, CH), 1) < HD

        def unnormalised(s, sink):
            m = jnp.maximum(jnp.max(s, axis=-1, keepdims=True), sink)
            p = jnp.exp(s - m)
            return p, 1.0 / (jnp.sum(p, axis=-1, keepdims=True) + jnp.exp(sink - m))

        def value_product(p):
            pr, ie, io = probs[p]
            return _dot(pr, vvs[p // 2]) * jnp.where(even_lanes, ie, io)

        for p in range(npairs):
            s = jnp.where(mask, pending.pop(0), -1e30)
            if p + ahead < npairs:
                pending.append(scores(p + ahead))
            pe, ie = unnormalised(s[:, :2 * CH], sk_ref[2 * p])
            po, io = unnormalised(s[:, 2 * CH:], sk_ref[2 * p + 1])
            probs.append((jnp.concatenate([pe, po], axis=1).astype(BF16), ie, io))
            if p >= 1:
                outs.append(value_product(p - 1))
        outs.append(value_product(npairs - 1))
        o_ref[...] = jnp.concatenate(outs, axis=1).astype(BF16)

    prev = lambda n: (jnp.maximum(n - 1, 0), 0)
    cur = lambda n: (n, 0)
    return _call(
        body, (sinks, q, k, k, va, va), name="attn_fwd", grid=(nb,),
        in_specs=[pl.BlockSpec(memory_space=pltpu.SMEM), pl.BlockSpec((CH, D), cur),
                  pl.BlockSpec((CH, KVW), prev), pl.BlockSpec((CH, KVW), cur),
                  pl.BlockSpec((CH, KVW), prev), pl.BlockSpec((CH, KVW), cur)],
        out_specs=pl.BlockSpec((CH, D), cur), out_shape=jax.ShapeDtypeStruct((T, D), BF16),
        sem=("parallel",), comm=comm)


def _attn_bwd(q, k, va, datt, sinks, rc, rs1, rs2, comm=None):
    T = q.shape[0]
    nb = T // CH

    def body(sk_ref, q_ref, kp_ref, kc_ref, vp_ref, vc_ref, do_ref, cq_ref, s1q_ref, s2q_ref, ck_ref, s1k_ref, s2k_ref,
             dq_ref, dk_ref, dv_ref, dsk_ref, kcar, vcar):
        n = pl.program_id(0)

        @pl.when(n == 0)
        def _():
            kcar[...] = jnp.zeros_like(kcar)
            vcar[...] = jnp.zeros_like(vcar)
            dsk_ref[...] = jnp.zeros_like(dsk_ref)

        def flush(kprev, vprev):
            ck, s1k, s2k = ck_ref[...], s1k_ref[...], s2k_ref[...]
            for j in range(KVW // CH):
                sl = slice(CH * j, CH * (j + 1))
                dk_ref[:, sl] = _rope_t(kcar[:, sl] + kprev[:, sl], ck, s1k, s2k).astype(BF16)
                dv_ref[:, sl] = (vcar[:, sl] + vprev[:, sl]).astype(BF16)

        @pl.when(n < nb)
        def _():
            mask = _attn_mask(n)
            kp, kc, vp, vc = kp_ref[...], kc_ref[...], vp_ref[...], vc_ref[...]
            cq, s1q, s2q = cq_ref[...], s1q_ref[...], s2q_ref[...]
            lane = lax.broadcasted_iota(jnp.int32, (1, CH), 1)
            dsk = jnp.zeros((1, CH), F32)
            npairs = D // CH
            kks = [_pair_layout(kp, kc, grp) for grp in range(NKV)]
            vvs = [_pair_layout(vp, vc, grp) for grp in range(NKV)]
            qs = [q_ref[:, CH * p:CH * (p + 1)] for p in range(npairs)]
            dos = [do_ref[:, CH * p:CH * (p + 1)].astype(BF16) for p in range(npairs)]

            def first(p):
                return _dot_nt(qs[p], kks[p // 2]), _dot_nt(dos[p], vvs[p // 2])

            def last(p, ds, pb):
                return (_rope_t(_dot(ds, kks[p // 2]), cq, s1q, s2q) * Q_SCALE, _dot_tn(qs[p], ds), _dot_tn(dos[p], pb))

            ahead = 2
            pending = [first(p) for p in range(ahead)]
            mids, ends = [], []
            for p in range(npairs):
                s, dp = pending.pop(0)
                s = jnp.where(mask, s, -1e30)
                if p + ahead < npairs:
                    pending.append(first(p + ahead))
                ds_parts, p_parts = [], []
                for par in range(2):
                    sl = slice(2 * CH * par, 2 * CH * (par + 1))
                    pr, psink = _softmax_sink(s[:, sl], sk_ref[2 * p + par])
                    delta = jnp.sum(pr * dp[:, sl], axis=-1, keepdims=True)
                    ds_parts.append(pr * (dp[:, sl] - delta))
                    p_parts.append(pr)
                    tot = -jnp.sum(psink * delta, axis=0, keepdims=True)
                    dsk = dsk + jnp.where(lane == 2 * p + par, tot, 0.0)
                mids.append((jnp.concatenate(ds_parts, axis=1).astype(BF16), jnp.concatenate(p_parts, axis=1).astype(BF16)))
                if p >= 1:
                    ends.append(last(p - 1, *mids[p - 1]))
            ends.append(last(npairs - 1, *mids[-1]))
            dq_cols = [e[0] for e in ends]
            def fold(i):
                rows = []
                for grp in range(NKV):
                    acc = ends[2 * grp][i] + ends[2 * grp + 1][i]
                    rows.append(acc[:HD, :2 * CH] + acc[HD:, 2 * CH:])
                return jnp.concatenate(rows, axis=0).T

            dkf, dvf = fold(1), fold(2)
            dq_ref[...] = jnp.concatenate(dq_cols, axis=1).astype(BF16)
            dsk_ref[...] += dsk
            flush(dkf[:CH], dvf[:CH])
            kcar[...] = dkf[CH:]
            vcar[...] = dvf[CH:]

        @pl.when(n == nb)
        def _():
            z = jnp.zeros((CH, KVW), F32)
            flush(z, z)

    last = nb - 1
    cur = lambda n: (jnp.minimum(n, last), 0)
    prev = lambda n: (jnp.clip(n - 1, 0, last), 0)
    sd = jax.ShapeDtypeStruct
    return _call(
        body, (sinks, q, k, k, va, va, datt, rc, rs1, rs2, rc, rs1, rs2), name="attn_bwd", grid=(nb + 1,),
        in_specs=[pl.BlockSpec(memory_space=pltpu.SMEM), pl.BlockSpec((CH, D), cur),
                  pl.BlockSpec((CH, KVW), prev), pl.BlockSpec((CH, KVW), cur),
                  pl.BlockSpec((CH, KVW), prev), pl.BlockSpec((CH, KVW), cur),
                  pl.BlockSpec((CH, D), cur),
                  pl.BlockSpec((CH, CH), cur), pl.BlockSpec((CH, CH), cur), pl.BlockSpec((CH, CH), cur),
                  pl.BlockSpec((CH, CH), prev), pl.BlockSpec((CH, CH), prev), pl.BlockSpec((CH, CH), prev)],
        out_specs=[pl.BlockSpec((CH, D), cur), pl.BlockSpec((CH, KVW), prev), pl.BlockSpec((CH, KVW), prev),
                   _const((1, CH))],
        out_shape=[sd((T, D), BF16), sd((T, KVW), BF16), sd((T, KVW), BF16), sd((1, CH), F32)],
        scratch_shapes=[pltpu.VMEM((CH, KVW), F32), pltpu.VMEM((CH, KVW), F32)], sem=("arbitrary",), comm=comm)


def _merge_fwd(a, att, ga, gb, x, w_a, w_b, w_o, g2, tm):
    T = x.shape[0]

    def body(a_ref, att_ref, ga_ref, gb_ref, x_ref, wa_ref, wb_ref, wo_ref, g_ref,
             pa_ref, pb_ref, mg_ref, mix_ref, x1_ref):
        pa = _dot(a_ref[...], wa_ref[...])
        pb = _dot(att_ref[...], wb_ref[...])
        pa_ref[...] = pa.astype(BF16)
        pb_ref[...] = pb.astype(BF16)
        merged = (_sigmoid(ga_ref[...].astype(F32)) * pa + _sigmoid(gb_ref[...].astype(F32)) * pb).astype(BF16)
        mg_ref[...] = merged
        mix = _dot(merged, wo_ref[...])
        mix_ref[...] = mix
        mhat, _ = _rms_hat(mix)
        x1_ref[...] = x_ref[...] + mhat * g_ref[...]

    sd = jax.ShapeDtypeStruct
    return pl.pallas_call(
        body, name="merge_fwd", grid=(T // tm,),
        in_specs=[_rows(tm, D)] * 5 + [_resident((D, D))] * 3 + [_const((1, D))],
        out_specs=[_rows(tm, D)] * 5,
        out_shape=[sd((T, D), BF16), sd((T, D), BF16), sd((T, D), BF16), sd((T, D), F32), sd((T, D), F32)],
        compiler_params=_cparams(("parallel",)),
    )(a, att, ga, gb, x, w_a, w_b, w_o, g2)


def _merge_bwd(dx1, mix, ga, gb, pa, pb, a, att, merged, w_a, w_b, w_o, g2, tm, comm=None):
    T = dx1.shape[0]
    nsteps = T // tm

    def body(dx1_ref, mix_ref, ga_ref, gb_ref, pa_ref, pb_ref, a_ref, att_ref, mg_ref, wa_ref, wb_ref, wo_ref, g_ref,
             dga_ref, dgb_ref, da_ref, datt_ref, dg_ref, dwa_ref, dwb_ref, dwo_ref, acc, sem):
        i = pl.program_id(0)

        @pl.when(i == 0)
        def _():
            dg_ref[...] = jnp.zeros_like(dg_ref)
            acc[...] = jnp.zeros_like(acc)

        mhat, r = _rms_hat(mix_ref[...])
        dmix, dg = _rms_bwd(mhat, r, g_ref[...], dx1_ref[...])
        dg_ref[...] += dg
        dmix = dmix.astype(BF16)
        dmerged = _dot_nt(dmix, wo_ref[...])
        sa = _sigmoid(ga_ref[...].astype(F32))
        sb = _sigmoid(gb_ref[...].astype(F32))
        dao = (dmerged * sa).astype(BF16)
        dbo = (dmerged * sb).astype(BF16)
        dga_ref[...] = (dmerged * pa_ref[...].astype(F32) * (sa * (1.0 - sa))).astype(BF16)
        dgb_ref[...] = (dmerged * pb_ref[...].astype(F32) * (sb * (1.0 - sb))).astype(BF16)
        da_ref[...] = _dot_nt(dao, wa_ref[...])
        datt_ref[...] = _dot_nt(dbo, wb_ref[...]).astype(BF16)
        acc[0] += _dot_tn(a_ref[...], dao)
        acc[1] += _dot_tn(att_ref[...], dbo)
        acc[2] += _dot_tn(mg_ref[...], dmix)

        @pl.when(i == nsteps - 1)
        def _():
            outs = [pltpu.make_async_copy(acc.at[j], ref, sem.at[j]) for j, ref in enumerate((dwa_ref, dwb_ref, dwo_ref))]
            for cp in outs:
                cp.start()
            for cp in outs:
                cp.wait()

    sd = jax.ShapeDtypeStruct
    return _call(
        body, (dx1, mix, ga, gb, pa, pb, a, att, merged, w_a, w_b, w_o, g2), name="merge_bwd", grid=(nsteps,),
        in_specs=[_rows(tm, D)] * 9 + [_resident((D, D))] * 3 + [_const((1, D))],
        out_specs=[_rows(tm, D)] * 4 + [_const((1, D))] + [ANY] * 3,
        out_shape=[sd((T, D), BF16), sd((T, D), BF16), sd((T, D), F32), sd((T, D), BF16), sd((1, D), F32)]
        + [sd((D, D), F32)] * 3,
        scratch_shapes=[pltpu.VMEM((3, D, D), F32), _dma_sems(3)], sem=("arbitrary",), comm=comm)


def _ffn(x1, target, w1, w2, g3, g4, tm):
    T = x1.shape[0]

    def body(x_ref, t_ref, w1_ref, w2_ref, g3_ref, g4_ref,
             hf_ref, f2_ref, dff_ref, df1_ref, dx_ref, ls_ref, dg3_ref, dg4_ref):
        @pl.when(pl.program_id(0) == 0)
        def _():
            ls_ref[...] = jnp.zeros_like(ls_ref)
            dg3_ref[...] = jnp.zeros_like(dg3_ref)
            dg4_ref[...] = jnp.zeros_like(dg4_ref)

        x = x_ref[...]
        g3, g4 = g3_ref[...], g4_ref[...]
        xhat, r3 = _rms_hat(x)
        hf = (xhat * g3).astype(BF16)
        hf_ref[...] = hf
        rl = jnp.maximum(_dot(hf, w1_ref[...]), 0.0)
        f2 = (rl * rl).astype(BF16)
        f2_ref[...] = f2
        fhat, r4 = _rms_hat(_dot(f2, w2_ref[...]))
        err = x + fhat * g4 - t_ref[...]
        ls_ref[...] += jnp.sum(err * err, axis=0, keepdims=True)
        dy = err * (1.0 / D)
        dff, dg4 = _rms_bwd(fhat, r4, g4, dy)
        dg4_ref[...] += dg4
        dff = dff.astype(BF16)
        dff_ref[...] = dff
        df1 = (_dot_nt(dff, w2_ref[...]) * (2.0 * rl)).astype(BF16)
        df1_ref[...] = df1
        dxn, dg3 = _rms_bwd(xhat, r3, g3, _dot_nt(df1, w1_ref[...]))
        dg3_ref[...] += dg3
        dx_ref[...] = dy + dxn

    sd = jax.ShapeDtypeStruct
    return pl.pallas_call(
        body, name="ffn_fwd_bwd", grid=(T // tm,),
        in_specs=[_rows(tm, D), _rows(tm, D), _resident((D, DFF)), _resident((DFF, D)), _const((1, D)), _const((1, D))],
        out_specs=[_rows(tm, D), _rows(tm, DFF), _rows(tm, D), _rows(tm, DFF), _rows(tm, D), _const((1, D)),
                   _const((1, D)), _const((1, D))],
        out_shape=[sd((T, D), BF16), sd((T, DFF), BF16), sd((T, D), BF16), sd((T, DFF), BF16), sd((T, D), F32),
                   sd((1, D), F32), sd((1, D), F32), sd((1, D), F32)],
        compiler_params=_cparams(("arbitrary",)),
    )(x1, target, w1, w2, g3, g4)


def _inproj_bwd(parts, x, dx1, g1, w_in, tm, comm=None):
    T = x.shape[0]
    widths = [p.shape[1] for p in parts]
    offs = [sum(widths[:i]) for i in range(len(widths) + 1)]
    assert offs[-1] == IN_W

    def body(*refs):
        n = len(parts)
        prefs = refs[:n]
        x_ref, dx1_ref, g_ref, w_ref, dx_ref, dp_ref, dg_ref = refs[n:]

        @pl.when(pl.program_id(0) == 0)
        def _():
            dg_ref[...] = jnp.zeros_like(dg_ref)

        for i in range(n):
            dp_ref[:, offs[i]:offs[i + 1]] = prefs[i][...]
        dh = _dot_nt(dp_ref[...], w_ref[...])
        xhat, r = _rms_hat(x_ref[...])
        dxn, dg = _rms_bwd(xhat, r, g_ref[...], dh)
        dg_ref[...] += dg
        dx_ref[...] = dx1_ref[...] + dxn

    sd = jax.ShapeDtypeStruct
    return _call(
        body, (*parts, x, dx1, g1, w_in), name="inproj_bwd", grid=(T // tm,),
        in_specs=[_rows(tm, w) for w in widths] + [_rows(tm, D), _rows(tm, D), _const((1, D)), _resident((D, IN_W))],
        out_specs=[_rows(tm, D), _rows(tm, IN_W), _const((1, D))],
        out_shape=[sd((T, D), F32), sd((T, IN_W), BF16), sd((1, D), F32)], sem=("arbitrary",), comm=comm)


def _wgrad(a, g, tn, tm, name, comm=None):
    T, K = a.shape
    N = g.shape[1]

    def body(a_ref, g_ref, o_ref):
        @pl.when(pl.program_id(1) == 0)
        def _():
            o_ref[...] = jnp.zeros_like(o_ref)

        o_ref[...] += _dot_tn(a_ref[...], g_ref[...])

    return _call(
        body, (a, g), name=name, grid=(N // tn, T // tm),
        in_specs=[pl.BlockSpec((tm, K), lambda j, t: (t, 0)), pl.BlockSpec((tm, tn), lambda j, t: (t, j))],
        out_specs=pl.BlockSpec((K, tn), lambda j, t: (0, j)),
        out_shape=jax.ShapeDtypeStruct((K, N), F32), sem=("parallel", "arbitrary"), comm=comm)


def _adamw(w, g, m, v, tr, name):
    R, C = w.shape
    bc1 = 1.0 / (1.0 - B1 ** STEP)
    bc2 = 1.0 / (1.0 - B2 ** STEP)

    def body(w_ref, g_ref, m_ref, v_ref, d_ref, nm_ref, nv_ref):
        g = g_ref[...]
        m = B1 * m_ref[...] + (1.0 - B1) * g
        v = B2 * v_ref[...] + (1.0 - B2) * (g * g)
        nm_ref[...] = m
        nv_ref[...] = v
        d_ref[...] = -LR * ((m * bc1) / (jnp.sqrt(v * bc2) + AEPS) + WD * w_ref[...])

    spec = pl.BlockSpec((tr, C), lambda i: (i, 0))
    return pl.pallas_call(
        body, name=name, grid=(R // tr,), in_specs=[spec] * 4, out_specs=[spec] * 3,
        out_shape=[jax.ShapeDtypeStruct((R, C), F32)] * 3,
        compiler_params=_cparams(("parallel",)),
    )(w, g, m, v)


BIG = (("col", (D, IN_W)), ("row", (D, D)), ("row", (D, D)), ("row", (D, D)), ("col", (D, DFF)), ("row", (DFF, D)))
NBIG = len(BIG)
ANY = pl.BlockSpec(memory_space=pl.ANY)


def _shard_shape(kind, shape):
    R, C = shape
    return (R, C // 4) if kind == "col" else (R // 4, C)


def _half_shape(kind, shape):
    R, C = shape
    return (R // 2, C) if kind == "col" else (R, C // 2)


def _piece_shape(kind, shape):
    R, C = shape
    return (R // 2, C // 4) if kind == "col" else (R // 4, C // 2)


def _own_region(ref, kind, shape, s):
    R, C = shape
    return ref.at[:, pl.ds(s * (C // 4), C // 4)] if kind == "col" else ref.at[pl.ds(s * (R // 4), R // 4), :]


def _ag_region(ref, kind, shape, s, hc):
    R, C = shape
    if kind == "col":
        return ref.at[pl.ds(hc * (R // 2), R // 2), pl.ds(s * (C // 4), C // 4)]
    return ref.at[pl.ds(s * (R // 4) + hc * (R // 8), R // 8), :]


def _ag_shard_half(ref, kind, shape, hc):
    R, C = shape
    return ref.at[pl.ds(hc * (R // 2), R // 2), :] if kind == "col" else ref.at[pl.ds(hc * (R // 8), R // 8), :]


def _grad_half(ref, kind, shape, hc):
    R, C = shape
    return ref.at[pl.ds(hc * (R // 2), R // 2), :] if kind == "col" else ref.at[:, pl.ds(hc * (C // 2), C // 2)]


def _half_piece(ref, kind, shape, s):
    R, C = shape
    return ref.at[:, pl.ds(s * (C // 4), C // 4)] if kind == "col" else ref.at[pl.ds(s * (R // 4), R // 4), :]


def _place():
    x, y, c = lax.axis_index("x"), lax.axis_index("y"), lax.axis_index("c")
    chips = [(1 - x, y), (x, 1 - y), (1 - x, 1 - y)]
    return x, y, c, chips


def _rcopy(src, dst, ssem, rsem, dev):
    return pltpu.make_async_remote_copy(src_ref=src, dst_ref=dst, send_sem=ssem, recv_sem=rsem,
                                        device_id=dev, device_id_type=MESH)


def _dma_sems(n):
    return pltpu.SemaphoreType.DMA((n,))


def _x_gather_ici(shards, ws):
    n = len(ws)
    specs = [BIG[w] for w in ws]

    def place():
        x, y, c, chips = _place()
        return c, chips, 2 * x + y

    def sends(sh, full, sc):
        c, chips, me_s = place()
        return [_rcopy(_ag_shard_half(sh[i], kind, shape, c), _ag_region(full[i], kind, shape, me_s, c),
                       sc[0].at[3 * i + j], sc[1].at[3 * i + j], (cx, cy, c))
                for i, (kind, shape) in enumerate(specs) for j, (cx, cy) in enumerate(chips)]

    def start(sh, full, sc):
        for i in range(n):
            pltpu.make_async_copy(sh[i], sc[4 + i], sc[2].at[i]).start()
        for cp in sends(sh, full, sc):
            cp.start()

    def finish(sh, full, sc):
        c, chips, me_s = place()
        stores = []
        for i, (kind, shape) in enumerate(specs):
            pltpu.make_async_copy(sh[i], sc[4 + i], sc[2].at[i]).wait()
            st = pltpu.make_async_copy(sc[4 + i], _own_region(full[i], kind, shape, me_s), sc[3].at[i])
            st.start()
            stores.append(st)
        for i, (kind, shape) in enumerate(specs):
            for j, (cx, cy) in enumerate(chips):
                reg = _ag_region(full[i], kind, shape, 2 * cx + cy, c)
                _rcopy(reg, reg, sc[0].at[3 * i + j], sc[1].at[3 * i + j], (cx, cy, c)).wait_recv()
        for cp in sends(sh, full, sc):
            cp.wait_send()
        for st in stores:
            st.wait()

    return _Exchange(
        shards, [jax.ShapeDtypeStruct(shape, BF16) for _, shape in specs], {},
        [_dma_sems(3 * n), _dma_sems(3 * n), _dma_sems(n), _dma_sems(n)]
        + [pltpu.VMEM(_shard_shape(k, s), BF16) for k, s in specs], start, finish)


def _x_gather_d2d(wholes, ws):
    specs = [BIG[w] for w in ws]
    n = len(ws)

    def copies(full, sc, mine):
        x, y, c, chips = _place()
        hc = c if mine else 1 - c
        return [_rcopy(reg, reg, sc[0].at[3 * i + j], sc[1].at[3 * i + j], (x, y, 1 - c))
                for i, (kind, shape) in enumerate(specs) for j, (cx, cy) in enumerate(chips)
                for reg in [_ag_region(full[i], kind, shape, 2 * cx + cy, hc)]]

    def start(_, full, sc):
        for cp in copies(full, sc, True):
            cp.start()

    def finish(_, full, sc):
        for cp in copies(full, sc, False):
            cp.wait_recv()
        for cp in copies(full, sc, True):
            cp.wait_send()

    return _Exchange(wholes, [jax.ShapeDtypeStruct(shape, BF16) for _, shape in specs], {i: i for i in range(n)},
                     [_dma_sems(3 * n), _dma_sems(3 * n)], start, finish)


def _x_grads_sibling(grads, ws):
    specs = [BIG[w] for w in ws]
    n = len(ws)

    def copies(g, got, sc):
        x, y, c, _ = _place()
        return [_rcopy(_grad_half(g[i], kind, shape, 1 - c), got[i], sc[0].at[i], sc[1].at[i], (x, y, 1 - c))
                for i, (kind, shape) in enumerate(specs)]

    def start(g, got, sc):
        for cp in copies(g, got, sc):
            cp.start()

    def finish(g, got, sc):
        for cp in copies(g, got, sc):
            cp.wait_recv()
        for cp in copies(g, got, sc):
            cp.wait_send()

    return _Exchange(grads, [jax.ShapeDtypeStruct(_half_shape(k, s), F32) for k, s in specs], {},
                     [_dma_sems(n), _dma_sems(n)], start, finish)


def _x_grads_chips(sums_bf, ws):
    specs = [BIG[w] for w in ws]
    n = len(ws)

    def copies(s16, got, sc):
        x, y, c, chips = _place()
        return [_rcopy(_half_piece(s16[i], kind, shape, 2 * cx + cy), got[i].at[j],
                       sc[0].at[3 * i + j], sc[1].at[3 * i + j], (cx, cy, c))
                for i, (kind, shape) in enumerate(specs) for j, (cx, cy) in enumerate(chips)]

    def start(s16, got, sc):
        for cp in copies(s16, got, sc):
            cp.start()

    def finish(s16, got, sc):
        for cp in copies(s16, got, sc):
            cp.wait_recv()
        for cp in copies(s16, got, sc):
            cp.wait_send()

    return _Exchange(sums_bf, [jax.ShapeDtypeStruct((3,) + _piece_shape(k, s), BF16) for k, s in specs], {},
                     [_dma_sems(3 * n), _dma_sems(3 * n)], start, finish)


def _shard_half(ref, kind, shape, hc):
    sr, sc = _shard_shape(kind, shape)
    return ref.at[pl.ds(hc * (sr // 2), sr // 2), :] if kind == "col" else ref.at[:, pl.ds(hc * (sc // 2), sc // 2)]


def _x_grads_share(shard_grads, ws):
    specs = [BIG[w] for w in ws]
    n = len(ws)

    def copies(g, sc, mine):
        x, y, c, _ = _place()
        hc = c if mine else 1 - c
        return [_rcopy(part, part, sc[0].at[i], sc[1].at[i], (x, y, 1 - c))
                for i, (kind, shape) in enumerate(specs) for part in [_shard_half(g[i], kind, shape, hc)]]

    def start(_, g, sc):
        for cp in copies(g, sc, True):
            cp.start()

    def finish(_, g, sc):
        for cp in copies(g, sc, False):
            cp.wait_recv()
        for cp in copies(g, sc, True):
            cp.wait_send()

    return _Exchange(shard_grads, [jax.ShapeDtypeStruct(_shard_shape(k, s), F32) for k, s in specs],
                     {i: i for i in range(n)}, [_dma_sems(n), _dma_sems(n)], start, finish)


ADD_BLOCK_BYTES = 4 * 1024 * 1024


def _add_rows(rows, cols):
    r = rows
    while r > 256 and r * cols * 4 > ADD_BLOCK_BYTES:
        r //= 2
    return r


def _add_halves(place, g, got, kind, name):
    R, C = g.shape
    hr, hcols = _half_shape(kind, (R, C))
    blk_rows = _add_rows(hr, hcols)
    steps = hr // blk_rows

    def body(p_ref, g_ref, b_ref, s_ref, sb_ref):
        s = g_ref[...] + b_ref[...]
        s_ref[...] = s
        sb_ref[...] = s.astype(BF16)

    if kind == "col":
        g_spec = pl.BlockSpec((blk_rows, C), lambda i, p: (p[0] * steps + i, 0))
    else:
        g_spec = pl.BlockSpec((blk_rows, hcols), lambda i, p: (i, p[0]))
    spec = pl.BlockSpec((blk_rows, hcols), lambda i, p: (i, 0))
    return pl.pallas_call(
        body, name=name,
        grid_spec=pltpu.PrefetchScalarGridSpec(num_scalar_prefetch=1, grid=(steps,), in_specs=[g_spec, spec],
                                               out_specs=[spec, spec]),
        out_shape=[jax.ShapeDtypeStruct((hr, hcols), F32), jax.ShapeDtypeStruct((hr, hcols), BF16)],
        compiler_params=_cparams(("parallel",)),
    )(place, g, got)


def _add_pieces(place, half, got, kind, shape, name):
    pr, pc = _piece_shape(kind, shape)
    blk_rows = _add_rows(pr, pc)
    steps = pr // blk_rows

    def body(p_ref, m_ref, g_ref, o_ref):
        acc = m_ref[...]
        for j in range(3):
            acc = acc + g_ref[j].astype(F32)
        o_ref[...] = acc

    if kind == "col":
        m_spec = pl.BlockSpec((blk_rows, pc), lambda i, p: (i, p[1]))
        o_spec = pl.BlockSpec((blk_rows, pc), lambda i, p: (p[0] * steps + i, 0))
    else:
        m_spec = pl.BlockSpec((blk_rows, pc), lambda i, p: (p[1] * steps + i, 0))
        o_spec = pl.BlockSpec((blk_rows, pc), lambda i, p: (i, p[0]))
    return pl.pallas_call(
        body, name=name,
        grid_spec=pltpu.PrefetchScalarGridSpec(
            num_scalar_prefetch=1, grid=(steps,),
            in_specs=[m_spec, pl.BlockSpec((3, blk_rows, pc), lambda i, p: (0, i, 0))], out_specs=o_spec),
        out_shape=jax.ShapeDtypeStruct(_shard_shape(kind, shape), F32),
        compiler_params=_cparams(("parallel",)),
    )(place, half, got)


SMALL_ROWS = 1024 + 8 * 8 + 8


def _x_small_all_reduce(p):
    def parts(p_ref, sc):
        slots, ssem, rsem = sc[0], sc[2], sc[3]
        x, y, c = lax.axis_index("x"), lax.axis_index("y"), lax.axis_index("c")
        me = 4 * x + 2 * y + c
        out = []
        for r in range(1, 8):
            bx, by, bc = (r >> 2) & 1, (r >> 1) & 1, r & 1
            tgt = (1 - x if bx else x, 1 - y if by else y, 1 - c if bc else c)
            send = _rcopy(p_ref, slots.at[me], ssem.at[r - 1], rsem.at[r - 1], tgt)
            src = 4 * tgt[0] + 2 * tgt[1] + tgt[2]
            recv = _rcopy(p_ref, slots.at[src], ssem.at[r - 1], rsem.at[r - 1], tgt)
            out.append((send, recv))
        return me, out

    def start(ins, outs, sc):
        me, cps = parts(ins[0], sc)
        pltpu.make_async_copy(ins[0], sc[0].at[me], sc[4].at[0]).start()
        for send, _ in cps:
            send.start()

    def finish(ins, outs, sc):
        me, cps = parts(ins[0], sc)
        pltpu.make_async_copy(ins[0], sc[0].at[me], sc[4].at[0]).wait()
        for _, recv in cps:
            recv.wait_recv()
        acc = sc[0][0]
        for d in range(1, 8):
            acc = acc + sc[0][d]
        sc[1][...] = acc
        back = pltpu.make_async_copy(sc[1], outs[0], sc[4].at[1])
        back.start()
        for send, _ in cps:
            send.wait_send()
        back.wait()

    return _Exchange([p], [jax.ShapeDtypeStruct((SMALL_ROWS, CH), F32)], {},
                     [pltpu.VMEM((8, SMALL_ROWS, CH), F32), pltpu.VMEM((SMALL_ROWS, CH), F32), _dma_sems(7), _dma_sems(7),
                      _dma_sems(2)], start, finish)


def _rope_tables(positions, comm=None):
    T = positions.shape[0]
    inv_freq = 500000.0 ** (-jnp.arange(0, 2 * ROPE_HALF, 2, dtype=F32) / (2 * ROPE_HALF))
    head = jnp.concatenate([inv_freq, inv_freq, jnp.zeros((HD - 2 * ROPE_HALF,), F32)])
    lane_freq = jnp.concatenate([head, head])[None, :]
    pos = jnp.broadcast_to(positions.astype(F32)[:, None], (T, CH))
    tm = min(1024, T)

    def body(p_ref, f_ref, c_ref, s1_ref, s2_ref):
        ang = p_ref[...] * f_ref[...]
        sin = jnp.sin(ang)
        first = (lax.broadcasted_iota(jnp.int32, ang.shape, 1) % HD) < ROPE_HALF
        c_ref[...] = jnp.cos(ang)
        s1_ref[...] = jnp.where(first, -sin, 0.0)
        s2_ref[...] = jnp.where(first, 0.0, sin)

    return _call(body, (pos, lane_freq), name="rope_tables", grid=(T // tm,),
                 in_specs=[_rows(tm, CH), _const((1, CH))], out_specs=[_rows(tm, CH)] * 3,
                 out_shape=[jax.ShapeDtypeStruct((T, CH), F32)] * 3, sem=("parallel",), comm=comm)


BIG_NAMES = ("w_in", "w_a", "w_b", "w_o", "w_ff_in", "w_ff_out")
SMALL_NAMES = ("w_spatial", "ln_v_gain", "ln_v_bias", "b_spatial", "sinks", "norm_mix_pre", "norm_mix_post",
               "norm_ff_pre", "norm_ff_post")
WEIGHT_ORDER = ("w_in", "ln_v_gain", "ln_v_bias", "w_spatial", "b_spatial", "sinks", "w_a", "w_b", "w_o",
                "norm_mix_pre", "norm_mix_post", "w_ff_in", "w_ff_out", "norm_ff_pre", "norm_ff_post")


def _pack_small(d, loss_sums=None):
    parts = []
    for n in SMALL_NAMES:
        flat = d[n].reshape(-1)
        pad = (-flat.shape[0]) % (8 * CH)
        parts.append(jnp.pad(flat, (0, pad)).reshape(-1, CH))
    parts.append(jnp.zeros((8, CH), F32) if loss_sums is None else loss_sums.reshape(8, CH))
    return jnp.concatenate(parts, axis=0)


def _unpack_small(p, like):
    out, row = {}, 0
    for n in SMALL_NAMES:
        size = like[n].size
        rows = -(-size // (8 * CH)) * 8
        out[n] = p[row:row + rows].reshape(-1)[:size].reshape(like[n].shape)
        row += rows
    return out


def kernel(x, positions, w_in, ln_v_gain, ln_v_bias, w_spatial, b_spatial, sinks, w_a, w_b, w_o, norm_mix_pre, norm_mix_post, w_ff_in, w_ff_out, norm_ff_pre, norm_ff_post, loss_target, m_w_in, m_ln_v_gain, m_ln_v_bias, m_w_spatial, m_b_spatial, m_sinks, m_w_a, m_w_b, m_w_o, m_norm_mix_pre, m_norm_mix_post, m_w_ff_in, m_w_ff_out, m_norm_ff_pre, m_norm_ff_post, v_w_in, v_ln_v_gain, v_ln_v_bias, v_w_spatial, v_b_spatial, v_sinks, v_w_a, v_w_b, v_w_o, v_norm_mix_pre, v_norm_mix_post, v_w_ff_in, v_w_ff_out, v_norm_ff_pre, v_norm_ff_post):
    w = dict(w_in=w_in, ln_v_gain=ln_v_gain, ln_v_bias=ln_v_bias, w_spatial=w_spatial, b_spatial=b_spatial, sinks=sinks,
             w_a=w_a, w_b=w_b, w_o=w_o, norm_mix_pre=norm_mix_pre, norm_mix_post=norm_mix_post, w_ff_in=w_ff_in,
             w_ff_out=w_ff_out, norm_ff_pre=norm_ff_pre, norm_ff_post=norm_ff_post)
    m = dict(w_in=m_w_in, ln_v_gain=m_ln_v_gain, ln_v_bias=m_ln_v_bias, w_spatial=m_w_spatial, b_spatial=m_b_spatial,
             sinks=m_sinks, w_a=m_w_a, w_b=m_w_b, w_o=m_w_o, norm_mix_pre=m_norm_mix_pre, norm_mix_post=m_norm_mix_post,
             w_ff_in=m_w_ff_in, w_ff_out=m_w_ff_out, norm_ff_pre=m_norm_ff_pre, norm_ff_post=m_norm_ff_post)
    v = dict(w_in=v_w_in, ln_v_gain=v_ln_v_gain, ln_v_bias=v_ln_v_bias, w_spatial=v_w_spatial, b_spatial=v_b_spatial,
             sinks=v_sinks, w_a=v_w_a, w_b=v_w_b, w_o=v_w_o, norm_mix_pre=v_norm_mix_pre, norm_mix_post=v_norm_mix_post,
             w_ff_in=v_w_ff_in, w_ff_out=v_w_ff_out, norm_ff_pre=v_norm_ff_pre, norm_ff_post=v_norm_ff_post)

    FIRST, REST = (0,), tuple(range(1, NBIG))
    shards = [w[n][0].astype(BF16) for n in BIG_NAMES]
    place = jnp.stack([lax.axis_index("c"), 2 * lax.axis_index("x") + lax.axis_index("y")]).astype(jnp.int32)
    xs, target = x[0], loss_target[0]
    T = xs.shape[0]
    wtm, wtm2 = min(1024, T), min(2048, T)
    g1, g2, g3, g4 = norm_mix_pre, norm_mix_post, norm_ff_pre, norm_ff_post
    w_sp, snk = w_spatial[0], sinks[0]
    MIX, FF = (1, 2, 3), (4, 5)
    bfull = jnp.repeat(b_spatial[0].T, CH, axis=1)

    def reduce_tail(ws, grads, got):
        sums = [_add_halves(place, grads[i], got[i], BIG[k][0], name="grad_add_sibling_" + BIG_NAMES[k])
                for i, k in enumerate(ws)]
        return sums, _x_grads_chips([s[1] for s in sums], ws)

    def reduce_end(ws, sums, pieces):
        return [_add_pieces(place, sums[i][0], pieces[i], *BIG[k], name="grad_add_chips_" + BIG_NAMES[k])
                for i, k in enumerate(ws)]

    (rc, rs1, rs2), w_in_part = _rope_tables(positions[0], comm=_x_gather_ici(shards[:1], FIRST))
    w_in_b = _run(_x_gather_d2d(w_in_part, FIRST), "gather_w_in_d2d")[0]
    (h, u, vs, q, k, va, ga, gb), ff_part = _inproj(xs, g1, w_in_b, rc, rs1, rs2, tm=512, comm=_x_gather_ici(shards[4:], FF))
    a, mix_part = _sgu_fwd(u, vs, ln_v_gain, ln_v_bias, w_sp, bfull, tm=512, comm=_x_gather_ici(shards[1:4], MIX))
    att, rest = _attn_fwd(q, k, va, snk, comm=_both(_x_gather_d2d(mix_part, MIX), _x_gather_d2d(ff_part, FF)))
    w_a_b, w_b_b, w_o_b, w_ff_in_b, w_ff_out_b = rest
    pa, pb, merged, mix, x1 = _merge_fwd(a, att, ga, gb, xs, w_a_b, w_b_b, w_o_b, g2, tm=512)
    hf, f2, dff, df1, dx1, lsum, dg3, dg4 = _ffn(x1, target, w_ff_in_b, w_ff_out_b, g3, g4, tm=256)

    dw_ff_out, _ = _wgrad(f2, dff, tn=1024, tm=512, name="wgrad_ff_out")
    dw_ff_in, _ = _wgrad(hf, df1, tn=2048, tm=wtm2, name="wgrad_ff_in")
    (dga, dgb, da, datt, dg2, dw_a, dw_b, dw_o), _ = _merge_bwd(
        dx1, mix, ga, gb, pa, pb, a, att, merged, w_a_b, w_b_b, w_o_b, g2, tm=256)
    grads_rest = [dw_a, dw_b, dw_o, dw_ff_in, dw_ff_out]
    (du, dvs, dws, dbs, dlg, dlb), got_rest = _sgu_bwd(
        u, vs, da, ln_v_gain, ln_v_bias, w_sp, bfull, tm=512, comm=_x_grads_sibling(grads_rest, REST))
    sums_rest, to_chips = reduce_tail(REST, grads_rest, got_rest)
    (dq, dk, dva, dsk), pieces_rest = _attn_bwd(q, k, va, datt, snk, rc, rs1, rs2, comm=to_chips)
    partial_rest = reduce_end(REST, sums_rest, pieces_rest)
    (dx, dproj, dg1), _ = _inproj_bwd([du, dvs, dq, dk, dva, dga, dgb], xs, dx1, g1, w_in_b, tm=512)
    small = dict(ln_v_gain=dlg, ln_v_bias=dlb, w_spatial=dws, b_spatial=dbs, sinks=dsk[:, :NQ],
                 norm_mix_pre=dg1, norm_mix_post=dg2, norm_ff_pre=dg3, norm_ff_post=dg4)
    dw_in, (gs, *shard_rest) = _wgrad(
        h, dproj, tn=IN_W // 2, tm=wtm, name="wgrad_in",
        comm=_both(_x_small_all_reduce(_pack_small(small, lsum)), _x_grads_share(partial_rest, REST)))
    got_in = _run(_x_grads_sibling([dw_in], FIRST), "grads_in_to_sibling")
    sums_in, to_chips = reduce_tail(FIRST, [dw_in], got_in)
    partial_in = reduce_end(FIRST, sums_in, _run(to_chips, "grads_in_to_chips"))
    shard_grads = list(_run(_x_grads_share(partial_in, FIRST), "grads_in_share")) + list(shard_rest)

    grad, delta, new_m, new_v = {}, {}, {}, {}
    for i, n in enumerate(BIG_NAMES):
        g = shard_grads[i]
        d_, m_, v_ = _adamw(w[n][0], g, m[n][0], v[n][0], tr=256, name="adamw_" + n)
        grad[n], delta[n], new_m[n], new_v[n] = g[None], d_[None], m_[None], v_[None]

    loss = 0.5 * jnp.sum(gs[SMALL_ROWS - 8:]) / D
    ds, ms, vs = _adamw(_pack_small(w), gs, _pack_small(m), _pack_small(v), tr=SMALL_ROWS, name="adamw_small")
    for packed, dst in ((gs, grad), (ds, delta), (ms, new_m), (vs, new_v)):
        dst.update(_unpack_small(packed, w))

    outs = [loss, dx[None]]
    for group in (grad, delta, new_m, new_v):
        outs.extend(group[n] for n in WEIGHT_ORDER)
    return tuple(outs)
```

```python
import functools

import jax
import jax.numpy as jnp
from jax import lax
from jax.experimental import pallas as pl
from jax.experimental.pallas import tpu as pltpu

F32 = jnp.float32
BF16 = jnp.bfloat16

D = 1024
CH = 128
NG = 8
HD = 64
NQ = 16
NKV = 4
KVW = NKV * HD
DFF = 4 * D
EPS = 1e-6
IN_W = 5632
SEG = (0, 1024, 2048, 3072, 3328, 3584, 4608, 5632)
ROPE_HALF = 8
Q_SCALE = HD ** -0.5

LR, B1, B2, AEPS, WD, STEP = 0.001, 0.9, 0.999, 1e-08, 0.01, 10

VMEM_LIMIT = 60 * 1024 * 1024
MESH = pl.DeviceIdType.MESH

_GELU_C0 = 0.7978845608028654
_GELU_C1 = 0.044715


def _cparams(sem=None):
    kw = dict(vmem_limit_bytes=VMEM_LIMIT)
    if sem is not None:
        kw["dimension_semantics"] = sem
    return pltpu.CompilerParams(**kw)


def _resident(shape):
    nd = len(shape)
    return pl.BlockSpec(shape, lambda *_: (0,) * nd, pipeline_mode=pl.Buffered(1))


def _const(shape):
    nd = len(shape)
    return pl.BlockSpec(shape, lambda *_: (0,) * nd)


def _rows(tm, w):
    return pl.BlockSpec((tm, w), lambda i: (i, 0))


class _Exchange:
    def __init__(self, ins, outs, aliases, scratch, start, finish):
        self.ins, self.outs, self.aliases, self.scratch = list(ins), list(outs), dict(aliases), list(scratch)
        self.start, self.finish = start, finish


def _both(a, b):
    na, ma, sa = len(a.ins), len(a.outs), len(a.scratch)

    def start(ci, co, cs):
        a.start(ci[:na], co[:ma], cs[:sa])
        b.start(ci[na:], co[ma:], cs[sa:])

    def finish(ci, co, cs):
        a.finish(ci[:na], co[:ma], cs[:sa])
        b.finish(ci[na:], co[ma:], cs[sa:])

    aliases = {**a.aliases, **{na + i: ma + j for i, j in b.aliases.items()}}
    return _Exchange(a.ins + b.ins, a.outs + b.outs, aliases, a.scratch + b.scratch, start, finish)


def _call(body, args, *, name, grid, in_specs, out_specs, out_shape, scratch_shapes=(), sem=None, comm=None):
    single = not isinstance(out_shape, (list, tuple))
    out_shape = [out_shape] if single else list(out_shape)
    out_specs = [out_specs] if single else list(out_specs)
    if comm is None:
        res = pl.pallas_call(body, name=name, grid=grid, in_specs=list(in_specs), out_specs=out_specs,
                             out_shape=out_shape, scratch_shapes=list(scratch_shapes),
                             compiler_params=_cparams(sem))(*args)
        return (res[0] if single else res), []
    n_in, n_out, n_scr = len(args), len(out_shape), len(scratch_shapes)
    nci, nco = len(comm.ins), len(comm.outs)
    steps = 1
    for g in grid:
        steps *= g

    def hosted(*refs):
        a, ci = refs[:n_in], refs[n_in:n_in + nci]
        o, co = refs[n_in + nci:n_in + nci + n_out], refs[n_in + nci + n_out:n_in + nci + n_out + nco]
        rest = refs[n_in + nci + n_out + nco:]
        scr, cs = rest[:n_scr], rest[n_scr:]
        step = pl.program_id(0)
        for d in range(1, len(grid)):
            step = step * grid[d] + pl.program_id(d)

        @pl.when(step == 0)
        def _():
            comm.start(ci, co, cs)

        body(*a, *o, *scr)

        @pl.when(step == steps - 1)
        def _():
            comm.finish(ci, co, cs)

    res = pl.pallas_call(
        hosted, name=name, grid=grid, in_specs=list(in_specs) + [ANY] * nci, out_specs=out_specs + [ANY] * nco,
        out_shape=out_shape + comm.outs, scratch_shapes=list(scratch_shapes) + comm.scratch,
        input_output_aliases={n_in + i: n_out + j for i, j in comm.aliases.items()},
        compiler_params=_cparams(("arbitrary",) * len(grid)),
    )(*args, *comm.ins)
    own = res[:n_out]
    return (own[0] if single else own), list(res[n_out:])


def _run(comm, name):
    nci = len(comm.ins)

    def body(*refs):
        ci, co, cs = refs[:nci], refs[nci:nci + len(comm.outs)], refs[nci + len(comm.outs):]
        comm.start(ci, co, cs)
        comm.finish(ci, co, cs)

    return pl.pallas_call(
        body, name=name, in_specs=[ANY] * nci, out_specs=[ANY] * len(comm.outs), out_shape=comm.outs,
        scratch_shapes=comm.scratch, input_output_aliases=comm.aliases,
        compiler_params=pltpu.CompilerParams(vmem_limit_bytes=VMEM_LIMIT),
    )(*comm.ins)


def _gelu(x):
    x2 = x * x
    t = jnp.tanh(x * (_GELU_C0 + (_GELU_C0 * _GELU_C1) * x2))
    hx = 0.5 * x
    return hx + hx * t, (t, x2, hx)


def _gelu_grad(parts):
    t, x2, hx = parts
    return (0.5 + 0.5 * t) + hx * (1.0 - t * t) * (_GELU_C0 + (3.0 * _GELU_C0 * _GELU_C1) * x2)


def _sigmoid(x):
    return 1.0 / (1.0 + jnp.exp(-x))


def _rms_hat(x):
    r = lax.rsqrt(jnp.mean(x * x, axis=-1, keepdims=True) + EPS)
    return x * r, r


def _rms_bwd(xhat, r, g, dout):
    dg = jnp.sum(dout * xhat, axis=0, keepdims=True)
    dy = dout * g
    dx = r * (dy - xhat * jnp.mean(dy * xhat, axis=-1, keepdims=True))
    return dx, dg


def _dot(a, b):
    return jnp.dot(a, b, preferred_element_type=F32)


def _dot_nt(a, b):
    return lax.dot_general(a, b, (((1,), (1,)), ((), ())), preferred_element_type=F32)


def _dot_tn(a, b):
    return lax.dot_general(a, b, (((0,), (0,)), ((), ())), preferred_element_type=F32)


def _rope(blk, c, s1, s2):
    return blk * c + pltpu.roll(blk, CH - ROPE_HALF, 1) * s1 + pltpu.roll(blk, ROPE_HALF, 1) * s2


def _rope_t(blk, c, s1, s2):
    return blk * c + pltpu.roll(blk * s1, ROPE_HALF, 1) + pltpu.roll(blk * s2, CH - ROPE_HALF, 1)


def _inproj(x, g1, w_in, rc, rs1, rs2, tm, comm=None):
    T = x.shape[0]

    def body(x_ref, g_ref, w_ref, c_ref, s1_ref, s2_ref,
             h_ref, u_ref, v_ref, q_ref, k_ref, va_ref, ga_ref, gb_ref):
        xhat, _ = _rms_hat(x_ref[...])
        h = (xhat * g_ref[...]).astype(BF16)
        h_ref[...] = h
        uv = _dot(h, w_ref[:, SEG[0]:SEG[2]])
        u_ref[...] = uv[:, :D]
        v_ref[...] = uv[:, D:]
        c, s1, s2 = c_ref[...], s1_ref[...], s2_ref[...]
        qkv = _dot(h, w_ref[:, SEG[2]:SEG[5]])
        for p in range(D // CH):
            blk = _rope(qkv[:, CH * p:CH * (p + 1)], c, s1, s2) * Q_SCALE
            q_ref[:, CH * p:CH * (p + 1)] = blk.astype(BF16)
        for p in range(KVW // CH):
            k_ref[:, CH * p:CH * (p + 1)] = _rope(qkv[:, D + CH * p:D + CH * (p + 1)], c, s1, s2).astype(BF16)
        va_ref[...] = qkv[:, D + KVW:].astype(BF16)
        gates = _dot(h, w_ref[:, SEG[5]:SEG[7]]).astype(BF16)
        ga_ref[...] = gates[:, :D]
        gb_ref[...] = gates[:, D:]

    sd = jax.ShapeDtypeStruct
    return _call(
        body, (x, g1, w_in, rc, rs1, rs2), name="inproj_fwd", grid=(T // tm,),
        in_specs=[_rows(tm, D), _const((1, D)), _resident((D, IN_W)), _rows(tm, CH), _rows(tm, CH), _rows(tm, CH)],
        out_specs=[_rows(tm, D), _rows(tm, D), _rows(tm, D), _rows(tm, D), _rows(tm, KVW), _rows(tm, KVW),
                   _rows(tm, D), _rows(tm, D)],
        out_shape=[sd((T, D), BF16), sd((T, D), F32), sd((T, D), F32), sd((T, D), BF16), sd((T, KVW), BF16),
                   sd((T, KVW), BF16), sd((T, D), BF16), sd((T, D), BF16)],
        sem=("parallel",), comm=comm)


def _sgu_common(u, vs, lng, lnb, ws_ref, bfull):
    nc = u.shape[0] // CH
    ug, tu = _gelu(u)
    vg, tv = _gelu(vs)
    mu = jnp.mean(vg, axis=-1, keepdims=True)
    xc = vg - mu
    rstd = lax.rsqrt(jnp.mean(xc * xc, axis=-1, keepdims=True) + EPS)
    vhat = xc * rstd
    vnb = (vhat * lng + lnb).astype(BF16)
    tri = lax.broadcasted_iota(jnp.int32, (CH, CH), 0) >= lax.broadcasted_iota(jnp.int32, (CH, CH), 1)
    wts, rhss, mixed = [], [], []
    for g in range(NG):
        wt = jnp.where(tri, ws_ref[g], 0.0).astype(BF16)
        rhs = jnp.concatenate([vnb[CH * c:CH * (c + 1), CH * g:CH * (g + 1)] for c in range(nc)], axis=1)
        mix = _dot(wt, rhs)
        wts.append(wt)
        rhss.append(rhs)
        mixed.append([mix[:, CH * c:CH * (c + 1)] + bfull[:, CH * g:CH * (g + 1)] for c in range(nc)])
    return nc, ug, tu, tv, rstd, vhat, tri, wts, rhss, mixed


def _sgu_fwd(u, vs, lng, lnb, ws, bfull, tm, comm=None):
    T = u.shape[0]

    def body(u_ref, v_ref, lng_ref, lnb_ref, ws_ref, bf_ref, a_ref):
        nc, ug, _, _, _, _, _, _, _, mixed = _sgu_common(
            u_ref[...], v_ref[...], lng_ref[...], lnb_ref[...], ws_ref, bf_ref[...])
        mixed_all = jnp.concatenate(
            [jnp.concatenate([mixed[g][c] for g in range(NG)], axis=1) for c in range(nc)], axis=0)
        a_ref[...] = (ug * mixed_all).astype(BF16)

    return _call(
        body, (u, vs, lng, lnb, ws, bfull), name="sgu_fwd", grid=(T // tm,),
        in_specs=[_rows(tm, D), _rows(tm, D), _const((1, D)), _const((1, D)), _const((NG, CH, CH)), _const((CH, D))],
        out_specs=_rows(tm, D), out_shape=jax.ShapeDtypeStruct((T, D), BF16), sem=("parallel",), comm=comm)


def _sgu_bwd(u, vs, da, lng, lnb, ws, bfull, tm, comm=None):
    T = u.shape[0]
    nsteps = T // tm

    def body(u_ref, v_ref, da_ref, lng_ref, lnb_ref, ws_ref, bf_ref,
             du_ref, dv_ref, dws_ref, dbs_ref, dlg_ref, dlb_ref, db_ref):
        i = pl.program_id(0)
        u, vs, da, lng = u_ref[...], v_ref[...], da_ref[...], lng_ref[...]
        nc, ug, tu, tv, rstd, vhat, tri, wts, rhss, mixed = _sgu_common(u, vs, lng, lnb_ref[...], ws_ref, bf_ref[...])

        @pl.when(i == 0)
        def _():
            dws_ref[...] = jnp.zeros_like(dws_ref)
            db_ref[...] = jnp.zeros_like(db_ref)
            dlg_ref[...] = jnp.zeros_like(dlg_ref)
            dlb_ref[...] = jnp.zeros_like(dlb_ref)

        mixed_all = jnp.concatenate(
            [jnp.concatenate([mixed[g][c] for g in range(NG)], axis=1) for c in range(nc)], axis=0)
        du_ref[...] = (da * mixed_all * _gelu_grad(tu)).astype(BF16)
        dmixed = da * ug
        dvn_cols = []
        for g in range(NG):
            dmix = [dmixed[CH * c:CH * (c + 1), CH * g:CH * (g + 1)] for c in range(nc)]
            db_ref[:, CH * g:CH * (g + 1)] += functools.reduce(lambda a, b: a + b, dmix)
            dm = jnp.concatenate(dmix, axis=1).astype(BF16)
            dws_ref[g] += _dot_nt(dm, rhss[g])
            dvn_cols.append(_dot_tn(wts[g], dm))
        dvn = jnp.concatenate(
            [jnp.concatenate([dvn_cols[g][:, CH * c:CH * (c + 1)] for g in range(NG)], axis=1) for c in range(nc)],
            axis=0)
        dlg_ref[...] += jnp.sum(dvn * vhat, axis=0, keepdims=True)
        dlb_ref[...] += jnp.sum(dvn, axis=0, keepdims=True)
        dvh = dvn * lng
        dvg = rstd * (dvh - jnp.mean(dvh, axis=-1, keepdims=True)
                      - vhat * jnp.mean(dvh * vhat, axis=-1, keepdims=True))
        dv_ref[...] = (dvg * _gelu_grad(tv)).astype(BF16)

        @pl.when(i == nsteps - 1)
        def _():
            for g in range(NG):
                dws_ref[g] = jnp.where(tri, dws_ref[g], 0.0)
                dbs_ref[g:g + 1, :] = jnp.sum(db_ref[:, CH * g:CH * (g + 1)].T, axis=0, keepdims=True)

    sd = jax.ShapeDtypeStruct
    return _call(
        body, (u, vs, da, lng, lnb, ws, bfull), name="sgu_bwd", grid=(nsteps,),
        in_specs=[_rows(tm, D), _rows(tm, D), _rows(tm, D), _const((1, D)), _const((1, D)), _const((NG, CH, CH)),
                  _const((CH, D))],
        out_specs=[_rows(tm, D), _rows(tm, D), _const((NG, CH, CH)), _const((NG, CH)), _const((1, D)), _const((1, D))],
        out_shape=[sd((T, D), BF16), sd((T, D), BF16), sd((NG, CH, CH), F32), sd((NG, CH), F32), sd((1, D), F32),
                   sd((1, D), F32)],
        scratch_shapes=[pltpu.VMEM((CH, D), F32)], sem=("arbitrary",), comm=comm)


def _pair_layout(prev, cur, grp):
    j, half = grp // 2, grp % 2
    blk = jnp.concatenate([prev[:, CH * j:CH * (j + 1)], cur[:, CH * j:CH * (j + 1)]], axis=0).astype(F32)
    lo = lax.broadcasted_iota(jnp.int32, blk.shape, 1) < HD
    rolled = pltpu.roll(blk, HD, 1)
    even = jnp.where(lo, blk if half == 0 else rolled, 0.0)
    odd = jnp.where(lo, 0.0, rolled if half == 0 else blk)
    return jnp.concatenate([even, odd], axis=0).astype(BF16)


def _attn_mask(n):
    qi = lax.broadcasted_iota(jnp.int32, (CH, 2 * CH), 0)
    kc = lax.broadcasted_iota(jnp.int32, (CH, 2 * CH), 1)
    ok = (kc > qi) & (kc <= qi + CH) & ((kc >= CH) | (n > 0))
    return jnp.concatenate([ok, ok], axis=1)


def _softmax_sink(s, sink):
    m = jnp.maximum(jnp.max(s, axis=-1, keepdims=True), sink)
    p = jnp.exp(s - m)
    ps = jnp.exp(sink - m)
    inv = 1.0 / (jnp.sum(p, axis=-1, keepdims=True) + ps)
    return p * inv, ps * inv


def _attn_fwd(q, k, va, sinks, comm=None):
    T = q.shape[0]
    nb = T // CH

    def body(sk_ref, q_ref, kp_ref, kc_ref, vp_ref, vc_ref, o_ref):
        n = pl.program_id(0)
        mask = _attn_mask(n)
        kp, kc, vp, vc = kp_ref[...], kc_ref[...], vp_ref[...], vc_ref[...]
        kks = [_pair_layout(kp, kc, grp) for grp in range(NKV)]
        vvs = [_pair_layout(vp, vc, grp) for grp in range(NKV)]
        npairs = D // CH

        def scores(p):
            return _dot_nt(q_ref[:, CH * p:CH * (p + 1)], kks[p // 2])

        ahead = 3
        outs, probs = [], []
        pending = [scores(p) for p in range(ahead)]
        even_lanes = lax.broadcasted_iota(jnp.int32, (CH, CH), 1) < HD

        def unnormalised(s, sink):
            m = jnp.maximum(jnp.max(s, axis=-1, keepdims=True), sink)
            p = jnp.exp(s - m)
            return p, 1.0 / (jnp.sum(p, axis=-1, keepdims=True) + jnp.exp(sink - m))

        def value_product(p):
            pr, ie, io = probs[p]
            return _dot(pr, vvs[p // 2]) * jnp.where(even_lanes, ie, io)

        for p in range(npairs):
            s = jnp.where(mask, pending.pop(0), -1e30)
            if p + ahead < npairs:
                pending.append(scores(p + ahead))
            pe, ie = unnormalised(s[:, :2 * CH], sk_ref[2 * p])
            po, io = unnormalised(s[:, 2 * CH:], sk_ref[2 * p + 1])
            probs.append((jnp.concatenate([pe, po], axis=1).astype(BF16), ie, io))
            if p >= 1:
                outs.append(value_product(p - 1))
        outs.append(value_product(npairs - 1))
        o_ref[...] = jnp.concatenate(outs, axis=1).astype(BF16)

    prev = lambda n: (jnp.maximum(n - 1, 0), 0)
    cur = lambda n: (n, 0)
    return _call(
        body, (sinks, q, k, k, va, va), name="attn_fwd", grid=(nb,),
        in_specs=[pl.BlockSpec(memory_space=pltpu.SMEM), pl.BlockSpec((CH, D), cur),
                  pl.BlockSpec((CH, KVW), prev), pl.BlockSpec((CH, KVW), cur),
                  pl.BlockSpec((CH, KVW), prev), pl.BlockSpec((CH, KVW), cur)],
        out_specs=pl.BlockSpec((CH, D), cur), out_shape=jax.ShapeDtypeStruct((T, D), BF16),
        sem=("parallel",), comm=comm)


def _attn_bwd(q, k, va, datt, sinks, rc, rs1, rs2, comm=None):
    T = q.shape[0]
    nb = T // CH

    def body(sk_ref, q_ref, kp_ref, kc_ref, vp_ref, vc_ref, do_ref, cq_ref, s1q_ref, s2q_ref, ck_ref, s1k_ref, s2k_ref,
             dq_ref, dk_ref, dv_ref, dsk_ref, kcar, vcar):
        n = pl.program_id(0)

        @pl.when(n == 0)
        def _():
            kcar[...] = jnp.zeros_like(kcar)
            vcar[...] = jnp.zeros_like(vcar)
            dsk_ref[...] = jnp.zeros_like(dsk_ref)

        def flush(kprev, vprev):
            ck, s1k, s2k = ck_ref[...], s1k_ref[...], s2k_ref[...]
            for j in range(KVW // CH):
                sl = slice(CH * j, CH * (j + 1))
                dk_ref[:, sl] = _rope_t(kcar[:, sl] + kprev[:, sl], ck, s1k, s2k).astype(BF16)
                dv_ref[:, sl] = (vcar[:, sl] + vprev[:, sl]).astype(BF16)

        @pl.when(n < nb)
        def _():
            mask = _attn_mask(n)
            kp, kc, vp, vc = kp_ref[...], kc_ref[...], vp_ref[...], vc_ref[...]
            cq, s1q, s2q = cq_ref[...], s1q_ref[...], s2q_ref[...]
            lane = lax.broadcasted_iota(jnp.int32, (1, CH), 1)
            dsk = jnp.zeros((1, CH), F32)
            npairs = D // CH
            kks = [_pair_layout(kp, kc, grp) for grp in range(NKV)]
            vvs = [_pair_layout(vp, vc, grp) for grp in range(NKV)]
            qs = [q_ref[:, CH * p:CH * (p + 1)] for p in range(npairs)]
            dos = [do_ref[:, CH * p:CH * (p + 1)].astype(BF16) for p in range(npairs)]

            def first(p):
                return _dot_nt(qs[p], kks[p // 2]), _dot_nt(dos[p], vvs[p // 2])

            def last(p, ds, pb):
                return (_rope_t(_dot(ds, kks[p // 2]), cq, s1q, s2q) * Q_SCALE, _dot_tn(qs[p], ds), _dot_tn(dos[p], pb))

            ahead = 2
            pending = [first(p) for p in range(ahead)]
            mids, ends = [], []
            for p in range(npairs):
                s, dp = pending.pop(0)
                s = jnp.where(mask, s, -1e30)
                if p + ahead < npairs:
                    pending.append(first(p + ahead))
                ds_parts, p_parts = [], []
                for par in range(2):
                    sl = slice(2 * CH * par, 2 * CH * (par + 1))
                    pr, psink = _softmax_sink(s[:, sl], sk_ref[2 * p + par])
                    delta = jnp.sum(pr * dp[:, sl], axis=-1, keepdims=True)
                    ds_parts.append(pr * (dp[:, sl] - delta))
                    p_parts.append(pr)
                    tot = -jnp.sum(psink * delta, axis=0, keepdims=True)
                    dsk = dsk + jnp.where(lane == 2 * p + par, tot, 0.0)
                mids.append((jnp.concatenate(ds_parts, axis=1).astype(BF16), jnp.concatenate(p_parts, axis=1).astype(BF16)))
                if p >= 1:
                    ends.append(last(p - 1, *mids[p - 1]))
            ends.append(last(npairs - 1, *mids[-1]))
            dq_cols = [e[0] for e in ends]
            def fold(i):
                rows = []
                for grp in range(NKV):
                    acc = ends[2 * grp][i] + ends[2 * grp + 1][i]
                    rows.append(acc[:HD, :2 * CH] + acc[HD:, 2 * CH:])
                return jnp.concatenate(rows, axis=0).T

            dkf, dvf = fold(1), fold(2)
            dq_ref[...] = jnp.concatenate(dq_cols, axis=1).astype(BF16)
            dsk_ref[...] += dsk
            flush(dkf[:CH], dvf[:CH])
            kcar[...] = dkf[CH:]
            vcar[...] = dvf[CH:]

        @pl.when(n == nb)
        def _():
            z = jnp.zeros((CH, KVW), F32)
            flush(z, z)

    last = nb - 1
    cur = lambda n: (jnp.minimum(n, last), 0)
    prev = lambda n: (jnp.clip(n - 1, 0, last), 0)
    sd = jax.ShapeDtypeStruct
    return _call(
        body, (sinks, q, k, k, va, va, datt, rc, rs1, rs2, rc, rs1, rs2), name="attn_bwd", grid=(nb + 1,),
        in_specs=[pl.BlockSpec(memory_space=pltpu.SMEM), pl.BlockSpec((CH, D), cur),
                  pl.BlockSpec((CH, KVW), prev), pl.BlockSpec((CH, KVW), cur),
                  pl.BlockSpec((CH, KVW), prev), pl.BlockSpec((CH, KVW), cur),
                  pl.BlockSpec((CH, D), cur),
                  pl.BlockSpec((CH, CH), cur), pl.BlockSpec((CH, CH), cur), pl.BlockSpec((CH, CH), cur),
                  pl.BlockSpec((CH, CH), prev), pl.BlockSpec((CH, CH), prev), pl.BlockSpec((CH, CH), prev)],
        out_specs=[pl.BlockSpec((CH, D), cur), pl.BlockSpec((CH, KVW), prev), pl.BlockSpec((CH, KVW), prev),
                   _const((1, CH))],
        out_shape=[sd((T, D), BF16), sd((T, KVW), BF16), sd((T, KVW), BF16), sd((1, CH), F32)],
        scratch_shapes=[pltpu.VMEM((CH, KVW), F32), pltpu.VMEM((CH, KVW), F32)], sem=("arbitrary",), comm=comm)


def _merge_fwd(a, att, ga, gb, x, w_a, w_b, w_o, g2, tm):
    T = x.shape[0]

    def body(a_ref, att_ref, ga_ref, gb_ref, x_ref, wa_ref, wb_ref, wo_ref, g_ref,
             pa_ref, pb_ref, mg_ref, mix_ref, x1_ref):
        pa = _dot(a_ref[...], wa_ref[...])
        pb = _dot(att_ref[...], wb_ref[...])
        pa_ref[...] = pa.astype(BF16)
        pb_ref[...] = pb.astype(BF16)
        merged = (_sigmoid(ga_ref[...].astype(F32)) * pa + _sigmoid(gb_ref[...].astype(F32)) * pb).astype(BF16)
        mg_ref[...] = merged
        mix = _dot(merged, wo_ref[...])
        mix_ref[...] = mix
        mhat, _ = _rms_hat(mix)
        x1_ref[...] = x_ref[...] + mhat * g_ref[...]

    sd = jax.ShapeDtypeStruct
    return pl.pallas_call(
        body, name="merge_fwd", grid=(T // tm,),
        in_specs=[_rows(tm, D)] * 5 + [_resident((D, D))] * 3 + [_const((1, D))],
        out_specs=[_rows(tm, D)] * 5,
        out_shape=[sd((T, D), BF16), sd((T, D), BF16), sd((T, D), BF16), sd((T, D), F32), sd((T, D), F32)],
        compiler_params=_cparams(("parallel",)),
    )(a, att, ga, gb, x, w_a, w_b, w_o, g2)


def _merge_bwd(dx1, mix, ga, gb, pa, pb, a, att, merged, w_a, w_b, w_o, g2, tm, comm=None):
    T = dx1.shape[0]
    nsteps = T // tm

    def body(dx1_ref, mix_ref, ga_ref, gb_ref, pa_ref, pb_ref, a_ref, att_ref, mg_ref, wa_ref, wb_ref, wo_ref, g_ref,
             dga_ref, dgb_ref, da_ref, datt_ref, dg_ref, dwa_ref, dwb_ref, dwo_ref, acc, sem):
        i = pl.program_id(0)

        @pl.when(i == 0)
        def _():
            dg_ref[...] = jnp.zeros_like(dg_ref)
            acc[...] = jnp.zeros_like(acc)

        mhat, r = _rms_hat(mix_ref[...])
        dmix, dg = _rms_bwd(mhat, r, g_ref[...], dx1_ref[...])
        dg_ref[...] += dg
        dmix = dmix.astype(BF16)
        dmerged = _dot_nt(dmix, wo_ref[...])
        sa = _sigmoid(ga_ref[...].astype(F32))
        sb = _sigmoid(gb_ref[...].astype(F32))
        dao = (dmerged * sa).astype(BF16)
        dbo = (dmerged * sb).astype(BF16)
        dga_ref[...] = (dmerged * pa_ref[...].astype(F32) * (sa * (1.0 - sa))).astype(BF16)
        dgb_ref[...] = (dmerged * pb_ref[...].astype(F32) * (sb * (1.0 - sb))).astype(BF16)
        da_ref[...] = _dot_nt(dao, wa_ref[...])
        datt_ref[...] = _dot_nt(dbo, wb_ref[...]).astype(BF16)
        acc[0] += _dot_tn(a_ref[...], dao)
        acc[1] += _dot_tn(att_ref[...], dbo)
        acc[2] += _dot_tn(mg_ref[...], dmix)

        @pl.when(i == nsteps - 1)
        def _():
            outs = [pltpu.make_async_copy(acc.at[j], ref, sem.at[j]) for j, ref in enumerate((dwa_ref, dwb_ref, dwo_ref))]
            for cp in outs:
                cp.start()
            for cp in outs:
                cp.wait()

    sd = jax.ShapeDtypeStruct
    return _call(
        body, (dx1, mix, ga, gb, pa, pb, a, att, merged, w_a, w_b, w_o, g2), name="merge_bwd", grid=(nsteps,),
        in_specs=[_rows(tm, D)] * 9 + [_resident((D, D))] * 3 + [_const((1, D))],
        out_specs=[_rows(tm, D)] * 4 + [_const((1, D))] + [ANY] * 3,
        out_shape=[sd((T, D), BF16), sd((T, D), BF16), sd((T, D), F32), sd((T, D), BF16), sd((1, D), F32)]
        + [sd((D, D), F32)] * 3,
        scratch_shapes=[pltpu.VMEM((3, D, D), F32), _dma_sems(3)], sem=("arbitrary",), comm=comm)


def _ffn(x1, target, w1, w2, g3, g4, tm):
    T = x1.shape[0]

    def body(x_ref, t_ref, w1_ref, w2_ref, g3_ref, g4_ref,
             hf_ref, f2_ref, dff_ref, df1_ref, dx_ref, ls_ref, dg3_ref, dg4_ref):
        @pl.when(pl.program_id(0) == 0)
        def _():
            ls_ref[...] = jnp.zeros_like(ls_ref)
            dg3_ref[...] = jnp.zeros_like(dg3_ref)
            dg4_ref[...] = jnp.zeros_like(dg4_ref)

        x = x_ref[...]
        g3, g4 = g3_ref[...], g4_ref[...]
        xhat, r3 = _rms_hat(x)
        hf = (xhat * g3).astype(BF16)
        hf_ref[...] = hf
        rl = jnp.maximum(_dot(hf, w1_ref[...]), 0.0)
        f2 = (rl * rl).astype(BF16)
        f2_ref[...] = f2
        fhat, r4 = _rms_hat(_dot(f2, w2_ref[...]))
        err = x + fhat * g4 - t_ref[...]
        ls_ref[...] += jnp.sum(err * err, axis=0, keepdims=True)
        dy = err * (1.0 / D)
        dff, dg4 = _rms_bwd(fhat, r4, g4, dy)
        dg4_ref[...] += dg4
        dff = dff.astype(BF16)
        dff_ref[...] = dff
        df1 = (_dot_nt(dff, w2_ref[...]) * (2.0 * rl)).astype(BF16)
        df1_ref[...] = df1
        dxn, dg3 = _rms_bwd(xhat, r3, g3, _dot_nt(df1, w1_ref[...]))
        dg3_ref[...] += dg3
        dx_ref[...] = dy + dxn

    sd = jax.ShapeDtypeStruct
    return pl.pallas_call(
        body, name="ffn_fwd_bwd", grid=(T // tm,),
        in_specs=[_rows(tm, D), _rows(tm, D), _resident((D, DFF)), _resident((DFF, D)), _const((1, D)), _const((1, D))],
        out_specs=[_rows(tm, D), _rows(tm, DFF), _rows(tm, D), _rows(tm, DFF), _rows(tm, D), _const((1, D)),
                   _const((1, D)), _const((1, D))],
        out_shape=[sd((T, D), BF16), sd((T, DFF), BF16), sd((T, D), BF16), sd((T, DFF), BF16), sd((T, D), F32),
                   sd((1, D), F32), sd((1, D), F32), sd((1, D), F32)],
        compiler_params=_cparams(("arbitrary",)),
    )(x1, target, w1, w2, g3, g4)


def _inproj_bwd(parts, x, dx1, g1, w_in, tm, comm=None):
    T = x.shape[0]
    widths = [p.shape[1] for p in parts]
    offs = [sum(widths[:i]) for i in range(len(widths) + 1)]
    assert offs[-1] == IN_W

    def body(*refs):
        n = len(parts)
        prefs = refs[:n]
        x_ref, dx1_ref, g_ref, w_ref, dx_ref, dp_ref, dg_ref = refs[n:]

        @pl.when(pl.program_id(0) == 0)
        def _():
            dg_ref[...] = jnp.zeros_like(dg_ref)

        for i in range(n):
            dp_ref[:, offs[i]:offs[i + 1]] = prefs[i][...]
        dh = _dot_nt(dp_ref[...], w_ref[...])
        xhat, r = _rms_hat(x_ref[...])
        dxn, dg = _rms_bwd(xhat, r, g_ref[...], dh)
        dg_ref[...] += dg
        dx_ref[...] = dx1_ref[...] + dxn

    sd = jax.ShapeDtypeStruct
    return _call(
        body, (*parts, x, dx1, g1, w_in), name="inproj_bwd", grid=(T // tm,),
        in_specs=[_rows(tm, w) for w in widths] + [_rows(tm, D), _rows(tm, D), _const((1, D)), _resident((D, IN_W))],
        out_specs=[_rows(tm, D), _rows(tm, IN_W), _const((1, D))],
        out_shape=[sd((T, D), F32), sd((T, IN_W), BF16), sd((1, D), F32)], sem=("arbitrary",), comm=comm)


def _wgrad(a, g, tn, tm, name, comm=None):
    T, K = a.shape
    N = g.shape[1]

    def body(a_ref, g_ref, o_ref):
        @pl.when(pl.program_id(1) == 0)
        def _():
            o_ref[...] = jnp.zeros_like(o_ref)

        o_ref[...] += _dot_tn(a_ref[...], g_ref[...])

    return _call(
        body, (a, g), name=name, grid=(N // tn, T // tm),
        in_specs=[pl.BlockSpec((tm, K), lambda j, t: (t, 0)), pl.BlockSpec((tm, tn), lambda j, t: (t, j))],
        out_specs=pl.BlockSpec((K, tn), lambda j, t: (0, j)),
        out_shape=jax.ShapeDtypeStruct((K, N), F32), sem=("parallel", "arbitrary"), comm=comm)


def _adamw(w, g, m, v, tr, name):
    R, C = w.shape
    bc1 = 1.0 / (1.0 - B1 ** STEP)
    bc2 = 1.0 / (1.0 - B2 ** STEP)

    def body(w_ref, g_ref, m_ref, v_ref, go_ref, d_ref, nm_ref, nv_ref):
        g = g_ref[...]
        go_ref[...] = g
        m = B1 * m_ref[...] + (1.0 - B1) * g
        v = B2 * v_ref[...] + (1.0 - B2) * (g * g)
        nm_ref[...] = m
        nv_ref[...] = v
        d_ref[...] = -LR * ((m * bc1) / (jnp.sqrt(v * bc2) + AEPS) + WD * w_ref[...])

    spec = pl.BlockSpec((tr, C), lambda i: (i, 0))
    return pl.pallas_call(
        body, name=name, grid=(R // tr,), in_specs=[spec] * 4, out_specs=[spec] * 4,
        out_shape=[jax.ShapeDtypeStruct((R, C), F32)] * 4,
        compiler_params=_cparams(("parallel",)),
    )(w, g, m, v)


BIG = (("col", (D, IN_W)), ("row", (D, D)), ("row", (D, D)), ("row", (D, D)), ("col", (D, DFF)), ("row", (DFF, D)))
NBIG = len(BIG)
ANY = pl.BlockSpec(memory_space=pl.ANY)


def _shard_shape(kind, shape):
    R, C = shape
    return (R, C // 4) if kind == "col" else (R // 4, C)


def _half_shape(kind, shape):
    R, C = shape
    return (R // 2, C) if kind == "col" else (R, C // 2)


def _piece_shape(kind, shape):
    R, C = shape
    return (R // 2, C // 4) if kind == "col" else (R // 4, C // 2)


def _own_region(ref, kind, shape, s):
    R, C = shape
    return ref.at[:, pl.ds(s * (C // 4), C // 4)] if kind == "col" else ref.at[pl.ds(s * (R // 4), R // 4), :]


def _ag_region(ref, kind, shape, s, hc):
    R, C = shape
    if kind == "col":
        return ref.at[pl.ds(hc * (R // 2), R // 2), pl.ds(s * (C // 4), C // 4)]
    return ref.at[pl.ds(s * (R // 4) + hc * (R // 8), R // 8), :]


def _ag_shard_half(ref, kind, shape, hc):
    R, C = shape
    return ref.at[pl.ds(hc * (R // 2), R // 2), :] if kind == "col" else ref.at[pl.ds(hc * (R // 8), R // 8), :]


def _grad_half(ref, kind, shape, hc):
    R, C = shape
    return ref.at[pl.ds(hc * (R // 2), R // 2), :] if kind == "col" else ref.at[:, pl.ds(hc * (C // 2), C // 2)]


def _half_piece(ref, kind, shape, s):
    R, C = shape
    return ref.at[:, pl.ds(s * (C // 4), C // 4)] if kind == "col" else ref.at[pl.ds(s * (R // 4), R // 4), :]


def _place():
    x, y, c = lax.axis_index("x"), lax.axis_index("y"), lax.axis_index("c")
    chips = [(1 - x, y), (x, 1 - y), (1 - x, 1 - y)]
    return x, y, c, chips


def _rcopy(src, dst, ssem, rsem, dev):
    return pltpu.make_async_remote_copy(src_ref=src, dst_ref=dst, send_sem=ssem, recv_sem=rsem,
                                        device_id=dev, device_id_type=MESH)


def _dma_sems(n):
    return pltpu.SemaphoreType.DMA((n,))


def _x_gather_ici(shards, ws):
    n = len(ws)
    specs = [BIG[w] for w in ws]

    def place():
        x, y, c, chips = _place()
        return c, chips, 2 * x + y

    def sends(sh, full, sc):
        c, chips, me_s = place()
        return [_rcopy(_ag_shard_half(sh[i], kind, shape, c), _ag_region(full[i], kind, shape, me_s, c),
                       sc[0].at[3 * i + j], sc[1].at[3 * i + j], (cx, cy, c))
                for i, (kind, shape) in enumerate(specs) for j, (cx, cy) in enumerate(chips)]

    def start(sh, full, sc):
        for i in range(n):
            pltpu.make_async_copy(sh[i], sc[4 + i], sc[2].at[i]).start()
        for cp in sends(sh, full, sc):
            cp.start()

    def finish(sh, full, sc):
        c, chips, me_s = place()
        stores = []
        for i, (kind, shape) in enumerate(specs):
            pltpu.make_async_copy(sh[i], sc[4 + i], sc[2].at[i]).wait()
            st = pltpu.make_async_copy(sc[4 + i], _own_region(full[i], kind, shape, me_s), sc[3].at[i])
            st.start()
            stores.append(st)
        for i, (kind, shape) in enumerate(specs):
            for j, (cx, cy) in enumerate(chips):
                reg = _ag_region(full[i], kind, shape, 2 * cx + cy, c)
                _rcopy(reg, reg, sc[0].at[3 * i + j], sc[1].at[3 * i + j], (cx, cy, c)).wait_recv()
        for cp in sends(sh, full, sc):
            cp.wait_send()
        for st in stores:
            st.wait()

    return _Exchange(
        shards, [jax.ShapeDtypeStruct(shape, BF16) for _, shape in specs], {},
        [_dma_sems(3 * n), _dma_sems(3 * n), _dma_sems(n), _dma_sems(n)]
        + [pltpu.VMEM(_shard_shape(k, s), BF16) for k, s in specs], start, finish)


def _x_gather_d2d(wholes, ws):
    specs = [BIG[w] for w in ws]
    n = len(ws)

    def copies(full, sc, mine):
        x, y, c, chips = _place()
        hc = c if mine else 1 - c
        return [_rcopy(reg, reg, sc[0].at[3 * i + j], sc[1].at[3 * i + j], (x, y, 1 - c))
                for i, (kind, shape) in enumerate(specs) for j, (cx, cy) in enumerate(chips)
                for reg in [_ag_region(full[i], kind, shape, 2 * cx + cy, hc)]]

    def start(_, full, sc):
        for cp in copies(full, sc, True):
            cp.start()

    def finish(_, full, sc):
        for cp in copies(full, sc, False):
            cp.wait_recv()
        for cp in copies(full, sc, True):
            cp.wait_send()

    return _Exchange(wholes, [jax.ShapeDtypeStruct(shape, BF16) for _, shape in specs], {i: i for i in range(n)},
                     [_dma_sems(3 * n), _dma_sems(3 * n)], start, finish)


def _x_grads_sibling(grads, ws):
    specs = [BIG[w] for w in ws]
    n = len(ws)

    def copies(g, got, sc):
        x, y, c, _ = _place()
        return [_rcopy(_grad_half(g[i], kind, shape, 1 - c), got[i], sc[0].at[i], sc[1].at[i], (x, y, 1 - c))
                for i, (kind, shape) in enumerate(specs)]

    def start(g, got, sc):
        for cp in copies(g, got, sc):
            cp.start()

    def finish(g, got, sc):
        for cp in copies(g, got, sc):
            cp.wait_recv()
        for cp in copies(g, got, sc):
            cp.wait_send()

    return _Exchange(grads, [jax.ShapeDtypeStruct(_half_shape(k, s), F32) for k, s in specs], {},
                     [_dma_sems(n), _dma_sems(n)], start, finish)


def _x_grads_chips(sums_bf, ws):
    specs = [BIG[w] for w in ws]
    n = len(ws)

    def copies(s16, got, sc):
        x, y, c, chips = _place()
        return [_rcopy(_half_piece(s16[i], kind, shape, 2 * cx + cy), got[i].at[j],
                       sc[0].at[3 * i + j], sc[1].at[3 * i + j], (cx, cy, c))
                for i, (kind, shape) in enumerate(specs) for j, (cx, cy) in enumerate(chips)]

    def start(s16, got, sc):
        for cp in copies(s16, got, sc):
            cp.start()

    def finish(s16, got, sc):
        for cp in copies(s16, got, sc):
            cp.wait_recv()
        for cp in copies(s16, got, sc):
            cp.wait_send()

    return _Exchange(sums_bf, [jax.ShapeDtypeStruct((3,) + _piece_shape(k, s), BF16) for k, s in specs], {},
                     [_dma_sems(3 * n), _dma_sems(3 * n)], start, finish)


def _shard_half(ref, kind, shape, hc):
    sr, sc = _shard_shape(kind, shape)
    return ref.at[pl.ds(hc * (sr // 2), sr // 2), :] if kind == "col" else ref.at[:, pl.ds(hc * (sc // 2), sc // 2)]


def _x_grads_share(shard_grads, ws):
    specs = [BIG[w] for w in ws]
    n = len(ws)

    def copies(g, sc, mine):
        x, y, c, _ = _place()
        hc = c if mine else 1 - c
        return [_rcopy(part, part, sc[0].at[i], sc[1].at[i], (x, y, 1 - c))
                for i, (kind, shape) in enumerate(specs) for part in [_shard_half(g[i], kind, shape, hc)]]

    def start(_, g, sc):
        for cp in copies(g, sc, True):
            cp.start()

    def finish(_, g, sc):
        for cp in copies(g, sc, False):
            cp.wait_recv()
        for cp in copies(g, sc, True):
            cp.wait_send()

    return _Exchange(shard_grads, [jax.ShapeDtypeStruct(_shard_shape(k, s), F32) for k, s in specs],
                     {i: i for i in range(n)}, [_dma_sems(n), _dma_sems(n)], start, finish)


ADD_BLOCK_BYTES = 4 * 1024 * 1024


def _add_rows(rows, cols):
    r = rows
    while r > 256 and r * cols * 4 > ADD_BLOCK_BYTES:
        r //= 2
    return r


def _add_halves(place, g, got, kind, name):
    R, C = g.shape
    hr, hcols = _half_shape(kind, (R, C))
    blk_rows = _add_rows(hr, hcols)
    steps = hr // blk_rows

    def body(p_ref, g_ref, b_ref, s_ref, sb_ref):
        s = g_ref[...] + b_ref[...]
        s_ref[...] = s
        sb_ref[...] = s.astype(BF16)

    if kind == "col":
        g_spec = pl.BlockSpec((blk_rows, C), lambda i, p: (p[0] * steps + i, 0))
    else:
        g_spec = pl.BlockSpec((blk_rows, hcols), lambda i, p: (i, p[0]))
    spec = pl.BlockSpec((blk_rows, hcols), lambda i, p: (i, 0))
    return pl.pallas_call(
        body, name=name,
        grid_spec=pltpu.PrefetchScalarGridSpec(num_scalar_prefetch=1, grid=(steps,), in_specs=[g_spec, spec],
                                               out_specs=[spec, spec]),
        out_shape=[jax.ShapeDtypeStruct((hr, hcols), F32), jax.ShapeDtypeStruct((hr, hcols), BF16)],
        compiler_params=_cparams(("parallel",)),
    )(place, g, got)


def _add_pieces(place, half, got, kind, shape, name):
    pr, pc = _piece_shape(kind, shape)
    blk_rows = _add_rows(pr, pc)
    steps = pr // blk_rows

    def body(p_ref, m_ref, g_ref, o_ref):
        acc = m_ref[...]
        for j in range(3):
            acc = acc + g_ref[j].astype(F32)
        o_ref[...] = acc

    if kind == "col":
        m_spec = pl.BlockSpec((blk_rows, pc), lambda i, p: (i, p[1]))
        o_spec = pl.BlockSpec((blk_rows, pc), lambda i, p: (p[0] * steps + i, 0))
    else:
        m_spec = pl.BlockSpec((blk_rows, pc), lambda i, p: (p[1] * steps + i, 0))
        o_spec = pl.BlockSpec((blk_rows, pc), lambda i, p: (i, p[0]))
    return pl.pallas_call(
        body, name=name,
        grid_spec=pltpu.PrefetchScalarGridSpec(
            num_scalar_prefetch=1, grid=(steps,),
            in_specs=[m_spec, pl.BlockSpec((3, blk_rows, pc), lambda i, p: (0, i, 0))], out_specs=o_spec),
        out_shape=jax.ShapeDtypeStruct(_shard_shape(kind, shape), F32),
        compiler_params=_cparams(("parallel",)),
    )(place, half, got)


SMALL_ROWS = 1024 + 8 * 8 + 8


def _x_small_all_reduce(p):
    def parts(p_ref, sc):
        slots, ssem, rsem = sc[0], sc[2], sc[3]
        x, y, c = lax.axis_index("x"), lax.axis_index("y"), lax.axis_index("c")
        me = 4 * x + 2 * y + c
        out = []
        for r in range(1, 8):
            bx, by, bc = (r >> 2) & 1, (r >> 1) & 1, r & 1
            tgt = (1 - x if bx else x, 1 - y if by else y, 1 - c if bc else c)
            send = _rcopy(p_ref, slots.at[me], ssem.at[r - 1], rsem.at[r - 1], tgt)
            src = 4 * tgt[0] + 2 * tgt[1] + tgt[2]
            recv = _rcopy(p_ref, slots.at[src], ssem.at[r - 1], rsem.at[r - 1], tgt)
            out.append((send, recv))
        return me, out

    def start(ins, outs, sc):
        me, cps = parts(ins[0], sc)
        pltpu.make_async_copy(ins[0], sc[0].at[me], sc[4].at[0]).start()
        for send, _ in cps:
            send.start()

    def finish(ins, outs, sc):
        me, cps = parts(ins[0], sc)
        pltpu.make_async_copy(ins[0], sc[0].at[me], sc[4].at[0]).wait()
        for _, recv in cps:
            recv.wait_recv()
        acc = sc[0][0]
        for d in range(1, 8):
            acc = acc + sc[0][d]
        sc[1][...] = acc
        back = pltpu.make_async_copy(sc[1], outs[0], sc[4].at[1])
        back.start()
        for send, _ in cps:
            send.wait_send()
        back.wait()

    return _Exchange([p], [jax.ShapeDtypeStruct((SMALL_ROWS, CH), F32)], {},
                     [pltpu.VMEM((8, SMALL_ROWS, CH), F32), pltpu.VMEM((SMALL_ROWS, CH), F32), _dma_sems(7), _dma_sems(7),
                      _dma_sems(2)], start, finish)


def _rope_tables(positions, comm=None):
    T = positions.shape[0]
    inv_freq = 500000.0 ** (-jnp.arange(0, 2 * ROPE_HALF, 2, dtype=F32) / (2 * ROPE_HALF))
    head = jnp.concatenate([inv_freq, inv_freq, jnp.zeros((HD - 2 * ROPE_HALF,), F32)])
    lane_freq = jnp.concatenate([head, head])[None, :]
    pos = jnp.broadcast_to(positions.astype(F32)[:, None], (T, CH))
    tm = min(1024, T)

    def body(p_ref, f_ref, c_ref, s1_ref, s2_ref):
        ang = p_ref[...] * f_ref[...]
        sin = jnp.sin(ang)
        first = (lax.broadcasted_iota(jnp.int32, ang.shape, 1) % HD) < ROPE_HALF
        c_ref[...] = jnp.cos(ang)
        s1_ref[...] = jnp.where(first, -sin, 0.0)
        s2_ref[...] = jnp.where(first, 0.0, sin)

    return _call(body, (pos, lane_freq), name="rope_tables", grid=(T // tm,),
                 in_specs=[_rows(tm, CH), _const((1, CH))], out_specs=[_rows(tm, CH)] * 3,
                 out_shape=[jax.ShapeDtypeStruct((T, CH), F32)] * 3, sem=("parallel",), comm=comm)


BIG_NAMES = ("w_in", "w_a", "w_b", "w_o", "w_ff_in", "w_ff_out")
SMALL_NAMES = ("w_spatial", "ln_v_gain", "ln_v_bias", "b_spatial", "sinks", "norm_mix_pre", "norm_mix_post",
               "norm_ff_pre", "norm_ff_post")
WEIGHT_ORDER = ("w_in", "ln_v_gain", "ln_v_bias", "w_spatial", "b_spatial", "sinks", "w_a", "w_b", "w_o",
                "norm_mix_pre", "norm_mix_post", "w_ff_in", "w_ff_out", "norm_ff_pre", "norm_ff_post")


def _pack_small(d, loss_sums=None):
    parts = []
    for n in SMALL_NAMES:
        flat = d[n].reshape(-1)
        pad = (-flat.shape[0]) % (8 * CH)
        parts.append(jnp.pad(flat, (0, pad)).reshape(-1, CH))
    parts.append(jnp.zeros((8, CH), F32) if loss_sums is None else loss_sums.reshape(8, CH))
    return jnp.concatenate(parts, axis=0)


def _unpack_small(p, like):
    out, row = {}, 0
    for n in SMALL_NAMES:
        size = like[n].size
        rows = -(-size // (8 * CH)) * 8
        out[n] = p[row:row + rows].reshape(-1)[:size].reshape(like[n].shape)
        row += rows
    return out


def kernel(x, positions, w_in, ln_v_gain, ln_v_bias, w_spatial, b_spatial, sinks, w_a, w_b, w_o, norm_mix_pre, norm_mix_post, w_ff_in, w_ff_out, norm_ff_pre, norm_ff_post, loss_target, m_w_in, m_ln_v_gain, m_ln_v_bias, m_w_spatial, m_b_spatial, m_sinks, m_w_a, m_w_b, m_w_o, m_norm_mix_pre, m_norm_mix_post, m_w_ff_in, m_w_ff_out, m_norm_ff_pre, m_norm_ff_post, v_w_in, v_ln_v_gain, v_ln_v_bias, v_w_spatial, v_b_spatial, v_sinks, v_w_a, v_w_b, v_w_o, v_norm_mix_pre, v_norm_mix_post, v_w_ff_in, v_w_ff_out, v_norm_ff_pre, v_norm_ff_post):
    w = dict(w_in=w_in, ln_v_gain=ln_v_gain, ln_v_bias=ln_v_bias, w_spatial=w_spatial, b_spatial=b_spatial, sinks=sinks,
             w_a=w_a, w_b=w_b, w_o=w_o, norm_mix_pre=norm_mix_pre, norm_mix_post=norm_mix_post, w_ff_in=w_ff_in,
             w_ff_out=w_ff_out, norm_ff_pre=norm_ff_pre, norm_ff_post=norm_ff_post)
    m = dict(w_in=m_w_in, ln_v_gain=m_ln_v_gain, ln_v_bias=m_ln_v_bias, w_spatial=m_w_spatial, b_spatial=m_b_spatial,
             sinks=m_sinks, w_a=m_w_a, w_b=m_w_b, w_o=m_w_o, norm_mix_pre=m_norm_mix_pre, norm_mix_post=m_norm_mix_post,
             w_ff_in=m_w_ff_in, w_ff_out=m_w_ff_out, norm_ff_pre=m_norm_ff_pre, norm_ff_post=m_norm_ff_post)
    v = dict(w_in=v_w_in, ln_v_gain=v_ln_v_gain, ln_v_bias=v_ln_v_bias, w_spatial=v_w_spatial, b_spatial=v_b_spatial,
             sinks=v_sinks, w_a=v_w_a, w_b=v_w_b, w_o=v_w_o, norm_mix_pre=v_norm_mix_pre, norm_mix_post=v_norm_mix_post,
             w_ff_in=v_w_ff_in, w_ff_out=v_w_ff_out, norm_ff_pre=v_norm_ff_pre, norm_ff_post=v_norm_ff_post)

    FIRST, REST = (0,), tuple(range(1, NBIG))
    shards = [w[n][0].astype(BF16) for n in BIG_NAMES]
    place = jnp.stack([lax.axis_index("c"), 2 * lax.axis_index("x") + lax.axis_index("y")]).astype(jnp.int32)
    xs, target = x[0], loss_target[0]
    T = xs.shape[0]
    wtm, wtm2 = min(1024, T), min(2048, T)
    g1, g2, g3, g4 = norm_mix_pre, norm_mix_post, norm_ff_pre, norm_ff_post
    w_sp, snk = w_spatial[0], sinks[0]
    MIX, FF = (1, 2, 3), (4, 5)
    bfull = jnp.repeat(b_spatial[0].T, CH, axis=1)

    def reduce_tail(ws, grads, got):
        sums = [_add_halves(place, grads[i], got[i], BIG[k][0], name="grad_add_sibling_" + BIG_NAMES[k])
                for i, k in enumerate(ws)]
        return sums, _x_grads_chips([s[1] for s in sums], ws)

    def reduce_end(ws, sums, pieces):
        return [_add_pieces(place, sums[i][0], pieces[i], *BIG[k], name="grad_add_chips_" + BIG_NAMES[k])
                for i, k in enumerate(ws)]

    (rc, rs1, rs2), w_in_part = _rope_tables(positions[0], comm=_x_gather_ici(shards[:1], FIRST))
    w_in_b = _run(_x_gather_d2d(w_in_part, FIRST), "gather_w_in_d2d")[0]
    (h, u, vs, q, k, va, ga, gb), ff_part = _inproj(xs, g1, w_in_b, rc, rs1, rs2, tm=512, comm=_x_gather_ici(shards[4:], FF))
    a, mix_part = _sgu_fwd(u, vs, ln_v_gain, ln_v_bias, w_sp, bfull, tm=512, comm=_x_gather_ici(shards[1:4], MIX))
    att, rest = _attn_fwd(q, k, va, snk, comm=_both(_x_gather_d2d(mix_part, MIX), _x_gather_d2d(ff_part, FF)))
    w_a_b, w_b_b, w_o_b, w_ff_in_b, w_ff_out_b = rest
    pa, pb, merged, mix, x1 = _merge_fwd(a, att, ga, gb, xs, w_a_b, w_b_b, w_o_b, g2, tm=512)
    hf, f2, dff, df1, dx1, lsum, dg3, dg4 = _ffn(x1, target, w_ff_in_b, w_ff_out_b, g3, g4, tm=256)

    dw_ff_out, _ = _wgrad(f2, dff, tn=1024, tm=512, name="wgrad_ff_out")
    dw_ff_in, _ = _wgrad(hf, df1, tn=2048, tm=wtm2, name="wgrad_ff_in")
    (dga, dgb, da, datt, dg2, dw_a, dw_b, dw_o), _ = _merge_bwd(
        dx1, mix, ga, gb, pa, pb, a, att, merged, w_a_b, w_b_b, w_o_b, g2, tm=256)
    grads_rest = [dw_a, dw_b, dw_o, dw_ff_in, dw_ff_out]
    (du, dvs, dws, dbs, dlg, dlb), got_rest = _sgu_bwd(
        u, vs, da, ln_v_gain, ln_v_bias, w_sp, bfull, tm=512, comm=_x_grads_sibling(grads_rest, REST))
    sums_rest, to_chips = reduce_tail(REST, grads_rest, got_rest)
    (dq, dk, dva, dsk), pieces_rest = _attn_bwd(q, k, va, datt, snk, rc, rs1, rs2, comm=to_chips)
    partial_rest = reduce_end(REST, sums_rest, pieces_rest)
    (dx, dproj, dg1), _ = _inproj_bwd([du, dvs, dq, dk, dva, dga, dgb], xs, dx1, g1, w_in_b, tm=512)
    small = dict(ln_v_gain=dlg, ln_v_bias=dlb, w_spatial=dws, b_spatial=dbs, sinks=dsk[:, :NQ],
                 norm_mix_pre=dg1, norm_mix_post=dg2, norm_ff_pre=dg3, norm_ff_post=dg4)
    dw_in, (gs, *shard_rest) = _wgrad(
        h, dproj, tn=IN_W // 2, tm=wtm, name="wgrad_in",
        comm=_both(_x_small_all_reduce(_pack_small(small, lsum)), _x_grads_share(partial_rest, REST)))
    got_in = _run(_x_grads_sibling([dw_in], FIRST), "grads_in_to_sibling")
    sums_in, to_chips = reduce_tail(FIRST, [dw_in], got_in)
    partial_in = reduce_end(FIRST, sums_in, _run(to_chips, "grads_in_to_chips"))
    shard_grads = list(_run(_x_grads_share(partial_in, FIRST), "grads_in_share")) + list(shard_rest)

    grad, delta, new_m, new_v = {}, {}, {}, {}
    for i, n in enumerate(BIG_NAMES):
        g_, d_, m_, v_ = _adamw(w[n][0], shard_grads[i], m[n][0], v[n][0], tr=256, name="adamw_" + n)
        grad[n], delta[n], new_m[n], new_v[n] = g_[None], d_[None], m_[None], v_[None]

    loss = 0.5 * jnp.sum(gs[SMALL_ROWS - 8:]) / D
    gs, ds, ms, vs = _adamw(_pack_small(w), gs, _pack_small(m), _pack_small(v), tr=SMALL_ROWS, name="adamw_small")
    for packed, dst in ((gs, grad), (ds, delta), (ms, new_m), (vs, new_v)):
        dst.update(_unpack_small(packed, w))

    outs = [loss, dx[None]]
    for group in (grad, delta, new_m, new_v):
        outs.extend(group[n] for n in WEIGHT_ORDER)
    return tuple(outs)
```

```python
import functools

import jax
import jax.numpy as jnp
from jax import lax
from jax.experimental import pallas as pl
from jax.experimental.pallas import tpu as pltpu

F32 = jnp.float32
BF16 = jnp.bfloat16

D = 1024
CH = 128
NG = 8
HD = 64
NQ = 16
NKV = 4
KVW = NKV * HD
DFF = 4 * D
EPS = 1e-6
IN_W = 5632
SEG = (0, 1024, 2048, 3072, 3328, 3584, 4608, 5632)
ROPE_HALF = 8
Q_SCALE = HD ** -0.5

LR, B1, B2, AEPS, WD, STEP = 0.001, 0.9, 0.999, 1e-08, 0.01, 10

VMEM_LIMIT = 60 * 1024 * 1024
MESH = pl.DeviceIdType.MESH

_GELU_C0 = 0.7978845608028654
_GELU_C1 = 0.044715


def _cparams(sem=None):
    kw = dict(vmem_limit_bytes=VMEM_LIMIT)
    if sem is not None:
        kw["dimension_semantics"] = sem
    return pltpu.CompilerParams(**kw)


def _resident(shape):
    nd = len(shape)
    return pl.BlockSpec(shape, lambda *_: (0,) * nd, pipeline_mode=pl.Buffered(1))


def _const(shape):
    nd = len(shape)
    return pl.BlockSpec(shape, lambda *_: (0,) * nd)


def _rows(tm, w):
    return pl.BlockSpec((tm, w), lambda i: (i, 0))


class _Exchange:
    def __init__(self, ins, outs, aliases, scratch, start, finish):
        self.ins, self.outs, self.aliases, self.scratch = list(ins), list(outs), dict(aliases), list(scratch)
        self.start, self.finish = start, finish


def _both(a, b):
    na, ma, sa = len(a.ins), len(a.outs), len(a.scratch)

    def start(ci, co, cs):
        a.start(ci[:na], co[:ma], cs[:sa])
        b.start(ci[na:], co[ma:], cs[sa:])

    def finish(ci, co, cs):
        a.finish(ci[:na], co[:ma], cs[:sa])
        b.finish(ci[na:], co[ma:], cs[sa:])

    aliases = {**a.aliases, **{na + i: ma + j for i, j in b.aliases.items()}}
    return _Exchange(a.ins + b.ins, a.outs + b.outs, aliases, a.scratch + b.scratch, start, finish)


def _call(body, args, *, name, grid, in_specs, out_specs, out_shape, scratch_shapes=(), sem=None, comm=None):
    single = not isinstance(out_shape, (list, tuple))
    out_shape = [out_shape] if single else list(out_shape)
    out_specs = [out_specs] if single else list(out_specs)
    if comm is None:
        res = pl.pallas_call(body, name=name, grid=grid, in_specs=list(in_specs), out_specs=out_specs,
                             out_shape=out_shape, scratch_shapes=list(scratch_shapes),
                             compiler_params=_cparams(sem))(*args)
        return (res[0] if single else res), []
    n_in, n_out, n_scr = len(args), len(out_shape), len(scratch_shapes)
    nci, nco = len(comm.ins), len(comm.outs)
    steps = 1
    for g in grid:
        steps *= g

    def hosted(*refs):
        a, ci = refs[:n_in], refs[n_in:n_in + nci]
        o, co = refs[n_in + nci:n_in + nci + n_out], refs[n_in + nci + n_out:n_in + nci + n_out + nco]
        rest = refs[n_in + nci + n_out + nco:]
        scr, cs = rest[:n_scr], rest[n_scr:]
        step = pl.program_id(0)
        for d in range(1, len(grid)):
            step = step * grid[d] + pl.program_id(d)

        @pl.when(step == 0)
        def _():
            comm.start(ci, co, cs)

        body(*a, *o, *scr)

        @pl.when(step == steps - 1)
        def _():
            comm.finish(ci, co, cs)

    res = pl.pallas_call(
        hosted, name=name, grid=grid, in_specs=list(in_specs) + [ANY] * nci, out_specs=out_specs + [ANY] * nco,
        out_shape=out_shape + comm.outs, scratch_shapes=list(scratch_shapes) + comm.scratch,
        input_output_aliases={n_in + i: n_out + j for i, j in comm.aliases.items()},
        compiler_params=_cparams(("arbitrary",) * len(grid)),
    )(*args, *comm.ins)
    own = res[:n_out]
    return (own[0] if single else own), list(res[n_out:])


def _run(comm, name):
    nci = len(comm.ins)

    def body(*refs):
        ci, co, cs = refs[:nci], refs[nci:nci + len(comm.outs)], refs[nci + len(comm.outs):]
        comm.start(ci, co, cs)
        comm.finish(ci, co, cs)

    return pl.pallas_call(
        body, name=name, in_specs=[ANY] * nci, out_specs=[ANY] * len(comm.outs), out_shape=comm.outs,
        scratch_shapes=comm.scratch, input_output_aliases=comm.aliases,
        compiler_params=pltpu.CompilerParams(vmem_limit_bytes=VMEM_LIMIT),
    )(*comm.ins)


def _gelu(x):
    x2 = x * x
    t = jnp.tanh(x * (_GELU_C0 + (_GELU_C0 * _GELU_C1) * x2))
    hx = 0.5 * x
    return hx + hx * t, (t, x2, hx)


def _gelu_grad(parts):
    t, x2, hx = parts
    return (0.5 + 0.5 * t) + hx * (1.0 - t * t) * (_GELU_C0 + (3.0 * _GELU_C0 * _GELU_C1) * x2)


def _sigmoid(x):
    return 1.0 / (1.0 + jnp.exp(-x))


def _rms_hat(x):
    r = lax.rsqrt(jnp.mean(x * x, axis=-1, keepdims=True) + EPS)
    return x * r, r


def _rms_bwd(xhat, r, g, dout):
    dg = jnp.sum(dout * xhat, axis=0, keepdims=True)
    dy = dout * g
    dx = r * (dy - xhat * jnp.mean(dy * xhat, axis=-1, keepdims=True))
    return dx, dg


def _dot(a, b):
    return jnp.dot(a, b, preferred_element_type=F32)


def _dot_nt(a, b):
    return lax.dot_general(a, b, (((1,), (1,)), ((), ())), preferred_element_type=F32)


def _dot_tn(a, b):
    return lax.dot_general(a, b, (((0,), (0,)), ((), ())), preferred_element_type=F32)


def _rope(blk, c, s1, s2):
    return blk * c + pltpu.roll(blk, CH - ROPE_HALF, 1) * s1 + pltpu.roll(blk, ROPE_HALF, 1) * s2


def _rope_t(blk, c, s1, s2):
    return blk * c + pltpu.roll(blk * s1, ROPE_HALF, 1) + pltpu.roll(blk * s2, CH - ROPE_HALF, 1)


def _inproj(x, g1, w_in, rc, rs1, rs2, tm, comm=None):
    T = x.shape[0]

    def body(x_ref, g_ref, w_ref, c_ref, s1_ref, s2_ref,
             h_ref, u_ref, v_ref, q_ref, k_ref, va_ref, ga_ref, gb_ref):
        xhat, _ = _rms_hat(x_ref[...])
        h = (xhat * g_ref[...]).astype(BF16)
        h_ref[...] = h
        uv = _dot(h, w_ref[:, SEG[0]:SEG[2]])
        u_ref[...] = uv[:, :D]
        v_ref[...] = uv[:, D:]
        c, s1, s2 = c_ref[...], s1_ref[...], s2_ref[...]
        qkv = _dot(h, w_ref[:, SEG[2]:SEG[5]])
        for p in range(D // CH):
            blk = _rope(qkv[:, CH * p:CH * (p + 1)], c, s1, s2) * Q_SCALE
            q_ref[:, CH * p:CH * (p + 1)] = blk.astype(BF16)
        for p in range(KVW // CH):
            k_ref[:, CH * p:CH * (p + 1)] = _rope(qkv[:, D + CH * p:D + CH * (p + 1)], c, s1, s2).astype(BF16)
        va_ref[...] = qkv[:, D + KVW:].astype(BF16)
        gates = _dot(h, w_ref[:, SEG[5]:SEG[7]]).astype(BF16)
        ga_ref[...] = gates[:, :D]
        gb_ref[...] = gates[:, D:]

    sd = jax.ShapeDtypeStruct
    return _call(
        body, (x, g1, w_in, rc, rs1, rs2), name="inproj_fwd", grid=(T // tm,),
        in_specs=[_rows(tm, D), _const((1, D)), _resident((D, IN_W)), _rows(tm, CH), _rows(tm, CH), _rows(tm, CH)],
        out_specs=[_rows(tm, D), _rows(tm, D), _rows(tm, D), _rows(tm, D), _rows(tm, KVW), _rows(tm, KVW),
                   _rows(tm, D), _rows(tm, D)],
        out_shape=[sd((T, D), BF16), sd((T, D), F32), sd((T, D), F32), sd((T, D), BF16), sd((T, KVW), BF16),
                   sd((T, KVW), BF16), sd((T, D), BF16), sd((T, D), BF16)],
        sem=("parallel",), comm=comm)


def _sgu_common(u, vs, lng, lnb, ws_ref, bfull):
    nc = u.shape[0] // CH
    ug, tu = _gelu(u)
    vg, tv = _gelu(vs)
    mu = jnp.mean(vg, axis=-1, keepdims=True)
    xc = vg - mu
    rstd = lax.rsqrt(jnp.mean(xc * xc, axis=-1, keepdims=True) + EPS)
    vhat = xc * rstd
    vnb = (vhat * lng + lnb).astype(BF16)
    tri = lax.broadcasted_iota(jnp.int32, (CH, CH), 0) >= lax.broadcasted_iota(jnp.int32, (CH, CH), 1)
    wts, rhss, mixed = [], [], []
    for g in range(NG):
        wt = jnp.where(tri, ws_ref[g], 0.0).astype(BF16)
        rhs = jnp.concatenate([vnb[CH * c:CH * (c + 1), CH * g:CH * (g + 1)] for c in range(nc)], axis=1)
        mix = _dot(wt, rhs)
        wts.append(wt)
        rhss.append(rhs)
        mixed.append([mix[:, CH * c:CH * (c + 1)] + bfull[:, CH * g:CH * (g + 1)] for c in range(nc)])
    return nc, ug, tu, tv, rstd, vhat, tri, wts, rhss, mixed


def _sgu_fwd(u, vs, lng, lnb, ws, bfull, tm, comm=None):
    T = u.shape[0]

    def body(u_ref, v_ref, lng_ref, lnb_ref, ws_ref, bf_ref, a_ref):
        nc, ug, _, _, _, _, _, _, _, mixed = _sgu_common(
            u_ref[...], v_ref[...], lng_ref[...], lnb_ref[...], ws_ref, bf_ref[...])
        mixed_all = jnp.concatenate(
            [jnp.concatenate([mixed[g][c] for g in range(NG)], axis=1) for c in range(nc)], axis=0)
        a_ref[...] = (ug * mixed_all).astype(BF16)

    return _call(
        body, (u, vs, lng, lnb, ws, bfull), name="sgu_fwd", grid=(T // tm,),
        in_specs=[_rows(tm, D), _rows(tm, D), _const((1, D)), _const((1, D)), _const((NG, CH, CH)), _const((CH, D))],
        out_specs=_rows(tm, D), out_shape=jax.ShapeDtypeStruct((T, D), BF16), sem=("parallel",), comm=comm)


def _sgu_bwd(u, vs, da, lng, lnb, ws, bfull, tm, comm=None):
    T = u.shape[0]
    nsteps = T // tm

    def body(u_ref, v_ref, da_ref, lng_ref, lnb_ref, ws_ref, bf_ref,
             du_ref, dv_ref, dws_ref, dbs_ref, dlg_ref, dlb_ref, db_ref):
        i = pl.program_id(0)
        u, vs, da, lng = u_ref[...], v_ref[...], da_ref[...], lng_ref[...]
        nc, ug, tu, tv, rstd, vhat, tri, wts, rhss, mixed = _sgu_common(u, vs, lng, lnb_ref[...], ws_ref, bf_ref[...])

        @pl.when(i == 0)
        def _():
            dws_ref[...] = jnp.zeros_like(dws_ref)
            db_ref[...] = jnp.zeros_like(db_ref)
            dlg_ref[...] = jnp.zeros_like(dlg_ref)
            dlb_ref[...] = jnp.zeros_like(dlb_ref)

        mixed_all = jnp.concatenate(
            [jnp.concatenate([mixed[g][c] for g in range(NG)], axis=1) for c in range(nc)], axis=0)
        du_ref[...] = (da * mixed_all * _gelu_grad(tu)).astype(BF16)
        dmixed = da * ug
        dvn_cols = []
        for g in range(NG):
            dmix = [dmixed[CH * c:CH * (c + 1), CH * g:CH * (g + 1)] for c in range(nc)]
            db_ref[:, CH * g:CH * (g + 1)] += functools.reduce(lambda a, b: a + b, dmix)
            dm = jnp.concatenate(dmix, axis=1).astype(BF16)
            dws_ref[g] += _dot_nt(dm, rhss[g])
            dvn_cols.append(_dot_tn(wts[g], dm))
        dvn = jnp.concatenate(
            [jnp.concatenate([dvn_cols[g][:, CH * c:CH * (c + 1)] for g in range(NG)], axis=1) for c in range(nc)],
            axis=0)
        dlg_ref[...] += jnp.sum(dvn * vhat, axis=0, keepdims=True)
        dlb_ref[...] += jnp.sum(dvn, axis=0, keepdims=True)
        dvh = dvn * lng
        dvg = rstd * (dvh - jnp.mean(dvh, axis=-1, keepdims=True)
                      - vhat * jnp.mean(dvh * vhat, axis=-1, keepdims=True))
        dv_ref[...] = (dvg * _gelu_grad(tv)).astype(BF16)

        @pl.when(i == nsteps - 1)
        def _():
            for g in range(NG):
                dws_ref[g] = jnp.where(tri, dws_ref[g], 0.0)
                dbs_ref[g:g + 1, :] = jnp.sum(db_ref[:, CH * g:CH * (g + 1)].T, axis=0, keepdims=True)

    sd = jax.ShapeDtypeStruct
    return _call(
        body, (u, vs, da, lng, lnb, ws, bfull), name="sgu_bwd", grid=(nsteps,),
        in_specs=[_rows(tm, D), _rows(tm, D), _rows(tm, D), _const((1, D)), _const((1, D)), _const((NG, CH, CH)),
                  _const((CH, D))],
        out_specs=[_rows(tm, D), _rows(tm, D), _const((NG, CH, CH)), _const((NG, CH)), _const((1, D)), _const((1, D))],
        out_shape=[sd((T, D), BF16), sd((T, D), BF16), sd((NG, CH, CH), F32), sd((NG, CH), F32), sd((1, D), F32),
                   sd((1, D), F32)],
        scratch_shapes=[pltpu.VMEM((CH, D), F32)], sem=("arbitrary",), comm=comm)


def _pair_layout(prev, cur, grp):
    j, half = grp // 2, grp % 2
    blk = jnp.concatenate([prev[:, CH * j:CH * (j + 1)], cur[:, CH * j:CH * (j + 1)]], axis=0).astype(F32)
    lo = lax.broadcasted_iota(jnp.int32, blk.shape, 1) < HD
    rolled = pltpu.roll(blk, HD, 1)
    even = jnp.where(lo, blk if half == 0 else rolled, 0.0)
    odd = jnp.where(lo, 0.0, rolled if half == 0 else blk)
    return jnp.concatenate([even, odd], axis=0).astype(BF16)


def _attn_mask(n):
    qi = lax.broadcasted_iota(jnp.int32, (CH, 2 * CH), 0)
    kc = lax.broadcasted_iota(jnp.int32, (CH, 2 * CH), 1)
    ok = (kc > qi) & (kc <= qi + CH) & ((kc >= CH) | (n > 0))
    return jnp.concatenate([ok, ok], axis=1)


def _softmax_sink(s, sink):
    m = jnp.maximum(jnp.max(s, axis=-1, keepdims=True), sink)
    p = jnp.exp(s - m)
    ps = jnp.exp(sink - m)
    inv = 1.0 / (jnp.sum(p, axis=-1, keepdims=True) + ps)
    return p * inv, ps * inv


def _attn_fwd(q, k, va, sinks, comm=None):
    T = q.shape[0]
    nb = T // CH

    def body(sk_ref, q_ref, kp_ref, kc_ref, vp_ref, vc_ref, o_ref):
        n = pl.program_id(0)
        mask = _attn_mask(n)
        kp, kc, vp, vc = kp_ref[...], kc_ref[...], vp_ref[...], vc_ref[...]
        kks = [_pair_layout(kp, kc, grp) for grp in range(NKV)]
        vvs = [_pair_layout(vp, vc, grp) for grp in range(NKV)]
        npairs = D // CH

        def scores(p):
            return _dot_nt(q_ref[:, CH * p:CH * (p + 1)], kks[p // 2])

        ahead = 3
        outs, probs = [], []
        pending = [scores(p) for p in range(ahead)]
        even_lanes = lax.broadcasted_iota(jnp.int32, (CH, CH), 1) < HD

        def unnormalised(s, sink):
            m = jnp.maximum(jnp.max(s, axis=-1, keepdims=True), sink)
            p = jnp.exp(s - m)
            return p, 1.0 / (jnp.sum(p, axis=-1, keepdims=True) + jnp.exp(sink - m))

        def value_product(p):
            pr, ie, io = probs[p]
            return _dot(pr, vvs[p // 2]) * jnp.where(even_lanes, ie, io)

        for p in range(npairs):
            s = jnp.where(mask, pending.pop(0), -1e30)
            if p + ahead < npairs:
                pending.append(scores(p + ahead))
            pe, ie = unnormalised(s[:, :2 * CH], sk_ref[2 * p])
            po, io = unnormalised(s[:, 2 * CH:], sk_ref[2 * p + 1])
            probs.append((jnp.concatenate([pe, po], axis=1).astype(BF16), ie, io))
            if p >= 1:
                outs.append(value_product(p - 1))
        outs.append(value_product(npairs - 1))
        o_ref[...] = jnp.concatenate(outs, axis=1).astype(BF16)

    prev = lambda n: (jnp.maximum(n - 1, 0), 0)
    cur = lambda n: (n, 0)
    return _call(
        body, (sinks, q, k, k, va, va), name="attn_fwd", grid=(nb,),
        in_specs=[pl.BlockSpec(memory_space=pltpu.SMEM), pl.BlockSpec((CH, D), cur),
                  pl.BlockSpec((CH, KVW), prev), pl.BlockSpec((CH, KVW), cur),
                  pl.BlockSpec((CH, KVW), prev), pl.BlockSpec((CH, KVW), cur)],
        out_specs=pl.BlockSpec((CH, D), cur), out_shape=jax.ShapeDtypeStruct((T, D), BF16),
        sem=("parallel",), comm=comm)


def _attn_bwd(q, k, va, datt, sinks, rc, rs1, rs2, comm=None):
    T = q.shape[0]
    nb = T // CH

    def body(sk_ref, q_ref, kp_ref, kc_ref, vp_ref, vc_ref, do_ref, cq_ref, s1q_ref, s2q_ref, ck_ref, s1k_ref, s2k_ref,
             dq_ref, dk_ref, dv_ref, dsk_ref, kcar, vcar):
        n = pl.program_id(0)

        @pl.when(n == 0)
        def _():
            kcar[...] = jnp.zeros_like(kcar)
            vcar[...] = jnp.zeros_like(vcar)
            dsk_ref[...] = jnp.zeros_like(dsk_ref)

        def flush(kprev, vprev):
            ck, s1k, s2k = ck_ref[...], s1k_ref[...], s2k_ref[...]
            for j in range(KVW // CH):
                sl = slice(CH * j, CH * (j + 1))
                dk_ref[:, sl] = _rope_t(kcar[:, sl] + kprev[:, sl], ck, s1k, s2k).astype(BF16)
                dv_ref[:, sl] = (vcar[:, sl] + vprev[:, sl]).astype(BF16)

        @pl.when(n < nb)
        def _():
            mask = _attn_mask(n)
            kp, kc, vp, vc = kp_ref[...], kc_ref[...], vp_ref[...], vc_ref[...]
            cq, s1q, s2q = cq_ref[...], s1q_ref[...], s2q_ref[...]
            lane = lax.broadcasted_iota(jnp.int32, (1, CH), 1)
            dsk = jnp.zeros((1, CH), F32)
            npairs = D // CH
            kks = [_pair_layout(kp, kc, grp) for grp in range(NKV)]
            vvs = [_pair_layout(vp, vc, grp) for grp in range(NKV)]
            qs = [q_ref[:, CH * p:CH * (p + 1)] for p in range(npairs)]
            dos = [do_ref[:, CH * p:CH * (p + 1)].astype(BF16) for p in range(npairs)]

            def first(p):
                return _dot_nt(qs[p], kks[p // 2]), _dot_nt(dos[p], vvs[p // 2])

            def last(p, ds, pb):
                return (_rope_t(_dot(ds, kks[p // 2]), cq, s1q, s2q) * Q_SCALE, _dot_tn(qs[p], ds), _dot_tn(dos[p], pb))

            ahead = 2
            pending = [first(p) for p in range(ahead)]
            mids, ends = [], []
            for p in range(npairs):
                s, dp = pending.pop(0)
                s = jnp.where(mask, s, -1e30)
                if p + ahead < npairs:
                    pending.append(first(p + ahead))
                ds_parts, p_parts = [], []
                for par in range(2):
                    sl = slice(2 * CH * par, 2 * CH * (par + 1))
                    pr, psink = _softmax_sink(s[:, sl], sk_ref[2 * p + par])
                    delta = jnp.sum(pr * dp[:, sl], axis=-1, keepdims=True)
                    ds_parts.append(pr * (dp[:, sl] - delta))
                    p_parts.append(pr)
                    tot = -jnp.sum(psink * delta, axis=0, keepdims=True)
                    dsk = dsk + jnp.where(lane == 2 * p + par, tot, 0.0)
                mids.append((jnp.concatenate(ds_parts, axis=1).astype(BF16), jnp.concatenate(p_parts, axis=1).astype(BF16)))
                if p >= 1:
                    ends.append(last(p - 1, *mids[p - 1]))
            ends.append(last(npairs - 1, *mids[-1]))
            dq_cols = [e[0] for e in ends]
            def fold(i):
                rows = []
                for grp in range(NKV):
                    acc = ends[2 * grp][i] + ends[2 * grp + 1][i]
                    rows.append(acc[:HD, :2 * CH] + acc[HD:, 2 * CH:])
                return jnp.concatenate(rows, axis=0).T

            dkf, dvf = fold(1), fold(2)
            dq_ref[...] = jnp.concatenate(dq_cols, axis=1).astype(BF16)
            dsk_ref[...] += dsk
            flush(dkf[:CH], dvf[:CH])
            kcar[...] = dkf[CH:]
            vcar[...] = dvf[CH:]

        @pl.when(n == nb)
        def _():
            z = jnp.zeros((CH, KVW), F32)
            flush(z, z)

    last = nb - 1
    cur = lambda n: (jnp.minimum(n, last), 0)
    prev = lambda n: (jnp.clip(n - 1, 0, last), 0)
    sd = jax.ShapeDtypeStruct
    return _call(
        body, (sinks, q, k, k, va, va, datt, rc, rs1, rs2, rc, rs1, rs2), name="attn_bwd", grid=(nb + 1,),
        in_specs=[pl.BlockSpec(memory_space=pltpu.SMEM), pl.BlockSpec((CH, D), cur),
                  pl.BlockSpec((CH, KVW), prev), pl.BlockSpec((CH, KVW), cur),
                  pl.BlockSpec((CH, KVW), prev), pl.BlockSpec((CH, KVW), cur),
                  pl.BlockSpec((CH, D), cur),
                  pl.BlockSpec((CH, CH), cur), pl.BlockSpec((CH, CH), cur), pl.BlockSpec((CH, CH), cur),
                  pl.BlockSpec((CH, CH), prev), pl.BlockSpec((CH, CH), prev), pl.BlockSpec((CH, CH), prev)],
        out_specs=[pl.BlockSpec((CH, D), cur), pl.BlockSpec((CH, KVW), prev), pl.BlockSpec((CH, KVW), prev),
                   _const((1, CH))],
        out_shape=[sd((T, D), BF16), sd((T, KVW), BF16), sd((T, KVW), BF16), sd((1, CH), F32)],
        scratch_shapes=[pltpu.VMEM((CH, KVW), F32), pltpu.VMEM((CH, KVW), F32)], sem=("arbitrary",), comm=comm)


def _merge_fwd(a, att, ga, gb, x, w_a, w_b, w_o, g2, tm):
    T = x.shape[0]

    def body(a_ref, att_ref, ga_ref, gb_ref, x_ref, wa_ref, wb_ref, wo_ref, g_ref,
             pa_ref, pb_ref, mg_ref, mix_ref, x1_ref):
        pa = _dot(a_ref[...], wa_ref[...])
        pb = _dot(att_ref[...], wb_ref[...])
        pa_ref[...] = pa.astype(BF16)
        pb_ref[...] = pb.astype(BF16)
        merged = (_sigmoid(ga_ref[...].astype(F32)) * pa + _sigmoid(gb_ref[...].astype(F32)) * pb).astype(BF16)
        mg_ref[...] = merged
        mix = _dot(merged, wo_ref[...])
        mix_ref[...] = mix
        mhat, _ = _rms_hat(mix)
        x1_ref[...] = x_ref[...] + mhat * g_ref[...]

    sd = jax.ShapeDtypeStruct
    return pl.pallas_call(
        body, name="merge_fwd", grid=(T // tm,),
        in_specs=[_rows(tm, D)] * 5 + [_resident((D, D))] * 3 + [_const((1, D))],
        out_specs=[_rows(tm, D)] * 5,
        out_shape=[sd((T, D), BF16), sd((T, D), BF16), sd((T, D), BF16), sd((T, D), F32), sd((T, D), F32)],
        compiler_params=_cparams(("parallel",)),
    )(a, att, ga, gb, x, w_a, w_b, w_o, g2)


def _merge_bwd(dx1, mix, ga, gb, pa, pb, a, att, merged, w_a, w_b, w_o, g2, tm, comm=None):
    T = dx1.shape[0]
    nsteps = T // tm

    def body(dx1_ref, mix_ref, ga_ref, gb_ref, pa_ref, pb_ref, a_ref, att_ref, mg_ref, wa_ref, wb_ref, wo_ref, g_ref,
             dga_ref, dgb_ref, da_ref, datt_ref, dg_ref, dwa_ref, dwb_ref, dwo_ref, acc, sem):
        i = pl.program_id(0)

        @pl.when(i == 0)
        def _():
            dg_ref[...] = jnp.zeros_like(dg_ref)
            acc[...] = jnp.zeros_like(acc)

        mhat, r = _rms_hat(mix_ref[...])
        dmix, dg = _rms_bwd(mhat, r, g_ref[...], dx1_ref[...])
        dg_ref[...] += dg
        dmix = dmix.astype(BF16)
        dmerged = _dot_nt(dmix, wo_ref[...])
        sa = _sigmoid(ga_ref[...].astype(F32))
        sb = _sigmoid(gb_ref[...].astype(F32))
        dao = (dmerged * sa).astype(BF16)
        dbo = (dmerged * sb).astype(BF16)
        dga_ref[...] = (dmerged * pa_ref[...].astype(F32) * (sa * (1.0 - sa))).astype(BF16)
        dgb_ref[...] = (dmerged * pb_ref[...].astype(F32) * (sb * (1.0 - sb))).astype(BF16)
        da_ref[...] = _dot_nt(dao, wa_ref[...])
        datt_ref[...] = _dot_nt(dbo, wb_ref[...]).astype(BF16)
        acc[0] += _dot_tn(a_ref[...], dao)
        acc[1] += _dot_tn(att_ref[...], dbo)
        acc[2] += _dot_tn(mg_ref[...], dmix)

        @pl.when(i == nsteps - 1)
        def _():
            outs = [pltpu.make_async_copy(acc.at[j], ref, sem.at[j]) for j, ref in enumerate((dwa_ref, dwb_ref, dwo_ref))]
            for cp in outs:
                cp.start()
            for cp in outs:
                cp.wait()

    sd = jax.ShapeDtypeStruct
    return _call(
        body, (dx1, mix, ga, gb, pa, pb, a, att, merged, w_a, w_b, w_o, g2), name="merge_bwd", grid=(nsteps,),
        in_specs=[_rows(tm, D)] * 9 + [_resident((D, D))] * 3 + [_const((1, D))],
        out_specs=[_rows(tm, D)] * 4 + [_const((1, D))] + [ANY] * 3,
        out_shape=[sd((T, D), BF16), sd((T, D), BF16), sd((T, D), F32), sd((T, D), BF16), sd((1, D), F32)]
        + [sd((D, D), F32)] * 3,
        scratch_shapes=[pltpu.VMEM((3, D, D), F32), _dma_sems(3)], sem=("arbitrary",), comm=comm)


def _ffn(x1, target, w1, w2, g3, g4, tm):
    T = x1.shape[0]

    def body(x_ref, t_ref, w1_ref, w2_ref, g3_ref, g4_ref,
             hf_ref, f2_ref, dff_ref, df1_ref, dx_ref, ls_ref, dg3_ref, dg4_ref):
        @pl.when(pl.program_id(0) == 0)
        def _():
            ls_ref[...] = jnp.zeros_like(ls_ref)
            dg3_ref[...] = jnp.zeros_like(dg3_ref)
            dg4_ref[...] = jnp.zeros_like(dg4_ref)

        x = x_ref[...]
        g3, g4 = g3_ref[...], g4_ref[...]
        xhat, r3 = _rms_hat(x)
        hf = (xhat * g3).astype(BF16)
        hf_ref[...] = hf
        rl = jnp.maximum(_dot(hf, w1_ref[...]), 0.0)
        f2 = (rl * rl).astype(BF16)
        f2_ref[...] = f2
        fhat, r4 = _rms_hat(_dot(f2, w2_ref[...]))
        err = x + fhat * g4 - t_ref[...]
        ls_ref[...] += jnp.sum(err * err, axis=0, keepdims=True)
        dy = err * (1.0 / D)
        dff, dg4 = _rms_bwd(fhat, r4, g4, dy)
        dg4_ref[...] += dg4
        dff = dff.astype(BF16)
        dff_ref[...] = dff
        df1 = (_dot_nt(dff, w2_ref[...]) * (2.0 * rl)).astype(BF16)
        df1_ref[...] = df1
        dxn, dg3 = _rms_bwd(xhat, r3, g3, _dot_nt(df1, w1_ref[...]))
        dg3_ref[...] += dg3
        dx_ref[...] = dy + dxn

    sd = jax.ShapeDtypeStruct
    return pl.pallas_call(
        body, name="ffn_fwd_bwd", grid=(T // tm,),
        in_specs=[_rows(tm, D), _rows(tm, D), _resident((D, DFF)), _resident((DFF, D)), _const((1, D)), _const((1, D))],
        out_specs=[_rows(tm, D), _rows(tm, DFF), _rows(tm, D), _rows(tm, DFF), _rows(tm, D), _const((1, D)),
                   _const((1, D)), _const((1, D))],
        out_shape=[sd((T, D), BF16), sd((T, DFF), BF16), sd((T, D), BF16), sd((T, DFF), BF16), sd((T, D), F32),
                   sd((1, D), F32), sd((1, D), F32), sd((1, D), F32)],
        compiler_params=_cparams(("arbitrary",)),
    )(x1, target, w1, w2, g3, g4)


def _inproj_bwd(parts, x, dx1, g1, w_in, tm, comm=None):
    T = x.shape[0]
    widths = [p.shape[1] for p in parts]
    offs = [sum(widths[:i]) for i in range(len(widths) + 1)]
    assert offs[-1] == IN_W

    def body(*refs):
        n = len(parts)
        prefs = refs[:n]
        x_ref, dx1_ref, g_ref, w_ref, dx_ref, dp_ref, dg_ref = refs[n:]

        @pl.when(pl.program_id(0) == 0)
        def _():
            dg_ref[...] = jnp.zeros_like(dg_ref)

        for i in range(n):
            dp_ref[:, offs[i]:offs[i + 1]] = prefs[i][...]
        dh = _dot_nt(dp_ref[...], w_ref[...])
        xhat, r = _rms_hat(x_ref[...])
        dxn, dg = _rms_bwd(xhat, r, g_ref[...], dh)
        dg_ref[...] += dg
        dx_ref[...] = dx1_ref[...] + dxn

    sd = jax.ShapeDtypeStruct
    return _call(
        body, (*parts, x, dx1, g1, w_in), name="inproj_bwd", grid=(T // tm,),
        in_specs=[_rows(tm, w) for w in widths] + [_rows(tm, D), _rows(tm, D), _const((1, D)), _resident((D, IN_W))],
        out_specs=[_rows(tm, D), _rows(tm, IN_W), _const((1, D))],
        out_shape=[sd((T, D), F32), sd((T, IN_W), BF16), sd((1, D), F32)], sem=("arbitrary",), comm=comm)


def _wgrad(a, g, tn, tm, name, comm=None):
    T, K = a.shape
    N = g.shape[1]

    def body(a_ref, g_ref, o_ref):
        @pl.when(pl.program_id(1) == 0)
        def _():
            o_ref[...] = jnp.zeros_like(o_ref)

        o_ref[...] += _dot_tn(a_ref[...], g_ref[...])

    return _call(
        body, (a, g), name=name, grid=(N // tn, T // tm),
        in_specs=[pl.BlockSpec((tm, K), lambda j, t: (t, 0)), pl.BlockSpec((tm, tn), lambda j, t: (t, j))],
        out_specs=pl.BlockSpec((K, tn), lambda j, t: (0, j)),
        out_shape=jax.ShapeDtypeStruct((K, N), F32), sem=("parallel", "arbitrary"), comm=comm)


def _wgrad_rows_half(a, g, own, tn, tm, name, comm=None):
    T, K = a.shape
    N = g.shape[1]

    def body(lo_ref, hi_ref, g_ref, o_ref):
        @pl.when(pl.program_id(1) == 0)
        def _():
            o_ref[...] = jnp.zeros_like(o_ref)

        first = (lax.axis_index("c") == 0) == own
        o_ref[...] += _dot_tn(jnp.where(first, lo_ref[...], hi_ref[...]), g_ref[...])

    return _call(
        body, (a, a, g), name=name, grid=(N // tn, T // tm),
        in_specs=[pl.BlockSpec((tm, K // 2), lambda j, t: (t, 0)), pl.BlockSpec((tm, K // 2), lambda j, t: (t, 1)),
                  pl.BlockSpec((tm, tn), lambda j, t: (t, j))],
        out_specs=pl.BlockSpec((K // 2, tn), lambda j, t: (0, j)),
        out_shape=jax.ShapeDtypeStruct((K // 2, N), F32), sem=("arbitrary", "arbitrary"), comm=comm)


def _adamw(w, g, m, v, tr, name):
    R, C = w.shape
    bc1 = 1.0 / (1.0 - B1 ** STEP)
    bc2 = 1.0 / (1.0 - B2 ** STEP)

    def body(w_ref, g_ref, m_ref, v_ref, go_ref, d_ref, nm_ref, nv_ref):
        g = g_ref[...]
        go_ref[...] = g
        m = B1 * m_ref[...] + (1.0 - B1) * g
        v = B2 * v_ref[...] + (1.0 - B2) * (g * g)
        nm_ref[...] = m
        nv_ref[...] = v
        d_ref[...] = -LR * ((m * bc1) / (jnp.sqrt(v * bc2) + AEPS) + WD * w_ref[...])

    spec = pl.BlockSpec((tr, C), lambda i: (i, 0))
    return pl.pallas_call(
        body, name=name, grid=(R // tr,), in_specs=[spec] * 4, out_specs=[spec] * 4,
        out_shape=[jax.ShapeDtypeStruct((R, C), F32)] * 4,
        compiler_params=_cparams(("parallel",)),
    )(w, g, m, v)


BIG = (("col", (D, IN_W)), ("row", (D, D)), ("row", (D, D)), ("row", (D, D)), ("col", (D, DFF)), ("row", (DFF, D)))
NBIG = len(BIG)
ANY = pl.BlockSpec(memory_space=pl.ANY)


def _shard_shape(kind, shape):
    R, C = shape
    return (R, C // 4) if kind == "col" else (R // 4, C)


def _half_shape(kind, shape):
    R, C = shape
    return (R // 2, C) if kind == "col" else (R, C // 2)


def _piece_shape(kind, shape):
    R, C = shape
    return (R // 2, C // 4) if kind == "col" else (R // 4, C // 2)


def _own_region(ref, kind, shape, s):
    R, C = shape
    return ref.at[:, pl.ds(s * (C // 4), C // 4)] if kind == "col" else ref.at[pl.ds(s * (R // 4), R // 4), :]


def _ag_region(ref, kind, shape, s, hc):
    R, C = shape
    if kind == "col":
        return ref.at[pl.ds(hc * (R // 2), R // 2), pl.ds(s * (C // 4), C // 4)]
    return ref.at[pl.ds(s * (R // 4) + hc * (R // 8), R // 8), :]


def _ag_shard_half(ref, kind, shape, hc):
    R, C = shape
    return ref.at[pl.ds(hc * (R // 2), R // 2), :] if kind == "col" else ref.at[pl.ds(hc * (R // 8), R // 8), :]


def _grad_half(ref, kind, shape, hc):
    R, C = shape
    return ref.at[pl.ds(hc * (R // 2), R // 2), :] if kind == "col" else ref.at[:, pl.ds(hc * (C // 2), C // 2)]


def _half_piece(ref, kind, shape, s):
    R, C = shape
    return ref.at[:, pl.ds(s * (C // 4), C // 4)] if kind == "col" else ref.at[pl.ds(s * (R // 4), R // 4), :]


def _place():
    x, y, c = lax.axis_index("x"), lax.axis_index("y"), lax.axis_index("c")
    chips = [(1 - x, y), (x, 1 - y), (1 - x, 1 - y)]
    return x, y, c, chips


def _rcopy(src, dst, ssem, rsem, dev):
    return pltpu.make_async_remote_copy(src_ref=src, dst_ref=dst, send_sem=ssem, recv_sem=rsem,
                                        device_id=dev, device_id_type=MESH)


def _dma_sems(n):
    return pltpu.SemaphoreType.DMA((n,))


def _x_gather_ici(shards, ws):
    n = len(ws)
    specs = [BIG[w] for w in ws]

    def place():
        x, y, c, chips = _place()
        return c, chips, 2 * x + y

    def sends(sh, full, sc):
        c, chips, me_s = place()
        return [_rcopy(_ag_shard_half(sh[i], kind, shape, c), _ag_region(full[i], kind, shape, me_s, c),
                       sc[0].at[3 * i + j], sc[1].at[3 * i + j], (cx, cy, c))
                for i, (kind, shape) in enumerate(specs) for j, (cx, cy) in enumerate(chips)]

    def start(sh, full, sc):
        for i in range(n):
            pltpu.make_async_copy(sh[i], sc[4 + i], sc[2].at[i]).start()
        for cp in sends(sh, full, sc):
            cp.start()

    def finish(sh, full, sc):
        c, chips, me_s = place()
        stores = []
        for i, (kind, shape) in enumerate(specs):
            pltpu.make_async_copy(sh[i], sc[4 + i], sc[2].at[i]).wait()
            st = pltpu.make_async_copy(sc[4 + i], _own_region(full[i], kind, shape, me_s), sc[3].at[i])
            st.start()
            stores.append(st)
        for i, (kind, shape) in enumerate(specs):
            for j, (cx, cy) in enumerate(chips):
                reg = _ag_region(full[i], kind, shape, 2 * cx + cy, c)
                _rcopy(reg, reg, sc[0].at[3 * i + j], sc[1].at[3 * i + j], (cx, cy, c)).wait_recv()
        for cp in sends(sh, full, sc):
            cp.wait_send()
        for st in stores:
            st.wait()

    return _Exchange(
        shards, [jax.ShapeDtypeStruct(shape, BF16) for _, shape in specs], {},
        [_dma_sems(3 * n), _dma_sems(3 * n), _dma_sems(n), _dma_sems(n)]
        + [pltpu.VMEM(_shard_shape(k, s), BF16) for k, s in specs], start, finish)


def _x_gather_d2d(wholes, ws):
    specs = [BIG[w] for w in ws]
    n = len(ws)

    def copies(full, sc, mine):
        x, y, c, chips = _place()
        hc = c if mine else 1 - c
        return [_rcopy(reg, reg, sc[0].at[3 * i + j], sc[1].at[3 * i + j], (x, y, 1 - c))
                for i, (kind, shape) in enumerate(specs) for j, (cx, cy) in enumerate(chips)
                for reg in [_ag_region(full[i], kind, shape, 2 * cx + cy, hc)]]

    def start(_, full, sc):
        for cp in copies(full, sc, True):
            cp.start()

    def finish(_, full, sc):
        for cp in copies(full, sc, False):
            cp.wait_recv()
        for cp in copies(full, sc, True):
            cp.wait_send()

    return _Exchange(wholes, [jax.ShapeDtypeStruct(shape, BF16) for _, shape in specs], {i: i for i in range(n)},
                     [_dma_sems(3 * n), _dma_sems(3 * n)], start, finish)


def _x_grads_sibling(grads, ws):
    specs = [BIG[w] for w in ws]
    n = len(ws)

    def copies(g, got, sc):
        x, y, c, _ = _place()
        return [_rcopy(_grad_half(g[i], kind, shape, 1 - c), got[i], sc[0].at[i], sc[1].at[i], (x, y, 1 - c))
                for i, (kind, shape) in enumerate(specs)]

    def start(g, got, sc):
        for cp in copies(g, got, sc):
            cp.start()

    def finish(g, got, sc):
        for cp in copies(g, got, sc):
            cp.wait_recv()
        for cp in copies(g, got, sc):
            cp.wait_send()

    return _Exchange(grads, [jax.ShapeDtypeStruct(_half_shape(k, s), F32) for k, s in specs], {},
                     [_dma_sems(n), _dma_sems(n)], start, finish)


def _x_to_sibling(arr):
    def copy(ins, outs, sc):
        x, y, c, _ = _place()
        return _rcopy(ins[0], outs[0], sc[0].at[0], sc[1].at[0], (x, y, 1 - c))

    def finish(ins, outs, sc):
        copy(ins, outs, sc).wait_recv()
        copy(ins, outs, sc).wait_send()

    return _Exchange([arr], [jax.ShapeDtypeStruct(arr.shape, arr.dtype)], {}, [_dma_sems(1), _dma_sems(1)],
                     lambda ins, outs, sc: copy(ins, outs, sc).start(), finish)


def _x_grads_chips(sums_bf, ws):
    specs = [BIG[w] for w in ws]
    n = len(ws)

    def copies(s16, got, sc):
        x, y, c, chips = _place()
        return [_rcopy(_half_piece(s16[i], kind, shape, 2 * cx + cy), got[i].at[j],
                       sc[0].at[3 * i + j], sc[1].at[3 * i + j], (cx, cy, c))
                for i, (kind, shape) in enumerate(specs) for j, (cx, cy) in enumerate(chips)]

    def start(s16, got, sc):
        for cp in copies(s16, got, sc):
            cp.start()

    def finish(s16, got, sc):
        for cp in copies(s16, got, sc):
            cp.wait_recv()
        for cp in copies(s16, got, sc):
            cp.wait_send()

    return _Exchange(sums_bf, [jax.ShapeDtypeStruct((3,) + _piece_shape(k, s), BF16) for k, s in specs], {},
                     [_dma_sems(3 * n), _dma_sems(3 * n)], start, finish)


def _shard_half(ref, kind, shape, hc):
    sr, sc = _shard_shape(kind, shape)
    return ref.at[pl.ds(hc * (sr // 2), sr // 2), :] if kind == "col" else ref.at[:, pl.ds(hc * (sc // 2), sc // 2)]


def _x_grads_share(shard_grads, ws):
    specs = [BIG[w] for w in ws]
    n = len(ws)

    def copies(g, sc, mine):
        x, y, c, _ = _place()
        hc = c if mine else 1 - c
        return [_rcopy(part, part, sc[0].at[i], sc[1].at[i], (x, y, 1 - c))
                for i, (kind, shape) in enumerate(specs) for part in [_shard_half(g[i], kind, shape, hc)]]

    def start(_, g, sc):
        for cp in copies(g, sc, True):
            cp.start()

    def finish(_, g, sc):
        for cp in copies(g, sc, False):
            cp.wait_recv()
        for cp in copies(g, sc, True):
            cp.wait_send()

    return _Exchange(shard_grads, [jax.ShapeDtypeStruct(_shard_shape(k, s), F32) for k, s in specs],
                     {i: i for i in range(n)}, [_dma_sems(n), _dma_sems(n)], start, finish)


ADD_BLOCK_BYTES = 4 * 1024 * 1024


def _add_rows(rows, cols):
    r = rows
    while r > 256 and r * cols * 4 > ADD_BLOCK_BYTES:
        r //= 2
    return r


def _add_halves(place, g, got, kind, name):
    R, C = g.shape
    hr, hcols = _half_shape(kind, (R, C))
    blk_rows = _add_rows(hr, hcols)
    steps = hr // blk_rows

    def body(p_ref, g_ref, b_ref, s_ref, sb_ref):
        s = g_ref[...] + b_ref[...]
        s_ref[...] = s
        sb_ref[...] = s.astype(BF16)

    if kind == "col":
        g_spec = pl.BlockSpec((blk_rows, C), lambda i, p: (p[0] * steps + i, 0))
    else:
        g_spec = pl.BlockSpec((blk_rows, hcols), lambda i, p: (i, p[0]))
    spec = pl.BlockSpec((blk_rows, hcols), lambda i, p: (i, 0))
    return pl.pallas_call(
        body, name=name,
        grid_spec=pltpu.PrefetchScalarGridSpec(num_scalar_prefetch=1, grid=(steps,), in_specs=[g_spec, spec],
                                               out_specs=[spec, spec]),
        out_shape=[jax.ShapeDtypeStruct((hr, hcols), F32), jax.ShapeDtypeStruct((hr, hcols), BF16)],
        compiler_params=_cparams(("parallel",)),
    )(place, g, got)


def _add_pair(a, b, name):
    R, C = a.shape
    rows = _add_rows(R, C)

    def body(a_ref, b_ref, s_ref, sb_ref):
        s = a_ref[...] + b_ref[...]
        s_ref[...] = s
        sb_ref[...] = s.astype(BF16)

    spec = pl.BlockSpec((rows, C), lambda i: (i, 0))
    return pl.pallas_call(
        body, name=name, grid=(R // rows,), in_specs=[spec, spec], out_specs=[spec, spec],
        out_shape=[jax.ShapeDtypeStruct((R, C), F32), jax.ShapeDtypeStruct((R, C), BF16)],
        compiler_params=_cparams(("parallel",)),
    )(a, b)


def _add_pieces(place, half, got, kind, shape, name):
    pr, pc = _piece_shape(kind, shape)
    blk_rows = _add_rows(pr, pc)
    steps = pr // blk_rows

    def body(p_ref, m_ref, g_ref, o_ref):
        acc = m_ref[...]
        for j in range(3):
            acc = acc + g_ref[j].astype(F32)
        o_ref[...] = acc

    if kind == "col":
        m_spec = pl.BlockSpec((blk_rows, pc), lambda i, p: (i, p[1]))
        o_spec = pl.BlockSpec((blk_rows, pc), lambda i, p: (p[0] * steps + i, 0))
    else:
        m_spec = pl.BlockSpec((blk_rows, pc), lambda i, p: (p[1] * steps + i, 0))
        o_spec = pl.BlockSpec((blk_rows, pc), lambda i, p: (i, p[0]))
    return pl.pallas_call(
        body, name=name,
        grid_spec=pltpu.PrefetchScalarGridSpec(
            num_scalar_prefetch=1, grid=(steps,),
            in_specs=[m_spec, pl.BlockSpec((3, blk_rows, pc), lambda i, p: (0, i, 0))], out_specs=o_spec),
        out_shape=jax.ShapeDtypeStruct(_shard_shape(kind, shape), F32),
        compiler_params=_cparams(("parallel",)),
    )(place, half, got)


SMALL_ROWS = 1024 + 8 * 8 + 8


def _x_small_all_reduce(p):
    def parts(p_ref, sc):
        slots, ssem, rsem = sc[0], sc[2], sc[3]
        x, y, c = lax.axis_index("x"), lax.axis_index("y"), lax.axis_index("c")
        me = 4 * x + 2 * y + c
        out = []
        for r in range(1, 8):
            bx, by, bc = (r >> 2) & 1, (r >> 1) & 1, r & 1
            tgt = (1 - x if bx else x, 1 - y if by else y, 1 - c if bc else c)
            send = _rcopy(p_ref, slots.at[me], ssem.at[r - 1], rsem.at[r - 1], tgt)
            src = 4 * tgt[0] + 2 * tgt[1] + tgt[2]
            recv = _rcopy(p_ref, slots.at[src], ssem.at[r - 1], rsem.at[r - 1], tgt)
            out.append((send, recv))
        return me, out

    def start(ins, outs, sc):
        me, cps = parts(ins[0], sc)
        pltpu.make_async_copy(ins[0], sc[0].at[me], sc[4].at[0]).start()
        for send, _ in cps:
            send.start()

    def finish(ins, outs, sc):
        me, cps = parts(ins[0], sc)
        pltpu.make_async_copy(ins[0], sc[0].at[me], sc[4].at[0]).wait()
        for _, recv in cps:
            recv.wait_recv()
        acc = sc[0][0]
        for d in range(1, 8):
            acc = acc + sc[0][d]
        sc[1][...] = acc
        back = pltpu.make_async_copy(sc[1], outs[0], sc[4].at[1])
        back.start()
        for send, _ in cps:
            send.wait_send()
        back.wait()

    return _Exchange([p], [jax.ShapeDtypeStruct((SMALL_ROWS, CH), F32)], {},
                     [pltpu.VMEM((8, SMALL_ROWS, CH), F32), pltpu.VMEM((SMALL_ROWS, CH), F32), _dma_sems(7), _dma_sems(7),
                      _dma_sems(2)], start, finish)


def _rope_tables(positions, comm=None):
    T = positions.shape[0]
    inv_freq = 500000.0 ** (-jnp.arange(0, 2 * ROPE_HALF, 2, dtype=F32) / (2 * ROPE_HALF))
    head = jnp.concatenate([inv_freq, inv_freq, jnp.zeros((HD - 2 * ROPE_HALF,), F32)])
    lane_freq = jnp.concatenate([head, head])[None, :]
    pos = jnp.broadcast_to(positions.astype(F32)[:, None], (T, CH))
    tm = min(1024, T)

    def body(p_ref, f_ref, c_ref, s1_ref, s2_ref):
        ang = p_ref[...] * f_ref[...]
        sin = jnp.sin(ang)
        first = (lax.broadcasted_iota(jnp.int32, ang.shape, 1) % HD) < ROPE_HALF
        c_ref[...] = jnp.cos(ang)
        s1_ref[...] = jnp.where(first, -sin, 0.0)
        s2_ref[...] = jnp.where(first, 0.0, sin)

    return _call(body, (pos, lane_freq), name="rope_tables", grid=(T // tm,),
                 in_specs=[_rows(tm, CH), _const((1, CH))], out_specs=[_rows(tm, CH)] * 3,
                 out_shape=[jax.ShapeDtypeStruct((T, CH), F32)] * 3, sem=("parallel",), comm=comm)


BIG_NAMES = ("w_in", "w_a", "w_b", "w_o", "w_ff_in", "w_ff_out")
SMALL_NAMES = ("w_spatial", "ln_v_gain", "ln_v_bias", "b_spatial", "sinks", "norm_mix_pre", "norm_mix_post",
               "norm_ff_pre", "norm_ff_post")
WEIGHT_ORDER = ("w_in", "ln_v_gain", "ln_v_bias", "w_spatial", "b_spatial", "sinks", "w_a", "w_b", "w_o",
                "norm_mix_pre", "norm_mix_post", "w_ff_in", "w_ff_out", "norm_ff_pre", "norm_ff_post")


def _pack_small(d, loss_sums=None):
    parts = []
    for n in SMALL_NAMES:
        flat = d[n].reshape(-1)
        pad = (-flat.shape[0]) % (8 * CH)
        parts.append(jnp.pad(flat, (0, pad)).reshape(-1, CH))
    parts.append(jnp.zeros((8, CH), F32) if loss_sums is None else loss_sums.reshape(8, CH))
    return jnp.concatenate(parts, axis=0)


def _unpack_small(p, like):
    out, row = {}, 0
    for n in SMALL_NAMES:
        size = like[n].size
        rows = -(-size // (8 * CH)) * 8
        out[n] = p[row:row + rows].reshape(-1)[:size].reshape(like[n].shape)
        row += rows
    return out


def kernel(x, positions, w_in, ln_v_gain, ln_v_bias, w_spatial, b_spatial, sinks, w_a, w_b, w_o, norm_mix_pre, norm_mix_post, w_ff_in, w_ff_out, norm_ff_pre, norm_ff_post, loss_target, m_w_in, m_ln_v_gain, m_ln_v_bias, m_w_spatial, m_b_spatial, m_sinks, m_w_a, m_w_b, m_w_o, m_norm_mix_pre, m_norm_mix_post, m_w_ff_in, m_w_ff_out, m_norm_ff_pre, m_norm_ff_post, v_w_in, v_ln_v_gain, v_ln_v_bias, v_w_spatial, v_b_spatial, v_sinks, v_w_a, v_w_b, v_w_o, v_norm_mix_pre, v_norm_mix_post, v_w_ff_in, v_w_ff_out, v_norm_ff_pre, v_norm_ff_post):
    w = dict(w_in=w_in, ln_v_gain=ln_v_gain, ln_v_bias=ln_v_bias, w_spatial=w_spatial, b_spatial=b_spatial, sinks=sinks,
             w_a=w_a, w_b=w_b, w_o=w_o, norm_mix_pre=norm_mix_pre, norm_mix_post=norm_mix_post, w_ff_in=w_ff_in,
             w_ff_out=w_ff_out, norm_ff_pre=norm_ff_pre, norm_ff_post=norm_ff_post)
    m = dict(w_in=m_w_in, ln_v_gain=m_ln_v_gain, ln_v_bias=m_ln_v_bias, w_spatial=m_w_spatial, b_spatial=m_b_spatial,
             sinks=m_sinks, w_a=m_w_a, w_b=m_w_b, w_o=m_w_o, norm_mix_pre=m_norm_mix_pre, norm_mix_post=m_norm_mix_post,
             w_ff_in=m_w_ff_in, w_ff_out=m_w_ff_out, norm_ff_pre=m_norm_ff_pre, norm_ff_post=m_norm_ff_post)
    v = dict(w_in=v_w_in, ln_v_gain=v_ln_v_gain, ln_v_bias=v_ln_v_bias, w_spatial=v_w_spatial, b_spatial=v_b_spatial,
             sinks=v_sinks, w_a=v_w_a, w_b=v_w_b, w_o=v_w_o, norm_mix_pre=v_norm_mix_pre, norm_mix_post=v_norm_mix_post,
             w_ff_in=v_w_ff_in, w_ff_out=v_w_ff_out, norm_ff_pre=v_norm_ff_pre, norm_ff_post=v_norm_ff_post)

    FIRST, REST = (0,), tuple(range(1, NBIG))
    shards = [w[n][0].astype(BF16) for n in BIG_NAMES]
    place = jnp.stack([lax.axis_index("c"), 2 * lax.axis_index("x") + lax.axis_index("y")]).astype(jnp.int32)
    xs, target = x[0], loss_target[0]
    T = xs.shape[0]
    wtm, wtm2 = min(1024, T), min(2048, T)
    g1, g2, g3, g4 = norm_mix_pre, norm_mix_post, norm_ff_pre, norm_ff_post
    w_sp, snk = w_spatial[0], sinks[0]
    MIX, FF = (1, 2, 3), (4, 5)
    bfull = jnp.repeat(b_spatial[0].T, CH, axis=1)

    def reduce_tail(ws, grads, got):
        sums = [_add_halves(place, grads[i], got[i], BIG[k][0], name="grad_add_sibling_" + BIG_NAMES[k])
                for i, k in enumerate(ws)]
        return sums, _x_grads_chips([s[1] for s in sums], ws)

    def reduce_end(ws, sums, pieces):
        return [_add_pieces(place, sums[i][0], pieces[i], *BIG[k], name="grad_add_chips_" + BIG_NAMES[k])
                for i, k in enumerate(ws)]

    (rc, rs1, rs2), w_in_part = _rope_tables(positions[0], comm=_x_gather_ici(shards[:1], FIRST))
    w_in_b = _run(_x_gather_d2d(w_in_part, FIRST), "gather_w_in_d2d")[0]
    (h, u, vs, q, k, va, ga, gb), ff_part = _inproj(xs, g1, w_in_b, rc, rs1, rs2, tm=512, comm=_x_gather_ici(shards[4:], FF))
    a, mix_part = _sgu_fwd(u, vs, ln_v_gain, ln_v_bias, w_sp, bfull, tm=512, comm=_x_gather_ici(shards[1:4], MIX))
    att, rest = _attn_fwd(q, k, va, snk, comm=_both(_x_gather_d2d(mix_part, MIX), _x_gather_d2d(ff_part, FF)))
    w_a_b, w_b_b, w_o_b, w_ff_in_b, w_ff_out_b = rest
    pa, pb, merged, mix, x1 = _merge_fwd(a, att, ga, gb, xs, w_a_b, w_b_b, w_o_b, g2, tm=512)
    hf, f2, dff, df1, dx1, lsum, dg3, dg4 = _ffn(x1, target, w_ff_in_b, w_ff_out_b, g3, g4, tm=256)

    dw_ff_out, _ = _wgrad(f2, dff, tn=1024, tm=512, name="wgrad_ff_out")
    dw_ff_in, _ = _wgrad(hf, df1, tn=2048, tm=wtm2, name="wgrad_ff_in")
    (dga, dgb, da, datt, dg2, dw_a, dw_b, dw_o), _ = _merge_bwd(
        dx1, mix, ga, gb, pa, pb, a, att, merged, w_a_b, w_b_b, w_o_b, g2, tm=256)
    grads_rest = [dw_a, dw_b, dw_o, dw_ff_in, dw_ff_out]
    (du, dvs, dws, dbs, dlg, dlb), got_rest = _sgu_bwd(
        u, vs, da, ln_v_gain, ln_v_bias, w_sp, bfull, tm=512, comm=_x_grads_sibling(grads_rest, REST))
    sums_rest, to_chips = reduce_tail(REST, grads_rest, got_rest)
    (dq, dk, dva, dsk), pieces_rest = _attn_bwd(q, k, va, datt, snk, rc, rs1, rs2, comm=to_chips)
    partial_rest = reduce_end(REST, sums_rest, pieces_rest)
    (dx, dproj, dg1), _ = _inproj_bwd([du, dvs, dq, dk, dva, dga, dgb], xs, dx1, g1, w_in_b, tm=512)
    small = dict(ln_v_gain=dlg, ln_v_bias=dlb, w_spatial=dws, b_spatial=dbs, sinks=dsk[:, :NQ],
                 norm_mix_pre=dg1, norm_mix_post=dg2, norm_ff_pre=dg3, norm_ff_post=dg4)
    dw_in_send, (gs, *shard_rest) = _wgrad_rows_half(
        h, dproj, own=False, tn=IN_W // 2, tm=wtm2, name="wgrad_in_sibling_half",
        comm=_both(_x_small_all_reduce(_pack_small(small, lsum)), _x_grads_share(partial_rest, REST)))
    dw_in_keep, (got_in,) = _wgrad_rows_half(h, dproj, own=True, tn=IN_W // 2, tm=wtm2, name="wgrad_in_own_half",
                                             comm=_x_to_sibling(dw_in_send))
    sums_in = [_add_pair(dw_in_keep, got_in, name="grad_add_sibling_w_in")]
    to_chips = _x_grads_chips([sums_in[0][1]], FIRST)
    partial_in = reduce_end(FIRST, sums_in, _run(to_chips, "grads_in_to_chips"))
    shard_grads = list(_run(_x_grads_share(partial_in, FIRST), "grads_in_share")) + list(shard_rest)

    grad, delta, new_m, new_v = {}, {}, {}, {}
    for i, n in enumerate(BIG_NAMES):
        g_, d_, m_, v_ = _adamw(w[n][0], shard_grads[i], m[n][0], v[n][0], tr=256, name="adamw_" + n)
        grad[n], delta[n], new_m[n], new_v[n] = g_[None], d_[None], m_[None], v_[None]

    loss = 0.5 * jnp.sum(gs[SMALL_ROWS - 8:]) / D
    gs, ds, ms, vs = _adamw(_pack_small(w), gs, _pack_small(m), _pack_small(v), tr=SMALL_ROWS, name="adamw_small")
    for packed, dst in ((gs, grad), (ds, delta), (ms, new_m), (vs, new_v)):
        dst.update(_unpack_small(packed, w))

    outs = [loss, dx[None]]
    for group in (grad, delta, new_m, new_v):
        outs.extend(group[n] for n in WEIGHT_ORDER)
    return tuple(outs)
```

```python
import functools

import jax
import jax.numpy as jnp
from jax import lax
from jax.experimental import pallas as pl
from jax.experimental.pallas import tpu as pltpu

F32 = jnp.float32
BF16 = jnp.bfloat16

D = 1024
CH = 128
NG = 8
HD = 64
NQ = 16
NKV = 4
KVW = NKV * HD
DFF = 4 * D
EPS = 1e-6
IN_W = 5632
SEG = (0, 1024, 2048, 3072, 3328, 3584, 4608, 5632)
ROPE_HALF = 8
Q_SCALE = HD ** -0.5

LR, B1, B2, AEPS, WD, STEP = 0.001, 0.9, 0.999, 1e-08, 0.01, 10

VMEM_LIMIT = 60 * 1024 * 1024
MESH = pl.DeviceIdType.MESH

_GELU_C0 = 0.7978845608028654
_GELU_C1 = 0.044715


def _cparams(sem=None):
    kw = dict(vmem_limit_bytes=VMEM_LIMIT)
    if sem is not None:
        kw["dimension_semantics"] = sem
    return pltpu.CompilerParams(**kw)


def _resident(shape):
    nd = len(shape)
    return pl.BlockSpec(shape, lambda *_: (0,) * nd, pipeline_mode=pl.Buffered(1))


def _const(shape):
    nd = len(shape)
    return pl.BlockSpec(shape, lambda *_: (0,) * nd)


def _rows(tm, w):
    return pl.BlockSpec((tm, w), lambda i: (i, 0))


class _Exchange:
    def __init__(self, ins, outs, aliases, scratch, start, finish):
        self.ins, self.outs, self.aliases, self.scratch = list(ins), list(outs), dict(aliases), list(scratch)
        self.start, self.finish = start, finish


def _both(a, b):
    na, ma, sa = len(a.ins), len(a.outs), len(a.scratch)

    def start(ci, co, cs):
        a.start(ci[:na], co[:ma], cs[:sa])
        b.start(ci[na:], co[ma:], cs[sa:])

    def finish(ci, co, cs):
        a.finish(ci[:na], co[:ma], cs[:sa])
        b.finish(ci[na:], co[ma:], cs[sa:])

    aliases = {**a.aliases, **{na + i: ma + j for i, j in b.aliases.items()}}
    return _Exchange(a.ins + b.ins, a.outs + b.outs, aliases, a.scratch + b.scratch, start, finish)


def _call(body, args, *, name, grid, in_specs, out_specs, out_shape, scratch_shapes=(), sem=None, comm=None):
    single = not isinstance(out_shape, (list, tuple))
    out_shape = [out_shape] if single else list(out_shape)
    out_specs = [out_specs] if single else list(out_specs)
    if comm is None:
        res = pl.pallas_call(body, name=name, grid=grid, in_specs=list(in_specs), out_specs=out_specs,
                             out_shape=out_shape, scratch_shapes=list(scratch_shapes),
                             compiler_params=_cparams(sem))(*args)
        return (res[0] if single else res), []
    n_in, n_out, n_scr = len(args), len(out_shape), len(scratch_shapes)
    nci, nco = len(comm.ins), len(comm.outs)
    steps = 1
    for g in grid:
        steps *= g

    def hosted(*refs):
        a, ci = refs[:n_in], refs[n_in:n_in + nci]
        o, co = refs[n_in + nci:n_in + nci + n_out], refs[n_in + nci + n_out:n_in + nci + n_out + nco]
        rest = refs[n_in + nci + n_out + nco:]
        scr, cs = rest[:n_scr], rest[n_scr:]
        step = pl.program_id(0)
        for d in range(1, len(grid)):
            step = step * grid[d] + pl.program_id(d)

        @pl.when(step == 0)
        def _():
            comm.start(ci, co, cs)

        body(*a, *o, *scr)

        @pl.when(step == steps - 1)
        def _():
            comm.finish(ci, co, cs)

    res = pl.pallas_call(
        hosted, name=name, grid=grid, in_specs=list(in_specs) + [ANY] * nci, out_specs=out_specs + [ANY] * nco,
        out_shape=out_shape + comm.outs, scratch_shapes=list(scratch_shapes) + comm.scratch,
        input_output_aliases={n_in + i: n_out + j for i, j in comm.aliases.items()},
        compiler_params=_cparams(("arbitrary",) * len(grid)),
    )(*args, *comm.ins)
    own = res[:n_out]
    return (own[0] if single else own), list(res[n_out:])


def _run(comm, name):
    nci = len(comm.ins)

    def body(*refs):
        ci, co, cs = refs[:nci], refs[nci:nci + len(comm.outs)], refs[nci + len(comm.outs):]
        comm.start(ci, co, cs)
        comm.finish(ci, co, cs)

    return pl.pallas_call(
        body, name=name, in_specs=[ANY] * nci, out_specs=[ANY] * len(comm.outs), out_shape=comm.outs,
        scratch_shapes=comm.scratch, input_output_aliases=comm.aliases,
        compiler_params=pltpu.CompilerParams(vmem_limit_bytes=VMEM_LIMIT),
    )(*comm.ins)


def _gelu(x):
    x2 = x * x
    t = jnp.tanh(x * (_GELU_C0 + (_GELU_C0 * _GELU_C1) * x2))
    hx = 0.5 * x
    return hx + hx * t, (t, x2, hx)


def _gelu_grad(parts):
    t, x2, hx = parts
    return (0.5 + 0.5 * t) + hx * (1.0 - t * t) * (_GELU_C0 + (3.0 * _GELU_C0 * _GELU_C1) * x2)


def _sigmoid(x):
    return 1.0 / (1.0 + jnp.exp(-x))


def _rms_hat(x):
    r = lax.rsqrt(jnp.mean(x * x, axis=-1, keepdims=True) + EPS)
    return x * r, r


def _rms_bwd(xhat, r, g, dout):
    dg = jnp.sum(dout * xhat, axis=0, keepdims=True)
    dy = dout * g
    dx = r * (dy - xhat * jnp.mean(dy * xhat, axis=-1, keepdims=True))
    return dx, dg


def _dot(a, b):
    return jnp.dot(a, b, preferred_element_type=F32)


def _dot_nt(a, b):
    return lax.dot_general(a, b, (((1,), (1,)), ((), ())), preferred_element_type=F32)


def _dot_tn(a, b):
    return lax.dot_general(a, b, (((0,), (0,)), ((), ())), preferred_element_type=F32)


def _rope(blk, c, s1, s2):
    return blk * c + pltpu.roll(blk, CH - ROPE_HALF, 1) * s1 + pltpu.roll(blk, ROPE_HALF, 1) * s2


def _rope_t(blk, c, s1, s2):
    return blk * c + pltpu.roll(blk * s1, ROPE_HALF, 1) + pltpu.roll(blk * s2, CH - ROPE_HALF, 1)


def _inproj(x, g1, w_in, rc, rs1, rs2, tm, comm=None):
    T = x.shape[0]

    def body(x_ref, g_ref, w_ref, c_ref, s1_ref, s2_ref,
             h_ref, u_ref, v_ref, q_ref, k_ref, va_ref, ga_ref, gb_ref):
        xhat, _ = _rms_hat(x_ref[...])
        h = (xhat * g_ref[...]).astype(BF16)
        h_ref[...] = h
        uv = _dot(h, w_ref[:, SEG[0]:SEG[2]])
        u_ref[...] = uv[:, :D]
        v_ref[...] = uv[:, D:]
        c, s1, s2 = c_ref[...], s1_ref[...], s2_ref[...]
        qkv = _dot(h, w_ref[:, SEG[2]:SEG[5]])
        for p in range(D // CH):
            blk = _rope(qkv[:, CH * p:CH * (p + 1)], c, s1, s2) * Q_SCALE
            q_ref[:, CH * p:CH * (p + 1)] = blk.astype(BF16)
        for p in range(KVW // CH):
            k_ref[:, CH * p:CH * (p + 1)] = _rope(qkv[:, D + CH * p:D + CH * (p + 1)], c, s1, s2).astype(BF16)
        va_ref[...] = qkv[:, D + KVW:].astype(BF16)
        gates = _dot(h, w_ref[:, SEG[5]:SEG[7]]).astype(BF16)
        ga_ref[...] = gates[:, :D]
        gb_ref[...] = gates[:, D:]

    sd = jax.ShapeDtypeStruct
    return _call(
        body, (x, g1, w_in, rc, rs1, rs2), name="inproj_fwd", grid=(T // tm,),
        in_specs=[_rows(tm, D), _const((1, D)), _resident((D, IN_W)), _rows(tm, CH), _rows(tm, CH), _rows(tm, CH)],
        out_specs=[_rows(tm, D), _rows(tm, D), _rows(tm, D), _rows(tm, D), _rows(tm, KVW), _rows(tm, KVW),
                   _rows(tm, D), _rows(tm, D)],
        out_shape=[sd((T, D), BF16), sd((T, D), F32), sd((T, D), F32), sd((T, D), BF16), sd((T, KVW), BF16),
                   sd((T, KVW), BF16), sd((T, D), BF16), sd((T, D), BF16)],
        sem=("parallel",), comm=comm)


def _sgu_common(u, vs, lng, lnb, ws_ref, bfull):
    nc = u.shape[0] // CH
    ug, tu = _gelu(u)
    vg, tv = _gelu(vs)
    mu = jnp.mean(vg, axis=-1, keepdims=True)
    xc = vg - mu
    rstd = lax.rsqrt(jnp.mean(xc * xc, axis=-1, keepdims=True) + EPS)
    vhat = xc * rstd
    vnb = (vhat * lng + lnb).astype(BF16)
    tri = lax.broadcasted_iota(jnp.int32, (CH, CH), 0) >= lax.broadcasted_iota(jnp.int32, (CH, CH), 1)
    wts, rhss, mixed = [], [], []
    for g in range(NG):
        wt = jnp.where(tri, ws_ref[g], 0.0).astype(BF16)
        rhs = jnp.concatenate([vnb[CH * c:CH * (c + 1), CH * g:CH * (g + 1)] for c in range(nc)], axis=1)
        mix = _dot(wt, rhs)
        wts.append(wt)
        rhss.append(rhs)
        mixed.append([mix[:, CH * c:CH * (c + 1)] + bfull[:, CH * g:CH * (g + 1)] for c in range(nc)])
    return nc, ug, tu, tv, rstd, vhat, tri, wts, rhss, mixed


def _sgu_fwd(u, vs, lng, lnb, ws, bfull, tm, comm=None):
    T = u.shape[0]

    def body(u_ref, v_ref, lng_ref, lnb_ref, ws_ref, bf_ref, a_ref):
        nc, ug, _, _, _, _, _, _, _, mixed = _sgu_common(
            u_ref[...], v_ref[...], lng_ref[...], lnb_ref[...], ws_ref, bf_ref[...])
        mixed_all = jnp.concatenate(
            [jnp.concatenate([mixed[g][c] for g in range(NG)], axis=1) for c in range(nc)], axis=0)
        a_ref[...] = (ug * mixed_all).astype(BF16)

    return _call(
        body, (u, vs, lng, lnb, ws, bfull), name="sgu_fwd", grid=(T // tm,),
        in_specs=[_rows(tm, D), _rows(tm, D), _const((1, D)), _const((1, D)), _const((NG, CH, CH)), _const((CH, D))],
        out_specs=_rows(tm, D), out_shape=jax.ShapeDtypeStruct((T, D), BF16), sem=("parallel",), comm=comm)


def _sgu_bwd(u, vs, da, lng, lnb, ws, bfull, tm, comm=None):
    T = u.shape[0]
    nsteps = T // tm

    def body(u_ref, v_ref, da_ref, lng_ref, lnb_ref, ws_ref, bf_ref,
             du_ref, dv_ref, dws_ref, dbs_ref, dlg_ref, dlb_ref, db_ref):
        i = pl.program_id(0)
        u, vs, da, lng = u_ref[...], v_ref[...], da_ref[...], lng_ref[...]
        nc, ug, tu, tv, rstd, vhat, tri, wts, rhss, mixed = _sgu_common(u, vs, lng, lnb_ref[...], ws_ref, bf_ref[...])

        @pl.when(i == 0)
        def _():
            dws_ref[...] = jnp.zeros_like(dws_ref)
            db_ref[...] = jnp.zeros_like(db_ref)
            dlg_ref[...] = jnp.zeros_like(dlg_ref)
            dlb_ref[...] = jnp.zeros_like(dlb_ref)

        mixed_all = jnp.concatenate(
            [jnp.concatenate([mixed[g][c] for g in range(NG)], axis=1) for c in range(nc)], axis=0)
        du_ref[...] = (da * mixed_all * _gelu_grad(tu)).astype(BF16)
        dmixed = da * ug
        dvn_cols = []
        for g in range(NG):
            dmix = [dmixed[CH * c:CH * (c + 1), CH * g:CH * (g + 1)] for c in range(nc)]
            db_ref[:, CH * g:CH * (g + 1)] += functools.reduce(lambda a, b: a + b, dmix)
            dm = jnp.concatenate(dmix, axis=1).astype(BF16)
            dws_ref[g] += _dot_nt(dm, rhss[g])
            dvn_cols.append(_dot_tn(wts[g], dm))
        dvn = jnp.concatenate(
            [jnp.concatenate([dvn_cols[g][:, CH * c:CH * (c + 1)] for g in range(NG)], axis=1) for c in range(nc)],
            axis=0)
        dlg_ref[...] += jnp.sum(dvn * vhat, axis=0, keepdims=True)
        dlb_ref[...] += jnp.sum(dvn, axis=0, keepdims=True)
        dvh = dvn * lng
        dvg = rstd * (dvh - jnp.mean(dvh, axis=-1, keepdims=True)
                      - vhat * jnp.mean(dvh * vhat, axis=-1, keepdims=True))
        dv_ref[...] = (dvg * _gelu_grad(tv)).astype(BF16)

        @pl.when(i == nsteps - 1)
        def _():
            for g in range(NG):
                dws_ref[g] = jnp.where(tri, dws_ref[g], 0.0)
                dbs_ref[g:g + 1, :] = jnp.sum(db_ref[:, CH * g:CH * (g + 1)].T, axis=0, keepdims=True)

    sd = jax.ShapeDtypeStruct
    return _call(
        body, (u, vs, da, lng, lnb, ws, bfull), name="sgu_bwd", grid=(nsteps,),
        in_specs=[_rows(tm, D), _rows(tm, D), _rows(tm, D), _const((1, D)), _const((1, D)), _const((NG, CH, CH)),
                  _const((CH, D))],
        out_specs=[_rows(tm, D), _rows(tm, D), _const((NG, CH, CH)), _const((NG, CH)), _const((1, D)), _const((1, D))],
        out_shape=[sd((T, D), BF16), sd((T, D), BF16), sd((NG, CH, CH), F32), sd((NG, CH), F32), sd((1, D), F32),
                   sd((1, D), F32)],
        scratch_shapes=[pltpu.VMEM((CH, D), F32)], sem=("arbitrary",), comm=comm)


def _pair_layout(prev, cur, grp):
    j, half = grp // 2, grp % 2
    blk = jnp.concatenate([prev[:, CH * j:CH * (j + 1)], cur[:, CH * j:CH * (j + 1)]], axis=0).astype(F32)
    lo = lax.broadcasted_iota(jnp.int32, blk.shape, 1) < HD
    rolled = pltpu.roll(blk, HD, 1)
    even = jnp.where(lo, blk if half == 0 else rolled, 0.0)
    odd = jnp.where(lo, 0.0, rolled if half == 0 else blk)
    return jnp.concatenate([even, odd], axis=0).astype(BF16)


def _attn_mask(n):
    qi = lax.broadcasted_iota(jnp.int32, (CH, 2 * CH), 0)
    kc = lax.broadcasted_iota(jnp.int32, (CH, 2 * CH), 1)
    ok = (kc > qi) & (kc <= qi + CH) & ((kc >= CH) | (n > 0))
    return jnp.concatenate([ok, ok], axis=1)


def _softmax_sink(s, sink):
    m = jnp.maximum(jnp.max(s, axis=-1, keepdims=True), sink)
    p = jnp.exp(s - m)
    ps = jnp.exp(sink - m)
    inv = 1.0 / (jnp.sum(p, axis=-1, keepdims=True) + ps)
    return p * inv, ps * inv


def _attn_fwd(q, k, va, sinks, comm=None):
    T = q.shape[0]
    nb = T // CH

    def body(sk_ref, q_ref, kp_ref, kc_ref, vp_ref, vc_ref, o_ref):
        n = pl.program_id(0)
        mask = _attn_mask(n)
        kp, kc, vp, vc = kp_ref[...], kc_ref[...], vp_ref[...], vc_ref[...]
        kks = [_pair_layout(kp, kc, grp) for grp in range(NKV)]
        vvs = [_pair_layout(vp, vc, grp) for grp in range(NKV)]
        npairs = D // CH

        def scores(p):
            return _dot_nt(q_ref[:, CH * p:CH * (p + 1)], kks[p // 2])

        ahead = 3
        outs, probs = [], []
        pending = [scores(p) for p in range(ahead)]
        even_lanes = lax.broadcasted_iota(jnp.int32, (CH, CH), 1) < HD

        def unnormalised(s, sink):
            m = jnp.maximum(jnp.max(s, axis=-1, keepdims=True), sink)
            p = jnp.exp(s - m)
            return p, 1.0 / (jnp.sum(p, axis=-1, keepdims=True) + jnp.exp(sink - m))

        def value_product(p):
            pr, ie, io = probs[p]
            return _dot(pr, vvs[p // 2]) * jnp.where(even_lanes, ie, io)

        for p in range(npairs):
            s = jnp.where(mask, pending.pop(0), -1e30)
            if p + ahead < npairs:
                pending.append(scores(p + ahead))
            pe, ie = unnormalised(s[:, :2 * CH], sk_ref[2 * p])
            po, io = unnormalised(s[:, 2 * CH:], sk_ref[2 * p + 1])
            probs.append((jnp.concatenate([pe, po], axis=1).astype(BF16), ie, io))
            if p >= 1:
                outs.append(value_product(p - 1))
        outs.append(value_product(npairs - 1))
        o_ref[...] = jnp.concatenate(outs, axis=1).astype(BF16)

    prev = lambda n: (jnp.maximum(n - 1, 0), 0)
    cur = lambda n: (n, 0)
    return _call(
        body, (sinks, q, k, k, va, va), name="attn_fwd", grid=(nb,),
        in_specs=[pl.BlockSpec(memory_space=pltpu.SMEM), pl.BlockSpec((CH, D), cur),
                  pl.BlockSpec((CH, KVW), prev), pl.BlockSpec((CH, KVW), cur),
                  pl.BlockSpec((CH, KVW), prev), pl.BlockSpec((CH, KVW), cur)],
        out_specs=pl.BlockSpec((CH, D), cur), out_shape=jax.ShapeDtypeStruct((T, D), BF16),
        sem=("parallel",), comm=comm)


def _attn_bwd(q, k, va, datt, sinks, rc, rs1, rs2, comm=None):
    T = q.shape[0]
    nb = T // CH

    def body(sk_ref, q_ref, kp_ref, kc_ref, vp_ref, vc_ref, do_ref, cq_ref, s1q_ref, s2q_ref, ck_ref, s1k_ref, s2k_ref,
             dq_ref, dk_ref, dv_ref, dsk_ref, kcar, vcar):
        n = pl.program_id(0)

        @pl.when(n == 0)
        def _():
            kcar[...] = jnp.zeros_like(kcar)
            vcar[...] = jnp.zeros_like(vcar)
            dsk_ref[...] = jnp.zeros_like(dsk_ref)

        def flush(kprev, vprev):
            ck, s1k, s2k = ck_ref[...], s1k_ref[...], s2k_ref[...]
            for j in range(KVW // CH):
                sl = slice(CH * j, CH * (j + 1))
                dk_ref[:, sl] = _rope_t(kcar[:, sl] + kprev[:, sl], ck, s1k, s2k).astype(BF16)
                dv_ref[:, sl] = (vcar[:, sl] + vprev[:, sl]).astype(BF16)

        @pl.when(n < nb)
        def _():
            mask = _attn_mask(n)
            kp, kc, vp, vc = kp_ref[...], kc_ref[...], vp_ref[...], vc_ref[...]
            cq, s1q, s2q = cq_ref[...], s1q_ref[...], s2q_ref[...]
            lane = lax.broadcasted_iota(jnp.int32, (1, CH), 1)
            dsk = jnp.zeros((1, CH), F32)
            npairs = D // CH
            kks = [_pair_layout(kp, kc, grp) for grp in range(NKV)]
            vvs = [_pair_layout(vp, vc, grp) for grp in range(NKV)]
            qs = [q_ref[:, CH * p:CH * (p + 1)] for p in range(npairs)]
            dos = [do_ref[:, CH * p:CH * (p + 1)].astype(BF16) for p in range(npairs)]

            def first(p):
                return _dot_nt(qs[p], kks[p // 2]), _dot_nt(dos[p], vvs[p // 2])

            def last(p, ds, pb):
                return (_rope_t(_dot(ds, kks[p // 2]), cq, s1q, s2q) * Q_SCALE, _dot_tn(qs[p], ds), _dot_tn(dos[p], pb))

            ahead = 2
            pending = [first(p) for p in range(ahead)]
            mids, ends = [], []
            for p in range(npairs):
                s, dp = pending.pop(0)
                s = jnp.where(mask, s, -1e30)
                if p + ahead < npairs:
                    pending.append(first(p + ahead))
                ds_parts, p_parts = [], []
                for par in range(2):
                    sl = slice(2 * CH * par, 2 * CH * (par + 1))
                    pr, psink = _softmax_sink(s[:, sl], sk_ref[2 * p + par])
                    delta = jnp.sum(pr * dp[:, sl], axis=-1, keepdims=True)
                    ds_parts.append(pr * (dp[:, sl] - delta))
                    p_parts.append(pr)
                    tot = -jnp.sum(psink * delta, axis=0, keepdims=True)
                    dsk = dsk + jnp.where(lane == 2 * p + par, tot, 0.0)
                mids.append((jnp.concatenate(ds_parts, axis=1).astype(BF16), jnp.concatenate(p_parts, axis=1).astype(BF16)))
                if p >= 1:
                    ends.append(last(p - 1, *mids[p - 1]))
            ends.append(last(npairs - 1, *mids[-1]))
            dq_cols = [e[0] for e in ends]
            def fold(i):
                rows = []
                for grp in range(NKV):
                    acc = ends[2 * grp][i] + ends[2 * grp + 1][i]
                    rows.append(acc[:HD, :2 * CH] + acc[HD:, 2 * CH:])
                return jnp.concatenate(rows, axis=0).T

            dkf, dvf = fold(1), fold(2)
            dq_ref[...] = jnp.concatenate(dq_cols, axis=1).astype(BF16)
            dsk_ref[...] += dsk
            flush(dkf[:CH], dvf[:CH])
            kcar[...] = dkf[CH:]
            vcar[...] = dvf[CH:]

        @pl.when(n == nb)
        def _():
            z = jnp.zeros((CH, KVW), F32)
            flush(z, z)

    last = nb - 1
    cur = lambda n: (jnp.minimum(n, last), 0)
    prev = lambda n: (jnp.clip(n - 1, 0, last), 0)
    sd = jax.ShapeDtypeStruct
    return _call(
        body, (sinks, q, k, k, va, va, datt, rc, rs1, rs2, rc, rs1, rs2), name="attn_bwd", grid=(nb + 1,),
        in_specs=[pl.BlockSpec(memory_space=pltpu.SMEM), pl.BlockSpec((CH, D), cur),
                  pl.BlockSpec((CH, KVW), prev), pl.BlockSpec((CH, KVW), cur),
                  pl.BlockSpec((CH, KVW), prev), pl.BlockSpec((CH, KVW), cur),
                  pl.BlockSpec((CH, D), cur),
                  pl.BlockSpec((CH, CH), cur), pl.BlockSpec((CH, CH), cur), pl.BlockSpec((CH, CH), cur),
                  pl.BlockSpec((CH, CH), prev), pl.BlockSpec((CH, CH), prev), pl.BlockSpec((CH, CH), prev)],
        out_specs=[pl.BlockSpec((CH, D), cur), pl.BlockSpec((CH, KVW), prev), pl.BlockSpec((CH, KVW), prev),
                   _const((1, CH))],
        out_shape=[sd((T, D), BF16), sd((T, KVW), BF16), sd((T, KVW), BF16), sd((1, CH), F32)],
        scratch_shapes=[pltpu.VMEM((CH, KVW), F32), pltpu.VMEM((CH, KVW), F32)], sem=("arbitrary",), comm=comm)


def _merge_fwd(a, att, ga, gb, x, w_a, w_b, w_o, g2, tm):
    T = x.shape[0]

    def body(a_ref, att_ref, ga_ref, gb_ref, x_ref, wa_ref, wb_ref, wo_ref, g_ref,
             pa_ref, pb_ref, mg_ref, mix_ref, x1_ref):
        pa = _dot(a_ref[...], wa_ref[...])
        pb = _dot(att_ref[...], wb_ref[...])
        pa_ref[...] = pa.astype(BF16)
        pb_ref[...] = pb.astype(BF16)
        merged = (_sigmoid(ga_ref[...].astype(F32)) * pa + _sigmoid(gb_ref[...].astype(F32)) * pb).astype(BF16)
        mg_ref[...] = merged
        mix = _dot(merged, wo_ref[...])
        mix_ref[...] = mix
        mhat, _ = _rms_hat(mix)
        x1_ref[...] = x_ref[...] + mhat * g_ref[...]

    sd = jax.ShapeDtypeStruct
    return pl.pallas_call(
        body, name="merge_fwd", grid=(T // tm,),
        in_specs=[_rows(tm, D)] * 5 + [_resident((D, D))] * 3 + [_const((1, D))],
        out_specs=[_rows(tm, D)] * 5,
        out_shape=[sd((T, D), BF16), sd((T, D), BF16), sd((T, D), BF16), sd((T, D), F32), sd((T, D), F32)],
        compiler_params=_cparams(("parallel",)),
    )(a, att, ga, gb, x, w_a, w_b, w_o, g2)


def _merge_bwd(dx1, mix, ga, gb, pa, pb, a, att, merged, w_a, w_b, w_o, g2, tm, comm=None):
    T = dx1.shape[0]
    nsteps = T // tm

    def body(dx1_ref, mix_ref, ga_ref, gb_ref, pa_ref, pb_ref, a_ref, att_ref, mg_ref, wa_ref, wb_ref, wo_ref, g_ref,
             dga_ref, dgb_ref, da_ref, datt_ref, dg_ref, dwa_ref, dwb_ref, dwo_ref, acc, sem):
        i = pl.program_id(0)

        @pl.when(i == 0)
        def _():
            dg_ref[...] = jnp.zeros_like(dg_ref)
            acc[...] = jnp.zeros_like(acc)

        mhat, r = _rms_hat(mix_ref[...])
        dmix, dg = _rms_bwd(mhat, r, g_ref[...], dx1_ref[...])
        dg_ref[...] += dg
        dmix = dmix.astype(BF16)
        dmerged = _dot_nt(dmix, wo_ref[...])
        sa = _sigmoid(ga_ref[...].astype(F32))
        sb = _sigmoid(gb_ref[...].astype(F32))
        dao = (dmerged * sa).astype(BF16)
        dbo = (dmerged * sb).astype(BF16)
        dga_ref[...] = (dmerged * pa_ref[...].astype(F32) * (sa * (1.0 - sa))).astype(BF16)
        dgb_ref[...] = (dmerged * pb_ref[...].astype(F32) * (sb * (1.0 - sb))).astype(BF16)
        da_ref[...] = _dot_nt(dao, wa_ref[...])
        datt_ref[...] = _dot_nt(dbo, wb_ref[...]).astype(BF16)
        acc[0] += _dot_tn(a_ref[...], dao)
        acc[1] += _dot_tn(att_ref[...], dbo)
        acc[2] += _dot_tn(mg_ref[...], dmix)

        @pl.when(i == nsteps - 1)
        def _():
            outs = [pltpu.make_async_copy(acc.at[j], ref, sem.at[j]) for j, ref in enumerate((dwa_ref, dwb_ref, dwo_ref))]
            for cp in outs:
                cp.start()
            for cp in outs:
                cp.wait()

    sd = jax.ShapeDtypeStruct
    return _call(
        body, (dx1, mix, ga, gb, pa, pb, a, att, merged, w_a, w_b, w_o, g2), name="merge_bwd", grid=(nsteps,),
        in_specs=[_rows(tm, D)] * 9 + [_resident((D, D))] * 3 + [_const((1, D))],
        out_specs=[_rows(tm, D)] * 4 + [_const((1, D))] + [ANY] * 3,
        out_shape=[sd((T, D), BF16), sd((T, D), BF16), sd((T, D), F32), sd((T, D), BF16), sd((1, D), F32)]
        + [sd((D, D), F32)] * 3,
        scratch_shapes=[pltpu.VMEM((3, D, D), F32), _dma_sems(3)], sem=("arbitrary",), comm=comm)


def _ffn(x1, target, w1, w2, g3, g4, tm):
    T = x1.shape[0]

    def body(x_ref, t_ref, w1_ref, w2_ref, g3_ref, g4_ref,
             hf_ref, f2_ref, dff_ref, df1_ref, dx_ref, ls_ref, dg3_ref, dg4_ref):
        @pl.when(pl.program_id(0) == 0)
        def _():
            ls_ref[...] = jnp.zeros_like(ls_ref)
            dg3_ref[...] = jnp.zeros_like(dg3_ref)
            dg4_ref[...] = jnp.zeros_like(dg4_ref)

        x = x_ref[...]
        g3, g4 = g3_ref[...], g4_ref[...]
        xhat, r3 = _rms_hat(x)
        hf = (xhat * g3).astype(BF16)
        hf_ref[...] = hf
        rl = jnp.maximum(_dot(hf, w1_ref[...]), 0.0)
        f2 = (rl * rl).astype(BF16)
        f2_ref[...] = f2
        fhat, r4 = _rms_hat(_dot(f2, w2_ref[...]))
        err = x + fhat * g4 - t_ref[...]
        ls_ref[...] += jnp.sum(err * err, axis=0, keepdims=True)
        dy = err * (1.0 / D)
        dff, dg4 = _rms_bwd(fhat, r4, g4, dy)
        dg4_ref[...] += dg4
        dff = dff.astype(BF16)
        dff_ref[...] = dff
        df1 = (_dot_nt(dff, w2_ref[...]) * (2.0 * rl)).astype(BF16)
        df1_ref[...] = df1
        dxn, dg3 = _rms_bwd(xhat, r3, g3, _dot_nt(df1, w1_ref[...]))
        dg3_ref[...] += dg3
        dx_ref[...] = dy + dxn

    sd = jax.ShapeDtypeStruct
    return pl.pallas_call(
        body, name="ffn_fwd_bwd", grid=(T // tm,),
        in_specs=[_rows(tm, D), _rows(tm, D), _resident((D, DFF)), _resident((DFF, D)), _const((1, D)), _const((1, D))],
        out_specs=[_rows(tm, D), _rows(tm, DFF), _rows(tm, D), _rows(tm, DFF), _rows(tm, D), _const((1, D)),
                   _const((1, D)), _const((1, D))],
        out_shape=[sd((T, D), BF16), sd((T, DFF), BF16), sd((T, D), BF16), sd((T, DFF), BF16), sd((T, D), F32),
                   sd((1, D), F32), sd((1, D), F32), sd((1, D), F32)],
        compiler_params=_cparams(("arbitrary",)),
    )(x1, target, w1, w2, g3, g4)


def _inproj_bwd(parts, x, dx1, g1, w_in, tm, comm=None):
    T = x.shape[0]
    widths = [p.shape[1] for p in parts]
    offs = [sum(widths[:i]) for i in range(len(widths) + 1)]
    assert offs[-1] == IN_W

    def body(*refs):
        n = len(parts)
        prefs = refs[:n]
        x_ref, dx1_ref, g_ref, w_ref, dx_ref, dp_ref, dg_ref = refs[n:]

        @pl.when(pl.program_id(0) == 0)
        def _():
            dg_ref[...] = jnp.zeros_like(dg_ref)

        for i in range(n):
            dp_ref[:, offs[i]:offs[i + 1]] = prefs[i][...]
        dh = _dot_nt(dp_ref[...], w_ref[...])
        xhat, r = _rms_hat(x_ref[...])
        dxn, dg = _rms_bwd(xhat, r, g_ref[...], dh)
        dg_ref[...] += dg
        dx_ref[...] = dx1_ref[...] + dxn

    sd = jax.ShapeDtypeStruct
    return _call(
        body, (*parts, x, dx1, g1, w_in), name="inproj_bwd", grid=(T // tm,),
        in_specs=[_rows(tm, w) for w in widths] + [_rows(tm, D), _rows(tm, D), _const((1, D)), _resident((D, IN_W))],
        out_specs=[_rows(tm, D), _rows(tm, IN_W), _const((1, D))],
        out_shape=[sd((T, D), F32), sd((T, IN_W), BF16), sd((1, D), F32)], sem=("arbitrary",), comm=comm)


def _wgrad(a, g, tn, tm, name, comm=None):
    T, K = a.shape
    N = g.shape[1]

    def body(a_ref, g_ref, o_ref):
        @pl.when(pl.program_id(1) == 0)
        def _():
            o_ref[...] = jnp.zeros_like(o_ref)

        o_ref[...] += _dot_tn(a_ref[...], g_ref[...])

    return _call(
        body, (a, g), name=name, grid=(N // tn, T // tm),
        in_specs=[pl.BlockSpec((tm, K), lambda j, t: (t, 0)), pl.BlockSpec((tm, tn), lambda j, t: (t, j))],
        out_specs=pl.BlockSpec((K, tn), lambda j, t: (0, j)),
        out_shape=jax.ShapeDtypeStruct((K, N), F32), sem=("parallel", "arbitrary"), comm=comm)


def _adamw(ws, gs, ms, vs, trs, name):
    n = len(ws)
    walk = _Walk(w.shape[0] // tr for w, tr in zip(ws, trs))
    bc1 = 1.0 / (1.0 - B1 ** STEP)
    bc2 = 1.0 / (1.0 - B2 ** STEP)

    def body(*refs):
        i = pl.program_id(0)
        for k in range(n):
            mine = tuple(refs[j * n + k] for j in range(8))

            @pl.when(walk.mine(k, i))
            def _(mine=mine):
                w_ref, g_ref, m_ref, v_ref, go_ref, d_ref, nm_ref, nv_ref = mine
                g = g_ref[...]
                go_ref[...] = g
                m = B1 * m_ref[...] + (1.0 - B1) * g
                v = B2 * v_ref[...] + (1.0 - B2) * (g * g)
                nm_ref[...] = m
                nv_ref[...] = v
                d_ref[...] = -LR * ((m * bc1) / (jnp.sqrt(v * bc2) + AEPS) + WD * w_ref[...])

    def spec(k):
        return pl.BlockSpec((trs[k], ws[k].shape[1]), lambda i: (walk.tile(k, i), 0))

    specs = [spec(k) for k in range(n)]
    res = pl.pallas_call(
        body, name=name, grid=(walk.steps,), in_specs=specs * 4, out_specs=specs * 4,
        out_shape=[jax.ShapeDtypeStruct(w.shape, F32) for w in ws] * 4,
        compiler_params=_cparams(("arbitrary",)),
    )(*ws, *gs, *ms, *vs)
    return [tuple(res[j * n + k] for j in range(4)) for k in range(n)]


BIG = (("col", (D, IN_W)), ("row", (D, D)), ("row", (D, D)), ("row", (D, D)), ("col", (D, DFF)), ("row", (DFF, D)))
NBIG = len(BIG)
ANY = pl.BlockSpec(memory_space=pl.ANY)


def _shard_shape(kind, shape):
    R, C = shape
    return (R, C // 4) if kind == "col" else (R // 4, C)


def _half_shape(kind, shape):
    R, C = shape
    return (R // 2, C) if kind == "col" else (R, C // 2)


def _piece_shape(kind, shape):
    R, C = shape
    return (R // 2, C // 4) if kind == "col" else (R // 4, C // 2)


def _own_region(ref, kind, shape, s):
    R, C = shape
    return ref.at[:, pl.ds(s * (C // 4), C // 4)] if kind == "col" else ref.at[pl.ds(s * (R // 4), R // 4), :]


def _ag_region(ref, kind, shape, s, hc):
    R, C = shape
    if kind == "col":
        return ref.at[pl.ds(hc * (R // 2), R // 2), pl.ds(s * (C // 4), C // 4)]
    return ref.at[pl.ds(s * (R // 4) + hc * (R // 8), R // 8), :]


def _ag_shard_half(ref, kind, shape, hc):
    R, C = shape
    return ref.at[pl.ds(hc * (R // 2), R // 2), :] if kind == "col" else ref.at[pl.ds(hc * (R // 8), R // 8), :]


def _grad_half(ref, kind, shape, hc):
    R, C = shape
    return ref.at[pl.ds(hc * (R // 2), R // 2), :] if kind == "col" else ref.at[:, pl.ds(hc * (C // 2), C // 2)]


def _half_piece(ref, kind, shape, s):
    R, C = shape
    return ref.at[:, pl.ds(s * (C // 4), C // 4)] if kind == "col" else ref.at[pl.ds(s * (R // 4), R // 4), :]


def _place():
    x, y, c = lax.axis_index("x"), lax.axis_index("y"), lax.axis_index("c")
    chips = [(1 - x, y), (x, 1 - y), (1 - x, 1 - y)]
    return x, y, c, chips


def _rcopy(src, dst, ssem, rsem, dev):
    return pltpu.make_async_remote_copy(src_ref=src, dst_ref=dst, send_sem=ssem, recv_sem=rsem,
                                        device_id=dev, device_id_type=MESH)


def _dma_sems(n):
    return pltpu.SemaphoreType.DMA((n,))


def _x_gather_ici(shards, ws):
    n = len(ws)
    specs = [BIG[w] for w in ws]

    def place():
        x, y, c, chips = _place()
        return c, chips, 2 * x + y

    def sends(sh, full, sc):
        c, chips, me_s = place()
        return [_rcopy(_ag_shard_half(sh[i], kind, shape, c), _ag_region(full[i], kind, shape, me_s, c),
                       sc[0].at[3 * i + j], sc[1].at[3 * i + j], (cx, cy, c))
                for i, (kind, shape) in enumerate(specs) for j, (cx, cy) in enumerate(chips)]

    def start(sh, full, sc):
        for i in range(n):
            pltpu.make_async_copy(sh[i], sc[4 + i], sc[2].at[i]).start()
        for cp in sends(sh, full, sc):
            cp.start()

    def finish(sh, full, sc):
        c, chips, me_s = place()
        stores = []
        for i, (kind, shape) in enumerate(specs):
            pltpu.make_async_copy(sh[i], sc[4 + i], sc[2].at[i]).wait()
            st = pltpu.make_async_copy(sc[4 + i], _own_region(full[i], kind, shape, me_s), sc[3].at[i])
            st.start()
            stores.append(st)
        for i, (kind, shape) in enumerate(specs):
            for j, (cx, cy) in enumerate(chips):
                reg = _ag_region(full[i], kind, shape, 2 * cx + cy, c)
                _rcopy(reg, reg, sc[0].at[3 * i + j], sc[1].at[3 * i + j], (cx, cy, c)).wait_recv()
        for cp in sends(sh, full, sc):
            cp.wait_send()
        for st in stores:
            st.wait()

    return _Exchange(
        shards, [jax.ShapeDtypeStruct(shape, BF16) for _, shape in specs], {},
        [_dma_sems(3 * n), _dma_sems(3 * n), _dma_sems(n), _dma_sems(n)]
        + [pltpu.VMEM(_shard_shape(k, s), BF16) for k, s in specs], start, finish)


def _x_gather_d2d(wholes, ws):
    specs = [BIG[w] for w in ws]
    n = len(ws)

    def copies(full, sc, mine):
        x, y, c, chips = _place()
        hc = c if mine else 1 - c
        return [_rcopy(reg, reg, sc[0].at[3 * i + j], sc[1].at[3 * i + j], (x, y, 1 - c))
                for i, (kind, shape) in enumerate(specs) for j, (cx, cy) in enumerate(chips)
                for reg in [_ag_region(full[i], kind, shape, 2 * cx + cy, hc)]]

    def start(_, full, sc):
        for cp in copies(full, sc, True):
            cp.start()

    def finish(_, full, sc):
        for cp in copies(full, sc, False):
            cp.wait_recv()
        for cp in copies(full, sc, True):
            cp.wait_send()

    return _Exchange(wholes, [jax.ShapeDtypeStruct(shape, BF16) for _, shape in specs], {i: i for i in range(n)},
                     [_dma_sems(3 * n), _dma_sems(3 * n)], start, finish)


def _x_grads_sibling(grads, ws):
    specs = [BIG[w] for w in ws]
    n = len(ws)

    def copies(g, got, sc):
        x, y, c, _ = _place()
        return [_rcopy(_grad_half(g[i], kind, shape, 1 - c), got[i], sc[0].at[i], sc[1].at[i], (x, y, 1 - c))
                for i, (kind, shape) in enumerate(specs)]

    def start(g, got, sc):
        for cp in copies(g, got, sc):
            cp.start()

    def finish(g, got, sc):
        for cp in copies(g, got, sc):
            cp.wait_recv()
        for cp in copies(g, got, sc):
            cp.wait_send()

    return _Exchange(grads, [jax.ShapeDtypeStruct(_half_shape(k, s), F32) for k, s in specs], {},
                     [_dma_sems(n), _dma_sems(n)], start, finish)


def _x_grads_chips(sums_bf, ws):
    specs = [BIG[w] for w in ws]
    n = len(ws)

    def copies(s16, got, sc):
        x, y, c, chips = _place()
        return [_rcopy(_half_piece(s16[i], kind, shape, 2 * cx + cy), got[i].at[j],
                       sc[0].at[3 * i + j], sc[1].at[3 * i + j], (cx, cy, c))
                for i, (kind, shape) in enumerate(specs) for j, (cx, cy) in enumerate(chips)]

    def start(s16, got, sc):
        for cp in copies(s16, got, sc):
            cp.start()

    def finish(s16, got, sc):
        for cp in copies(s16, got, sc):
            cp.wait_recv()
        for cp in copies(s16, got, sc):
            cp.wait_send()

    return _Exchange(sums_bf, [jax.ShapeDtypeStruct((3,) + _piece_shape(k, s), BF16) for k, s in specs], {},
                     [_dma_sems(3 * n), _dma_sems(3 * n)], start, finish)


def _shard_half(ref, kind, shape, hc):
    sr, sc = _shard_shape(kind, shape)
    return ref.at[pl.ds(hc * (sr // 2), sr // 2), :] if kind == "col" else ref.at[:, pl.ds(hc * (sc // 2), sc // 2)]


def _x_grads_share(shard_grads, ws):
    specs = [BIG[w] for w in ws]
    n = len(ws)

    def copies(g, sc, mine):
        x, y, c, _ = _place()
        hc = c if mine else 1 - c
        return [_rcopy(part, part, sc[0].at[i], sc[1].at[i], (x, y, 1 - c))
                for i, (kind, shape) in enumerate(specs) for part in [_shard_half(g[i], kind, shape, hc)]]

    def start(_, g, sc):
        for cp in copies(g, sc, True):
            cp.start()

    def finish(_, g, sc):
        for cp in copies(g, sc, False):
            cp.wait_recv()
        for cp in copies(g, sc, True):
            cp.wait_send()

    return _Exchange(shard_grads, [jax.ShapeDtypeStruct(_shard_shape(k, s), F32) for k, s in specs],
                     {i: i for i in range(n)}, [_dma_sems(n), _dma_sems(n)], start, finish)


ADD_BLOCK_BYTES = 4 * 1024 * 1024


def _add_rows(rows, cols, n_arrays):
    limit = ADD_BLOCK_BYTES // (1 if n_arrays == 1 else 4)
    r = rows
    while r > 64 and r * cols * 4 > limit:
        r //= 2
    return r


class _Walk:
    def __init__(self, tiles):
        self.tiles = list(tiles)
        self.starts = [sum(self.tiles[:k]) for k in range(len(self.tiles))]
        self.steps = sum(self.tiles)

    def tile(self, k, i):
        return jnp.clip(i - self.starts[k], 0, self.tiles[k] - 1)

    def mine(self, k, i):
        return (i >= self.starts[k]) & (i < self.starts[k] + self.tiles[k])


def _add_halves(place, gs, gots, kinds, name):
    n = len(gs)
    halves = [_half_shape(kind, g.shape) for g, kind in zip(gs, kinds)]
    rows = [_add_rows(hr, hc, n) for hr, hc in halves]
    walk = _Walk(hr // r for (hr, _), r in zip(halves, rows))

    def body(p_ref, *refs):
        i = pl.program_id(0)
        for k in range(n):
            g_ref, b_ref, s_ref, sb_ref = (refs[j * n + k] for j in range(4))

            @pl.when(walk.mine(k, i))
            def _(g_ref=g_ref, b_ref=b_ref, s_ref=s_ref, sb_ref=sb_ref):
                s = g_ref[...] + b_ref[...]
                s_ref[...] = s
                sb_ref[...] = s.astype(BF16)

    def g_spec(k):
        if kinds[k] == "col":
            return pl.BlockSpec((rows[k], gs[k].shape[1]), lambda i, p: (p[0] * walk.tiles[k] + walk.tile(k, i), 0))
        return pl.BlockSpec((rows[k], halves[k][1]), lambda i, p: (walk.tile(k, i), p[0]))

    def spec(k):
        return pl.BlockSpec((rows[k], halves[k][1]), lambda i, p: (walk.tile(k, i), 0))

    specs = [spec(k) for k in range(n)]
    res = pl.pallas_call(
        body, name=name,
        grid_spec=pltpu.PrefetchScalarGridSpec(num_scalar_prefetch=1, grid=(walk.steps,),
                                               in_specs=[g_spec(k) for k in range(n)] + specs, out_specs=specs + specs),
        out_shape=[jax.ShapeDtypeStruct(h, F32) for h in halves] + [jax.ShapeDtypeStruct(h, BF16) for h in halves],
        compiler_params=_cparams(("arbitrary",)),
    )(place, *gs, *gots)
    return [(res[k], res[n + k]) for k in range(n)]


def _add_pieces(place, halves, gots, specs_big, name):
    n = len(halves)
    pieces = [_piece_shape(kind, shape) for kind, shape in specs_big]
    rows = [_add_rows(pr, pc, n) for pr, pc in pieces]
    walk = _Walk(pr // r for (pr, _), r in zip(pieces, rows))

    def body(p_ref, *refs):
        i = pl.program_id(0)
        for k in range(n):
            m_ref, g_ref, o_ref = (refs[j * n + k] for j in range(3))

            @pl.when(walk.mine(k, i))
            def _(m_ref=m_ref, g_ref=g_ref, o_ref=o_ref):
                acc = m_ref[...]
                for j in range(3):
                    acc = acc + g_ref[j].astype(F32)
                o_ref[...] = acc

    def m_spec(k):
        if specs_big[k][0] == "col":
            return pl.BlockSpec((rows[k], pieces[k][1]), lambda i, p: (walk.tile(k, i), p[1]))
        return pl.BlockSpec((rows[k], pieces[k][1]), lambda i, p: (p[1] * walk.tiles[k] + walk.tile(k, i), 0))

    def got_spec(k):
        return pl.BlockSpec((3, rows[k], pieces[k][1]), lambda i, p: (0, walk.tile(k, i), 0))

    def o_spec(k):
        if specs_big[k][0] == "col":
            return pl.BlockSpec((rows[k], pieces[k][1]), lambda i, p: (p[0] * walk.tiles[k] + walk.tile(k, i), 0))
        return pl.BlockSpec((rows[k], pieces[k][1]), lambda i, p: (walk.tile(k, i), p[0]))

    return pl.pallas_call(
        body, name=name,
        grid_spec=pltpu.PrefetchScalarGridSpec(
            num_scalar_prefetch=1, grid=(walk.steps,),
            in_specs=[m_spec(k) for k in range(n)] + [got_spec(k) for k in range(n)],
            out_specs=[o_spec(k) for k in range(n)]),
        out_shape=[jax.ShapeDtypeStruct(_shard_shape(kind, shape), F32) for kind, shape in specs_big],
        compiler_params=_cparams(("arbitrary",)),
    )(place, *halves, *gots)


SMALL_ROWS = 1024 + 8 * 8 + 8


def _x_small_all_reduce(p):
    def parts(p_ref, sc):
        slots, ssem, rsem = sc[0], sc[2], sc[3]
        x, y, c = lax.axis_index("x"), lax.axis_index("y"), lax.axis_index("c")
        me = 4 * x + 2 * y + c
        out = []
        for r in range(1, 8):
            bx, by, bc = (r >> 2) & 1, (r >> 1) & 1, r & 1
            tgt = (1 - x if bx else x, 1 - y if by else y, 1 - c if bc else c)
            send = _rcopy(p_ref, slots.at[me], ssem.at[r - 1], rsem.at[r - 1], tgt)
            src = 4 * tgt[0] + 2 * tgt[1] + tgt[2]
            recv = _rcopy(p_ref, slots.at[src], ssem.at[r - 1], rsem.at[r - 1], tgt)
            out.append((send, recv))
        return me, out

    def start(ins, outs, sc):
        me, cps = parts(ins[0], sc)
        pltpu.make_async_copy(ins[0], sc[0].at[me], sc[4].at[0]).start()
        for send, _ in cps:
            send.start()

    def finish(ins, outs, sc):
        me, cps = parts(ins[0], sc)
        pltpu.make_async_copy(ins[0], sc[0].at[me], sc[4].at[0]).wait()
        for _, recv in cps:
            recv.wait_recv()
        acc = sc[0][0]
        for d in range(1, 8):
            acc = acc + sc[0][d]
        sc[1][...] = acc
        back = pltpu.make_async_copy(sc[1], outs[0], sc[4].at[1])
        back.start()
        for send, _ in cps:
            send.wait_send()
        back.wait()

    return _Exchange([p], [jax.ShapeDtypeStruct((SMALL_ROWS, CH), F32)], {},
                     [pltpu.VMEM((8, SMALL_ROWS, CH), F32), pltpu.VMEM((SMALL_ROWS, CH), F32), _dma_sems(7), _dma_sems(7),
                      _dma_sems(2)], start, finish)


def _rope_tables(positions, comm=None):
    T = positions.shape[0]
    inv_freq = 500000.0 ** (-jnp.arange(0, 2 * ROPE_HALF, 2, dtype=F32) / (2 * ROPE_HALF))
    head = jnp.concatenate([inv_freq, inv_freq, jnp.zeros((HD - 2 * ROPE_HALF,), F32)])
    lane_freq = jnp.concatenate([head, head])[None, :]
    pos = jnp.broadcast_to(positions.astype(F32)[:, None], (T, CH))
    tm = min(1024, T)

    def body(p_ref, f_ref, c_ref, s1_ref, s2_ref):
        ang = p_ref[...] * f_ref[...]
        sin = jnp.sin(ang)
        first = (lax.broadcasted_iota(jnp.int32, ang.shape, 1) % HD) < ROPE_HALF
        c_ref[...] = jnp.cos(ang)
        s1_ref[...] = jnp.where(first, -sin, 0.0)
        s2_ref[...] = jnp.where(first, 0.0, sin)

    return _call(body, (pos, lane_freq), name="rope_tables", grid=(T // tm,),
                 in_specs=[_rows(tm, CH), _const((1, CH))], out_specs=[_rows(tm, CH)] * 3,
                 out_shape=[jax.ShapeDtypeStruct((T, CH), F32)] * 3, sem=("parallel",), comm=comm)


BIG_NAMES = ("w_in", "w_a", "w_b", "w_o", "w_ff_in", "w_ff_out")
SMALL_NAMES = ("w_spatial", "ln_v_gain", "ln_v_bias", "b_spatial", "sinks", "norm_mix_pre", "norm_mix_post",
               "norm_ff_pre", "norm_ff_post")
WEIGHT_ORDER = ("w_in", "ln_v_gain", "ln_v_bias", "w_spatial", "b_spatial", "sinks", "w_a", "w_b", "w_o",
                "norm_mix_pre", "norm_mix_post", "w_ff_in", "w_ff_out", "norm_ff_pre", "norm_ff_post")


def _pack_small(d, loss_sums=None):
    parts = []
    for n in SMALL_NAMES:
        flat = d[n].reshape(-1)
        pad = (-flat.shape[0]) % (8 * CH)
        parts.append(jnp.pad(flat, (0, pad)).reshape(-1, CH))
    parts.append(jnp.zeros((8, CH), F32) if loss_sums is None else loss_sums.reshape(8, CH))
    return jnp.concatenate(parts, axis=0)


def _unpack_small(p, like):
    out, row = {}, 0
    for n in SMALL_NAMES:
        size = like[n].size
        rows = -(-size // (8 * CH)) * 8
        out[n] = p[row:row + rows].reshape(-1)[:size].reshape(like[n].shape)
        row += rows
    return out


def kernel(x, positions, w_in, ln_v_gain, ln_v_bias, w_spatial, b_spatial, sinks, w_a, w_b, w_o, norm_mix_pre, norm_mix_post, w_ff_in, w_ff_out, norm_ff_pre, norm_ff_post, loss_target, m_w_in, m_ln_v_gain, m_ln_v_bias, m_w_spatial, m_b_spatial, m_sinks, m_w_a, m_w_b, m_w_o, m_norm_mix_pre, m_norm_mix_post, m_w_ff_in, m_w_ff_out, m_norm_ff_pre, m_norm_ff_post, v_w_in, v_ln_v_gain, v_ln_v_bias, v_w_spatial, v_b_spatial, v_sinks, v_w_a, v_w_b, v_w_o, v_norm_mix_pre, v_norm_mix_post, v_w_ff_in, v_w_ff_out, v_norm_ff_pre, v_norm_ff_post):
    w = dict(w_in=w_in, ln_v_gain=ln_v_gain, ln_v_bias=ln_v_bias, w_spatial=w_spatial, b_spatial=b_spatial, sinks=sinks,
             w_a=w_a, w_b=w_b, w_o=w_o, norm_mix_pre=norm_mix_pre, norm_mix_post=norm_mix_post, w_ff_in=w_ff_in,
             w_ff_out=w_ff_out, norm_ff_pre=norm_ff_pre, norm_ff_post=norm_ff_post)
    m = dict(w_in=m_w_in, ln_v_gain=m_ln_v_gain, ln_v_bias=m_ln_v_bias, w_spatial=m_w_spatial, b_spatial=m_b_spatial,
             sinks=m_sinks, w_a=m_w_a, w_b=m_w_b, w_o=m_w_o, norm_mix_pre=m_norm_mix_pre, norm_mix_post=m_norm_mix_post,
             w_ff_in=m_w_ff_in, w_ff_out=m_w_ff_out, norm_ff_pre=m_norm_ff_pre, norm_ff_post=m_norm_ff_post)
    v = dict(w_in=v_w_in, ln_v_gain=v_ln_v_gain, ln_v_bias=v_ln_v_bias, w_spatial=v_w_spatial, b_spatial=v_b_spatial,
             sinks=v_sinks, w_a=v_w_a, w_b=v_w_b, w_o=v_w_o, norm_mix_pre=v_norm_mix_pre, norm_mix_post=v_norm_mix_post,
             w_ff_in=v_w_ff_in, w_ff_out=v_w_ff_out, norm_ff_pre=v_norm_ff_pre, norm_ff_post=v_norm_ff_post)

    FIRST, REST = (0,), tuple(range(1, NBIG))
    shards = [w[n][0].astype(BF16) for n in BIG_NAMES]
    place = jnp.stack([lax.axis_index("c"), 2 * lax.axis_index("x") + lax.axis_index("y")]).astype(jnp.int32)
    xs, target = x[0], loss_target[0]
    T = xs.shape[0]
    wtm, wtm2 = min(1024, T), min(2048, T)
    g1, g2, g3, g4 = norm_mix_pre, norm_mix_post, norm_ff_pre, norm_ff_post
    w_sp, snk = w_spatial[0], sinks[0]
    MIX, FF = (1, 2, 3), (4, 5)
    bfull = jnp.repeat(b_spatial[0].T, CH, axis=1)

    def reduce_tail(ws, grads, got):
        tag = "_".join(BIG_NAMES[k] for k in ws)
        sums = _add_halves(place, grads, got, [BIG[k][0] for k in ws], name="grad_add_sibling_" + tag)
        return sums, _x_grads_chips([s[1] for s in sums], ws)

    def reduce_end(ws, sums, pieces):
        tag = "_".join(BIG_NAMES[k] for k in ws)
        return _add_pieces(place, [s[0] for s in sums], pieces, [BIG[k] for k in ws], name="grad_add_chips_" + tag)

    (rc, rs1, rs2), w_in_part = _rope_tables(positions[0], comm=_x_gather_ici(shards[:1], FIRST))
    w_in_b = _run(_x_gather_d2d(w_in_part, FIRST), "gather_w_in_d2d")[0]
    (h, u, vs, q, k, va, ga, gb), ff_part = _inproj(xs, g1, w_in_b, rc, rs1, rs2, tm=512, comm=_x_gather_ici(shards[4:], FF))
    a, mix_part = _sgu_fwd(u, vs, ln_v_gain, ln_v_bias, w_sp, bfull, tm=512, comm=_x_gather_ici(shards[1:4], MIX))
    att, rest = _attn_fwd(q, k, va, snk, comm=_both(_x_gather_d2d(mix_part, MIX), _x_gather_d2d(ff_part, FF)))
    w_a_b, w_b_b, w_o_b, w_ff_in_b, w_ff_out_b = rest
    pa, pb, merged, mix, x1 = _merge_fwd(a, att, ga, gb, xs, w_a_b, w_b_b, w_o_b, g2, tm=512)
    hf, f2, dff, df1, dx1, lsum, dg3, dg4 = _ffn(x1, target, w_ff_in_b, w_ff_out_b, g3, g4, tm=256)

    dw_ff_out, _ = _wgrad(f2, dff, tn=1024, tm=512, name="wgrad_ff_out")
    dw_ff_in, _ = _wgrad(hf, df1, tn=2048, tm=wtm2, name="wgrad_ff_in")
    (dga, dgb, da, datt, dg2, dw_a, dw_b, dw_o), _ = _merge_bwd(
        dx1, mix, ga, gb, pa, pb, a, att, merged, w_a_b, w_b_b, w_o_b, g2, tm=256)
    grads_rest = [dw_a, dw_b, dw_o, dw_ff_in, dw_ff_out]
    (du, dvs, dws, dbs, dlg, dlb), got_rest = _sgu_bwd(
        u, vs, da, ln_v_gain, ln_v_bias, w_sp, bfull, tm=512, comm=_x_grads_sibling(grads_rest, REST))
    sums_rest, to_chips = reduce_tail(REST, grads_rest, got_rest)
    (dq, dk, dva, dsk), pieces_rest = _attn_bwd(q, k, va, datt, snk, rc, rs1, rs2, comm=to_chips)
    partial_rest = reduce_end(REST, sums_rest, pieces_rest)
    (dx, dproj, dg1), _ = _inproj_bwd([du, dvs, dq, dk, dva, dga, dgb], xs, dx1, g1, w_in_b, tm=512)
    small = dict(ln_v_gain=dlg, ln_v_bias=dlb, w_spatial=dws, b_spatial=dbs, sinks=dsk[:, :NQ],
                 norm_mix_pre=dg1, norm_mix_post=dg2, norm_ff_pre=dg3, norm_ff_post=dg4)
    dw_in, (gs, *shard_rest) = _wgrad(
        h, dproj, tn=IN_W // 2, tm=wtm, name="wgrad_in",
        comm=_both(_x_small_all_reduce(_pack_small(small, lsum)), _x_grads_share(partial_rest, REST)))
    got_in = _run(_x_grads_sibling([dw_in], FIRST), "grads_in_to_sibling")
    sums_in, to_chips = reduce_tail(FIRST, [dw_in], got_in)
    partial_in = reduce_end(FIRST, sums_in, _run(to_chips, "grads_in_to_chips"))
    g_in = _run(_x_grads_share(partial_in, FIRST), "grads_in_share")[0]

    loss = 0.5 * jnp.sum(gs[SMALL_ROWS - 8:]) / D
    rest_names = [BIG_NAMES[k] for k in REST]
    early = _adamw([w[n][0] for n in rest_names] + [_pack_small(w)], list(shard_rest) + [gs],
                   [m[n][0] for n in rest_names] + [_pack_small(m)], [v[n][0] for n in rest_names] + [_pack_small(v)],
                   [64] * len(REST) + [SMALL_ROWS], name="adamw_rest_and_small")
    last = _adamw([w["w_in"][0]], [g_in], [m["w_in"][0]], [v["w_in"][0]], [256], name="adamw_w_in")
    grad, delta, new_m, new_v = {}, {}, {}, {}
    for n, (g_, d_, m_, v_) in zip(["w_in"] + rest_names, last + early[:-1]):
        grad[n], delta[n], new_m[n], new_v[n] = g_[None], d_[None], m_[None], v_[None]
    gs, ds, ms, vs = early[-1]
    for packed, dst in ((gs, grad), (ds, delta), (ms, new_m), (vs, new_v)):
        dst.update(_unpack_small(packed, w))

    outs = [loss, dx[None]]
    for group in (grad, delta, new_m, new_v):
        outs.extend(group[n] for n in WEIGHT_ORDER)
    return tuple(outs)
```

```python
import functools

import jax
import jax.numpy as jnp
from jax import lax
from jax.experimental import pallas as pl
from jax.experimental.pallas import tpu as pltpu

F32 = jnp.float32
BF16 = jnp.bfloat16

D = 1024
CH = 128
NG = 8
HD = 64
NQ = 16
NKV = 4
KVW = NKV * HD
DFF = 4 * D
EPS = 1e-6
IN_W = 5632
SEG = (0, 1024, 2048, 3072, 3328, 3584, 4608, 5632)
ROPE_HALF = 8
Q_SCALE = HD ** -0.5

LR, B1, B2, AEPS, WD, STEP = 0.001, 0.9, 0.999, 1e-08, 0.01, 10

VMEM_LIMIT = 60 * 1024 * 1024
MESH = pl.DeviceIdType.MESH

_GELU_C0 = 0.7978845608028654
_GELU_C1 = 0.044715


def _cparams(sem=None):
    kw = dict(vmem_limit_bytes=VMEM_LIMIT)
    if sem is not None:
        kw["dimension_semantics"] = sem
    return pltpu.CompilerParams(**kw)


def _resident(shape):
    nd = len(shape)
    return pl.BlockSpec(shape, lambda *_: (0,) * nd, pipeline_mode=pl.Buffered(1))


def _const(shape):
    nd = len(shape)
    return pl.BlockSpec(shape, lambda *_: (0,) * nd)


def _rows(tm, w):
    return pl.BlockSpec((tm, w), lambda i: (i, 0))


class _Exchange:
    def __init__(self, ins, outs, aliases, scratch, start, finish):
        self.ins, self.outs, self.aliases, self.scratch = list(ins), list(outs), dict(aliases), list(scratch)
        self.start, self.finish = start, finish


def _both(a, b):
    na, ma, sa = len(a.ins), len(a.outs), len(a.scratch)

    def start(ci, co, cs):
        a.start(ci[:na], co[:ma], cs[:sa])
        b.start(ci[na:], co[ma:], cs[sa:])

    def finish(ci, co, cs):
        a.finish(ci[:na], co[:ma], cs[:sa])
        b.finish(ci[na:], co[ma:], cs[sa:])

    aliases = {**a.aliases, **{na + i: ma + j for i, j in b.aliases.items()}}
    return _Exchange(a.ins + b.ins, a.outs + b.outs, aliases, a.scratch + b.scratch, start, finish)


def _call(body, args, *, name, grid, in_specs, out_specs, out_shape, scratch_shapes=(), sem=None, comm=None):
    single = not isinstance(out_shape, (list, tuple))
    out_shape = [out_shape] if single else list(out_shape)
    out_specs = [out_specs] if single else list(out_specs)
    if comm is None:
        res = pl.pallas_call(body, name=name, grid=grid, in_specs=list(in_specs), out_specs=out_specs,
                             out_shape=out_shape, scratch_shapes=list(scratch_shapes),
                             compiler_params=_cparams(sem))(*args)
        return (res[0] if single else res), []
    n_in, n_out, n_scr = len(args), len(out_shape), len(scratch_shapes)
    nci, nco = len(comm.ins), len(comm.outs)
    steps = 1
    for g in grid:
        steps *= g

    def hosted(*refs):
        a, ci = refs[:n_in], refs[n_in:n_in + nci]
        o, co = refs[n_in + nci:n_in + nci + n_out], refs[n_in + nci + n_out:n_in + nci + n_out + nco]
        rest = refs[n_in + nci + n_out + nco:]
        scr, cs = rest[:n_scr], rest[n_scr:]
        step = pl.program_id(0)
        for d in range(1, len(grid)):
            step = step * grid[d] + pl.program_id(d)

        @pl.when(step == 0)
        def _():
            comm.start(ci, co, cs)

        body(*a, *o, *scr)

        @pl.when(step == steps - 1)
        def _():
            comm.finish(ci, co, cs)

    res = pl.pallas_call(
        hosted, name=name, grid=grid, in_specs=list(in_specs) + [ANY] * nci, out_specs=out_specs + [ANY] * nco,
        out_shape=out_shape + comm.outs, scratch_shapes=list(scratch_shapes) + comm.scratch,
        input_output_aliases={n_in + i: n_out + j for i, j in comm.aliases.items()},
        compiler_params=_cparams(("arbitrary",) * len(grid)),
    )(*args, *comm.ins)
    own = res[:n_out]
    return (own[0] if single else own), list(res[n_out:])


def _run(comm, name):
    nci = len(comm.ins)

    def body(*refs):
        ci, co, cs = refs[:nci], refs[nci:nci + len(comm.outs)], refs[nci + len(comm.outs):]
        comm.start(ci, co, cs)
        comm.finish(ci, co, cs)

    return pl.pallas_call(
        body, name=name, in_specs=[ANY] * nci, out_specs=[ANY] * len(comm.outs), out_shape=comm.outs,
        scratch_shapes=comm.scratch, input_output_aliases=comm.aliases,
        compiler_params=pltpu.CompilerParams(vmem_limit_bytes=VMEM_LIMIT),
    )(*comm.ins)


def _gelu(x):
    x2 = x * x
    t = jnp.tanh(x * (_GELU_C0 + (_GELU_C0 * _GELU_C1) * x2))
    hx = 0.5 * x
    return hx + hx * t, (t, x2, hx)


def _gelu_grad(parts):
    t, x2, hx = parts
    return (0.5 + 0.5 * t) + hx * (1.0 - t * t) * (_GELU_C0 + (3.0 * _GELU_C0 * _GELU_C1) * x2)


def _sigmoid(x):
    return 1.0 / (1.0 + jnp.exp(-x))


def _rms_hat(x):
    r = lax.rsqrt(jnp.mean(x * x, axis=-1, keepdims=True) + EPS)
    return x * r, r


def _rms_bwd(xhat, r, g, dout):
    dg = jnp.sum(dout * xhat, axis=0, keepdims=True)
    dy = dout * g
    dx = r * (dy - xhat * jnp.mean(dy * xhat, axis=-1, keepdims=True))
    return dx, dg


def _dot(a, b):
    return jnp.dot(a, b, preferred_element_type=F32)


def _dot_nt(a, b):
    return lax.dot_general(a, b, (((1,), (1,)), ((), ())), preferred_element_type=F32)


def _dot_tn(a, b):
    return lax.dot_general(a, b, (((0,), (0,)), ((), ())), preferred_element_type=F32)


def _rope(blk, c, s1, s2):
    return blk * c + pltpu.roll(blk, CH - ROPE_HALF, 1) * s1 + pltpu.roll(blk, ROPE_HALF, 1) * s2


def _rope_t(blk, c, s1, s2):
    return blk * c + pltpu.roll(blk * s1, ROPE_HALF, 1) + pltpu.roll(blk * s2, CH - ROPE_HALF, 1)


def _inproj(x, g1, w_in, rc, rs1, rs2, tm, comm=None):
    T = x.shape[0]

    def body(x_ref, g_ref, w_ref, c_ref, s1_ref, s2_ref,
             h_ref, u_ref, v_ref, q_ref, k_ref, va_ref, ga_ref, gb_ref):
        xhat, _ = _rms_hat(x_ref[...])
        h = (xhat * g_ref[...]).astype(BF16)
        h_ref[...] = h
        uv = _dot(h, w_ref[:, SEG[0]:SEG[2]])
        u_ref[...] = uv[:, :D]
        v_ref[...] = uv[:, D:]
        c, s1, s2 = c_ref[...], s1_ref[...], s2_ref[...]
        qkv = _dot(h, w_ref[:, SEG[2]:SEG[5]])
        for p in range(D // CH):
            blk = _rope(qkv[:, CH * p:CH * (p + 1)], c, s1, s2) * Q_SCALE
            q_ref[:, CH * p:CH * (p + 1)] = blk.astype(BF16)
        for p in range(KVW // CH):
            k_ref[:, CH * p:CH * (p + 1)] = _rope(qkv[:, D + CH * p:D + CH * (p + 1)], c, s1, s2).astype(BF16)
        va_ref[...] = qkv[:, D + KVW:].astype(BF16)
        gates = _dot(h, w_ref[:, SEG[5]:SEG[7]]).astype(BF16)
        ga_ref[...] = gates[:, :D]
        gb_ref[...] = gates[:, D:]

    sd = jax.ShapeDtypeStruct
    return _call(
        body, (x, g1, w_in, rc, rs1, rs2), name="inproj_fwd", grid=(T // tm,),
        in_specs=[_rows(tm, D), _const((1, D)), _resident((D, IN_W)), _rows(tm, CH), _rows(tm, CH), _rows(tm, CH)],
        out_specs=[_rows(tm, D), _rows(tm, D), _rows(tm, D), _rows(tm, D), _rows(tm, KVW), _rows(tm, KVW),
                   _rows(tm, D), _rows(tm, D)],
        out_shape=[sd((T, D), BF16), sd((T, D), F32), sd((T, D), F32), sd((T, D), BF16), sd((T, KVW), BF16),
                   sd((T, KVW), BF16), sd((T, D), BF16), sd((T, D), BF16)],
        sem=("parallel",), comm=comm)


def _sgu_common(u, vs, lng, lnb, ws_ref, bfull):
    nc = u.shape[0] // CH
    ug, tu = _gelu(u)
    vg, tv = _gelu(vs)
    mu = jnp.mean(vg, axis=-1, keepdims=True)
    xc = vg - mu
    rstd = lax.rsqrt(jnp.mean(xc * xc, axis=-1, keepdims=True) + EPS)
    vhat = xc * rstd
    vnb = (vhat * lng + lnb).astype(BF16)
    tri = lax.broadcasted_iota(jnp.int32, (CH, CH), 0) >= lax.broadcasted_iota(jnp.int32, (CH, CH), 1)
    wts, rhss, mixed = [], [], []
    for g in range(NG):
        wt = jnp.where(tri, ws_ref[g], 0.0).astype(BF16)
        rhs = jnp.concatenate([vnb[CH * c:CH * (c + 1), CH * g:CH * (g + 1)] for c in range(nc)], axis=1)
        mix = _dot(wt, rhs)
        wts.append(wt)
        rhss.append(rhs)
        mixed.append([mix[:, CH * c:CH * (c + 1)] + bfull[:, CH * g:CH * (g + 1)] for c in range(nc)])
    return nc, ug, tu, tv, rstd, vhat, tri, wts, rhss, mixed


def _sgu_fwd(u, vs, lng, lnb, ws, bfull, tm, comm=None):
    T = u.shape[0]

    def body(u_ref, v_ref, lng_ref, lnb_ref, ws_ref, bf_ref, a_ref):
        nc, ug, _, _, _, _, _, _, _, mixed = _sgu_common(
            u_ref[...], v_ref[...], lng_ref[...], lnb_ref[...], ws_ref, bf_ref[...])
        mixed_all = jnp.concatenate(
            [jnp.concatenate([mixed[g][c] for g in range(NG)], axis=1) for c in range(nc)], axis=0)
        a_ref[...] = (ug * mixed_all).astype(BF16)

    return _call(
        body, (u, vs, lng, lnb, ws, bfull), name="sgu_fwd", grid=(T // tm,),
        in_specs=[_rows(tm, D), _rows(tm, D), _const((1, D)), _const((1, D)), _const((NG, CH, CH)), _const((CH, D))],
        out_specs=_rows(tm, D), out_shape=jax.ShapeDtypeStruct((T, D), BF16), sem=("parallel",), comm=comm)


def _sgu_bwd(u, vs, da, lng, lnb, ws, bfull, tm, comm=None):
    T = u.shape[0]
    nsteps = T // tm

    def body(u_ref, v_ref, da_ref, lng_ref, lnb_ref, ws_ref, bf_ref,
             du_ref, dv_ref, dws_ref, dbs_ref, dlg_ref, dlb_ref, db_ref):
        i = pl.program_id(0)
        u, vs, da, lng = u_ref[...], v_ref[...], da_ref[...], lng_ref[...]
        nc, ug, tu, tv, rstd, vhat, tri, wts, rhss, mixed = _sgu_common(u, vs, lng, lnb_ref[...], ws_ref, bf_ref[...])

        @pl.when(i == 0)
        def _():
            dws_ref[...] = jnp.zeros_like(dws_ref)
            db_ref[...] = jnp.zeros_like(db_ref)
            dlg_ref[...] = jnp.zeros_like(dlg_ref)
            dlb_ref[...] = jnp.zeros_like(dlb_ref)

        mixed_all = jnp.concatenate(
            [jnp.concatenate([mixed[g][c] for g in range(NG)], axis=1) for c in range(nc)], axis=0)
        du_ref[...] = (da * mixed_all * _gelu_grad(tu)).astype(BF16)
        dmixed = da * ug
        dvn_cols = []
        for g in range(NG):
            dmix = [dmixed[CH * c:CH * (c + 1), CH * g:CH * (g + 1)] for c in range(nc)]
            db_ref[:, CH * g:CH * (g + 1)] += functools.reduce(lambda a, b: a + b, dmix)
            dm = jnp.concatenate(dmix, axis=1).astype(BF16)
            dws_ref[g] += _dot_nt(dm, rhss[g])
            dvn_cols.append(_dot_tn(wts[g], dm))
        dvn = jnp.concatenate(
            [jnp.concatenate([dvn_cols[g][:, CH * c:CH * (c + 1)] for g in range(NG)], axis=1) for c in range(nc)],
            axis=0)
        dlg_ref[...] += jnp.sum(dvn * vhat, axis=0, keepdims=True)
        dlb_ref[...] += jnp.sum(dvn, axis=0, keepdims=True)
        dvh = dvn * lng
        dvg = rstd * (dvh - jnp.mean(dvh, axis=-1, keepdims=True)
                      - vhat * jnp.mean(dvh * vhat, axis=-1, keepdims=True))
        dv_ref[...] = (dvg * _gelu_grad(tv)).astype(BF16)

        @pl.when(i == nsteps - 1)
        def _():
            for g in range(NG):
                dws_ref[g] = jnp.where(tri, dws_ref[g], 0.0)
                dbs_ref[g:g + 1, :] = jnp.sum(db_ref[:, CH * g:CH * (g + 1)].T, axis=0, keepdims=True)

    sd = jax.ShapeDtypeStruct
    return _call(
        body, (u, vs, da, lng, lnb, ws, bfull), name="sgu_bwd", grid=(nsteps,),
        in_specs=[_rows(tm, D), _rows(tm, D), _rows(tm, D), _const((1, D)), _const((1, D)), _const((NG, CH, CH)),
                  _const((CH, D))],
        out_specs=[_rows(tm, D), _rows(tm, D), _const((NG, CH, CH)), _const((NG, CH)), _const((1, D)), _const((1, D))],
        out_shape=[sd((T, D), BF16), sd((T, D), BF16), sd((NG, CH, CH), F32), sd((NG, CH), F32), sd((1, D), F32),
                   sd((1, D), F32)],
        scratch_shapes=[pltpu.VMEM((CH, D), F32)], sem=("arbitrary",), comm=comm)


def _pair_layout(prev, cur, grp):
    j, half = grp // 2, grp % 2
    blk = jnp.concatenate([prev[:, CH * j:CH * (j + 1)], cur[:, CH * j:CH * (j + 1)]], axis=0).astype(F32)
    lo = lax.broadcasted_iota(jnp.int32, blk.shape, 1) < HD
    rolled = pltpu.roll(blk, HD, 1)
    even = jnp.where(lo, blk if half == 0 else rolled, 0.0)
    odd = jnp.where(lo, 0.0, rolled if half == 0 else blk)
    return jnp.concatenate([even, odd], axis=0).astype(BF16)


def _attn_mask(n):
    qi = lax.broadcasted_iota(jnp.int32, (CH, 2 * CH), 0)
    kc = lax.broadcasted_iota(jnp.int32, (CH, 2 * CH), 1)
    ok = (kc > qi) & (kc <= qi + CH) & ((kc >= CH) | (n > 0))
    return jnp.concatenate([ok, ok], axis=1)


def _softmax_sink(s, sink):
    m = jnp.maximum(jnp.max(s, axis=-1, keepdims=True), sink)
    p = jnp.exp(s - m)
    ps = jnp.exp(sink - m)
    inv = 1.0 / (jnp.sum(p, axis=-1, keepdims=True) + ps)
    return p * inv, ps * inv


def _attn_fwd(q, k, va, sinks, comm=None):
    T = q.shape[0]
    nsteps = T // (2 * CH)
    npairs = D // CH

    def body(sk_ref, q_ref, kp_ref, kc_ref, vp_ref, vc_ref, o_ref):
        n = pl.program_id(0)
        even_lanes = lax.broadcasted_iota(jnp.int32, (CH, CH), 1) < HD
        ks = [kp_ref[...], kc_ref[:CH], kc_ref[CH:]]
        vs = [vp_ref[...], vc_ref[:CH], vc_ref[CH:]]
        masks = [_attn_mask(2 * n), _attn_mask(1)]
        kks = [[_pair_layout(ks[b], ks[b + 1], grp) for grp in range(NKV)] for b in range(2)]
        vvs = [[_pair_layout(vs[b], vs[b + 1], grp) for grp in range(NKV)] for b in range(2)]
        work = [(b, p) for b in range(2) for p in range(npairs)]

        def scores(i):
            b, p = work[i]
            return _dot_nt(q_ref[CH * b:CH * (b + 1), CH * p:CH * (p + 1)], kks[b][p // 2])

        def unnormalised(s, sink):
            m = jnp.maximum(jnp.max(s, axis=-1, keepdims=True), sink)
            p = jnp.exp(s - m)
            return p, 1.0 / (jnp.sum(p, axis=-1, keepdims=True) + jnp.exp(sink - m))

        def value_product(i):
            b, p = work[i]
            pr, ie, io = probs[i]
            return _dot(pr, vvs[b][p // 2]) * jnp.where(even_lanes, ie, io)

        ahead = 3
        outs, probs = [], []
        pending = [scores(i) for i in range(ahead)]
        for i, (b, p) in enumerate(work):
            s = jnp.where(masks[b], pending.pop(0), -1e30)
            if i + ahead < len(work):
                pending.append(scores(i + ahead))
            pe, ie = unnormalised(s[:, :2 * CH], sk_ref[2 * p])
            po, io = unnormalised(s[:, 2 * CH:], sk_ref[2 * p + 1])
            probs.append((jnp.concatenate([pe, po], axis=1).astype(BF16), ie, io))
            if i >= 1:
                outs.append(value_product(i - 1))
        outs.append(value_product(len(work) - 1))
        for b in range(2):
            o_ref[CH * b:CH * (b + 1), :] = jnp.concatenate(outs[npairs * b:npairs * (b + 1)], axis=1).astype(BF16)

    prev = lambda n: (jnp.maximum(2 * n - 1, 0), 0)
    cur = lambda n: (n, 0)
    return _call(
        body, (sinks, q, k, k, va, va), name="attn_fwd", grid=(nsteps,),
        in_specs=[pl.BlockSpec(memory_space=pltpu.SMEM), pl.BlockSpec((2 * CH, D), cur),
                  pl.BlockSpec((CH, KVW), prev), pl.BlockSpec((2 * CH, KVW), cur),
                  pl.BlockSpec((CH, KVW), prev), pl.BlockSpec((2 * CH, KVW), cur)],
        out_specs=pl.BlockSpec((2 * CH, D), cur), out_shape=jax.ShapeDtypeStruct((T, D), BF16),
        sem=("parallel",), comm=comm)


def _attn_bwd(q, k, va, datt, sinks, rc, rs1, rs2, comm=None):
    T = q.shape[0]
    nb = T // CH

    def body(sk_ref, q_ref, kp_ref, kc_ref, vp_ref, vc_ref, do_ref, cq_ref, s1q_ref, s2q_ref, ck_ref, s1k_ref, s2k_ref,
             dq_ref, dk_ref, dv_ref, dsk_ref, kcar, vcar):
        n = pl.program_id(0)

        @pl.when(n == 0)
        def _():
            kcar[...] = jnp.zeros_like(kcar)
            vcar[...] = jnp.zeros_like(vcar)
            dsk_ref[...] = jnp.zeros_like(dsk_ref)

        def flush(kprev, vprev):
            ck, s1k, s2k = ck_ref[...], s1k_ref[...], s2k_ref[...]
            for j in range(KVW // CH):
                sl = slice(CH * j, CH * (j + 1))
                dk_ref[:, sl] = _rope_t(kcar[:, sl] + kprev[:, sl], ck, s1k, s2k).astype(BF16)
                dv_ref[:, sl] = (vcar[:, sl] + vprev[:, sl]).astype(BF16)

        @pl.when(n < nb)
        def _():
            mask = _attn_mask(n)
            kp, kc, vp, vc = kp_ref[...], kc_ref[...], vp_ref[...], vc_ref[...]
            cq, s1q, s2q = cq_ref[...], s1q_ref[...], s2q_ref[...]
            lane = lax.broadcasted_iota(jnp.int32, (1, CH), 1)
            dsk = jnp.zeros((1, CH), F32)
            npairs = D // CH
            kks = [_pair_layout(kp, kc, grp) for grp in range(NKV)]
            vvs = [_pair_layout(vp, vc, grp) for grp in range(NKV)]
            qs = [q_ref[:, CH * p:CH * (p + 1)] for p in range(npairs)]
            dos = [do_ref[:, CH * p:CH * (p + 1)].astype(BF16) for p in range(npairs)]

            def first(p):
                return _dot_nt(qs[p], kks[p // 2]), _dot_nt(dos[p], vvs[p // 2])

            def last(p, ds, pb):
                return (_rope_t(_dot(ds, kks[p // 2]), cq, s1q, s2q) * Q_SCALE, _dot_tn(qs[p], ds), _dot_tn(dos[p], pb))

            ahead = 2
            pending = [first(p) for p in range(ahead)]
            mids, ends = [], []
            for p in range(npairs):
                s, dp = pending.pop(0)
                s = jnp.where(mask, s, -1e30)
                if p + ahead < npairs:
                    pending.append(first(p + ahead))
                ds_parts, p_parts = [], []
                for par in range(2):
                    sl = slice(2 * CH * par, 2 * CH * (par + 1))
                    pr, psink = _softmax_sink(s[:, sl], sk_ref[2 * p + par])
                    delta = jnp.sum(pr * dp[:, sl], axis=-1, keepdims=True)
                    ds_parts.append(pr * (dp[:, sl] - delta))
                    p_parts.append(pr)
                    tot = -jnp.sum(psink * delta, axis=0, keepdims=True)
                    dsk = dsk + jnp.where(lane == 2 * p + par, tot, 0.0)
                mids.append((jnp.concatenate(ds_parts, axis=1).astype(BF16), jnp.concatenate(p_parts, axis=1).astype(BF16)))
                if p >= 1:
                    ends.append(last(p - 1, *mids[p - 1]))
            ends.append(last(npairs - 1, *mids[-1]))
            dq_cols = [e[0] for e in ends]
            def fold(i):
                rows = []
                for grp in range(NKV):
                    acc = ends[2 * grp][i] + ends[2 * grp + 1][i]
                    rows.append(acc[:HD, :2 * CH] + acc[HD:, 2 * CH:])
                return jnp.concatenate(rows, axis=0).T

            dkf, dvf = fold(1), fold(2)
            dq_ref[...] = jnp.concatenate(dq_cols, axis=1).astype(BF16)
            dsk_ref[...] += dsk
            flush(dkf[:CH], dvf[:CH])
            kcar[...] = dkf[CH:]
            vcar[...] = dvf[CH:]

        @pl.when(n == nb)
        def _():
            z = jnp.zeros((CH, KVW), F32)
            flush(z, z)

    last = nb - 1
    cur = lambda n: (jnp.minimum(n, last), 0)
    prev = lambda n: (jnp.clip(n - 1, 0, last), 0)
    sd = jax.ShapeDtypeStruct
    return _call(
        body, (sinks, q, k, k, va, va, datt, rc, rs1, rs2, rc, rs1, rs2), name="attn_bwd", grid=(nb + 1,),
        in_specs=[pl.BlockSpec(memory_space=pltpu.SMEM), pl.BlockSpec((CH, D), cur),
                  pl.BlockSpec((CH, KVW), prev), pl.BlockSpec((CH, KVW), cur),
                  pl.BlockSpec((CH, KVW), prev), pl.BlockSpec((CH, KVW), cur),
                  pl.BlockSpec((CH, D), cur),
                  pl.BlockSpec((CH, CH), cur), pl.BlockSpec((CH, CH), cur), pl.BlockSpec((CH, CH), cur),
                  pl.BlockSpec((CH, CH), prev), pl.BlockSpec((CH, CH), prev), pl.BlockSpec((CH, CH), prev)],
        out_specs=[pl.BlockSpec((CH, D), cur), pl.BlockSpec((CH, KVW), prev), pl.BlockSpec((CH, KVW), prev),
                   _const((1, CH))],
        out_shape=[sd((T, D), BF16), sd((T, KVW), BF16), sd((T, KVW), BF16), sd((1, CH), F32)],
        scratch_shapes=[pltpu.VMEM((CH, KVW), F32), pltpu.VMEM((CH, KVW), F32)], sem=("arbitrary",), comm=comm)


def _merge_fwd(a, att, ga, gb, x, w_a, w_b, w_o, g2, tm):
    T = x.shape[0]

    def body(a_ref, att_ref, ga_ref, gb_ref, x_ref, wa_ref, wb_ref, wo_ref, g_ref,
             pa_ref, pb_ref, mg_ref, mix_ref, x1_ref):
        pa = _dot(a_ref[...], wa_ref[...])
        pb = _dot(att_ref[...], wb_ref[...])
        pa_ref[...] = pa.astype(BF16)
        pb_ref[...] = pb.astype(BF16)
        merged = (_sigmoid(ga_ref[...].astype(F32)) * pa + _sigmoid(gb_ref[...].astype(F32)) * pb).astype(BF16)
        mg_ref[...] = merged
        mix = _dot(merged, wo_ref[...])
        mix_ref[...] = mix
        mhat, _ = _rms_hat(mix)
        x1_ref[...] = x_ref[...] + mhat * g_ref[...]

    sd = jax.ShapeDtypeStruct
    return pl.pallas_call(
        body, name="merge_fwd", grid=(T // tm,),
        in_specs=[_rows(tm, D)] * 5 + [_resident((D, D))] * 3 + [_const((1, D))],
        out_specs=[_rows(tm, D)] * 5,
        out_shape=[sd((T, D), BF16), sd((T, D), BF16), sd((T, D), BF16), sd((T, D), F32), sd((T, D), F32)],
        compiler_params=_cparams(("parallel",)),
    )(a, att, ga, gb, x, w_a, w_b, w_o, g2)


def _merge_bwd(dx1, mix, ga, gb, pa, pb, a, att, merged, w_a, w_b, w_o, g2, tm, comm=None):
    T = dx1.shape[0]
    nsteps = T // tm

    def body(dx1_ref, mix_ref, ga_ref, gb_ref, pa_ref, pb_ref, a_ref, att_ref, mg_ref, wa_ref, wb_ref, wo_ref, g_ref,
             dga_ref, dgb_ref, da_ref, datt_ref, dg_ref, dwa_ref, dwb_ref, dwo_ref, acc, sem):
        i = pl.program_id(0)

        @pl.when(i == 0)
        def _():
            dg_ref[...] = jnp.zeros_like(dg_ref)
            acc[...] = jnp.zeros_like(acc)

        mhat, r = _rms_hat(mix_ref[...])
        dmix, dg = _rms_bwd(mhat, r, g_ref[...], dx1_ref[...])
        dg_ref[...] += dg
        dmix = dmix.astype(BF16)
        dmerged = _dot_nt(dmix, wo_ref[...])
        sa = _sigmoid(ga_ref[...].astype(F32))
        sb = _sigmoid(gb_ref[...].astype(F32))
        dao = (dmerged * sa).astype(BF16)
        dbo = (dmerged * sb).astype(BF16)
        dga_ref[...] = (dmerged * pa_ref[...].astype(F32) * (sa * (1.0 - sa))).astype(BF16)
        dgb_ref[...] = (dmerged * pb_ref[...].astype(F32) * (sb * (1.0 - sb))).astype(BF16)
        da_ref[...] = _dot_nt(dao, wa_ref[...])
        datt_ref[...] = _dot_nt(dbo, wb_ref[...]).astype(BF16)
        acc[0] += _dot_tn(a_ref[...], dao)
        acc[1] += _dot_tn(att_ref[...], dbo)
        acc[2] += _dot_tn(mg_ref[...], dmix)

        @pl.when(i == nsteps - 1)
        def _():
            outs = [pltpu.make_async_copy(acc.at[j], ref, sem.at[j]) for j, ref in enumerate((dwa_ref, dwb_ref, dwo_ref))]
            for cp in outs:
                cp.start()
            for cp in outs:
                cp.wait()

    sd = jax.ShapeDtypeStruct
    return _call(
        body, (dx1, mix, ga, gb, pa, pb, a, att, merged, w_a, w_b, w_o, g2), name="merge_bwd", grid=(nsteps,),
        in_specs=[_rows(tm, D)] * 9 + [_resident((D, D))] * 3 + [_const((1, D))],
        out_specs=[_rows(tm, D)] * 4 + [_const((1, D))] + [ANY] * 3,
        out_shape=[sd((T, D), BF16), sd((T, D), BF16), sd((T, D), F32), sd((T, D), BF16), sd((1, D), F32)]
        + [sd((D, D), F32)] * 3,
        scratch_shapes=[pltpu.VMEM((3, D, D), F32), _dma_sems(3)], sem=("arbitrary",), comm=comm)


def _ffn(x1, target, w1, w2, g3, g4, tm):
    T = x1.shape[0]

    def body(x_ref, t_ref, w1_ref, w2_ref, g3_ref, g4_ref,
             hf_ref, f2_ref, dff_ref, df1_ref, dx_ref, ls_ref, dg3_ref, dg4_ref):
        @pl.when(pl.program_id(0) == 0)
        def _():
            ls_ref[...] = jnp.zeros_like(ls_ref)
            dg3_ref[...] = jnp.zeros_like(dg3_ref)
            dg4_ref[...] = jnp.zeros_like(dg4_ref)

        x = x_ref[...]
        g3, g4 = g3_ref[...], g4_ref[...]
        xhat, r3 = _rms_hat(x)
        hf = (xhat * g3).astype(BF16)
        hf_ref[...] = hf
        rl = jnp.maximum(_dot(hf, w1_ref[...]), 0.0)
        f2 = (rl * rl).astype(BF16)
        f2_ref[...] = f2
        fhat, r4 = _rms_hat(_dot(f2, w2_ref[...]))
        err = x + fhat * g4 - t_ref[...]
        ls_ref[...] += jnp.sum(err * err, axis=0, keepdims=True)
        dy = err * (1.0 / D)
        dff, dg4 = _rms_bwd(fhat, r4, g4, dy)
        dg4_ref[...] += dg4
        dff = dff.astype(BF16)
        dff_ref[...] = dff
        df1 = (_dot_nt(dff, w2_ref[...]) * (2.0 * rl)).astype(BF16)
        df1_ref[...] = df1
        dxn, dg3 = _rms_bwd(xhat, r3, g3, _dot_nt(df1, w1_ref[...]))
        dg3_ref[...] += dg3
        dx_ref[...] = dy + dxn

    sd = jax.ShapeDtypeStruct
    return pl.pallas_call(
        body, name="ffn_fwd_bwd", grid=(T // tm,),
        in_specs=[_rows(tm, D), _rows(tm, D), _resident((D, DFF)), _resident((DFF, D)), _const((1, D)), _const((1, D))],
        out_specs=[_rows(tm, D), _rows(tm, DFF), _rows(tm, D), _rows(tm, DFF), _rows(tm, D), _const((1, D)),
                   _const((1, D)), _const((1, D))],
        out_shape=[sd((T, D), BF16), sd((T, DFF), BF16), sd((T, D), BF16), sd((T, DFF), BF16), sd((T, D), F32),
                   sd((1, D), F32), sd((1, D), F32), sd((1, D), F32)],
        compiler_params=_cparams(("arbitrary",)),
    )(x1, target, w1, w2, g3, g4)


def _inproj_bwd(parts, x, dx1, g1, w_in, tm, comm=None):
    T = x.shape[0]
    widths = [p.shape[1] for p in parts]
    offs = [sum(widths[:i]) for i in range(len(widths) + 1)]
    assert offs[-1] == IN_W

    def body(*refs):
        n = len(parts)
        prefs = refs[:n]
        x_ref, dx1_ref, g_ref, w_ref, dx_ref, dp_ref, dg_ref = refs[n:]

        @pl.when(pl.program_id(0) == 0)
        def _():
            dg_ref[...] = jnp.zeros_like(dg_ref)

        for i in range(n):
            dp_ref[:, offs[i]:offs[i + 1]] = prefs[i][...]
        dh = _dot_nt(dp_ref[...], w_ref[...])
        xhat, r = _rms_hat(x_ref[...])
        dxn, dg = _rms_bwd(xhat, r, g_ref[...], dh)
        dg_ref[...] += dg
        dx_ref[...] = dx1_ref[...] + dxn

    sd = jax.ShapeDtypeStruct
    return _call(
        body, (*parts, x, dx1, g1, w_in), name="inproj_bwd", grid=(T // tm,),
        in_specs=[_rows(tm, w) for w in widths] + [_rows(tm, D), _rows(tm, D), _const((1, D)), _resident((D, IN_W))],
        out_specs=[_rows(tm, D), _rows(tm, IN_W), _const((1, D))],
        out_shape=[sd((T, D), F32), sd((T, IN_W), BF16), sd((1, D), F32)], sem=("arbitrary",), comm=comm)


def _wgrad(a, g, tn, tm, name, comm=None):
    T, K = a.shape
    N = g.shape[1]

    def body(a_ref, g_ref, o_ref):
        @pl.when(pl.program_id(1) == 0)
        def _():
            o_ref[...] = jnp.zeros_like(o_ref)

        o_ref[...] += _dot_tn(a_ref[...], g_ref[...])

    return _call(
        body, (a, g), name=name, grid=(N // tn, T // tm),
        in_specs=[pl.BlockSpec((tm, K), lambda j, t: (t, 0)), pl.BlockSpec((tm, tn), lambda j, t: (t, j))],
        out_specs=pl.BlockSpec((K, tn), lambda j, t: (0, j)),
        out_shape=jax.ShapeDtypeStruct((K, N), F32), sem=("parallel", "arbitrary"), comm=comm)


def _adamw(ws, gs, ms, vs, trs, name):
    n = len(ws)
    walk = _Walk(w.shape[0] // tr for w, tr in zip(ws, trs))
    bc1 = 1.0 / (1.0 - B1 ** STEP)
    bc2 = 1.0 / (1.0 - B2 ** STEP)

    def body(*refs):
        i = pl.program_id(0)
        for k in range(n):
            mine = tuple(refs[j * n + k] for j in range(8))

            @pl.when(walk.mine(k, i))
            def _(mine=mine):
                w_ref, g_ref, m_ref, v_ref, go_ref, d_ref, nm_ref, nv_ref = mine
                g = g_ref[...]
                go_ref[...] = g
                m = B1 * m_ref[...] + (1.0 - B1) * g
                v = B2 * v_ref[...] + (1.0 - B2) * (g * g)
                nm_ref[...] = m
                nv_ref[...] = v
                d_ref[...] = -LR * ((m * bc1) / (jnp.sqrt(v * bc2) + AEPS) + WD * w_ref[...])

    def spec(k):
        return pl.BlockSpec((trs[k], ws[k].shape[1]), lambda i: (walk.tile(k, i), 0))

    specs = [spec(k) for k in range(n)]
    res = pl.pallas_call(
        body, name=name, grid=(walk.steps,), in_specs=specs * 4, out_specs=specs * 4,
        out_shape=[jax.ShapeDtypeStruct(w.shape, F32) for w in ws] * 4,
        compiler_params=_cparams(("arbitrary",)),
    )(*ws, *gs, *ms, *vs)
    return [tuple(res[j * n + k] for j in range(4)) for k in range(n)]


BIG = (("col", (D, IN_W)), ("row", (D, D)), ("row", (D, D)), ("row", (D, D)), ("col", (D, DFF)), ("row", (DFF, D)))
NBIG = len(BIG)
ANY = pl.BlockSpec(memory_space=pl.ANY)


def _shard_shape(kind, shape):
    R, C = shape
    return (R, C // 4) if kind == "col" else (R // 4, C)


def _half_shape(kind, shape):
    R, C = shape
    return (R // 2, C) if kind == "col" else (R, C // 2)


def _piece_shape(kind, shape):
    R, C = shape
    return (R // 2, C // 4) if kind == "col" else (R // 4, C // 2)


def _own_region(ref, kind, shape, s):
    R, C = shape
    return ref.at[:, pl.ds(s * (C // 4), C // 4)] if kind == "col" else ref.at[pl.ds(s * (R // 4), R // 4), :]


def _ag_region(ref, kind, shape, s, hc):
    R, C = shape
    if kind == "col":
        return ref.at[pl.ds(hc * (R // 2), R // 2), pl.ds(s * (C // 4), C // 4)]
    return ref.at[pl.ds(s * (R // 4) + hc * (R // 8), R // 8), :]


def _ag_shard_half(ref, kind, shape, hc):
    R, C = shape
    return ref.at[pl.ds(hc * (R // 2), R // 2), :] if kind == "col" else ref.at[pl.ds(hc * (R // 8), R // 8), :]


def _grad_half(ref, kind, shape, hc):
    R, C = shape
    return ref.at[pl.ds(hc * (R // 2), R // 2), :] if kind == "col" else ref.at[:, pl.ds(hc * (C // 2), C // 2)]


def _half_piece(ref, kind, shape, s):
    R, C = shape
    return ref.at[:, pl.ds(s * (C // 4), C // 4)] if kind == "col" else ref.at[pl.ds(s * (R // 4), R // 4), :]


def _place():
    x, y, c = lax.axis_index("x"), lax.axis_index("y"), lax.axis_index("c")
    chips = [(1 - x, y), (x, 1 - y), (1 - x, 1 - y)]
    return x, y, c, chips


def _rcopy(src, dst, ssem, rsem, dev):
    return pltpu.make_async_remote_copy(src_ref=src, dst_ref=dst, send_sem=ssem, recv_sem=rsem,
                                        device_id=dev, device_id_type=MESH)


def _dma_sems(n):
    return pltpu.SemaphoreType.DMA((n,))


def _x_gather_ici(shards, ws):
    n = len(ws)
    specs = [BIG[w] for w in ws]

    def place():
        x, y, c, chips = _place()
        return c, chips, 2 * x + y

    def sends(sh, full, sc):
        c, chips, me_s = place()
        return [_rcopy(_ag_shard_half(sh[i], kind, shape, c), _ag_region(full[i], kind, shape, me_s, c),
                       sc[0].at[3 * i + j], sc[1].at[3 * i + j], (cx, cy, c))
                for i, (kind, shape) in enumerate(specs) for j, (cx, cy) in enumerate(chips)]

    def start(sh, full, sc):
        for i in range(n):
            pltpu.make_async_copy(sh[i], sc[4 + i], sc[2].at[i]).start()
        for cp in sends(sh, full, sc):
            cp.start()

    def finish(sh, full, sc):
        c, chips, me_s = place()
        stores = []
        for i, (kind, shape) in enumerate(specs):
            pltpu.make_async_copy(sh[i], sc[4 + i], sc[2].at[i]).wait()
            st = pltpu.make_async_copy(sc[4 + i], _own_region(full[i], kind, shape, me_s), sc[3].at[i])
            st.start()
            stores.append(st)
        for i, (kind, shape) in enumerate(specs):
            for j, (cx, cy) in enumerate(chips):
                reg = _ag_region(full[i], kind, shape, 2 * cx + cy, c)
                _rcopy(reg, reg, sc[0].at[3 * i + j], sc[1].at[3 * i + j], (cx, cy, c)).wait_recv()
        for cp in sends(sh, full, sc):
            cp.wait_send()
        for st in stores:
            st.wait()

    return _Exchange(
        shards, [jax.ShapeDtypeStruct(shape, BF16) for _, shape in specs], {},
        [_dma_sems(3 * n), _dma_sems(3 * n), _dma_sems(n), _dma_sems(n)]
        + [pltpu.VMEM(_shard_shape(k, s), BF16) for k, s in specs], start, finish)


def _x_gather_d2d(wholes, ws):
    specs = [BIG[w] for w in ws]
    n = len(ws)

    def copies(full, sc, mine):
        x, y, c, chips = _place()
        hc = c if mine else 1 - c
        return [_rcopy(reg, reg, sc[0].at[3 * i + j], sc[1].at[3 * i + j], (x, y, 1 - c))
                for i, (kind, shape) in enumerate(specs) for j, (cx, cy) in enumerate(chips)
                for reg in [_ag_region(full[i], kind, shape, 2 * cx + cy, hc)]]

    def start(_, full, sc):
        for cp in copies(full, sc, True):
            cp.start()

    def finish(_, full, sc):
        for cp in copies(full, sc, False):
            cp.wait_recv()
        for cp in copies(full, sc, True):
            cp.wait_send()

    return _Exchange(wholes, [jax.ShapeDtypeStruct(shape, BF16) for _, shape in specs], {i: i for i in range(n)},
                     [_dma_sems(3 * n), _dma_sems(3 * n)], start, finish)


def _x_grads_sibling(grads, ws):
    specs = [BIG[w] for w in ws]
    n = len(ws)

    def copies(g, got, sc):
        x, y, c, _ = _place()
        return [_rcopy(_grad_half(g[i], kind, shape, 1 - c), got[i], sc[0].at[i], sc[1].at[i], (x, y, 1 - c))
                for i, (kind, shape) in enumerate(specs)]

    def start(g, got, sc):
        for cp in copies(g, got, sc):
            cp.start()

    def finish(g, got, sc):
        for cp in copies(g, got, sc):
            cp.wait_recv()
        for cp in copies(g, got, sc):
            cp.wait_send()

    return _Exchange(grads, [jax.ShapeDtypeStruct(_half_shape(k, s), F32) for k, s in specs], {},
                     [_dma_sems(n), _dma_sems(n)], start, finish)


def _x_grads_chips(sums_bf, ws):
    specs = [BIG[w] for w in ws]
    n = len(ws)

    def copies(s16, got, sc):
        x, y, c, chips = _place()
        return [_rcopy(_half_piece(s16[i], kind, shape, 2 * cx + cy), got[i].at[j],
                       sc[0].at[3 * i + j], sc[1].at[3 * i + j], (cx, cy, c))
                for i, (kind, shape) in enumerate(specs) for j, (cx, cy) in enumerate(chips)]

    def start(s16, got, sc):
        for cp in copies(s16, got, sc):
            cp.start()

    def finish(s16, got, sc):
        for cp in copies(s16, got, sc):
            cp.wait_recv()
        for cp in copies(s16, got, sc):
            cp.wait_send()

    return _Exchange(sums_bf, [jax.ShapeDtypeStruct((3,) + _piece_shape(k, s), BF16) for k, s in specs], {},
                     [_dma_sems(3 * n), _dma_sems(3 * n)], start, finish)


def _shard_half(ref, kind, shape, hc):
    sr, sc = _shard_shape(kind, shape)
    return ref.at[pl.ds(hc * (sr // 2), sr // 2), :] if kind == "col" else ref.at[:, pl.ds(hc * (sc // 2), sc // 2)]


def _x_grads_share(shard_grads, ws):
    specs = [BIG[w] for w in ws]
    n = len(ws)

    def copies(g, sc, mine):
        x, y, c, _ = _place()
        hc = c if mine else 1 - c
        return [_rcopy(part, part, sc[0].at[i], sc[1].at[i], (x, y, 1 - c))
                for i, (kind, shape) in enumerate(specs) for part in [_shard_half(g[i], kind, shape, hc)]]

    def start(_, g, sc):
        for cp in copies(g, sc, True):
            cp.start()

    def finish(_, g, sc):
        for cp in copies(g, sc, False):
            cp.wait_recv()
        for cp in copies(g, sc, True):
            cp.wait_send()

    return _Exchange(shard_grads, [jax.ShapeDtypeStruct(_shard_shape(k, s), F32) for k, s in specs],
                     {i: i for i in range(n)}, [_dma_sems(n), _dma_sems(n)], start, finish)


ADD_BLOCK_BYTES = 4 * 1024 * 1024


def _add_rows(rows, cols, n_arrays):
    limit = ADD_BLOCK_BYTES // (1 if n_arrays == 1 else 4)
    r = rows
    while r > 64 and r * cols * 4 > limit:
        r //= 2
    return r


class _Walk:
    def __init__(self, tiles):
        self.tiles = list(tiles)
        self.starts = [sum(self.tiles[:k]) for k in range(len(self.tiles))]
        self.steps = sum(self.tiles)

    def tile(self, k, i):
        return jnp.clip(i - self.starts[k], 0, self.tiles[k] - 1)

    def mine(self, k, i):
        return (i >= self.starts[k]) & (i < self.starts[k] + self.tiles[k])


def _add_halves(place, gs, gots, kinds, name):
    n = len(gs)
    halves = [_half_shape(kind, g.shape) for g, kind in zip(gs, kinds)]
    rows = [_add_rows(hr, hc, n) for hr, hc in halves]
    walk = _Walk(hr // r for (hr, _), r in zip(halves, rows))

    def body(p_ref, *refs):
        i = pl.program_id(0)
        for k in range(n):
            g_ref, b_ref, s_ref, sb_ref = (refs[j * n + k] for j in range(4))

            @pl.when(walk.mine(k, i))
            def _(g_ref=g_ref, b_ref=b_ref, s_ref=s_ref, sb_ref=sb_ref):
                s = g_ref[...] + b_ref[...]
                s_ref[...] = s
                sb_ref[...] = s.astype(BF16)

    def g_spec(k):
        if kinds[k] == "col":
            return pl.BlockSpec((rows[k], gs[k].shape[1]), lambda i, p: (p[0] * walk.tiles[k] + walk.tile(k, i), 0))
        return pl.BlockSpec((rows[k], halves[k][1]), lambda i, p: (walk.tile(k, i), p[0]))

    def spec(k):
        return pl.BlockSpec((rows[k], halves[k][1]), lambda i, p: (walk.tile(k, i), 0))

    specs = [spec(k) for k in range(n)]
    res = pl.pallas_call(
        body, name=name,
        grid_spec=pltpu.PrefetchScalarGridSpec(num_scalar_prefetch=1, grid=(walk.steps,),
                                               in_specs=[g_spec(k) for k in range(n)] + specs, out_specs=specs + specs),
        out_shape=[jax.ShapeDtypeStruct(h, F32) for h in halves] + [jax.ShapeDtypeStruct(h, BF16) for h in halves],
        compiler_params=_cparams(("arbitrary",)),
    )(place, *gs, *gots)
    return [(res[k], res[n + k]) for k in range(n)]


def _add_pieces(place, halves, gots, specs_big, name):
    n = len(halves)
    pieces = [_piece_shape(kind, shape) for kind, shape in specs_big]
    rows = [_add_rows(pr, pc, n) for pr, pc in pieces]
    walk = _Walk(pr // r for (pr, _), r in zip(pieces, rows))

    def body(p_ref, *refs):
        i = pl.program_id(0)
        for k in range(n):
            m_ref, g_ref, o_ref = (refs[j * n + k] for j in range(3))

            @pl.when(walk.mine(k, i))
            def _(m_ref=m_ref, g_ref=g_ref, o_ref=o_ref):
                acc = m_ref[...]
                for j in range(3):
                    acc = acc + g_ref[j].astype(F32)
                o_ref[...] = acc

    def m_spec(k):
        if specs_big[k][0] == "col":
            return pl.BlockSpec((rows[k], pieces[k][1]), lambda i, p: (walk.tile(k, i), p[1]))
        return pl.BlockSpec((rows[k], pieces[k][1]), lambda i, p: (p[1] * walk.tiles[k] + walk.tile(k, i), 0))

    def got_spec(k):
        return pl.BlockSpec((3, rows[k], pieces[k][1]), lambda i, p: (0, walk.tile(k, i), 0))

    def o_spec(k):
        if specs_big[k][0] == "col":
            return pl.BlockSpec((rows[k], pieces[k][1]), lambda i, p: (p[0] * walk.tiles[k] + walk.tile(k, i), 0))
        return pl.BlockSpec((rows[k], pieces[k][1]), lambda i, p: (walk.tile(k, i), p[0]))

    return pl.pallas_call(
        body, name=name,
        grid_spec=pltpu.PrefetchScalarGridSpec(
            num_scalar_prefetch=1, grid=(walk.steps,),
            in_specs=[m_spec(k) for k in range(n)] + [got_spec(k) for k in range(n)],
            out_specs=[o_spec(k) for k in range(n)]),
        out_shape=[jax.ShapeDtypeStruct(_shard_shape(kind, shape), F32) for kind, shape in specs_big],
        compiler_params=_cparams(("arbitrary",)),
    )(place, *halves, *gots)


SMALL_ROWS = 1024 + 8 * 8 + 8


def _x_small_all_reduce(p):
    def parts(p_ref, sc):
        slots, ssem, rsem = sc[0], sc[2], sc[3]
        x, y, c = lax.axis_index("x"), lax.axis_index("y"), lax.axis_index("c")
        me = 4 * x + 2 * y + c
        out = []
        for r in range(1, 8):
            bx, by, bc = (r >> 2) & 1, (r >> 1) & 1, r & 1
            tgt = (1 - x if bx else x, 1 - y if by else y, 1 - c if bc else c)
            send = _rcopy(p_ref, slots.at[me], ssem.at[r - 1], rsem.at[r - 1], tgt)
            src = 4 * tgt[0] + 2 * tgt[1] + tgt[2]
            recv = _rcopy(p_ref, slots.at[src], ssem.at[r - 1], rsem.at[r - 1], tgt)
            out.append((send, recv))
        return me, out

    def start(ins, outs, sc):
        me, cps = parts(ins[0], sc)
        pltpu.make_async_copy(ins[0], sc[0].at[me], sc[4].at[0]).start()
        for send, _ in cps:
            send.start()

    def finish(ins, outs, sc):
        me, cps = parts(ins[0], sc)
        pltpu.make_async_copy(ins[0], sc[0].at[me], sc[4].at[0]).wait()
        for _, recv in cps:
            recv.wait_recv()
        acc = sc[0][0]
        for d in range(1, 8):
            acc = acc + sc[0][d]
        sc[1][...] = acc
        back = pltpu.make_async_copy(sc[1], outs[0], sc[4].at[1])
        back.start()
        for send, _ in cps:
            send.wait_send()
        back.wait()

    return _Exchange([p], [jax.ShapeDtypeStruct((SMALL_ROWS, CH), F32)], {},
                     [pltpu.VMEM((8, SMALL_ROWS, CH), F32), pltpu.VMEM((SMALL_ROWS, CH), F32), _dma_sems(7), _dma_sems(7),
                      _dma_sems(2)], start, finish)


def _rope_tables(positions, comm=None):
    T = positions.shape[0]
    inv_freq = 500000.0 ** (-jnp.arange(0, 2 * ROPE_HALF, 2, dtype=F32) / (2 * ROPE_HALF))
    head = jnp.concatenate([inv_freq, inv_freq, jnp.zeros((HD - 2 * ROPE_HALF,), F32)])
    lane_freq = jnp.concatenate([head, head])[None, :]
    pos = jnp.broadcast_to(positions.astype(F32)[:, None], (T, CH))
    tm = min(1024, T)

    def body(p_ref, f_ref, c_ref, s1_ref, s2_ref):
        ang = p_ref[...] * f_ref[...]
        sin = jnp.sin(ang)
        first = (lax.broadcasted_iota(jnp.int32, ang.shape, 1) % HD) < ROPE_HALF
        c_ref[...] = jnp.cos(ang)
        s1_ref[...] = jnp.where(first, -sin, 0.0)
        s2_ref[...] = jnp.where(first, 0.0, sin)

    return _call(body, (pos, lane_freq), name="rope_tables", grid=(T // tm,),
                 in_specs=[_rows(tm, CH), _const((1, CH))], out_specs=[_rows(tm, CH)] * 3,
                 out_shape=[jax.ShapeDtypeStruct((T, CH), F32)] * 3, sem=("parallel",), comm=comm)


BIG_NAMES = ("w_in", "w_a", "w_b", "w_o", "w_ff_in", "w_ff_out")
SMALL_NAMES = ("w_spatial", "ln_v_gain", "ln_v_bias", "b_spatial", "sinks", "norm_mix_pre", "norm_mix_post",
               "norm_ff_pre", "norm_ff_post")
WEIGHT_ORDER = ("w_in", "ln_v_gain", "ln_v_bias", "w_spatial", "b_spatial", "sinks", "w_a", "w_b", "w_o",
                "norm_mix_pre", "norm_mix_post", "w_ff_in", "w_ff_out", "norm_ff_pre", "norm_ff_post")


def _pack_small(d, loss_sums=None):
    parts = []
    for n in SMALL_NAMES:
        flat = d[n].reshape(-1)
        pad = (-flat.shape[0]) % (8 * CH)
        parts.append(jnp.pad(flat, (0, pad)).reshape(-1, CH))
    parts.append(jnp.zeros((8, CH), F32) if loss_sums is None else loss_sums.reshape(8, CH))
    return jnp.concatenate(parts, axis=0)


def _unpack_small(p, like):
    out, row = {}, 0
    for n in SMALL_NAMES:
        size = like[n].size
        rows = -(-size // (8 * CH)) * 8
        out[n] = p[row:row + rows].reshape(-1)[:size].reshape(like[n].shape)
        row += rows
    return out


def kernel(x, positions, w_in, ln_v_gain, ln_v_bias, w_spatial, b_spatial, sinks, w_a, w_b, w_o, norm_mix_pre, norm_mix_post, w_ff_in, w_ff_out, norm_ff_pre, norm_ff_post, loss_target, m_w_in, m_ln_v_gain, m_ln_v_bias, m_w_spatial, m_b_spatial, m_sinks, m_w_a, m_w_b, m_w_o, m_norm_mix_pre, m_norm_mix_post, m_w_ff_in, m_w_ff_out, m_norm_ff_pre, m_norm_ff_post, v_w_in, v_ln_v_gain, v_ln_v_bias, v_w_spatial, v_b_spatial, v_sinks, v_w_a, v_w_b, v_w_o, v_norm_mix_pre, v_norm_mix_post, v_w_ff_in, v_w_ff_out, v_norm_ff_pre, v_norm_ff_post):
    w = dict(w_in=w_in, ln_v_gain=ln_v_gain, ln_v_bias=ln_v_bias, w_spatial=w_spatial, b_spatial=b_spatial, sinks=sinks,
             w_a=w_a, w_b=w_b, w_o=w_o, norm_mix_pre=norm_mix_pre, norm_mix_post=norm_mix_post, w_ff_in=w_ff_in,
             w_ff_out=w_ff_out, norm_ff_pre=norm_ff_pre, norm_ff_post=norm_ff_post)
    m = dict(w_in=m_w_in, ln_v_gain=m_ln_v_gain, ln_v_bias=m_ln_v_bias, w_spatial=m_w_spatial, b_spatial=m_b_spatial,
             sinks=m_sinks, w_a=m_w_a, w_b=m_w_b, w_o=m_w_o, norm_mix_pre=m_norm_mix_pre, norm_mix_post=m_norm_mix_post,
             w_ff_in=m_w_ff_in, w_ff_out=m_w_ff_out, norm_ff_pre=m_norm_ff_pre, norm_ff_post=m_norm_ff_post)
    v = dict(w_in=v_w_in, ln_v_gain=v_ln_v_gain, ln_v_bias=v_ln_v_bias, w_spatial=v_w_spatial, b_spatial=v_b_spatial,
             sinks=v_sinks, w_a=v_w_a, w_b=v_w_b, w_o=v_w_o, norm_mix_pre=v_norm_mix_pre, norm_mix_post=v_norm_mix_post,
             w_ff_in=v_w_ff_in, w_ff_out=v_w_ff_out, norm_ff_pre=v_norm_ff_pre, norm_ff_post=v_norm_ff_post)

    FIRST, REST = (0,), tuple(range(1, NBIG))
    shards = [w[n][0].astype(BF16) for n in BIG_NAMES]
    place = jnp.stack([lax.axis_index("c"), 2 * lax.axis_index("x") + lax.axis_index("y")]).astype(jnp.int32)
    xs, target = x[0], loss_target[0]
    T = xs.shape[0]
    wtm, wtm2 = min(1024, T), min(2048, T)
    g1, g2, g3, g4 = norm_mix_pre, norm_mix_post, norm_ff_pre, norm_ff_post
    w_sp, snk = w_spatial[0], sinks[0]
    MIX, FF = (1, 2, 3), (4, 5)
    bfull = jnp.repeat(b_spatial[0].T, CH, axis=1)

    def reduce_tail(ws, grads, got):
        tag = "_".join(BIG_NAMES[k] for k in ws)
        sums = _add_halves(place, grads, got, [BIG[k][0] for k in ws], name="grad_add_sibling_" + tag)
        return sums, _x_grads_chips([s[1] for s in sums], ws)

    def reduce_end(ws, sums, pieces):
        tag = "_".join(BIG_NAMES[k] for k in ws)
        return _add_pieces(place, [s[0] for s in sums], pieces, [BIG[k] for k in ws], name="grad_add_chips_" + tag)

    (rc, rs1, rs2), w_in_part = _rope_tables(positions[0], comm=_x_gather_ici(shards[:1], FIRST))
    w_in_b = _run(_x_gather_d2d(w_in_part, FIRST), "gather_w_in_d2d")[0]
    (h, u, vs, q, k, va, ga, gb), ff_part = _inproj(xs, g1, w_in_b, rc, rs1, rs2, tm=512, comm=_x_gather_ici(shards[4:], FF))
    a, mix_part = _sgu_fwd(u, vs, ln_v_gain, ln_v_bias, w_sp, bfull, tm=512, comm=_x_gather_ici(shards[1:4], MIX))
    att, rest = _attn_fwd(q, k, va, snk, comm=_both(_x_gather_d2d(mix_part, MIX), _x_gather_d2d(ff_part, FF)))
    w_a_b, w_b_b, w_o_b, w_ff_in_b, w_ff_out_b = rest
    pa, pb, merged, mix, x1 = _merge_fwd(a, att, ga, gb, xs, w_a_b, w_b_b, w_o_b, g2, tm=512)
    hf, f2, dff, df1, dx1, lsum, dg3, dg4 = _ffn(x1, target, w_ff_in_b, w_ff_out_b, g3, g4, tm=256)

    dw_ff_out, _ = _wgrad(f2, dff, tn=1024, tm=512, name="wgrad_ff_out")
    dw_ff_in, _ = _wgrad(hf, df1, tn=2048, tm=wtm2, name="wgrad_ff_in")
    (dga, dgb, da, datt, dg2, dw_a, dw_b, dw_o), _ = _merge_bwd(
        dx1, mix, ga, gb, pa, pb, a, att, merged, w_a_b, w_b_b, w_o_b, g2, tm=256)
    grads_rest = [dw_a, dw_b, dw_o, dw_ff_in, dw_ff_out]
    (du, dvs, dws, dbs, dlg, dlb), got_rest = _sgu_bwd(
        u, vs, da, ln_v_gain, ln_v_bias, w_sp, bfull, tm=512, comm=_x_grads_sibling(grads_rest, REST))
    sums_rest, to_chips = reduce_tail(REST, grads_rest, got_rest)
    (dq, dk, dva, dsk), pieces_rest = _attn_bwd(q, k, va, datt, snk, rc, rs1, rs2, comm=to_chips)
    partial_rest = reduce_end(REST, sums_rest, pieces_rest)
    (dx, dproj, dg1), _ = _inproj_bwd([du, dvs, dq, dk, dva, dga, dgb], xs, dx1, g1, w_in_b, tm=512)
    small = dict(ln_v_gain=dlg, ln_v_bias=dlb, w_spatial=dws, b_spatial=dbs, sinks=dsk[:, :NQ],
                 norm_mix_pre=dg1, norm_mix_post=dg2, norm_ff_pre=dg3, norm_ff_post=dg4)
    dw_in, (gs, *shard_rest) = _wgrad(
        h, dproj, tn=IN_W // 2, tm=wtm, name="wgrad_in",
        comm=_both(_x_small_all_reduce(_pack_small(small, lsum)), _x_grads_share(partial_rest, REST)))
    got_in = _run(_x_grads_sibling([dw_in], FIRST), "grads_in_to_sibling")
    sums_in, to_chips = reduce_tail(FIRST, [dw_in], got_in)
    partial_in = reduce_end(FIRST, sums_in, _run(to_chips, "grads_in_to_chips"))
    g_in = _run(_x_grads_share(partial_in, FIRST), "grads_in_share")[0]

    loss = 0.5 * jnp.sum(gs[SMALL_ROWS - 8:]) / D
    grad, delta, new_m, new_v = {}, {}, {}, {}
    for n, g in zip(BIG_NAMES, [g_in] + list(shard_rest)):
        (g_, d_, m_, v_), = _adamw([w[n][0]], [g], [m[n][0]], [v[n][0]], [256], name="adamw_" + n)
        grad[n], delta[n], new_m[n], new_v[n] = g_[None], d_[None], m_[None], v_[None]
    (gs, ds, ms, vs), = _adamw([_pack_small(w)], [gs], [_pack_small(m)], [_pack_small(v)], [SMALL_ROWS], name="adamw_small")
    for packed, dst in ((gs, grad), (ds, delta), (ms, new_m), (vs, new_v)):
        dst.update(_unpack_small(packed, w))

    outs = [loss, dx[None]]
    for group in (grad, delta, new_m, new_v):
        outs.extend(group[n] for n in WEIGHT_ORDER)
    return tuple(outs)
```

```python
import functools

import jax
import jax.numpy as jnp
from jax import lax
from jax.experimental import pallas as pl
from jax.experimental.pallas import tpu as pltpu

F32 = jnp.float32
BF16 = jnp.bfloat16

D = 1024
CH = 128
NG = 8
HD = 64
NQ = 16
NKV = 4
KVW = NKV * HD
DFF = 4 * D
EPS = 1e-6
IN_W = 5632
SEG = (0, 1024, 2048, 3072, 3328, 3584, 4608, 5632)
ROPE_HALF = 8
Q_SCALE = HD ** -0.5

LR, B1, B2, AEPS, WD, STEP = 0.001, 0.9, 0.999, 1e-08, 0.01, 10

VMEM_PHYSICAL = 64 * 1024 * 1024
VMEM_LIMIT = 60 * 1024 * 1024
MESH = pl.DeviceIdType.MESH

_GELU_C0 = 0.7978845608028654
_GELU_C1 = 0.044715


def _cparams(sem=None):
    kw = dict(vmem_limit_bytes=VMEM_LIMIT)
    if sem is not None:
        kw["dimension_semantics"] = sem
    return pltpu.CompilerParams(**kw)


def _resident(shape):
    nd = len(shape)
    return pl.BlockSpec(shape, lambda *_: (0,) * nd, pipeline_mode=pl.Buffered(1))


def _const(shape):
    nd = len(shape)
    return pl.BlockSpec(shape, lambda *_: (0,) * nd)


def _rows(tm, w):
    return pl.BlockSpec((tm, w), lambda i: (i, 0))


class _Exchange:
    def __init__(self, ins, outs, aliases, scratch, start, finish):
        self.ins, self.outs, self.aliases, self.scratch = list(ins), list(outs), dict(aliases), list(scratch)
        self.start, self.finish = start, finish


def _both(a, b):
    na, ma, sa = len(a.ins), len(a.outs), len(a.scratch)

    def start(ci, co, cs):
        a.start(ci[:na], co[:ma], cs[:sa])
        b.start(ci[na:], co[ma:], cs[sa:])

    def finish(ci, co, cs):
        a.finish(ci[:na], co[:ma], cs[:sa])
        b.finish(ci[na:], co[ma:], cs[sa:])

    aliases = {**a.aliases, **{na + i: ma + j for i, j in b.aliases.items()}}
    return _Exchange(a.ins + b.ins, a.outs + b.outs, aliases, a.scratch + b.scratch, start, finish)


def _call(body, args, *, name, grid, in_specs, out_specs, out_shape, scratch_shapes=(), sem=None, comm=None):
    single = not isinstance(out_shape, (list, tuple))
    out_shape = [out_shape] if single else list(out_shape)
    out_specs = [out_specs] if single else list(out_specs)
    if comm is None:
        res = pl.pallas_call(body, name=name, grid=grid, in_specs=list(in_specs), out_specs=out_specs,
                             out_shape=out_shape, scratch_shapes=list(scratch_shapes),
                             compiler_params=_cparams(sem))(*args)
        return (res[0] if single else res), []
    n_in, n_out, n_scr = len(args), len(out_shape), len(scratch_shapes)
    nci, nco = len(comm.ins), len(comm.outs)
    steps = 1
    for g in grid:
        steps *= g

    def hosted(*refs):
        a, ci = refs[:n_in], refs[n_in:n_in + nci]
        o, co = refs[n_in + nci:n_in + nci + n_out], refs[n_in + nci + n_out:n_in + nci + n_out + nco]
        rest = refs[n_in + nci + n_out + nco:]
        scr, cs = rest[:n_scr], rest[n_scr:]
        step = pl.program_id(0)
        for d in range(1, len(grid)):
            step = step * grid[d] + pl.program_id(d)

        @pl.when(step == 0)
        def _():
            comm.start(ci, co, cs)

        body(*a, *o, *scr)

        @pl.when(step == steps - 1)
        def _():
            comm.finish(ci, co, cs)

    res = pl.pallas_call(
        hosted, name=name, grid=grid, in_specs=list(in_specs) + [ANY] * nci, out_specs=out_specs + [ANY] * nco,
        out_shape=out_shape + comm.outs, scratch_shapes=list(scratch_shapes) + comm.scratch,
        input_output_aliases={n_in + i: n_out + j for i, j in comm.aliases.items()},
        compiler_params=_cparams(("arbitrary",) * len(grid)),
    )(*args, *comm.ins)
    own = res[:n_out]
    return (own[0] if single else own), list(res[n_out:])


def _run(comm, name):
    nci = len(comm.ins)

    def body(*refs):
        ci, co, cs = refs[:nci], refs[nci:nci + len(comm.outs)], refs[nci + len(comm.outs):]
        comm.start(ci, co, cs)
        comm.finish(ci, co, cs)

    return pl.pallas_call(
        body, name=name, in_specs=[ANY] * nci, out_specs=[ANY] * len(comm.outs), out_shape=comm.outs,
        scratch_shapes=comm.scratch, input_output_aliases=comm.aliases,
        compiler_params=pltpu.CompilerParams(vmem_limit_bytes=VMEM_LIMIT),
    )(*comm.ins)


def _gelu(x):
    x2 = x * x
    t = jnp.tanh(x * (_GELU_C0 + (_GELU_C0 * _GELU_C1) * x2))
    hx = 0.5 * x
    return hx + hx * t, (t, x2, hx)


def _gelu_grad(parts):
    t, x2, hx = parts
    return (0.5 + 0.5 * t) + hx * (1.0 - t * t) * (_GELU_C0 + (3.0 * _GELU_C0 * _GELU_C1) * x2)


def _sigmoid(x):
    return 1.0 / (1.0 + jnp.exp(-x))


def _rms_hat(x):
    r = lax.rsqrt(jnp.mean(x * x, axis=-1, keepdims=True) + EPS)
    return x * r, r


def _rms_bwd(xhat, r, g, dout):
    dg = jnp.sum(dout * xhat, axis=0, keepdims=True)
    dy = dout * g
    dx = r * (dy - xhat * jnp.mean(dy * xhat, axis=-1, keepdims=True))
    return dx, dg


def _dot(a, b):
    return jnp.dot(a, b, preferred_element_type=F32)


def _dot_nt(a, b):
    return lax.dot_general(a, b, (((1,), (1,)), ((), ())), preferred_element_type=F32)


def _dot_tn(a, b):
    return lax.dot_general(a, b, (((0,), (0,)), ((), ())), preferred_element_type=F32)


def _rope(blk, c, s1, s2):
    return blk * c + pltpu.roll(blk, CH - ROPE_HALF, 1) * s1 + pltpu.roll(blk, ROPE_HALF, 1) * s2


def _rope_t(blk, c, s1, s2):
    return blk * c + pltpu.roll(blk * s1, ROPE_HALF, 1) + pltpu.roll(blk * s2, CH - ROPE_HALF, 1)


def _inproj(x, g1, w_in, rc, rs1, rs2, tm, comm=None):
    T = x.shape[0]

    def body(x_ref, g_ref, w_ref, c_ref, s1_ref, s2_ref,
             h_ref, u_ref, v_ref, q_ref, k_ref, va_ref, ga_ref, gb_ref):
        xhat, _ = _rms_hat(x_ref[...])
        h = (xhat * g_ref[...]).astype(BF16)
        h_ref[...] = h
        uv = _dot(h, w_ref[:, SEG[0]:SEG[2]])
        u_ref[...] = uv[:, :D]
        v_ref[...] = uv[:, D:]
        c, s1, s2 = c_ref[...], s1_ref[...], s2_ref[...]
        qkv = _dot(h, w_ref[:, SEG[2]:SEG[5]])
        for p in range(D // CH):
            blk = _rope(qkv[:, CH * p:CH * (p + 1)], c, s1, s2) * Q_SCALE
            q_ref[:, CH * p:CH * (p + 1)] = blk.astype(BF16)
        for p in range(KVW // CH):
            k_ref[:, CH * p:CH * (p + 1)] = _rope(qkv[:, D + CH * p:D + CH * (p + 1)], c, s1, s2).astype(BF16)
        va_ref[...] = qkv[:, D + KVW:].astype(BF16)
        gates = _dot(h, w_ref[:, SEG[5]:SEG[7]]).astype(BF16)
        ga_ref[...] = gates[:, :D]
        gb_ref[...] = gates[:, D:]

    sd = jax.ShapeDtypeStruct
    return _call(
        body, (x, g1, w_in, rc, rs1, rs2), name="inproj_fwd", grid=(T // tm,),
        in_specs=[_rows(tm, D), _const((1, D)), _resident((D, IN_W)), _rows(tm, CH), _rows(tm, CH), _rows(tm, CH)],
        out_specs=[_rows(tm, D), _rows(tm, D), _rows(tm, D), _rows(tm, D), _rows(tm, KVW), _rows(tm, KVW),
                   _rows(tm, D), _rows(tm, D)],
        out_shape=[sd((T, D), BF16), sd((T, D), F32), sd((T, D), F32), sd((T, D), BF16), sd((T, KVW), BF16),
                   sd((T, KVW), BF16), sd((T, D), BF16), sd((T, D), BF16)],
        sem=("parallel",), comm=comm)


def _sgu_common(u, vs, lng, lnb, ws_ref, bfull):
    nc = u.shape[0] // CH
    ug, tu = _gelu(u)
    vg, tv = _gelu(vs)
    mu = jnp.mean(vg, axis=-1, keepdims=True)
    xc = vg - mu
    rstd = lax.rsqrt(jnp.mean(xc * xc, axis=-1, keepdims=True) + EPS)
    vhat = xc * rstd
    vnb = (vhat * lng + lnb).astype(BF16)
    tri = lax.broadcasted_iota(jnp.int32, (CH, CH), 0) >= lax.broadcasted_iota(jnp.int32, (CH, CH), 1)
    wts, rhss, mixed = [], [], []
    for g in range(NG):
        wt = jnp.where(tri, ws_ref[g], 0.0).astype(BF16)
        rhs = jnp.concatenate([vnb[CH * c:CH * (c + 1), CH * g:CH * (g + 1)] for c in range(nc)], axis=1)
        mix = _dot(wt, rhs)
        wts.append(wt)
        rhss.append(rhs)
        mixed.append([mix[:, CH * c:CH * (c + 1)] + bfull[:, CH * g:CH * (g + 1)] for c in range(nc)])
    return nc, ug, tu, tv, rstd, vhat, tri, wts, rhss, mixed


def _sgu_fwd(u, vs, lng, lnb, ws, bfull, tm, comm=None):
    T = u.shape[0]

    def body(u_ref, v_ref, lng_ref, lnb_ref, ws_ref, bf_ref, a_ref):
        nc, ug, _, _, _, _, _, _, _, mixed = _sgu_common(
            u_ref[...], v_ref[...], lng_ref[...], lnb_ref[...], ws_ref, bf_ref[...])
        mixed_all = jnp.concatenate(
            [jnp.concatenate([mixed[g][c] for g in range(NG)], axis=1) for c in range(nc)], axis=0)
        a_ref[...] = (ug * mixed_all).astype(BF16)

    return _call(
        body, (u, vs, lng, lnb, ws, bfull), name="sgu_fwd", grid=(T // tm,),
        in_specs=[_rows(tm, D), _rows(tm, D), _const((1, D)), _const((1, D)), _const((NG, CH, CH)), _const((CH, D))],
        out_specs=_rows(tm, D), out_shape=jax.ShapeDtypeStruct((T, D), BF16), sem=("parallel",), comm=comm)


def _sgu_bwd(u, vs, da, lng, lnb, ws, bfull, tm, comm=None):
    T = u.shape[0]
    nsteps = T // tm

    def body(u_ref, v_ref, da_ref, lng_ref, lnb_ref, ws_ref, bf_ref,
             du_ref, dv_ref, dws_ref, dbs_ref, dlg_ref, dlb_ref, db_ref):
        i = pl.program_id(0)
        u, vs, da, lng = u_ref[...], v_ref[...], da_ref[...], lng_ref[...]
        nc, ug, tu, tv, rstd, vhat, tri, wts, rhss, mixed = _sgu_common(u, vs, lng, lnb_ref[...], ws_ref, bf_ref[...])

        @pl.when(i == 0)
        def _():
            dws_ref[...] = jnp.zeros_like(dws_ref)
            db_ref[...] = jnp.zeros_like(db_ref)
            dlg_ref[...] = jnp.zeros_like(dlg_ref)
            dlb_ref[...] = jnp.zeros_like(dlb_ref)

        mixed_all = jnp.concatenate(
            [jnp.concatenate([mixed[g][c] for g in range(NG)], axis=1) for c in range(nc)], axis=0)
        du_ref[...] = (da * mixed_all * _gelu_grad(tu)).astype(BF16)
        dmixed = da * ug
        dvn_cols = []
        for g in range(NG):
            dmix = [dmixed[CH * c:CH * (c + 1), CH * g:CH * (g + 1)] for c in range(nc)]
            db_ref[:, CH * g:CH * (g + 1)] += functools.reduce(lambda a, b: a + b, dmix)
            dm = jnp.concatenate(dmix, axis=1).astype(BF16)
            dws_ref[g] += _dot_nt(dm, rhss[g])
            dvn_cols.append(_dot_tn(wts[g], dm))
        dvn = jnp.concatenate(
            [jnp.concatenate([dvn_cols[g][:, CH * c:CH * (c + 1)] for g in range(NG)], axis=1) for c in range(nc)],
            axis=0)
        dlg_ref[...] += jnp.sum(dvn * vhat, axis=0, keepdims=True)
        dlb_ref[...] += jnp.sum(dvn, axis=0, keepdims=True)
        dvh = dvn * lng
        dvg = rstd * (dvh - jnp.mean(dvh, axis=-1, keepdims=True)
                      - vhat * jnp.mean(dvh * vhat, axis=-1, keepdims=True))
        dv_ref[...] = (dvg * _gelu_grad(tv)).astype(BF16)

        @pl.when(i == nsteps - 1)
        def _():
            for g in range(NG):
                dws_ref[g] = jnp.where(tri, dws_ref[g], 0.0)
                dbs_ref[g:g + 1, :] = jnp.sum(db_ref[:, CH * g:CH * (g + 1)].T, axis=0, keepdims=True)

    sd = jax.ShapeDtypeStruct
    return _call(
        body, (u, vs, da, lng, lnb, ws, bfull), name="sgu_bwd", grid=(nsteps,),
        in_specs=[_rows(tm, D), _rows(tm, D), _rows(tm, D), _const((1, D)), _const((1, D)), _const((NG, CH, CH)),
                  _const((CH, D))],
        out_specs=[_rows(tm, D), _rows(tm, D), _const((NG, CH, CH)), _const((NG, CH)), _const((1, D)), _const((1, D))],
        out_shape=[sd((T, D), BF16), sd((T, D), BF16), sd((NG, CH, CH), F32), sd((NG, CH), F32), sd((1, D), F32),
                   sd((1, D), F32)],
        scratch_shapes=[pltpu.VMEM((CH, D), F32)], sem=("arbitrary",), comm=comm)


def _pair_layout(prev, cur, grp):
    j, half = grp // 2, grp % 2
    blk = jnp.concatenate([prev[:, CH * j:CH * (j + 1)], cur[:, CH * j:CH * (j + 1)]], axis=0).astype(F32)
    lo = lax.broadcasted_iota(jnp.int32, blk.shape, 1) < HD
    rolled = pltpu.roll(blk, HD, 1)
    even = jnp.where(lo, blk if half == 0 else rolled, 0.0)
    odd = jnp.where(lo, 0.0, rolled if half == 0 else blk)
    return jnp.concatenate([even, odd], axis=0).astype(BF16)


def _attn_mask(n):
    qi = lax.broadcasted_iota(jnp.int32, (CH, 2 * CH), 0)
    kc = lax.broadcasted_iota(jnp.int32, (CH, 2 * CH), 1)
    ok = (kc > qi) & (kc <= qi + CH) & ((kc >= CH) | (n > 0))
    return jnp.concatenate([ok, ok], axis=1)


def _softmax_sink(s, sink):
    m = jnp.maximum(jnp.max(s, axis=-1, keepdims=True), sink)
    p = jnp.exp(s - m)
    ps = jnp.exp(sink - m)
    inv = 1.0 / (jnp.sum(p, axis=-1, keepdims=True) + ps)
    return p * inv, ps * inv


QUERY_BLOCKS_PER_STEP = 2


def _attn_fwd(q, k, va, sinks, comm=None):
    T = q.shape[0]
    nblk = QUERY_BLOCKS_PER_STEP
    nsteps = T // (nblk * CH)
    npairs = D // CH

    def body(sk_ref, q_ref, kp_ref, kc_ref, vp_ref, vc_ref, o_ref):
        n = pl.program_id(0)
        even_lanes = lax.broadcasted_iota(jnp.int32, (CH, CH), 1) < HD
        ks = [kp_ref[...]] + [kc_ref[CH * b:CH * (b + 1)] for b in range(nblk)]
        vs = [vp_ref[...]] + [vc_ref[CH * b:CH * (b + 1)] for b in range(nblk)]
        masks = [_attn_mask(nblk * n)] + [_attn_mask(1)] * (nblk - 1)
        kks = [[_pair_layout(ks[b], ks[b + 1], grp) for grp in range(NKV)] for b in range(nblk)]
        vvs = [[_pair_layout(vs[b], vs[b + 1], grp) for grp in range(NKV)] for b in range(nblk)]
        work = [(b, p) for b in range(nblk) for p in range(npairs)]

        def scores(i):
            b, p = work[i]
            return _dot_nt(q_ref[CH * b:CH * (b + 1), CH * p:CH * (p + 1)], kks[b][p // 2])

        def unnormalised(s, sink):
            m = jnp.maximum(jnp.max(s, axis=-1, keepdims=True), sink)
            p = jnp.exp(s - m)
            return p, 1.0 / (jnp.sum(p, axis=-1, keepdims=True) + jnp.exp(sink - m))

        def value_product(i):
            b, p = work[i]
            pr, ie, io = probs[i]
            return _dot(pr, vvs[b][p // 2]) * jnp.where(even_lanes, ie, io)

        ahead = 3
        outs, probs = [], []
        pending = [scores(i) for i in range(ahead)]
        for i, (b, p) in enumerate(work):
            s = jnp.where(masks[b], pending.pop(0), -1e30)
            if i + ahead < len(work):
                pending.append(scores(i + ahead))
            pe, ie = unnormalised(s[:, :2 * CH], sk_ref[2 * p])
            po, io = unnormalised(s[:, 2 * CH:], sk_ref[2 * p + 1])
            probs.append((jnp.concatenate([pe, po], axis=1).astype(BF16), ie, io))
            if i >= 1:
                outs.append(value_product(i - 1))
        outs.append(value_product(len(work) - 1))
        for b in range(nblk):
            o_ref[CH * b:CH * (b + 1), :] = jnp.concatenate(outs[npairs * b:npairs * (b + 1)], axis=1).astype(BF16)

    prev = lambda n: (jnp.maximum(nblk * n - 1, 0), 0)
    cur = lambda n: (n, 0)
    return _call(
        body, (sinks, q, k, k, va, va), name="attn_fwd", grid=(nsteps,),
        in_specs=[pl.BlockSpec(memory_space=pltpu.SMEM), pl.BlockSpec((nblk * CH, D), cur),
                  pl.BlockSpec((CH, KVW), prev), pl.BlockSpec((nblk * CH, KVW), cur),
                  pl.BlockSpec((CH, KVW), prev), pl.BlockSpec((nblk * CH, KVW), cur)],
        out_specs=pl.BlockSpec((nblk * CH, D), cur), out_shape=jax.ShapeDtypeStruct((T, D), BF16),
        sem=("parallel",), comm=comm)


def _attn_bwd(q, k, va, datt, sinks, rc, rs1, rs2, comm=None):
    T = q.shape[0]
    nb = T // CH

    def body(sk_ref, q_ref, kp_ref, kc_ref, vp_ref, vc_ref, do_ref, cq_ref, s1q_ref, s2q_ref, ck_ref, s1k_ref, s2k_ref,
             dq_ref, dk_ref, dv_ref, dsk_ref, kcar, vcar):
        n = pl.program_id(0)

        @pl.when(n == 0)
        def _():
            kcar[...] = jnp.zeros_like(kcar)
            vcar[...] = jnp.zeros_like(vcar)
            dsk_ref[...] = jnp.zeros_like(dsk_ref)

        def flush(kprev, vprev):
            ck, s1k, s2k = ck_ref[...], s1k_ref[...], s2k_ref[...]
            for j in range(KVW // CH):
                sl = slice(CH * j, CH * (j + 1))
                dk_ref[:, sl] = _rope_t(kcar[:, sl] + kprev[:, sl], ck, s1k, s2k).astype(BF16)
                dv_ref[:, sl] = (vcar[:, sl] + vprev[:, sl]).astype(BF16)

        @pl.when(n < nb)
        def _():
            mask = _attn_mask(n)
            kp, kc, vp, vc = kp_ref[...], kc_ref[...], vp_ref[...], vc_ref[...]
            cq, s1q, s2q = cq_ref[...], s1q_ref[...], s2q_ref[...]
            lane = lax.broadcasted_iota(jnp.int32, (1, CH), 1)
            dsk = jnp.zeros((1, CH), F32)
            npairs = D // CH
            kks = [_pair_layout(kp, kc, grp) for grp in range(NKV)]
            vvs = [_pair_layout(vp, vc, grp) for grp in range(NKV)]
            qs = [q_ref[:, CH * p:CH * (p + 1)] for p in range(npairs)]
            dos = [do_ref[:, CH * p:CH * (p + 1)].astype(BF16) for p in range(npairs)]

            def first(p):
                return _dot_nt(qs[p], kks[p // 2]), _dot_nt(dos[p], vvs[p // 2])

            def last(p, ds, pb):
                return (_rope_t(_dot(ds, kks[p // 2]), cq, s1q, s2q) * Q_SCALE, _dot_tn(qs[p], ds), _dot_tn(dos[p], pb))

            ahead = 2
            pending = [first(p) for p in range(ahead)]
            mids, ends = [], []
            for p in range(npairs):
                s, dp = pending.pop(0)
                s = jnp.where(mask, s, -1e30)
                if p + ahead < npairs:
                    pending.append(first(p + ahead))
                ds_parts, p_parts = [], []
                for par in range(2):
                    sl = slice(2 * CH * par, 2 * CH * (par + 1))
                    pr, psink = _softmax_sink(s[:, sl], sk_ref[2 * p + par])
                    delta = jnp.sum(pr * dp[:, sl], axis=-1, keepdims=True)
                    ds_parts.append(pr * (dp[:, sl] - delta))
                    p_parts.append(pr)
                    tot = -jnp.sum(psink * delta, axis=0, keepdims=True)
                    dsk = dsk + jnp.where(lane == 2 * p + par, tot, 0.0)
                mids.append((jnp.concatenate(ds_parts, axis=1).astype(BF16), jnp.concatenate(p_parts, axis=1).astype(BF16)))
                if p >= 1:
                    ends.append(last(p - 1, *mids[p - 1]))
            ends.append(last(npairs - 1, *mids[-1]))
            dq_cols = [e[0] for e in ends]
            def fold(i):
                rows = []
                for grp in range(NKV):
                    acc = ends[2 * grp][i] + ends[2 * grp + 1][i]
                    rows.append(acc[:HD, :2 * CH] + acc[HD:, 2 * CH:])
                return jnp.concatenate(rows, axis=0).T

            dkf, dvf = fold(1), fold(2)
            dq_ref[...] = jnp.concatenate(dq_cols, axis=1).astype(BF16)
            dsk_ref[...] += dsk
            flush(dkf[:CH], dvf[:CH])
            kcar[...] = dkf[CH:]
            vcar[...] = dvf[CH:]

        @pl.when(n == nb)
        def _():
            z = jnp.zeros((CH, KVW), F32)
            flush(z, z)

    last = nb - 1
    cur = lambda n: (jnp.minimum(n, last), 0)
    prev = lambda n: (jnp.clip(n - 1, 0, last), 0)
    sd = jax.ShapeDtypeStruct
    return _call(
        body, (sinks, q, k, k, va, va, datt, rc, rs1, rs2, rc, rs1, rs2), name="attn_bwd", grid=(nb + 1,),
        in_specs=[pl.BlockSpec(memory_space=pltpu.SMEM), pl.BlockSpec((CH, D), cur),
                  pl.BlockSpec((CH, KVW), prev), pl.BlockSpec((CH, KVW), cur),
                  pl.BlockSpec((CH, KVW), prev), pl.BlockSpec((CH, KVW), cur),
                  pl.BlockSpec((CH, D), cur),
                  pl.BlockSpec((CH, CH), cur), pl.BlockSpec((CH, CH), cur), pl.BlockSpec((CH, CH), cur),
                  pl.BlockSpec((CH, CH), prev), pl.BlockSpec((CH, CH), prev), pl.BlockSpec((CH, CH), prev)],
        out_specs=[pl.BlockSpec((CH, D), cur), pl.BlockSpec((CH, KVW), prev), pl.BlockSpec((CH, KVW), prev),
                   _const((1, CH))],
        out_shape=[sd((T, D), BF16), sd((T, KVW), BF16), sd((T, KVW), BF16), sd((1, CH), F32)],
        scratch_shapes=[pltpu.VMEM((CH, KVW), F32), pltpu.VMEM((CH, KVW), F32)], sem=("arbitrary",), comm=comm)


def _merge_fwd(a, att, ga, gb, x, w_a, w_b, w_o, g2, tm):
    T = x.shape[0]

    def body(a_ref, att_ref, ga_ref, gb_ref, x_ref, wa_ref, wb_ref, wo_ref, g_ref,
             pa_ref, pb_ref, mg_ref, mix_ref, x1_ref):
        pa = _dot(a_ref[...], wa_ref[...])
        pb = _dot(att_ref[...], wb_ref[...])
        pa_ref[...] = pa.astype(BF16)
        pb_ref[...] = pb.astype(BF16)
        merged = (_sigmoid(ga_ref[...].astype(F32)) * pa + _sigmoid(gb_ref[...].astype(F32)) * pb).astype(BF16)
        mg_ref[...] = merged
        mix = _dot(merged, wo_ref[...])
        mix_ref[...] = mix
        mhat, _ = _rms_hat(mix)
        x1_ref[...] = x_ref[...] + mhat * g_ref[...]

    sd = jax.ShapeDtypeStruct
    return pl.pallas_call(
        body, name="merge_fwd", grid=(T // tm,),
        in_specs=[_rows(tm, D)] * 5 + [_resident((D, D))] * 3 + [_const((1, D))],
        out_specs=[_rows(tm, D)] * 5,
        out_shape=[sd((T, D), BF16), sd((T, D), BF16), sd((T, D), BF16), sd((T, D), F32), sd((T, D), F32)],
        compiler_params=_cparams(("parallel",)),
    )(a, att, ga, gb, x, w_a, w_b, w_o, g2)


def _merge_bwd(dx1, mix, ga, gb, pa, pb, a, att, merged, w_a, w_b, w_o, g2, tm, comm=None):
    T = dx1.shape[0]
    nsteps = T // tm

    def body(dx1_ref, mix_ref, ga_ref, gb_ref, pa_ref, pb_ref, a_ref, att_ref, mg_ref, wa_ref, wb_ref, wo_ref, g_ref,
             dga_ref, dgb_ref, da_ref, datt_ref, dg_ref, dwa_ref, dwb_ref, dwo_ref, acc, sem):
        i = pl.program_id(0)

        @pl.when(i == 0)
        def _():
            dg_ref[...] = jnp.zeros_like(dg_ref)
            acc[...] = jnp.zeros_like(acc)

        mhat, r = _rms_hat(mix_ref[...])
        dmix, dg = _rms_bwd(mhat, r, g_ref[...], dx1_ref[...])
        dg_ref[...] += dg
        dmix = dmix.astype(BF16)
        dmerged = _dot_nt(dmix, wo_ref[...])
        sa = _sigmoid(ga_ref[...].astype(F32))
        sb = _sigmoid(gb_ref[...].astype(F32))
        dao = (dmerged * sa).astype(BF16)
        dbo = (dmerged * sb).astype(BF16)
        dga_ref[...] = (dmerged * pa_ref[...].astype(F32) * (sa * (1.0 - sa))).astype(BF16)
        dgb_ref[...] = (dmerged * pb_ref[...].astype(F32) * (sb * (1.0 - sb))).astype(BF16)
        da_ref[...] = _dot_nt(dao, wa_ref[...])
        datt_ref[...] = _dot_nt(dbo, wb_ref[...]).astype(BF16)
        acc[0] += _dot_tn(a_ref[...], dao)
        acc[1] += _dot_tn(att_ref[...], dbo)
        acc[2] += _dot_tn(mg_ref[...], dmix)

        @pl.when(i == nsteps - 1)
        def _():
            outs = [pltpu.make_async_copy(acc.at[j], ref, sem.at[j]) for j, ref in enumerate((dwa_ref, dwb_ref, dwo_ref))]
            for cp in outs:
                cp.start()
            for cp in outs:
                cp.wait()

    sd = jax.ShapeDtypeStruct
    return _call(
        body, (dx1, mix, ga, gb, pa, pb, a, att, merged, w_a, w_b, w_o, g2), name="merge_bwd", grid=(nsteps,),
        in_specs=[_rows(tm, D)] * 9 + [_resident((D, D))] * 3 + [_const((1, D))],
        out_specs=[_rows(tm, D)] * 4 + [_const((1, D))] + [ANY] * 3,
        out_shape=[sd((T, D), BF16), sd((T, D), BF16), sd((T, D), F32), sd((T, D), BF16), sd((1, D), F32)]
        + [sd((D, D), F32)] * 3,
        scratch_shapes=[pltpu.VMEM((3, D, D), F32), _dma_sems(3)], sem=("arbitrary",), comm=comm)


def _ffn(x1, target, w1, w2, g3, g4, tm):
    T = x1.shape[0]

    def body(x_ref, t_ref, w1_ref, w2_ref, g3_ref, g4_ref,
             hf_ref, f2_ref, dff_ref, df1_ref, dx_ref, ls_ref, dg3_ref, dg4_ref):
        @pl.when(pl.program_id(0) == 0)
        def _():
            ls_ref[...] = jnp.zeros_like(ls_ref)
            dg3_ref[...] = jnp.zeros_like(dg3_ref)
            dg4_ref[...] = jnp.zeros_like(dg4_ref)

        x = x_ref[...]
        g3, g4 = g3_ref[...], g4_ref[...]
        xhat, r3 = _rms_hat(x)
        hf = (xhat * g3).astype(BF16)
        hf_ref[...] = hf
        rl = jnp.maximum(_dot(hf, w1_ref[...]), 0.0)
        f2 = (rl * rl).astype(BF16)
        f2_ref[...] = f2
        fhat, r4 = _rms_hat(_dot(f2, w2_ref[...]))
        err = x + fhat * g4 - t_ref[...]
        ls_ref[...] += jnp.sum(err * err, axis=0, keepdims=True)
        dy = err * (1.0 / D)
        dff, dg4 = _rms_bwd(fhat, r4, g4, dy)
        dg4_ref[...] += dg4
        dff = dff.astype(BF16)
        dff_ref[...] = dff
        df1 = (_dot_nt(dff, w2_ref[...]) * (2.0 * rl)).astype(BF16)
        df1_ref[...] = df1
        dxn, dg3 = _rms_bwd(xhat, r3, g3, _dot_nt(df1, w1_ref[...]))
        dg3_ref[...] += dg3
        dx_ref[...] = dy + dxn

    sd = jax.ShapeDtypeStruct
    return pl.pallas_call(
        body, name="ffn_fwd_bwd", grid=(T // tm,),
        in_specs=[_rows(tm, D), _rows(tm, D), _resident((D, DFF)), _resident((DFF, D)), _const((1, D)), _const((1, D))],
        out_specs=[_rows(tm, D), _rows(tm, DFF), _rows(tm, D), _rows(tm, DFF), _rows(tm, D), _const((1, D)),
                   _const((1, D)), _const((1, D))],
        out_shape=[sd((T, D), BF16), sd((T, DFF), BF16), sd((T, D), BF16), sd((T, DFF), BF16), sd((T, D), F32),
                   sd((1, D), F32), sd((1, D), F32), sd((1, D), F32)],
        compiler_params=pltpu.CompilerParams(vmem_limit_bytes=VMEM_PHYSICAL, dimension_semantics=("arbitrary",)),
    )(x1, target, w1, w2, g3, g4)


def _inproj_bwd(parts, x, dx1, g1, w_in, tm, comm=None):
    T = x.shape[0]
    widths = [p.shape[1] for p in parts]
    offs = [sum(widths[:i]) for i in range(len(widths) + 1)]
    assert offs[-1] == IN_W

    def body(*refs):
        n = len(parts)
        prefs = refs[:n]
        x_ref, dx1_ref, g_ref, w_ref, dx_ref, dp_ref, dg_ref = refs[n:]

        @pl.when(pl.program_id(0) == 0)
        def _():
            dg_ref[...] = jnp.zeros_like(dg_ref)

        for i in range(n):
            dp_ref[:, offs[i]:offs[i + 1]] = prefs[i][...]
        dh = _dot_nt(dp_ref[...], w_ref[...])
        xhat, r = _rms_hat(x_ref[...])
        dxn, dg = _rms_bwd(xhat, r, g_ref[...], dh)
        dg_ref[...] += dg
        dx_ref[...] = dx1_ref[...] + dxn

    sd = jax.ShapeDtypeStruct
    return _call(
        body, (*parts, x, dx1, g1, w_in), name="inproj_bwd", grid=(T // tm,),
        in_specs=[_rows(tm, w) for w in widths] + [_rows(tm, D), _rows(tm, D), _const((1, D)), _resident((D, IN_W))],
        out_specs=[_rows(tm, D), _rows(tm, IN_W), _const((1, D))],
        out_shape=[sd((T, D), F32), sd((T, IN_W), BF16), sd((1, D), F32)], sem=("arbitrary",), comm=comm)


def _wgrad(a, g, tn, tm, name, comm=None):
    T, K = a.shape
    N = g.shape[1]

    def body(a_ref, g_ref, o_ref):
        @pl.when(pl.program_id(1) == 0)
        def _():
            o_ref[...] = jnp.zeros_like(o_ref)

        o_ref[...] += _dot_tn(a_ref[...], g_ref[...])

    return _call(
        body, (a, g), name=name, grid=(N // tn, T // tm),
        in_specs=[pl.BlockSpec((tm, K), lambda j, t: (t, 0)), pl.BlockSpec((tm, tn), lambda j, t: (t, j))],
        out_specs=pl.BlockSpec((K, tn), lambda j, t: (0, j)),
        out_shape=jax.ShapeDtypeStruct((K, N), F32), sem=("parallel", "arbitrary"), comm=comm)


def _adamw(ws, gs, ms, vs, trs, name):
    n = len(ws)
    walk = _Walk(w.shape[0] // tr for w, tr in zip(ws, trs))
    bc1 = 1.0 / (1.0 - B1 ** STEP)
    bc2 = 1.0 / (1.0 - B2 ** STEP)

    def body(*refs):
        i = pl.program_id(0)
        for k in range(n):
            mine = tuple(refs[j * n + k] for j in range(8))

            @pl.when(walk.mine(k, i))
            def _(mine=mine):
                w_ref, g_ref, m_ref, v_ref, go_ref, d_ref, nm_ref, nv_ref = mine
                g = g_ref[...]
                go_ref[...] = g
                m = B1 * m_ref[...] + (1.0 - B1) * g
                v = B2 * v_ref[...] + (1.0 - B2) * (g * g)
                nm_ref[...] = m
                nv_ref[...] = v
                d_ref[...] = -LR * ((m * bc1) / (jnp.sqrt(v * bc2) + AEPS) + WD * w_ref[...])

    def spec(k):
        return pl.BlockSpec((trs[k], ws[k].shape[1]), lambda i: (walk.tile(k, i), 0))

    specs = [spec(k) for k in range(n)]
    res = pl.pallas_call(
        body, name=name, grid=(walk.steps,), in_specs=specs * 4, out_specs=specs * 4,
        out_shape=[jax.ShapeDtypeStruct(w.shape, F32) for w in ws] * 4,
        compiler_params=_cparams(("arbitrary",)),
    )(*ws, *gs, *ms, *vs)
    return [tuple(res[j * n + k] for j in range(4)) for k in range(n)]


BIG = (("col", (D, IN_W)), ("row", (D, D)), ("row", (D, D)), ("row", (D, D)), ("col", (D, DFF)), ("row", (DFF, D)))
NBIG = len(BIG)
ANY = pl.BlockSpec(memory_space=pl.ANY)


def _shard_shape(kind, shape):
    R, C = shape
    return (R, C // 4) if kind == "col" else (R // 4, C)


def _half_shape(kind, shape):
    R, C = shape
    return (R // 2, C) if kind == "col" else (R, C // 2)


def _piece_shape(kind, shape):
    R, C = shape
    return (R // 2, C // 4) if kind == "col" else (R // 4, C // 2)


def _own_region(ref, kind, shape, s):
    R, C = shape
    return ref.at[:, pl.ds(s * (C // 4), C // 4)] if kind == "col" else ref.at[pl.ds(s * (R // 4), R // 4), :]


def _ag_region(ref, kind, shape, s, hc):
    R, C = shape
    if kind == "col":
        return ref.at[pl.ds(hc * (R // 2), R // 2), pl.ds(s * (C // 4), C // 4)]
    return ref.at[pl.ds(s * (R // 4) + hc * (R // 8), R // 8), :]


def _ag_shard_half(ref, kind, shape, hc):
    R, C = shape
    return ref.at[pl.ds(hc * (R // 2), R // 2), :] if kind == "col" else ref.at[pl.ds(hc * (R // 8), R // 8), :]


def _grad_half(ref, kind, shape, hc):
    R, C = shape
    return ref.at[pl.ds(hc * (R // 2), R // 2), :] if kind == "col" else ref.at[:, pl.ds(hc * (C // 2), C // 2)]


def _half_piece(ref, kind, shape, s):
    R, C = shape
    return ref.at[:, pl.ds(s * (C // 4), C // 4)] if kind == "col" else ref.at[pl.ds(s * (R // 4), R // 4), :]


def _place():
    x, y, c = lax.axis_index("x"), lax.axis_index("y"), lax.axis_index("c")
    chips = [(1 - x, y), (x, 1 - y), (1 - x, 1 - y)]
    return x, y, c, chips


def _rcopy(src, dst, ssem, rsem, dev):
    return pltpu.make_async_remote_copy(src_ref=src, dst_ref=dst, send_sem=ssem, recv_sem=rsem,
                                        device_id=dev, device_id_type=MESH)


def _dma_sems(n):
    return pltpu.SemaphoreType.DMA((n,))


def _x_gather_ici(shards, ws):
    n = len(ws)
    specs = [BIG[w] for w in ws]

    def place():
        x, y, c, chips = _place()
        return c, chips, 2 * x + y

    def sends(sh, full, sc):
        c, chips, me_s = place()
        return [_rcopy(_ag_shard_half(sh[i], kind, shape, c), _ag_region(full[i], kind, shape, me_s, c),
                       sc[0].at[3 * i + j], sc[1].at[3 * i + j], (cx, cy, c))
                for i, (kind, shape) in enumerate(specs) for j, (cx, cy) in enumerate(chips)]

    def start(sh, full, sc):
        for i in range(n):
            pltpu.make_async_copy(sh[i], sc[4 + i], sc[2].at[i]).start()
        for cp in sends(sh, full, sc):
            cp.start()

    def finish(sh, full, sc):
        c, chips, me_s = place()
        stores = []
        for i, (kind, shape) in enumerate(specs):
            pltpu.make_async_copy(sh[i], sc[4 + i], sc[2].at[i]).wait()
            st = pltpu.make_async_copy(sc[4 + i], _own_region(full[i], kind, shape, me_s), sc[3].at[i])
            st.start()
            stores.append(st)
        for i, (kind, shape) in enumerate(specs):
            for j, (cx, cy) in enumerate(chips):
                reg = _ag_region(full[i], kind, shape, 2 * cx + cy, c)
                _rcopy(reg, reg, sc[0].at[3 * i + j], sc[1].at[3 * i + j], (cx, cy, c)).wait_recv()
        for cp in sends(sh, full, sc):
            cp.wait_send()
        for st in stores:
            st.wait()

    return _Exchange(
        shards, [jax.ShapeDtypeStruct(shape, BF16) for _, shape in specs], {},
        [_dma_sems(3 * n), _dma_sems(3 * n), _dma_sems(n), _dma_sems(n)]
        + [pltpu.VMEM(_shard_shape(k, s), BF16) for k, s in specs], start, finish)


def _x_gather_d2d(wholes, ws):
    specs = [BIG[w] for w in ws]
    n = len(ws)

    def copies(full, sc, mine):
        x, y, c, chips = _place()
        hc = c if mine else 1 - c
        return [_rcopy(reg, reg, sc[0].at[3 * i + j], sc[1].at[3 * i + j], (x, y, 1 - c))
                for i, (kind, shape) in enumerate(specs) for j, (cx, cy) in enumerate(chips)
                for reg in [_ag_region(full[i], kind, shape, 2 * cx + cy, hc)]]

    def start(_, full, sc):
        for cp in copies(full, sc, True):
            cp.start()

    def finish(_, full, sc):
        for cp in copies(full, sc, False):
            cp.wait_recv()
        for cp in copies(full, sc, True):
            cp.wait_send()

    return _Exchange(wholes, [jax.ShapeDtypeStruct(shape, BF16) for _, shape in specs], {i: i for i in range(n)},
                     [_dma_sems(3 * n), _dma_sems(3 * n)], start, finish)


def _x_grads_sibling(grads, ws):
    specs = [BIG[w] for w in ws]
    n = len(ws)

    def copies(g, got, sc):
        x, y, c, _ = _place()
        return [_rcopy(_grad_half(g[i], kind, shape, 1 - c), got[i], sc[0].at[i], sc[1].at[i], (x, y, 1 - c))
                for i, (kind, shape) in enumerate(specs)]

    def start(g, got, sc):
        for cp in copies(g, got, sc):
            cp.start()

    def finish(g, got, sc):
        for cp in copies(g, got, sc):
            cp.wait_recv()
        for cp in copies(g, got, sc):
            cp.wait_send()

    return _Exchange(grads, [jax.ShapeDtypeStruct(_half_shape(k, s), F32) for k, s in specs], {},
                     [_dma_sems(n), _dma_sems(n)], start, finish)


def _x_grads_chips(sums_bf, ws):
    specs = [BIG[w] for w in ws]
    n = len(ws)

    def copies(s16, got, sc):
        x, y, c, chips = _place()
        return [_rcopy(_half_piece(s16[i], kind, shape, 2 * cx + cy), got[i].at[j],
                       sc[0].at[3 * i + j], sc[1].at[3 * i + j], (cx, cy, c))
                for i, (kind, shape) in enumerate(specs) for j, (cx, cy) in enumerate(chips)]

    def start(s16, got, sc):
        for cp in copies(s16, got, sc):
            cp.start()

    def finish(s16, got, sc):
        for cp in copies(s16, got, sc):
            cp.wait_recv()
        for cp in copies(s16, got, sc):
            cp.wait_send()

    return _Exchange(sums_bf, [jax.ShapeDtypeStruct((3,) + _piece_shape(k, s), BF16) for k, s in specs], {},
                     [_dma_sems(3 * n), _dma_sems(3 * n)], start, finish)


def _shard_half(ref, kind, shape, hc):
    sr, sc = _shard_shape(kind, shape)
    return ref.at[pl.ds(hc * (sr // 2), sr // 2), :] if kind == "col" else ref.at[:, pl.ds(hc * (sc // 2), sc // 2)]


def _x_grads_share(shard_grads, ws):
    specs = [BIG[w] for w in ws]
    n = len(ws)

    def copies(g, sc, mine):
        x, y, c, _ = _place()
        hc = c if mine else 1 - c
        return [_rcopy(part, part, sc[0].at[i], sc[1].at[i], (x, y, 1 - c))
                for i, (kind, shape) in enumerate(specs) for part in [_shard_half(g[i], kind, shape, hc)]]

    def start(_, g, sc):
        for cp in copies(g, sc, True):
            cp.start()

    def finish(_, g, sc):
        for cp in copies(g, sc, False):
            cp.wait_recv()
        for cp in copies(g, sc, True):
            cp.wait_send()

    return _Exchange(shard_grads, [jax.ShapeDtypeStruct(_shard_shape(k, s), F32) for k, s in specs],
                     {i: i for i in range(n)}, [_dma_sems(n), _dma_sems(n)], start, finish)


ADD_BLOCK_BYTES = 4 * 1024 * 1024


def _add_rows(rows, cols, n_arrays):
    limit = ADD_BLOCK_BYTES // (1 if n_arrays == 1 else 4)
    r = rows
    while r > 64 and r * cols * 4 > limit:
        r //= 2
    return r


class _Walk:
    def __init__(self, tiles):
        self.tiles = list(tiles)
        self.starts = [sum(self.tiles[:k]) for k in range(len(self.tiles))]
        self.steps = sum(self.tiles)

    def tile(self, k, i):
        return jnp.clip(i - self.starts[k], 0, self.tiles[k] - 1)

    def mine(self, k, i):
        return (i >= self.starts[k]) & (i < self.starts[k] + self.tiles[k])


def _add_halves(place, gs, gots, kinds, name):
    n = len(gs)
    halves = [_half_shape(kind, g.shape) for g, kind in zip(gs, kinds)]
    rows = [_add_rows(hr, hc, n) for hr, hc in halves]
    walk = _Walk(hr // r for (hr, _), r in zip(halves, rows))

    def body(p_ref, *refs):
        i = pl.program_id(0)
        for k in range(n):
            g_ref, b_ref, s_ref, sb_ref = (refs[j * n + k] for j in range(4))

            @pl.when(walk.mine(k, i))
            def _(g_ref=g_ref, b_ref=b_ref, s_ref=s_ref, sb_ref=sb_ref):
                s = g_ref[...] + b_ref[...]
                s_ref[...] = s
                sb_ref[...] = s.astype(BF16)

    def g_spec(k):
        if kinds[k] == "col":
            return pl.BlockSpec((rows[k], gs[k].shape[1]), lambda i, p: (p[0] * walk.tiles[k] + walk.tile(k, i), 0))
        return pl.BlockSpec((rows[k], halves[k][1]), lambda i, p: (walk.tile(k, i), p[0]))

    def spec(k):
        return pl.BlockSpec((rows[k], halves[k][1]), lambda i, p: (walk.tile(k, i), 0))

    specs = [spec(k) for k in range(n)]
    res = pl.pallas_call(
        body, name=name,
        grid_spec=pltpu.PrefetchScalarGridSpec(num_scalar_prefetch=1, grid=(walk.steps,),
                                               in_specs=[g_spec(k) for k in range(n)] + specs, out_specs=specs + specs),
        out_shape=[jax.ShapeDtypeStruct(h, F32) for h in halves] + [jax.ShapeDtypeStruct(h, BF16) for h in halves],
        compiler_params=_cparams(("arbitrary",)),
    )(place, *gs, *gots)
    return [(res[k], res[n + k]) for k in range(n)]


def _add_pieces(place, halves, gots, specs_big, name):
    n = len(halves)
    pieces = [_piece_shape(kind, shape) for kind, shape in specs_big]
    rows = [_add_rows(pr, pc, n) for pr, pc in pieces]
    walk = _Walk(pr // r for (pr, _), r in zip(pieces, rows))

    def body(p_ref, *refs):
        i = pl.program_id(0)
        for k in range(n):
            m_ref, g_ref, o_ref = (refs[j * n + k] for j in range(3))

            @pl.when(walk.mine(k, i))
            def _(m_ref=m_ref, g_ref=g_ref, o_ref=o_ref):
                acc = m_ref[...]
                for j in range(3):
                    acc = acc + g_ref[j].astype(F32)
                o_ref[...] = acc

    def m_spec(k):
        if specs_big[k][0] == "col":
            return pl.BlockSpec((rows[k], pieces[k][1]), lambda i, p: (walk.tile(k, i), p[1]))
        return pl.BlockSpec((rows[k], pieces[k][1]), lambda i, p: (p[1] * walk.tiles[k] + walk.tile(k, i), 0))

    def got_spec(k):
        return pl.BlockSpec((3, rows[k], pieces[k][1]), lambda i, p: (0, walk.tile(k, i), 0))

    def o_spec(k):
        if specs_big[k][0] == "col":
            return pl.BlockSpec((rows[k], pieces[k][1]), lambda i, p: (p[0] * walk.tiles[k] + walk.tile(k, i), 0))
        return pl.BlockSpec((rows[k], pieces[k][1]), lambda i, p: (walk.tile(k, i), p[0]))

    return pl.pallas_call(
        body, name=name,
        grid_spec=pltpu.PrefetchScalarGridSpec(
            num_scalar_prefetch=1, grid=(walk.steps,),
            in_specs=[m_spec(k) for k in range(n)] + [got_spec(k) for k in range(n)],
            out_specs=[o_spec(k) for k in range(n)]),
        out_shape=[jax.ShapeDtypeStruct(_shard_shape(kind, shape), F32) for kind, shape in specs_big],
        compiler_params=_cparams(("arbitrary",)),
    )(place, *halves, *gots)


SMALL_ROWS = 1024 + 8 * 8 + 8


def _x_small_all_reduce(p):
    def parts(p_ref, sc):
        slots, ssem, rsem = sc[0], sc[2], sc[3]
        x, y, c = lax.axis_index("x"), lax.axis_index("y"), lax.axis_index("c")
        me = 4 * x + 2 * y + c
        out = []
        for r in range(1, 8):
            bx, by, bc = (r >> 2) & 1, (r >> 1) & 1, r & 1
            tgt = (1 - x if bx else x, 1 - y if by else y, 1 - c if bc else c)
            send = _rcopy(p_ref, slots.at[me], ssem.at[r - 1], rsem.at[r - 1], tgt)
            src = 4 * tgt[0] + 2 * tgt[1] + tgt[2]
            recv = _rcopy(p_ref, slots.at[src], ssem.at[r - 1], rsem.at[r - 1], tgt)
            out.append((send, recv))
        return me, out

    def start(ins, outs, sc):
        me, cps = parts(ins[0], sc)
        pltpu.make_async_copy(ins[0], sc[0].at[me], sc[4].at[0]).start()
        for send, _ in cps:
            send.start()

    def finish(ins, outs, sc):
        me, cps = parts(ins[0], sc)
        pltpu.make_async_copy(ins[0], sc[0].at[me], sc[4].at[0]).wait()
        for _, recv in cps:
            recv.wait_recv()
        acc = sc[0][0]
        for d in range(1, 8):
            acc = acc + sc[0][d]
        sc[1][...] = acc
        back = pltpu.make_async_copy(sc[1], outs[0], sc[4].at[1])
        back.start()
        for send, _ in cps:
            send.wait_send()
        back.wait()

    return _Exchange([p], [jax.ShapeDtypeStruct((SMALL_ROWS, CH), F32)], {},
                     [pltpu.VMEM((8, SMALL_ROWS, CH), F32), pltpu.VMEM((SMALL_ROWS, CH), F32), _dma_sems(7), _dma_sems(7),
                      _dma_sems(2)], start, finish)


def _rope_tables(positions, comm=None):
    T = positions.shape[0]
    inv_freq = 500000.0 ** (-jnp.arange(0, 2 * ROPE_HALF, 2, dtype=F32) / (2 * ROPE_HALF))
    head = jnp.concatenate([inv_freq, inv_freq, jnp.zeros((HD - 2 * ROPE_HALF,), F32)])
    lane_freq = jnp.concatenate([head, head])[None, :]
    pos = jnp.broadcast_to(positions.astype(F32)[:, None], (T, CH))
    tm = min(1024, T)

    def body(p_ref, f_ref, c_ref, s1_ref, s2_ref):
        ang = p_ref[...] * f_ref[...]
        sin = jnp.sin(ang)
        first = (lax.broadcasted_iota(jnp.int32, ang.shape, 1) % HD) < ROPE_HALF
        c_ref[...] = jnp.cos(ang)
        s1_ref[...] = jnp.where(first, -sin, 0.0)
        s2_ref[...] = jnp.where(first, 0.0, sin)

    return _call(body, (pos, lane_freq), name="rope_tables", grid=(T // tm,),
                 in_specs=[_rows(tm, CH), _const((1, CH))], out_specs=[_rows(tm, CH)] * 3,
                 out_shape=[jax.ShapeDtypeStruct((T, CH), F32)] * 3, sem=("parallel",), comm=comm)


BIG_NAMES = ("w_in", "w_a", "w_b", "w_o", "w_ff_in", "w_ff_out")
SMALL_NAMES = ("w_spatial", "ln_v_gain", "ln_v_bias", "b_spatial", "sinks", "norm_mix_pre", "norm_mix_post",
               "norm_ff_pre", "norm_ff_post")
WEIGHT_ORDER = ("w_in", "ln_v_gain", "ln_v_bias", "w_spatial", "b_spatial", "sinks", "w_a", "w_b", "w_o",
                "norm_mix_pre", "norm_mix_post", "w_ff_in", "w_ff_out", "norm_ff_pre", "norm_ff_post")


def _pack_small(d, loss_sums=None):
    parts = []
    for n in SMALL_NAMES:
        flat = d[n].reshape(-1)
        pad = (-flat.shape[0]) % (8 * CH)
        parts.append(jnp.pad(flat, (0, pad)).reshape(-1, CH))
    parts.append(jnp.zeros((8, CH), F32) if loss_sums is None else loss_sums.reshape(8, CH))
    return jnp.concatenate(parts, axis=0)


def _unpack_small(p, like):
    out, row = {}, 0
    for n in SMALL_NAMES:
        size = like[n].size
        rows = -(-size // (8 * CH)) * 8
        out[n] = p[row:row + rows].reshape(-1)[:size].reshape(like[n].shape)
        row += rows
    return out


def kernel(x, positions, w_in, ln_v_gain, ln_v_bias, w_spatial, b_spatial, sinks, w_a, w_b, w_o, norm_mix_pre, norm_mix_post, w_ff_in, w_ff_out, norm_ff_pre, norm_ff_post, loss_target, m_w_in, m_ln_v_gain, m_ln_v_bias, m_w_spatial, m_b_spatial, m_sinks, m_w_a, m_w_b, m_w_o, m_norm_mix_pre, m_norm_mix_post, m_w_ff_in, m_w_ff_out, m_norm_ff_pre, m_norm_ff_post, v_w_in, v_ln_v_gain, v_ln_v_bias, v_w_spatial, v_b_spatial, v_sinks, v_w_a, v_w_b, v_w_o, v_norm_mix_pre, v_norm_mix_post, v_w_ff_in, v_w_ff_out, v_norm_ff_pre, v_norm_ff_post):
    w = dict(w_in=w_in, ln_v_gain=ln_v_gain, ln_v_bias=ln_v_bias, w_spatial=w_spatial, b_spatial=b_spatial, sinks=sinks,
             w_a=w_a, w_b=w_b, w_o=w_o, norm_mix_pre=norm_mix_pre, norm_mix_post=norm_mix_post, w_ff_in=w_ff_in,
             w_ff_out=w_ff_out, norm_ff_pre=norm_ff_pre, norm_ff_post=norm_ff_post)
    m = dict(w_in=m_w_in, ln_v_gain=m_ln_v_gain, ln_v_bias=m_ln_v_bias, w_spatial=m_w_spatial, b_spatial=m_b_spatial,
             sinks=m_sinks, w_a=m_w_a, w_b=m_w_b, w_o=m_w_o, norm_mix_pre=m_norm_mix_pre, norm_mix_post=m_norm_mix_post,
             w_ff_in=m_w_ff_in, w_ff_out=m_w_ff_out, norm_ff_pre=m_norm_ff_pre, norm_ff_post=m_norm_ff_post)
    v = dict(w_in=v_w_in, ln_v_gain=v_ln_v_gain, ln_v_bias=v_ln_v_bias, w_spatial=v_w_spatial, b_spatial=v_b_spatial,
             sinks=v_sinks, w_a=v_w_a, w_b=v_w_b, w_o=v_w_o, norm_mix_pre=v_norm_mix_pre, norm_mix_post=v_norm_mix_post,
             w_ff_in=v_w_ff_in, w_ff_out=v_w_ff_out, norm_ff_pre=v_norm_ff_pre, norm_ff_post=v_norm_ff_post)

    FIRST, REST = (0,), tuple(range(1, NBIG))
    shards = [w[n][0].astype(BF16) for n in BIG_NAMES]
    place = jnp.stack([lax.axis_index("c"), 2 * lax.axis_index("x") + lax.axis_index("y")]).astype(jnp.int32)
    xs, target = x[0], loss_target[0]
    T = xs.shape[0]
    wtm, wtm2 = min(1024, T), min(2048, T)
    g1, g2, g3, g4 = norm_mix_pre, norm_mix_post, norm_ff_pre, norm_ff_post
    w_sp, snk = w_spatial[0], sinks[0]
    MIX, FF = (1, 2, 3), (4, 5)
    bfull = jnp.repeat(b_spatial[0].T, CH, axis=1)

    def reduce_tail(ws, grads, got):
        tag = "_".join(BIG_NAMES[k] for k in ws)
        sums = _add_halves(place, grads, got, [BIG[k][0] for k in ws], name="grad_add_sibling_" + tag)
        return sums, _x_grads_chips([s[1] for s in sums], ws)

    def reduce_end(ws, sums, pieces):
        tag = "_".join(BIG_NAMES[k] for k in ws)
        return _add_pieces(place, [s[0] for s in sums], pieces, [BIG[k] for k in ws], name="grad_add_chips_" + tag)

    (rc, rs1, rs2), w_in_part = _rope_tables(positions[0], comm=_x_gather_ici(shards[:1], FIRST))
    w_in_b = _run(_x_gather_d2d(w_in_part, FIRST), "gather_w_in_d2d")[0]
    (h, u, vs, q, k, va, ga, gb), ff_part = _inproj(xs, g1, w_in_b, rc, rs1, rs2, tm=512, comm=_x_gather_ici(shards[4:], FF))
    a, mix_part = _sgu_fwd(u, vs, ln_v_gain, ln_v_bias, w_sp, bfull, tm=512, comm=_x_gather_ici(shards[1:4], MIX))
    att, rest = _attn_fwd(q, k, va, snk, comm=_both(_x_gather_d2d(mix_part, MIX), _x_gather_d2d(ff_part, FF)))
    w_a_b, w_b_b, w_o_b, w_ff_in_b, w_ff_out_b = rest
    pa, pb, merged, mix, x1 = _merge_fwd(a, att, ga, gb, xs, w_a_b, w_b_b, w_o_b, g2, tm=512)
    hf, f2, dff, df1, dx1, lsum, dg3, dg4 = _ffn(x1, target, w_ff_in_b, w_ff_out_b, g3, g4, tm=512)

    dw_ff_out, _ = _wgrad(f2, dff, tn=1024, tm=512, name="wgrad_ff_out")
    dw_ff_in, _ = _wgrad(hf, df1, tn=2048, tm=wtm2, name="wgrad_ff_in")
    (dga, dgb, da, datt, dg2, dw_a, dw_b, dw_o), _ = _merge_bwd(
        dx1, mix, ga, gb, pa, pb, a, att, merged, w_a_b, w_b_b, w_o_b, g2, tm=512)
    grads_rest = [dw_a, dw_b, dw_o, dw_ff_in, dw_ff_out]
    (du, dvs, dws, dbs, dlg, dlb), got_rest = _sgu_bwd(
        u, vs, da, ln_v_gain, ln_v_bias, w_sp, bfull, tm=512, comm=_x_grads_sibling(grads_rest, REST))
    sums_rest, to_chips = reduce_tail(REST, grads_rest, got_rest)
    (dq, dk, dva, dsk), pieces_rest = _attn_bwd(q, k, va, datt, snk, rc, rs1, rs2, comm=to_chips)
    partial_rest = reduce_end(REST, sums_rest, pieces_rest)
    (dx, dproj, dg1), _ = _inproj_bwd([du, dvs, dq, dk, dva, dga, dgb], xs, dx1, g1, w_in_b, tm=512)
    small = dict(ln_v_gain=dlg, ln_v_bias=dlb, w_spatial=dws, b_spatial=dbs, sinks=dsk[:, :NQ],
                 norm_mix_pre=dg1, norm_mix_post=dg2, norm_ff_pre=dg3, norm_ff_post=dg4)
    dw_in, (gs, *shard_rest) = _wgrad(
        h, dproj, tn=IN_W // 2, tm=wtm, name="wgrad_in",
        comm=_both(_x_small_all_reduce(_pack_small(small, lsum)), _x_grads_share(partial_rest, REST)))
    got_in = _run(_x_grads_sibling([dw_in], FIRST), "grads_in_to_sibling")
    sums_in, to_chips = reduce_tail(FIRST, [dw_in], got_in)
    partial_in = reduce_end(FIRST, sums_in, _run(to_chips, "grads_in_to_chips"))
    g_in = _run(_x_grads_share(partial_in, FIRST), "grads_in_share")[0]

    loss = 0.5 * jnp.sum(gs[SMALL_ROWS - 8:]) / D
    grad, delta, new_m, new_v = {}, {}, {}, {}
    for n, g in zip(BIG_NAMES, [g_in] + list(shard_rest)):
        (g_, d_, m_, v_), = _adamw([w[n][0]], [g], [m[n][0]], [v[n][0]], [256], name="adamw_" + n)
        grad[n], delta[n], new_m[n], new_v[n] = g_[None], d_[None], m_[None], v_[None]
    (gs, ds, ms, vs), = _adamw([_pack_small(w)], [gs], [_pack_small(m)], [_pack_small(v)], [SMALL_ROWS], name="adamw_small")
    for packed, dst in ((gs, grad), (ds, delta), (ms, new_m), (vs, new_v)):
        dst.update(_unpack_small(packed, w))

    outs = [loss, dx[None]]
    for group in (grad, delta, new_m, new_v):
        outs.extend(group[n] for n in WEIGHT_ORDER)
    return tuple(outs)
```

```python
import functools

import jax
import jax.numpy as jnp
from jax import lax
from jax.experimental import pallas as pl
from jax.experimental.pallas import tpu as pltpu

F32 = jnp.float32
BF16 = jnp.bfloat16

D = 1024
CH = 128
NG = 8
HD = 64
NQ = 16
NKV = 4
KVW = NKV * HD
DFF = 4 * D
EPS = 1e-6
IN_W = 5632
SEG = (0, 1024, 2048, 3072, 3328, 3584, 4608, 5632)
ROPE_HALF = 8
Q_SCALE = HD ** -0.5

LR, B1, B2, AEPS, WD, STEP = 0.001, 0.9, 0.999, 1e-08, 0.01, 10

VMEM_PHYSICAL = 64 * 1024 * 1024
VMEM_LIMIT = 60 * 1024 * 1024
MESH = pl.DeviceIdType.MESH

_GELU_C0 = 0.7978845608028654
_GELU_C1 = 0.044715


def _cparams(sem=None):
    kw = dict(vmem_limit_bytes=VMEM_LIMIT)
    if sem is not None:
        kw["dimension_semantics"] = sem
    return pltpu.CompilerParams(**kw)


def _resident(shape):
    nd = len(shape)
    return pl.BlockSpec(shape, lambda *_: (0,) * nd, pipeline_mode=pl.Buffered(1))


def _const(shape):
    nd = len(shape)
    return pl.BlockSpec(shape, lambda *_: (0,) * nd)


def _rows(tm, w):
    return pl.BlockSpec((tm, w), lambda i: (i, 0))


class _Exchange:
    def __init__(self, ins, outs, aliases, scratch, start, finish):
        self.ins, self.outs, self.aliases, self.scratch = list(ins), list(outs), dict(aliases), list(scratch)
        self.start, self.finish = start, finish


def _both(a, b):
    na, ma, sa = len(a.ins), len(a.outs), len(a.scratch)

    def start(ci, co, cs):
        a.start(ci[:na], co[:ma], cs[:sa])
        b.start(ci[na:], co[ma:], cs[sa:])

    def finish(ci, co, cs):
        a.finish(ci[:na], co[:ma], cs[:sa])
        b.finish(ci[na:], co[ma:], cs[sa:])

    aliases = {**a.aliases, **{na + i: ma + j for i, j in b.aliases.items()}}
    return _Exchange(a.ins + b.ins, a.outs + b.outs, aliases, a.scratch + b.scratch, start, finish)


def _call(body, args, *, name, grid, in_specs, out_specs, out_shape, scratch_shapes=(), sem=None, comm=None):
    single = not isinstance(out_shape, (list, tuple))
    out_shape = [out_shape] if single else list(out_shape)
    out_specs = [out_specs] if single else list(out_specs)
    if comm is None:
        res = pl.pallas_call(body, name=name, grid=grid, in_specs=list(in_specs), out_specs=out_specs,
                             out_shape=out_shape, scratch_shapes=list(scratch_shapes),
                             compiler_params=_cparams(sem))(*args)
        return (res[0] if single else res), []
    n_in, n_out, n_scr = len(args), len(out_shape), len(scratch_shapes)
    nci, nco = len(comm.ins), len(comm.outs)
    steps = 1
    for g in grid:
        steps *= g

    def hosted(*refs):
        a, ci = refs[:n_in], refs[n_in:n_in + nci]
        o, co = refs[n_in + nci:n_in + nci + n_out], refs[n_in + nci + n_out:n_in + nci + n_out + nco]
        rest = refs[n_in + nci + n_out + nco:]
        scr, cs = rest[:n_scr], rest[n_scr:]
        step = pl.program_id(0)
        for d in range(1, len(grid)):
            step = step * grid[d] + pl.program_id(d)

        @pl.when(step == 0)
        def _():
            comm.start(ci, co, cs)

        body(*a, *o, *scr)

        @pl.when(step == steps - 1)
        def _():
            comm.finish(ci, co, cs)

    res = pl.pallas_call(
        hosted, name=name, grid=grid, in_specs=list(in_specs) + [ANY] * nci, out_specs=out_specs + [ANY] * nco,
        out_shape=out_shape + comm.outs, scratch_shapes=list(scratch_shapes) + comm.scratch,
        input_output_aliases={n_in + i: n_out + j for i, j in comm.aliases.items()},
        compiler_params=_cparams(("arbitrary",) * len(grid)),
    )(*args, *comm.ins)
    own = res[:n_out]
    return (own[0] if single else own), list(res[n_out:])


def _run(comm, name):
    nci = len(comm.ins)

    def body(*refs):
        ci, co, cs = refs[:nci], refs[nci:nci + len(comm.outs)], refs[nci + len(comm.outs):]
        comm.start(ci, co, cs)
        comm.finish(ci, co, cs)

    return pl.pallas_call(
        body, name=name, in_specs=[ANY] * nci, out_specs=[ANY] * len(comm.outs), out_shape=comm.outs,
        scratch_shapes=comm.scratch, input_output_aliases=comm.aliases,
        compiler_params=pltpu.CompilerParams(vmem_limit_bytes=VMEM_LIMIT),
    )(*comm.ins)


def _gelu(x):
    x2 = x * x
    t = jnp.tanh(x * (_GELU_C0 + (_GELU_C0 * _GELU_C1) * x2))
    hx = 0.5 * x
    return hx + hx * t, (t, x2, hx)


def _gelu_grad(parts):
    t, x2, hx = parts
    return (0.5 + 0.5 * t) + hx * (1.0 - t * t) * (_GELU_C0 + (3.0 * _GELU_C0 * _GELU_C1) * x2)


def _sigmoid(x):
    return 1.0 / (1.0 + jnp.exp(-x))


def _rms_hat(x):
    r = lax.rsqrt(jnp.mean(x * x, axis=-1, keepdims=True) + EPS)
    return x * r, r


def _rms_bwd(xhat, r, g, dout):
    dg = jnp.sum(dout * xhat, axis=0, keepdims=True)
    dy = dout * g
    dx = r * (dy - xhat * jnp.mean(dy * xhat, axis=-1, keepdims=True))
    return dx, dg


def _dot(a, b):
    return jnp.dot(a, b, preferred_element_type=F32)


def _dot_nt(a, b):
    return lax.dot_general(a, b, (((1,), (1,)), ((), ())), preferred_element_type=F32)


def _dot_tn(a, b):
    return lax.dot_general(a, b, (((0,), (0,)), ((), ())), preferred_element_type=F32)


def _rope(blk, c, s1, s2):
    return blk * c + pltpu.roll(blk, CH - ROPE_HALF, 1) * s1 + pltpu.roll(blk, ROPE_HALF, 1) * s2


def _rope_t(blk, c, s1, s2):
    return blk * c + pltpu.roll(blk * s1, ROPE_HALF, 1) + pltpu.roll(blk * s2, CH - ROPE_HALF, 1)


def _inproj(x, g1, w_in, rc, rs1, rs2, tm, comm=None):
    T = x.shape[0]

    def body(x_ref, g_ref, w_ref, c_ref, s1_ref, s2_ref,
             h_ref, u_ref, v_ref, q_ref, k_ref, va_ref, ga_ref, gb_ref):
        xhat, _ = _rms_hat(x_ref[...])
        h = (xhat * g_ref[...]).astype(BF16)
        h_ref[...] = h
        uv = _dot(h, w_ref[:, SEG[0]:SEG[2]])
        u_ref[...] = uv[:, :D]
        v_ref[...] = uv[:, D:]
        c, s1, s2 = c_ref[...], s1_ref[...], s2_ref[...]
        qkv = _dot(h, w_ref[:, SEG[2]:SEG[5]])
        for p in range(D // CH):
            blk = _rope(qkv[:, CH * p:CH * (p + 1)], c, s1, s2) * Q_SCALE
            q_ref[:, CH * p:CH * (p + 1)] = blk.astype(BF16)
        for p in range(KVW // CH):
            k_ref[:, CH * p:CH * (p + 1)] = _rope(qkv[:, D + CH * p:D + CH * (p + 1)], c, s1, s2).astype(BF16)
        va_ref[...] = qkv[:, D + KVW:].astype(BF16)
        gates = _dot(h, w_ref[:, SEG[5]:SEG[7]]).astype(BF16)
        ga_ref[...] = gates[:, :D]
        gb_ref[...] = gates[:, D:]

    sd = jax.ShapeDtypeStruct
    return _call(
        body, (x, g1, w_in, rc, rs1, rs2), name="inproj_fwd", grid=(T // tm,),
        in_specs=[_rows(tm, D), _const((1, D)), _resident((D, IN_W)), _rows(tm, CH), _rows(tm, CH), _rows(tm, CH)],
        out_specs=[_rows(tm, D), _rows(tm, D), _rows(tm, D), _rows(tm, D), _rows(tm, KVW), _rows(tm, KVW),
                   _rows(tm, D), _rows(tm, D)],
        out_shape=[sd((T, D), BF16), sd((T, D), F32), sd((T, D), F32), sd((T, D), BF16), sd((T, KVW), BF16),
                   sd((T, KVW), BF16), sd((T, D), BF16), sd((T, D), BF16)],
        sem=("parallel",), comm=comm)


def _sgu_common(u, vs, lng, lnb, ws_ref, bfull):
    nc = u.shape[0] // CH
    ug, tu = _gelu(u)
    vg, tv = _gelu(vs)
    mu = jnp.mean(vg, axis=-1, keepdims=True)
    xc = vg - mu
    rstd = lax.rsqrt(jnp.mean(xc * xc, axis=-1, keepdims=True) + EPS)
    vhat = xc * rstd
    vnb = (vhat * lng + lnb).astype(BF16)
    tri = lax.broadcasted_iota(jnp.int32, (CH, CH), 0) >= lax.broadcasted_iota(jnp.int32, (CH, CH), 1)
    wts, rhss, mixed = [], [], []
    for g in range(NG):
        wt = jnp.where(tri, ws_ref[g], 0.0).astype(BF16)
        rhs = jnp.concatenate([vnb[CH * c:CH * (c + 1), CH * g:CH * (g + 1)] for c in range(nc)], axis=1)
        mix = _dot(wt, rhs)
        wts.append(wt)
        rhss.append(rhs)
        mixed.append([mix[:, CH * c:CH * (c + 1)] + bfull[:, CH * g:CH * (g + 1)] for c in range(nc)])
    return nc, ug, tu, tv, rstd, vhat, tri, wts, rhss, mixed


def _sgu_fwd(u, vs, lng, lnb, ws, bfull, tm, comm=None):
    T = u.shape[0]

    def body(u_ref, v_ref, lng_ref, lnb_ref, ws_ref, bf_ref, a_ref):
        nc, ug, _, _, _, _, _, _, _, mixed = _sgu_common(
            u_ref[...], v_ref[...], lng_ref[...], lnb_ref[...], ws_ref, bf_ref[...])
        mixed_all = jnp.concatenate(
            [jnp.concatenate([mixed[g][c] for g in range(NG)], axis=1) for c in range(nc)], axis=0)
        a_ref[...] = (ug * mixed_all).astype(BF16)

    return _call(
        body, (u, vs, lng, lnb, ws, bfull), name="sgu_fwd", grid=(T // tm,),
        in_specs=[_rows(tm, D), _rows(tm, D), _const((1, D)), _const((1, D)), _const((NG, CH, CH)), _const((CH, D))],
        out_specs=_rows(tm, D), out_shape=jax.ShapeDtypeStruct((T, D), BF16), sem=("parallel",), comm=comm)


def _sgu_bwd(u, vs, da, lng, lnb, ws, bfull, tm, comm=None):
    T = u.shape[0]
    nsteps = T // tm

    def body(u_ref, v_ref, da_ref, lng_ref, lnb_ref, ws_ref, bf_ref,
             du_ref, dv_ref, dws_ref, dbs_ref, dlg_ref, dlb_ref, db_ref):
        i = pl.program_id(0)
        u, vs, da, lng = u_ref[...], v_ref[...], da_ref[...], lng_ref[...]
        nc, ug, tu, tv, rstd, vhat, tri, wts, rhss, mixed = _sgu_common(u, vs, lng, lnb_ref[...], ws_ref, bf_ref[...])

        @pl.when(i == 0)
        def _():
            dws_ref[...] = jnp.zeros_like(dws_ref)
            db_ref[...] = jnp.zeros_like(db_ref)
            dlg_ref[...] = jnp.zeros_like(dlg_ref)
            dlb_ref[...] = jnp.zeros_like(dlb_ref)

        mixed_all = jnp.concatenate(
            [jnp.concatenate([mixed[g][c] for g in range(NG)], axis=1) for c in range(nc)], axis=0)
        du_ref[...] = (da * mixed_all * _gelu_grad(tu)).astype(BF16)
        dmixed = da * ug
        dvn_cols = []
        for g in range(NG):
            dmix = [dmixed[CH * c:CH * (c + 1), CH * g:CH * (g + 1)] for c in range(nc)]
            db_ref[:, CH * g:CH * (g + 1)] += functools.reduce(lambda a, b: a + b, dmix)
            dm = jnp.concatenate(dmix, axis=1).astype(BF16)
            dws_ref[g] += _dot_nt(dm, rhss[g])
            dvn_cols.append(_dot_tn(wts[g], dm))
        dvn = jnp.concatenate(
            [jnp.concatenate([dvn_cols[g][:, CH * c:CH * (c + 1)] for g in range(NG)], axis=1) for c in range(nc)],
            axis=0)
        dlg_ref[...] += jnp.sum(dvn * vhat, axis=0, keepdims=True)
        dlb_ref[...] += jnp.sum(dvn, axis=0, keepdims=True)
        dvh = dvn * lng
        dvg = rstd * (dvh - jnp.mean(dvh, axis=-1, keepdims=True)
                      - vhat * jnp.mean(dvh * vhat, axis=-1, keepdims=True))
        dv_ref[...] = (dvg * _gelu_grad(tv)).astype(BF16)

        @pl.when(i == nsteps - 1)
        def _():
            for g in range(NG):
                dws_ref[g] = jnp.where(tri, dws_ref[g], 0.0)
                dbs_ref[g:g + 1, :] = jnp.sum(db_ref[:, CH * g:CH * (g + 1)].T, axis=0, keepdims=True)

    sd = jax.ShapeDtypeStruct
    return _call(
        body, (u, vs, da, lng, lnb, ws, bfull), name="sgu_bwd", grid=(nsteps,),
        in_specs=[_rows(tm, D), _rows(tm, D), _rows(tm, D), _const((1, D)), _const((1, D)), _const((NG, CH, CH)),
                  _const((CH, D))],
        out_specs=[_rows(tm, D), _rows(tm, D), _const((NG, CH, CH)), _const((NG, CH)), _const((1, D)), _const((1, D))],
        out_shape=[sd((T, D), BF16), sd((T, D), BF16), sd((NG, CH, CH), F32), sd((NG, CH), F32), sd((1, D), F32),
                   sd((1, D), F32)],
        scratch_shapes=[pltpu.VMEM((CH, D), F32)], sem=("arbitrary",), comm=comm)


def _pair_layout(prev, cur, grp):
    j, half = grp // 2, grp % 2
    blk = jnp.concatenate([prev[:, CH * j:CH * (j + 1)], cur[:, CH * j:CH * (j + 1)]], axis=0).astype(F32)
    lo = lax.broadcasted_iota(jnp.int32, blk.shape, 1) < HD
    rolled = pltpu.roll(blk, HD, 1)
    even = jnp.where(lo, blk if half == 0 else rolled, 0.0)
    odd = jnp.where(lo, 0.0, rolled if half == 0 else blk)
    return jnp.concatenate([even, odd], axis=0).astype(BF16)


def _attn_mask(n):
    qi = lax.broadcasted_iota(jnp.int32, (CH, 2 * CH), 0)
    kc = lax.broadcasted_iota(jnp.int32, (CH, 2 * CH), 1)
    ok = (kc > qi) & (kc <= qi + CH) & ((kc >= CH) | (n > 0))
    return jnp.concatenate([ok, ok], axis=1)


def _softmax_sink(s, sink):
    m = jnp.maximum(jnp.max(s, axis=-1, keepdims=True), sink)
    p = jnp.exp(s - m)
    ps = jnp.exp(sink - m)
    inv = 1.0 / (jnp.sum(p, axis=-1, keepdims=True) + ps)
    return p * inv, ps * inv


QUERY_BLOCKS_PER_STEP = 2


def _attn_fwd(q, k, va, sinks, comm=None):
    T = q.shape[0]
    nblk = QUERY_BLOCKS_PER_STEP
    nsteps = T // (nblk * CH)
    npairs = D // CH

    def body(sk_ref, q_ref, kp_ref, kc_ref, vp_ref, vc_ref, o_ref):
        n = pl.program_id(0)
        even_lanes = lax.broadcasted_iota(jnp.int32, (CH, CH), 1) < HD
        ks = [kp_ref[...]] + [kc_ref[CH * b:CH * (b + 1)] for b in range(nblk)]
        vs = [vp_ref[...]] + [vc_ref[CH * b:CH * (b + 1)] for b in range(nblk)]
        masks = [_attn_mask(nblk * n)] + [_attn_mask(1)] * (nblk - 1)
        kks = [[_pair_layout(ks[b], ks[b + 1], grp) for grp in range(NKV)] for b in range(nblk)]
        vvs = [[_pair_layout(vs[b], vs[b + 1], grp) for grp in range(NKV)] for b in range(nblk)]
        work = [(b, p) for b in range(nblk) for p in range(npairs)]

        def scores(i):
            b, p = work[i]
            return _dot_nt(q_ref[CH * b:CH * (b + 1), CH * p:CH * (p + 1)], kks[b][p // 2])

        def unnormalised(s, sink):
            m = jnp.maximum(jnp.max(s, axis=-1, keepdims=True), sink)
            p = jnp.exp(s - m)
            return p, 1.0 / (jnp.sum(p, axis=-1, keepdims=True) + jnp.exp(sink - m))

        def value_product(i):
            b, p = work[i]
            pr, ie, io = probs[i]
            return _dot(pr, vvs[b][p // 2]) * jnp.where(even_lanes, ie, io)

        ahead = 3
        outs, probs = [], []
        pending = [scores(i) for i in range(ahead)]
        for i, (b, p) in enumerate(work):
            s = jnp.where(masks[b], pending.pop(0), -1e30)
            if i + ahead < len(work):
                pending.append(scores(i + ahead))
            pe, ie = unnormalised(s[:, :2 * CH], sk_ref[2 * p])
            po, io = unnormalised(s[:, 2 * CH:], sk_ref[2 * p + 1])
            probs.append((jnp.concatenate([pe, po], axis=1).astype(BF16), ie, io))
            if i >= 1:
                outs.append(value_product(i - 1))
        outs.append(value_product(len(work) - 1))
        for b in range(nblk):
            o_ref[CH * b:CH * (b + 1), :] = jnp.concatenate(outs[npairs * b:npairs * (b + 1)], axis=1).astype(BF16)

    prev = lambda n: (jnp.maximum(nblk * n - 1, 0), 0)
    cur = lambda n: (n, 0)
    return _call(
        body, (sinks, q, k, k, va, va), name="attn_fwd", grid=(nsteps,),
        in_specs=[pl.BlockSpec(memory_space=pltpu.SMEM), pl.BlockSpec((nblk * CH, D), cur),
                  pl.BlockSpec((CH, KVW), prev), pl.BlockSpec((nblk * CH, KVW), cur),
                  pl.BlockSpec((CH, KVW), prev), pl.BlockSpec((nblk * CH, KVW), cur)],
        out_specs=pl.BlockSpec((nblk * CH, D), cur), out_shape=jax.ShapeDtypeStruct((T, D), BF16),
        sem=("parallel",), comm=comm)


def _attn_bwd(q, k, va, datt, sinks, rc, rs1, rs2, comm=None):
    T = q.shape[0]
    nb = T // CH

    def body(sk_ref, q_ref, kp_ref, kc_ref, vp_ref, vc_ref, do_ref, cq_ref, s1q_ref, s2q_ref, ck_ref, s1k_ref, s2k_ref,
             dq_ref, dk_ref, dv_ref, dsk_ref, kcar, vcar):
        n = pl.program_id(0)

        @pl.when(n == 0)
        def _():
            kcar[...] = jnp.zeros_like(kcar)
            vcar[...] = jnp.zeros_like(vcar)
            dsk_ref[...] = jnp.zeros_like(dsk_ref)

        def flush(kprev, vprev):
            ck, s1k, s2k = ck_ref[...], s1k_ref[...], s2k_ref[...]
            for j in range(KVW // CH):
                sl = slice(CH * j, CH * (j + 1))
                dk_ref[:, sl] = _rope_t(kcar[:, sl] + kprev[:, sl], ck, s1k, s2k).astype(BF16)
                dv_ref[:, sl] = (vcar[:, sl] + vprev[:, sl]).astype(BF16)

        @pl.when(n < nb)
        def _():
            mask = _attn_mask(n)
            kp, kc, vp, vc = kp_ref[...], kc_ref[...], vp_ref[...], vc_ref[...]
            cq, s1q, s2q = cq_ref[...], s1q_ref[...], s2q_ref[...]
            lane = lax.broadcasted_iota(jnp.int32, (1, CH), 1)
            dsk = jnp.zeros((1, CH), F32)
            npairs = D // CH
            kks = [_pair_layout(kp, kc, grp) for grp in range(NKV)]
            vvs = [_pair_layout(vp, vc, grp) for grp in range(NKV)]
            qs = [q_ref[:, CH * p:CH * (p + 1)] for p in range(npairs)]
            dos = [do_ref[:, CH * p:CH * (p + 1)].astype(BF16) for p in range(npairs)]

            def first(p):
                return _dot_nt(qs[p], kks[p // 2]), _dot_nt(dos[p], vvs[p // 2])

            def last(p, ds, pb):
                return (_rope_t(_dot(ds, kks[p // 2]), cq, s1q, s2q) * Q_SCALE, _dot_tn(qs[p], ds), _dot_tn(dos[p], pb))

            ahead = 2
            pending = [first(p) for p in range(ahead)]
            mids, ends = [], []
            for p in range(npairs):
                s, dp = pending.pop(0)
                s = jnp.where(mask, s, -1e30)
                if p + ahead < npairs:
                    pending.append(first(p + ahead))
                ds_parts, p_parts = [], []
                for par in range(2):
                    sl = slice(2 * CH * par, 2 * CH * (par + 1))
                    pr, psink = _softmax_sink(s[:, sl], sk_ref[2 * p + par])
                    delta = jnp.sum(pr * dp[:, sl], axis=-1, keepdims=True)
                    ds_parts.append(pr * (dp[:, sl] - delta))
                    p_parts.append(pr)
                    tot = -jnp.sum(psink * delta, axis=0, keepdims=True)
                    dsk = dsk + jnp.where(lane == 2 * p + par, tot, 0.0)
                mids.append((jnp.concatenate(ds_parts, axis=1).astype(BF16), jnp.concatenate(p_parts, axis=1).astype(BF16)))
                if p >= 1:
                    ends.append(last(p - 1, *mids[p - 1]))
            ends.append(last(npairs - 1, *mids[-1]))
            dq_cols = [e[0] for e in ends]
            def fold(i):
                rows = []
                for grp in range(NKV):
                    acc = ends[2 * grp][i] + ends[2 * grp + 1][i]
                    rows.append(acc[:HD, :2 * CH] + acc[HD:, 2 * CH:])
                return jnp.concatenate(rows, axis=0).T

            dkf, dvf = fold(1), fold(2)
            dq_ref[...] = jnp.concatenate(dq_cols, axis=1).astype(BF16)
            dsk_ref[...] += dsk
            flush(dkf[:CH], dvf[:CH])
            kcar[...] = dkf[CH:]
            vcar[...] = dvf[CH:]

        @pl.when(n == nb)
        def _():
            z = jnp.zeros((CH, KVW), F32)
            flush(z, z)

    last = nb - 1
    cur = lambda n: (jnp.minimum(n, last), 0)
    prev = lambda n: (jnp.clip(n - 1, 0, last), 0)
    sd = jax.ShapeDtypeStruct
    return _call(
        body, (sinks, q, k, k, va, va, datt, rc, rs1, rs2, rc, rs1, rs2), name="attn_bwd", grid=(nb + 1,),
        in_specs=[pl.BlockSpec(memory_space=pltpu.SMEM), pl.BlockSpec((CH, D), cur),
                  pl.BlockSpec((CH, KVW), prev), pl.BlockSpec((CH, KVW), cur),
                  pl.BlockSpec((CH, KVW), prev), pl.BlockSpec((CH, KVW), cur),
                  pl.BlockSpec((CH, D), cur),
                  pl.BlockSpec((CH, CH), cur), pl.BlockSpec((CH, CH), cur), pl.BlockSpec((CH, CH), cur),
                  pl.BlockSpec((CH, CH), prev), pl.BlockSpec((CH, CH), prev), pl.BlockSpec((CH, CH), prev)],
        out_specs=[pl.BlockSpec((CH, D), cur), pl.BlockSpec((CH, KVW), prev), pl.BlockSpec((CH, KVW), prev),
                   _const((1, CH))],
        out_shape=[sd((T, D), BF16), sd((T, KVW), BF16), sd((T, KVW), BF16), sd((1, CH), F32)],
        scratch_shapes=[pltpu.VMEM((CH, KVW), F32), pltpu.VMEM((CH, KVW), F32)], sem=("arbitrary",), comm=comm)


def _merge_fwd(a, att, ga, gb, x, w_a, w_b, w_o, g2, tm, comm=None):
    T = x.shape[0]

    def body(a_ref, att_ref, ga_ref, gb_ref, x_ref, wa_ref, wb_ref, wo_ref, g_ref,
             pa_ref, pb_ref, mg_ref, mix_ref, x1_ref):
        pa = _dot(a_ref[...], wa_ref[...])
        pb = _dot(att_ref[...], wb_ref[...])
        pa_ref[...] = pa.astype(BF16)
        pb_ref[...] = pb.astype(BF16)
        merged = (_sigmoid(ga_ref[...].astype(F32)) * pa + _sigmoid(gb_ref[...].astype(F32)) * pb).astype(BF16)
        mg_ref[...] = merged
        mix = _dot(merged, wo_ref[...])
        mix_ref[...] = mix
        mhat, _ = _rms_hat(mix)
        x1_ref[...] = x_ref[...] + mhat * g_ref[...]

    sd = jax.ShapeDtypeStruct
    return _call(
        body, (a, att, ga, gb, x, w_a, w_b, w_o, g2), name="merge_fwd", grid=(T // tm,),
        in_specs=[_rows(tm, D)] * 5 + [_resident((D, D))] * 3 + [_const((1, D))],
        out_specs=[_rows(tm, D)] * 5,
        out_shape=[sd((T, D), BF16), sd((T, D), BF16), sd((T, D), BF16), sd((T, D), F32), sd((T, D), F32)],
        sem=("parallel",), comm=comm)


def _merge_bwd(dx1, mix, ga, gb, pa, pb, a, att, merged, w_a, w_b, w_o, g2, tm, comm=None):
    T = dx1.shape[0]
    nsteps = T // tm

    def body(dx1_ref, mix_ref, ga_ref, gb_ref, pa_ref, pb_ref, a_ref, att_ref, mg_ref, wa_ref, wb_ref, wo_ref, g_ref,
             dga_ref, dgb_ref, da_ref, datt_ref, dg_ref, dwa_ref, dwb_ref, dwo_ref, acc, sem):
        i = pl.program_id(0)

        @pl.when(i == 0)
        def _():
            dg_ref[...] = jnp.zeros_like(dg_ref)
            acc[...] = jnp.zeros_like(acc)

        mhat, r = _rms_hat(mix_ref[...])
        dmix, dg = _rms_bwd(mhat, r, g_ref[...], dx1_ref[...])
        dg_ref[...] += dg
        dmix = dmix.astype(BF16)
        dmerged = _dot_nt(dmix, wo_ref[...])
        sa = _sigmoid(ga_ref[...].astype(F32))
        sb = _sigmoid(gb_ref[...].astype(F32))
        dao = (dmerged * sa).astype(BF16)
        dbo = (dmerged * sb).astype(BF16)
        dga_ref[...] = (dmerged * pa_ref[...].astype(F32) * (sa * (1.0 - sa))).astype(BF16)
        dgb_ref[...] = (dmerged * pb_ref[...].astype(F32) * (sb * (1.0 - sb))).astype(BF16)
        da_ref[...] = _dot_nt(dao, wa_ref[...])
        datt_ref[...] = _dot_nt(dbo, wb_ref[...]).astype(BF16)
        acc[0] += _dot_tn(a_ref[...], dao)
        acc[1] += _dot_tn(att_ref[...], dbo)
        acc[2] += _dot_tn(mg_ref[...], dmix)

        @pl.when(i == nsteps - 1)
        def _():
            outs = [pltpu.make_async_copy(acc.at[j], ref, sem.at[j]) for j, ref in enumerate((dwa_ref, dwb_ref, dwo_ref))]
            for cp in outs:
                cp.start()
            for cp in outs:
                cp.wait()

    sd = jax.ShapeDtypeStruct
    return _call(
        body, (dx1, mix, ga, gb, pa, pb, a, att, merged, w_a, w_b, w_o, g2), name="merge_bwd", grid=(nsteps,),
        in_specs=[_rows(tm, D)] * 9 + [_resident((D, D))] * 3 + [_const((1, D))],
        out_specs=[_rows(tm, D)] * 4 + [_const((1, D))] + [ANY] * 3,
        out_shape=[sd((T, D), BF16), sd((T, D), BF16), sd((T, D), F32), sd((T, D), BF16), sd((1, D), F32)]
        + [sd((D, D), F32)] * 3,
        scratch_shapes=[pltpu.VMEM((3, D, D), F32), _dma_sems(3)], sem=("arbitrary",), comm=comm)


def _ffn(x1, target, w1, w2, g3, g4, tm):
    T = x1.shape[0]

    def body(x_ref, t_ref, w1_ref, w2_ref, g3_ref, g4_ref,
             hf_ref, f2_ref, dff_ref, df1_ref, dx_ref, ls_ref, dg3_ref, dg4_ref):
        @pl.when(pl.program_id(0) == 0)
        def _():
            ls_ref[...] = jnp.zeros_like(ls_ref)
            dg3_ref[...] = jnp.zeros_like(dg3_ref)
            dg4_ref[...] = jnp.zeros_like(dg4_ref)

        x = x_ref[...]
        g3, g4 = g3_ref[...], g4_ref[...]
        xhat, r3 = _rms_hat(x)
        hf = (xhat * g3).astype(BF16)
        hf_ref[...] = hf
        rl = jnp.maximum(_dot(hf, w1_ref[...]), 0.0)
        f2 = (rl * rl).astype(BF16)
        f2_ref[...] = f2
        fhat, r4 = _rms_hat(_dot(f2, w2_ref[...]))
        err = x + fhat * g4 - t_ref[...]
        ls_ref[...] += jnp.sum(err * err, axis=0, keepdims=True)
        dy = err * (1.0 / D)
        dff, dg4 = _rms_bwd(fhat, r4, g4, dy)
        dg4_ref[...] += dg4
        dff = dff.astype(BF16)
        dff_ref[...] = dff
        df1 = (_dot_nt(dff, w2_ref[...]) * (2.0 * rl)).astype(BF16)
        df1_ref[...] = df1
        dxn, dg3 = _rms_bwd(xhat, r3, g3, _dot_nt(df1, w1_ref[...]))
        dg3_ref[...] += dg3
        dx_ref[...] = dy + dxn

    sd = jax.ShapeDtypeStruct
    return pl.pallas_call(
        body, name="ffn_fwd_bwd", grid=(T // tm,),
        in_specs=[_rows(tm, D), _rows(tm, D), _resident((D, DFF)), _resident((DFF, D)), _const((1, D)), _const((1, D))],
        out_specs=[_rows(tm, D), _rows(tm, DFF), _rows(tm, D), _rows(tm, DFF), _rows(tm, D), _const((1, D)),
                   _const((1, D)), _const((1, D))],
        out_shape=[sd((T, D), BF16), sd((T, DFF), BF16), sd((T, D), BF16), sd((T, DFF), BF16), sd((T, D), F32),
                   sd((1, D), F32), sd((1, D), F32), sd((1, D), F32)],
        compiler_params=pltpu.CompilerParams(vmem_limit_bytes=VMEM_PHYSICAL, dimension_semantics=("arbitrary",)),
    )(x1, target, w1, w2, g3, g4)


def _inproj_bwd(parts, x, dx1, g1, w_in, tm, comm=None):
    T = x.shape[0]
    widths = [p.shape[1] for p in parts]
    offs = [sum(widths[:i]) for i in range(len(widths) + 1)]
    assert offs[-1] == IN_W

    def body(*refs):
        n = len(parts)
        prefs = refs[:n]
        x_ref, dx1_ref, g_ref, w_ref, dx_ref, dp_ref, dg_ref = refs[n:]

        @pl.when(pl.program_id(0) == 0)
        def _():
            dg_ref[...] = jnp.zeros_like(dg_ref)

        for i in range(n):
            dp_ref[:, offs[i]:offs[i + 1]] = prefs[i][...]
        dh = _dot_nt(dp_ref[...], w_ref[...])
        xhat, r = _rms_hat(x_ref[...])
        dxn, dg = _rms_bwd(xhat, r, g_ref[...], dh)
        dg_ref[...] += dg
        dx_ref[...] = dx1_ref[...] + dxn

    sd = jax.ShapeDtypeStruct
    return _call(
        body, (*parts, x, dx1, g1, w_in), name="inproj_bwd", grid=(T // tm,),
        in_specs=[_rows(tm, w) for w in widths] + [_rows(tm, D), _rows(tm, D), _const((1, D)), _resident((D, IN_W))],
        out_specs=[_rows(tm, D), _rows(tm, IN_W), _const((1, D))],
        out_shape=[sd((T, D), F32), sd((T, IN_W), BF16), sd((1, D), F32)], sem=("arbitrary",), comm=comm)


def _wgrad(a, g, tn, tm, name, comm=None):
    T, K = a.shape
    N = g.shape[1]

    def body(a_ref, g_ref, o_ref):
        @pl.when(pl.program_id(1) == 0)
        def _():
            o_ref[...] = jnp.zeros_like(o_ref)

        o_ref[...] += _dot_tn(a_ref[...], g_ref[...])

    return _call(
        body, (a, g), name=name, grid=(N // tn, T // tm),
        in_specs=[pl.BlockSpec((tm, K), lambda j, t: (t, 0)), pl.BlockSpec((tm, tn), lambda j, t: (t, j))],
        out_specs=pl.BlockSpec((K, tn), lambda j, t: (0, j)),
        out_shape=jax.ShapeDtypeStruct((K, N), F32), sem=("parallel", "arbitrary"), comm=comm)


def _adamw(ws, gs, ms, vs, trs, name):
    n = len(ws)
    walk = _Walk(w.shape[0] // tr for w, tr in zip(ws, trs))
    bc1 = 1.0 / (1.0 - B1 ** STEP)
    bc2 = 1.0 / (1.0 - B2 ** STEP)

    def body(*refs):
        i = pl.program_id(0)
        for k in range(n):
            mine = tuple(refs[j * n + k] for j in range(8))

            @pl.when(walk.mine(k, i))
            def _(mine=mine):
                w_ref, g_ref, m_ref, v_ref, go_ref, d_ref, nm_ref, nv_ref = mine
                g = g_ref[...]
                go_ref[...] = g
                m = B1 * m_ref[...] + (1.0 - B1) * g
                v = B2 * v_ref[...] + (1.0 - B2) * (g * g)
                nm_ref[...] = m
                nv_ref[...] = v
                d_ref[...] = -LR * ((m * bc1) / (jnp.sqrt(v * bc2) + AEPS) + WD * w_ref[...])

    def spec(k):
        return pl.BlockSpec((trs[k], ws[k].shape[1]), lambda i: (walk.tile(k, i), 0))

    specs = [spec(k) for k in range(n)]
    res = pl.pallas_call(
        body, name=name, grid=(walk.steps,), in_specs=specs * 4, out_specs=specs * 4,
        out_shape=[jax.ShapeDtypeStruct(w.shape, F32) for w in ws] * 4,
        compiler_params=_cparams(("arbitrary",)),
    )(*ws, *gs, *ms, *vs)
    return [tuple(res[j * n + k] for j in range(4)) for k in range(n)]


BIG = (("col", (D, IN_W)), ("row", (D, D)), ("row", (D, D)), ("row", (D, D)), ("col", (D, DFF)), ("row", (DFF, D)))
NBIG = len(BIG)
ANY = pl.BlockSpec(memory_space=pl.ANY)


def _shard_shape(kind, shape):
    R, C = shape
    return (R, C // 4) if kind == "col" else (R // 4, C)


def _half_shape(kind, shape):
    R, C = shape
    return (R // 2, C) if kind == "col" else (R, C // 2)


def _piece_shape(kind, shape):
    R, C = shape
    return (R // 2, C // 4) if kind == "col" else (R // 4, C // 2)


def _own_region(ref, kind, shape, s):
    R, C = shape
    return ref.at[:, pl.ds(s * (C // 4), C // 4)] if kind == "col" else ref.at[pl.ds(s * (R // 4), R // 4), :]


def _ag_region(ref, kind, shape, s, hc):
    R, C = shape
    if kind == "col":
        return ref.at[pl.ds(hc * (R // 2), R // 2), pl.ds(s * (C // 4), C // 4)]
    return ref.at[pl.ds(s * (R // 4) + hc * (R // 8), R // 8), :]


def _ag_shard_half(ref, kind, shape, hc):
    R, C = shape
    return ref.at[pl.ds(hc * (R // 2), R // 2), :] if kind == "col" else ref.at[pl.ds(hc * (R // 8), R // 8), :]


def _grad_half(ref, kind, shape, hc):
    R, C = shape
    return ref.at[pl.ds(hc * (R // 2), R // 2), :] if kind == "col" else ref.at[:, pl.ds(hc * (C // 2), C // 2)]


def _half_piece(ref, kind, shape, s):
    R, C = shape
    return ref.at[:, pl.ds(s * (C // 4), C // 4)] if kind == "col" else ref.at[pl.ds(s * (R // 4), R // 4), :]


def _place():
    x, y, c = lax.axis_index("x"), lax.axis_index("y"), lax.axis_index("c")
    chips = [(1 - x, y), (x, 1 - y), (1 - x, 1 - y)]
    return x, y, c, chips


def _rcopy(src, dst, ssem, rsem, dev):
    return pltpu.make_async_remote_copy(src_ref=src, dst_ref=dst, send_sem=ssem, recv_sem=rsem,
                                        device_id=dev, device_id_type=MESH)


def _dma_sems(n):
    return pltpu.SemaphoreType.DMA((n,))


def _x_gather_ici(shards, ws):
    n = len(ws)
    specs = [BIG[w] for w in ws]

    def place():
        x, y, c, chips = _place()
        return c, chips, 2 * x + y

    def sends(sh, full, sc):
        c, chips, me_s = place()
        return [_rcopy(_ag_shard_half(sh[i], kind, shape, c), _ag_region(full[i], kind, shape, me_s, c),
                       sc[0].at[3 * i + j], sc[1].at[3 * i + j], (cx, cy, c))
                for i, (kind, shape) in enumerate(specs) for j, (cx, cy) in enumerate(chips)]

    def start(sh, full, sc):
        for i in range(n):
            pltpu.make_async_copy(sh[i], sc[4 + i], sc[2].at[i]).start()
        for cp in sends(sh, full, sc):
            cp.start()

    def finish(sh, full, sc):
        c, chips, me_s = place()
        stores = []
        for i, (kind, shape) in enumerate(specs):
            pltpu.make_async_copy(sh[i], sc[4 + i], sc[2].at[i]).wait()
            st = pltpu.make_async_copy(sc[4 + i], _own_region(full[i], kind, shape, me_s), sc[3].at[i])
            st.start()
            stores.append(st)
        for i, (kind, shape) in enumerate(specs):
            for j, (cx, cy) in enumerate(chips):
                reg = _ag_region(full[i], kind, shape, 2 * cx + cy, c)
                _rcopy(reg, reg, sc[0].at[3 * i + j], sc[1].at[3 * i + j], (cx, cy, c)).wait_recv()
        for cp in sends(sh, full, sc):
            cp.wait_send()
        for st in stores:
            st.wait()

    return _Exchange(
        shards, [jax.ShapeDtypeStruct(shape, BF16) for _, shape in specs], {},
        [_dma_sems(3 * n), _dma_sems(3 * n), _dma_sems(n), _dma_sems(n)]
        + [pltpu.VMEM(_shard_shape(k, s), BF16) for k, s in specs], start, finish)


def _x_gather_d2d(wholes, ws):
    specs = [BIG[w] for w in ws]
    n = len(ws)

    def copies(full, sc, mine):
        x, y, c, chips = _place()
        hc = c if mine else 1 - c
        return [_rcopy(reg, reg, sc[0].at[3 * i + j], sc[1].at[3 * i + j], (x, y, 1 - c))
                for i, (kind, shape) in enumerate(specs) for j, (cx, cy) in enumerate(chips)
                for reg in [_ag_region(full[i], kind, shape, 2 * cx + cy, hc)]]

    def start(_, full, sc):
        for cp in copies(full, sc, True):
            cp.start()

    def finish(_, full, sc):
        for cp in copies(full, sc, False):
            cp.wait_recv()
        for cp in copies(full, sc, True):
            cp.wait_send()

    return _Exchange(wholes, [jax.ShapeDtypeStruct(shape, BF16) for _, shape in specs], {i: i for i in range(n)},
                     [_dma_sems(3 * n), _dma_sems(3 * n)], start, finish)


def _x_grads_sibling(grads, ws):
    specs = [BIG[w] for w in ws]
    n = len(ws)

    def copies(g, got, sc):
        x, y, c, _ = _place()
        return [_rcopy(_grad_half(g[i], kind, shape, 1 - c), got[i], sc[0].at[i], sc[1].at[i], (x, y, 1 - c))
                for i, (kind, shape) in enumerate(specs)]

    def start(g, got, sc):
        for cp in copies(g, got, sc):
            cp.start()

    def finish(g, got, sc):
        for cp in copies(g, got, sc):
            cp.wait_recv()
        for cp in copies(g, got, sc):
            cp.wait_send()

    return _Exchange(grads, [jax.ShapeDtypeStruct(_half_shape(k, s), F32) for k, s in specs], {},
                     [_dma_sems(n), _dma_sems(n)], start, finish)


def _x_grads_chips(sums_bf, ws):
    specs = [BIG[w] for w in ws]
    n = len(ws)

    def copies(s16, got, sc):
        x, y, c, chips = _place()
        return [_rcopy(_half_piece(s16[i], kind, shape, 2 * cx + cy), got[i].at[j],
                       sc[0].at[3 * i + j], sc[1].at[3 * i + j], (cx, cy, c))
                for i, (kind, shape) in enumerate(specs) for j, (cx, cy) in enumerate(chips)]

    def start(s16, got, sc):
        for cp in copies(s16, got, sc):
            cp.start()

    def finish(s16, got, sc):
        for cp in copies(s16, got, sc):
            cp.wait_recv()
        for cp in copies(s16, got, sc):
            cp.wait_send()

    return _Exchange(sums_bf, [jax.ShapeDtypeStruct((3,) + _piece_shape(k, s), BF16) for k, s in specs], {},
                     [_dma_sems(3 * n), _dma_sems(3 * n)], start, finish)


def _shard_half(ref, kind, shape, hc):
    sr, sc = _shard_shape(kind, shape)
    return ref.at[pl.ds(hc * (sr // 2), sr // 2), :] if kind == "col" else ref.at[:, pl.ds(hc * (sc // 2), sc // 2)]


def _x_grads_share(shard_grads, ws):
    specs = [BIG[w] for w in ws]
    n = len(ws)

    def copies(g, sc, mine):
        x, y, c, _ = _place()
        hc = c if mine else 1 - c
        return [_rcopy(part, part, sc[0].at[i], sc[1].at[i], (x, y, 1 - c))
                for i, (kind, shape) in enumerate(specs) for part in [_shard_half(g[i], kind, shape, hc)]]

    def start(_, g, sc):
        for cp in copies(g, sc, True):
            cp.start()

    def finish(_, g, sc):
        for cp in copies(g, sc, False):
            cp.wait_recv()
        for cp in copies(g, sc, True):
            cp.wait_send()

    return _Exchange(shard_grads, [jax.ShapeDtypeStruct(_shard_shape(k, s), F32) for k, s in specs],
                     {i: i for i in range(n)}, [_dma_sems(n), _dma_sems(n)], start, finish)


ADD_BLOCK_BYTES = 4 * 1024 * 1024


def _add_rows(rows, cols, n_arrays):
    limit = ADD_BLOCK_BYTES // (1 if n_arrays == 1 else 4)
    r = rows
    while r > 64 and r * cols * 4 > limit:
        r //= 2
    return r


class _Walk:
    def __init__(self, tiles):
        self.tiles = list(tiles)
        self.starts = [sum(self.tiles[:k]) for k in range(len(self.tiles))]
        self.steps = sum(self.tiles)

    def tile(self, k, i):
        return jnp.clip(i - self.starts[k], 0, self.tiles[k] - 1)

    def mine(self, k, i):
        return (i >= self.starts[k]) & (i < self.starts[k] + self.tiles[k])


def _add_halves(place, gs, gots, kinds, name):
    n = len(gs)
    halves = [_half_shape(kind, g.shape) for g, kind in zip(gs, kinds)]
    rows = [_add_rows(hr, hc, n) for hr, hc in halves]
    walk = _Walk(hr // r for (hr, _), r in zip(halves, rows))

    def body(p_ref, *refs):
        i = pl.program_id(0)
        for k in range(n):
            g_ref, b_ref, s_ref, sb_ref = (refs[j * n + k] for j in range(4))

            @pl.when(walk.mine(k, i))
            def _(g_ref=g_ref, b_ref=b_ref, s_ref=s_ref, sb_ref=sb_ref):
                s = g_ref[...] + b_ref[...]
                s_ref[...] = s
                sb_ref[...] = s.astype(BF16)

    def g_spec(k):
        if kinds[k] == "col":
            return pl.BlockSpec((rows[k], gs[k].shape[1]), lambda i, p: (p[0] * walk.tiles[k] + walk.tile(k, i), 0))
        return pl.BlockSpec((rows[k], halves[k][1]), lambda i, p: (walk.tile(k, i), p[0]))

    def spec(k):
        return pl.BlockSpec((rows[k], halves[k][1]), lambda i, p: (walk.tile(k, i), 0))

    specs = [spec(k) for k in range(n)]
    res = pl.pallas_call(
        body, name=name,
        grid_spec=pltpu.PrefetchScalarGridSpec(num_scalar_prefetch=1, grid=(walk.steps,),
                                               in_specs=[g_spec(k) for k in range(n)] + specs, out_specs=specs + specs),
        out_shape=[jax.ShapeDtypeStruct(h, F32) for h in halves] + [jax.ShapeDtypeStruct(h, BF16) for h in halves],
        compiler_params=_cparams(("arbitrary",)),
    )(place, *gs, *gots)
    return [(res[k], res[n + k]) for k in range(n)]


def _add_pieces(place, halves, gots, specs_big, name):
    n = len(halves)
    pieces = [_piece_shape(kind, shape) for kind, shape in specs_big]
    rows = [_add_rows(pr, pc, n) for pr, pc in pieces]
    walk = _Walk(pr // r for (pr, _), r in zip(pieces, rows))

    def body(p_ref, *refs):
        i = pl.program_id(0)
        for k in range(n):
            m_ref, g_ref, o_ref = (refs[j * n + k] for j in range(3))

            @pl.when(walk.mine(k, i))
            def _(m_ref=m_ref, g_ref=g_ref, o_ref=o_ref):
                acc = m_ref[...]
                for j in range(3):
                    acc = acc + g_ref[j].astype(F32)
                o_ref[...] = acc

    def m_spec(k):
        if specs_big[k][0] == "col":
            return pl.BlockSpec((rows[k], pieces[k][1]), lambda i, p: (walk.tile(k, i), p[1]))
        return pl.BlockSpec((rows[k], pieces[k][1]), lambda i, p: (p[1] * walk.tiles[k] + walk.tile(k, i), 0))

    def got_spec(k):
        return pl.BlockSpec((3, rows[k], pieces[k][1]), lambda i, p: (0, walk.tile(k, i), 0))

    def o_spec(k):
        if specs_big[k][0] == "col":
            return pl.BlockSpec((rows[k], pieces[k][1]), lambda i, p: (p[0] * walk.tiles[k] + walk.tile(k, i), 0))
        return pl.BlockSpec((rows[k], pieces[k][1]), lambda i, p: (walk.tile(k, i), p[0]))

    return pl.pallas_call(
        body, name=name,
        grid_spec=pltpu.PrefetchScalarGridSpec(
            num_scalar_prefetch=1, grid=(walk.steps,),
            in_specs=[m_spec(k) for k in range(n)] + [got_spec(k) for k in range(n)],
            out_specs=[o_spec(k) for k in range(n)]),
        out_shape=[jax.ShapeDtypeStruct(_shard_shape(kind, shape), F32) for kind, shape in specs_big],
        compiler_params=_cparams(("arbitrary",)),
    )(place, *halves, *gots)


SMALL_ROWS = 1024 + 8 * 8 + 8


def _x_small_all_reduce(p):
    def parts(p_ref, sc):
        slots, ssem, rsem = sc[0], sc[2], sc[3]
        x, y, c = lax.axis_index("x"), lax.axis_index("y"), lax.axis_index("c")
        me = 4 * x + 2 * y + c
        out = []
        for r in range(1, 8):
            bx, by, bc = (r >> 2) & 1, (r >> 1) & 1, r & 1
            tgt = (1 - x if bx else x, 1 - y if by else y, 1 - c if bc else c)
            send = _rcopy(p_ref, slots.at[me], ssem.at[r - 1], rsem.at[r - 1], tgt)
            src = 4 * tgt[0] + 2 * tgt[1] + tgt[2]
            recv = _rcopy(p_ref, slots.at[src], ssem.at[r - 1], rsem.at[r - 1], tgt)
            out.append((send, recv))
        return me, out

    def start(ins, outs, sc):
        me, cps = parts(ins[0], sc)
        pltpu.make_async_copy(ins[0], sc[0].at[me], sc[4].at[0]).start()
        for send, _ in cps:
            send.start()

    def finish(ins, outs, sc):
        me, cps = parts(ins[0], sc)
        pltpu.make_async_copy(ins[0], sc[0].at[me], sc[4].at[0]).wait()
        for _, recv in cps:
            recv.wait_recv()
        acc = sc[0][0]
        for d in range(1, 8):
            acc = acc + sc[0][d]
        sc[1][...] = acc
        back = pltpu.make_async_copy(sc[1], outs[0], sc[4].at[1])
        back.start()
        for send, _ in cps:
            send.wait_send()
        back.wait()

    return _Exchange([p], [jax.ShapeDtypeStruct((SMALL_ROWS, CH), F32)], {},
                     [pltpu.VMEM((8, SMALL_ROWS, CH), F32), pltpu.VMEM((SMALL_ROWS, CH), F32), _dma_sems(7), _dma_sems(7),
                      _dma_sems(2)], start, finish)


def _rope_tables(positions, comm=None):
    T = positions.shape[0]
    inv_freq = 500000.0 ** (-jnp.arange(0, 2 * ROPE_HALF, 2, dtype=F32) / (2 * ROPE_HALF))
    head = jnp.concatenate([inv_freq, inv_freq, jnp.zeros((HD - 2 * ROPE_HALF,), F32)])
    lane_freq = jnp.concatenate([head, head])[None, :]
    pos = jnp.broadcast_to(positions.astype(F32)[:, None], (T, CH))
    tm = min(1024, T)

    def body(p_ref, f_ref, c_ref, s1_ref, s2_ref):
        ang = p_ref[...] * f_ref[...]
        sin = jnp.sin(ang)
        first = (lax.broadcasted_iota(jnp.int32, ang.shape, 1) % HD) < ROPE_HALF
        c_ref[...] = jnp.cos(ang)
        s1_ref[...] = jnp.where(first, -sin, 0.0)
        s2_ref[...] = jnp.where(first, 0.0, sin)

    return _call(body, (pos, lane_freq), name="rope_tables", grid=(T // tm,),
                 in_specs=[_rows(tm, CH), _const((1, CH))], out_specs=[_rows(tm, CH)] * 3,
                 out_shape=[jax.ShapeDtypeStruct((T, CH), F32)] * 3, sem=("parallel",), comm=comm)


BIG_NAMES = ("w_in", "w_a", "w_b", "w_o", "w_ff_in", "w_ff_out")
SMALL_NAMES = ("w_spatial", "ln_v_gain", "ln_v_bias", "b_spatial", "sinks", "norm_mix_pre", "norm_mix_post",
               "norm_ff_pre", "norm_ff_post")
WEIGHT_ORDER = ("w_in", "ln_v_gain", "ln_v_bias", "w_spatial", "b_spatial", "sinks", "w_a", "w_b", "w_o",
                "norm_mix_pre", "norm_mix_post", "w_ff_in", "w_ff_out", "norm_ff_pre", "norm_ff_post")


def _pack_small(d, loss_sums=None):
    parts = []
    for n in SMALL_NAMES:
        flat = d[n].reshape(-1)
        pad = (-flat.shape[0]) % (8 * CH)
        parts.append(jnp.pad(flat, (0, pad)).reshape(-1, CH))
    parts.append(jnp.zeros((8, CH), F32) if loss_sums is None else loss_sums.reshape(8, CH))
    return jnp.concatenate(parts, axis=0)


def _unpack_small(p, like):
    out, row = {}, 0
    for n in SMALL_NAMES:
        size = like[n].size
        rows = -(-size // (8 * CH)) * 8
        out[n] = p[row:row + rows].reshape(-1)[:size].reshape(like[n].shape)
        row += rows
    return out


def kernel(x, positions, w_in, ln_v_gain, ln_v_bias, w_spatial, b_spatial, sinks, w_a, w_b, w_o, norm_mix_pre, norm_mix_post, w_ff_in, w_ff_out, norm_ff_pre, norm_ff_post, loss_target, m_w_in, m_ln_v_gain, m_ln_v_bias, m_w_spatial, m_b_spatial, m_sinks, m_w_a, m_w_b, m_w_o, m_norm_mix_pre, m_norm_mix_post, m_w_ff_in, m_w_ff_out, m_norm_ff_pre, m_norm_ff_post, v_w_in, v_ln_v_gain, v_ln_v_bias, v_w_spatial, v_b_spatial, v_sinks, v_w_a, v_w_b, v_w_o, v_norm_mix_pre, v_norm_mix_post, v_w_ff_in, v_w_ff_out, v_norm_ff_pre, v_norm_ff_post):
    w = dict(w_in=w_in, ln_v_gain=ln_v_gain, ln_v_bias=ln_v_bias, w_spatial=w_spatial, b_spatial=b_spatial, sinks=sinks,
             w_a=w_a, w_b=w_b, w_o=w_o, norm_mix_pre=norm_mix_pre, norm_mix_post=norm_mix_post, w_ff_in=w_ff_in,
             w_ff_out=w_ff_out, norm_ff_pre=norm_ff_pre, norm_ff_post=norm_ff_post)
    m = dict(w_in=m_w_in, ln_v_gain=m_ln_v_gain, ln_v_bias=m_ln_v_bias, w_spatial=m_w_spatial, b_spatial=m_b_spatial,
             sinks=m_sinks, w_a=m_w_a, w_b=m_w_b, w_o=m_w_o, norm_mix_pre=m_norm_mix_pre, norm_mix_post=m_norm_mix_post,
             w_ff_in=m_w_ff_in, w_ff_out=m_w_ff_out, norm_ff_pre=m_norm_ff_pre, norm_ff_post=m_norm_ff_post)
    v = dict(w_in=v_w_in, ln_v_gain=v_ln_v_gain, ln_v_bias=v_ln_v_bias, w_spatial=v_w_spatial, b_spatial=v_b_spatial,
             sinks=v_sinks, w_a=v_w_a, w_b=v_w_b, w_o=v_w_o, norm_mix_pre=v_norm_mix_pre, norm_mix_post=v_norm_mix_post,
             w_ff_in=v_w_ff_in, w_ff_out=v_w_ff_out, norm_ff_pre=v_norm_ff_pre, norm_ff_post=v_norm_ff_post)

    FIRST, REST = (0,), tuple(range(1, NBIG))
    shards = [w[n][0].astype(BF16) for n in BIG_NAMES]
    place = jnp.stack([lax.axis_index("c"), 2 * lax.axis_index("x") + lax.axis_index("y")]).astype(jnp.int32)
    xs, target = x[0], loss_target[0]
    T = xs.shape[0]
    wtm, wtm2 = min(1024, T), min(2048, T)
    g1, g2, g3, g4 = norm_mix_pre, norm_mix_post, norm_ff_pre, norm_ff_post
    w_sp, snk = w_spatial[0], sinks[0]
    MIX = (1, 2, 3)
    bfull = jnp.repeat(b_spatial[0].T, CH, axis=1)

    def reduce_tail(ws, grads, got):
        tag = "_".join(BIG_NAMES[k] for k in ws)
        sums = _add_halves(place, grads, got, [BIG[k][0] for k in ws], name="grad_add_sibling_" + tag)
        return sums, _x_grads_chips([s[1] for s in sums], ws)

    def reduce_end(ws, sums, pieces):
        tag = "_".join(BIG_NAMES[k] for k in ws)
        return _add_pieces(place, [s[0] for s in sums], pieces, [BIG[k] for k in ws], name="grad_add_chips_" + tag)

    (rc, rs1, rs2), w_in_part = _rope_tables(positions[0], comm=_x_gather_ici(shards[:1], FIRST))
    w_in_b = _run(_x_gather_d2d(w_in_part, FIRST), "gather_w_in_d2d")[0]
    FF_IN, FF_OUT = (4,), (5,)
    (h, u, vs, q, k, va, ga, gb), ffi_part = _inproj(xs, g1, w_in_b, rc, rs1, rs2, tm=512,
                                                    comm=_x_gather_ici(shards[4:5], FF_IN))
    a, mix_part = _sgu_fwd(u, vs, ln_v_gain, ln_v_bias, w_sp, bfull, tm=512, comm=_x_gather_ici(shards[1:4], MIX))
    att, (*rest, ffo_part) = _attn_fwd(
        q, k, va, snk, comm=_both(_both(_x_gather_d2d(mix_part, MIX), _x_gather_d2d(ffi_part, FF_IN)),
                                  _x_gather_ici(shards[5:6], FF_OUT)))
    w_a_b, w_b_b, w_o_b, w_ff_in_b = rest
    (pa, pb, merged, mix, x1), (w_ff_out_b,) = _merge_fwd(a, att, ga, gb, xs, w_a_b, w_b_b, w_o_b, g2, tm=512,
                                                          comm=_x_gather_d2d([ffo_part], FF_OUT))
    hf, f2, dff, df1, dx1, lsum, dg3, dg4 = _ffn(x1, target, w_ff_in_b, w_ff_out_b, g3, g4, tm=512)

    dw_ff_out, _ = _wgrad(f2, dff, tn=1024, tm=512, name="wgrad_ff_out")
    dw_ff_in, _ = _wgrad(hf, df1, tn=2048, tm=wtm2, name="wgrad_ff_in")
    (dga, dgb, da, datt, dg2, dw_a, dw_b, dw_o), _ = _merge_bwd(
        dx1, mix, ga, gb, pa, pb, a, att, merged, w_a_b, w_b_b, w_o_b, g2, tm=512)
    grads_rest = [dw_a, dw_b, dw_o, dw_ff_in, dw_ff_out]
    (du, dvs, dws, dbs, dlg, dlb), got_rest = _sgu_bwd(
        u, vs, da, ln_v_gain, ln_v_bias, w_sp, bfull, tm=512, comm=_x_grads_sibling(grads_rest, REST))
    sums_rest, to_chips = reduce_tail(REST, grads_rest, got_rest)
    (dq, dk, dva, dsk), pieces_rest = _attn_bwd(q, k, va, datt, snk, rc, rs1, rs2, comm=to_chips)
    partial_rest = reduce_end(REST, sums_rest, pieces_rest)
    (dx, dproj, dg1), _ = _inproj_bwd([du, dvs, dq, dk, dva, dga, dgb], xs, dx1, g1, w_in_b, tm=512)
    small = dict(ln_v_gain=dlg, ln_v_bias=dlb, w_spatial=dws, b_spatial=dbs, sinks=dsk[:, :NQ],
                 norm_mix_pre=dg1, norm_mix_post=dg2, norm_ff_pre=dg3, norm_ff_post=dg4)
    dw_in, (gs, *shard_rest) = _wgrad(
        h, dproj, tn=IN_W // 2, tm=wtm, name="wgrad_in",
        comm=_both(_x_small_all_reduce(_pack_small(small, lsum)), _x_grads_share(partial_rest, REST)))
    got_in = _run(_x_grads_sibling([dw_in], FIRST), "grads_in_to_sibling")
    sums_in, to_chips = reduce_tail(FIRST, [dw_in], got_in)
    partial_in = reduce_end(FIRST, sums_in, _run(to_chips, "grads_in_to_chips"))
    g_in = _run(_x_grads_share(partial_in, FIRST), "grads_in_share")[0]

    loss = 0.5 * jnp.sum(gs[SMALL_ROWS - 8:]) / D
    grad, delta, new_m, new_v = {}, {}, {}, {}
    for n, g in zip(BIG_NAMES, [g_in] + list(shard_rest)):
        (g_, d_, m_, v_), = _adamw([w[n][0]], [g], [m[n][0]], [v[n][0]], [256], name="adamw_" + n)
        grad[n], delta[n], new_m[n], new_v[n] = g_[None], d_[None], m_[None], v_[None]
    (gs, ds, ms, vs), = _adamw([_pack_small(w)], [gs], [_pack_small(m)], [_pack_small(v)], [SMALL_ROWS], name="adamw_small")
    for packed, dst in ((gs, grad), (ds, delta), (ms, new_m), (vs, new_v)):
        dst.update(_unpack_small(packed, w))

    outs = [loss, dx[None]]
    for group in (grad, delta, new_m, new_v):
        outs.extend(group[n] for n in WEIGHT_ORDER)
    return tuple(outs)
```

```python
import functools

import jax
import jax.numpy as jnp
from jax import lax
from jax.experimental import pallas as pl
from jax.experimental.pallas import tpu as pltpu

F32 = jnp.float32
BF16 = jnp.bfloat16

D = 1024
CH = 128
NG = 8
HD = 64
NQ = 16
NKV = 4
KVW = NKV * HD
DFF = 4 * D
EPS = 1e-6
IN_W = 5632
SEG = (0, 1024, 2048, 3072, 3328, 3584, 4608, 5632)
ROPE_HALF = 8
Q_SCALE = HD ** -0.5

LR, B1, B2, AEPS, WD, STEP = 0.001, 0.9, 0.999, 1e-08, 0.01, 10

VMEM_PHYSICAL = 64 * 1024 * 1024
VMEM_LIMIT = 60 * 1024 * 1024
MESH = pl.DeviceIdType.MESH

_GELU_C0 = 0.7978845608028654
_GELU_C1 = 0.044715


def _cparams(sem=None, vmem=None):
    kw = dict(vmem_limit_bytes=VMEM_LIMIT if vmem is None else vmem)
    if sem is not None:
        kw["dimension_semantics"] = sem
    return pltpu.CompilerParams(**kw)


def _resident(shape):
    nd = len(shape)
    return pl.BlockSpec(shape, lambda *_: (0,) * nd, pipeline_mode=pl.Buffered(1))


def _const(shape):
    nd = len(shape)
    return pl.BlockSpec(shape, lambda *_: (0,) * nd)


def _rows(tm, w):
    return pl.BlockSpec((tm, w), lambda i: (i, 0))


class _Exchange:
    def __init__(self, ins, outs, aliases, scratch, start, finish):
        self.ins, self.outs, self.aliases, self.scratch = list(ins), list(outs), dict(aliases), list(scratch)
        self.start, self.finish = start, finish


def _both(a, b):
    na, ma, sa = len(a.ins), len(a.outs), len(a.scratch)

    def start(ci, co, cs):
        a.start(ci[:na], co[:ma], cs[:sa])
        b.start(ci[na:], co[ma:], cs[sa:])

    def finish(ci, co, cs):
        a.finish(ci[:na], co[:ma], cs[:sa])
        b.finish(ci[na:], co[ma:], cs[sa:])

    aliases = {**a.aliases, **{na + i: ma + j for i, j in b.aliases.items()}}
    return _Exchange(a.ins + b.ins, a.outs + b.outs, aliases, a.scratch + b.scratch, start, finish)


def _call(body, args, *, name, grid, in_specs, out_specs, out_shape, scratch_shapes=(), sem=None, comm=None, vmem=None):
    single = not isinstance(out_shape, (list, tuple))
    out_shape = [out_shape] if single else list(out_shape)
    out_specs = [out_specs] if single else list(out_specs)
    if comm is None:
        res = pl.pallas_call(body, name=name, grid=grid, in_specs=list(in_specs), out_specs=out_specs,
                             out_shape=out_shape, scratch_shapes=list(scratch_shapes),
                             compiler_params=_cparams(sem, vmem))(*args)
        return (res[0] if single else res), []
    n_in, n_out, n_scr = len(args), len(out_shape), len(scratch_shapes)
    nci, nco = len(comm.ins), len(comm.outs)
    steps = 1
    for g in grid:
        steps *= g

    def hosted(*refs):
        a, ci = refs[:n_in], refs[n_in:n_in + nci]
        o, co = refs[n_in + nci:n_in + nci + n_out], refs[n_in + nci + n_out:n_in + nci + n_out + nco]
        rest = refs[n_in + nci + n_out + nco:]
        scr, cs = rest[:n_scr], rest[n_scr:]
        step = pl.program_id(0)
        for d in range(1, len(grid)):
            step = step * grid[d] + pl.program_id(d)

        @pl.when(step == 0)
        def _():
            comm.start(ci, co, cs)

        body(*a, *o, *scr)

        @pl.when(step == steps - 1)
        def _():
            comm.finish(ci, co, cs)

    res = pl.pallas_call(
        hosted, name=name, grid=grid, in_specs=list(in_specs) + [ANY] * nci, out_specs=out_specs + [ANY] * nco,
        out_shape=out_shape + comm.outs, scratch_shapes=list(scratch_shapes) + comm.scratch,
        input_output_aliases={n_in + i: n_out + j for i, j in comm.aliases.items()},
        compiler_params=_cparams(("arbitrary",) * len(grid), vmem),
    )(*args, *comm.ins)
    own = res[:n_out]
    return (own[0] if single else own), list(res[n_out:])


def _run(comm, name):
    nci = len(comm.ins)

    def body(*refs):
        ci, co, cs = refs[:nci], refs[nci:nci + len(comm.outs)], refs[nci + len(comm.outs):]
        comm.start(ci, co, cs)
        comm.finish(ci, co, cs)

    return pl.pallas_call(
        body, name=name, in_specs=[ANY] * nci, out_specs=[ANY] * len(comm.outs), out_shape=comm.outs,
        scratch_shapes=comm.scratch, input_output_aliases=comm.aliases,
        compiler_params=pltpu.CompilerParams(vmem_limit_bytes=VMEM_LIMIT),
    )(*comm.ins)


def _gelu(x):
    x2 = x * x
    t = jnp.tanh(x * (_GELU_C0 + (_GELU_C0 * _GELU_C1) * x2))
    hx = 0.5 * x
    return hx + hx * t, (t, x2, hx)


def _gelu_grad(parts):
    t, x2, hx = parts
    return (0.5 + 0.5 * t) + hx * (1.0 - t * t) * (_GELU_C0 + (3.0 * _GELU_C0 * _GELU_C1) * x2)


def _sigmoid(x):
    return 1.0 / (1.0 + jnp.exp(-x))


def _rms_hat(x):
    r = lax.rsqrt(jnp.mean(x * x, axis=-1, keepdims=True) + EPS)
    return x * r, r


def _rms_bwd(xhat, r, g, dout):
    dg = jnp.sum(dout * xhat, axis=0, keepdims=True)
    dy = dout * g
    dx = r * (dy - xhat * jnp.mean(dy * xhat, axis=-1, keepdims=True))
    return dx, dg


def _dot(a, b):
    return jnp.dot(a, b, preferred_element_type=F32)


def _dot_nt(a, b):
    return lax.dot_general(a, b, (((1,), (1,)), ((), ())), preferred_element_type=F32)


def _dot_tn(a, b):
    return lax.dot_general(a, b, (((0,), (0,)), ((), ())), preferred_element_type=F32)


def _rope(blk, c, s1, s2):
    return blk * c + pltpu.roll(blk, CH - ROPE_HALF, 1) * s1 + pltpu.roll(blk, ROPE_HALF, 1) * s2


def _rope_t(blk, c, s1, s2):
    return blk * c + pltpu.roll(blk * s1, ROPE_HALF, 1) + pltpu.roll(blk * s2, CH - ROPE_HALF, 1)


def _inproj(x, g1, w_in, rc, rs1, rs2, tm, comm=None):
    T = x.shape[0]

    def body(x_ref, g_ref, w_ref, c_ref, s1_ref, s2_ref,
             h_ref, u_ref, v_ref, q_ref, k_ref, va_ref, ga_ref, gb_ref):
        xhat, _ = _rms_hat(x_ref[...])
        h = (xhat * g_ref[...]).astype(BF16)
        h_ref[...] = h
        uv = _dot(h, w_ref[:, SEG[0]:SEG[2]])
        u_ref[...] = uv[:, :D]
        v_ref[...] = uv[:, D:]
        c, s1, s2 = c_ref[...], s1_ref[...], s2_ref[...]
        qkv = _dot(h, w_ref[:, SEG[2]:SEG[5]])
        for p in range(D // CH):
            blk = _rope(qkv[:, CH * p:CH * (p + 1)], c, s1, s2) * Q_SCALE
            q_ref[:, CH * p:CH * (p + 1)] = blk.astype(BF16)
        for p in range(KVW // CH):
            k_ref[:, CH * p:CH * (p + 1)] = _rope(qkv[:, D + CH * p:D + CH * (p + 1)], c, s1, s2).astype(BF16)
        va_ref[...] = qkv[:, D + KVW:].astype(BF16)
        gates = _dot(h, w_ref[:, SEG[5]:SEG[7]]).astype(BF16)
        ga_ref[...] = gates[:, :D]
        gb_ref[...] = gates[:, D:]

    sd = jax.ShapeDtypeStruct
    return _call(
        body, (x, g1, w_in, rc, rs1, rs2), name="inproj_fwd", grid=(T // tm,),
        in_specs=[_rows(tm, D), _const((1, D)), _resident((D, IN_W)), _rows(tm, CH), _rows(tm, CH), _rows(tm, CH)],
        out_specs=[_rows(tm, D), _rows(tm, D), _rows(tm, D), _rows(tm, D), _rows(tm, KVW), _rows(tm, KVW),
                   _rows(tm, D), _rows(tm, D)],
        out_shape=[sd((T, D), BF16), sd((T, D), F32), sd((T, D), F32), sd((T, D), BF16), sd((T, KVW), BF16),
                   sd((T, KVW), BF16), sd((T, D), BF16), sd((T, D), BF16)],
        sem=("parallel",), comm=comm)


def _sgu_common(u, vs, lng, lnb, ws_ref, bfull):
    nc = u.shape[0] // CH
    ug, tu = _gelu(u)
    vg, tv = _gelu(vs)
    mu = jnp.mean(vg, axis=-1, keepdims=True)
    xc = vg - mu
    rstd = lax.rsqrt(jnp.mean(xc * xc, axis=-1, keepdims=True) + EPS)
    vhat = xc * rstd
    vnb = (vhat * lng + lnb).astype(BF16)
    tri = lax.broadcasted_iota(jnp.int32, (CH, CH), 0) >= lax.broadcasted_iota(jnp.int32, (CH, CH), 1)
    wts, rhss, mixed = [], [], []
    for g in range(NG):
        wt = jnp.where(tri, ws_ref[g], 0.0).astype(BF16)
        rhs = jnp.concatenate([vnb[CH * c:CH * (c + 1), CH * g:CH * (g + 1)] for c in range(nc)], axis=1)
        mix = _dot(wt, rhs)
        wts.append(wt)
        rhss.append(rhs)
        mixed.append([mix[:, CH * c:CH * (c + 1)] + bfull[:, CH * g:CH * (g + 1)] for c in range(nc)])
    return nc, ug, tu, tv, rstd, vhat, tri, wts, rhss, mixed


def _sgu_fwd(u, vs, lng, lnb, ws, bfull, tm, comm=None):
    T = u.shape[0]

    def body(u_ref, v_ref, lng_ref, lnb_ref, ws_ref, bf_ref, a_ref):
        nc, ug, _, _, _, _, _, _, _, mixed = _sgu_common(
            u_ref[...], v_ref[...], lng_ref[...], lnb_ref[...], ws_ref, bf_ref[...])
        mixed_all = jnp.concatenate(
            [jnp.concatenate([mixed[g][c] for g in range(NG)], axis=1) for c in range(nc)], axis=0)
        a_ref[...] = (ug * mixed_all).astype(BF16)

    return _call(
        body, (u, vs, lng, lnb, ws, bfull), name="sgu_fwd", grid=(T // tm,),
        in_specs=[_rows(tm, D), _rows(tm, D), _const((1, D)), _const((1, D)), _const((NG, CH, CH)), _const((CH, D))],
        out_specs=_rows(tm, D), out_shape=jax.ShapeDtypeStruct((T, D), BF16), sem=("parallel",), comm=comm)


def _sgu_bwd(u, vs, da, lng, lnb, ws, bfull, tm, comm=None):
    T = u.shape[0]
    nsteps = T // tm

    def body(u_ref, v_ref, da_ref, lng_ref, lnb_ref, ws_ref, bf_ref,
             du_ref, dv_ref, dws_ref, dbs_ref, dlg_ref, dlb_ref, db_ref):
        i = pl.program_id(0)
        u, vs, da, lng = u_ref[...], v_ref[...], da_ref[...], lng_ref[...]
        nc, ug, tu, tv, rstd, vhat, tri, wts, rhss, mixed = _sgu_common(u, vs, lng, lnb_ref[...], ws_ref, bf_ref[...])

        @pl.when(i == 0)
        def _():
            dws_ref[...] = jnp.zeros_like(dws_ref)
            db_ref[...] = jnp.zeros_like(db_ref)
            dlg_ref[...] = jnp.zeros_like(dlg_ref)
            dlb_ref[...] = jnp.zeros_like(dlb_ref)

        mixed_all = jnp.concatenate(
            [jnp.concatenate([mixed[g][c] for g in range(NG)], axis=1) for c in range(nc)], axis=0)
        du_ref[...] = (da * mixed_all * _gelu_grad(tu)).astype(BF16)
        dmixed = da * ug
        dvn_cols = []
        for g in range(NG):
            dmix = [dmixed[CH * c:CH * (c + 1), CH * g:CH * (g + 1)] for c in range(nc)]
            db_ref[:, CH * g:CH * (g + 1)] += functools.reduce(lambda a, b: a + b, dmix)
            dm = jnp.concatenate(dmix, axis=1).astype(BF16)
            dws_ref[g] += _dot_nt(dm, rhss[g])
            dvn_cols.append(_dot_tn(wts[g], dm))
        dvn = jnp.concatenate(
            [jnp.concatenate([dvn_cols[g][:, CH * c:CH * (c + 1)] for g in range(NG)], axis=1) for c in range(nc)],
            axis=0)
        dlg_ref[...] += jnp.sum(dvn * vhat, axis=0, keepdims=True)
        dlb_ref[...] += jnp.sum(dvn, axis=0, keepdims=True)
        dvh = dvn * lng
        dvg = rstd * (dvh - jnp.mean(dvh, axis=-1, keepdims=True)
                      - vhat * jnp.mean(dvh * vhat, axis=-1, keepdims=True))
        dv_ref[...] = (dvg * _gelu_grad(tv)).astype(BF16)

        @pl.when(i == nsteps - 1)
        def _():
            for g in range(NG):
                dws_ref[g] = jnp.where(tri, dws_ref[g], 0.0)
                dbs_ref[g:g + 1, :] = jnp.sum(db_ref[:, CH * g:CH * (g + 1)].T, axis=0, keepdims=True)

    sd = jax.ShapeDtypeStruct
    return _call(
        body, (u, vs, da, lng, lnb, ws, bfull), name="sgu_bwd", grid=(nsteps,),
        in_specs=[_rows(tm, D), _rows(tm, D), _rows(tm, D), _const((1, D)), _const((1, D)), _const((NG, CH, CH)),
                  _const((CH, D))],
        out_specs=[_rows(tm, D), _rows(tm, D), _const((NG, CH, CH)), _const((NG, CH)), _const((1, D)), _const((1, D))],
        out_shape=[sd((T, D), BF16), sd((T, D), BF16), sd((NG, CH, CH), F32), sd((NG, CH), F32), sd((1, D), F32),
                   sd((1, D), F32)],
        scratch_shapes=[pltpu.VMEM((CH, D), F32)], sem=("arbitrary",), comm=comm)


def _pair_layout(prev, cur, grp):
    j, half = grp // 2, grp % 2
    blk = jnp.concatenate([prev[:, CH * j:CH * (j + 1)], cur[:, CH * j:CH * (j + 1)]], axis=0).astype(F32)
    lo = lax.broadcasted_iota(jnp.int32, blk.shape, 1) < HD
    rolled = pltpu.roll(blk, HD, 1)
    even = jnp.where(lo, blk if half == 0 else rolled, 0.0)
    odd = jnp.where(lo, 0.0, rolled if half == 0 else blk)
    return jnp.concatenate([even, odd], axis=0).astype(BF16)


def _attn_mask(n):
    qi = lax.broadcasted_iota(jnp.int32, (CH, 2 * CH), 0)
    kc = lax.broadcasted_iota(jnp.int32, (CH, 2 * CH), 1)
    ok = (kc > qi) & (kc <= qi + CH) & ((kc >= CH) | (n > 0))
    return jnp.concatenate([ok, ok], axis=1)


def _softmax_sink(s, sink):
    m = jnp.maximum(jnp.max(s, axis=-1, keepdims=True), sink)
    p = jnp.exp(s - m)
    ps = jnp.exp(sink - m)
    inv = 1.0 / (jnp.sum(p, axis=-1, keepdims=True) + ps)
    return p * inv, ps * inv


QUERY_BLOCKS_PER_STEP = 2


def _attn_fwd(q, k, va, sinks, comm=None):
    T = q.shape[0]
    nblk = QUERY_BLOCKS_PER_STEP
    nsteps = T // (nblk * CH)
    npairs = D // CH

    def body(sk_ref, q_ref, kp_ref, kc_ref, vp_ref, vc_ref, o_ref):
        n = pl.program_id(0)
        even_lanes = lax.broadcasted_iota(jnp.int32, (CH, CH), 1) < HD
        ks = [kp_ref[...]] + [kc_ref[CH * b:CH * (b + 1)] for b in range(nblk)]
        vs = [vp_ref[...]] + [vc_ref[CH * b:CH * (b + 1)] for b in range(nblk)]
        masks = [_attn_mask(nblk * n)] + [_attn_mask(1)] * (nblk - 1)
        kks = [[_pair_layout(ks[b], ks[b + 1], grp) for grp in range(NKV)] for b in range(nblk)]
        vvs = [[_pair_layout(vs[b], vs[b + 1], grp) for grp in range(NKV)] for b in range(nblk)]
        work = [(b, p) for b in range(nblk) for p in range(npairs)]

        def scores(i):
            b, p = work[i]
            return _dot_nt(q_ref[CH * b:CH * (b + 1), CH * p:CH * (p + 1)], kks[b][p // 2])

        def unnormalised(s, sink):
            m = jnp.maximum(jnp.max(s, axis=-1, keepdims=True), sink)
            p = jnp.exp(s - m)
            return p, 1.0 / (jnp.sum(p, axis=-1, keepdims=True) + jnp.exp(sink - m))

        def value_product(i):
            b, p = work[i]
            pr, ie, io = probs[i]
            return _dot(pr, vvs[b][p // 2]) * jnp.where(even_lanes, ie, io)

        ahead = 3
        outs, probs = [], []
        pending = [scores(i) for i in range(ahead)]
        for i, (b, p) in enumerate(work):
            s = jnp.where(masks[b], pending.pop(0), -1e30)
            if i + ahead < len(work):
                pending.append(scores(i + ahead))
            pe, ie = unnormalised(s[:, :2 * CH], sk_ref[2 * p])
            po, io = unnormalised(s[:, 2 * CH:], sk_ref[2 * p + 1])
            probs.append((jnp.concatenate([pe, po], axis=1).astype(BF16), ie, io))
            if i >= 1:
                outs.append(value_product(i - 1))
        outs.append(value_product(len(work) - 1))
        for b in range(nblk):
            o_ref[CH * b:CH * (b + 1), :] = jnp.concatenate(outs[npairs * b:npairs * (b + 1)], axis=1).astype(BF16)

    prev = lambda n: (jnp.maximum(nblk * n - 1, 0), 0)
    cur = lambda n: (n, 0)
    return _call(
        body, (sinks, q, k, k, va, va), name="attn_fwd", grid=(nsteps,),
        in_specs=[pl.BlockSpec(memory_space=pltpu.SMEM), pl.BlockSpec((nblk * CH, D), cur),
                  pl.BlockSpec((CH, KVW), prev), pl.BlockSpec((nblk * CH, KVW), cur),
                  pl.BlockSpec((CH, KVW), prev), pl.BlockSpec((nblk * CH, KVW), cur)],
        out_specs=pl.BlockSpec((nblk * CH, D), cur), out_shape=jax.ShapeDtypeStruct((T, D), BF16),
        sem=("parallel",), comm=comm)


def _attn_bwd(q, k, va, datt, sinks, rc, rs1, rs2, comm=None):
    T = q.shape[0]
    nb = T // CH

    def body(sk_ref, q_ref, kp_ref, kc_ref, vp_ref, vc_ref, do_ref, cq_ref, s1q_ref, s2q_ref, ck_ref, s1k_ref, s2k_ref,
             dq_ref, dk_ref, dv_ref, dsk_ref, kcar, vcar):
        n = pl.program_id(0)

        @pl.when(n == 0)
        def _():
            kcar[...] = jnp.zeros_like(kcar)
            vcar[...] = jnp.zeros_like(vcar)
            dsk_ref[...] = jnp.zeros_like(dsk_ref)

        def flush(kprev, vprev):
            ck, s1k, s2k = ck_ref[...], s1k_ref[...], s2k_ref[...]
            for j in range(KVW // CH):
                sl = slice(CH * j, CH * (j + 1))
                dk_ref[:, sl] = _rope_t(kcar[:, sl] + kprev[:, sl], ck, s1k, s2k).astype(BF16)
                dv_ref[:, sl] = (vcar[:, sl] + vprev[:, sl]).astype(BF16)

        @pl.when(n < nb)
        def _():
            mask = _attn_mask(n)
            kp, kc, vp, vc = kp_ref[...], kc_ref[...], vp_ref[...], vc_ref[...]
            cq, s1q, s2q = cq_ref[...], s1q_ref[...], s2q_ref[...]
            lane = lax.broadcasted_iota(jnp.int32, (1, CH), 1)
            dsk = jnp.zeros((1, CH), F32)
            npairs = D // CH
            kks = [_pair_layout(kp, kc, grp) for grp in range(NKV)]
            vvs = [_pair_layout(vp, vc, grp) for grp in range(NKV)]
            qs = [q_ref[:, CH * p:CH * (p + 1)] for p in range(npairs)]
            dos = [do_ref[:, CH * p:CH * (p + 1)].astype(BF16) for p in range(npairs)]

            def first(p):
                return _dot_nt(qs[p], kks[p // 2]), _dot_nt(dos[p], vvs[p // 2])

            def last(p, ds, pb):
                return (_rope_t(_dot(ds, kks[p // 2]), cq, s1q, s2q) * Q_SCALE, _dot_tn(qs[p], ds), _dot_tn(dos[p], pb))

            ahead = 2
            pending = [first(p) for p in range(ahead)]
            mids, ends = [], []
            for p in range(npairs):
                s, dp = pending.pop(0)
                s = jnp.where(mask, s, -1e30)
                if p + ahead < npairs:
                    pending.append(first(p + ahead))
                ds_parts, p_parts = [], []
                for par in range(2):
                    sl = slice(2 * CH * par, 2 * CH * (par + 1))
                    pr, psink = _softmax_sink(s[:, sl], sk_ref[2 * p + par])
                    delta = jnp.sum(pr * dp[:, sl], axis=-1, keepdims=True)
                    ds_parts.append(pr * (dp[:, sl] - delta))
                    p_parts.append(pr)
                    tot = -jnp.sum(psink * delta, axis=0, keepdims=True)
                    dsk = dsk + jnp.where(lane == 2 * p + par, tot, 0.0)
                mids.append((jnp.concatenate(ds_parts, axis=1).astype(BF16), jnp.concatenate(p_parts, axis=1).astype(BF16)))
                if p >= 1:
                    ends.append(last(p - 1, *mids[p - 1]))
            ends.append(last(npairs - 1, *mids[-1]))
            dq_cols = [e[0] for e in ends]
            def fold(i):
                rows = []
                for grp in range(NKV):
                    acc = ends[2 * grp][i] + ends[2 * grp + 1][i]
                    rows.append(acc[:HD, :2 * CH] + acc[HD:, 2 * CH:])
                return jnp.concatenate(rows, axis=0).T

            dkf, dvf = fold(1), fold(2)
            dq_ref[...] = jnp.concatenate(dq_cols, axis=1).astype(BF16)
            dsk_ref[...] += dsk
            flush(dkf[:CH], dvf[:CH])
            kcar[...] = dkf[CH:]
            vcar[...] = dvf[CH:]

        @pl.when(n == nb)
        def _():
            z = jnp.zeros((CH, KVW), F32)
            flush(z, z)

    last = nb - 1
    cur = lambda n: (jnp.minimum(n, last), 0)
    prev = lambda n: (jnp.clip(n - 1, 0, last), 0)
    sd = jax.ShapeDtypeStruct
    return _call(
        body, (sinks, q, k, k, va, va, datt, rc, rs1, rs2, rc, rs1, rs2), name="attn_bwd", grid=(nb + 1,),
        in_specs=[pl.BlockSpec(memory_space=pltpu.SMEM), pl.BlockSpec((CH, D), cur),
                  pl.BlockSpec((CH, KVW), prev), pl.BlockSpec((CH, KVW), cur),
                  pl.BlockSpec((CH, KVW), prev), pl.BlockSpec((CH, KVW), cur),
                  pl.BlockSpec((CH, D), cur),
                  pl.BlockSpec((CH, CH), cur), pl.BlockSpec((CH, CH), cur), pl.BlockSpec((CH, CH), cur),
                  pl.BlockSpec((CH, CH), prev), pl.BlockSpec((CH, CH), prev), pl.BlockSpec((CH, CH), prev)],
        out_specs=[pl.BlockSpec((CH, D), cur), pl.BlockSpec((CH, KVW), prev), pl.BlockSpec((CH, KVW), prev),
                   _const((1, CH))],
        out_shape=[sd((T, D), BF16), sd((T, KVW), BF16), sd((T, KVW), BF16), sd((1, CH), F32)],
        scratch_shapes=[pltpu.VMEM((CH, KVW), F32), pltpu.VMEM((CH, KVW), F32)], sem=("arbitrary",), comm=comm)


def _merge_fwd(a, att, ga, gb, x, w_a, w_b, w_o, g2, tm, comm=None):
    T = x.shape[0]

    def body(a_ref, att_ref, ga_ref, gb_ref, x_ref, wa_ref, wb_ref, wo_ref, g_ref,
             pa_ref, pb_ref, mg_ref, mix_ref, x1_ref):
        pa = _dot(a_ref[...], wa_ref[...])
        pb = _dot(att_ref[...], wb_ref[...])
        pa_ref[...] = pa.astype(BF16)
        pb_ref[...] = pb.astype(BF16)
        merged = (_sigmoid(ga_ref[...].astype(F32)) * pa + _sigmoid(gb_ref[...].astype(F32)) * pb).astype(BF16)
        mg_ref[...] = merged
        mix = _dot(merged, wo_ref[...])
        mix_ref[...] = mix
        mhat, _ = _rms_hat(mix)
        x1_ref[...] = x_ref[...] + mhat * g_ref[...]

    sd = jax.ShapeDtypeStruct
    return _call(
        body, (a, att, ga, gb, x, w_a, w_b, w_o, g2), name="merge_fwd", grid=(T // tm,),
        in_specs=[_rows(tm, D)] * 5 + [_resident((D, D))] * 3 + [_const((1, D))],
        out_specs=[_rows(tm, D)] * 5,
        out_shape=[sd((T, D), BF16), sd((T, D), BF16), sd((T, D), BF16), sd((T, D), F32), sd((T, D), F32)],
        sem=("parallel",), comm=comm)


def _merge_bwd(dx1, mix, ga, gb, pa, pb, a, att, merged, w_a, w_b, w_o, g2, tm, comm=None):
    T = dx1.shape[0]
    nsteps = T // tm

    def body(dx1_ref, mix_ref, ga_ref, gb_ref, pa_ref, pb_ref, a_ref, att_ref, mg_ref, wa_ref, wb_ref, wo_ref, g_ref,
             dga_ref, dgb_ref, da_ref, datt_ref, dg_ref, dwa_ref, dwb_ref, dwo_ref, acc, sem):
        i = pl.program_id(0)

        @pl.when(i == 0)
        def _():
            dg_ref[...] = jnp.zeros_like(dg_ref)
            acc[...] = jnp.zeros_like(acc)

        mhat, r = _rms_hat(mix_ref[...])
        dmix, dg = _rms_bwd(mhat, r, g_ref[...], dx1_ref[...])
        dg_ref[...] += dg
        dmix = dmix.astype(BF16)
        dmerged = _dot_nt(dmix, wo_ref[...])
        sa = _sigmoid(ga_ref[...].astype(F32))
        sb = _sigmoid(gb_ref[...].astype(F32))
        dao = (dmerged * sa).astype(BF16)
        dbo = (dmerged * sb).astype(BF16)
        dga_ref[...] = (dmerged * pa_ref[...].astype(F32) * (sa * (1.0 - sa))).astype(BF16)
        dgb_ref[...] = (dmerged * pb_ref[...].astype(F32) * (sb * (1.0 - sb))).astype(BF16)
        da_ref[...] = _dot_nt(dao, wa_ref[...])
        datt_ref[...] = _dot_nt(dbo, wb_ref[...]).astype(BF16)
        acc[0] += _dot_tn(a_ref[...], dao)
        acc[1] += _dot_tn(att_ref[...], dbo)
        acc[2] += _dot_tn(mg_ref[...], dmix)

        @pl.when(i == nsteps - 1)
        def _():
            outs = [pltpu.make_async_copy(acc.at[j], ref, sem.at[j]) for j, ref in enumerate((dwa_ref, dwb_ref, dwo_ref))]
            for cp in outs:
                cp.start()
            for cp in outs:
                cp.wait()

    sd = jax.ShapeDtypeStruct
    return _call(
        body, (dx1, mix, ga, gb, pa, pb, a, att, merged, w_a, w_b, w_o, g2), name="merge_bwd", grid=(nsteps,),
        in_specs=[_rows(tm, D)] * 9 + [_resident((D, D))] * 3 + [_const((1, D))],
        out_specs=[_rows(tm, D)] * 4 + [_const((1, D))] + [ANY] * 3,
        out_shape=[sd((T, D), BF16), sd((T, D), BF16), sd((T, D), F32), sd((T, D), BF16), sd((1, D), F32)]
        + [sd((D, D), F32)] * 3,
        scratch_shapes=[pltpu.VMEM((3, D, D), F32), _dma_sems(3)], sem=("arbitrary",), comm=comm)


def _ffn(x1, target, w1, w2, g3, g4, tm):
    T = x1.shape[0]

    def body(x_ref, t_ref, w1_ref, w2_ref, g3_ref, g4_ref,
             hf_ref, f2_ref, dff_ref, df1_ref, dx_ref, ls_ref, dg3_ref, dg4_ref):
        @pl.when(pl.program_id(0) == 0)
        def _():
            ls_ref[...] = jnp.zeros_like(ls_ref)
            dg3_ref[...] = jnp.zeros_like(dg3_ref)
            dg4_ref[...] = jnp.zeros_like(dg4_ref)

        x = x_ref[...]
        g3, g4 = g3_ref[...], g4_ref[...]
        xhat, r3 = _rms_hat(x)
        hf = (xhat * g3).astype(BF16)
        hf_ref[...] = hf
        rl = jnp.maximum(_dot(hf, w1_ref[...]), 0.0)
        f2 = (rl * rl).astype(BF16)
        f2_ref[...] = f2
        fhat, r4 = _rms_hat(_dot(f2, w2_ref[...]))
        err = x + fhat * g4 - t_ref[...]
        ls_ref[...] += jnp.sum(err * err, axis=0, keepdims=True)
        dy = err * (1.0 / D)
        dff, dg4 = _rms_bwd(fhat, r4, g4, dy)
        dg4_ref[...] += dg4
        dff = dff.astype(BF16)
        dff_ref[...] = dff
        df1 = (_dot_nt(dff, w2_ref[...]) * (2.0 * rl)).astype(BF16)
        df1_ref[...] = df1
        dxn, dg3 = _rms_bwd(xhat, r3, g3, _dot_nt(df1, w1_ref[...]))
        dg3_ref[...] += dg3
        dx_ref[...] = dy + dxn

    sd = jax.ShapeDtypeStruct
    return pl.pallas_call(
        body, name="ffn_fwd_bwd", grid=(T // tm,),
        in_specs=[_rows(tm, D), _rows(tm, D), _resident((D, DFF)), _resident((DFF, D)), _const((1, D)), _const((1, D))],
        out_specs=[_rows(tm, D), _rows(tm, DFF), _rows(tm, D), _rows(tm, DFF), _rows(tm, D), _const((1, D)),
                   _const((1, D)), _const((1, D))],
        out_shape=[sd((T, D), BF16), sd((T, DFF), BF16), sd((T, D), BF16), sd((T, DFF), BF16), sd((T, D), F32),
                   sd((1, D), F32), sd((1, D), F32), sd((1, D), F32)],
        compiler_params=pltpu.CompilerParams(vmem_limit_bytes=VMEM_PHYSICAL, dimension_semantics=("arbitrary",)),
    )(x1, target, w1, w2, g3, g4)


def _inproj_bwd(parts, x, dx1, g1, w_in, tm, comm=None):
    T = x.shape[0]
    widths = [p.shape[1] for p in parts]
    offs = [sum(widths[:i]) for i in range(len(widths) + 1)]
    assert offs[-1] == IN_W

    def body(*refs):
        n = len(parts)
        prefs = refs[:n]
        x_ref, dx1_ref, g_ref, w_ref, dx_ref, dp_ref, dg_ref = refs[n:]

        @pl.when(pl.program_id(0) == 0)
        def _():
            dg_ref[...] = jnp.zeros_like(dg_ref)

        for i in range(n):
            dp_ref[:, offs[i]:offs[i + 1]] = prefs[i][...]
        dh = _dot_nt(dp_ref[...], w_ref[...])
        xhat, r = _rms_hat(x_ref[...])
        dxn, dg = _rms_bwd(xhat, r, g_ref[...], dh)
        dg_ref[...] += dg
        dx_ref[...] = dx1_ref[...] + dxn

    sd = jax.ShapeDtypeStruct
    return _call(
        body, (*parts, x, dx1, g1, w_in), name="inproj_bwd", grid=(T // tm,),
        in_specs=[_rows(tm, w) for w in widths] + [_rows(tm, D), _rows(tm, D), _const((1, D)), _resident((D, IN_W))],
        out_specs=[_rows(tm, D), _rows(tm, IN_W), _const((1, D))],
        out_shape=[sd((T, D), F32), sd((T, IN_W), BF16), sd((1, D), F32)], sem=("arbitrary",), comm=comm)


def _wgrad(a, g, tn, tm, name, comm=None, vmem=None):
    T, K = a.shape
    N = g.shape[1]

    def body(a_ref, g_ref, o_ref):
        @pl.when(pl.program_id(1) == 0)
        def _():
            o_ref[...] = jnp.zeros_like(o_ref)

        o_ref[...] += _dot_tn(a_ref[...], g_ref[...])

    return _call(
        body, (a, g), name=name, grid=(N // tn, T // tm),
        in_specs=[pl.BlockSpec((tm, K), lambda j, t: (t, 0)), pl.BlockSpec((tm, tn), lambda j, t: (t, j))],
        out_specs=pl.BlockSpec((K, tn), lambda j, t: (0, j)),
        out_shape=jax.ShapeDtypeStruct((K, N), F32), sem=("parallel", "arbitrary"), comm=comm, vmem=vmem)


def _adamw(ws, gs, ms, vs, trs, name):
    n = len(ws)
    walk = _Walk(w.shape[0] // tr for w, tr in zip(ws, trs))
    bc1 = 1.0 / (1.0 - B1 ** STEP)
    bc2 = 1.0 / (1.0 - B2 ** STEP)

    def body(*refs):
        i = pl.program_id(0)
        for k in range(n):
            mine = tuple(refs[j * n + k] for j in range(8))

            @pl.when(walk.mine(k, i))
            def _(mine=mine):
                w_ref, g_ref, m_ref, v_ref, go_ref, d_ref, nm_ref, nv_ref = mine
                g = g_ref[...]
                go_ref[...] = g
                m = B1 * m_ref[...] + (1.0 - B1) * g
                v = B2 * v_ref[...] + (1.0 - B2) * (g * g)
                nm_ref[...] = m
                nv_ref[...] = v
                d_ref[...] = -LR * ((m * bc1) / (jnp.sqrt(v * bc2) + AEPS) + WD * w_ref[...])

    def spec(k):
        return pl.BlockSpec((trs[k], ws[k].shape[1]), lambda i: (walk.tile(k, i), 0))

    specs = [spec(k) for k in range(n)]
    res = pl.pallas_call(
        body, name=name, grid=(walk.steps,), in_specs=specs * 4, out_specs=specs * 4,
        out_shape=[jax.ShapeDtypeStruct(w.shape, F32) for w in ws] * 4,
        compiler_params=_cparams(("arbitrary",)),
    )(*ws, *gs, *ms, *vs)
    return [tuple(res[j * n + k] for j in range(4)) for k in range(n)]


BIG = (("col", (D, IN_W)), ("row", (D, D)), ("row", (D, D)), ("row", (D, D)), ("col", (D, DFF)), ("row", (DFF, D)))
NBIG = len(BIG)
ANY = pl.BlockSpec(memory_space=pl.ANY)


def _shard_shape(kind, shape):
    R, C = shape
    return (R, C // 4) if kind == "col" else (R // 4, C)


def _half_shape(kind, shape):
    R, C = shape
    return (R // 2, C) if kind == "col" else (R, C // 2)


def _piece_shape(kind, shape):
    R, C = shape
    return (R // 2, C // 4) if kind == "col" else (R // 4, C // 2)


def _own_region(ref, kind, shape, s):
    R, C = shape
    return ref.at[:, pl.ds(s * (C // 4), C // 4)] if kind == "col" else ref.at[pl.ds(s * (R // 4), R // 4), :]


def _ag_region(ref, kind, shape, s, hc):
    R, C = shape
    if kind == "col":
        return ref.at[pl.ds(hc * (R // 2), R // 2), pl.ds(s * (C // 4), C // 4)]
    return ref.at[pl.ds(s * (R // 4) + hc * (R // 8), R // 8), :]


def _ag_shard_half(ref, kind, shape, hc):
    R, C = shape
    return ref.at[pl.ds(hc * (R // 2), R // 2), :] if kind == "col" else ref.at[pl.ds(hc * (R // 8), R // 8), :]


def _grad_half(ref, kind, shape, hc):
    R, C = shape
    return ref.at[pl.ds(hc * (R // 2), R // 2), :] if kind == "col" else ref.at[:, pl.ds(hc * (C // 2), C // 2)]


def _half_piece(ref, kind, shape, s):
    R, C = shape
    return ref.at[:, pl.ds(s * (C // 4), C // 4)] if kind == "col" else ref.at[pl.ds(s * (R // 4), R // 4), :]


def _place():
    x, y, c = lax.axis_index("x"), lax.axis_index("y"), lax.axis_index("c")
    chips = [(1 - x, y), (x, 1 - y), (1 - x, 1 - y)]
    return x, y, c, chips


def _rcopy(src, dst, ssem, rsem, dev):
    return pltpu.make_async_remote_copy(src_ref=src, dst_ref=dst, send_sem=ssem, recv_sem=rsem,
                                        device_id=dev, device_id_type=MESH)


def _dma_sems(n):
    return pltpu.SemaphoreType.DMA((n,))


def _x_gather_ici(shards, ws):
    n = len(ws)
    specs = [BIG[w] for w in ws]

    def place():
        x, y, c, chips = _place()
        return c, chips, 2 * x + y

    def sends(sh, full, sc):
        c, chips, me_s = place()
        return [_rcopy(_ag_shard_half(sh[i], kind, shape, c), _ag_region(full[i], kind, shape, me_s, c),
                       sc[0].at[3 * i + j], sc[1].at[3 * i + j], (cx, cy, c))
                for i, (kind, shape) in enumerate(specs) for j, (cx, cy) in enumerate(chips)]

    def start(sh, full, sc):
        for i in range(n):
            pltpu.make_async_copy(sh[i], sc[4 + i], sc[2].at[i]).start()
        for cp in sends(sh, full, sc):
            cp.start()

    def finish(sh, full, sc):
        c, chips, me_s = place()
        stores = []
        for i, (kind, shape) in enumerate(specs):
            pltpu.make_async_copy(sh[i], sc[4 + i], sc[2].at[i]).wait()
            st = pltpu.make_async_copy(sc[4 + i], _own_region(full[i], kind, shape, me_s), sc[3].at[i])
            st.start()
            stores.append(st)
        for i, (kind, shape) in enumerate(specs):
            for j, (cx, cy) in enumerate(chips):
                reg = _ag_region(full[i], kind, shape, 2 * cx + cy, c)
                _rcopy(reg, reg, sc[0].at[3 * i + j], sc[1].at[3 * i + j], (cx, cy, c)).wait_recv()
        for cp in sends(sh, full, sc):
            cp.wait_send()
        for st in stores:
            st.wait()

    return _Exchange(
        shards, [jax.ShapeDtypeStruct(shape, BF16) for _, shape in specs], {},
        [_dma_sems(3 * n), _dma_sems(3 * n), _dma_sems(n), _dma_sems(n)]
        + [pltpu.VMEM(_shard_shape(k, s), BF16) for k, s in specs], start, finish)


def _x_gather_d2d(wholes, ws):
    specs = [BIG[w] for w in ws]
    n = len(ws)

    def copies(full, sc, mine):
        x, y, c, chips = _place()
        hc = c if mine else 1 - c
        return [_rcopy(reg, reg, sc[0].at[3 * i + j], sc[1].at[3 * i + j], (x, y, 1 - c))
                for i, (kind, shape) in enumerate(specs) for j, (cx, cy) in enumerate(chips)
                for reg in [_ag_region(full[i], kind, shape, 2 * cx + cy, hc)]]

    def start(_, full, sc):
        for cp in copies(full, sc, True):
            cp.start()

    def finish(_, full, sc):
        for cp in copies(full, sc, False):
            cp.wait_recv()
        for cp in copies(full, sc, True):
            cp.wait_send()

    return _Exchange(wholes, [jax.ShapeDtypeStruct(shape, BF16) for _, shape in specs], {i: i for i in range(n)},
                     [_dma_sems(3 * n), _dma_sems(3 * n)], start, finish)


def _x_grads_sibling(grads, ws):
    specs = [BIG[w] for w in ws]
    n = len(ws)

    def copies(g, got, sc):
        x, y, c, _ = _place()
        return [_rcopy(_grad_half(g[i], kind, shape, 1 - c), got[i], sc[0].at[i], sc[1].at[i], (x, y, 1 - c))
                for i, (kind, shape) in enumerate(specs)]

    def start(g, got, sc):
        for cp in copies(g, got, sc):
            cp.start()

    def finish(g, got, sc):
        for cp in copies(g, got, sc):
            cp.wait_recv()
        for cp in copies(g, got, sc):
            cp.wait_send()

    return _Exchange(grads, [jax.ShapeDtypeStruct(_half_shape(k, s), F32) for k, s in specs], {},
                     [_dma_sems(n), _dma_sems(n)], start, finish)


def _x_grads_chips(sums_bf, ws):
    specs = [BIG[w] for w in ws]
    n = len(ws)

    def copies(s16, got, sc):
        x, y, c, chips = _place()
        return [_rcopy(_half_piece(s16[i], kind, shape, 2 * cx + cy), got[i].at[j],
                       sc[0].at[3 * i + j], sc[1].at[3 * i + j], (cx, cy, c))
                for i, (kind, shape) in enumerate(specs) for j, (cx, cy) in enumerate(chips)]

    def start(s16, got, sc):
        for cp in copies(s16, got, sc):
            cp.start()

    def finish(s16, got, sc):
        for cp in copies(s16, got, sc):
            cp.wait_recv()
        for cp in copies(s16, got, sc):
            cp.wait_send()

    return _Exchange(sums_bf, [jax.ShapeDtypeStruct((3,) + _piece_shape(k, s), BF16) for k, s in specs], {},
                     [_dma_sems(3 * n), _dma_sems(3 * n)], start, finish)


def _shard_half(ref, kind, shape, hc):
    sr, sc = _shard_shape(kind, shape)
    return ref.at[pl.ds(hc * (sr // 2), sr // 2), :] if kind == "col" else ref.at[:, pl.ds(hc * (sc // 2), sc // 2)]


def _x_grads_share(shard_grads, ws):
    specs = [BIG[w] for w in ws]
    n = len(ws)

    def copies(g, sc, mine):
        x, y, c, _ = _place()
        hc = c if mine else 1 - c
        return [_rcopy(part, part, sc[0].at[i], sc[1].at[i], (x, y, 1 - c))
                for i, (kind, shape) in enumerate(specs) for part in [_shard_half(g[i], kind, shape, hc)]]

    def start(_, g, sc):
        for cp in copies(g, sc, True):
            cp.start()

    def finish(_, g, sc):
        for cp in copies(g, sc, False):
            cp.wait_recv()
        for cp in copies(g, sc, True):
            cp.wait_send()

    return _Exchange(shard_grads, [jax.ShapeDtypeStruct(_shard_shape(k, s), F32) for k, s in specs],
                     {i: i for i in range(n)}, [_dma_sems(n), _dma_sems(n)], start, finish)


ADD_BLOCK_BYTES = 4 * 1024 * 1024


def _add_rows(rows, cols, n_arrays):
    limit = ADD_BLOCK_BYTES // (1 if n_arrays == 1 else 4)
    r = rows
    while r > 64 and r * cols * 4 > limit:
        r //= 2
    return r


class _Walk:
    def __init__(self, tiles):
        self.tiles = list(tiles)
        self.starts = [sum(self.tiles[:k]) for k in range(len(self.tiles))]
        self.steps = sum(self.tiles)

    def tile(self, k, i):
        return jnp.clip(i - self.starts[k], 0, self.tiles[k] - 1)

    def mine(self, k, i):
        return (i >= self.starts[k]) & (i < self.starts[k] + self.tiles[k])


def _add_halves(place, gs, gots, kinds, name):
    n = len(gs)
    halves = [_half_shape(kind, g.shape) for g, kind in zip(gs, kinds)]
    rows = [_add_rows(hr, hc, n) for hr, hc in halves]
    walk = _Walk(hr // r for (hr, _), r in zip(halves, rows))

    def body(p_ref, *refs):
        i = pl.program_id(0)
        for k in range(n):
            g_ref, b_ref, s_ref, sb_ref = (refs[j * n + k] for j in range(4))

            @pl.when(walk.mine(k, i))
            def _(g_ref=g_ref, b_ref=b_ref, s_ref=s_ref, sb_ref=sb_ref):
                s = g_ref[...] + b_ref[...]
                s_ref[...] = s
                sb_ref[...] = s.astype(BF16)

    def g_spec(k):
        if kinds[k] == "col":
            return pl.BlockSpec((rows[k], gs[k].shape[1]), lambda i, p: (p[0] * walk.tiles[k] + walk.tile(k, i), 0))
        return pl.BlockSpec((rows[k], halves[k][1]), lambda i, p: (walk.tile(k, i), p[0]))

    def spec(k):
        return pl.BlockSpec((rows[k], halves[k][1]), lambda i, p: (walk.tile(k, i), 0))

    specs = [spec(k) for k in range(n)]
    res = pl.pallas_call(
        body, name=name,
        grid_spec=pltpu.PrefetchScalarGridSpec(num_scalar_prefetch=1, grid=(walk.steps,),
                                               in_specs=[g_spec(k) for k in range(n)] + specs, out_specs=specs + specs),
        out_shape=[jax.ShapeDtypeStruct(h, F32) for h in halves] + [jax.ShapeDtypeStruct(h, BF16) for h in halves],
        compiler_params=_cparams(("arbitrary",)),
    )(place, *gs, *gots)
    return [(res[k], res[n + k]) for k in range(n)]


def _add_pieces(place, halves, gots, specs_big, name):
    n = len(halves)
    pieces = [_piece_shape(kind, shape) for kind, shape in specs_big]
    rows = [_add_rows(pr, pc, n) for pr, pc in pieces]
    walk = _Walk(pr // r for (pr, _), r in zip(pieces, rows))

    def body(p_ref, *refs):
        i = pl.program_id(0)
        for k in range(n):
            m_ref, g_ref, o_ref = (refs[j * n + k] for j in range(3))

            @pl.when(walk.mine(k, i))
            def _(m_ref=m_ref, g_ref=g_ref, o_ref=o_ref):
                acc = m_ref[...]
                for j in range(3):
                    acc = acc + g_ref[j].astype(F32)
                o_ref[...] = acc

    def m_spec(k):
        if specs_big[k][0] == "col":
            return pl.BlockSpec((rows[k], pieces[k][1]), lambda i, p: (walk.tile(k, i), p[1]))
        return pl.BlockSpec((rows[k], pieces[k][1]), lambda i, p: (p[1] * walk.tiles[k] + walk.tile(k, i), 0))

    def got_spec(k):
        return pl.BlockSpec((3, rows[k], pieces[k][1]), lambda i, p: (0, walk.tile(k, i), 0))

    def o_spec(k):
        if specs_big[k][0] == "col":
            return pl.BlockSpec((rows[k], pieces[k][1]), lambda i, p: (p[0] * walk.tiles[k] + walk.tile(k, i), 0))
        return pl.BlockSpec((rows[k], pieces[k][1]), lambda i, p: (walk.tile(k, i), p[0]))

    return pl.pallas_call(
        body, name=name,
        grid_spec=pltpu.PrefetchScalarGridSpec(
            num_scalar_prefetch=1, grid=(walk.steps,),
            in_specs=[m_spec(k) for k in range(n)] + [got_spec(k) for k in range(n)],
            out_specs=[o_spec(k) for k in range(n)]),
        out_shape=[jax.ShapeDtypeStruct(_shard_shape(kind, shape), F32) for kind, shape in specs_big],
        compiler_params=_cparams(("arbitrary",)),
    )(place, *halves, *gots)


SMALL_ROWS = 1024 + 8 * 8 + 8


def _x_small_all_reduce(p):
    def parts(p_ref, sc):
        slots, ssem, rsem = sc[0], sc[2], sc[3]
        x, y, c = lax.axis_index("x"), lax.axis_index("y"), lax.axis_index("c")
        me = 4 * x + 2 * y + c
        out = []
        for r in range(1, 8):
            bx, by, bc = (r >> 2) & 1, (r >> 1) & 1, r & 1
            tgt = (1 - x if bx else x, 1 - y if by else y, 1 - c if bc else c)
            send = _rcopy(p_ref, slots.at[me], ssem.at[r - 1], rsem.at[r - 1], tgt)
            src = 4 * tgt[0] + 2 * tgt[1] + tgt[2]
            recv = _rcopy(p_ref, slots.at[src], ssem.at[r - 1], rsem.at[r - 1], tgt)
            out.append((send, recv))
        return me, out

    def start(ins, outs, sc):
        me, cps = parts(ins[0], sc)
        pltpu.make_async_copy(ins[0], sc[0].at[me], sc[4].at[0]).start()
        for send, _ in cps:
            send.start()

    def finish(ins, outs, sc):
        me, cps = parts(ins[0], sc)
        pltpu.make_async_copy(ins[0], sc[0].at[me], sc[4].at[0]).wait()
        for _, recv in cps:
            recv.wait_recv()
        acc = sc[0][0]
        for d in range(1, 8):
            acc = acc + sc[0][d]
        sc[1][...] = acc
        back = pltpu.make_async_copy(sc[1], outs[0], sc[4].at[1])
        back.start()
        for send, _ in cps:
            send.wait_send()
        back.wait()

    return _Exchange([p], [jax.ShapeDtypeStruct((SMALL_ROWS, CH), F32)], {},
                     [pltpu.VMEM((8, SMALL_ROWS, CH), F32), pltpu.VMEM((SMALL_ROWS, CH), F32), _dma_sems(7), _dma_sems(7),
                      _dma_sems(2)], start, finish)


def _rope_tables(positions, comm=None):
    T = positions.shape[0]
    inv_freq = 500000.0 ** (-jnp.arange(0, 2 * ROPE_HALF, 2, dtype=F32) / (2 * ROPE_HALF))
    head = jnp.concatenate([inv_freq, inv_freq, jnp.zeros((HD - 2 * ROPE_HALF,), F32)])
    lane_freq = jnp.concatenate([head, head])[None, :]
    pos = jnp.broadcast_to(positions.astype(F32)[:, None], (T, CH))
    tm = min(1024, T)

    def body(p_ref, f_ref, c_ref, s1_ref, s2_ref):
        ang = p_ref[...] * f_ref[...]
        sin = jnp.sin(ang)
        first = (lax.broadcasted_iota(jnp.int32, ang.shape, 1) % HD) < ROPE_HALF
        c_ref[...] = jnp.cos(ang)
        s1_ref[...] = jnp.where(first, -sin, 0.0)
        s2_ref[...] = jnp.where(first, 0.0, sin)

    return _call(body, (pos, lane_freq), name="rope_tables", grid=(T // tm,),
                 in_specs=[_rows(tm, CH), _const((1, CH))], out_specs=[_rows(tm, CH)] * 3,
                 out_shape=[jax.ShapeDtypeStruct((T, CH), F32)] * 3, sem=("parallel",), comm=comm)


BIG_NAMES = ("w_in", "w_a", "w_b", "w_o", "w_ff_in", "w_ff_out")
SMALL_NAMES = ("w_spatial", "ln_v_gain", "ln_v_bias", "b_spatial", "sinks", "norm_mix_pre", "norm_mix_post",
               "norm_ff_pre", "norm_ff_post")
WEIGHT_ORDER = ("w_in", "ln_v_gain", "ln_v_bias", "w_spatial", "b_spatial", "sinks", "w_a", "w_b", "w_o",
                "norm_mix_pre", "norm_mix_post", "w_ff_in", "w_ff_out", "norm_ff_pre", "norm_ff_post")


def _pack_small(d, loss_sums=None):
    parts = []
    for n in SMALL_NAMES:
        flat = d[n].reshape(-1)
        pad = (-flat.shape[0]) % (8 * CH)
        parts.append(jnp.pad(flat, (0, pad)).reshape(-1, CH))
    parts.append(jnp.zeros((8, CH), F32) if loss_sums is None else loss_sums.reshape(8, CH))
    return jnp.concatenate(parts, axis=0)


def _unpack_small(p, like):
    out, row = {}, 0
    for n in SMALL_NAMES:
        size = like[n].size
        rows = -(-size // (8 * CH)) * 8
        out[n] = p[row:row + rows].reshape(-1)[:size].reshape(like[n].shape)
        row += rows
    return out


def kernel(x, positions, w_in, ln_v_gain, ln_v_bias, w_spatial, b_spatial, sinks, w_a, w_b, w_o, norm_mix_pre, norm_mix_post, w_ff_in, w_ff_out, norm_ff_pre, norm_ff_post, loss_target, m_w_in, m_ln_v_gain, m_ln_v_bias, m_w_spatial, m_b_spatial, m_sinks, m_w_a, m_w_b, m_w_o, m_norm_mix_pre, m_norm_mix_post, m_w_ff_in, m_w_ff_out, m_norm_ff_pre, m_norm_ff_post, v_w_in, v_ln_v_gain, v_ln_v_bias, v_w_spatial, v_b_spatial, v_sinks, v_w_a, v_w_b, v_w_o, v_norm_mix_pre, v_norm_mix_post, v_w_ff_in, v_w_ff_out, v_norm_ff_pre, v_norm_ff_post):
    w = dict(w_in=w_in, ln_v_gain=ln_v_gain, ln_v_bias=ln_v_bias, w_spatial=w_spatial, b_spatial=b_spatial, sinks=sinks,
             w_a=w_a, w_b=w_b, w_o=w_o, norm_mix_pre=norm_mix_pre, norm_mix_post=norm_mix_post, w_ff_in=w_ff_in,
             w_ff_out=w_ff_out, norm_ff_pre=norm_ff_pre, norm_ff_post=norm_ff_post)
    m = dict(w_in=m_w_in, ln_v_gain=m_ln_v_gain, ln_v_bias=m_ln_v_bias, w_spatial=m_w_spatial, b_spatial=m_b_spatial,
             sinks=m_sinks, w_a=m_w_a, w_b=m_w_b, w_o=m_w_o, norm_mix_pre=m_norm_mix_pre, norm_mix_post=m_norm_mix_post,
             w_ff_in=m_w_ff_in, w_ff_out=m_w_ff_out, norm_ff_pre=m_norm_ff_pre, norm_ff_post=m_norm_ff_post)
    v = dict(w_in=v_w_in, ln_v_gain=v_ln_v_gain, ln_v_bias=v_ln_v_bias, w_spatial=v_w_spatial, b_spatial=v_b_spatial,
             sinks=v_sinks, w_a=v_w_a, w_b=v_w_b, w_o=v_w_o, norm_mix_pre=v_norm_mix_pre, norm_mix_post=v_norm_mix_post,
             w_ff_in=v_w_ff_in, w_ff_out=v_w_ff_out, norm_ff_pre=v_norm_ff_pre, norm_ff_post=v_norm_ff_post)

    FIRST, REST = (0,), tuple(range(1, NBIG))
    shards = [w[n][0].astype(BF16) for n in BIG_NAMES]
    place = jnp.stack([lax.axis_index("c"), 2 * lax.axis_index("x") + lax.axis_index("y")]).astype(jnp.int32)
    xs, target = x[0], loss_target[0]
    T = xs.shape[0]
    wtm, wtm2 = min(1024, T), min(2048, T)
    g1, g2, g3, g4 = norm_mix_pre, norm_mix_post, norm_ff_pre, norm_ff_post
    w_sp, snk = w_spatial[0], sinks[0]
    MIX, FF = (1, 2, 3), (4, 5)
    bfull = jnp.repeat(b_spatial[0].T, CH, axis=1)

    def reduce_tail(ws, grads, got):
        tag = "_".join(BIG_NAMES[k] for k in ws)
        sums = _add_halves(place, grads, got, [BIG[k][0] for k in ws], name="grad_add_sibling_" + tag)
        return sums, _x_grads_chips([s[1] for s in sums], ws)

    def reduce_end(ws, sums, pieces):
        tag = "_".join(BIG_NAMES[k] for k in ws)
        return _add_pieces(place, [s[0] for s in sums], pieces, [BIG[k] for k in ws], name="grad_add_chips_" + tag)

    (rc, rs1, rs2), w_in_part = _rope_tables(positions[0], comm=_x_gather_ici(shards[:1], FIRST))
    w_in_b = _run(_x_gather_d2d(w_in_part, FIRST), "gather_w_in_d2d")[0]
    (h, u, vs, q, k, va, ga, gb), ff_part = _inproj(xs, g1, w_in_b, rc, rs1, rs2, tm=512, comm=_x_gather_ici(shards[4:], FF))
    a, mix_part = _sgu_fwd(u, vs, ln_v_gain, ln_v_bias, w_sp, bfull, tm=512, comm=_x_gather_ici(shards[1:4], MIX))
    att, rest = _attn_fwd(q, k, va, snk, comm=_both(_x_gather_d2d(mix_part, MIX), _x_gather_d2d(ff_part, FF)))
    w_a_b, w_b_b, w_o_b, w_ff_in_b, w_ff_out_b = rest
    (pa, pb, merged, mix, x1), _ = _merge_fwd(a, att, ga, gb, xs, w_a_b, w_b_b, w_o_b, g2, tm=512)
    hf, f2, dff, df1, dx1, lsum, dg3, dg4 = _ffn(x1, target, w_ff_in_b, w_ff_out_b, g3, g4, tm=512)

    dw_ff_out, _ = _wgrad(f2, dff, tn=1024, tm=wtm, name="wgrad_ff_out")
    dw_ff_in, _ = _wgrad(hf, df1, tn=2048, tm=wtm2, name="wgrad_ff_in")
    (dga, dgb, da, datt, dg2, dw_a, dw_b, dw_o), _ = _merge_bwd(
        dx1, mix, ga, gb, pa, pb, a, att, merged, w_a_b, w_b_b, w_o_b, g2, tm=512)
    grads_rest = [dw_a, dw_b, dw_o, dw_ff_in, dw_ff_out]
    (du, dvs, dws, dbs, dlg, dlb), got_rest = _sgu_bwd(
        u, vs, da, ln_v_gain, ln_v_bias, w_sp, bfull, tm=512, comm=_x_grads_sibling(grads_rest, REST))
    sums_rest, to_chips = reduce_tail(REST, grads_rest, got_rest)
    (dq, dk, dva, dsk), pieces_rest = _attn_bwd(q, k, va, datt, snk, rc, rs1, rs2, comm=to_chips)
    partial_rest = reduce_end(REST, sums_rest, pieces_rest)
    (dx, dproj, dg1), _ = _inproj_bwd([du, dvs, dq, dk, dva, dga, dgb], xs, dx1, g1, w_in_b, tm=512)
    small = dict(ln_v_gain=dlg, ln_v_bias=dlb, w_spatial=dws, b_spatial=dbs, sinks=dsk[:, :NQ],
                 norm_mix_pre=dg1, norm_mix_post=dg2, norm_ff_pre=dg3, norm_ff_post=dg4)
    dw_in, (gs, *shard_rest) = _wgrad(
        h, dproj, tn=IN_W // 2, tm=wtm2, name="wgrad_in", vmem=VMEM_PHYSICAL,
        comm=_both(_x_small_all_reduce(_pack_small(small, lsum)), _x_grads_share(partial_rest, REST)))
    got_in = _run(_x_grads_sibling([dw_in], FIRST), "grads_in_to_sibling")
    sums_in, to_chips = reduce_tail(FIRST, [dw_in], got_in)
    partial_in = reduce_end(FIRST, sums_in, _run(to_chips, "grads_in_to_chips"))
    g_in = _run(_x_grads_share(partial_in, FIRST), "grads_in_share")[0]

    loss = 0.5 * jnp.sum(gs[SMALL_ROWS - 8:]) / D
    grad, delta, new_m, new_v = {}, {}, {}, {}
    for n, g in zip(BIG_NAMES, [g_in] + list(shard_rest)):
        (g_, d_, m_, v_), = _adamw([w[n][0]], [g], [m[n][0]], [v[n][0]], [256], name="adamw_" + n)
        grad[n], delta[n], new_m[n], new_v[n] = g_[None], d_[None], m_[None], v_[None]
    (gs, ds, ms, vs), = _adamw([_pack_small(w)], [gs], [_pack_small(m)], [_pack_small(v)], [SMALL_ROWS], name="adamw_small")
    for packed, dst in ((gs, grad), (ds, delta), (ms, new_m), (vs, new_v)):
        dst.update(_unpack_small(packed, w))

    outs = [loss, dx[None]]
    for group in (grad, delta, new_m, new_v):
        outs.extend(group[n] for n in WEIGHT_ORDER)
    return tuple(outs)
```

```python
import functools

import jax
import jax.numpy as jnp
from jax import lax
from jax.experimental import pallas as pl
from jax.experimental.pallas import tpu as pltpu

F32 = jnp.float32
BF16 = jnp.bfloat16

D = 1024
CH = 128
NG = 8
HD = 64
NQ = 16
NKV = 4
KVW = NKV * HD
DFF = 4 * D
EPS = 1e-6
IN_W = 5632
SEG = (0, 1024, 2048, 3072, 3328, 3584, 4608, 5632)
ROPE_HALF = 8
Q_SCALE = HD ** -0.5

LR, B1, B2, AEPS, WD, STEP = 0.001, 0.9, 0.999, 1e-08, 0.01, 10

VMEM_PHYSICAL = 64 * 1024 * 1024
VMEM_LIMIT = 60 * 1024 * 1024
MESH = pl.DeviceIdType.MESH

TOKEN_TILE = 512
WGRAD_TILES = {"w_ff_out": (512, 1024), "w_ff_in": (2048, 2048), "w_in": (2048, IN_W // 2)}

_GELU_C0 = 0.7978845608028654
_GELU_C1 = 0.044715


def _cparams(sem=None, vmem=None):
    kw = dict(vmem_limit_bytes=VMEM_LIMIT if vmem is None else vmem)
    if sem is not None:
        kw["dimension_semantics"] = sem
    return pltpu.CompilerParams(**kw)


def _resident(shape):
    nd = len(shape)
    return pl.BlockSpec(shape, lambda *_: (0,) * nd, pipeline_mode=pl.Buffered(1))


def _const(shape):
    nd = len(shape)
    return pl.BlockSpec(shape, lambda *_: (0,) * nd)


def _rows(tm, w):
    return pl.BlockSpec((tm, w), lambda i: (i, 0))


class _Exchange:
    def __init__(self, ins, outs, aliases, scratch, start, finish):
        self.ins, self.outs, self.aliases, self.scratch = list(ins), list(outs), dict(aliases), list(scratch)
        self.start, self.finish = start, finish


def _both(a, b):
    na, ma, sa = len(a.ins), len(a.outs), len(a.scratch)

    def start(ci, co, cs):
        a.start(ci[:na], co[:ma], cs[:sa])
        b.start(ci[na:], co[ma:], cs[sa:])

    def finish(ci, co, cs):
        a.finish(ci[:na], co[:ma], cs[:sa])
        b.finish(ci[na:], co[ma:], cs[sa:])

    aliases = {**a.aliases, **{na + i: ma + j for i, j in b.aliases.items()}}
    return _Exchange(a.ins + b.ins, a.outs + b.outs, aliases, a.scratch + b.scratch, start, finish)


def _call(body, args, *, name, grid, in_specs, out_specs, out_shape, scratch_shapes=(), sem=None, comm=None, vmem=None):
    single = not isinstance(out_shape, (list, tuple))
    out_shape = [out_shape] if single else list(out_shape)
    out_specs = [out_specs] if single else list(out_specs)
    if comm is None:
        res = pl.pallas_call(body, name=name, grid=grid, in_specs=list(in_specs), out_specs=out_specs,
                             out_shape=out_shape, scratch_shapes=list(scratch_shapes),
                             compiler_params=_cparams(sem, vmem))(*args)
        return (res[0] if single else res), []
    n_in, n_out, n_scr = len(args), len(out_shape), len(scratch_shapes)
    nci, nco = len(comm.ins), len(comm.outs)
    steps = 1
    for g in grid:
        steps *= g

    def hosted(*refs):
        a, ci = refs[:n_in], refs[n_in:n_in + nci]
        o, co = refs[n_in + nci:n_in + nci + n_out], refs[n_in + nci + n_out:n_in + nci + n_out + nco]
        rest = refs[n_in + nci + n_out + nco:]
        scr, cs = rest[:n_scr], rest[n_scr:]
        step = pl.program_id(0)
        for d in range(1, len(grid)):
            step = step * grid[d] + pl.program_id(d)

        @pl.when(step == 0)
        def _():
            comm.start(ci, co, cs)

        body(*a, *o, *scr)

        @pl.when(step == steps - 1)
        def _():
            comm.finish(ci, co, cs)

    res = pl.pallas_call(
        hosted, name=name, grid=grid, in_specs=list(in_specs) + [ANY] * nci, out_specs=out_specs + [ANY] * nco,
        out_shape=out_shape + comm.outs, scratch_shapes=list(scratch_shapes) + comm.scratch,
        input_output_aliases={n_in + i: n_out + j for i, j in comm.aliases.items()},
        compiler_params=_cparams(("arbitrary",) * len(grid), vmem),
    )(*args, *comm.ins)
    own = res[:n_out]
    return (own[0] if single else own), list(res[n_out:])


def _run(comm, name):
    nci = len(comm.ins)

    def body(*refs):
        ci, co, cs = refs[:nci], refs[nci:nci + len(comm.outs)], refs[nci + len(comm.outs):]
        comm.start(ci, co, cs)
        comm.finish(ci, co, cs)

    return pl.pallas_call(
        body, name=name, in_specs=[ANY] * nci, out_specs=[ANY] * len(comm.outs), out_shape=comm.outs,
        scratch_shapes=comm.scratch, input_output_aliases=comm.aliases,
        compiler_params=pltpu.CompilerParams(vmem_limit_bytes=VMEM_LIMIT),
    )(*comm.ins)


def _gelu(x):
    x2 = x * x
    t = jnp.tanh(x * (_GELU_C0 + (_GELU_C0 * _GELU_C1) * x2))
    hx = 0.5 * x
    return hx + hx * t, (t, x2, hx)


def _gelu_grad(parts):
    t, x2, hx = parts
    return (0.5 + 0.5 * t) + hx * (1.0 - t * t) * (_GELU_C0 + (3.0 * _GELU_C0 * _GELU_C1) * x2)


def _sigmoid(x):
    return 1.0 / (1.0 + jnp.exp(-x))


def _rms_hat(x):
    r = lax.rsqrt(jnp.mean(x * x, axis=-1, keepdims=True) + EPS)
    return x * r, r


def _rms_bwd(xhat, r, g, dout):
    dg = jnp.sum(dout * xhat, axis=0, keepdims=True)
    dy = dout * g
    dx = r * (dy - xhat * jnp.mean(dy * xhat, axis=-1, keepdims=True))
    return dx, dg


def _dot(a, b):
    return jnp.dot(a, b, preferred_element_type=F32)


def _dot_nt(a, b):
    return lax.dot_general(a, b, (((1,), (1,)), ((), ())), preferred_element_type=F32)


def _dot_tn(a, b):
    return lax.dot_general(a, b, (((0,), (0,)), ((), ())), preferred_element_type=F32)


def _rope(blk, c, s1, s2):
    return blk * c + pltpu.roll(blk, CH - ROPE_HALF, 1) * s1 + pltpu.roll(blk, ROPE_HALF, 1) * s2


def _rope_t(blk, c, s1, s2):
    return blk * c + pltpu.roll(blk * s1, ROPE_HALF, 1) + pltpu.roll(blk * s2, CH - ROPE_HALF, 1)


def _inproj(x, g1, w_in, rc, rs1, rs2, tm, comm=None):
    T = x.shape[0]

    def body(x_ref, g_ref, w_ref, c_ref, s1_ref, s2_ref,
             h_ref, u_ref, v_ref, q_ref, k_ref, va_ref, ga_ref, gb_ref):
        xhat, _ = _rms_hat(x_ref[...])
        h = (xhat * g_ref[...]).astype(BF16)
        h_ref[...] = h
        uv = _dot(h, w_ref[:, SEG[0]:SEG[2]])
        u_ref[...] = uv[:, :D]
        v_ref[...] = uv[:, D:]
        c, s1, s2 = c_ref[...], s1_ref[...], s2_ref[...]
        qkv = _dot(h, w_ref[:, SEG[2]:SEG[5]])
        for p in range(D // CH):
            blk = _rope(qkv[:, CH * p:CH * (p + 1)], c, s1, s2) * Q_SCALE
            q_ref[:, CH * p:CH * (p + 1)] = blk.astype(BF16)
        for p in range(KVW // CH):
            k_ref[:, CH * p:CH * (p + 1)] = _rope(qkv[:, D + CH * p:D + CH * (p + 1)], c, s1, s2).astype(BF16)
        va_ref[...] = qkv[:, D + KVW:].astype(BF16)
        gates = _dot(h, w_ref[:, SEG[5]:SEG[7]]).astype(BF16)
        ga_ref[...] = gates[:, :D]
        gb_ref[...] = gates[:, D:]

    sd = jax.ShapeDtypeStruct
    return _call(
        body, (x, g1, w_in, rc, rs1, rs2), name="inproj_fwd", grid=(T // tm,),
        in_specs=[_rows(tm, D), _const((1, D)), _resident((D, IN_W)), _rows(tm, CH), _rows(tm, CH), _rows(tm, CH)],
        out_specs=[_rows(tm, D), _rows(tm, D), _rows(tm, D), _rows(tm, D), _rows(tm, KVW), _rows(tm, KVW),
                   _rows(tm, D), _rows(tm, D)],
        out_shape=[sd((T, D), BF16), sd((T, D), F32), sd((T, D), F32), sd((T, D), BF16), sd((T, KVW), BF16),
                   sd((T, KVW), BF16), sd((T, D), BF16), sd((T, D), BF16)],
        sem=("parallel",), comm=comm)


def _sgu_common(u, vs, lng, lnb, ws_ref, bfull):
    nc = u.shape[0] // CH
    ug, tu = _gelu(u)
    vg, tv = _gelu(vs)
    mu = jnp.mean(vg, axis=-1, keepdims=True)
    xc = vg - mu
    rstd = lax.rsqrt(jnp.mean(xc * xc, axis=-1, keepdims=True) + EPS)
    vhat = xc * rstd
    vnb = (vhat * lng + lnb).astype(BF16)
    tri = lax.broadcasted_iota(jnp.int32, (CH, CH), 0) >= lax.broadcasted_iota(jnp.int32, (CH, CH), 1)
    wts, rhss, mixed = [], [], []
    for g in range(NG):
        wt = jnp.where(tri, ws_ref[g], 0.0).astype(BF16)
        rhs = jnp.concatenate([vnb[CH * c:CH * (c + 1), CH * g:CH * (g + 1)] for c in range(nc)], axis=1)
        mix = _dot(wt, rhs)
        wts.append(wt)
        rhss.append(rhs)
        mixed.append([mix[:, CH * c:CH * (c + 1)] + bfull[:, CH * g:CH * (g + 1)] for c in range(nc)])
    return nc, ug, tu, tv, rstd, vhat, tri, wts, rhss, mixed


def _sgu_fwd(u, vs, lng, lnb, ws, bfull, tm, comm=None):
    T = u.shape[0]

    def body(u_ref, v_ref, lng_ref, lnb_ref, ws_ref, bf_ref, a_ref):
        nc, ug, _, _, _, _, _, _, _, mixed = _sgu_common(
            u_ref[...], v_ref[...], lng_ref[...], lnb_ref[...], ws_ref, bf_ref[...])
        mixed_all = jnp.concatenate(
            [jnp.concatenate([mixed[g][c] for g in range(NG)], axis=1) for c in range(nc)], axis=0)
        a_ref[...] = (ug * mixed_all).astype(BF16)

    return _call(
        body, (u, vs, lng, lnb, ws, bfull), name="sgu_fwd", grid=(T // tm,),
        in_specs=[_rows(tm, D), _rows(tm, D), _const((1, D)), _const((1, D)), _const((NG, CH, CH)), _const((CH, D))],
        out_specs=_rows(tm, D), out_shape=jax.ShapeDtypeStruct((T, D), BF16), sem=("parallel",), comm=comm)


def _sgu_bwd(u, vs, da, lng, lnb, ws, bfull, tm, comm=None):
    T = u.shape[0]
    nsteps = T // tm

    def body(u_ref, v_ref, da_ref, lng_ref, lnb_ref, ws_ref, bf_ref,
             du_ref, dv_ref, dws_ref, dbs_ref, dlg_ref, dlb_ref, db_ref):
        i = pl.program_id(0)
        u, vs, da, lng = u_ref[...], v_ref[...], da_ref[...], lng_ref[...]
        nc, ug, tu, tv, rstd, vhat, tri, wts, rhss, mixed = _sgu_common(u, vs, lng, lnb_ref[...], ws_ref, bf_ref[...])

        @pl.when(i == 0)
        def _():
            dws_ref[...] = jnp.zeros_like(dws_ref)
            db_ref[...] = jnp.zeros_like(db_ref)
            dlg_ref[...] = jnp.zeros_like(dlg_ref)
            dlb_ref[...] = jnp.zeros_like(dlb_ref)

        mixed_all = jnp.concatenate(
            [jnp.concatenate([mixed[g][c] for g in range(NG)], axis=1) for c in range(nc)], axis=0)
        du_ref[...] = (da * mixed_all * _gelu_grad(tu)).astype(BF16)
        dmixed = da * ug
        dvn_cols = []
        for g in range(NG):
            dmix = [dmixed[CH * c:CH * (c + 1), CH * g:CH * (g + 1)] for c in range(nc)]
            db_ref[:, CH * g:CH * (g + 1)] += functools.reduce(lambda a, b: a + b, dmix)
            dm = jnp.concatenate(dmix, axis=1).astype(BF16)
            dws_ref[g] += _dot_nt(dm, rhss[g])
            dvn_cols.append(_dot_tn(wts[g], dm))
        dvn = jnp.concatenate(
            [jnp.concatenate([dvn_cols[g][:, CH * c:CH * (c + 1)] for g in range(NG)], axis=1) for c in range(nc)],
            axis=0)
        dlg_ref[...] += jnp.sum(dvn * vhat, axis=0, keepdims=True)
        dlb_ref[...] += jnp.sum(dvn, axis=0, keepdims=True)
        dvh = dvn * lng
        dvg = rstd * (dvh - jnp.mean(dvh, axis=-1, keepdims=True)
                      - vhat * jnp.mean(dvh * vhat, axis=-1, keepdims=True))
        dv_ref[...] = (dvg * _gelu_grad(tv)).astype(BF16)

        @pl.when(i == nsteps - 1)
        def _():
            for g in range(NG):
                dws_ref[g] = jnp.where(tri, dws_ref[g], 0.0)
                dbs_ref[g:g + 1, :] = jnp.sum(db_ref[:, CH * g:CH * (g + 1)].T, axis=0, keepdims=True)

    sd = jax.ShapeDtypeStruct
    return _call(
        body, (u, vs, da, lng, lnb, ws, bfull), name="sgu_bwd", grid=(nsteps,),
        in_specs=[_rows(tm, D), _rows(tm, D), _rows(tm, D), _const((1, D)), _const((1, D)), _const((NG, CH, CH)),
                  _const((CH, D))],
        out_specs=[_rows(tm, D), _rows(tm, D), _const((NG, CH, CH)), _const((NG, CH)), _const((1, D)), _const((1, D))],
        out_shape=[sd((T, D), BF16), sd((T, D), BF16), sd((NG, CH, CH), F32), sd((NG, CH), F32), sd((1, D), F32),
                   sd((1, D), F32)],
        scratch_shapes=[pltpu.VMEM((CH, D), F32)], sem=("arbitrary",), comm=comm)


def _pair_layout(prev, cur, grp):
    j, half = grp // 2, grp % 2
    blk = jnp.concatenate([prev[:, CH * j:CH * (j + 1)], cur[:, CH * j:CH * (j + 1)]], axis=0).astype(F32)
    lo = lax.broadcasted_iota(jnp.int32, blk.shape, 1) < HD
    rolled = pltpu.roll(blk, HD, 1)
    even = jnp.where(lo, blk if half == 0 else rolled, 0.0)
    odd = jnp.where(lo, 0.0, rolled if half == 0 else blk)
    return jnp.concatenate([even, odd], axis=0).astype(BF16)


def _attn_mask(n):
    qi = lax.broadcasted_iota(jnp.int32, (CH, 2 * CH), 0)
    kc = lax.broadcasted_iota(jnp.int32, (CH, 2 * CH), 1)
    ok = (kc > qi) & (kc <= qi + CH) & ((kc >= CH) | (n > 0))
    return jnp.concatenate([ok, ok], axis=1)


def _softmax_sink(s, sink):
    m = jnp.maximum(jnp.max(s, axis=-1, keepdims=True), sink)
    p = jnp.exp(s - m)
    ps = jnp.exp(sink - m)
    inv = 1.0 / (jnp.sum(p, axis=-1, keepdims=True) + ps)
    return p * inv, ps * inv


QUERY_BLOCKS_PER_STEP = 2


def _attn_fwd(q, k, va, sinks, comm=None):
    T = q.shape[0]
    nblk = QUERY_BLOCKS_PER_STEP
    nsteps = T // (nblk * CH)
    npairs = D // CH

    def body(sk_ref, q_ref, kp_ref, kc_ref, vp_ref, vc_ref, o_ref):
        n = pl.program_id(0)
        even_lanes = lax.broadcasted_iota(jnp.int32, (CH, CH), 1) < HD
        ks = [kp_ref[...]] + [kc_ref[CH * b:CH * (b + 1)] for b in range(nblk)]
        vs = [vp_ref[...]] + [vc_ref[CH * b:CH * (b + 1)] for b in range(nblk)]
        masks = [_attn_mask(nblk * n)] + [_attn_mask(1)] * (nblk - 1)
        kks = [[_pair_layout(ks[b], ks[b + 1], grp) for grp in range(NKV)] for b in range(nblk)]
        vvs = [[_pair_layout(vs[b], vs[b + 1], grp) for grp in range(NKV)] for b in range(nblk)]
        work = [(b, p) for b in range(nblk) for p in range(npairs)]

        def scores(i):
            b, p = work[i]
            return _dot_nt(q_ref[CH * b:CH * (b + 1), CH * p:CH * (p + 1)], kks[b][p // 2])

        def unnormalised(s, sink):
            m = jnp.maximum(jnp.max(s, axis=-1, keepdims=True), sink)
            p = jnp.exp(s - m)
            return p, 1.0 / (jnp.sum(p, axis=-1, keepdims=True) + jnp.exp(sink - m))

        def value_product(i):
            b, p = work[i]
            pr, ie, io = probs[i]
            return _dot(pr, vvs[b][p // 2]) * jnp.where(even_lanes, ie, io)

        ahead = 3
        outs, probs = [], []
        pending = [scores(i) for i in range(ahead)]
        for i, (b, p) in enumerate(work):
            s = jnp.where(masks[b], pending.pop(0), -1e30)
            if i + ahead < len(work):
                pending.append(scores(i + ahead))
            pe, ie = unnormalised(s[:, :2 * CH], sk_ref[2 * p])
            po, io = unnormalised(s[:, 2 * CH:], sk_ref[2 * p + 1])
            probs.append((jnp.concatenate([pe, po], axis=1).astype(BF16), ie, io))
            if i >= 1:
                outs.append(value_product(i - 1))
        outs.append(value_product(len(work) - 1))
        for b in range(nblk):
            o_ref[CH * b:CH * (b + 1), :] = jnp.concatenate(outs[npairs * b:npairs * (b + 1)], axis=1).astype(BF16)

    prev = lambda n: (jnp.maximum(nblk * n - 1, 0), 0)
    cur = lambda n: (n, 0)
    return _call(
        body, (sinks, q, k, k, va, va), name="attn_fwd", grid=(nsteps,),
        in_specs=[pl.BlockSpec(memory_space=pltpu.SMEM), pl.BlockSpec((nblk * CH, D), cur),
                  pl.BlockSpec((CH, KVW), prev), pl.BlockSpec((nblk * CH, KVW), cur),
                  pl.BlockSpec((CH, KVW), prev), pl.BlockSpec((nblk * CH, KVW), cur)],
        out_specs=pl.BlockSpec((nblk * CH, D), cur), out_shape=jax.ShapeDtypeStruct((T, D), BF16),
        sem=("parallel",), comm=comm)


def _attn_bwd(q, k, va, datt, sinks, rc, rs1, rs2, comm=None):
    T = q.shape[0]
    nb = T // CH

    def body(sk_ref, q_ref, kp_ref, kc_ref, vp_ref, vc_ref, do_ref, cq_ref, s1q_ref, s2q_ref, ck_ref, s1k_ref, s2k_ref,
             dq_ref, dk_ref, dv_ref, dsk_ref, kcar, vcar):
        n = pl.program_id(0)

        @pl.when(n == 0)
        def _():
            kcar[...] = jnp.zeros_like(kcar)
            vcar[...] = jnp.zeros_like(vcar)
            dsk_ref[...] = jnp.zeros_like(dsk_ref)

        def flush(kprev, vprev):
            ck, s1k, s2k = ck_ref[...], s1k_ref[...], s2k_ref[...]
            for j in range(KVW // CH):
                sl = slice(CH * j, CH * (j + 1))
                dk_ref[:, sl] = _rope_t(kcar[:, sl] + kprev[:, sl], ck, s1k, s2k).astype(BF16)
                dv_ref[:, sl] = (vcar[:, sl] + vprev[:, sl]).astype(BF16)

        @pl.when(n < nb)
        def _():
            mask = _attn_mask(n)
            kp, kc, vp, vc = kp_ref[...], kc_ref[...], vp_ref[...], vc_ref[...]
            cq, s1q, s2q = cq_ref[...], s1q_ref[...], s2q_ref[...]
            lane = lax.broadcasted_iota(jnp.int32, (1, CH), 1)
            dsk = jnp.zeros((1, CH), F32)
            npairs = D // CH
            kks = [_pair_layout(kp, kc, grp) for grp in range(NKV)]
            vvs = [_pair_layout(vp, vc, grp) for grp in range(NKV)]
            qs = [q_ref[:, CH * p:CH * (p + 1)] for p in range(npairs)]
            dos = [do_ref[:, CH * p:CH * (p + 1)].astype(BF16) for p in range(npairs)]

            def first(p):
                return _dot_nt(qs[p], kks[p // 2]), _dot_nt(dos[p], vvs[p // 2])

            def last(p, ds, pb):
                return (_rope_t(_dot(ds, kks[p // 2]), cq, s1q, s2q) * Q_SCALE, _dot_tn(qs[p], ds), _dot_tn(dos[p], pb))

            ahead = 2
            pending = [first(p) for p in range(ahead)]
            mids, ends = [], []
            for p in range(npairs):
                s, dp = pending.pop(0)
                s = jnp.where(mask, s, -1e30)
                if p + ahead < npairs:
                    pending.append(first(p + ahead))
                ds_parts, p_parts = [], []
                for par in range(2):
                    sl = slice(2 * CH * par, 2 * CH * (par + 1))
                    pr, psink = _softmax_sink(s[:, sl], sk_ref[2 * p + par])
                    delta = jnp.sum(pr * dp[:, sl], axis=-1, keepdims=True)
                    ds_parts.append(pr * (dp[:, sl] - delta))
                    p_parts.append(pr)
                    tot = -jnp.sum(psink * delta, axis=0, keepdims=True)
                    dsk = dsk + jnp.where(lane == 2 * p + par, tot, 0.0)
                mids.append((jnp.concatenate(ds_parts, axis=1).astype(BF16), jnp.concatenate(p_parts, axis=1).astype(BF16)))
                if p >= 1:
                    ends.append(last(p - 1, *mids[p - 1]))
            ends.append(last(npairs - 1, *mids[-1]))
            dq_cols = [e[0] for e in ends]
            def fold(i):
                rows = []
                for grp in range(NKV):
                    acc = ends[2 * grp][i] + ends[2 * grp + 1][i]
                    rows.append(acc[:HD, :2 * CH] + acc[HD:, 2 * CH:])
                return jnp.concatenate(rows, axis=0).T

            dkf, dvf = fold(1), fold(2)
            dq_ref[...] = jnp.concatenate(dq_cols, axis=1).astype(BF16)
            dsk_ref[...] += dsk
            flush(dkf[:CH], dvf[:CH])
            kcar[...] = dkf[CH:]
            vcar[...] = dvf[CH:]

        @pl.when(n == nb)
        def _():
            z = jnp.zeros((CH, KVW), F32)
            flush(z, z)

    last = nb - 1
    cur = lambda n: (jnp.minimum(n, last), 0)
    prev = lambda n: (jnp.clip(n - 1, 0, last), 0)
    sd = jax.ShapeDtypeStruct
    return _call(
        body, (sinks, q, k, k, va, va, datt, rc, rs1, rs2, rc, rs1, rs2), name="attn_bwd", grid=(nb + 1,),
        in_specs=[pl.BlockSpec(memory_space=pltpu.SMEM), pl.BlockSpec((CH, D), cur),
                  pl.BlockSpec((CH, KVW), prev), pl.BlockSpec((CH, KVW), cur),
                  pl.BlockSpec((CH, KVW), prev), pl.BlockSpec((CH, KVW), cur),
                  pl.BlockSpec((CH, D), cur),
                  pl.BlockSpec((CH, CH), cur), pl.BlockSpec((CH, CH), cur), pl.BlockSpec((CH, CH), cur),
                  pl.BlockSpec((CH, CH), prev), pl.BlockSpec((CH, CH), prev), pl.BlockSpec((CH, CH), prev)],
        out_specs=[pl.BlockSpec((CH, D), cur), pl.BlockSpec((CH, KVW), prev), pl.BlockSpec((CH, KVW), prev),
                   _const((1, CH))],
        out_shape=[sd((T, D), BF16), sd((T, KVW), BF16), sd((T, KVW), BF16), sd((1, CH), F32)],
        scratch_shapes=[pltpu.VMEM((CH, KVW), F32), pltpu.VMEM((CH, KVW), F32)], sem=("arbitrary",), comm=comm)


def _merge_fwd(a, att, ga, gb, x, w_a, w_b, w_o, g2, tm, comm=None):
    T = x.shape[0]

    def body(a_ref, att_ref, ga_ref, gb_ref, x_ref, wa_ref, wb_ref, wo_ref, g_ref,
             pa_ref, pb_ref, mg_ref, mix_ref, x1_ref):
        pa = _dot(a_ref[...], wa_ref[...])
        pb = _dot(att_ref[...], wb_ref[...])
        pa_ref[...] = pa.astype(BF16)
        pb_ref[...] = pb.astype(BF16)
        merged = (_sigmoid(ga_ref[...].astype(F32)) * pa + _sigmoid(gb_ref[...].astype(F32)) * pb).astype(BF16)
        mg_ref[...] = merged
        mix = _dot(merged, wo_ref[...])
        mix_ref[...] = mix
        mhat, _ = _rms_hat(mix)
        x1_ref[...] = x_ref[...] + mhat * g_ref[...]

    sd = jax.ShapeDtypeStruct
    return _call(
        body, (a, att, ga, gb, x, w_a, w_b, w_o, g2), name="merge_fwd", grid=(T // tm,),
        in_specs=[_rows(tm, D)] * 5 + [_resident((D, D))] * 3 + [_const((1, D))],
        out_specs=[_rows(tm, D)] * 5,
        out_shape=[sd((T, D), BF16), sd((T, D), BF16), sd((T, D), BF16), sd((T, D), F32), sd((T, D), F32)],
        sem=("parallel",), comm=comm)


def _merge_bwd(dx1, mix, ga, gb, pa, pb, a, att, merged, w_a, w_b, w_o, g2, tm, comm=None):
    T = dx1.shape[0]
    nsteps = T // tm

    def body(dx1_ref, mix_ref, ga_ref, gb_ref, pa_ref, pb_ref, a_ref, att_ref, mg_ref, wa_ref, wb_ref, wo_ref, g_ref,
             dga_ref, dgb_ref, da_ref, datt_ref, dg_ref, dwa_ref, dwb_ref, dwo_ref, acc, sem):
        i = pl.program_id(0)

        @pl.when(i == 0)
        def _():
            dg_ref[...] = jnp.zeros_like(dg_ref)
            acc[...] = jnp.zeros_like(acc)

        mhat, r = _rms_hat(mix_ref[...])
        dmix, dg = _rms_bwd(mhat, r, g_ref[...], dx1_ref[...])
        dg_ref[...] += dg
        dmix = dmix.astype(BF16)
        dmerged = _dot_nt(dmix, wo_ref[...])
        sa = _sigmoid(ga_ref[...].astype(F32))
        sb = _sigmoid(gb_ref[...].astype(F32))
        dao = (dmerged * sa).astype(BF16)
        dbo = (dmerged * sb).astype(BF16)
        dga_ref[...] = (dmerged * pa_ref[...].astype(F32) * (sa * (1.0 - sa))).astype(BF16)
        dgb_ref[...] = (dmerged * pb_ref[...].astype(F32) * (sb * (1.0 - sb))).astype(BF16)
        da_ref[...] = _dot_nt(dao, wa_ref[...])
        datt_ref[...] = _dot_nt(dbo, wb_ref[...]).astype(BF16)
        acc[0] += _dot_tn(a_ref[...], dao)
        acc[1] += _dot_tn(att_ref[...], dbo)
        acc[2] += _dot_tn(mg_ref[...], dmix)

        @pl.when(i == nsteps - 1)
        def _():
            outs = [pltpu.make_async_copy(acc.at[j], ref, sem.at[j]) for j, ref in enumerate((dwa_ref, dwb_ref, dwo_ref))]
            for cp in outs:
                cp.start()
            for cp in outs:
                cp.wait()

    sd = jax.ShapeDtypeStruct
    return _call(
        body, (dx1, mix, ga, gb, pa, pb, a, att, merged, w_a, w_b, w_o, g2), name="merge_bwd", grid=(nsteps,),
        in_specs=[_rows(tm, D)] * 9 + [_resident((D, D))] * 3 + [_const((1, D))],
        out_specs=[_rows(tm, D)] * 4 + [_const((1, D))] + [ANY] * 3,
        out_shape=[sd((T, D), BF16), sd((T, D), BF16), sd((T, D), F32), sd((T, D), BF16), sd((1, D), F32)]
        + [sd((D, D), F32)] * 3,
        scratch_shapes=[pltpu.VMEM((3, D, D), F32), _dma_sems(3)], sem=("arbitrary",), comm=comm)


def _ffn(x1, target, w1, w2, g3, g4, tm):
    T = x1.shape[0]

    def body(x_ref, t_ref, w1_ref, w2_ref, g3_ref, g4_ref,
             hf_ref, f2_ref, dff_ref, df1_ref, dx_ref, ls_ref, dg3_ref, dg4_ref):
        @pl.when(pl.program_id(0) == 0)
        def _():
            ls_ref[...] = jnp.zeros_like(ls_ref)
            dg3_ref[...] = jnp.zeros_like(dg3_ref)
            dg4_ref[...] = jnp.zeros_like(dg4_ref)

        x = x_ref[...]
        g3, g4 = g3_ref[...], g4_ref[...]
        xhat, r3 = _rms_hat(x)
        hf = (xhat * g3).astype(BF16)
        hf_ref[...] = hf
        rl = jnp.maximum(_dot(hf, w1_ref[...]), 0.0)
        f2 = (rl * rl).astype(BF16)
        f2_ref[...] = f2
        fhat, r4 = _rms_hat(_dot(f2, w2_ref[...]))
        err = x + fhat * g4 - t_ref[...]
        ls_ref[...] += jnp.sum(err * err, axis=0, keepdims=True)
        dy = err * (1.0 / D)
        dff, dg4 = _rms_bwd(fhat, r4, g4, dy)
        dg4_ref[...] += dg4
        dff = dff.astype(BF16)
        dff_ref[...] = dff
        df1 = (_dot_nt(dff, w2_ref[...]) * (2.0 * rl)).astype(BF16)
        df1_ref[...] = df1
        dxn, dg3 = _rms_bwd(xhat, r3, g3, _dot_nt(df1, w1_ref[...]))
        dg3_ref[...] += dg3
        dx_ref[...] = dy + dxn

    sd = jax.ShapeDtypeStruct
    return pl.pallas_call(
        body, name="ffn_fwd_bwd", grid=(T // tm,),
        in_specs=[_rows(tm, D), _rows(tm, D), _resident((D, DFF)), _resident((DFF, D)), _const((1, D)), _const((1, D))],
        out_specs=[_rows(tm, D), _rows(tm, DFF), _rows(tm, D), _rows(tm, DFF), _rows(tm, D), _const((1, D)),
                   _const((1, D)), _const((1, D))],
        out_shape=[sd((T, D), BF16), sd((T, DFF), BF16), sd((T, D), BF16), sd((T, DFF), BF16), sd((T, D), F32),
                   sd((1, D), F32), sd((1, D), F32), sd((1, D), F32)],
        compiler_params=pltpu.CompilerParams(vmem_limit_bytes=VMEM_PHYSICAL, dimension_semantics=("arbitrary",)),
    )(x1, target, w1, w2, g3, g4)


def _inproj_bwd(parts, x, dx1, g1, w_in, tm, comm=None):
    T = x.shape[0]
    widths = [p.shape[1] for p in parts]
    offs = [sum(widths[:i]) for i in range(len(widths) + 1)]
    assert offs[-1] == IN_W

    def body(*refs):
        n = len(parts)
        prefs = refs[:n]
        x_ref, dx1_ref, g_ref, w_ref, dx_ref, dp_ref, dg_ref = refs[n:]

        @pl.when(pl.program_id(0) == 0)
        def _():
            dg_ref[...] = jnp.zeros_like(dg_ref)

        for i in range(n):
            dp_ref[:, offs[i]:offs[i + 1]] = prefs[i][...]
        dh = _dot_nt(dp_ref[...], w_ref[...])
        xhat, r = _rms_hat(x_ref[...])
        dxn, dg = _rms_bwd(xhat, r, g_ref[...], dh)
        dg_ref[...] += dg
        dx_ref[...] = dx1_ref[...] + dxn

    sd = jax.ShapeDtypeStruct
    return _call(
        body, (*parts, x, dx1, g1, w_in), name="inproj_bwd", grid=(T // tm,),
        in_specs=[_rows(tm, w) for w in widths] + [_rows(tm, D), _rows(tm, D), _const((1, D)), _resident((D, IN_W))],
        out_specs=[_rows(tm, D), _rows(tm, IN_W), _const((1, D))],
        out_shape=[sd((T, D), F32), sd((T, IN_W), BF16), sd((1, D), F32)], sem=("arbitrary",), comm=comm)


def _wgrad(a, g, tn, tm, name, comm=None, vmem=None):
    T, K = a.shape
    N = g.shape[1]

    def body(a_ref, g_ref, o_ref):
        @pl.when(pl.program_id(1) == 0)
        def _():
            o_ref[...] = jnp.zeros_like(o_ref)

        o_ref[...] += _dot_tn(a_ref[...], g_ref[...])

    return _call(
        body, (a, g), name=name, grid=(N // tn, T // tm),
        in_specs=[pl.BlockSpec((tm, K), lambda j, t: (t, 0)), pl.BlockSpec((tm, tn), lambda j, t: (t, j))],
        out_specs=pl.BlockSpec((K, tn), lambda j, t: (0, j)),
        out_shape=jax.ShapeDtypeStruct((K, N), F32), sem=("parallel", "arbitrary"), comm=comm, vmem=vmem)


def _adamw(ws, gs, ms, vs, trs, name):
    n = len(ws)
    walk = _Walk(w.shape[0] // tr for w, tr in zip(ws, trs))
    bc1 = 1.0 / (1.0 - B1 ** STEP)
    bc2 = 1.0 / (1.0 - B2 ** STEP)

    def body(*refs):
        i = pl.program_id(0)
        for k in range(n):
            mine = tuple(refs[j * n + k] for j in range(8))

            @pl.when(walk.mine(k, i))
            def _(mine=mine):
                w_ref, g_ref, m_ref, v_ref, go_ref, d_ref, nm_ref, nv_ref = mine
                g = g_ref[...]
                go_ref[...] = g
                m = B1 * m_ref[...] + (1.0 - B1) * g
                v = B2 * v_ref[...] + (1.0 - B2) * (g * g)
                nm_ref[...] = m
                nv_ref[...] = v
                d_ref[...] = -LR * ((m * bc1) / (jnp.sqrt(v * bc2) + AEPS) + WD * w_ref[...])

    def spec(k):
        return pl.BlockSpec((trs[k], ws[k].shape[1]), lambda i: (walk.tile(k, i), 0))

    specs = [spec(k) for k in range(n)]
    res = pl.pallas_call(
        body, name=name, grid=(walk.steps,), in_specs=specs * 4, out_specs=specs * 4,
        out_shape=[jax.ShapeDtypeStruct(w.shape, F32) for w in ws] * 4,
        compiler_params=_cparams(("arbitrary",)),
    )(*ws, *gs, *ms, *vs)
    return [tuple(res[j * n + k] for j in range(4)) for k in range(n)]


BIG = (("col", (D, IN_W)), ("row", (D, D)), ("row", (D, D)), ("row", (D, D)), ("col", (D, DFF)), ("row", (DFF, D)))
NBIG = len(BIG)
ANY = pl.BlockSpec(memory_space=pl.ANY)


def _shard_shape(kind, shape):
    R, C = shape
    return (R, C // 4) if kind == "col" else (R // 4, C)


def _half_shape(kind, shape):
    R, C = shape
    return (R // 2, C) if kind == "col" else (R, C // 2)


def _piece_shape(kind, shape):
    R, C = shape
    return (R // 2, C // 4) if kind == "col" else (R // 4, C // 2)


def _own_region(ref, kind, shape, s):
    R, C = shape
    return ref.at[:, pl.ds(s * (C // 4), C // 4)] if kind == "col" else ref.at[pl.ds(s * (R // 4), R // 4), :]


def _ag_region(ref, kind, shape, s, hc):
    R, C = shape
    if kind == "col":
        return ref.at[pl.ds(hc * (R // 2), R // 2), pl.ds(s * (C // 4), C // 4)]
    return ref.at[pl.ds(s * (R // 4) + hc * (R // 8), R // 8), :]


def _ag_shard_half(ref, kind, shape, hc):
    R, C = shape
    return ref.at[pl.ds(hc * (R // 2), R // 2), :] if kind == "col" else ref.at[pl.ds(hc * (R // 8), R // 8), :]


def _grad_half(ref, kind, shape, hc):
    R, C = shape
    return ref.at[pl.ds(hc * (R // 2), R // 2), :] if kind == "col" else ref.at[:, pl.ds(hc * (C // 2), C // 2)]


def _half_piece(ref, kind, shape, s):
    R, C = shape
    return ref.at[:, pl.ds(s * (C // 4), C // 4)] if kind == "col" else ref.at[pl.ds(s * (R // 4), R // 4), :]


def _place():
    x, y, c = lax.axis_index("x"), lax.axis_index("y"), lax.axis_index("c")
    chips = [(1 - x, y), (x, 1 - y), (1 - x, 1 - y)]
    return x, y, c, chips


def _rcopy(src, dst, ssem, rsem, dev):
    return pltpu.make_async_remote_copy(src_ref=src, dst_ref=dst, send_sem=ssem, recv_sem=rsem,
                                        device_id=dev, device_id_type=MESH)


def _dma_sems(n):
    return pltpu.SemaphoreType.DMA((n,))


def _x_gather_ici(shards, ws):
    n = len(ws)
    specs = [BIG[w] for w in ws]

    def place():
        x, y, c, chips = _place()
        return c, chips, 2 * x + y

    def sends(sh, full, sc):
        c, chips, me_s = place()
        return [_rcopy(_ag_shard_half(sh[i], kind, shape, c), _ag_region(full[i], kind, shape, me_s, c),
                       sc[0].at[3 * i + j], sc[1].at[3 * i + j], (cx, cy, c))
                for i, (kind, shape) in enumerate(specs) for j, (cx, cy) in enumerate(chips)]

    def start(sh, full, sc):
        for i in range(n):
            pltpu.make_async_copy(sh[i], sc[4 + i], sc[2].at[i]).start()
        for cp in sends(sh, full, sc):
            cp.start()

    def finish(sh, full, sc):
        c, chips, me_s = place()
        stores = []
        for i, (kind, shape) in enumerate(specs):
            pltpu.make_async_copy(sh[i], sc[4 + i], sc[2].at[i]).wait()
            st = pltpu.make_async_copy(sc[4 + i], _own_region(full[i], kind, shape, me_s), sc[3].at[i])
            st.start()
            stores.append(st)
        for i, (kind, shape) in enumerate(specs):
            for j, (cx, cy) in enumerate(chips):
                reg = _ag_region(full[i], kind, shape, 2 * cx + cy, c)
                _rcopy(reg, reg, sc[0].at[3 * i + j], sc[1].at[3 * i + j], (cx, cy, c)).wait_recv()
        for cp in sends(sh, full, sc):
            cp.wait_send()
        for st in stores:
            st.wait()

    return _Exchange(
        shards, [jax.ShapeDtypeStruct(shape, BF16) for _, shape in specs], {},
        [_dma_sems(3 * n), _dma_sems(3 * n), _dma_sems(n), _dma_sems(n)]
        + [pltpu.VMEM(_shard_shape(k, s), BF16) for k, s in specs], start, finish)


def _x_gather_d2d(wholes, ws):
    specs = [BIG[w] for w in ws]
    n = len(ws)

    def copies(full, sc, mine):
        x, y, c, chips = _place()
        hc = c if mine else 1 - c
        return [_rcopy(reg, reg, sc[0].at[3 * i + j], sc[1].at[3 * i + j], (x, y, 1 - c))
                for i, (kind, shape) in enumerate(specs) for j, (cx, cy) in enumerate(chips)
                for reg in [_ag_region(full[i], kind, shape, 2 * cx + cy, hc)]]

    def start(_, full, sc):
        for cp in copies(full, sc, True):
            cp.start()

    def finish(_, full, sc):
        for cp in copies(full, sc, False):
            cp.wait_recv()
        for cp in copies(full, sc, True):
            cp.wait_send()

    return _Exchange(wholes, [jax.ShapeDtypeStruct(shape, BF16) for _, shape in specs], {i: i for i in range(n)},
                     [_dma_sems(3 * n), _dma_sems(3 * n)], start, finish)


def _x_grads_sibling(grads, ws):
    specs = [BIG[w] for w in ws]
    n = len(ws)

    def copies(g, got, sc):
        x, y, c, _ = _place()
        return [_rcopy(_grad_half(g[i], kind, shape, 1 - c), got[i], sc[0].at[i], sc[1].at[i], (x, y, 1 - c))
                for i, (kind, shape) in enumerate(specs)]

    def start(g, got, sc):
        for cp in copies(g, got, sc):
            cp.start()

    def finish(g, got, sc):
        for cp in copies(g, got, sc):
            cp.wait_recv()
        for cp in copies(g, got, sc):
            cp.wait_send()

    return _Exchange(grads, [jax.ShapeDtypeStruct(_half_shape(k, s), F32) for k, s in specs], {},
                     [_dma_sems(n), _dma_sems(n)], start, finish)


def _x_grads_chips(sums_bf, ws):
    specs = [BIG[w] for w in ws]
    n = len(ws)

    def copies(s16, got, sc):
        x, y, c, chips = _place()
        return [_rcopy(_half_piece(s16[i], kind, shape, 2 * cx + cy), got[i].at[j],
                       sc[0].at[3 * i + j], sc[1].at[3 * i + j], (cx, cy, c))
                for i, (kind, shape) in enumerate(specs) for j, (cx, cy) in enumerate(chips)]

    def start(s16, got, sc):
        for cp in copies(s16, got, sc):
            cp.start()

    def finish(s16, got, sc):
        for cp in copies(s16, got, sc):
            cp.wait_recv()
        for cp in copies(s16, got, sc):
            cp.wait_send()

    return _Exchange(sums_bf, [jax.ShapeDtypeStruct((3,) + _piece_shape(k, s), BF16) for k, s in specs], {},
                     [_dma_sems(3 * n), _dma_sems(3 * n)], start, finish)


def _shard_half(ref, kind, shape, hc):
    sr, sc = _shard_shape(kind, shape)
    return ref.at[pl.ds(hc * (sr // 2), sr // 2), :] if kind == "col" else ref.at[:, pl.ds(hc * (sc // 2), sc // 2)]


def _x_grads_share(shard_grads, ws):
    specs = [BIG[w] for w in ws]
    n = len(ws)

    def copies(g, sc, mine):
        x, y, c, _ = _place()
        hc = c if mine else 1 - c
        return [_rcopy(part, part, sc[0].at[i], sc[1].at[i], (x, y, 1 - c))
                for i, (kind, shape) in enumerate(specs) for part in [_shard_half(g[i], kind, shape, hc)]]

    def start(_, g, sc):
        for cp in copies(g, sc, True):
            cp.start()

    def finish(_, g, sc):
        for cp in copies(g, sc, False):
            cp.wait_recv()
        for cp in copies(g, sc, True):
            cp.wait_send()

    return _Exchange(shard_grads, [jax.ShapeDtypeStruct(_shard_shape(k, s), F32) for k, s in specs],
                     {i: i for i in range(n)}, [_dma_sems(n), _dma_sems(n)], start, finish)


ADD_BLOCK_BYTES = 4 * 1024 * 1024


def _add_rows(rows, cols, n_arrays):
    limit = ADD_BLOCK_BYTES // (1 if n_arrays == 1 else 4)
    r = rows
    while r > 64 and r * cols * 4 > limit:
        r //= 2
    return r


class _Walk:
    def __init__(self, tiles):
        self.tiles = list(tiles)
        self.starts = [sum(self.tiles[:k]) for k in range(len(self.tiles))]
        self.steps = sum(self.tiles)

    def tile(self, k, i):
        return jnp.clip(i - self.starts[k], 0, self.tiles[k] - 1)

    def mine(self, k, i):
        return (i >= self.starts[k]) & (i < self.starts[k] + self.tiles[k])


def _add_halves(place, gs, gots, kinds, name):
    n = len(gs)
    halves = [_half_shape(kind, g.shape) for g, kind in zip(gs, kinds)]
    rows = [_add_rows(hr, hc, n) for hr, hc in halves]
    walk = _Walk(hr // r for (hr, _), r in zip(halves, rows))

    def body(p_ref, *refs):
        i = pl.program_id(0)
        for k in range(n):
            g_ref, b_ref, s_ref, sb_ref = (refs[j * n + k] for j in range(4))

            @pl.when(walk.mine(k, i))
            def _(g_ref=g_ref, b_ref=b_ref, s_ref=s_ref, sb_ref=sb_ref):
                s = g_ref[...] + b_ref[...]
                s_ref[...] = s
                sb_ref[...] = s.astype(BF16)

    def g_spec(k):
        if kinds[k] == "col":
            return pl.BlockSpec((rows[k], gs[k].shape[1]), lambda i, p: (p[0] * walk.tiles[k] + walk.tile(k, i), 0))
        return pl.BlockSpec((rows[k], halves[k][1]), lambda i, p: (walk.tile(k, i), p[0]))

    def spec(k):
        return pl.BlockSpec((rows[k], halves[k][1]), lambda i, p: (walk.tile(k, i), 0))

    specs = [spec(k) for k in range(n)]
    res = pl.pallas_call(
        body, name=name,
        grid_spec=pltpu.PrefetchScalarGridSpec(num_scalar_prefetch=1, grid=(walk.steps,),
                                               in_specs=[g_spec(k) for k in range(n)] + specs, out_specs=specs + specs),
        out_shape=[jax.ShapeDtypeStruct(h, F32) for h in halves] + [jax.ShapeDtypeStruct(h, BF16) for h in halves],
        compiler_params=_cparams(("arbitrary",)),
    )(place, *gs, *gots)
    return [(res[k], res[n + k]) for k in range(n)]


def _add_pieces(place, halves, gots, specs_big, name):
    n = len(halves)
    pieces = [_piece_shape(kind, shape) for kind, shape in specs_big]
    rows = [_add_rows(pr, pc, n) for pr, pc in pieces]
    walk = _Walk(pr // r for (pr, _), r in zip(pieces, rows))

    def body(p_ref, *refs):
        i = pl.program_id(0)
        for k in range(n):
            m_ref, g_ref, o_ref = (refs[j * n + k] for j in range(3))

            @pl.when(walk.mine(k, i))
            def _(m_ref=m_ref, g_ref=g_ref, o_ref=o_ref):
                acc = m_ref[...]
                for j in range(3):
                    acc = acc + g_ref[j].astype(F32)
                o_ref[...] = acc

    def m_spec(k):
        if specs_big[k][0] == "col":
            return pl.BlockSpec((rows[k], pieces[k][1]), lambda i, p: (walk.tile(k, i), p[1]))
        return pl.BlockSpec((rows[k], pieces[k][1]), lambda i, p: (p[1] * walk.tiles[k] + walk.tile(k, i), 0))

    def got_spec(k):
        return pl.BlockSpec((3, rows[k], pieces[k][1]), lambda i, p: (0, walk.tile(k, i), 0))

    def o_spec(k):
        if specs_big[k][0] == "col":
            return pl.BlockSpec((rows[k], pieces[k][1]), lambda i, p: (p[0] * walk.tiles[k] + walk.tile(k, i), 0))
        return pl.BlockSpec((rows[k], pieces[k][1]), lambda i, p: (walk.tile(k, i), p[0]))

    return pl.pallas_call(
        body, name=name,
        grid_spec=pltpu.PrefetchScalarGridSpec(
            num_scalar_prefetch=1, grid=(walk.steps,),
            in_specs=[m_spec(k) for k in range(n)] + [got_spec(k) for k in range(n)],
            out_specs=[o_spec(k) for k in range(n)]),
        out_shape=[jax.ShapeDtypeStruct(_shard_shape(kind, shape), F32) for kind, shape in specs_big],
        compiler_params=_cparams(("arbitrary",)),
    )(place, *halves, *gots)


SMALL_ROWS = 1024 + 8 * 8 + 8


def _x_small_all_reduce(p):
    def parts(p_ref, sc):
        slots, ssem, rsem = sc[0], sc[2], sc[3]
        x, y, c = lax.axis_index("x"), lax.axis_index("y"), lax.axis_index("c")
        me = 4 * x + 2 * y + c
        out = []
        for r in range(1, 8):
            bx, by, bc = (r >> 2) & 1, (r >> 1) & 1, r & 1
            tgt = (1 - x if bx else x, 1 - y if by else y, 1 - c if bc else c)
            send = _rcopy(p_ref, slots.at[me], ssem.at[r - 1], rsem.at[r - 1], tgt)
            src = 4 * tgt[0] + 2 * tgt[1] + tgt[2]
            recv = _rcopy(p_ref, slots.at[src], ssem.at[r - 1], rsem.at[r - 1], tgt)
            out.append((send, recv))
        return me, out

    def start(ins, outs, sc):
        me, cps = parts(ins[0], sc)
        pltpu.make_async_copy(ins[0], sc[0].at[me], sc[4].at[0]).start()
        for send, _ in cps:
            send.start()

    def finish(ins, outs, sc):
        me, cps = parts(ins[0], sc)
        pltpu.make_async_copy(ins[0], sc[0].at[me], sc[4].at[0]).wait()
        for _, recv in cps:
            recv.wait_recv()
        acc = sc[0][0]
        for d in range(1, 8):
            acc = acc + sc[0][d]
        sc[1][...] = acc
        back = pltpu.make_async_copy(sc[1], outs[0], sc[4].at[1])
        back.start()
        for send, _ in cps:
            send.wait_send()
        back.wait()

    return _Exchange([p], [jax.ShapeDtypeStruct((SMALL_ROWS, CH), F32)], {},
                     [pltpu.VMEM((8, SMALL_ROWS, CH), F32), pltpu.VMEM((SMALL_ROWS, CH), F32), _dma_sems(7), _dma_sems(7),
                      _dma_sems(2)], start, finish)


def _rope_tables(positions, comm=None):
    T = positions.shape[0]
    inv_freq = 500000.0 ** (-jnp.arange(0, 2 * ROPE_HALF, 2, dtype=F32) / (2 * ROPE_HALF))
    head = jnp.concatenate([inv_freq, inv_freq, jnp.zeros((HD - 2 * ROPE_HALF,), F32)])
    lane_freq = jnp.concatenate([head, head])[None, :]
    pos = jnp.broadcast_to(positions.astype(F32)[:, None], (T, CH))
    tm = min(1024, T)

    def body(p_ref, f_ref, c_ref, s1_ref, s2_ref):
        ang = p_ref[...] * f_ref[...]
        sin = jnp.sin(ang)
        first = (lax.broadcasted_iota(jnp.int32, ang.shape, 1) % HD) < ROPE_HALF
        c_ref[...] = jnp.cos(ang)
        s1_ref[...] = jnp.where(first, -sin, 0.0)
        s2_ref[...] = jnp.where(first, 0.0, sin)

    return _call(body, (pos, lane_freq), name="rope_tables", grid=(T // tm,),
                 in_specs=[_rows(tm, CH), _const((1, CH))], out_specs=[_rows(tm, CH)] * 3,
                 out_shape=[jax.ShapeDtypeStruct((T, CH), F32)] * 3, sem=("parallel",), comm=comm)


BIG_NAMES = ("w_in", "w_a", "w_b", "w_o", "w_ff_in", "w_ff_out")
SMALL_NAMES = ("w_spatial", "ln_v_gain", "ln_v_bias", "b_spatial", "sinks", "norm_mix_pre", "norm_mix_post",
               "norm_ff_pre", "norm_ff_post")
WEIGHT_ORDER = ("w_in", "ln_v_gain", "ln_v_bias", "w_spatial", "b_spatial", "sinks", "w_a", "w_b", "w_o",
                "norm_mix_pre", "norm_mix_post", "w_ff_in", "w_ff_out", "norm_ff_pre", "norm_ff_post")


def _pack_small(d, loss_sums=None):
    parts = []
    for n in SMALL_NAMES:
        flat = d[n].reshape(-1)
        pad = (-flat.shape[0]) % (8 * CH)
        parts.append(jnp.pad(flat, (0, pad)).reshape(-1, CH))
    parts.append(jnp.zeros((8, CH), F32) if loss_sums is None else loss_sums.reshape(8, CH))
    return jnp.concatenate(parts, axis=0)


def _unpack_small(p, like):
    out, row = {}, 0
    for n in SMALL_NAMES:
        size = like[n].size
        rows = -(-size // (8 * CH)) * 8
        out[n] = p[row:row + rows].reshape(-1)[:size].reshape(like[n].shape)
        row += rows
    return out


def kernel(x, positions, w_in, ln_v_gain, ln_v_bias, w_spatial, b_spatial, sinks, w_a, w_b, w_o, norm_mix_pre, norm_mix_post, w_ff_in, w_ff_out, norm_ff_pre, norm_ff_post, loss_target, m_w_in, m_ln_v_gain, m_ln_v_bias, m_w_spatial, m_b_spatial, m_sinks, m_w_a, m_w_b, m_w_o, m_norm_mix_pre, m_norm_mix_post, m_w_ff_in, m_w_ff_out, m_norm_ff_pre, m_norm_ff_post, v_w_in, v_ln_v_gain, v_ln_v_bias, v_w_spatial, v_b_spatial, v_sinks, v_w_a, v_w_b, v_w_o, v_norm_mix_pre, v_norm_mix_post, v_w_ff_in, v_w_ff_out, v_norm_ff_pre, v_norm_ff_post):
    w = dict(w_in=w_in, ln_v_gain=ln_v_gain, ln_v_bias=ln_v_bias, w_spatial=w_spatial, b_spatial=b_spatial, sinks=sinks,
             w_a=w_a, w_b=w_b, w_o=w_o, norm_mix_pre=norm_mix_pre, norm_mix_post=norm_mix_post, w_ff_in=w_ff_in,
             w_ff_out=w_ff_out, norm_ff_pre=norm_ff_pre, norm_ff_post=norm_ff_post)
    m = dict(w_in=m_w_in, ln_v_gain=m_ln_v_gain, ln_v_bias=m_ln_v_bias, w_spatial=m_w_spatial, b_spatial=m_b_spatial,
             sinks=m_sinks, w_a=m_w_a, w_b=m_w_b, w_o=m_w_o, norm_mix_pre=m_norm_mix_pre, norm_mix_post=m_norm_mix_post,
             w_ff_in=m_w_ff_in, w_ff_out=m_w_ff_out, norm_ff_pre=m_norm_ff_pre, norm_ff_post=m_norm_ff_post)
    v = dict(w_in=v_w_in, ln_v_gain=v_ln_v_gain, ln_v_bias=v_ln_v_bias, w_spatial=v_w_spatial, b_spatial=v_b_spatial,
             sinks=v_sinks, w_a=v_w_a, w_b=v_w_b, w_o=v_w_o, norm_mix_pre=v_norm_mix_pre, norm_mix_post=v_norm_mix_post,
             w_ff_in=v_w_ff_in, w_ff_out=v_w_ff_out, norm_ff_pre=v_norm_ff_pre, norm_ff_post=v_norm_ff_post)

    FIRST, REST = (0,), tuple(range(1, NBIG))
    shards = [w[n][0].astype(BF16) for n in BIG_NAMES]
    place = jnp.stack([lax.axis_index("c"), 2 * lax.axis_index("x") + lax.axis_index("y")]).astype(jnp.int32)
    xs, target = x[0], loss_target[0]
    T = xs.shape[0]
    tile = min(TOKEN_TILE, T)
    wtiles = {n: dict(tm=min(tm, T), tn=tn) for n, (tm, tn) in WGRAD_TILES.items()}
    g1, g2, g3, g4 = norm_mix_pre, norm_mix_post, norm_ff_pre, norm_ff_post
    w_sp, snk = w_spatial[0], sinks[0]
    MIX, FF = (1, 2, 3), (4, 5)
    bfull = jnp.repeat(b_spatial[0].T, CH, axis=1)

    def reduce_tail(ws, grads, got):
        tag = "_".join(BIG_NAMES[k] for k in ws)
        sums = _add_halves(place, grads, got, [BIG[k][0] for k in ws], name="grad_add_sibling_" + tag)
        return sums, _x_grads_chips([s[1] for s in sums], ws)

    def reduce_end(ws, sums, pieces):
        tag = "_".join(BIG_NAMES[k] for k in ws)
        return _add_pieces(place, [s[0] for s in sums], pieces, [BIG[k] for k in ws], name="grad_add_chips_" + tag)

    (rc, rs1, rs2), w_in_part = _rope_tables(positions[0], comm=_x_gather_ici(shards[:1], FIRST))
    w_in_b = _run(_x_gather_d2d(w_in_part, FIRST), "gather_w_in_d2d")[0]
    (h, u, vs, q, k, va, ga, gb), ff_part = _inproj(xs, g1, w_in_b, rc, rs1, rs2, tm=tile, comm=_x_gather_ici(shards[4:], FF))
    a, mix_part = _sgu_fwd(u, vs, ln_v_gain, ln_v_bias, w_sp, bfull, tm=tile, comm=_x_gather_ici(shards[1:4], MIX))
    att, rest = _attn_fwd(q, k, va, snk, comm=_both(_x_gather_d2d(mix_part, MIX), _x_gather_d2d(ff_part, FF)))
    w_a_b, w_b_b, w_o_b, w_ff_in_b, w_ff_out_b = rest
    (pa, pb, merged, mix, x1), _ = _merge_fwd(a, att, ga, gb, xs, w_a_b, w_b_b, w_o_b, g2, tm=tile)
    hf, f2, dff, df1, dx1, lsum, dg3, dg4 = _ffn(x1, target, w_ff_in_b, w_ff_out_b, g3, g4, tm=tile)

    dw_ff_out, _ = _wgrad(f2, dff, name="wgrad_ff_out", **wtiles["w_ff_out"])
    dw_ff_in, _ = _wgrad(hf, df1, name="wgrad_ff_in", **wtiles["w_ff_in"])
    (dga, dgb, da, datt, dg2, dw_a, dw_b, dw_o), _ = _merge_bwd(
        dx1, mix, ga, gb, pa, pb, a, att, merged, w_a_b, w_b_b, w_o_b, g2, tm=tile)
    grads_rest = [dw_a, dw_b, dw_o, dw_ff_in, dw_ff_out]
    (du, dvs, dws, dbs, dlg, dlb), got_rest = _sgu_bwd(
        u, vs, da, ln_v_gain, ln_v_bias, w_sp, bfull, tm=tile, comm=_x_grads_sibling(grads_rest, REST))
    sums_rest, to_chips = reduce_tail(REST, grads_rest, got_rest)
    (dq, dk, dva, dsk), pieces_rest = _attn_bwd(q, k, va, datt, snk, rc, rs1, rs2, comm=to_chips)
    partial_rest = reduce_end(REST, sums_rest, pieces_rest)
    (dx, dproj, dg1), _ = _inproj_bwd([du, dvs, dq, dk, dva, dga, dgb], xs, dx1, g1, w_in_b, tm=tile)
    small = dict(ln_v_gain=dlg, ln_v_bias=dlb, w_spatial=dws, b_spatial=dbs, sinks=dsk[:, :NQ],
                 norm_mix_pre=dg1, norm_mix_post=dg2, norm_ff_pre=dg3, norm_ff_post=dg4)
    dw_in, (gs, *shard_rest) = _wgrad(
        h, dproj, name="wgrad_in", vmem=VMEM_PHYSICAL, **wtiles["w_in"],
        comm=_both(_x_small_all_reduce(_pack_small(small, lsum)), _x_grads_share(partial_rest, REST)))
    got_in = _run(_x_grads_sibling([dw_in], FIRST), "grads_in_to_sibling")
    sums_in, to_chips = reduce_tail(FIRST, [dw_in], got_in)
    partial_in = reduce_end(FIRST, sums_in, _run(to_chips, "grads_in_to_chips"))
    g_in = _run(_x_grads_share(partial_in, FIRST), "grads_in_share")[0]

    loss = 0.5 * jnp.sum(gs[SMALL_ROWS - 8:]) / D
    grad, delta, new_m, new_v = {}, {}, {}, {}
    for n, g in zip(BIG_NAMES, [g_in] + list(shard_rest)):
        (g_, d_, m_, v_), = _adamw([w[n][0]], [g], [m[n][0]], [v[n][0]], [256], name="adamw_" + n)
        grad[n], delta[n], new_m[n], new_v[n] = g_[None], d_[None], m_[None], v_[None]
    (gs, ds, ms, vs), = _adamw([_pack_small(w)], [gs], [_pack_small(m)], [_pack_small(v)], [SMALL_ROWS], name="adamw_small")
    for packed, dst in ((gs, grad), (ds, delta), (ms, new_m), (vs, new_v)):
        dst.update(_unpack_small(packed, w))

    outs = [loss, dx[None]]
    for group in (grad, delta, new_m, new_v):
        outs.extend(group[n] for n in WEIGHT_ORDER)
    return tuple(outs)
```

```python
import functools

import jax
import jax.numpy as jnp
from jax import lax
from jax.experimental import pallas as pl
from jax.experimental.pallas import tpu as pltpu

F32 = jnp.float32
BF16 = jnp.bfloat16

D = 1024
CH = 128
NG = 8
HD = 64
NQ = 16
NKV = 4
KVW = NKV * HD
DFF = 4 * D
EPS = 1e-6
IN_W = 5632
SEG = (0, 1024, 2048, 3072, 3328, 3584, 4608, 5632)
ROPE_HALF = 8
Q_SCALE = HD ** -0.5

LR, B1, B2, AEPS, WD, STEP = 0.001, 0.9, 0.999, 1e-08, 0.01, 10

VMEM_PHYSICAL = 64 * 1024 * 1024
VMEM_LIMIT = 60 * 1024 * 1024
MESH = pl.DeviceIdType.MESH

TOKEN_TILE = 512
WGRAD_TILES = {"w_ff_out": (512, 1024), "w_ff_in": (2048, 2048), "w_in": (2048, IN_W // 2)}

_GELU_C0 = 0.7978845608028654
_GELU_C1 = 0.044715


def _cparams(sem=None, vmem=None):
    kw = dict(vmem_limit_bytes=VMEM_LIMIT if vmem is None else vmem)
    if sem is not None:
        kw["dimension_semantics"] = sem
    return pltpu.CompilerParams(**kw)


def _resident(shape):
    nd = len(shape)
    return pl.BlockSpec(shape, lambda *_: (0,) * nd, pipeline_mode=pl.Buffered(1))


def _const(shape):
    nd = len(shape)
    return pl.BlockSpec(shape, lambda *_: (0,) * nd)


def _rows(tm, w):
    return pl.BlockSpec((tm, w), lambda i: (i, 0))


class _Exchange:
    def __init__(self, ins, outs, aliases, scratch, start, finish):
        self.ins, self.outs, self.aliases, self.scratch = list(ins), list(outs), dict(aliases), list(scratch)
        self.start, self.finish = start, finish


def _both(a, b):
    na, ma, sa = len(a.ins), len(a.outs), len(a.scratch)

    def start(ci, co, cs):
        a.start(ci[:na], co[:ma], cs[:sa])
        b.start(ci[na:], co[ma:], cs[sa:])

    def finish(ci, co, cs):
        a.finish(ci[:na], co[:ma], cs[:sa])
        b.finish(ci[na:], co[ma:], cs[sa:])

    aliases = {**a.aliases, **{na + i: ma + j for i, j in b.aliases.items()}}
    return _Exchange(a.ins + b.ins, a.outs + b.outs, aliases, a.scratch + b.scratch, start, finish)


def _call(body, args, *, name, grid, in_specs, out_specs, out_shape, scratch_shapes=(), sem=None, comm=None, vmem=None):
    single = not isinstance(out_shape, (list, tuple))
    out_shape = [out_shape] if single else list(out_shape)
    out_specs = [out_specs] if single else list(out_specs)
    if comm is None:
        res = pl.pallas_call(body, name=name, grid=grid, in_specs=list(in_specs), out_specs=out_specs,
                             out_shape=out_shape, scratch_shapes=list(scratch_shapes),
                             compiler_params=_cparams(sem, vmem))(*args)
        return (res[0] if single else res), []
    n_in, n_out, n_scr = len(args), len(out_shape), len(scratch_shapes)
    nci, nco = len(comm.ins), len(comm.outs)
    steps = 1
    for g in grid:
        steps *= g

    def hosted(*refs):
        a, ci = refs[:n_in], refs[n_in:n_in + nci]
        o, co = refs[n_in + nci:n_in + nci + n_out], refs[n_in + nci + n_out:n_in + nci + n_out + nco]
        rest = refs[n_in + nci + n_out + nco:]
        scr, cs = rest[:n_scr], rest[n_scr:]
        step = pl.program_id(0)
        for d in range(1, len(grid)):
            step = step * grid[d] + pl.program_id(d)

        @pl.when(step == 0)
        def _():
            comm.start(ci, co, cs)

        body(*a, *o, *scr)

        @pl.when(step == steps - 1)
        def _():
            comm.finish(ci, co, cs)

    res = pl.pallas_call(
        hosted, name=name, grid=grid, in_specs=list(in_specs) + [ANY] * nci, out_specs=out_specs + [ANY] * nco,
        out_shape=out_shape + comm.outs, scratch_shapes=list(scratch_shapes) + comm.scratch,
        input_output_aliases={n_in + i: n_out + j for i, j in comm.aliases.items()},
        compiler_params=_cparams(("arbitrary",) * len(grid), vmem),
    )(*args, *comm.ins)
    own = res[:n_out]
    return (own[0] if single else own), list(res[n_out:])


def _run(comm, name):
    nci = len(comm.ins)

    def body(*refs):
        ci, co, cs = refs[:nci], refs[nci:nci + len(comm.outs)], refs[nci + len(comm.outs):]
        comm.start(ci, co, cs)
        comm.finish(ci, co, cs)

    return pl.pallas_call(
        body, name=name, in_specs=[ANY] * nci, out_specs=[ANY] * len(comm.outs), out_shape=comm.outs,
        scratch_shapes=comm.scratch, input_output_aliases=comm.aliases,
        compiler_params=pltpu.CompilerParams(vmem_limit_bytes=VMEM_LIMIT),
    )(*comm.ins)


def _gelu(x):
    x2 = x * x
    t = jnp.tanh(x * (_GELU_C0 + (_GELU_C0 * _GELU_C1) * x2))
    hx = 0.5 * x
    return hx + hx * t, (t, x2, hx)


def _gelu_grad(parts):
    t, x2, hx = parts
    return (0.5 + 0.5 * t) + hx * (1.0 - t * t) * (_GELU_C0 + (3.0 * _GELU_C0 * _GELU_C1) * x2)


def _sigmoid(x):
    return 1.0 / (1.0 + jnp.exp(-x))


def _rms_hat(x):
    r = lax.rsqrt(jnp.mean(x * x, axis=-1, keepdims=True) + EPS)
    return x * r, r


def _rms_bwd(xhat, r, g, dout):
    dg = jnp.sum(dout * xhat, axis=0, keepdims=True)
    dy = dout * g
    dx = r * (dy - xhat * jnp.mean(dy * xhat, axis=-1, keepdims=True))
    return dx, dg


def _dot(a, b):
    return jnp.dot(a, b, preferred_element_type=F32)


def _dot_nt(a, b):
    return lax.dot_general(a, b, (((1,), (1,)), ((), ())), preferred_element_type=F32)


def _dot_tn(a, b):
    return lax.dot_general(a, b, (((0,), (0,)), ((), ())), preferred_element_type=F32)


def _rope(blk, c, s1, s2):
    return blk * c + pltpu.roll(blk, CH - ROPE_HALF, 1) * s1 + pltpu.roll(blk, ROPE_HALF, 1) * s2


def _rope_t(blk, c, s1, s2):
    return blk * c + pltpu.roll(blk * s1, ROPE_HALF, 1) + pltpu.roll(blk * s2, CH - ROPE_HALF, 1)


def _inproj(x, g1, w_in, rc, rs1, rs2, tm, comm=None):
    T = x.shape[0]

    def body(x_ref, g_ref, w_ref, c_ref, s1_ref, s2_ref,
             h_ref, u_ref, v_ref, q_ref, k_ref, va_ref, ga_ref, gb_ref):
        xhat, _ = _rms_hat(x_ref[...])
        h = (xhat * g_ref[...]).astype(BF16)
        h_ref[...] = h
        uv = _dot(h, w_ref[:, SEG[0]:SEG[2]])
        u_ref[...] = uv[:, :D]
        v_ref[...] = uv[:, D:]
        c, s1, s2 = c_ref[...], s1_ref[...], s2_ref[...]
        qkv = _dot(h, w_ref[:, SEG[2]:SEG[5]])
        for p in range(D // CH):
            blk = _rope(qkv[:, CH * p:CH * (p + 1)], c, s1, s2) * Q_SCALE
            q_ref[:, CH * p:CH * (p + 1)] = blk.astype(BF16)
        for p in range(KVW // CH):
            k_ref[:, CH * p:CH * (p + 1)] = _rope(qkv[:, D + CH * p:D + CH * (p + 1)], c, s1, s2).astype(BF16)
        va_ref[...] = qkv[:, D + KVW:].astype(BF16)
        gates = _dot(h, w_ref[:, SEG[5]:SEG[7]]).astype(BF16)
        ga_ref[...] = gates[:, :D]
        gb_ref[...] = gates[:, D:]

    sd = jax.ShapeDtypeStruct
    return _call(
        body, (x, g1, w_in, rc, rs1, rs2), name="inproj_fwd", grid=(T // tm,),
        in_specs=[_rows(tm, D), _const((1, D)), _resident((D, IN_W)), _rows(tm, CH), _rows(tm, CH), _rows(tm, CH)],
        out_specs=[_rows(tm, D), _rows(tm, D), _rows(tm, D), _rows(tm, D), _rows(tm, KVW), _rows(tm, KVW),
                   _rows(tm, D), _rows(tm, D)],
        out_shape=[sd((T, D), BF16), sd((T, D), F32), sd((T, D), F32), sd((T, D), BF16), sd((T, KVW), BF16),
                   sd((T, KVW), BF16), sd((T, D), BF16), sd((T, D), BF16)],
        sem=("parallel",), comm=comm)


def _sgu_common(u, vs, lng, lnb, ws_ref, bfull):
    nc = u.shape[0] // CH
    ug, tu = _gelu(u)
    vg, tv = _gelu(vs)
    mu = jnp.mean(vg, axis=-1, keepdims=True)
    xc = vg - mu
    rstd = lax.rsqrt(jnp.mean(xc * xc, axis=-1, keepdims=True) + EPS)
    vhat = xc * rstd
    vnb = (vhat * lng + lnb).astype(BF16)
    tri = lax.broadcasted_iota(jnp.int32, (CH, CH), 0) >= lax.broadcasted_iota(jnp.int32, (CH, CH), 1)
    wts, rhss, mixed = [], [], []
    for g in range(NG):
        wt = jnp.where(tri, ws_ref[g], 0.0).astype(BF16)
        rhs = jnp.concatenate([vnb[CH * c:CH * (c + 1), CH * g:CH * (g + 1)] for c in range(nc)], axis=1)
        mix = _dot(wt, rhs)
        wts.append(wt)
        rhss.append(rhs)
        mixed.append([mix[:, CH * c:CH * (c + 1)] + bfull[:, CH * g:CH * (g + 1)] for c in range(nc)])
    return nc, ug, tu, tv, rstd, vhat, tri, wts, rhss, mixed


def _sgu_fwd(u, vs, lng, lnb, ws, bfull, tm, comm=None):
    T = u.shape[0]

    def body(u_ref, v_ref, lng_ref, lnb_ref, ws_ref, bf_ref, a_ref):
        nc, ug, _, _, _, _, _, _, _, mixed = _sgu_common(
            u_ref[...], v_ref[...], lng_ref[...], lnb_ref[...], ws_ref, bf_ref[...])
        mixed_all = jnp.concatenate(
            [jnp.concatenate([mixed[g][c] for g in range(NG)], axis=1) for c in range(nc)], axis=0)
        a_ref[...] = (ug * mixed_all).astype(BF16)

    return _call(
        body, (u, vs, lng, lnb, ws, bfull), name="sgu_fwd", grid=(T // tm,),
        in_specs=[_rows(tm, D), _rows(tm, D), _const((1, D)), _const((1, D)), _const((NG, CH, CH)), _const((CH, D))],
        out_specs=_rows(tm, D), out_shape=jax.ShapeDtypeStruct((T, D), BF16), sem=("parallel",), comm=comm)


def _sgu_bwd(u, vs, da, lng, lnb, ws, bfull, tm, comm=None):
    T = u.shape[0]
    nsteps = T // tm

    def body(u_ref, v_ref, da_ref, lng_ref, lnb_ref, ws_ref, bf_ref,
             du_ref, dv_ref, dws_ref, dbs_ref, dlg_ref, dlb_ref, db_ref):
        i = pl.program_id(0)
        u, vs, da, lng = u_ref[...], v_ref[...], da_ref[...], lng_ref[...]
        nc, ug, tu, tv, rstd, vhat, tri, wts, rhss, mixed = _sgu_common(u, vs, lng, lnb_ref[...], ws_ref, bf_ref[...])

        @pl.when(i == 0)
        def _():
            dws_ref[...] = jnp.zeros_like(dws_ref)
            db_ref[...] = jnp.zeros_like(db_ref)
            dlg_ref[...] = jnp.zeros_like(dlg_ref)
            dlb_ref[...] = jnp.zeros_like(dlb_ref)

        mixed_all = jnp.concatenate(
            [jnp.concatenate([mixed[g][c] for g in range(NG)], axis=1) for c in range(nc)], axis=0)
        du_ref[...] = (da * mixed_all * _gelu_grad(tu)).astype(BF16)
        dmixed = da * ug
        dvn_cols = []
        for g in range(NG):
            dmix = [dmixed[CH * c:CH * (c + 1), CH * g:CH * (g + 1)] for c in range(nc)]
            db_ref[:, CH * g:CH * (g + 1)] += functools.reduce(lambda a, b: a + b, dmix)
            dm = jnp.concatenate(dmix, axis=1).astype(BF16)
            dws_ref[g] += _dot_nt(dm, rhss[g])
            dvn_cols.append(_dot_tn(wts[g], dm))
        dvn = jnp.concatenate(
            [jnp.concatenate([dvn_cols[g][:, CH * c:CH * (c + 1)] for g in range(NG)], axis=1) for c in range(nc)],
            axis=0)
        dlg_ref[...] += jnp.sum(dvn * vhat, axis=0, keepdims=True)
        dlb_ref[...] += jnp.sum(dvn, axis=0, keepdims=True)
        dvh = dvn * lng
        dvg = rstd * (dvh - jnp.mean(dvh, axis=-1, keepdims=True)
                      - vhat * jnp.mean(dvh * vhat, axis=-1, keepdims=True))
        dv_ref[...] = (dvg * _gelu_grad(tv)).astype(BF16)

        @pl.when(i == nsteps - 1)
        def _():
            for g in range(NG):
                dws_ref[g] = jnp.where(tri, dws_ref[g], 0.0)
                dbs_ref[g:g + 1, :] = jnp.sum(db_ref[:, CH * g:CH * (g + 1)].T, axis=0, keepdims=True)

    sd = jax.ShapeDtypeStruct
    return _call(
        body, (u, vs, da, lng, lnb, ws, bfull), name="sgu_bwd", grid=(nsteps,),
        in_specs=[_rows(tm, D), _rows(tm, D), _rows(tm, D), _const((1, D)), _const((1, D)), _const((NG, CH, CH)),
                  _const((CH, D))],
        out_specs=[_rows(tm, D), _rows(tm, D), _const((NG, CH, CH)), _const((NG, CH)), _const((1, D)), _const((1, D))],
        out_shape=[sd((T, D), BF16), sd((T, D), BF16), sd((NG, CH, CH), F32), sd((NG, CH), F32), sd((1, D), F32),
                   sd((1, D), F32)],
        scratch_shapes=[pltpu.VMEM((CH, D), F32)], sem=("arbitrary",), comm=comm)


def _pair_layout(prev, cur, grp):
    j, half = grp // 2, grp % 2
    blk = jnp.concatenate([prev[:, CH * j:CH * (j + 1)], cur[:, CH * j:CH * (j + 1)]], axis=0).astype(F32)
    lo = lax.broadcasted_iota(jnp.int32, blk.shape, 1) < HD
    rolled = pltpu.roll(blk, HD, 1)
    even = jnp.where(lo, blk if half == 0 else rolled, 0.0)
    odd = jnp.where(lo, 0.0, rolled if half == 0 else blk)
    return jnp.concatenate([even, odd], axis=0).astype(BF16)


def _attn_mask(n):
    qi = lax.broadcasted_iota(jnp.int32, (CH, 2 * CH), 0)
    kc = lax.broadcasted_iota(jnp.int32, (CH, 2 * CH), 1)
    ok = (kc > qi) & (kc <= qi + CH) & ((kc >= CH) | (n > 0))
    return jnp.concatenate([ok, ok], axis=1)


def _softmax_sink(s, sink):
    m = jnp.maximum(jnp.max(s, axis=-1, keepdims=True), sink)
    p = jnp.exp(s - m)
    ps = jnp.exp(sink - m)
    inv = 1.0 / (jnp.sum(p, axis=-1, keepdims=True) + ps)
    return p * inv, ps * inv


QUERY_BLOCKS_PER_STEP = 2


def _attn_fwd(q, k, va, sinks, comm=None):
    T = q.shape[0]
    nblk = QUERY_BLOCKS_PER_STEP
    nsteps = T // (nblk * CH)
    npairs = D // CH

    def body(sk_ref, q_ref, kp_ref, kc_ref, vp_ref, vc_ref, o_ref):
        n = pl.program_id(0)
        even_lanes = lax.broadcasted_iota(jnp.int32, (CH, CH), 1) < HD
        ks = [kp_ref[...]] + [kc_ref[CH * b:CH * (b + 1)] for b in range(nblk)]
        vs = [vp_ref[...]] + [vc_ref[CH * b:CH * (b + 1)] for b in range(nblk)]
        masks = [_attn_mask(nblk * n)] + [_attn_mask(1)] * (nblk - 1)
        kks = [[_pair_layout(ks[b], ks[b + 1], grp) for grp in range(NKV)] for b in range(nblk)]
        vvs = [[_pair_layout(vs[b], vs[b + 1], grp) for grp in range(NKV)] for b in range(nblk)]
        work = [(b, p) for b in range(nblk) for p in range(npairs)]

        def scores(i):
            b, p = work[i]
            return _dot_nt(q_ref[CH * b:CH * (b + 1), CH * p:CH * (p + 1)], kks[b][p // 2])

        def unnormalised(s, sink):
            m = jnp.maximum(jnp.max(s, axis=-1, keepdims=True), sink)
            p = jnp.exp(s - m)
            return p, 1.0 / (jnp.sum(p, axis=-1, keepdims=True) + jnp.exp(sink - m))

        def value_product(i):
            b, p = work[i]
            pr, ie, io = probs[i]
            return _dot(pr, vvs[b][p // 2]) * jnp.where(even_lanes, ie, io)

        ahead = 3
        outs, probs = [], []
        pending = [scores(i) for i in range(ahead)]
        for i, (b, p) in enumerate(work):
            s = jnp.where(masks[b], pending.pop(0), -1e30)
            if i + ahead < len(work):
                pending.append(scores(i + ahead))
            pe, ie = unnormalised(s[:, :2 * CH], sk_ref[2 * p])
            po, io = unnormalised(s[:, 2 * CH:], sk_ref[2 * p + 1])
            probs.append((jnp.concatenate([pe, po], axis=1).astype(BF16), ie, io))
            if i >= 1:
                outs.append(value_product(i - 1))
        outs.append(value_product(len(work) - 1))
        for b in range(nblk):
            o_ref[CH * b:CH * (b + 1), :] = jnp.concatenate(outs[npairs * b:npairs * (b + 1)], axis=1).astype(BF16)

    prev = lambda n: (jnp.maximum(nblk * n - 1, 0), 0)
    cur = lambda n: (n, 0)
    return _call(
        body, (sinks, q, k, k, va, va), name="attn_fwd", grid=(nsteps,),
        in_specs=[pl.BlockSpec(memory_space=pltpu.SMEM), pl.BlockSpec((nblk * CH, D), cur),
                  pl.BlockSpec((CH, KVW), prev), pl.BlockSpec((nblk * CH, KVW), cur),
                  pl.BlockSpec((CH, KVW), prev), pl.BlockSpec((nblk * CH, KVW), cur)],
        out_specs=pl.BlockSpec((nblk * CH, D), cur), out_shape=jax.ShapeDtypeStruct((T, D), BF16),
        sem=("parallel",), comm=comm)


def _attn_bwd(q, k, va, datt, sinks, rc, rs1, rs2, comm=None):
    T = q.shape[0]
    nb = T // CH

    def body(sk_ref, q_ref, kp_ref, kc_ref, vp_ref, vc_ref, do_ref, cq_ref, s1q_ref, s2q_ref, ck_ref, s1k_ref, s2k_ref,
             dq_ref, dk_ref, dv_ref, dsk_ref, kcar, vcar):
        n = pl.program_id(0)

        @pl.when(n == 0)
        def _():
            kcar[...] = jnp.zeros_like(kcar)
            vcar[...] = jnp.zeros_like(vcar)
            dsk_ref[...] = jnp.zeros_like(dsk_ref)

        def flush(kprev, vprev):
            ck, s1k, s2k = ck_ref[...], s1k_ref[...], s2k_ref[...]
            for j in range(KVW // CH):
                sl = slice(CH * j, CH * (j + 1))
                dk_ref[:, sl] = _rope_t(kcar[:, sl] + kprev[:, sl], ck, s1k, s2k).astype(BF16)
                dv_ref[:, sl] = (vcar[:, sl] + vprev[:, sl]).astype(BF16)

        @pl.when(n < nb)
        def _():
            mask = _attn_mask(n)
            kp, kc, vp, vc = kp_ref[...], kc_ref[...], vp_ref[...], vc_ref[...]
            cq, s1q, s2q = cq_ref[...], s1q_ref[...], s2q_ref[...]
            lane = lax.broadcasted_iota(jnp.int32, (1, CH), 1)
            dsk = jnp.zeros((1, CH), F32)
            npairs = D // CH
            kks = [_pair_layout(kp, kc, grp) for grp in range(NKV)]
            vvs = [_pair_layout(vp, vc, grp) for grp in range(NKV)]
            qs = [q_ref[:, CH * p:CH * (p + 1)] for p in range(npairs)]
            dos = [do_ref[:, CH * p:CH * (p + 1)].astype(BF16) for p in range(npairs)]

            def first(p):
                return _dot_nt(qs[p], kks[p // 2]), _dot_nt(dos[p], vvs[p // 2])

            def last(p, ds, pb):
                return (_rope_t(_dot(ds, kks[p // 2]), cq, s1q, s2q) * Q_SCALE, _dot_tn(qs[p], ds), _dot_tn(dos[p], pb))

            ahead = 2
            pending = [first(p) for p in range(ahead)]
            mids, ends = [], []
            for p in range(npairs):
                s, dp = pending.pop(0)
                s = jnp.where(mask, s, -1e30)
                if p + ahead < npairs:
                    pending.append(first(p + ahead))
                ds_parts, p_parts = [], []
                for par in range(2):
                    sl = slice(2 * CH * par, 2 * CH * (par + 1))
                    pr, psink = _softmax_sink(s[:, sl], sk_ref[2 * p + par])
                    delta = jnp.sum(pr * dp[:, sl], axis=-1, keepdims=True)
                    ds_parts.append(pr * (dp[:, sl] - delta))
                    p_parts.append(pr)
                    tot = -jnp.sum(psink * delta, axis=0, keepdims=True)
                    dsk = dsk + jnp.where(lane == 2 * p + par, tot, 0.0)
                mids.append((jnp.concatenate(ds_parts, axis=1).astype(BF16), jnp.concatenate(p_parts, axis=1).astype(BF16)))
                if p >= 1:
                    ends.append(last(p - 1, *mids[p - 1]))
            ends.append(last(npairs - 1, *mids[-1]))
            dq_cols = [e[0] for e in ends]
            def fold(i):
                rows = []
                for grp in range(NKV):
                    acc = ends[2 * grp][i] + ends[2 * grp + 1][i]
                    rows.append(acc[:HD, :2 * CH] + acc[HD:, 2 * CH:])
                return jnp.concatenate(rows, axis=0).T

            dkf, dvf = fold(1), fold(2)
            dq_ref[...] = jnp.concatenate(dq_cols, axis=1).astype(BF16)
            dsk_ref[...] += dsk
            flush(dkf[:CH], dvf[:CH])
            kcar[...] = dkf[CH:]
            vcar[...] = dvf[CH:]

        @pl.when(n == nb)
        def _():
            z = jnp.zeros((CH, KVW), F32)
            flush(z, z)

    last = nb - 1
    cur = lambda n: (jnp.minimum(n, last), 0)
    prev = lambda n: (jnp.clip(n - 1, 0, last), 0)
    sd = jax.ShapeDtypeStruct
    return _call(
        body, (sinks, q, k, k, va, va, datt, rc, rs1, rs2, rc, rs1, rs2), name="attn_bwd", grid=(nb + 1,),
        in_specs=[pl.BlockSpec(memory_space=pltpu.SMEM), pl.BlockSpec((CH, D), cur),
                  pl.BlockSpec((CH, KVW), prev), pl.BlockSpec((CH, KVW), cur),
                  pl.BlockSpec((CH, KVW), prev), pl.BlockSpec((CH, KVW), cur),
                  pl.BlockSpec((CH, D), cur),
                  pl.BlockSpec((CH, CH), cur), pl.BlockSpec((CH, CH), cur), pl.BlockSpec((CH, CH), cur),
                  pl.BlockSpec((CH, CH), prev), pl.BlockSpec((CH, CH), prev), pl.BlockSpec((CH, CH), prev)],
        out_specs=[pl.BlockSpec((CH, D), cur), pl.BlockSpec((CH, KVW), prev), pl.BlockSpec((CH, KVW), prev),
                   _const((1, CH))],
        out_shape=[sd((T, D), BF16), sd((T, KVW), BF16), sd((T, KVW), BF16), sd((1, CH), F32)],
        scratch_shapes=[pltpu.VMEM((CH, KVW), F32), pltpu.VMEM((CH, KVW), F32)], sem=("arbitrary",), comm=comm)


def _merge_fwd(a, att, ga, gb, x, w_a, w_b, w_o, g2, tm, comm=None):
    T = x.shape[0]

    def body(a_ref, att_ref, ga_ref, gb_ref, x_ref, wa_ref, wb_ref, wo_ref, g_ref,
             pa_ref, pb_ref, mg_ref, mix_ref, x1_ref):
        pa = _dot(a_ref[...], wa_ref[...])
        pb = _dot(att_ref[...], wb_ref[...])
        pa_ref[...] = pa.astype(BF16)
        pb_ref[...] = pb.astype(BF16)
        merged = (_sigmoid(ga_ref[...].astype(F32)) * pa + _sigmoid(gb_ref[...].astype(F32)) * pb).astype(BF16)
        mg_ref[...] = merged
        mix = _dot(merged, wo_ref[...])
        mix_ref[...] = mix
        mhat, _ = _rms_hat(mix)
        x1_ref[...] = x_ref[...] + mhat * g_ref[...]

    sd = jax.ShapeDtypeStruct
    return _call(
        body, (a, att, ga, gb, x, w_a, w_b, w_o, g2), name="merge_fwd", grid=(T // tm,),
        in_specs=[_rows(tm, D)] * 5 + [_resident((D, D))] * 3 + [_const((1, D))],
        out_specs=[_rows(tm, D)] * 5,
        out_shape=[sd((T, D), BF16), sd((T, D), BF16), sd((T, D), BF16), sd((T, D), F32), sd((T, D), F32)],
        sem=("parallel",), comm=comm)


def _sgu_merge_fwd(u, vs, lng, lnb, ws, bfull, att, ga, gb, x, w_a, w_b, w_o, g2, tm, comm=None):
    T = x.shape[0]

    def body(u_ref, v_ref, lng_ref, lnb_ref, ws_ref, bf_ref, att_ref, ga_ref, gb_ref, x_ref, wa_ref, wb_ref, wo_ref, g_ref,
             a_ref, pa_ref, pb_ref, mg_ref, mix_ref, x1_ref):
        pb = _dot(att_ref[...], wb_ref[...])
        nc, ug, _, _, _, _, _, _, _, mixed = _sgu_common(
            u_ref[...], v_ref[...], lng_ref[...], lnb_ref[...], ws_ref, bf_ref[...])
        mixed_all = jnp.concatenate(
            [jnp.concatenate([mixed[g][c] for g in range(NG)], axis=1) for c in range(nc)], axis=0)
        a = (ug * mixed_all).astype(BF16)
        a_ref[...] = a
        pa = _dot(a, wa_ref[...])
        pa_ref[...] = pa.astype(BF16)
        pb_ref[...] = pb.astype(BF16)
        merged = (_sigmoid(ga_ref[...].astype(F32)) * pa + _sigmoid(gb_ref[...].astype(F32)) * pb).astype(BF16)
        mg_ref[...] = merged
        mix = _dot(merged, wo_ref[...])
        mix_ref[...] = mix
        mhat, _ = _rms_hat(mix)
        x1_ref[...] = x_ref[...] + mhat * g_ref[...]

    sd = jax.ShapeDtypeStruct
    return _call(
        body, (u, vs, lng, lnb, ws, bfull, att, ga, gb, x, w_a, w_b, w_o, g2), name="sgu_merge_fwd", grid=(T // tm,),
        in_specs=[_rows(tm, D), _rows(tm, D), _const((1, D)), _const((1, D)), _const((NG, CH, CH)), _const((CH, D))]
        + [_rows(tm, D)] * 4 + [_resident((D, D))] * 3 + [_const((1, D))],
        out_specs=[_rows(tm, D)] * 6,
        out_shape=[sd((T, D), BF16)] * 4 + [sd((T, D), F32)] * 2,
        sem=("parallel",), comm=comm, vmem=VMEM_PHYSICAL)


def _merge_bwd(dx1, mix, ga, gb, pa, pb, a, att, merged, w_a, w_b, w_o, g2, tm, comm=None):
    T = dx1.shape[0]
    nsteps = T // tm

    def body(dx1_ref, mix_ref, ga_ref, gb_ref, pa_ref, pb_ref, a_ref, att_ref, mg_ref, wa_ref, wb_ref, wo_ref, g_ref,
             dga_ref, dgb_ref, da_ref, datt_ref, dg_ref, dwa_ref, dwb_ref, dwo_ref, acc, sem):
        i = pl.program_id(0)

        @pl.when(i == 0)
        def _():
            dg_ref[...] = jnp.zeros_like(dg_ref)
            acc[...] = jnp.zeros_like(acc)

        mhat, r = _rms_hat(mix_ref[...])
        dmix, dg = _rms_bwd(mhat, r, g_ref[...], dx1_ref[...])
        dg_ref[...] += dg
        dmix = dmix.astype(BF16)
        dmerged = _dot_nt(dmix, wo_ref[...])
        sa = _sigmoid(ga_ref[...].astype(F32))
        sb = _sigmoid(gb_ref[...].astype(F32))
        dao = (dmerged * sa).astype(BF16)
        dbo = (dmerged * sb).astype(BF16)
        dga_ref[...] = (dmerged * pa_ref[...].astype(F32) * (sa * (1.0 - sa))).astype(BF16)
        dgb_ref[...] = (dmerged * pb_ref[...].astype(F32) * (sb * (1.0 - sb))).astype(BF16)
        da_ref[...] = _dot_nt(dao, wa_ref[...])
        datt_ref[...] = _dot_nt(dbo, wb_ref[...]).astype(BF16)
        acc[0] += _dot_tn(a_ref[...], dao)
        acc[1] += _dot_tn(att_ref[...], dbo)
        acc[2] += _dot_tn(mg_ref[...], dmix)

        @pl.when(i == nsteps - 1)
        def _():
            outs = [pltpu.make_async_copy(acc.at[j], ref, sem.at[j]) for j, ref in enumerate((dwa_ref, dwb_ref, dwo_ref))]
            for cp in outs:
                cp.start()
            for cp in outs:
                cp.wait()

    sd = jax.ShapeDtypeStruct
    return _call(
        body, (dx1, mix, ga, gb, pa, pb, a, att, merged, w_a, w_b, w_o, g2), name="merge_bwd", grid=(nsteps,),
        in_specs=[_rows(tm, D)] * 9 + [_resident((D, D))] * 3 + [_const((1, D))],
        out_specs=[_rows(tm, D)] * 4 + [_const((1, D))] + [ANY] * 3,
        out_shape=[sd((T, D), BF16), sd((T, D), BF16), sd((T, D), F32), sd((T, D), BF16), sd((1, D), F32)]
        + [sd((D, D), F32)] * 3,
        scratch_shapes=[pltpu.VMEM((3, D, D), F32), _dma_sems(3)], sem=("arbitrary",), comm=comm)


def _ffn(x1, target, w1, w2, g3, g4, tm):
    T = x1.shape[0]

    def body(x_ref, t_ref, w1_ref, w2_ref, g3_ref, g4_ref,
             hf_ref, f2_ref, dff_ref, df1_ref, dx_ref, ls_ref, dg3_ref, dg4_ref):
        @pl.when(pl.program_id(0) == 0)
        def _():
            ls_ref[...] = jnp.zeros_like(ls_ref)
            dg3_ref[...] = jnp.zeros_like(dg3_ref)
            dg4_ref[...] = jnp.zeros_like(dg4_ref)

        x = x_ref[...]
        g3, g4 = g3_ref[...], g4_ref[...]
        xhat, r3 = _rms_hat(x)
        hf = (xhat * g3).astype(BF16)
        hf_ref[...] = hf
        rl = jnp.maximum(_dot(hf, w1_ref[...]), 0.0)
        f2 = (rl * rl).astype(BF16)
        f2_ref[...] = f2
        fhat, r4 = _rms_hat(_dot(f2, w2_ref[...]))
        err = x + fhat * g4 - t_ref[...]
        ls_ref[...] += jnp.sum(err * err, axis=0, keepdims=True)
        dy = err * (1.0 / D)
        dff, dg4 = _rms_bwd(fhat, r4, g4, dy)
        dg4_ref[...] += dg4
        dff = dff.astype(BF16)
        dff_ref[...] = dff
        df1 = (_dot_nt(dff, w2_ref[...]) * (2.0 * rl)).astype(BF16)
        df1_ref[...] = df1
        dxn, dg3 = _rms_bwd(xhat, r3, g3, _dot_nt(df1, w1_ref[...]))
        dg3_ref[...] += dg3
        dx_ref[...] = dy + dxn

    sd = jax.ShapeDtypeStruct
    return pl.pallas_call(
        body, name="ffn_fwd_bwd", grid=(T // tm,),
        in_specs=[_rows(tm, D), _rows(tm, D), _resident((D, DFF)), _resident((DFF, D)), _const((1, D)), _const((1, D))],
        out_specs=[_rows(tm, D), _rows(tm, DFF), _rows(tm, D), _rows(tm, DFF), _rows(tm, D), _const((1, D)),
                   _const((1, D)), _const((1, D))],
        out_shape=[sd((T, D), BF16), sd((T, DFF), BF16), sd((T, D), BF16), sd((T, DFF), BF16), sd((T, D), F32),
                   sd((1, D), F32), sd((1, D), F32), sd((1, D), F32)],
        compiler_params=pltpu.CompilerParams(vmem_limit_bytes=VMEM_PHYSICAL, dimension_semantics=("arbitrary",)),
    )(x1, target, w1, w2, g3, g4)


def _inproj_bwd(parts, x, dx1, g1, w_in, tm, comm=None):
    T = x.shape[0]
    widths = [p.shape[1] for p in parts]
    offs = [sum(widths[:i]) for i in range(len(widths) + 1)]
    assert offs[-1] == IN_W

    def body(*refs):
        n = len(parts)
        prefs = refs[:n]
        x_ref, dx1_ref, g_ref, w_ref, dx_ref, dp_ref, dg_ref = refs[n:]

        @pl.when(pl.program_id(0) == 0)
        def _():
            dg_ref[...] = jnp.zeros_like(dg_ref)

        for i in range(n):
            dp_ref[:, offs[i]:offs[i + 1]] = prefs[i][...]
        dh = _dot_nt(dp_ref[...], w_ref[...])
        xhat, r = _rms_hat(x_ref[...])
        dxn, dg = _rms_bwd(xhat, r, g_ref[...], dh)
        dg_ref[...] += dg
        dx_ref[...] = dx1_ref[...] + dxn

    sd = jax.ShapeDtypeStruct
    return _call(
        body, (*parts, x, dx1, g1, w_in), name="inproj_bwd", grid=(T // tm,),
        in_specs=[_rows(tm, w) for w in widths] + [_rows(tm, D), _rows(tm, D), _const((1, D)), _resident((D, IN_W))],
        out_specs=[_rows(tm, D), _rows(tm, IN_W), _const((1, D))],
        out_shape=[sd((T, D), F32), sd((T, IN_W), BF16), sd((1, D), F32)], sem=("arbitrary",), comm=comm)


def _wgrad(a, g, tn, tm, name, comm=None, vmem=None):
    T, K = a.shape
    N = g.shape[1]

    def body(a_ref, g_ref, o_ref):
        @pl.when(pl.program_id(1) == 0)
        def _():
            o_ref[...] = jnp.zeros_like(o_ref)

        o_ref[...] += _dot_tn(a_ref[...], g_ref[...])

    return _call(
        body, (a, g), name=name, grid=(N // tn, T // tm),
        in_specs=[pl.BlockSpec((tm, K), lambda j, t: (t, 0)), pl.BlockSpec((tm, tn), lambda j, t: (t, j))],
        out_specs=pl.BlockSpec((K, tn), lambda j, t: (0, j)),
        out_shape=jax.ShapeDtypeStruct((K, N), F32), sem=("parallel", "arbitrary"), comm=comm, vmem=vmem)


def _adamw(ws, gs, ms, vs, trs, name):
    n = len(ws)
    walk = _Walk(w.shape[0] // tr for w, tr in zip(ws, trs))
    bc1 = 1.0 / (1.0 - B1 ** STEP)
    bc2 = 1.0 / (1.0 - B2 ** STEP)

    def body(*refs):
        i = pl.program_id(0)
        for k in range(n):
            mine = tuple(refs[j * n + k] for j in range(8))

            @pl.when(walk.mine(k, i))
            def _(mine=mine):
                w_ref, g_ref, m_ref, v_ref, go_ref, d_ref, nm_ref, nv_ref = mine
                g = g_ref[...]
                go_ref[...] = g
                m = B1 * m_ref[...] + (1.0 - B1) * g
                v = B2 * v_ref[...] + (1.0 - B2) * (g * g)
                nm_ref[...] = m
                nv_ref[...] = v
                d_ref[...] = -LR * ((m * bc1) / (jnp.sqrt(v * bc2) + AEPS) + WD * w_ref[...])

    def spec(k):
        return pl.BlockSpec((trs[k], ws[k].shape[1]), lambda i: (walk.tile(k, i), 0))

    specs = [spec(k) for k in range(n)]
    res = pl.pallas_call(
        body, name=name, grid=(walk.steps,), in_specs=specs * 4, out_specs=specs * 4,
        out_shape=[jax.ShapeDtypeStruct(w.shape, F32) for w in ws] * 4,
        compiler_params=_cparams(("arbitrary",)),
    )(*ws, *gs, *ms, *vs)
    return [tuple(res[j * n + k] for j in range(4)) for k in range(n)]


BIG = (("col", (D, IN_W)), ("row", (D, D)), ("row", (D, D)), ("row", (D, D)), ("col", (D, DFF)), ("row", (DFF, D)))
NBIG = len(BIG)
ANY = pl.BlockSpec(memory_space=pl.ANY)


def _shard_shape(kind, shape):
    R, C = shape
    return (R, C // 4) if kind == "col" else (R // 4, C)


def _half_shape(kind, shape):
    R, C = shape
    return (R // 2, C) if kind == "col" else (R, C // 2)


def _piece_shape(kind, shape):
    R, C = shape
    return (R // 2, C // 4) if kind == "col" else (R // 4, C // 2)


def _own_region(ref, kind, shape, s):
    R, C = shape
    return ref.at[:, pl.ds(s * (C // 4), C // 4)] if kind == "col" else ref.at[pl.ds(s * (R // 4), R // 4), :]


def _ag_region(ref, kind, shape, s, hc):
    R, C = shape
    if kind == "col":
        return ref.at[pl.ds(hc * (R // 2), R // 2), pl.ds(s * (C // 4), C // 4)]
    return ref.at[pl.ds(s * (R // 4) + hc * (R // 8), R // 8), :]


def _ag_shard_half(ref, kind, shape, hc):
    R, C = shape
    return ref.at[pl.ds(hc * (R // 2), R // 2), :] if kind == "col" else ref.at[pl.ds(hc * (R // 8), R // 8), :]


def _grad_half(ref, kind, shape, hc):
    R, C = shape
    return ref.at[pl.ds(hc * (R // 2), R // 2), :] if kind == "col" else ref.at[:, pl.ds(hc * (C // 2), C // 2)]


def _half_piece(ref, kind, shape, s):
    R, C = shape
    return ref.at[:, pl.ds(s * (C // 4), C // 4)] if kind == "col" else ref.at[pl.ds(s * (R // 4), R // 4), :]


def _place():
    x, y, c = lax.axis_index("x"), lax.axis_index("y"), lax.axis_index("c")
    chips = [(1 - x, y), (x, 1 - y), (1 - x, 1 - y)]
    return x, y, c, chips


def _rcopy(src, dst, ssem, rsem, dev):
    return pltpu.make_async_remote_copy(src_ref=src, dst_ref=dst, send_sem=ssem, recv_sem=rsem,
                                        device_id=dev, device_id_type=MESH)


def _dma_sems(n):
    return pltpu.SemaphoreType.DMA((n,))


def _x_gather_ici(shards, ws):
    n = len(ws)
    specs = [BIG[w] for w in ws]

    def place():
        x, y, c, chips = _place()
        return c, chips, 2 * x + y

    def sends(sh, full, sc):
        c, chips, me_s = place()
        return [_rcopy(_ag_shard_half(sh[i], kind, shape, c), _ag_region(full[i], kind, shape, me_s, c),
                       sc[0].at[3 * i + j], sc[1].at[3 * i + j], (cx, cy, c))
                for i, (kind, shape) in enumerate(specs) for j, (cx, cy) in enumerate(chips)]

    def start(sh, full, sc):
        for i in range(n):
            pltpu.make_async_copy(sh[i], sc[4 + i], sc[2].at[i]).start()
        for cp in sends(sh, full, sc):
            cp.start()

    def finish(sh, full, sc):
        c, chips, me_s = place()
        stores = []
        for i, (kind, shape) in enumerate(specs):
            pltpu.make_async_copy(sh[i], sc[4 + i], sc[2].at[i]).wait()
            st = pltpu.make_async_copy(sc[4 + i], _own_region(full[i], kind, shape, me_s), sc[3].at[i])
            st.start()
            stores.append(st)
        for i, (kind, shape) in enumerate(specs):
            for j, (cx, cy) in enumerate(chips):
                reg = _ag_region(full[i], kind, shape, 2 * cx + cy, c)
                _rcopy(reg, reg, sc[0].at[3 * i + j], sc[1].at[3 * i + j], (cx, cy, c)).wait_recv()
        for cp in sends(sh, full, sc):
            cp.wait_send()
        for st in stores:
            st.wait()

    return _Exchange(
        shards, [jax.ShapeDtypeStruct(shape, BF16) for _, shape in specs], {},
        [_dma_sems(3 * n), _dma_sems(3 * n), _dma_sems(n), _dma_sems(n)]
        + [pltpu.VMEM(_shard_shape(k, s), BF16) for k, s in specs], start, finish)


def _x_gather_d2d(wholes, ws):
    specs = [BIG[w] for w in ws]
    n = len(ws)

    def copies(full, sc, mine):
        x, y, c, chips = _place()
        hc = c if mine else 1 - c
        return [_rcopy(reg, reg, sc[0].at[3 * i + j], sc[1].at[3 * i + j], (x, y, 1 - c))
                for i, (kind, shape) in enumerate(specs) for j, (cx, cy) in enumerate(chips)
                for reg in [_ag_region(full[i], kind, shape, 2 * cx + cy, hc)]]

    def start(_, full, sc):
        for cp in copies(full, sc, True):
            cp.start()

    def finish(_, full, sc):
        for cp in copies(full, sc, False):
            cp.wait_recv()
        for cp in copies(full, sc, True):
            cp.wait_send()

    return _Exchange(wholes, [jax.ShapeDtypeStruct(shape, BF16) for _, shape in specs], {i: i for i in range(n)},
                     [_dma_sems(3 * n), _dma_sems(3 * n)], start, finish)


def _x_grads_sibling(grads, ws):
    specs = [BIG[w] for w in ws]
    n = len(ws)

    def copies(g, got, sc):
        x, y, c, _ = _place()
        return [_rcopy(_grad_half(g[i], kind, shape, 1 - c), got[i], sc[0].at[i], sc[1].at[i], (x, y, 1 - c))
                for i, (kind, shape) in enumerate(specs)]

    def start(g, got, sc):
        for cp in copies(g, got, sc):
            cp.start()

    def finish(g, got, sc):
        for cp in copies(g, got, sc):
            cp.wait_recv()
        for cp in copies(g, got, sc):
            cp.wait_send()

    return _Exchange(grads, [jax.ShapeDtypeStruct(_half_shape(k, s), F32) for k, s in specs], {},
                     [_dma_sems(n), _dma_sems(n)], start, finish)


def _x_grads_chips(sums_bf, ws):
    specs = [BIG[w] for w in ws]
    n = len(ws)

    def copies(s16, got, sc):
        x, y, c, chips = _place()
        return [_rcopy(_half_piece(s16[i], kind, shape, 2 * cx + cy), got[i].at[j],
                       sc[0].at[3 * i + j], sc[1].at[3 * i + j], (cx, cy, c))
                for i, (kind, shape) in enumerate(specs) for j, (cx, cy) in enumerate(chips)]

    def start(s16, got, sc):
        for cp in copies(s16, got, sc):
            cp.start()

    def finish(s16, got, sc):
        for cp in copies(s16, got, sc):
            cp.wait_recv()
        for cp in copies(s16, got, sc):
            cp.wait_send()

    return _Exchange(sums_bf, [jax.ShapeDtypeStruct((3,) + _piece_shape(k, s), BF16) for k, s in specs], {},
                     [_dma_sems(3 * n), _dma_sems(3 * n)], start, finish)


def _shard_half(ref, kind, shape, hc):
    sr, sc = _shard_shape(kind, shape)
    return ref.at[pl.ds(hc * (sr // 2), sr // 2), :] if kind == "col" else ref.at[:, pl.ds(hc * (sc // 2), sc // 2)]


def _x_grads_share(shard_grads, ws):
    specs = [BIG[w] for w in ws]
    n = len(ws)

    def copies(g, sc, mine):
        x, y, c, _ = _place()
        hc = c if mine else 1 - c
        return [_rcopy(part, part, sc[0].at[i], sc[1].at[i], (x, y, 1 - c))
                for i, (kind, shape) in enumerate(specs) for part in [_shard_half(g[i], kind, shape, hc)]]

    def start(_, g, sc):
        for cp in copies(g, sc, True):
            cp.start()

    def finish(_, g, sc):
        for cp in copies(g, sc, False):
            cp.wait_recv()
        for cp in copies(g, sc, True):
            cp.wait_send()

    return _Exchange(shard_grads, [jax.ShapeDtypeStruct(_shard_shape(k, s), F32) for k, s in specs],
                     {i: i for i in range(n)}, [_dma_sems(n), _dma_sems(n)], start, finish)


ADD_BLOCK_BYTES = 4 * 1024 * 1024


def _add_rows(rows, cols, n_arrays):
    limit = ADD_BLOCK_BYTES // (1 if n_arrays == 1 else 4)
    r = rows
    while r > 64 and r * cols * 4 > limit:
        r //= 2
    return r


class _Walk:
    def __init__(self, tiles):
        self.tiles = list(tiles)
        self.starts = [sum(self.tiles[:k]) for k in range(len(self.tiles))]
        self.steps = sum(self.tiles)

    def tile(self, k, i):
        return jnp.clip(i - self.starts[k], 0, self.tiles[k] - 1)

    def mine(self, k, i):
        return (i >= self.starts[k]) & (i < self.starts[k] + self.tiles[k])


def _add_halves(place, gs, gots, kinds, name):
    n = len(gs)
    halves = [_half_shape(kind, g.shape) for g, kind in zip(gs, kinds)]
    rows = [_add_rows(hr, hc, n) for hr, hc in halves]
    walk = _Walk(hr // r for (hr, _), r in zip(halves, rows))

    def body(p_ref, *refs):
        i = pl.program_id(0)
        for k in range(n):
            g_ref, b_ref, s_ref, sb_ref = (refs[j * n + k] for j in range(4))

            @pl.when(walk.mine(k, i))
            def _(g_ref=g_ref, b_ref=b_ref, s_ref=s_ref, sb_ref=sb_ref):
                s = g_ref[...] + b_ref[...]
                s_ref[...] = s
                sb_ref[...] = s.astype(BF16)

    def g_spec(k):
        if kinds[k] == "col":
            return pl.BlockSpec((rows[k], gs[k].shape[1]), lambda i, p: (p[0] * walk.tiles[k] + walk.tile(k, i), 0))
        return pl.BlockSpec((rows[k], halves[k][1]), lambda i, p: (walk.tile(k, i), p[0]))

    def spec(k):
        return pl.BlockSpec((rows[k], halves[k][1]), lambda i, p: (walk.tile(k, i), 0))

    specs = [spec(k) for k in range(n)]
    res = pl.pallas_call(
        body, name=name,
        grid_spec=pltpu.PrefetchScalarGridSpec(num_scalar_prefetch=1, grid=(walk.steps,),
                                               in_specs=[g_spec(k) for k in range(n)] + specs, out_specs=specs + specs),
        out_shape=[jax.ShapeDtypeStruct(h, F32) for h in halves] + [jax.ShapeDtypeStruct(h, BF16) for h in halves],
        compiler_params=_cparams(("arbitrary",)),
    )(place, *gs, *gots)
    return [(res[k], res[n + k]) for k in range(n)]


def _add_pieces(place, halves, gots, specs_big, name):
    n = len(halves)
    pieces = [_piece_shape(kind, shape) for kind, shape in specs_big]
    rows = [_add_rows(pr, pc, n) for pr, pc in pieces]
    walk = _Walk(pr // r for (pr, _), r in zip(pieces, rows))

    def body(p_ref, *refs):
        i = pl.program_id(0)
        for k in range(n):
            m_ref, g_ref, o_ref = (refs[j * n + k] for j in range(3))

            @pl.when(walk.mine(k, i))
            def _(m_ref=m_ref, g_ref=g_ref, o_ref=o_ref):
                acc = m_ref[...]
                for j in range(3):
                    acc = acc + g_ref[j].astype(F32)
                o_ref[...] = acc

    def m_spec(k):
        if specs_big[k][0] == "col":
            return pl.BlockSpec((rows[k], pieces[k][1]), lambda i, p: (walk.tile(k, i), p[1]))
        return pl.BlockSpec((rows[k], pieces[k][1]), lambda i, p: (p[1] * walk.tiles[k] + walk.tile(k, i), 0))

    def got_spec(k):
        return pl.BlockSpec((3, rows[k], pieces[k][1]), lambda i, p: (0, walk.tile(k, i), 0))

    def o_spec(k):
        if specs_big[k][0] == "col":
            return pl.BlockSpec((rows[k], pieces[k][1]), lambda i, p: (p[0] * walk.tiles[k] + walk.tile(k, i), 0))
        return pl.BlockSpec((rows[k], pieces[k][1]), lambda i, p: (walk.tile(k, i), p[0]))

    return pl.pallas_call(
        body, name=name,
        grid_spec=pltpu.PrefetchScalarGridSpec(
            num_scalar_prefetch=1, grid=(walk.steps,),
            in_specs=[m_spec(k) for k in range(n)] + [got_spec(k) for k in range(n)],
            out_specs=[o_spec(k) for k in range(n)]),
        out_shape=[jax.ShapeDtypeStruct(_shard_shape(kind, shape), F32) for kind, shape in specs_big],
        compiler_params=_cparams(("arbitrary",)),
    )(place, *halves, *gots)


SMALL_ROWS = 1024 + 8 * 8 + 8


def _x_small_all_reduce(p):
    def parts(p_ref, sc):
        slots, ssem, rsem = sc[0], sc[2], sc[3]
        x, y, c = lax.axis_index("x"), lax.axis_index("y"), lax.axis_index("c")
        me = 4 * x + 2 * y + c
        out = []
        for r in range(1, 8):
            bx, by, bc = (r >> 2) & 1, (r >> 1) & 1, r & 1
            tgt = (1 - x if bx else x, 1 - y if by else y, 1 - c if bc else c)
            send = _rcopy(p_ref, slots.at[me], ssem.at[r - 1], rsem.at[r - 1], tgt)
            src = 4 * tgt[0] + 2 * tgt[1] + tgt[2]
            recv = _rcopy(p_ref, slots.at[src], ssem.at[r - 1], rsem.at[r - 1], tgt)
            out.append((send, recv))
        return me, out

    def start(ins, outs, sc):
        me, cps = parts(ins[0], sc)
        pltpu.make_async_copy(ins[0], sc[0].at[me], sc[4].at[0]).start()
        for send, _ in cps:
            send.start()

    def finish(ins, outs, sc):
        me, cps = parts(ins[0], sc)
        pltpu.make_async_copy(ins[0], sc[0].at[me], sc[4].at[0]).wait()
        for _, recv in cps:
            recv.wait_recv()
        acc = sc[0][0]
        for d in range(1, 8):
            acc = acc + sc[0][d]
        sc[1][...] = acc
        back = pltpu.make_async_copy(sc[1], outs[0], sc[4].at[1])
        back.start()
        for send, _ in cps:
            send.wait_send()
        back.wait()

    return _Exchange([p], [jax.ShapeDtypeStruct((SMALL_ROWS, CH), F32)], {},
                     [pltpu.VMEM((8, SMALL_ROWS, CH), F32), pltpu.VMEM((SMALL_ROWS, CH), F32), _dma_sems(7), _dma_sems(7),
                      _dma_sems(2)], start, finish)


def _rope_tables(positions, comm=None):
    T = positions.shape[0]
    inv_freq = 500000.0 ** (-jnp.arange(0, 2 * ROPE_HALF, 2, dtype=F32) / (2 * ROPE_HALF))
    head = jnp.concatenate([inv_freq, inv_freq, jnp.zeros((HD - 2 * ROPE_HALF,), F32)])
    lane_freq = jnp.concatenate([head, head])[None, :]
    pos = jnp.broadcast_to(positions.astype(F32)[:, None], (T, CH))
    tm = min(1024, T)

    def body(p_ref, f_ref, c_ref, s1_ref, s2_ref):
        ang = p_ref[...] * f_ref[...]
        sin = jnp.sin(ang)
        first = (lax.broadcasted_iota(jnp.int32, ang.shape, 1) % HD) < ROPE_HALF
        c_ref[...] = jnp.cos(ang)
        s1_ref[...] = jnp.where(first, -sin, 0.0)
        s2_ref[...] = jnp.where(first, 0.0, sin)

    return _call(body, (pos, lane_freq), name="rope_tables", grid=(T // tm,),
                 in_specs=[_rows(tm, CH), _const((1, CH))], out_specs=[_rows(tm, CH)] * 3,
                 out_shape=[jax.ShapeDtypeStruct((T, CH), F32)] * 3, sem=("parallel",), comm=comm)


BIG_NAMES = ("w_in", "w_a", "w_b", "w_o", "w_ff_in", "w_ff_out")
SMALL_NAMES = ("w_spatial", "ln_v_gain", "ln_v_bias", "b_spatial", "sinks", "norm_mix_pre", "norm_mix_post",
               "norm_ff_pre", "norm_ff_post")
WEIGHT_ORDER = ("w_in", "ln_v_gain", "ln_v_bias", "w_spatial", "b_spatial", "sinks", "w_a", "w_b", "w_o",
                "norm_mix_pre", "norm_mix_post", "w_ff_in", "w_ff_out", "norm_ff_pre", "norm_ff_post")


def _pack_small(d, loss_sums=None):
    parts = []
    for n in SMALL_NAMES:
        flat = d[n].reshape(-1)
        pad = (-flat.shape[0]) % (8 * CH)
        parts.append(jnp.pad(flat, (0, pad)).reshape(-1, CH))
    parts.append(jnp.zeros((8, CH), F32) if loss_sums is None else loss_sums.reshape(8, CH))
    return jnp.concatenate(parts, axis=0)


def _unpack_small(p, like):
    out, row = {}, 0
    for n in SMALL_NAMES:
        size = like[n].size
        rows = -(-size // (8 * CH)) * 8
        out[n] = p[row:row + rows].reshape(-1)[:size].reshape(like[n].shape)
        row += rows
    return out


def kernel(x, positions, w_in, ln_v_gain, ln_v_bias, w_spatial, b_spatial, sinks, w_a, w_b, w_o, norm_mix_pre, norm_mix_post, w_ff_in, w_ff_out, norm_ff_pre, norm_ff_post, loss_target, m_w_in, m_ln_v_gain, m_ln_v_bias, m_w_spatial, m_b_spatial, m_sinks, m_w_a, m_w_b, m_w_o, m_norm_mix_pre, m_norm_mix_post, m_w_ff_in, m_w_ff_out, m_norm_ff_pre, m_norm_ff_post, v_w_in, v_ln_v_gain, v_ln_v_bias, v_w_spatial, v_b_spatial, v_sinks, v_w_a, v_w_b, v_w_o, v_norm_mix_pre, v_norm_mix_post, v_w_ff_in, v_w_ff_out, v_norm_ff_pre, v_norm_ff_post):
    w = dict(w_in=w_in, ln_v_gain=ln_v_gain, ln_v_bias=ln_v_bias, w_spatial=w_spatial, b_spatial=b_spatial, sinks=sinks,
             w_a=w_a, w_b=w_b, w_o=w_o, norm_mix_pre=norm_mix_pre, norm_mix_post=norm_mix_post, w_ff_in=w_ff_in,
             w_ff_out=w_ff_out, norm_ff_pre=norm_ff_pre, norm_ff_post=norm_ff_post)
    m = dict(w_in=m_w_in, ln_v_gain=m_ln_v_gain, ln_v_bias=m_ln_v_bias, w_spatial=m_w_spatial, b_spatial=m_b_spatial,
             sinks=m_sinks, w_a=m_w_a, w_b=m_w_b, w_o=m_w_o, norm_mix_pre=m_norm_mix_pre, norm_mix_post=m_norm_mix_post,
             w_ff_in=m_w_ff_in, w_ff_out=m_w_ff_out, norm_ff_pre=m_norm_ff_pre, norm_ff_post=m_norm_ff_post)
    v = dict(w_in=v_w_in, ln_v_gain=v_ln_v_gain, ln_v_bias=v_ln_v_bias, w_spatial=v_w_spatial, b_spatial=v_b_spatial,
             sinks=v_sinks, w_a=v_w_a, w_b=v_w_b, w_o=v_w_o, norm_mix_pre=v_norm_mix_pre, norm_mix_post=v_norm_mix_post,
             w_ff_in=v_w_ff_in, w_ff_out=v_w_ff_out, norm_ff_pre=v_norm_ff_pre, norm_ff_post=v_norm_ff_post)

    FIRST, REST = (0,), tuple(range(1, NBIG))
    shards = [w[n][0].astype(BF16) for n in BIG_NAMES]
    place = jnp.stack([lax.axis_index("c"), 2 * lax.axis_index("x") + lax.axis_index("y")]).astype(jnp.int32)
    xs, target = x[0], loss_target[0]
    T = xs.shape[0]
    tile = min(TOKEN_TILE, T)
    wtiles = {n: dict(tm=min(tm, T), tn=tn) for n, (tm, tn) in WGRAD_TILES.items()}
    g1, g2, g3, g4 = norm_mix_pre, norm_mix_post, norm_ff_pre, norm_ff_post
    w_sp, snk = w_spatial[0], sinks[0]
    MIX, FF = (1, 2, 3), (4, 5)
    bfull = jnp.repeat(b_spatial[0].T, CH, axis=1)

    def reduce_tail(ws, grads, got):
        tag = "_".join(BIG_NAMES[k] for k in ws)
        sums = _add_halves(place, grads, got, [BIG[k][0] for k in ws], name="grad_add_sibling_" + tag)
        return sums, _x_grads_chips([s[1] for s in sums], ws)

    def reduce_end(ws, sums, pieces):
        tag = "_".join(BIG_NAMES[k] for k in ws)
        return _add_pieces(place, [s[0] for s in sums], pieces, [BIG[k] for k in ws], name="grad_add_chips_" + tag)

    (rc, rs1, rs2), w_in_part = _rope_tables(positions[0], comm=_x_gather_ici(shards[:1], FIRST))
    w_in_b = _run(_x_gather_d2d(w_in_part, FIRST), "gather_w_in_d2d")[0]
    (h, u, vs, q, k, va, ga, gb), mix_part = _inproj(xs, g1, w_in_b, rc, rs1, rs2, tm=tile,
                                                    comm=_x_gather_ici(shards[1:4], MIX))
    att, (w_a_b, w_b_b, w_o_b, *ff_part) = _attn_fwd(
        q, k, va, snk, comm=_both(_x_gather_d2d(mix_part, MIX), _x_gather_ici(shards[4:], FF)))
    (a, pa, pb, merged, mix, x1), (w_ff_in_b, w_ff_out_b) = _sgu_merge_fwd(
        u, vs, ln_v_gain, ln_v_bias, w_sp, bfull, att, ga, gb, xs, w_a_b, w_b_b, w_o_b, g2, tm=tile,
        comm=_x_gather_d2d(ff_part, FF))
    hf, f2, dff, df1, dx1, lsum, dg3, dg4 = _ffn(x1, target, w_ff_in_b, w_ff_out_b, g3, g4, tm=tile)

    dw_ff_out, _ = _wgrad(f2, dff, name="wgrad_ff_out", **wtiles["w_ff_out"])
    dw_ff_in, _ = _wgrad(hf, df1, name="wgrad_ff_in", **wtiles["w_ff_in"])
    (dga, dgb, da, datt, dg2, dw_a, dw_b, dw_o), _ = _merge_bwd(
        dx1, mix, ga, gb, pa, pb, a, att, merged, w_a_b, w_b_b, w_o_b, g2, tm=tile)
    grads_rest = [dw_a, dw_b, dw_o, dw_ff_in, dw_ff_out]
    (du, dvs, dws, dbs, dlg, dlb), got_rest = _sgu_bwd(
        u, vs, da, ln_v_gain, ln_v_bias, w_sp, bfull, tm=tile, comm=_x_grads_sibling(grads_rest, REST))
    sums_rest, to_chips = reduce_tail(REST, grads_rest, got_rest)
    (dq, dk, dva, dsk), pieces_rest = _attn_bwd(q, k, va, datt, snk, rc, rs1, rs2, comm=to_chips)
    partial_rest = reduce_end(REST, sums_rest, pieces_rest)
    (dx, dproj, dg1), _ = _inproj_bwd([du, dvs, dq, dk, dva, dga, dgb], xs, dx1, g1, w_in_b, tm=tile)
    small = dict(ln_v_gain=dlg, ln_v_bias=dlb, w_spatial=dws, b_spatial=dbs, sinks=dsk[:, :NQ],
                 norm_mix_pre=dg1, norm_mix_post=dg2, norm_ff_pre=dg3, norm_ff_post=dg4)
    dw_in, (gs, *shard_rest) = _wgrad(
        h, dproj, name="wgrad_in", vmem=VMEM_PHYSICAL, **wtiles["w_in"],
        comm=_both(_x_small_all_reduce(_pack_small(small, lsum)), _x_grads_share(partial_rest, REST)))
    got_in = _run(_x_grads_sibling([dw_in], FIRST), "grads_in_to_sibling")
    sums_in, to_chips = reduce_tail(FIRST, [dw_in], got_in)
    partial_in = reduce_end(FIRST, sums_in, _run(to_chips, "grads_in_to_chips"))
    g_in = _run(_x_grads_share(partial_in, FIRST), "grads_in_share")[0]

    loss = 0.5 * jnp.sum(gs[SMALL_ROWS - 8:]) / D
    grad, delta, new_m, new_v = {}, {}, {}, {}
    for n, g in zip(BIG_NAMES, [g_in] + list(shard_rest)):
        (g_, d_, m_, v_), = _adamw([w[n][0]], [g], [m[n][0]], [v[n][0]], [256], name="adamw_" + n)
        grad[n], delta[n], new_m[n], new_v[n] = g_[None], d_[None], m_[None], v_[None]
    (gs, ds, ms, vs), = _adamw([_pack_small(w)], [gs], [_pack_small(m)], [_pack_small(v)], [SMALL_ROWS], name="adamw_small")
    for packed, dst in ((gs, grad), (ds, delta), (ms, new_m), (vs, new_v)):
        dst.update(_unpack_small(packed, w))

    outs = [loss, dx[None]]
    for group in (grad, delta, new_m, new_v):
        outs.extend(group[n] for n in WEIGHT_ORDER)
    return tuple(outs)
```

```python
import functools

import jax
import jax.numpy as jnp
from jax import lax
from jax.experimental import pallas as pl
from jax.experimental.pallas import tpu as pltpu

F32 = jnp.float32
BF16 = jnp.bfloat16

D = 1024
CH = 128
NG = 8
HD = 64
NQ = 16
NKV = 4
KVW = NKV * HD
DFF = 4 * D
EPS = 1e-6
IN_W = 5632
SEG = (0, 1024, 2048, 3072, 3328, 3584, 4608, 5632)
ROPE_HALF = 8
Q_SCALE = HD ** -0.5

LR, B1, B2, AEPS, WD, STEP = 0.001, 0.9, 0.999, 1e-08, 0.01, 10

VMEM_PHYSICAL = 64 * 1024 * 1024
VMEM_LIMIT = 60 * 1024 * 1024
MESH = pl.DeviceIdType.MESH

TOKEN_TILE = 512
WGRAD_TILES = {"w_ff_out": (512, 1024), "w_ff_in": (2048, 2048), "w_in": (2048, IN_W // 2)}

_GELU_C0 = 0.7978845608028654
_GELU_C1 = 0.044715


def _cparams(sem=None, vmem=None):
    kw = dict(vmem_limit_bytes=VMEM_LIMIT if vmem is None else vmem)
    if sem is not None:
        kw["dimension_semantics"] = sem
    return pltpu.CompilerParams(**kw)


def _resident(shape):
    nd = len(shape)
    return pl.BlockSpec(shape, lambda *_: (0,) * nd, pipeline_mode=pl.Buffered(1))


def _const(shape):
    nd = len(shape)
    return pl.BlockSpec(shape, lambda *_: (0,) * nd)


def _rows(tm, w):
    return pl.BlockSpec((tm, w), lambda i: (i, 0))


class _Exchange:
    def __init__(self, ins, outs, aliases, scratch, start, finish):
        self.ins, self.outs, self.aliases, self.scratch = list(ins), list(outs), dict(aliases), list(scratch)
        self.start, self.finish = start, finish


def _both(a, b):
    na, ma, sa = len(a.ins), len(a.outs), len(a.scratch)

    def start(ci, co, cs):
        a.start(ci[:na], co[:ma], cs[:sa])
        b.start(ci[na:], co[ma:], cs[sa:])

    def finish(ci, co, cs):
        a.finish(ci[:na], co[:ma], cs[:sa])
        b.finish(ci[na:], co[ma:], cs[sa:])

    aliases = {**a.aliases, **{na + i: ma + j for i, j in b.aliases.items()}}
    return _Exchange(a.ins + b.ins, a.outs + b.outs, aliases, a.scratch + b.scratch, start, finish)


def _call(body, args, *, name, grid, in_specs, out_specs, out_shape, scratch_shapes=(), sem=None, comm=None, vmem=None):
    single = not isinstance(out_shape, (list, tuple))
    out_shape = [out_shape] if single else list(out_shape)
    out_specs = [out_specs] if single else list(out_specs)
    if comm is None:
        res = pl.pallas_call(body, name=name, grid=grid, in_specs=list(in_specs), out_specs=out_specs,
                             out_shape=out_shape, scratch_shapes=list(scratch_shapes),
                             compiler_params=_cparams(sem, vmem))(*args)
        return (res[0] if single else res), []
    n_in, n_out, n_scr = len(args), len(out_shape), len(scratch_shapes)
    nci, nco = len(comm.ins), len(comm.outs)
    steps = 1
    for g in grid:
        steps *= g

    def hosted(*refs):
        a, ci = refs[:n_in], refs[n_in:n_in + nci]
        o, co = refs[n_in + nci:n_in + nci + n_out], refs[n_in + nci + n_out:n_in + nci + n_out + nco]
        rest = refs[n_in + nci + n_out + nco:]
        scr, cs = rest[:n_scr], rest[n_scr:]
        step = pl.program_id(0)
        for d in range(1, len(grid)):
            step = step * grid[d] + pl.program_id(d)

        @pl.when(step == 0)
        def _():
            comm.start(ci, co, cs)

        body(*a, *o, *scr)

        @pl.when(step == steps - 1)
        def _():
            comm.finish(ci, co, cs)

    res = pl.pallas_call(
        hosted, name=name, grid=grid, in_specs=list(in_specs) + [ANY] * nci, out_specs=out_specs + [ANY] * nco,
        out_shape=out_shape + comm.outs, scratch_shapes=list(scratch_shapes) + comm.scratch,
        input_output_aliases={n_in + i: n_out + j for i, j in comm.aliases.items()},
        compiler_params=_cparams(("arbitrary",) * len(grid), vmem),
    )(*args, *comm.ins)
    own = res[:n_out]
    return (own[0] if single else own), list(res[n_out:])


def _run(comm, name):
    nci = len(comm.ins)

    def body(*refs):
        ci, co, cs = refs[:nci], refs[nci:nci + len(comm.outs)], refs[nci + len(comm.outs):]
        comm.start(ci, co, cs)
        comm.finish(ci, co, cs)

    return pl.pallas_call(
        body, name=name, in_specs=[ANY] * nci, out_specs=[ANY] * len(comm.outs), out_shape=comm.outs,
        scratch_shapes=comm.scratch, input_output_aliases=comm.aliases,
        compiler_params=pltpu.CompilerParams(vmem_limit_bytes=VMEM_LIMIT),
    )(*comm.ins)


def _gelu(x):
    x2 = x * x
    t = jnp.tanh(x * (_GELU_C0 + (_GELU_C0 * _GELU_C1) * x2))
    hx = 0.5 * x
    return hx + hx * t, (t, x2, hx)


def _gelu_grad(parts):
    t, x2, hx = parts
    return (0.5 + 0.5 * t) + hx * (1.0 - t * t) * (_GELU_C0 + (3.0 * _GELU_C0 * _GELU_C1) * x2)


def _sigmoid(x):
    return 1.0 / (1.0 + jnp.exp(-x))


def _rms_hat(x):
    r = lax.rsqrt(jnp.mean(x * x, axis=-1, keepdims=True) + EPS)
    return x * r, r


def _rms_bwd(xhat, r, g, dout):
    dg = jnp.sum(dout * xhat, axis=0, keepdims=True)
    dy = dout * g
    dx = r * (dy - xhat * jnp.mean(dy * xhat, axis=-1, keepdims=True))
    return dx, dg


def _dot(a, b):
    return jnp.dot(a, b, preferred_element_type=F32)


def _dot_nt(a, b):
    return lax.dot_general(a, b, (((1,), (1,)), ((), ())), preferred_element_type=F32)


def _dot_tn(a, b):
    return lax.dot_general(a, b, (((0,), (0,)), ((), ())), preferred_element_type=F32)


def _rope(blk, c, s1, s2):
    return blk * c + pltpu.roll(blk, CH - ROPE_HALF, 1) * s1 + pltpu.roll(blk, ROPE_HALF, 1) * s2


def _rope_t(blk, c, s1, s2):
    return blk * c + pltpu.roll(blk * s1, ROPE_HALF, 1) + pltpu.roll(blk * s2, CH - ROPE_HALF, 1)


def _inproj(x, g1, w_in, rc, rs1, rs2, tm, comm=None):
    T = x.shape[0]

    def body(x_ref, g_ref, w_ref, c_ref, s1_ref, s2_ref,
             h_ref, u_ref, v_ref, q_ref, k_ref, va_ref, ga_ref, gb_ref):
        xhat, _ = _rms_hat(x_ref[...])
        h = (xhat * g_ref[...]).astype(BF16)
        h_ref[...] = h
        uv = _dot(h, w_ref[:, SEG[0]:SEG[2]])
        u_ref[...] = uv[:, :D]
        v_ref[...] = uv[:, D:]
        c, s1, s2 = c_ref[...], s1_ref[...], s2_ref[...]
        qkv = _dot(h, w_ref[:, SEG[2]:SEG[5]])
        for p in range(D // CH):
            blk = _rope(qkv[:, CH * p:CH * (p + 1)], c, s1, s2) * Q_SCALE
            q_ref[:, CH * p:CH * (p + 1)] = blk.astype(BF16)
        for p in range(KVW // CH):
            k_ref[:, CH * p:CH * (p + 1)] = _rope(qkv[:, D + CH * p:D + CH * (p + 1)], c, s1, s2).astype(BF16)
        va_ref[...] = qkv[:, D + KVW:].astype(BF16)
        gates = _dot(h, w_ref[:, SEG[5]:SEG[7]]).astype(BF16)
        ga_ref[...] = gates[:, :D]
        gb_ref[...] = gates[:, D:]

    sd = jax.ShapeDtypeStruct
    return _call(
        body, (x, g1, w_in, rc, rs1, rs2), name="inproj_fwd", grid=(T // tm,),
        in_specs=[_rows(tm, D), _const((1, D)), _resident((D, IN_W)), _rows(tm, CH), _rows(tm, CH), _rows(tm, CH)],
        out_specs=[_rows(tm, D), _rows(tm, D), _rows(tm, D), _rows(tm, D), _rows(tm, KVW), _rows(tm, KVW),
                   _rows(tm, D), _rows(tm, D)],
        out_shape=[sd((T, D), BF16), sd((T, D), F32), sd((T, D), F32), sd((T, D), BF16), sd((T, KVW), BF16),
                   sd((T, KVW), BF16), sd((T, D), BF16), sd((T, D), BF16)],
        sem=("parallel",), comm=comm)


def _sgu_common(u, vs, lng, lnb, ws_ref, bfull):
    nc = u.shape[0] // CH
    ug, tu = _gelu(u)
    vg, tv = _gelu(vs)
    mu = jnp.mean(vg, axis=-1, keepdims=True)
    xc = vg - mu
    rstd = lax.rsqrt(jnp.mean(xc * xc, axis=-1, keepdims=True) + EPS)
    vhat = xc * rstd
    vnb = (vhat * lng + lnb).astype(BF16)
    tri = lax.broadcasted_iota(jnp.int32, (CH, CH), 0) >= lax.broadcasted_iota(jnp.int32, (CH, CH), 1)
    wts, rhss, mixed = [], [], []
    for g in range(NG):
        wt = jnp.where(tri, ws_ref[g], 0.0).astype(BF16)
        rhs = jnp.concatenate([vnb[CH * c:CH * (c + 1), CH * g:CH * (g + 1)] for c in range(nc)], axis=1)
        mix = _dot(wt, rhs)
        wts.append(wt)
        rhss.append(rhs)
        mixed.append([mix[:, CH * c:CH * (c + 1)] + bfull[:, CH * g:CH * (g + 1)] for c in range(nc)])
    return nc, ug, tu, tv, rstd, vhat, tri, wts, rhss, mixed


def _sgu_fwd(u, vs, lng, lnb, ws, bfull, tm, comm=None):
    T = u.shape[0]

    def body(u_ref, v_ref, lng_ref, lnb_ref, ws_ref, bf_ref, a_ref):
        nc, ug, _, _, _, _, _, _, _, mixed = _sgu_common(
            u_ref[...], v_ref[...], lng_ref[...], lnb_ref[...], ws_ref, bf_ref[...])
        mixed_all = jnp.concatenate(
            [jnp.concatenate([mixed[g][c] for g in range(NG)], axis=1) for c in range(nc)], axis=0)
        a_ref[...] = (ug * mixed_all).astype(BF16)

    return _call(
        body, (u, vs, lng, lnb, ws, bfull), name="sgu_fwd", grid=(T // tm,),
        in_specs=[_rows(tm, D), _rows(tm, D), _const((1, D)), _const((1, D)), _const((NG, CH, CH)), _const((CH, D))],
        out_specs=_rows(tm, D), out_shape=jax.ShapeDtypeStruct((T, D), BF16), sem=("parallel",), comm=comm)


def _sgu_bwd(u, vs, da, lng, lnb, ws, bfull, tm, comm=None):
    T = u.shape[0]
    nsteps = T // tm

    def body(u_ref, v_ref, da_ref, lng_ref, lnb_ref, ws_ref, bf_ref,
             du_ref, dv_ref, dws_ref, dbs_ref, dlg_ref, dlb_ref, db_ref):
        i = pl.program_id(0)
        u, vs, da, lng = u_ref[...], v_ref[...], da_ref[...], lng_ref[...]
        nc, ug, tu, tv, rstd, vhat, tri, wts, rhss, mixed = _sgu_common(u, vs, lng, lnb_ref[...], ws_ref, bf_ref[...])

        @pl.when(i == 0)
        def _():
            dws_ref[...] = jnp.zeros_like(dws_ref)
            db_ref[...] = jnp.zeros_like(db_ref)
            dlg_ref[...] = jnp.zeros_like(dlg_ref)
            dlb_ref[...] = jnp.zeros_like(dlb_ref)

        mixed_all = jnp.concatenate(
            [jnp.concatenate([mixed[g][c] for g in range(NG)], axis=1) for c in range(nc)], axis=0)
        du_ref[...] = (da * mixed_all * _gelu_grad(tu)).astype(BF16)
        dmixed = da * ug
        dvn_cols = []
        for g in range(NG):
            dmix = [dmixed[CH * c:CH * (c + 1), CH * g:CH * (g + 1)] for c in range(nc)]
            db_ref[:, CH * g:CH * (g + 1)] += functools.reduce(lambda a, b: a + b, dmix)
            dm = jnp.concatenate(dmix, axis=1).astype(BF16)
            dws_ref[g] += _dot_nt(dm, rhss[g])
            dvn_cols.append(_dot_tn(wts[g], dm))
        dvn = jnp.concatenate(
            [jnp.concatenate([dvn_cols[g][:, CH * c:CH * (c + 1)] for g in range(NG)], axis=1) for c in range(nc)],
            axis=0)
        dlg_ref[...] += jnp.sum(dvn * vhat, axis=0, keepdims=True)
        dlb_ref[...] += jnp.sum(dvn, axis=0, keepdims=True)
        dvh = dvn * lng
        dvg = rstd * (dvh - jnp.mean(dvh, axis=-1, keepdims=True)
                      - vhat * jnp.mean(dvh * vhat, axis=-1, keepdims=True))
        dv_ref[...] = (dvg * _gelu_grad(tv)).astype(BF16)

        @pl.when(i == nsteps - 1)
        def _():
            for g in range(NG):
                dws_ref[g] = jnp.where(tri, dws_ref[g], 0.0)
                dbs_ref[g:g + 1, :] = jnp.sum(db_ref[:, CH * g:CH * (g + 1)].T, axis=0, keepdims=True)

    sd = jax.ShapeDtypeStruct
    return _call(
        body, (u, vs, da, lng, lnb, ws, bfull), name="sgu_bwd", grid=(nsteps,),
        in_specs=[_rows(tm, D), _rows(tm, D), _rows(tm, D), _const((1, D)), _const((1, D)), _const((NG, CH, CH)),
                  _const((CH, D))],
        out_specs=[_rows(tm, D), _rows(tm, D), _const((NG, CH, CH)), _const((NG, CH)), _const((1, D)), _const((1, D))],
        out_shape=[sd((T, D), BF16), sd((T, D), BF16), sd((NG, CH, CH), F32), sd((NG, CH), F32), sd((1, D), F32),
                   sd((1, D), F32)],
        scratch_shapes=[pltpu.VMEM((CH, D), F32)], sem=("arbitrary",), comm=comm)


def _pair_layout(prev, cur, grp):
    j, half = grp // 2, grp % 2
    blk = jnp.concatenate([prev[:, CH * j:CH * (j + 1)], cur[:, CH * j:CH * (j + 1)]], axis=0).astype(F32)
    lo = lax.broadcasted_iota(jnp.int32, blk.shape, 1) < HD
    rolled = pltpu.roll(blk, HD, 1)
    even = jnp.where(lo, blk if half == 0 else rolled, 0.0)
    odd = jnp.where(lo, 0.0, rolled if half == 0 else blk)
    return jnp.concatenate([even, odd], axis=0).astype(BF16)


def _attn_mask(n):
    qi = lax.broadcasted_iota(jnp.int32, (CH, 2 * CH), 0)
    kc = lax.broadcasted_iota(jnp.int32, (CH, 2 * CH), 1)
    ok = (kc > qi) & (kc <= qi + CH) & ((kc >= CH) | (n > 0))
    return jnp.concatenate([ok, ok], axis=1)


def _softmax_sink(s, sink):
    m = jnp.maximum(jnp.max(s, axis=-1, keepdims=True), sink)
    p = jnp.exp(s - m)
    ps = jnp.exp(sink - m)
    inv = 1.0 / (jnp.sum(p, axis=-1, keepdims=True) + ps)
    return p * inv, ps * inv


QUERY_BLOCKS_PER_STEP = 2


def _attn_fwd(q, k, va, sinks, comm=None):
    T = q.shape[0]
    nblk = QUERY_BLOCKS_PER_STEP
    nsteps = T // (nblk * CH)
    npairs = D // CH

    def body(sk_ref, q_ref, kp_ref, kc_ref, vp_ref, vc_ref, o_ref):
        n = pl.program_id(0)
        even_lanes = lax.broadcasted_iota(jnp.int32, (CH, CH), 1) < HD
        ks = [kp_ref[...]] + [kc_ref[CH * b:CH * (b + 1)] for b in range(nblk)]
        vs = [vp_ref[...]] + [vc_ref[CH * b:CH * (b + 1)] for b in range(nblk)]
        masks = [_attn_mask(nblk * n)] + [_attn_mask(1)] * (nblk - 1)
        kks = [[_pair_layout(ks[b], ks[b + 1], grp) for grp in range(NKV)] for b in range(nblk)]
        vvs = [[_pair_layout(vs[b], vs[b + 1], grp) for grp in range(NKV)] for b in range(nblk)]
        work = [(b, p) for b in range(nblk) for p in range(npairs)]

        def scores(i):
            b, p = work[i]
            return _dot_nt(q_ref[CH * b:CH * (b + 1), CH * p:CH * (p + 1)], kks[b][p // 2])

        def unnormalised(s, sink):
            m = jnp.maximum(jnp.max(s, axis=-1, keepdims=True), sink)
            p = jnp.exp(s - m)
            return p, 1.0 / (jnp.sum(p, axis=-1, keepdims=True) + jnp.exp(sink - m))

        def value_product(i):
            b, p = work[i]
            pr, ie, io = probs[i]
            return _dot(pr, vvs[b][p // 2]) * jnp.where(even_lanes, ie, io)

        ahead = 3
        outs, probs = [], []
        pending = [scores(i) for i in range(ahead)]
        for i, (b, p) in enumerate(work):
            s = jnp.where(masks[b], pending.pop(0), -1e30)
            if i + ahead < len(work):
                pending.append(scores(i + ahead))
            pe, ie = unnormalised(s[:, :2 * CH], sk_ref[2 * p])
            po, io = unnormalised(s[:, 2 * CH:], sk_ref[2 * p + 1])
            probs.append((jnp.concatenate([pe, po], axis=1).astype(BF16), ie, io))
            if i >= 1:
                outs.append(value_product(i - 1))
        outs.append(value_product(len(work) - 1))
        for b in range(nblk):
            o_ref[CH * b:CH * (b + 1), :] = jnp.concatenate(outs[npairs * b:npairs * (b + 1)], axis=1).astype(BF16)

    prev = lambda n: (jnp.maximum(nblk * n - 1, 0), 0)
    cur = lambda n: (n, 0)
    return _call(
        body, (sinks, q, k, k, va, va), name="attn_fwd", grid=(nsteps,),
        in_specs=[pl.BlockSpec(memory_space=pltpu.SMEM), pl.BlockSpec((nblk * CH, D), cur),
                  pl.BlockSpec((CH, KVW), prev), pl.BlockSpec((nblk * CH, KVW), cur),
                  pl.BlockSpec((CH, KVW), prev), pl.BlockSpec((nblk * CH, KVW), cur)],
        out_specs=pl.BlockSpec((nblk * CH, D), cur), out_shape=jax.ShapeDtypeStruct((T, D), BF16),
        sem=("parallel",), comm=comm)


def _attn_bwd(q, k, va, datt, sinks, rc, rs1, rs2, comm=None):
    T = q.shape[0]
    nb = T // CH

    def body(sk_ref, q_ref, kp_ref, kc_ref, vp_ref, vc_ref, do_ref, cq_ref, s1q_ref, s2q_ref, ck_ref, s1k_ref, s2k_ref,
             dq_ref, dk_ref, dv_ref, dsk_ref, kcar, vcar):
        n = pl.program_id(0)

        @pl.when(n == 0)
        def _():
            kcar[...] = jnp.zeros_like(kcar)
            vcar[...] = jnp.zeros_like(vcar)
            dsk_ref[...] = jnp.zeros_like(dsk_ref)

        def flush(kprev, vprev):
            ck, s1k, s2k = ck_ref[...], s1k_ref[...], s2k_ref[...]
            for j in range(KVW // CH):
                sl = slice(CH * j, CH * (j + 1))
                dk_ref[:, sl] = _rope_t(kcar[:, sl] + kprev[:, sl], ck, s1k, s2k).astype(BF16)
                dv_ref[:, sl] = (vcar[:, sl] + vprev[:, sl]).astype(BF16)

        @pl.when(n < nb)
        def _():
            mask = _attn_mask(n)
            kp, kc, vp, vc = kp_ref[...], kc_ref[...], vp_ref[...], vc_ref[...]
            cq, s1q, s2q = cq_ref[...], s1q_ref[...], s2q_ref[...]
            lane = lax.broadcasted_iota(jnp.int32, (1, CH), 1)
            dsk = jnp.zeros((1, CH), F32)
            npairs = D // CH
            kks = [_pair_layout(kp, kc, grp) for grp in range(NKV)]
            vvs = [_pair_layout(vp, vc, grp) for grp in range(NKV)]
            qs = [q_ref[:, CH * p:CH * (p + 1)] for p in range(npairs)]
            dos = [do_ref[:, CH * p:CH * (p + 1)].astype(BF16) for p in range(npairs)]

            def first(p):
                return _dot_nt(qs[p], kks[p // 2]), _dot_nt(dos[p], vvs[p // 2])

            def last(p, ds, pb):
                return (_rope_t(_dot(ds, kks[p // 2]), cq, s1q, s2q) * Q_SCALE, _dot_tn(qs[p], ds), _dot_tn(dos[p], pb))

            ahead = 2
            pending = [first(p) for p in range(ahead)]
            mids, ends = [], []
            for p in range(npairs):
                s, dp = pending.pop(0)
                s = jnp.where(mask, s, -1e30)
                if p + ahead < npairs:
                    pending.append(first(p + ahead))
                ds_parts, p_parts = [], []
                for par in range(2):
                    sl = slice(2 * CH * par, 2 * CH * (par + 1))
                    pr, psink = _softmax_sink(s[:, sl], sk_ref[2 * p + par])
                    delta = jnp.sum(pr * dp[:, sl], axis=-1, keepdims=True)
                    ds_parts.append(pr * (dp[:, sl] - delta))
                    p_parts.append(pr)
                    tot = -jnp.sum(psink * delta, axis=0, keepdims=True)
                    dsk = dsk + jnp.where(lane == 2 * p + par, tot, 0.0)
                mids.append((jnp.concatenate(ds_parts, axis=1).astype(BF16), jnp.concatenate(p_parts, axis=1).astype(BF16)))
                if p >= 1:
                    ends.append(last(p - 1, *mids[p - 1]))
            ends.append(last(npairs - 1, *mids[-1]))
            dq_cols = [e[0] for e in ends]
            def fold(i):
                rows = []
                for grp in range(NKV):
                    acc = ends[2 * grp][i] + ends[2 * grp + 1][i]
                    rows.append(acc[:HD, :2 * CH] + acc[HD:, 2 * CH:])
                return jnp.concatenate(rows, axis=0).T

            dkf, dvf = fold(1), fold(2)
            dq_ref[...] = jnp.concatenate(dq_cols, axis=1).astype(BF16)
            dsk_ref[...] += dsk
            flush(dkf[:CH], dvf[:CH])
            kcar[...] = dkf[CH:]
            vcar[...] = dvf[CH:]

        @pl.when(n == nb)
        def _():
            z = jnp.zeros((CH, KVW), F32)
            flush(z, z)

    last = nb - 1
    cur = lambda n: (jnp.minimum(n, last), 0)
    prev = lambda n: (jnp.clip(n - 1, 0, last), 0)
    sd = jax.ShapeDtypeStruct
    return _call(
        body, (sinks, q, k, k, va, va, datt, rc, rs1, rs2, rc, rs1, rs2), name="attn_bwd", grid=(nb + 1,),
        in_specs=[pl.BlockSpec(memory_space=pltpu.SMEM), pl.BlockSpec((CH, D), cur),
                  pl.BlockSpec((CH, KVW), prev), pl.BlockSpec((CH, KVW), cur),
                  pl.BlockSpec((CH, KVW), prev), pl.BlockSpec((CH, KVW), cur),
                  pl.BlockSpec((CH, D), cur),
                  pl.BlockSpec((CH, CH), cur), pl.BlockSpec((CH, CH), cur), pl.BlockSpec((CH, CH), cur),
                  pl.BlockSpec((CH, CH), prev), pl.BlockSpec((CH, CH), prev), pl.BlockSpec((CH, CH), prev)],
        out_specs=[pl.BlockSpec((CH, D), cur), pl.BlockSpec((CH, KVW), prev), pl.BlockSpec((CH, KVW), prev),
                   _const((1, CH))],
        out_shape=[sd((T, D), BF16), sd((T, KVW), BF16), sd((T, KVW), BF16), sd((1, CH), F32)],
        scratch_shapes=[pltpu.VMEM((CH, KVW), F32), pltpu.VMEM((CH, KVW), F32)], sem=("arbitrary",), comm=comm)


def _merge_fwd(a, att, ga, gb, x, w_a, w_b, w_o, g2, tm, comm=None):
    T = x.shape[0]

    def body(a_ref, att_ref, ga_ref, gb_ref, x_ref, wa_ref, wb_ref, wo_ref, g_ref,
             pa_ref, pb_ref, mg_ref, mix_ref, x1_ref):
        pa = _dot(a_ref[...], wa_ref[...])
        pb = _dot(att_ref[...], wb_ref[...])
        pa_ref[...] = pa.astype(BF16)
        pb_ref[...] = pb.astype(BF16)
        merged = (_sigmoid(ga_ref[...].astype(F32)) * pa + _sigmoid(gb_ref[...].astype(F32)) * pb).astype(BF16)
        mg_ref[...] = merged
        mix = _dot(merged, wo_ref[...])
        mix_ref[...] = mix
        mhat, _ = _rms_hat(mix)
        x1_ref[...] = x_ref[...] + mhat * g_ref[...]

    sd = jax.ShapeDtypeStruct
    return _call(
        body, (a, att, ga, gb, x, w_a, w_b, w_o, g2), name="merge_fwd", grid=(T // tm,),
        in_specs=[_rows(tm, D)] * 5 + [_resident((D, D))] * 3 + [_const((1, D))],
        out_specs=[_rows(tm, D)] * 5,
        out_shape=[sd((T, D), BF16), sd((T, D), BF16), sd((T, D), BF16), sd((T, D), F32), sd((T, D), F32)],
        sem=("parallel",), comm=comm)


def _sgu_merge_fwd(u, vs, lng, lnb, ws, bfull, att, ga, gb, x, w_a, w_b, w_o, g2, tm, comm=None):
    T = x.shape[0]

    def body(u_ref, v_ref, lng_ref, lnb_ref, ws_ref, bf_ref, att_ref, ga_ref, gb_ref, x_ref, wa_ref, wb_ref, wo_ref, g_ref,
             a_ref, pa_ref, pb_ref, mg_ref, mix_ref, x1_ref):
        pb = _dot(att_ref[...], wb_ref[...])
        nc, ug, _, _, _, _, _, _, _, mixed = _sgu_common(
            u_ref[...], v_ref[...], lng_ref[...], lnb_ref[...], ws_ref, bf_ref[...])
        mixed_all = jnp.concatenate(
            [jnp.concatenate([mixed[g][c] for g in range(NG)], axis=1) for c in range(nc)], axis=0)
        a = (ug * mixed_all).astype(BF16)
        a_ref[...] = a
        pa = _dot(a, wa_ref[...])
        pa_ref[...] = pa.astype(BF16)
        pb_ref[...] = pb.astype(BF16)
        merged = (_sigmoid(ga_ref[...].astype(F32)) * pa + _sigmoid(gb_ref[...].astype(F32)) * pb).astype(BF16)
        mg_ref[...] = merged
        mix = _dot(merged, wo_ref[...])
        mix_ref[...] = mix
        mhat, _ = _rms_hat(mix)
        x1_ref[...] = x_ref[...] + mhat * g_ref[...]

    sd = jax.ShapeDtypeStruct
    return _call(
        body, (u, vs, lng, lnb, ws, bfull, att, ga, gb, x, w_a, w_b, w_o, g2), name="sgu_merge_fwd", grid=(T // tm,),
        in_specs=[_rows(tm, D), _rows(tm, D), _const((1, D)), _const((1, D)), _const((NG, CH, CH)), _const((CH, D))]
        + [_rows(tm, D)] * 4 + [_resident((D, D))] * 3 + [_const((1, D))],
        out_specs=[_rows(tm, D)] * 6,
        out_shape=[sd((T, D), BF16)] * 4 + [sd((T, D), F32)] * 2,
        sem=("parallel",), comm=comm, vmem=VMEM_PHYSICAL)


def _merge_bwd(dx1, mix, ga, gb, pa, pb, a, att, merged, w_a, w_b, w_o, g2, tm, comm=None):
    T = dx1.shape[0]
    nsteps = T // tm

    def body(dx1_ref, mix_ref, ga_ref, gb_ref, pa_ref, pb_ref, a_ref, att_ref, mg_ref, wa_ref, wb_ref, wo_ref, g_ref,
             dga_ref, dgb_ref, da_ref, datt_ref, dg_ref, dwa_ref, dwb_ref, dwo_ref, acc, sem):
        i = pl.program_id(0)

        @pl.when(i == 0)
        def _():
            dg_ref[...] = jnp.zeros_like(dg_ref)
            acc[...] = jnp.zeros_like(acc)

        mhat, r = _rms_hat(mix_ref[...])
        dmix, dg = _rms_bwd(mhat, r, g_ref[...], dx1_ref[...])
        dg_ref[...] += dg
        dmix = dmix.astype(BF16)
        dmerged = _dot_nt(dmix, wo_ref[...])
        sa = _sigmoid(ga_ref[...].astype(F32))
        sb = _sigmoid(gb_ref[...].astype(F32))
        dao = (dmerged * sa).astype(BF16)
        dbo = (dmerged * sb).astype(BF16)
        dga_ref[...] = (dmerged * pa_ref[...].astype(F32) * (sa * (1.0 - sa))).astype(BF16)
        dgb_ref[...] = (dmerged * pb_ref[...].astype(F32) * (sb * (1.0 - sb))).astype(BF16)
        da_ref[...] = _dot_nt(dao, wa_ref[...])
        datt_ref[...] = _dot_nt(dbo, wb_ref[...]).astype(BF16)
        acc[0] += _dot_tn(a_ref[...], dao)
        acc[1] += _dot_tn(att_ref[...], dbo)
        acc[2] += _dot_tn(mg_ref[...], dmix)

        @pl.when(i == nsteps - 1)
        def _():
            outs = [pltpu.make_async_copy(acc.at[j], ref, sem.at[j]) for j, ref in enumerate((dwa_ref, dwb_ref, dwo_ref))]
            for cp in outs:
                cp.start()
            for cp in outs:
                cp.wait()

    sd = jax.ShapeDtypeStruct
    return _call(
        body, (dx1, mix, ga, gb, pa, pb, a, att, merged, w_a, w_b, w_o, g2), name="merge_bwd", grid=(nsteps,),
        in_specs=[_rows(tm, D)] * 9 + [_resident((D, D))] * 3 + [_const((1, D))],
        out_specs=[_rows(tm, D)] * 4 + [_const((1, D))] + [ANY] * 3,
        out_shape=[sd((T, D), BF16), sd((T, D), BF16), sd((T, D), F32), sd((T, D), BF16), sd((1, D), F32)]
        + [sd((D, D), F32)] * 3,
        scratch_shapes=[pltpu.VMEM((3, D, D), F32), _dma_sems(3)], sem=("arbitrary",), comm=comm)


def _ffn(x1, target, w1, w2, g3, g4, tm):
    T = x1.shape[0]

    def body(x_ref, t_ref, w1_ref, w2_ref, g3_ref, g4_ref,
             hf_ref, f2_ref, dff_ref, df1_ref, dx_ref, ls_ref, dg3_ref, dg4_ref):
        @pl.when(pl.program_id(0) == 0)
        def _():
            ls_ref[...] = jnp.zeros_like(ls_ref)
            dg3_ref[...] = jnp.zeros_like(dg3_ref)
            dg4_ref[...] = jnp.zeros_like(dg4_ref)

        x = x_ref[...]
        g3, g4 = g3_ref[...], g4_ref[...]
        xhat, r3 = _rms_hat(x)
        hf = (xhat * g3).astype(BF16)
        hf_ref[...] = hf
        rl = jnp.maximum(_dot(hf, w1_ref[...]), 0.0)
        f2 = (rl * rl).astype(BF16)
        f2_ref[...] = f2
        fhat, r4 = _rms_hat(_dot(f2, w2_ref[...]))
        err = x + fhat * g4 - t_ref[...]
        ls_ref[...] += jnp.sum(err * err, axis=0, keepdims=True)
        dy = err * (1.0 / D)
        dff, dg4 = _rms_bwd(fhat, r4, g4, dy)
        dg4_ref[...] += dg4
        dff = dff.astype(BF16)
        dff_ref[...] = dff
        df1 = (_dot_nt(dff, w2_ref[...]) * (2.0 * rl)).astype(BF16)
        df1_ref[...] = df1
        dxn, dg3 = _rms_bwd(xhat, r3, g3, _dot_nt(df1, w1_ref[...]))
        dg3_ref[...] += dg3
        dx_ref[...] = dy + dxn

    sd = jax.ShapeDtypeStruct
    return pl.pallas_call(
        body, name="ffn_fwd_bwd", grid=(T // tm,),
        in_specs=[_rows(tm, D), _rows(tm, D), _resident((D, DFF)), _resident((DFF, D)), _const((1, D)), _const((1, D))],
        out_specs=[_rows(tm, D), _rows(tm, DFF), _rows(tm, D), _rows(tm, DFF), _rows(tm, D), _const((1, D)),
                   _const((1, D)), _const((1, D))],
        out_shape=[sd((T, D), BF16), sd((T, DFF), BF16), sd((T, D), BF16), sd((T, DFF), BF16), sd((T, D), F32),
                   sd((1, D), F32), sd((1, D), F32), sd((1, D), F32)],
        compiler_params=pltpu.CompilerParams(vmem_limit_bytes=VMEM_PHYSICAL, dimension_semantics=("arbitrary",)),
    )(x1, target, w1, w2, g3, g4)


def _inproj_bwd(parts, x, dx1, g1, w_in, tm, comm=None):
    T = x.shape[0]
    widths = [p.shape[1] for p in parts]
    offs = [sum(widths[:i]) for i in range(len(widths) + 1)]
    assert offs[-1] == IN_W

    def body(*refs):
        n = len(parts)
        prefs = refs[:n]
        x_ref, dx1_ref, g_ref, w_ref, dx_ref, dp_ref, dg_ref = refs[n:]

        @pl.when(pl.program_id(0) == 0)
        def _():
            dg_ref[...] = jnp.zeros_like(dg_ref)

        for i in range(n):
            dp_ref[:, offs[i]:offs[i + 1]] = prefs[i][...]
        dh = _dot_nt(dp_ref[...], w_ref[...])
        xhat, r = _rms_hat(x_ref[...])
        dxn, dg = _rms_bwd(xhat, r, g_ref[...], dh)
        dg_ref[...] += dg
        dx_ref[...] = dx1_ref[...] + dxn

    sd = jax.ShapeDtypeStruct
    return _call(
        body, (*parts, x, dx1, g1, w_in), name="inproj_bwd", grid=(T // tm,),
        in_specs=[_rows(tm, w) for w in widths] + [_rows(tm, D), _rows(tm, D), _const((1, D)), _resident((D, IN_W))],
        out_specs=[_rows(tm, D), _rows(tm, IN_W), _const((1, D))],
        out_shape=[sd((T, D), F32), sd((T, IN_W), BF16), sd((1, D), F32)], sem=("arbitrary",), comm=comm)


def _wgrad(a, g, tn, tm, name, comm=None, vmem=None):
    T, K = a.shape
    N = g.shape[1]

    def body(a_ref, g_ref, o_ref):
        @pl.when(pl.program_id(1) == 0)
        def _():
            o_ref[...] = jnp.zeros_like(o_ref)

        o_ref[...] += _dot_tn(a_ref[...], g_ref[...])

    return _call(
        body, (a, g), name=name, grid=(N // tn, T // tm),
        in_specs=[pl.BlockSpec((tm, K), lambda j, t: (t, 0)), pl.BlockSpec((tm, tn), lambda j, t: (t, j))],
        out_specs=pl.BlockSpec((K, tn), lambda j, t: (0, j)),
        out_shape=jax.ShapeDtypeStruct((K, N), F32), sem=("parallel", "arbitrary"), comm=comm, vmem=vmem)


def _adamw(ws, gs, ms, vs, trs, name):
    n = len(ws)
    walk = _Walk(w.shape[0] // tr for w, tr in zip(ws, trs))
    bc1 = 1.0 / (1.0 - B1 ** STEP)
    bc2 = 1.0 / (1.0 - B2 ** STEP)

    def body(*refs):
        i = pl.program_id(0)
        for k in range(n):
            mine = tuple(refs[j * n + k] for j in range(8))

            @pl.when(walk.mine(k, i))
            def _(mine=mine):
                w_ref, g_ref, m_ref, v_ref, go_ref, d_ref, nm_ref, nv_ref = mine
                g = g_ref[...]
                go_ref[...] = g
                m = B1 * m_ref[...] + (1.0 - B1) * g
                v = B2 * v_ref[...] + (1.0 - B2) * (g * g)
                nm_ref[...] = m
                nv_ref[...] = v
                d_ref[...] = -LR * ((m * bc1) / (jnp.sqrt(v * bc2) + AEPS) + WD * w_ref[...])

    def spec(k):
        return pl.BlockSpec((trs[k], ws[k].shape[1]), lambda i: (walk.tile(k, i), 0))

    specs = [spec(k) for k in range(n)]
    res = pl.pallas_call(
        body, name=name, grid=(walk.steps,), in_specs=specs * 4, out_specs=specs * 4,
        out_shape=[jax.ShapeDtypeStruct(w.shape, F32) for w in ws] * 4,
        compiler_params=_cparams(("arbitrary",)),
    )(*ws, *gs, *ms, *vs)
    return [tuple(res[j * n + k] for j in range(4)) for k in range(n)]


BIG = (("col", (D, IN_W)), ("row", (D, D)), ("row", (D, D)), ("row", (D, D)), ("col", (D, DFF)), ("row", (DFF, D)))
NBIG = len(BIG)
ANY = pl.BlockSpec(memory_space=pl.ANY)


def _shard_shape(kind, shape):
    R, C = shape
    return (R, C // 4) if kind == "col" else (R // 4, C)


def _half_shape(kind, shape):
    R, C = shape
    return (R // 2, C) if kind == "col" else (R, C // 2)


def _piece_shape(kind, shape):
    R, C = shape
    return (R // 2, C // 4) if kind == "col" else (R // 4, C // 2)


def _own_region(ref, kind, shape, s):
    R, C = shape
    return ref.at[:, pl.ds(s * (C // 4), C // 4)] if kind == "col" else ref.at[pl.ds(s * (R // 4), R // 4), :]


def _ag_region(ref, kind, shape, s, hc):
    R, C = shape
    if kind == "col":
        return ref.at[pl.ds(hc * (R // 2), R // 2), pl.ds(s * (C // 4), C // 4)]
    return ref.at[pl.ds(s * (R // 4) + hc * (R // 8), R // 8), :]


def _ag_shard_half(ref, kind, shape, hc):
    R, C = shape
    return ref.at[pl.ds(hc * (R // 2), R // 2), :] if kind == "col" else ref.at[pl.ds(hc * (R // 8), R // 8), :]


def _grad_half(ref, kind, shape, hc):
    R, C = shape
    return ref.at[pl.ds(hc * (R // 2), R // 2), :] if kind == "col" else ref.at[:, pl.ds(hc * (C // 2), C // 2)]


def _half_piece(ref, kind, shape, s):
    R, C = shape
    return ref.at[:, pl.ds(s * (C // 4), C // 4)] if kind == "col" else ref.at[pl.ds(s * (R // 4), R // 4), :]


def _place():
    x, y, c = lax.axis_index("x"), lax.axis_index("y"), lax.axis_index("c")
    chips = [(1 - x, y), (x, 1 - y), (1 - x, 1 - y)]
    return x, y, c, chips


def _rcopy(src, dst, ssem, rsem, dev):
    return pltpu.make_async_remote_copy(src_ref=src, dst_ref=dst, send_sem=ssem, recv_sem=rsem,
                                        device_id=dev, device_id_type=MESH)


def _dma_sems(n):
    return pltpu.SemaphoreType.DMA((n,))


def _x_gather_ici(shards, ws):
    n = len(ws)
    specs = [BIG[w] for w in ws]

    def place():
        x, y, c, chips = _place()
        return c, chips, 2 * x + y

    def sends(sh, full, sc):
        c, chips, me_s = place()
        return [_rcopy(_ag_shard_half(sh[i], kind, shape, c), _ag_region(full[i], kind, shape, me_s, c),
                       sc[0].at[3 * i + j], sc[1].at[3 * i + j], (cx, cy, c))
                for i, (kind, shape) in enumerate(specs) for j, (cx, cy) in enumerate(chips)]

    def start(sh, full, sc):
        for i in range(n):
            pltpu.make_async_copy(sh[i], sc[4 + i], sc[2].at[i]).start()
        for cp in sends(sh, full, sc):
            cp.start()

    def finish(sh, full, sc):
        c, chips, me_s = place()
        stores = []
        for i, (kind, shape) in enumerate(specs):
            pltpu.make_async_copy(sh[i], sc[4 + i], sc[2].at[i]).wait()
            st = pltpu.make_async_copy(sc[4 + i], _own_region(full[i], kind, shape, me_s), sc[3].at[i])
            st.start()
            stores.append(st)
        for i, (kind, shape) in enumerate(specs):
            for j, (cx, cy) in enumerate(chips):
                reg = _ag_region(full[i], kind, shape, 2 * cx + cy, c)
                _rcopy(reg, reg, sc[0].at[3 * i + j], sc[1].at[3 * i + j], (cx, cy, c)).wait_recv()
        for cp in sends(sh, full, sc):
            cp.wait_send()
        for st in stores:
            st.wait()

    return _Exchange(
        shards, [jax.ShapeDtypeStruct(shape, BF16) for _, shape in specs], {},
        [_dma_sems(3 * n), _dma_sems(3 * n), _dma_sems(n), _dma_sems(n)]
        + [pltpu.VMEM(_shard_shape(k, s), BF16) for k, s in specs], start, finish)


def _x_gather_d2d(wholes, ws):
    specs = [BIG[w] for w in ws]
    n = len(ws)

    def copies(full, sc, mine):
        x, y, c, chips = _place()
        hc = c if mine else 1 - c
        return [_rcopy(reg, reg, sc[0].at[3 * i + j], sc[1].at[3 * i + j], (x, y, 1 - c))
                for i, (kind, shape) in enumerate(specs) for j, (cx, cy) in enumerate(chips)
                for reg in [_ag_region(full[i], kind, shape, 2 * cx + cy, hc)]]

    def start(_, full, sc):
        for cp in copies(full, sc, True):
            cp.start()

    def finish(_, full, sc):
        for cp in copies(full, sc, False):
            cp.wait_recv()
        for cp in copies(full, sc, True):
            cp.wait_send()

    return _Exchange(wholes, [jax.ShapeDtypeStruct(shape, BF16) for _, shape in specs], {i: i for i in range(n)},
                     [_dma_sems(3 * n), _dma_sems(3 * n)], start, finish)


def _x_grads_sibling(grads, ws):
    specs = [BIG[w] for w in ws]
    n = len(ws)

    def copies(g, got, sc):
        x, y, c, _ = _place()
        return [_rcopy(_grad_half(g[i], kind, shape, 1 - c), got[i], sc[0].at[i], sc[1].at[i], (x, y, 1 - c))
                for i, (kind, shape) in enumerate(specs)]

    def start(g, got, sc):
        for cp in copies(g, got, sc):
            cp.start()

    def finish(g, got, sc):
        for cp in copies(g, got, sc):
            cp.wait_recv()
        for cp in copies(g, got, sc):
            cp.wait_send()

    return _Exchange(grads, [jax.ShapeDtypeStruct(_half_shape(k, s), F32) for k, s in specs], {},
                     [_dma_sems(n), _dma_sems(n)], start, finish)


def _x_grads_chips(sums_bf, ws):
    specs = [BIG[w] for w in ws]
    n = len(ws)

    def copies(s16, got, sc):
        x, y, c, chips = _place()
        return [_rcopy(_half_piece(s16[i], kind, shape, 2 * cx + cy), got[i].at[j],
                       sc[0].at[3 * i + j], sc[1].at[3 * i + j], (cx, cy, c))
                for i, (kind, shape) in enumerate(specs) for j, (cx, cy) in enumerate(chips)]

    def start(s16, got, sc):
        for cp in copies(s16, got, sc):
            cp.start()

    def finish(s16, got, sc):
        for cp in copies(s16, got, sc):
            cp.wait_recv()
        for cp in copies(s16, got, sc):
            cp.wait_send()

    return _Exchange(sums_bf, [jax.ShapeDtypeStruct((3,) + _piece_shape(k, s), BF16) for k, s in specs], {},
                     [_dma_sems(3 * n), _dma_sems(3 * n)], start, finish)


def _shard_half(ref, kind, shape, hc):
    sr, sc = _shard_shape(kind, shape)
    return ref.at[pl.ds(hc * (sr // 2), sr // 2), :] if kind == "col" else ref.at[:, pl.ds(hc * (sc // 2), sc // 2)]


def _x_grads_share(shard_grads, ws):
    specs = [BIG[w] for w in ws]
    n = len(ws)

    def copies(g, sc, mine):
        x, y, c, _ = _place()
        hc = c if mine else 1 - c
        return [_rcopy(part, part, sc[0].at[i], sc[1].at[i], (x, y, 1 - c))
                for i, (kind, shape) in enumerate(specs) for part in [_shard_half(g[i], kind, shape, hc)]]

    def start(_, g, sc):
        for cp in copies(g, sc, True):
            cp.start()

    def finish(_, g, sc):
        for cp in copies(g, sc, False):
            cp.wait_recv()
        for cp in copies(g, sc, True):
            cp.wait_send()

    return _Exchange(shard_grads, [jax.ShapeDtypeStruct(_shard_shape(k, s), F32) for k, s in specs],
                     {i: i for i in range(n)}, [_dma_sems(n), _dma_sems(n)], start, finish)


ADD_BLOCK_BYTES = 4 * 1024 * 1024


def _add_rows(rows, cols, n_arrays):
    limit = ADD_BLOCK_BYTES // (1 if n_arrays == 1 else 4)
    r = rows
    while r > 64 and r * cols * 4 > limit:
        r //= 2
    return r


class _Walk:
    def __init__(self, tiles):
        self.tiles = list(tiles)
        self.starts = [sum(self.tiles[:k]) for k in range(len(self.tiles))]
        self.steps = sum(self.tiles)

    def tile(self, k, i):
        return jnp.clip(i - self.starts[k], 0, self.tiles[k] - 1)

    def mine(self, k, i):
        return (i >= self.starts[k]) & (i < self.starts[k] + self.tiles[k])


def _add_halves(place, gs, gots, kinds, name):
    n = len(gs)
    halves = [_half_shape(kind, g.shape) for g, kind in zip(gs, kinds)]
    rows = [_add_rows(hr, hc, n) for hr, hc in halves]
    walk = _Walk(hr // r for (hr, _), r in zip(halves, rows))

    def body(p_ref, *refs):
        i = pl.program_id(0)
        for k in range(n):
            g_ref, b_ref, s_ref, sb_ref = (refs[j * n + k] for j in range(4))

            @pl.when(walk.mine(k, i))
            def _(g_ref=g_ref, b_ref=b_ref, s_ref=s_ref, sb_ref=sb_ref):
                s = g_ref[...] + b_ref[...]
                s_ref[...] = s
                sb_ref[...] = s.astype(BF16)

    def g_spec(k):
        if kinds[k] == "col":
            return pl.BlockSpec((rows[k], gs[k].shape[1]), lambda i, p: (p[0] * walk.tiles[k] + walk.tile(k, i), 0))
        return pl.BlockSpec((rows[k], halves[k][1]), lambda i, p: (walk.tile(k, i), p[0]))

    def spec(k):
        return pl.BlockSpec((rows[k], halves[k][1]), lambda i, p: (walk.tile(k, i), 0))

    specs = [spec(k) for k in range(n)]
    res = pl.pallas_call(
        body, name=name,
        grid_spec=pltpu.PrefetchScalarGridSpec(num_scalar_prefetch=1, grid=(walk.steps,),
                                               in_specs=[g_spec(k) for k in range(n)] + specs, out_specs=specs + specs),
        out_shape=[jax.ShapeDtypeStruct(h, F32) for h in halves] + [jax.ShapeDtypeStruct(h, BF16) for h in halves],
        compiler_params=_cparams(("arbitrary",)),
    )(place, *gs, *gots)
    return [(res[k], res[n + k]) for k in range(n)]


def _add_pieces(place, halves, gots, specs_big, name):
    n = len(halves)
    pieces = [_piece_shape(kind, shape) for kind, shape in specs_big]
    rows = [_add_rows(pr, pc, n) for pr, pc in pieces]
    walk = _Walk(pr // r for (pr, _), r in zip(pieces, rows))

    def body(p_ref, *refs):
        i = pl.program_id(0)
        for k in range(n):
            m_ref, g_ref, o_ref = (refs[j * n + k] for j in range(3))

            @pl.when(walk.mine(k, i))
            def _(m_ref=m_ref, g_ref=g_ref, o_ref=o_ref):
                acc = m_ref[...]
                for j in range(3):
                    acc = acc + g_ref[j].astype(F32)
                o_ref[...] = acc

    def m_spec(k):
        if specs_big[k][0] == "col":
            return pl.BlockSpec((rows[k], pieces[k][1]), lambda i, p: (walk.tile(k, i), p[1]))
        return pl.BlockSpec((rows[k], pieces[k][1]), lambda i, p: (p[1] * walk.tiles[k] + walk.tile(k, i), 0))

    def got_spec(k):
        return pl.BlockSpec((3, rows[k], pieces[k][1]), lambda i, p: (0, walk.tile(k, i), 0))

    def o_spec(k):
        if specs_big[k][0] == "col":
            return pl.BlockSpec((rows[k], pieces[k][1]), lambda i, p: (p[0] * walk.tiles[k] + walk.tile(k, i), 0))
        return pl.BlockSpec((rows[k], pieces[k][1]), lambda i, p: (walk.tile(k, i), p[0]))

    return pl.pallas_call(
        body, name=name,
        grid_spec=pltpu.PrefetchScalarGridSpec(
            num_scalar_prefetch=1, grid=(walk.steps,),
            in_specs=[m_spec(k) for k in range(n)] + [got_spec(k) for k in range(n)],
            out_specs=[o_spec(k) for k in range(n)]),
        out_shape=[jax.ShapeDtypeStruct(_shard_shape(kind, shape), F32) for kind, shape in specs_big],
        compiler_params=_cparams(("arbitrary",)),
    )(place, *halves, *gots)


SMALL_ROWS = 1024 + 8 * 8 + 8


def _x_small_all_reduce(p):
    def parts(p_ref, sc):
        slots, ssem, rsem = sc[0], sc[2], sc[3]
        x, y, c = lax.axis_index("x"), lax.axis_index("y"), lax.axis_index("c")
        me = 4 * x + 2 * y + c
        out = []
        for r in range(1, 8):
            bx, by, bc = (r >> 2) & 1, (r >> 1) & 1, r & 1
            tgt = (1 - x if bx else x, 1 - y if by else y, 1 - c if bc else c)
            send = _rcopy(p_ref, slots.at[me], ssem.at[r - 1], rsem.at[r - 1], tgt)
            src = 4 * tgt[0] + 2 * tgt[1] + tgt[2]
            recv = _rcopy(p_ref, slots.at[src], ssem.at[r - 1], rsem.at[r - 1], tgt)
            out.append((send, recv))
        return me, out

    def start(ins, outs, sc):
        me, cps = parts(ins[0], sc)
        pltpu.make_async_copy(ins[0], sc[0].at[me], sc[4].at[0]).start()
        for send, _ in cps:
            send.start()

    def finish(ins, outs, sc):
        me, cps = parts(ins[0], sc)
        pltpu.make_async_copy(ins[0], sc[0].at[me], sc[4].at[0]).wait()
        for _, recv in cps:
            recv.wait_recv()
        acc = sc[0][0]
        for d in range(1, 8):
            acc = acc + sc[0][d]
        sc[1][...] = acc
        back = pltpu.make_async_copy(sc[1], outs[0], sc[4].at[1])
        back.start()
        for send, _ in cps:
            send.wait_send()
        back.wait()

    return _Exchange([p], [jax.ShapeDtypeStruct((SMALL_ROWS, CH), F32)], {},
                     [pltpu.VMEM((8, SMALL_ROWS, CH), F32), pltpu.VMEM((SMALL_ROWS, CH), F32), _dma_sems(7), _dma_sems(7),
                      _dma_sems(2)], start, finish)


def _rope_tables(positions, comm=None):
    T = positions.shape[0]
    inv_freq = 500000.0 ** (-jnp.arange(0, 2 * ROPE_HALF, 2, dtype=F32) / (2 * ROPE_HALF))
    head = jnp.concatenate([inv_freq, inv_freq, jnp.zeros((HD - 2 * ROPE_HALF,), F32)])
    lane_freq = jnp.concatenate([head, head])[None, :]
    pos = jnp.broadcast_to(positions.astype(F32)[:, None], (T, CH))
    tm = min(1024, T)

    def body(p_ref, f_ref, c_ref, s1_ref, s2_ref):
        ang = p_ref[...] * f_ref[...]
        sin = jnp.sin(ang)
        first = (lax.broadcasted_iota(jnp.int32, ang.shape, 1) % HD) < ROPE_HALF
        c_ref[...] = jnp.cos(ang)
        s1_ref[...] = jnp.where(first, -sin, 0.0)
        s2_ref[...] = jnp.where(first, 0.0, sin)

    return _call(body, (pos, lane_freq), name="rope_tables", grid=(T // tm,),
                 in_specs=[_rows(tm, CH), _const((1, CH))], out_specs=[_rows(tm, CH)] * 3,
                 out_shape=[jax.ShapeDtypeStruct((T, CH), F32)] * 3, sem=("parallel",), comm=comm)


BIG_NAMES = ("w_in", "w_a", "w_b", "w_o", "w_ff_in", "w_ff_out")
SMALL_NAMES = ("w_spatial", "ln_v_gain", "ln_v_bias", "b_spatial", "sinks", "norm_mix_pre", "norm_mix_post",
               "norm_ff_pre", "norm_ff_post")
WEIGHT_ORDER = ("w_in", "ln_v_gain", "ln_v_bias", "w_spatial", "b_spatial", "sinks", "w_a", "w_b", "w_o",
                "norm_mix_pre", "norm_mix_post", "w_ff_in", "w_ff_out", "norm_ff_pre", "norm_ff_post")


def _pack_small(d, loss_sums=None):
    parts = []
    for n in SMALL_NAMES:
        flat = d[n].reshape(-1)
        pad = (-flat.shape[0]) % (8 * CH)
        parts.append(jnp.pad(flat, (0, pad)).reshape(-1, CH))
    parts.append(jnp.zeros((8, CH), F32) if loss_sums is None else loss_sums.reshape(8, CH))
    return jnp.concatenate(parts, axis=0)


def _unpack_small(p, like):
    out, row = {}, 0
    for n in SMALL_NAMES:
        size = like[n].size
        rows = -(-size // (8 * CH)) * 8
        out[n] = p[row:row + rows].reshape(-1)[:size].reshape(like[n].shape)
        row += rows
    return out


def kernel(x, positions, w_in, ln_v_gain, ln_v_bias, w_spatial, b_spatial, sinks, w_a, w_b, w_o, norm_mix_pre, norm_mix_post, w_ff_in, w_ff_out, norm_ff_pre, norm_ff_post, loss_target, m_w_in, m_ln_v_gain, m_ln_v_bias, m_w_spatial, m_b_spatial, m_sinks, m_w_a, m_w_b, m_w_o, m_norm_mix_pre, m_norm_mix_post, m_w_ff_in, m_w_ff_out, m_norm_ff_pre, m_norm_ff_post, v_w_in, v_ln_v_gain, v_ln_v_bias, v_w_spatial, v_b_spatial, v_sinks, v_w_a, v_w_b, v_w_o, v_norm_mix_pre, v_norm_mix_post, v_w_ff_in, v_w_ff_out, v_norm_ff_pre, v_norm_ff_post):
    w = dict(w_in=w_in, ln_v_gain=ln_v_gain, ln_v_bias=ln_v_bias, w_spatial=w_spatial, b_spatial=b_spatial, sinks=sinks,
             w_a=w_a, w_b=w_b, w_o=w_o, norm_mix_pre=norm_mix_pre, norm_mix_post=norm_mix_post, w_ff_in=w_ff_in,
             w_ff_out=w_ff_out, norm_ff_pre=norm_ff_pre, norm_ff_post=norm_ff_post)
    m = dict(w_in=m_w_in, ln_v_gain=m_ln_v_gain, ln_v_bias=m_ln_v_bias, w_spatial=m_w_spatial, b_spatial=m_b_spatial,
             sinks=m_sinks, w_a=m_w_a, w_b=m_w_b, w_o=m_w_o, norm_mix_pre=m_norm_mix_pre, norm_mix_post=m_norm_mix_post,
             w_ff_in=m_w_ff_in, w_ff_out=m_w_ff_out, norm_ff_pre=m_norm_ff_pre, norm_ff_post=m_norm_ff_post)
    v = dict(w_in=v_w_in, ln_v_gain=v_ln_v_gain, ln_v_bias=v_ln_v_bias, w_spatial=v_w_spatial, b_spatial=v_b_spatial,
             sinks=v_sinks, w_a=v_w_a, w_b=v_w_b, w_o=v_w_o, norm_mix_pre=v_norm_mix_pre, norm_mix_post=v_norm_mix_post,
             w_ff_in=v_w_ff_in, w_ff_out=v_w_ff_out, norm_ff_pre=v_norm_ff_pre, norm_ff_post=v_norm_ff_post)

    FIRST, REST = (0,), tuple(range(1, NBIG))
    shards = [w[n][0].astype(BF16) for n in BIG_NAMES]
    place = jnp.stack([lax.axis_index("c"), 2 * lax.axis_index("x") + lax.axis_index("y")]).astype(jnp.int32)
    xs, target = x[0], loss_target[0]
    T = xs.shape[0]
    tile = min(TOKEN_TILE, T)
    wtiles = {n: dict(tm=min(tm, T), tn=tn) for n, (tm, tn) in WGRAD_TILES.items()}
    g1, g2, g3, g4 = norm_mix_pre, norm_mix_post, norm_ff_pre, norm_ff_post
    w_sp, snk = w_spatial[0], sinks[0]
    MIX, FF = (1, 2, 3), (4, 5)
    bfull = jnp.repeat(b_spatial[0].T, CH, axis=1)

    def reduce_tail(ws, grads, got):
        tag = "_".join(BIG_NAMES[k] for k in ws)
        sums = _add_halves(place, grads, got, [BIG[k][0] for k in ws], name="grad_add_sibling_" + tag)
        return sums, _x_grads_chips([s[1] for s in sums], ws)

    def reduce_end(ws, sums, pieces):
        tag = "_".join(BIG_NAMES[k] for k in ws)
        return _add_pieces(place, [s[0] for s in sums], pieces, [BIG[k] for k in ws], name="grad_add_chips_" + tag)

    (rc, rs1, rs2), w_in_part = _rope_tables(positions[0], comm=_x_gather_ici(shards[:1], FIRST))
    w_in_b = _run(_x_gather_d2d(w_in_part, FIRST), "gather_w_in_d2d")[0]
    EARLY, FF_OUT = (1, 2, 3, 4), (5,)
    (h, u, vs, q, k, va, ga, gb), early_part = _inproj(xs, g1, w_in_b, rc, rs1, rs2, tm=tile,
                                                      comm=_x_gather_ici(shards[1:5], EARLY))
    att, (w_a_b, w_b_b, w_o_b, w_ff_in_b, ffo_part) = _attn_fwd(
        q, k, va, snk, comm=_both(_x_gather_d2d(early_part, EARLY), _x_gather_ici(shards[5:], FF_OUT)))
    (a, pa, pb, merged, mix, x1), (w_ff_out_b,) = _sgu_merge_fwd(
        u, vs, ln_v_gain, ln_v_bias, w_sp, bfull, att, ga, gb, xs, w_a_b, w_b_b, w_o_b, g2, tm=tile,
        comm=_x_gather_d2d([ffo_part], FF_OUT))
    hf, f2, dff, df1, dx1, lsum, dg3, dg4 = _ffn(x1, target, w_ff_in_b, w_ff_out_b, g3, g4, tm=tile)

    dw_ff_out, _ = _wgrad(f2, dff, name="wgrad_ff_out", **wtiles["w_ff_out"])
    dw_ff_in, _ = _wgrad(hf, df1, name="wgrad_ff_in", **wtiles["w_ff_in"])
    (dga, dgb, da, datt, dg2, dw_a, dw_b, dw_o), _ = _merge_bwd(
        dx1, mix, ga, gb, pa, pb, a, att, merged, w_a_b, w_b_b, w_o_b, g2, tm=tile)
    grads_rest = [dw_a, dw_b, dw_o, dw_ff_in, dw_ff_out]
    (du, dvs, dws, dbs, dlg, dlb), got_rest = _sgu_bwd(
        u, vs, da, ln_v_gain, ln_v_bias, w_sp, bfull, tm=tile, comm=_x_grads_sibling(grads_rest, REST))
    sums_rest, to_chips = reduce_tail(REST, grads_rest, got_rest)
    (dq, dk, dva, dsk), pieces_rest = _attn_bwd(q, k, va, datt, snk, rc, rs1, rs2, comm=to_chips)
    partial_rest = reduce_end(REST, sums_rest, pieces_rest)
    (dx, dproj, dg1), _ = _inproj_bwd([du, dvs, dq, dk, dva, dga, dgb], xs, dx1, g1, w_in_b, tm=tile)
    small = dict(ln_v_gain=dlg, ln_v_bias=dlb, w_spatial=dws, b_spatial=dbs, sinks=dsk[:, :NQ],
                 norm_mix_pre=dg1, norm_mix_post=dg2, norm_ff_pre=dg3, norm_ff_post=dg4)
    dw_in, (gs, *shard_rest) = _wgrad(
        h, dproj, name="wgrad_in", vmem=VMEM_PHYSICAL, **wtiles["w_in"],
        comm=_both(_x_small_all_reduce(_pack_small(small, lsum)), _x_grads_share(partial_rest, REST)))
    got_in = _run(_x_grads_sibling([dw_in], FIRST), "grads_in_to_sibling")
    sums_in, to_chips = reduce_tail(FIRST, [dw_in], got_in)
    partial_in = reduce_end(FIRST, sums_in, _run(to_chips, "grads_in_to_chips"))
    g_in = _run(_x_grads_share(partial_in, FIRST), "grads_in_share")[0]

    loss = 0.5 * jnp.sum(gs[SMALL_ROWS - 8:]) / D
    grad, delta, new_m, new_v = {}, {}, {}, {}
    for n, g in zip(BIG_NAMES, [g_in] + list(shard_rest)):
        (g_, d_, m_, v_), = _adamw([w[n][0]], [g], [m[n][0]], [v[n][0]], [256], name="adamw_" + n)
        grad[n], delta[n], new_m[n], new_v[n] = g_[None], d_[None], m_[None], v_[None]
    (gs, ds, ms, vs), = _adamw([_pack_small(w)], [gs], [_pack_small(m)], [_pack_small(v)], [SMALL_ROWS], name="adamw_small")
    for packed, dst in ((gs, grad), (ds, delta), (ms, new_m), (vs, new_v)):
        dst.update(_unpack_small(packed, w))

    outs = [loss, dx[None]]
    for group in (grad, delta, new_m, new_v):
        outs.extend(group[n] for n in WEIGHT_ORDER)
    return tuple(outs)
```

```python
import functools

import jax
import jax.numpy as jnp
from jax import lax
from jax.experimental import pallas as pl
from jax.experimental.pallas import tpu as pltpu

F32 = jnp.float32
BF16 = jnp.bfloat16

D = 1024
CH = 128
NG = 8
HD = 64
NQ = 16
NKV = 4
KVW = NKV * HD
DFF = 4 * D
EPS = 1e-6
IN_W = 5632
SEG = (0, 1024, 2048, 3072, 3328, 3584, 4608, 5632)
ROPE_HALF = 8
Q_SCALE = HD ** -0.5

LR, B1, B2, AEPS, WD, STEP = 0.001, 0.9, 0.999, 1e-08, 0.01, 10

VMEM_PHYSICAL = 64 * 1024 * 1024
VMEM_LIMIT = 60 * 1024 * 1024
MESH = pl.DeviceIdType.MESH

TOKEN_TILE = 512
WGRAD_TILES = {"w_ff_out": (512, 1024), "w_ff_in": (2048, 2048), "w_in": (2048, IN_W // 2)}

_GELU_C0 = 0.7978845608028654
_GELU_C1 = 0.044715


def _cparams(sem=None, vmem=None):
    kw = dict(vmem_limit_bytes=VMEM_LIMIT if vmem is None else vmem)
    if sem is not None:
        kw["dimension_semantics"] = sem
    return pltpu.CompilerParams(**kw)


def _resident(shape):
    nd = len(shape)
    return pl.BlockSpec(shape, lambda *_: (0,) * nd, pipeline_mode=pl.Buffered(1))


def _const(shape):
    nd = len(shape)
    return pl.BlockSpec(shape, lambda *_: (0,) * nd)


def _rows(tm, w):
    return pl.BlockSpec((tm, w), lambda i: (i, 0))


class _Exchange:
    def __init__(self, ins, outs, aliases, scratch, start, finish):
        self.ins, self.outs, self.aliases, self.scratch = list(ins), list(outs), dict(aliases), list(scratch)
        self.start, self.finish = start, finish


def _both(a, b):
    na, ma, sa = len(a.ins), len(a.outs), len(a.scratch)

    def start(ci, co, cs):
        a.start(ci[:na], co[:ma], cs[:sa])
        b.start(ci[na:], co[ma:], cs[sa:])

    def finish(ci, co, cs):
        a.finish(ci[:na], co[:ma], cs[:sa])
        b.finish(ci[na:], co[ma:], cs[sa:])

    aliases = {**a.aliases, **{na + i: ma + j for i, j in b.aliases.items()}}
    return _Exchange(a.ins + b.ins, a.outs + b.outs, aliases, a.scratch + b.scratch, start, finish)


def _call(body, args, *, name, grid, in_specs, out_specs, out_shape, scratch_shapes=(), sem=None, comm=None, vmem=None):
    single = not isinstance(out_shape, (list, tuple))
    out_shape = [out_shape] if single else list(out_shape)
    out_specs = [out_specs] if single else list(out_specs)
    if comm is None:
        res = pl.pallas_call(body, name=name, grid=grid, in_specs=list(in_specs), out_specs=out_specs,
                             out_shape=out_shape, scratch_shapes=list(scratch_shapes),
                             compiler_params=_cparams(sem, vmem))(*args)
        return (res[0] if single else res), []
    n_in, n_out, n_scr = len(args), len(out_shape), len(scratch_shapes)
    nci, nco = len(comm.ins), len(comm.outs)
    steps = 1
    for g in grid:
        steps *= g

    def hosted(*refs):
        a, ci = refs[:n_in], refs[n_in:n_in + nci]
        o, co = refs[n_in + nci:n_in + nci + n_out], refs[n_in + nci + n_out:n_in + nci + n_out + nco]
        rest = refs[n_in + nci + n_out + nco:]
        scr, cs = rest[:n_scr], rest[n_scr:]
        step = pl.program_id(0)
        for d in range(1, len(grid)):
            step = step * grid[d] + pl.program_id(d)

        @pl.when(step == 0)
        def _():
            comm.start(ci, co, cs)

        body(*a, *o, *scr)

        @pl.when(step == steps - 1)
        def _():
            comm.finish(ci, co, cs)

    res = pl.pallas_call(
        hosted, name=name, grid=grid, in_specs=list(in_specs) + [ANY] * nci, out_specs=out_specs + [ANY] * nco,
        out_shape=out_shape + comm.outs, scratch_shapes=list(scratch_shapes) + comm.scratch,
        input_output_aliases={n_in + i: n_out + j for i, j in comm.aliases.items()},
        compiler_params=_cparams(("arbitrary",) * len(grid), vmem),
    )(*args, *comm.ins)
    own = res[:n_out]
    return (own[0] if single else own), list(res[n_out:])


def _run(comm, name):
    nci = len(comm.ins)

    def body(*refs):
        ci, co, cs = refs[:nci], refs[nci:nci + len(comm.outs)], refs[nci + len(comm.outs):]
        comm.start(ci, co, cs)
        comm.finish(ci, co, cs)

    return pl.pallas_call(
        body, name=name, in_specs=[ANY] * nci, out_specs=[ANY] * len(comm.outs), out_shape=comm.outs,
        scratch_shapes=comm.scratch, input_output_aliases=comm.aliases,
        compiler_params=pltpu.CompilerParams(vmem_limit_bytes=VMEM_LIMIT),
    )(*comm.ins)


def _gelu(x):
    x2 = x * x
    t = jnp.tanh(x * (_GELU_C0 + (_GELU_C0 * _GELU_C1) * x2))
    hx = 0.5 * x
    return hx + hx * t, (t, x2, hx)


def _gelu_grad(parts):
    t, x2, hx = parts
    return (0.5 + 0.5 * t) + hx * (1.0 - t * t) * (_GELU_C0 + (3.0 * _GELU_C0 * _GELU_C1) * x2)


def _sigmoid(x):
    return 1.0 / (1.0 + jnp.exp(-x))


def _rms_hat(x):
    r = lax.rsqrt(jnp.mean(x * x, axis=-1, keepdims=True) + EPS)
    return x * r, r


def _rms_bwd(xhat, r, g, dout):
    dg = jnp.sum(dout * xhat, axis=0, keepdims=True)
    dy = dout * g
    dx = r * (dy - xhat * jnp.mean(dy * xhat, axis=-1, keepdims=True))
    return dx, dg


def _dot(a, b):
    return jnp.dot(a, b, preferred_element_type=F32)


def _dot_nt(a, b):
    return lax.dot_general(a, b, (((1,), (1,)), ((), ())), preferred_element_type=F32)


def _dot_tn(a, b):
    return lax.dot_general(a, b, (((0,), (0,)), ((), ())), preferred_element_type=F32)


def _rope(blk, c, s1, s2):
    return blk * c + pltpu.roll(blk, CH - ROPE_HALF, 1) * s1 + pltpu.roll(blk, ROPE_HALF, 1) * s2


def _rope_t(blk, c, s1, s2):
    return blk * c + pltpu.roll(blk * s1, ROPE_HALF, 1) + pltpu.roll(blk * s2, CH - ROPE_HALF, 1)


def _inproj(x, g1, w_in, rc, rs1, rs2, tm, comm=None):
    T = x.shape[0]

    def body(x_ref, g_ref, w_ref, c_ref, s1_ref, s2_ref,
             h_ref, u_ref, v_ref, q_ref, k_ref, va_ref, ga_ref, gb_ref):
        xhat, _ = _rms_hat(x_ref[...])
        h = (xhat * g_ref[...]).astype(BF16)
        h_ref[...] = h
        uv = _dot(h, w_ref[:, SEG[0]:SEG[2]])
        u_ref[...] = uv[:, :D]
        v_ref[...] = uv[:, D:]
        c, s1, s2 = c_ref[...], s1_ref[...], s2_ref[...]
        qkv = _dot(h, w_ref[:, SEG[2]:SEG[5]])
        for p in range(D // CH):
            blk = _rope(qkv[:, CH * p:CH * (p + 1)], c, s1, s2) * Q_SCALE
            q_ref[:, CH * p:CH * (p + 1)] = blk.astype(BF16)
        for p in range(KVW // CH):
            k_ref[:, CH * p:CH * (p + 1)] = _rope(qkv[:, D + CH * p:D + CH * (p + 1)], c, s1, s2).astype(BF16)
        va_ref[...] = qkv[:, D + KVW:].astype(BF16)
        gates = _dot(h, w_ref[:, SEG[5]:SEG[7]]).astype(BF16)
        ga_ref[...] = gates[:, :D]
        gb_ref[...] = gates[:, D:]

    sd = jax.ShapeDtypeStruct
    return _call(
        body, (x, g1, w_in, rc, rs1, rs2), name="inproj_fwd", grid=(T // tm,),
        in_specs=[_rows(tm, D), _const((1, D)), _resident((D, IN_W)), _rows(tm, CH), _rows(tm, CH), _rows(tm, CH)],
        out_specs=[_rows(tm, D), _rows(tm, D), _rows(tm, D), _rows(tm, D), _rows(tm, KVW), _rows(tm, KVW),
                   _rows(tm, D), _rows(tm, D)],
        out_shape=[sd((T, D), BF16), sd((T, D), F32), sd((T, D), F32), sd((T, D), BF16), sd((T, KVW), BF16),
                   sd((T, KVW), BF16), sd((T, D), BF16), sd((T, D), BF16)],
        sem=("parallel",), comm=comm)


def _sgu_common(u, vs, lng, lnb, ws_ref, bfull):
    nc = u.shape[0] // CH
    ug, tu = _gelu(u)
    vg, tv = _gelu(vs)
    mu = jnp.mean(vg, axis=-1, keepdims=True)
    xc = vg - mu
    rstd = lax.rsqrt(jnp.mean(xc * xc, axis=-1, keepdims=True) + EPS)
    vhat = xc * rstd
    vnb = (vhat * lng + lnb).astype(BF16)
    tri = lax.broadcasted_iota(jnp.int32, (CH, CH), 0) >= lax.broadcasted_iota(jnp.int32, (CH, CH), 1)
    wts, rhss, mixed = [], [], []
    for g in range(NG):
        wt = jnp.where(tri, ws_ref[g], 0.0).astype(BF16)
        rhs = jnp.concatenate([vnb[CH * c:CH * (c + 1), CH * g:CH * (g + 1)] for c in range(nc)], axis=1)
        mix = _dot(wt, rhs)
        wts.append(wt)
        rhss.append(rhs)
        mixed.append([mix[:, CH * c:CH * (c + 1)] + bfull[:, CH * g:CH * (g + 1)] for c in range(nc)])
    return nc, ug, tu, tv, rstd, vhat, tri, wts, rhss, mixed


def _sgu_fwd(u, vs, lng, lnb, ws, bfull, tm, comm=None):
    T = u.shape[0]

    def body(u_ref, v_ref, lng_ref, lnb_ref, ws_ref, bf_ref, a_ref):
        nc, ug, _, _, _, _, _, _, _, mixed = _sgu_common(
            u_ref[...], v_ref[...], lng_ref[...], lnb_ref[...], ws_ref, bf_ref[...])
        mixed_all = jnp.concatenate(
            [jnp.concatenate([mixed[g][c] for g in range(NG)], axis=1) for c in range(nc)], axis=0)
        a_ref[...] = (ug * mixed_all).astype(BF16)

    return _call(
        body, (u, vs, lng, lnb, ws, bfull), name="sgu_fwd", grid=(T // tm,),
        in_specs=[_rows(tm, D), _rows(tm, D), _const((1, D)), _const((1, D)), _const((NG, CH, CH)), _const((CH, D))],
        out_specs=_rows(tm, D), out_shape=jax.ShapeDtypeStruct((T, D), BF16), sem=("parallel",), comm=comm)


def _sgu_bwd(u, vs, da, lng, lnb, ws, bfull, tm, comm=None):
    T = u.shape[0]
    nsteps = T // tm

    def body(u_ref, v_ref, da_ref, lng_ref, lnb_ref, ws_ref, bf_ref,
             du_ref, dv_ref, dws_ref, dbs_ref, dlg_ref, dlb_ref, db_ref):
        i = pl.program_id(0)
        u, vs, da, lng = u_ref[...], v_ref[...], da_ref[...], lng_ref[...]
        nc, ug, tu, tv, rstd, vhat, tri, wts, rhss, mixed = _sgu_common(u, vs, lng, lnb_ref[...], ws_ref, bf_ref[...])

        @pl.when(i == 0)
        def _():
            dws_ref[...] = jnp.zeros_like(dws_ref)
            db_ref[...] = jnp.zeros_like(db_ref)
            dlg_ref[...] = jnp.zeros_like(dlg_ref)
            dlb_ref[...] = jnp.zeros_like(dlb_ref)

        mixed_all = jnp.concatenate(
            [jnp.concatenate([mixed[g][c] for g in range(NG)], axis=1) for c in range(nc)], axis=0)
        du_ref[...] = (da * mixed_all * _gelu_grad(tu)).astype(BF16)
        dmixed = da * ug
        dvn_cols = []
        for g in range(NG):
            dmix = [dmixed[CH * c:CH * (c + 1), CH * g:CH * (g + 1)] for c in range(nc)]
            db_ref[:, CH * g:CH * (g + 1)] += functools.reduce(lambda a, b: a + b, dmix)
            dm = jnp.concatenate(dmix, axis=1).astype(BF16)
            dws_ref[g] += _dot_nt(dm, rhss[g])
            dvn_cols.append(_dot_tn(wts[g], dm))
        dvn = jnp.concatenate(
            [jnp.concatenate([dvn_cols[g][:, CH * c:CH * (c + 1)] for g in range(NG)], axis=1) for c in range(nc)],
            axis=0)
        dlg_ref[...] += jnp.sum(dvn * vhat, axis=0, keepdims=True)
        dlb_ref[...] += jnp.sum(dvn, axis=0, keepdims=True)
        dvh = dvn * lng
        dvg = rstd * (dvh - jnp.mean(dvh, axis=-1, keepdims=True)
                      - vhat * jnp.mean(dvh * vhat, axis=-1, keepdims=True))
        dv_ref[...] = (dvg * _gelu_grad(tv)).astype(BF16)

        @pl.when(i == nsteps - 1)
        def _():
            for g in range(NG):
                dws_ref[g] = jnp.where(tri, dws_ref[g], 0.0)
                dbs_ref[g:g + 1, :] = jnp.sum(db_ref[:, CH * g:CH * (g + 1)].T, axis=0, keepdims=True)

    sd = jax.ShapeDtypeStruct
    return _call(
        body, (u, vs, da, lng, lnb, ws, bfull), name="sgu_bwd", grid=(nsteps,),
        in_specs=[_rows(tm, D), _rows(tm, D), _rows(tm, D), _const((1, D)), _const((1, D)), _const((NG, CH, CH)),
                  _const((CH, D))],
        out_specs=[_rows(tm, D), _rows(tm, D), _const((NG, CH, CH)), _const((NG, CH)), _const((1, D)), _const((1, D))],
        out_shape=[sd((T, D), BF16), sd((T, D), BF16), sd((NG, CH, CH), F32), sd((NG, CH), F32), sd((1, D), F32),
                   sd((1, D), F32)],
        scratch_shapes=[pltpu.VMEM((CH, D), F32)], sem=("arbitrary",), comm=comm)


def _pair_layout(prev, cur, grp):
    j, half = grp // 2, grp % 2
    blk = jnp.concatenate([prev[:, CH * j:CH * (j + 1)], cur[:, CH * j:CH * (j + 1)]], axis=0).astype(F32)
    lo = lax.broadcasted_iota(jnp.int32, blk.shape, 1) < HD
    rolled = pltpu.roll(blk, HD, 1)
    even = jnp.where(lo, blk if half == 0 else rolled, 0.0)
    odd = jnp.where(lo, 0.0, rolled if half == 0 else blk)
    return jnp.concatenate([even, odd], axis=0).astype(BF16)


def _attn_mask(n):
    qi = lax.broadcasted_iota(jnp.int32, (CH, 2 * CH), 0)
    kc = lax.broadcasted_iota(jnp.int32, (CH, 2 * CH), 1)
    ok = (kc > qi) & (kc <= qi + CH) & ((kc >= CH) | (n > 0))
    return jnp.concatenate([ok, ok], axis=1)


def _softmax_sink(s, sink):
    m = jnp.maximum(jnp.max(s, axis=-1, keepdims=True), sink)
    p = jnp.exp(s - m)
    ps = jnp.exp(sink - m)
    inv = 1.0 / (jnp.sum(p, axis=-1, keepdims=True) + ps)
    return p * inv, ps * inv


QUERY_BLOCKS_PER_STEP = 2


def _attn_fwd(q, k, va, sinks, comm=None):
    T = q.shape[0]
    nblk = QUERY_BLOCKS_PER_STEP
    nsteps = T // (nblk * CH)
    npairs = D // CH

    def body(sk_ref, q_ref, kp_ref, kc_ref, vp_ref, vc_ref, o_ref):
        n = pl.program_id(0)
        even_lanes = lax.broadcasted_iota(jnp.int32, (CH, CH), 1) < HD
        ks = [kp_ref[...]] + [kc_ref[CH * b:CH * (b + 1)] for b in range(nblk)]
        vs = [vp_ref[...]] + [vc_ref[CH * b:CH * (b + 1)] for b in range(nblk)]
        masks = [_attn_mask(nblk * n)] + [_attn_mask(1)] * (nblk - 1)
        kks = [[_pair_layout(ks[b], ks[b + 1], grp) for grp in range(NKV)] for b in range(nblk)]
        vvs = [[_pair_layout(vs[b], vs[b + 1], grp) for grp in range(NKV)] for b in range(nblk)]
        work = [(b, p) for b in range(nblk) for p in range(npairs)]

        def scores(i):
            b, p = work[i]
            return _dot_nt(q_ref[CH * b:CH * (b + 1), CH * p:CH * (p + 1)], kks[b][p // 2])

        def unnormalised(s, sink):
            m = jnp.maximum(jnp.max(s, axis=-1, keepdims=True), sink)
            p = jnp.exp(s - m)
            return p, 1.0 / (jnp.sum(p, axis=-1, keepdims=True) + jnp.exp(sink - m))

        def value_product(i):
            b, p = work[i]
            pr, ie, io = probs[i]
            return _dot(pr, vvs[b][p // 2]) * jnp.where(even_lanes, ie, io)

        ahead = 3
        outs, probs = [], []
        pending = [scores(i) for i in range(ahead)]
        for i, (b, p) in enumerate(work):
            s = jnp.where(masks[b], pending.pop(0), -1e30)
            if i + ahead < len(work):
                pending.append(scores(i + ahead))
            pe, ie = unnormalised(s[:, :2 * CH], sk_ref[2 * p])
            po, io = unnormalised(s[:, 2 * CH:], sk_ref[2 * p + 1])
            probs.append((jnp.concatenate([pe, po], axis=1).astype(BF16), ie, io))
            if i >= 1:
                outs.append(value_product(i - 1))
        outs.append(value_product(len(work) - 1))
        for b in range(nblk):
            o_ref[CH * b:CH * (b + 1), :] = jnp.concatenate(outs[npairs * b:npairs * (b + 1)], axis=1).astype(BF16)

    prev = lambda n: (jnp.maximum(nblk * n - 1, 0), 0)
    cur = lambda n: (n, 0)
    return _call(
        body, (sinks, q, k, k, va, va), name="attn_fwd", grid=(nsteps,),
        in_specs=[pl.BlockSpec(memory_space=pltpu.SMEM), pl.BlockSpec((nblk * CH, D), cur),
                  pl.BlockSpec((CH, KVW), prev), pl.BlockSpec((nblk * CH, KVW), cur),
                  pl.BlockSpec((CH, KVW), prev), pl.BlockSpec((nblk * CH, KVW), cur)],
        out_specs=pl.BlockSpec((nblk * CH, D), cur), out_shape=jax.ShapeDtypeStruct((T, D), BF16),
        sem=("parallel",), comm=comm)


def _attn_bwd(q, k, va, datt, sinks, rc, rs1, rs2, comm=None):
    T = q.shape[0]
    nb = T // CH

    def body(sk_ref, q_ref, kp_ref, kc_ref, vp_ref, vc_ref, do_ref, cq_ref, s1q_ref, s2q_ref, ck_ref, s1k_ref, s2k_ref,
             dq_ref, dk_ref, dv_ref, dsk_ref, kcar, vcar):
        n = pl.program_id(0)

        @pl.when(n == 0)
        def _():
            kcar[...] = jnp.zeros_like(kcar)
            vcar[...] = jnp.zeros_like(vcar)
            dsk_ref[...] = jnp.zeros_like(dsk_ref)

        def flush(kprev, vprev):
            ck, s1k, s2k = ck_ref[...], s1k_ref[...], s2k_ref[...]
            for j in range(KVW // CH):
                sl = slice(CH * j, CH * (j + 1))
                dk_ref[:, sl] = _rope_t(kcar[:, sl] + kprev[:, sl], ck, s1k, s2k).astype(BF16)
                dv_ref[:, sl] = (vcar[:, sl] + vprev[:, sl]).astype(BF16)

        @pl.when(n < nb)
        def _():
            mask = _attn_mask(n)
            kp, kc, vp, vc = kp_ref[...], kc_ref[...], vp_ref[...], vc_ref[...]
            cq, s1q, s2q = cq_ref[...], s1q_ref[...], s2q_ref[...]
            lane = lax.broadcasted_iota(jnp.int32, (1, CH), 1)
            dsk = jnp.zeros((1, CH), F32)
            npairs = D // CH
            kks = [_pair_layout(kp, kc, grp) for grp in range(NKV)]
            vvs = [_pair_layout(vp, vc, grp) for grp in range(NKV)]
            qs = [q_ref[:, CH * p:CH * (p + 1)] for p in range(npairs)]
            dos = [do_ref[:, CH * p:CH * (p + 1)].astype(BF16) for p in range(npairs)]

            def first(p):
                return _dot_nt(qs[p], kks[p // 2]), _dot_nt(dos[p], vvs[p // 2])

            def last(p, ds, pb):
                return (_rope_t(_dot(ds, kks[p // 2]), cq, s1q, s2q) * Q_SCALE, _dot_tn(qs[p], ds), _dot_tn(dos[p], pb))

            ahead = 2
            pending = [first(p) for p in range(ahead)]
            mids, ends = [], []
            for p in range(npairs):
                s, dp = pending.pop(0)
                s = jnp.where(mask, s, -1e30)
                if p + ahead < npairs:
                    pending.append(first(p + ahead))
                ds_parts, p_parts = [], []
                for par in range(2):
                    sl = slice(2 * CH * par, 2 * CH * (par + 1))
                    pr, psink = _softmax_sink(s[:, sl], sk_ref[2 * p + par])
                    delta = jnp.sum(pr * dp[:, sl], axis=-1, keepdims=True)
                    ds_parts.append(pr * (dp[:, sl] - delta))
                    p_parts.append(pr)
                    tot = -jnp.sum(psink * delta, axis=0, keepdims=True)
                    dsk = dsk + jnp.where(lane == 2 * p + par, tot, 0.0)
                mids.append((jnp.concatenate(ds_parts, axis=1).astype(BF16), jnp.concatenate(p_parts, axis=1).astype(BF16)))
                if p >= 1:
                    ends.append(last(p - 1, *mids[p - 1]))
            ends.append(last(npairs - 1, *mids[-1]))
            dq_cols = [e[0] for e in ends]
            def fold(i):
                rows = []
                for grp in range(NKV):
                    acc = ends[2 * grp][i] + ends[2 * grp + 1][i]
                    rows.append(acc[:HD, :2 * CH] + acc[HD:, 2 * CH:])
                return jnp.concatenate(rows, axis=0).T

            dkf, dvf = fold(1), fold(2)
            dq_ref[...] = jnp.concatenate(dq_cols, axis=1).astype(BF16)
            dsk_ref[...] += dsk
            flush(dkf[:CH], dvf[:CH])
            kcar[...] = dkf[CH:]
            vcar[...] = dvf[CH:]

        @pl.when(n == nb)
        def _():
            z = jnp.zeros((CH, KVW), F32)
            flush(z, z)

    last = nb - 1
    cur = lambda n: (jnp.minimum(n, last), 0)
    prev = lambda n: (jnp.clip(n - 1, 0, last), 0)
    sd = jax.ShapeDtypeStruct
    return _call(
        body, (sinks, q, k, k, va, va, datt, rc, rs1, rs2, rc, rs1, rs2), name="attn_bwd", grid=(nb + 1,),
        in_specs=[pl.BlockSpec(memory_space=pltpu.SMEM), pl.BlockSpec((CH, D), cur),
                  pl.BlockSpec((CH, KVW), prev), pl.BlockSpec((CH, KVW), cur),
                  pl.BlockSpec((CH, KVW), prev), pl.BlockSpec((CH, KVW), cur),
                  pl.BlockSpec((CH, D), cur),
                  pl.BlockSpec((CH, CH), cur), pl.BlockSpec((CH, CH), cur), pl.BlockSpec((CH, CH), cur),
                  pl.BlockSpec((CH, CH), prev), pl.BlockSpec((CH, CH), prev), pl.BlockSpec((CH, CH), prev)],
        out_specs=[pl.BlockSpec((CH, D), cur), pl.BlockSpec((CH, KVW), prev), pl.BlockSpec((CH, KVW), prev),
                   _const((1, CH))],
        out_shape=[sd((T, D), BF16), sd((T, KVW), BF16), sd((T, KVW), BF16), sd((1, CH), F32)],
        scratch_shapes=[pltpu.VMEM((CH, KVW), F32), pltpu.VMEM((CH, KVW), F32)], sem=("arbitrary",), comm=comm)


def _merge_fwd(a, att, ga, gb, x, w_a, w_b, w_o, g2, tm, comm=None):
    T = x.shape[0]

    def body(a_ref, att_ref, ga_ref, gb_ref, x_ref, wa_ref, wb_ref, wo_ref, g_ref,
             pa_ref, pb_ref, mg_ref, mix_ref, x1_ref):
        pa = _dot(a_ref[...], wa_ref[...])
        pb = _dot(att_ref[...], wb_ref[...])
        pa_ref[...] = pa.astype(BF16)
        pb_ref[...] = pb.astype(BF16)
        merged = (_sigmoid(ga_ref[...].astype(F32)) * pa + _sigmoid(gb_ref[...].astype(F32)) * pb).astype(BF16)
        mg_ref[...] = merged
        mix = _dot(merged, wo_ref[...])
        mix_ref[...] = mix
        mhat, _ = _rms_hat(mix)
        x1_ref[...] = x_ref[...] + mhat * g_ref[...]

    sd = jax.ShapeDtypeStruct
    return _call(
        body, (a, att, ga, gb, x, w_a, w_b, w_o, g2), name="merge_fwd", grid=(T // tm,),
        in_specs=[_rows(tm, D)] * 5 + [_resident((D, D))] * 3 + [_const((1, D))],
        out_specs=[_rows(tm, D)] * 5,
        out_shape=[sd((T, D), BF16), sd((T, D), BF16), sd((T, D), BF16), sd((T, D), F32), sd((T, D), F32)],
        sem=("parallel",), comm=comm)


def _sgu_merge_fwd(u, vs, lng, lnb, ws, bfull, att, ga, gb, x, w_a, w_b, w_o, g2, tm, comm=None):
    T = x.shape[0]

    def body(u_ref, v_ref, lng_ref, lnb_ref, ws_ref, bf_ref, att_ref, ga_ref, gb_ref, x_ref, wa_ref, wb_ref, wo_ref, g_ref,
             a_ref, pa_ref, pb_ref, mg_ref, mix_ref, x1_ref):
        pb = _dot(att_ref[...], wb_ref[...])
        nc, ug, _, _, _, _, _, _, _, mixed = _sgu_common(
            u_ref[...], v_ref[...], lng_ref[...], lnb_ref[...], ws_ref, bf_ref[...])
        mixed_all = jnp.concatenate(
            [jnp.concatenate([mixed[g][c] for g in range(NG)], axis=1) for c in range(nc)], axis=0)
        a = (ug * mixed_all).astype(BF16)
        a_ref[...] = a
        pa = _dot(a, wa_ref[...])
        pa_ref[...] = pa.astype(BF16)
        pb_ref[...] = pb.astype(BF16)
        merged = (_sigmoid(ga_ref[...].astype(F32)) * pa + _sigmoid(gb_ref[...].astype(F32)) * pb).astype(BF16)
        mg_ref[...] = merged
        mix = _dot(merged, wo_ref[...])
        mix_ref[...] = mix
        mhat, _ = _rms_hat(mix)
        x1_ref[...] = x_ref[...] + mhat * g_ref[...]

    sd = jax.ShapeDtypeStruct
    return _call(
        body, (u, vs, lng, lnb, ws, bfull, att, ga, gb, x, w_a, w_b, w_o, g2), name="sgu_merge_fwd", grid=(T // tm,),
        in_specs=[_rows(tm, D), _rows(tm, D), _const((1, D)), _const((1, D)), _const((NG, CH, CH)), _const((CH, D))]
        + [_rows(tm, D)] * 4 + [_resident((D, D))] * 3 + [_const((1, D))],
        out_specs=[_rows(tm, D)] * 6,
        out_shape=[sd((T, D), BF16)] * 4 + [sd((T, D), F32)] * 2,
        sem=("parallel",), comm=comm, vmem=VMEM_PHYSICAL)


def _merge_bwd(dx1, mix, ga, gb, pa, pb, a, att, merged, w_a, w_b, w_o, g2, tm, comm=None):
    T = dx1.shape[0]
    nsteps = T // tm

    def body(dx1_ref, mix_ref, ga_ref, gb_ref, pa_ref, pb_ref, a_ref, att_ref, mg_ref, wa_ref, wb_ref, wo_ref, g_ref,
             dga_ref, dgb_ref, da_ref, datt_ref, dg_ref, dwa_ref, dwb_ref, dwo_ref, acc, sem):
        i = pl.program_id(0)

        @pl.when(i == 0)
        def _():
            dg_ref[...] = jnp.zeros_like(dg_ref)
            acc[...] = jnp.zeros_like(acc)

        mhat, r = _rms_hat(mix_ref[...])
        dmix, dg = _rms_bwd(mhat, r, g_ref[...], dx1_ref[...])
        dg_ref[...] += dg
        dmix = dmix.astype(BF16)
        dmerged = _dot_nt(dmix, wo_ref[...])
        sa = _sigmoid(ga_ref[...].astype(F32))
        sb = _sigmoid(gb_ref[...].astype(F32))
        dao = (dmerged * sa).astype(BF16)
        dbo = (dmerged * sb).astype(BF16)
        dga_ref[...] = (dmerged * pa_ref[...].astype(F32) * (sa * (1.0 - sa))).astype(BF16)
        dgb_ref[...] = (dmerged * pb_ref[...].astype(F32) * (sb * (1.0 - sb))).astype(BF16)
        da_ref[...] = _dot_nt(dao, wa_ref[...])
        datt_ref[...] = _dot_nt(dbo, wb_ref[...]).astype(BF16)
        acc[0] += _dot_tn(a_ref[...], dao)
        acc[1] += _dot_tn(att_ref[...], dbo)
        acc[2] += _dot_tn(mg_ref[...], dmix)

        @pl.when(i == nsteps - 1)
        def _():
            outs = [pltpu.make_async_copy(acc.at[j], ref, sem.at[j]) for j, ref in enumerate((dwa_ref, dwb_ref, dwo_ref))]
            for cp in outs:
                cp.start()
            for cp in outs:
                cp.wait()

    sd = jax.ShapeDtypeStruct
    return _call(
        body, (dx1, mix, ga, gb, pa, pb, a, att, merged, w_a, w_b, w_o, g2), name="merge_bwd", grid=(nsteps,),
        in_specs=[_rows(tm, D)] * 9 + [_resident((D, D))] * 3 + [_const((1, D))],
        out_specs=[_rows(tm, D)] * 4 + [_const((1, D))] + [ANY] * 3,
        out_shape=[sd((T, D), BF16), sd((T, D), BF16), sd((T, D), F32), sd((T, D), BF16), sd((1, D), F32)]
        + [sd((D, D), F32)] * 3,
        scratch_shapes=[pltpu.VMEM((3, D, D), F32), _dma_sems(3)], sem=("arbitrary",), comm=comm)


def _ffn(x1, target, w1, w2, g3, g4, tm):
    T = x1.shape[0]

    def body(x_ref, t_ref, w1_ref, w2_ref, g3_ref, g4_ref,
             hf_ref, f2_ref, dff_ref, df1_ref, dx_ref, ls_ref, dg3_ref, dg4_ref):
        @pl.when(pl.program_id(0) == 0)
        def _():
            ls_ref[...] = jnp.zeros_like(ls_ref)
            dg3_ref[...] = jnp.zeros_like(dg3_ref)
            dg4_ref[...] = jnp.zeros_like(dg4_ref)

        x = x_ref[...]
        g3, g4 = g3_ref[...], g4_ref[...]
        xhat, r3 = _rms_hat(x)
        hf = (xhat * g3).astype(BF16)
        hf_ref[...] = hf
        rl = jnp.maximum(_dot(hf, w1_ref[...]), 0.0)
        f2 = (rl * rl).astype(BF16)
        f2_ref[...] = f2
        fhat, r4 = _rms_hat(_dot(f2, w2_ref[...]))
        err = x + fhat * g4 - t_ref[...]
        ls_ref[...] += jnp.sum(err * err, axis=0, keepdims=True)
        dy = err * (1.0 / D)
        dff, dg4 = _rms_bwd(fhat, r4, g4, dy)
        dg4_ref[...] += dg4
        dff = dff.astype(BF16)
        dff_ref[...] = dff
        df1 = (_dot_nt(dff, w2_ref[...]) * (2.0 * rl)).astype(BF16)
        df1_ref[...] = df1
        dxn, dg3 = _rms_bwd(xhat, r3, g3, _dot_nt(df1, w1_ref[...]))
        dg3_ref[...] += dg3
        dx_ref[...] = dy + dxn

    sd = jax.ShapeDtypeStruct
    return pl.pallas_call(
        body, name="ffn_fwd_bwd", grid=(T // tm,),
        in_specs=[_rows(tm, D), _rows(tm, D), _resident((D, DFF)), _resident((DFF, D)), _const((1, D)), _const((1, D))],
        out_specs=[_rows(tm, D), _rows(tm, DFF), _rows(tm, D), _rows(tm, DFF), _rows(tm, D), _const((1, D)),
                   _const((1, D)), _const((1, D))],
        out_shape=[sd((T, D), BF16), sd((T, DFF), BF16), sd((T, D), BF16), sd((T, DFF), BF16), sd((T, D), F32),
                   sd((1, D), F32), sd((1, D), F32), sd((1, D), F32)],
        compiler_params=pltpu.CompilerParams(vmem_limit_bytes=VMEM_PHYSICAL, dimension_semantics=("arbitrary",)),
    )(x1, target, w1, w2, g3, g4)


def _inproj_bwd(parts, x, dx1, g1, w_in, tm, comm=None):
    T = x.shape[0]
    widths = [p.shape[1] for p in parts]
    offs = [sum(widths[:i]) for i in range(len(widths) + 1)]
    assert offs[-1] == IN_W

    def body(*refs):
        n = len(parts)
        prefs = refs[:n]
        x_ref, dx1_ref, g_ref, w_ref, dx_ref, dp_ref, dg_ref = refs[n:]

        @pl.when(pl.program_id(0) == 0)
        def _():
            dg_ref[...] = jnp.zeros_like(dg_ref)

        for i in range(n):
            dp_ref[:, offs[i]:offs[i + 1]] = prefs[i][...]
        dh = _dot_nt(dp_ref[...], w_ref[...])
        xhat, r = _rms_hat(x_ref[...])
        dxn, dg = _rms_bwd(xhat, r, g_ref[...], dh)
        dg_ref[...] += dg
        dx_ref[...] = dx1_ref[...] + dxn

    sd = jax.ShapeDtypeStruct
    return _call(
        body, (*parts, x, dx1, g1, w_in), name="inproj_bwd", grid=(T // tm,),
        in_specs=[_rows(tm, w) for w in widths] + [_rows(tm, D), _rows(tm, D), _const((1, D)), _resident((D, IN_W))],
        out_specs=[_rows(tm, D), _rows(tm, IN_W), _const((1, D))],
        out_shape=[sd((T, D), F32), sd((T, IN_W), BF16), sd((1, D), F32)], sem=("arbitrary",), comm=comm)


def _sgu_inproj_bwd(u, vs, da, lng, lnb, ws, bfull, parts, x, dx1, g1, w_in, tm):
    T = x.shape[0]
    nsteps = T // tm
    widths = [p.shape[1] for p in parts]
    offs = [2 * D + sum(widths[:i]) for i in range(len(widths) + 1)]
    assert offs[-1] == IN_W
    n = len(parts)

    def body(*refs):
        u_ref, v_ref, da_ref, lng_ref, lnb_ref, ws_ref, bf_ref = refs[:7]
        prefs = refs[7:7 + n]
        x_ref, dx1_ref, g_ref, w_ref = refs[7 + n:11 + n]
        dx_ref, dp_ref, dg_ref, dws_ref, dbs_ref, dlg_ref, dlb_ref, db_ref = refs[11 + n:]
        i = pl.program_id(0)

        @pl.when(i == 0)
        def _():
            for ref in (dg_ref, dws_ref, db_ref, dlg_ref, dlb_ref):
                ref[...] = jnp.zeros_like(ref)

        for j in range(n):
            dp_ref[:, offs[j]:offs[j + 1]] = prefs[j][...]
        cut = 2 * D + (IN_W - 2 * D) // 2
        dh_a = _dot_nt(dp_ref[:, 2 * D:cut], w_ref[:, 2 * D:cut])
        u, vs, da, lng = u_ref[...], v_ref[...], da_ref[...], lng_ref[...]
        nc, ug, tu, tv, rstd, vhat, tri, wts, rhss, mixed = _sgu_common(u, vs, lng, lnb_ref[...], ws_ref, bf_ref[...])
        mixed_all = jnp.concatenate(
            [jnp.concatenate([mixed[g][c] for g in range(NG)], axis=1) for c in range(nc)], axis=0)
        dp_ref[:, :D] = (da * mixed_all * _gelu_grad(tu)).astype(BF16)
        dmixed = da * ug
        dvn_cols = []
        for g in range(NG):
            dmix = [dmixed[CH * c:CH * (c + 1), CH * g:CH * (g + 1)] for c in range(nc)]
            db_ref[:, CH * g:CH * (g + 1)] += functools.reduce(lambda a, b: a + b, dmix)
            dm = jnp.concatenate(dmix, axis=1).astype(BF16)
            dws_ref[g] += _dot_nt(dm, rhss[g])
            dvn_cols.append(_dot_tn(wts[g], dm))
        dh_b = _dot_nt(dp_ref[:, cut:], w_ref[:, cut:])
        dvn = jnp.concatenate(
            [jnp.concatenate([dvn_cols[g][:, CH * c:CH * (c + 1)] for g in range(NG)], axis=1) for c in range(nc)],
            axis=0)
        dlg_ref[...] += jnp.sum(dvn * vhat, axis=0, keepdims=True)
        dlb_ref[...] += jnp.sum(dvn, axis=0, keepdims=True)
        dvh = dvn * lng
        dvg = rstd * (dvh - jnp.mean(dvh, axis=-1, keepdims=True)
                      - vhat * jnp.mean(dvh * vhat, axis=-1, keepdims=True))
        dp_ref[:, D:2 * D] = (dvg * _gelu_grad(tv)).astype(BF16)

        dh = dh_a + dh_b + _dot_nt(dp_ref[:, :2 * D], w_ref[:, :2 * D])
        xhat, r = _rms_hat(x_ref[...])
        dxn, dg = _rms_bwd(xhat, r, g_ref[...], dh)
        dg_ref[...] += dg
        dx_ref[...] = dx1_ref[...] + dxn

        @pl.when(i == nsteps - 1)
        def _():
            for g in range(NG):
                dws_ref[g] = jnp.where(tri, dws_ref[g], 0.0)
                dbs_ref[g:g + 1, :] = jnp.sum(db_ref[:, CH * g:CH * (g + 1)].T, axis=0, keepdims=True)

    sd = jax.ShapeDtypeStruct
    outs, _ = _call(
        body, (u, vs, da, lng, lnb, ws, bfull, *parts, x, dx1, g1, w_in), name="sgu_inproj_bwd", grid=(nsteps,),
        in_specs=[_rows(tm, D), _rows(tm, D), _rows(tm, D), _const((1, D)), _const((1, D)), _const((NG, CH, CH)),
                  _const((CH, D))] + [_rows(tm, w) for w in widths]
        + [_rows(tm, D), _rows(tm, D), _const((1, D)), _resident((D, IN_W))],
        out_specs=[_rows(tm, D), _rows(tm, IN_W), _const((1, D)), _const((NG, CH, CH)), _const((NG, CH)), _const((1, D)),
                   _const((1, D))],
        out_shape=[sd((T, D), F32), sd((T, IN_W), BF16), sd((1, D), F32), sd((NG, CH, CH), F32), sd((NG, CH), F32),
                   sd((1, D), F32), sd((1, D), F32)],
        scratch_shapes=[pltpu.VMEM((CH, D), F32)], sem=("arbitrary",), vmem=VMEM_PHYSICAL)
    return outs


def _wgrad(a, g, tn, tm, name, comm=None, vmem=None):
    T, K = a.shape
    N = g.shape[1]

    def body(a_ref, g_ref, o_ref):
        @pl.when(pl.program_id(1) == 0)
        def _():
            o_ref[...] = jnp.zeros_like(o_ref)

        o_ref[...] += _dot_tn(a_ref[...], g_ref[...])

    return _call(
        body, (a, g), name=name, grid=(N // tn, T // tm),
        in_specs=[pl.BlockSpec((tm, K), lambda j, t: (t, 0)), pl.BlockSpec((tm, tn), lambda j, t: (t, j))],
        out_specs=pl.BlockSpec((K, tn), lambda j, t: (0, j)),
        out_shape=jax.ShapeDtypeStruct((K, N), F32), sem=("parallel", "arbitrary"), comm=comm, vmem=vmem)


def _adamw(ws, gs, ms, vs, trs, name):
    n = len(ws)
    walk = _Walk(w.shape[0] // tr for w, tr in zip(ws, trs))
    bc1 = 1.0 / (1.0 - B1 ** STEP)
    bc2 = 1.0 / (1.0 - B2 ** STEP)

    def body(*refs):
        i = pl.program_id(0)
        for k in range(n):
            mine = tuple(refs[j * n + k] for j in range(8))

            @pl.when(walk.mine(k, i))
            def _(mine=mine):
                w_ref, g_ref, m_ref, v_ref, go_ref, d_ref, nm_ref, nv_ref = mine
                g = g_ref[...]
                go_ref[...] = g
                m = B1 * m_ref[...] + (1.0 - B1) * g
                v = B2 * v_ref[...] + (1.0 - B2) * (g * g)
                nm_ref[...] = m
                nv_ref[...] = v
                d_ref[...] = -LR * ((m * bc1) / (jnp.sqrt(v * bc2) + AEPS) + WD * w_ref[...])

    def spec(k):
        return pl.BlockSpec((trs[k], ws[k].shape[1]), lambda i: (walk.tile(k, i), 0))

    specs = [spec(k) for k in range(n)]
    res = pl.pallas_call(
        body, name=name, grid=(walk.steps,), in_specs=specs * 4, out_specs=specs * 4,
        out_shape=[jax.ShapeDtypeStruct(w.shape, F32) for w in ws] * 4,
        compiler_params=_cparams(("arbitrary",)),
    )(*ws, *gs, *ms, *vs)
    return [tuple(res[j * n + k] for j in range(4)) for k in range(n)]


BIG = (("col", (D, IN_W)), ("row", (D, D)), ("row", (D, D)), ("row", (D, D)), ("col", (D, DFF)), ("row", (DFF, D)))
NBIG = len(BIG)
ANY = pl.BlockSpec(memory_space=pl.ANY)


def _shard_shape(kind, shape):
    R, C = shape
    return (R, C // 4) if kind == "col" else (R // 4, C)


def _half_shape(kind, shape):
    R, C = shape
    return (R // 2, C) if kind == "col" else (R, C // 2)


def _piece_shape(kind, shape):
    R, C = shape
    return (R // 2, C // 4) if kind == "col" else (R // 4, C // 2)


def _own_region(ref, kind, shape, s):
    R, C = shape
    return ref.at[:, pl.ds(s * (C // 4), C // 4)] if kind == "col" else ref.at[pl.ds(s * (R // 4), R // 4), :]


def _ag_region(ref, kind, shape, s, hc):
    R, C = shape
    if kind == "col":
        return ref.at[pl.ds(hc * (R // 2), R // 2), pl.ds(s * (C // 4), C // 4)]
    return ref.at[pl.ds(s * (R // 4) + hc * (R // 8), R // 8), :]


def _ag_shard_half(ref, kind, shape, hc):
    R, C = shape
    return ref.at[pl.ds(hc * (R // 2), R // 2), :] if kind == "col" else ref.at[pl.ds(hc * (R // 8), R // 8), :]


def _grad_half(ref, kind, shape, hc):
    R, C = shape
    return ref.at[pl.ds(hc * (R // 2), R // 2), :] if kind == "col" else ref.at[:, pl.ds(hc * (C // 2), C // 2)]


def _half_piece(ref, kind, shape, s):
    R, C = shape
    return ref.at[:, pl.ds(s * (C // 4), C // 4)] if kind == "col" else ref.at[pl.ds(s * (R // 4), R // 4), :]


def _place():
    x, y, c = lax.axis_index("x"), lax.axis_index("y"), lax.axis_index("c")
    chips = [(1 - x, y), (x, 1 - y), (1 - x, 1 - y)]
    return x, y, c, chips


def _rcopy(src, dst, ssem, rsem, dev):
    return pltpu.make_async_remote_copy(src_ref=src, dst_ref=dst, send_sem=ssem, recv_sem=rsem,
                                        device_id=dev, device_id_type=MESH)


def _dma_sems(n):
    return pltpu.SemaphoreType.DMA((n,))


def _x_gather_ici(shards, ws):
    n = len(ws)
    specs = [BIG[w] for w in ws]

    def place():
        x, y, c, chips = _place()
        return c, chips, 2 * x + y

    def sends(sh, full, sc):
        c, chips, me_s = place()
        return [_rcopy(_ag_shard_half(sh[i], kind, shape, c), _ag_region(full[i], kind, shape, me_s, c),
                       sc[0].at[3 * i + j], sc[1].at[3 * i + j], (cx, cy, c))
                for i, (kind, shape) in enumerate(specs) for j, (cx, cy) in enumerate(chips)]

    def start(sh, full, sc):
        for i in range(n):
            pltpu.make_async_copy(sh[i], sc[4 + i], sc[2].at[i]).start()
        for cp in sends(sh, full, sc):
            cp.start()

    def finish(sh, full, sc):
        c, chips, me_s = place()
        stores = []
        for i, (kind, shape) in enumerate(specs):
            pltpu.make_async_copy(sh[i], sc[4 + i], sc[2].at[i]).wait()
            st = pltpu.make_async_copy(sc[4 + i], _own_region(full[i], kind, shape, me_s), sc[3].at[i])
            st.start()
            stores.append(st)
        for i, (kind, shape) in enumerate(specs):
            for j, (cx, cy) in enumerate(chips):
                reg = _ag_region(full[i], kind, shape, 2 * cx + cy, c)
                _rcopy(reg, reg, sc[0].at[3 * i + j], sc[1].at[3 * i + j], (cx, cy, c)).wait_recv()
        for cp in sends(sh, full, sc):
            cp.wait_send()
        for st in stores:
            st.wait()

    return _Exchange(
        shards, [jax.ShapeDtypeStruct(shape, BF16) for _, shape in specs], {},
        [_dma_sems(3 * n), _dma_sems(3 * n), _dma_sems(n), _dma_sems(n)]
        + [pltpu.VMEM(_shard_shape(k, s), BF16) for k, s in specs], start, finish)


def _x_gather_d2d(wholes, ws):
    specs = [BIG[w] for w in ws]
    n = len(ws)

    def copies(full, sc, mine):
        x, y, c, chips = _place()
        hc = c if mine else 1 - c
        return [_rcopy(reg, reg, sc[0].at[3 * i + j], sc[1].at[3 * i + j], (x, y, 1 - c))
                for i, (kind, shape) in enumerate(specs) for j, (cx, cy) in enumerate(chips)
                for reg in [_ag_region(full[i], kind, shape, 2 * cx + cy, hc)]]

    def start(_, full, sc):
        for cp in copies(full, sc, True):
            cp.start()

    def finish(_, full, sc):
        for cp in copies(full, sc, False):
            cp.wait_recv()
        for cp in copies(full, sc, True):
            cp.wait_send()

    return _Exchange(wholes, [jax.ShapeDtypeStruct(shape, BF16) for _, shape in specs], {i: i for i in range(n)},
                     [_dma_sems(3 * n), _dma_sems(3 * n)], start, finish)


def _x_grads_sibling(grads, ws):
    specs = [BIG[w] for w in ws]
    n = len(ws)

    def copies(g, got, sc):
        x, y, c, _ = _place()
        return [_rcopy(_grad_half(g[i], kind, shape, 1 - c), got[i], sc[0].at[i], sc[1].at[i], (x, y, 1 - c))
                for i, (kind, shape) in enumerate(specs)]

    def start(g, got, sc):
        for cp in copies(g, got, sc):
            cp.start()

    def finish(g, got, sc):
        for cp in copies(g, got, sc):
            cp.wait_recv()
        for cp in copies(g, got, sc):
            cp.wait_send()

    return _Exchange(grads, [jax.ShapeDtypeStruct(_half_shape(k, s), F32) for k, s in specs], {},
                     [_dma_sems(n), _dma_sems(n)], start, finish)


def _x_grads_chips(sums_bf, ws):
    specs = [BIG[w] for w in ws]
    n = len(ws)

    def copies(s16, got, sc):
        x, y, c, chips = _place()
        return [_rcopy(_half_piece(s16[i], kind, shape, 2 * cx + cy), got[i].at[j],
                       sc[0].at[3 * i + j], sc[1].at[3 * i + j], (cx, cy, c))
                for i, (kind, shape) in enumerate(specs) for j, (cx, cy) in enumerate(chips)]

    def start(s16, got, sc):
        for cp in copies(s16, got, sc):
            cp.start()

    def finish(s16, got, sc):
        for cp in copies(s16, got, sc):
            cp.wait_recv()
        for cp in copies(s16, got, sc):
            cp.wait_send()

    return _Exchange(sums_bf, [jax.ShapeDtypeStruct((3,) + _piece_shape(k, s), BF16) for k, s in specs], {},
                     [_dma_sems(3 * n), _dma_sems(3 * n)], start, finish)


def _shard_half(ref, kind, shape, hc):
    sr, sc = _shard_shape(kind, shape)
    return ref.at[pl.ds(hc * (sr // 2), sr // 2), :] if kind == "col" else ref.at[:, pl.ds(hc * (sc // 2), sc // 2)]


def _x_grads_share(shard_grads, ws):
    specs = [BIG[w] for w in ws]
    n = len(ws)

    def copies(g, sc, mine):
        x, y, c, _ = _place()
        hc = c if mine else 1 - c
        return [_rcopy(part, part, sc[0].at[i], sc[1].at[i], (x, y, 1 - c))
                for i, (kind, shape) in enumerate(specs) for part in [_shard_half(g[i], kind, shape, hc)]]

    def start(_, g, sc):
        for cp in copies(g, sc, True):
            cp.start()

    def finish(_, g, sc):
        for cp in copies(g, sc, False):
            cp.wait_recv()
        for cp in copies(g, sc, True):
            cp.wait_send()

    return _Exchange(shard_grads, [jax.ShapeDtypeStruct(_shard_shape(k, s), F32) for k, s in specs],
                     {i: i for i in range(n)}, [_dma_sems(n), _dma_sems(n)], start, finish)


ADD_BLOCK_BYTES = 4 * 1024 * 1024


def _add_rows(rows, cols, n_arrays):
    limit = ADD_BLOCK_BYTES // (1 if n_arrays == 1 else 4)
    r = rows
    while r > 64 and r * cols * 4 > limit:
        r //= 2
    return r


class _Walk:
    def __init__(self, tiles):
        self.tiles = list(tiles)
        self.starts = [sum(self.tiles[:k]) for k in range(len(self.tiles))]
        self.steps = sum(self.tiles)

    def tile(self, k, i):
        return jnp.clip(i - self.starts[k], 0, self.tiles[k] - 1)

    def mine(self, k, i):
        return (i >= self.starts[k]) & (i < self.starts[k] + self.tiles[k])


def _add_halves(place, gs, gots, kinds, name):
    n = len(gs)
    halves = [_half_shape(kind, g.shape) for g, kind in zip(gs, kinds)]
    rows = [_add_rows(hr, hc, n) for hr, hc in halves]
    walk = _Walk(hr // r for (hr, _), r in zip(halves, rows))

    def body(p_ref, *refs):
        i = pl.program_id(0)
        for k in range(n):
            g_ref, b_ref, s_ref, sb_ref = (refs[j * n + k] for j in range(4))

            @pl.when(walk.mine(k, i))
            def _(g_ref=g_ref, b_ref=b_ref, s_ref=s_ref, sb_ref=sb_ref):
                s = g_ref[...] + b_ref[...]
                s_ref[...] = s
                sb_ref[...] = s.astype(BF16)

    def g_spec(k):
        if kinds[k] == "col":
            return pl.BlockSpec((rows[k], gs[k].shape[1]), lambda i, p: (p[0] * walk.tiles[k] + walk.tile(k, i), 0))
        return pl.BlockSpec((rows[k], halves[k][1]), lambda i, p: (walk.tile(k, i), p[0]))

    def spec(k):
        return pl.BlockSpec((rows[k], halves[k][1]), lambda i, p: (walk.tile(k, i), 0))

    specs = [spec(k) for k in range(n)]
    res = pl.pallas_call(
        body, name=name,
        grid_spec=pltpu.PrefetchScalarGridSpec(num_scalar_prefetch=1, grid=(walk.steps,),
                                               in_specs=[g_spec(k) for k in range(n)] + specs, out_specs=specs + specs),
        out_shape=[jax.ShapeDtypeStruct(h, F32) for h in halves] + [jax.ShapeDtypeStruct(h, BF16) for h in halves],
        compiler_params=_cparams(("arbitrary",)),
    )(place, *gs, *gots)
    return [(res[k], res[n + k]) for k in range(n)]


def _add_pieces(place, halves, gots, specs_big, name):
    n = len(halves)
    pieces = [_piece_shape(kind, shape) for kind, shape in specs_big]
    rows = [_add_rows(pr, pc, n) for pr, pc in pieces]
    walk = _Walk(pr // r for (pr, _), r in zip(pieces, rows))

    def body(p_ref, *refs):
        i = pl.program_id(0)
        for k in range(n):
            m_ref, g_ref, o_ref = (refs[j * n + k] for j in range(3))

            @pl.when(walk.mine(k, i))
            def _(m_ref=m_ref, g_ref=g_ref, o_ref=o_ref):
                acc = m_ref[...]
                for j in range(3):
                    acc = acc + g_ref[j].astype(F32)
                o_ref[...] = acc

    def m_spec(k):
        if specs_big[k][0] == "col":
            return pl.BlockSpec((rows[k], pieces[k][1]), lambda i, p: (walk.tile(k, i), p[1]))
        return pl.BlockSpec((rows[k], pieces[k][1]), lambda i, p: (p[1] * walk.tiles[k] + walk.tile(k, i), 0))

    def got_spec(k):
        return pl.BlockSpec((3, rows[k], pieces[k][1]), lambda i, p: (0, walk.tile(k, i), 0))

    def o_spec(k):
        if specs_big[k][0] == "col":
            return pl.BlockSpec((rows[k], pieces[k][1]), lambda i, p: (p[0] * walk.tiles[k] + walk.tile(k, i), 0))
        return pl.BlockSpec((rows[k], pieces[k][1]), lambda i, p: (walk.tile(k, i), p[0]))

    return pl.pallas_call(
        body, name=name,
        grid_spec=pltpu.PrefetchScalarGridSpec(
            num_scalar_prefetch=1, grid=(walk.steps,),
            in_specs=[m_spec(k) for k in range(n)] + [got_spec(k) for k in range(n)],
            out_specs=[o_spec(k) for k in range(n)]),
        out_shape=[jax.ShapeDtypeStruct(_shard_shape(kind, shape), F32) for kind, shape in specs_big],
        compiler_params=_cparams(("arbitrary",)),
    )(place, *halves, *gots)


SMALL_ROWS = 1024 + 8 * 8 + 8


def _x_small_all_reduce(p):
    def parts(p_ref, sc):
        slots, ssem, rsem = sc[0], sc[2], sc[3]
        x, y, c = lax.axis_index("x"), lax.axis_index("y"), lax.axis_index("c")
        me = 4 * x + 2 * y + c
        out = []
        for r in range(1, 8):
            bx, by, bc = (r >> 2) & 1, (r >> 1) & 1, r & 1
            tgt = (1 - x if bx else x, 1 - y if by else y, 1 - c if bc else c)
            send = _rcopy(p_ref, slots.at[me], ssem.at[r - 1], rsem.at[r - 1], tgt)
            src = 4 * tgt[0] + 2 * tgt[1] + tgt[2]
            recv = _rcopy(p_ref, slots.at[src], ssem.at[r - 1], rsem.at[r - 1], tgt)
            out.append((send, recv))
        return me, out

    def start(ins, outs, sc):
        me, cps = parts(ins[0], sc)
        pltpu.make_async_copy(ins[0], sc[0].at[me], sc[4].at[0]).start()
        for send, _ in cps:
            send.start()

    def finish(ins, outs, sc):
        me, cps = parts(ins[0], sc)
        pltpu.make_async_copy(ins[0], sc[0].at[me], sc[4].at[0]).wait()
        for _, recv in cps:
            recv.wait_recv()
        acc = sc[0][0]
        for d in range(1, 8):
            acc = acc + sc[0][d]
        sc[1][...] = acc
        back = pltpu.make_async_copy(sc[1], outs[0], sc[4].at[1])
        back.start()
        for send, _ in cps:
            send.wait_send()
        back.wait()

    return _Exchange([p], [jax.ShapeDtypeStruct((SMALL_ROWS, CH), F32)], {},
                     [pltpu.VMEM((8, SMALL_ROWS, CH), F32), pltpu.VMEM((SMALL_ROWS, CH), F32), _dma_sems(7), _dma_sems(7),
                      _dma_sems(2)], start, finish)


def _rope_tables(positions, comm=None):
    T = positions.shape[0]
    inv_freq = 500000.0 ** (-jnp.arange(0, 2 * ROPE_HALF, 2, dtype=F32) / (2 * ROPE_HALF))
    head = jnp.concatenate([inv_freq, inv_freq, jnp.zeros((HD - 2 * ROPE_HALF,), F32)])
    lane_freq = jnp.concatenate([head, head])[None, :]
    pos = jnp.broadcast_to(positions.astype(F32)[:, None], (T, CH))
    tm = min(1024, T)

    def body(p_ref, f_ref, c_ref, s1_ref, s2_ref):
        ang = p_ref[...] * f_ref[...]
        sin = jnp.sin(ang)
        first = (lax.broadcasted_iota(jnp.int32, ang.shape, 1) % HD) < ROPE_HALF
        c_ref[...] = jnp.cos(ang)
        s1_ref[...] = jnp.where(first, -sin, 0.0)
        s2_ref[...] = jnp.where(first, 0.0, sin)

    return _call(body, (pos, lane_freq), name="rope_tables", grid=(T // tm,),
                 in_specs=[_rows(tm, CH), _const((1, CH))], out_specs=[_rows(tm, CH)] * 3,
                 out_shape=[jax.ShapeDtypeStruct((T, CH), F32)] * 3, sem=("parallel",), comm=comm)


BIG_NAMES = ("w_in", "w_a", "w_b", "w_o", "w_ff_in", "w_ff_out")
SMALL_NAMES = ("w_spatial", "ln_v_gain", "ln_v_bias", "b_spatial", "sinks", "norm_mix_pre", "norm_mix_post",
               "norm_ff_pre", "norm_ff_post")
WEIGHT_ORDER = ("w_in", "ln_v_gain", "ln_v_bias", "w_spatial", "b_spatial", "sinks", "w_a", "w_b", "w_o",
                "norm_mix_pre", "norm_mix_post", "w_ff_in", "w_ff_out", "norm_ff_pre", "norm_ff_post")


def _pack_small(d, loss_sums=None):
    parts = []
    for n in SMALL_NAMES:
        flat = d[n].reshape(-1)
        pad = (-flat.shape[0]) % (8 * CH)
        parts.append(jnp.pad(flat, (0, pad)).reshape(-1, CH))
    parts.append(jnp.zeros((8, CH), F32) if loss_sums is None else loss_sums.reshape(8, CH))
    return jnp.concatenate(parts, axis=0)


def _unpack_small(p, like):
    out, row = {}, 0
    for n in SMALL_NAMES:
        size = like[n].size
        rows = -(-size // (8 * CH)) * 8
        out[n] = p[row:row + rows].reshape(-1)[:size].reshape(like[n].shape)
        row += rows
    return out


def kernel(x, positions, w_in, ln_v_gain, ln_v_bias, w_spatial, b_spatial, sinks, w_a, w_b, w_o, norm_mix_pre, norm_mix_post, w_ff_in, w_ff_out, norm_ff_pre, norm_ff_post, loss_target, m_w_in, m_ln_v_gain, m_ln_v_bias, m_w_spatial, m_b_spatial, m_sinks, m_w_a, m_w_b, m_w_o, m_norm_mix_pre, m_norm_mix_post, m_w_ff_in, m_w_ff_out, m_norm_ff_pre, m_norm_ff_post, v_w_in, v_ln_v_gain, v_ln_v_bias, v_w_spatial, v_b_spatial, v_sinks, v_w_a, v_w_b, v_w_o, v_norm_mix_pre, v_norm_mix_post, v_w_ff_in, v_w_ff_out, v_norm_ff_pre, v_norm_ff_post):
    w = dict(w_in=w_in, ln_v_gain=ln_v_gain, ln_v_bias=ln_v_bias, w_spatial=w_spatial, b_spatial=b_spatial, sinks=sinks,
             w_a=w_a, w_b=w_b, w_o=w_o, norm_mix_pre=norm_mix_pre, norm_mix_post=norm_mix_post, w_ff_in=w_ff_in,
             w_ff_out=w_ff_out, norm_ff_pre=norm_ff_pre, norm_ff_post=norm_ff_post)
    m = dict(w_in=m_w_in, ln_v_gain=m_ln_v_gain, ln_v_bias=m_ln_v_bias, w_spatial=m_w_spatial, b_spatial=m_b_spatial,
             sinks=m_sinks, w_a=m_w_a, w_b=m_w_b, w_o=m_w_o, norm_mix_pre=m_norm_mix_pre, norm_mix_post=m_norm_mix_post,
             w_ff_in=m_w_ff_in, w_ff_out=m_w_ff_out, norm_ff_pre=m_norm_ff_pre, norm_ff_post=m_norm_ff_post)
    v = dict(w_in=v_w_in, ln_v_gain=v_ln_v_gain, ln_v_bias=v_ln_v_bias, w_spatial=v_w_spatial, b_spatial=v_b_spatial,
             sinks=v_sinks, w_a=v_w_a, w_b=v_w_b, w_o=v_w_o, norm_mix_pre=v_norm_mix_pre, norm_mix_post=v_norm_mix_post,
             w_ff_in=v_w_ff_in, w_ff_out=v_w_ff_out, norm_ff_pre=v_norm_ff_pre, norm_ff_post=v_norm_ff_post)

    FIRST, REST = (0,), tuple(range(1, NBIG))
    shards = [w[n][0].astype(BF16) for n in BIG_NAMES]
    place = jnp.stack([lax.axis_index("c"), 2 * lax.axis_index("x") + lax.axis_index("y")]).astype(jnp.int32)
    xs, target = x[0], loss_target[0]
    T = xs.shape[0]
    tile = min(TOKEN_TILE, T)
    wtiles = {n: dict(tm=min(tm, T), tn=tn) for n, (tm, tn) in WGRAD_TILES.items()}
    g1, g2, g3, g4 = norm_mix_pre, norm_mix_post, norm_ff_pre, norm_ff_post
    w_sp, snk = w_spatial[0], sinks[0]
    MIX, FF = (1, 2, 3), (4, 5)
    bfull = jnp.repeat(b_spatial[0].T, CH, axis=1)

    def reduce_tail(ws, grads, got):
        tag = "_".join(BIG_NAMES[k] for k in ws)
        sums = _add_halves(place, grads, got, [BIG[k][0] for k in ws], name="grad_add_sibling_" + tag)
        return sums, _x_grads_chips([s[1] for s in sums], ws)

    def reduce_end(ws, sums, pieces):
        tag = "_".join(BIG_NAMES[k] for k in ws)
        return _add_pieces(place, [s[0] for s in sums], pieces, [BIG[k] for k in ws], name="grad_add_chips_" + tag)

    (rc, rs1, rs2), w_in_part = _rope_tables(positions[0], comm=_x_gather_ici(shards[:1], FIRST))
    w_in_b = _run(_x_gather_d2d(w_in_part, FIRST), "gather_w_in_d2d")[0]
    EARLY, FF_OUT = (1, 2, 3, 4), (5,)
    (h, u, vs, q, k, va, ga, gb), early_part = _inproj(xs, g1, w_in_b, rc, rs1, rs2, tm=tile,
                                                      comm=_x_gather_ici(shards[1:5], EARLY))
    att, (w_a_b, w_b_b, w_o_b, w_ff_in_b, ffo_part) = _attn_fwd(
        q, k, va, snk, comm=_both(_x_gather_d2d(early_part, EARLY), _x_gather_ici(shards[5:], FF_OUT)))
    (a, pa, pb, merged, mix, x1), (w_ff_out_b,) = _sgu_merge_fwd(
        u, vs, ln_v_gain, ln_v_bias, w_sp, bfull, att, ga, gb, xs, w_a_b, w_b_b, w_o_b, g2, tm=tile,
        comm=_x_gather_d2d([ffo_part], FF_OUT))
    hf, f2, dff, df1, dx1, lsum, dg3, dg4 = _ffn(x1, target, w_ff_in_b, w_ff_out_b, g3, g4, tm=tile)

    dw_ff_out, _ = _wgrad(f2, dff, name="wgrad_ff_out", **wtiles["w_ff_out"])
    dw_ff_in, _ = _wgrad(hf, df1, name="wgrad_ff_in", **wtiles["w_ff_in"])
    grads_ff = [dw_ff_in, dw_ff_out]
    (dga, dgb, da, datt, dg2, dw_a, dw_b, dw_o), got_ff = _merge_bwd(
        dx1, mix, ga, gb, pa, pb, a, att, merged, w_a_b, w_b_b, w_o_b, g2, tm=tile, comm=_x_grads_sibling(grads_ff, FF))
    grads_mix = [dw_a, dw_b, dw_o]
    sums_ff, ff_to_chips = reduce_tail(FF, grads_ff, got_ff)
    (dq, dk, dva, dsk), (*pieces_ff, got_a, got_b, got_o) = _attn_bwd(
        q, k, va, datt, snk, rc, rs1, rs2, comm=_both(ff_to_chips, _x_grads_sibling(grads_mix, MIX)))
    partial_ff = reduce_end(FF, sums_ff, pieces_ff)
    sums_mix, mix_to_chips = reduce_tail(MIX, grads_mix, [got_a, got_b, got_o])
    dx, dproj, dg1, dws, dbs, dlg, dlb = _sgu_inproj_bwd(
        u, vs, da, ln_v_gain, ln_v_bias, w_sp, bfull, [dq, dk, dva, dga, dgb], xs, dx1, g1, w_in_b, tm=tile // 2)
    small = dict(ln_v_gain=dlg, ln_v_bias=dlb, w_spatial=dws, b_spatial=dbs, sinks=dsk[:, :NQ],
                 norm_mix_pre=dg1, norm_mix_post=dg2, norm_ff_pre=dg3, norm_ff_post=dg4)
    dw_in, (gs, shard_ff_in, shard_ff_out, *pieces_mix) = _wgrad(
        h, dproj, name="wgrad_in", vmem=VMEM_PHYSICAL, **wtiles["w_in"],
        comm=_both(_both(_x_small_all_reduce(_pack_small(small, lsum)), _x_grads_share(partial_ff, FF)), mix_to_chips))
    partial_mix = reduce_end(MIX, sums_mix, pieces_mix)
    got_in = _run(_x_grads_sibling([dw_in], FIRST), "grads_in_to_sibling")
    sums_in, to_chips = reduce_tail(FIRST, [dw_in], got_in)
    partial_in = reduce_end(FIRST, sums_in, _run(to_chips, "grads_in_to_chips"))
    g_in, *shard_mix = _run(_x_grads_share(list(partial_in) + list(partial_mix), FIRST + MIX), "grads_in_mix_share")
    shard_rest = list(shard_mix) + [shard_ff_in, shard_ff_out]

    loss = 0.5 * jnp.sum(gs[SMALL_ROWS - 8:]) / D
    grad, delta, new_m, new_v = {}, {}, {}, {}
    for n, g in zip(BIG_NAMES, [g_in] + list(shard_rest)):
        (g_, d_, m_, v_), = _adamw([w[n][0]], [g], [m[n][0]], [v[n][0]], [256], name="adamw_" + n)
        grad[n], delta[n], new_m[n], new_v[n] = g_[None], d_[None], m_[None], v_[None]
    (gs, ds, ms, vs), = _adamw([_pack_small(w)], [gs], [_pack_small(m)], [_pack_small(v)], [SMALL_ROWS], name="adamw_small")
    for packed, dst in ((gs, grad), (ds, delta), (ms, new_m), (vs, new_v)):
        dst.update(_unpack_small(packed, w))

    outs = [loss, dx[None]]
    for group in (grad, delta, new_m, new_v):
        outs.extend(group[n] for n in WEIGHT_ORDER)
    return tuple(outs)
```

```python
import functools

import jax
import jax.numpy as jnp
from jax import lax
from jax.experimental import pallas as pl
from jax.experimental.pallas import tpu as pltpu

F32 = jnp.float32
BF16 = jnp.bfloat16

D = 1024
CH = 128
NG = 8
HD = 64
NQ = 16
NKV = 4
KVW = NKV * HD
DFF = 4 * D
EPS = 1e-6
IN_W = 5632
SEG = (0, 1024, 2048, 3072, 3328, 3584, 4608, 5632)
ROPE_HALF = 8
Q_SCALE = HD ** -0.5

LR, B1, B2, AEPS, WD, STEP = 0.001, 0.9, 0.999, 1e-08, 0.01, 10

VMEM_PHYSICAL = 64 * 1024 * 1024
VMEM_LIMIT = 60 * 1024 * 1024
MESH = pl.DeviceIdType.MESH

TOKEN_TILE = 512
WGRAD_TILES = {"w_ff_out": (512, 1024), "w_ff_in": (2048, 2048), "w_in": (2048, IN_W // 2)}

_GELU_C0 = 0.7978845608028654
_GELU_C1 = 0.044715


def _cparams(sem=None, vmem=None):
    kw = dict(vmem_limit_bytes=VMEM_LIMIT if vmem is None else vmem)
    if sem is not None:
        kw["dimension_semantics"] = sem
    return pltpu.CompilerParams(**kw)


def _resident(shape):
    nd = len(shape)
    return pl.BlockSpec(shape, lambda *_: (0,) * nd, pipeline_mode=pl.Buffered(1))


def _const(shape):
    nd = len(shape)
    return pl.BlockSpec(shape, lambda *_: (0,) * nd)


def _rows(tm, w):
    return pl.BlockSpec((tm, w), lambda i: (i, 0))


class _Exchange:
    def __init__(self, ins, outs, aliases, scratch, start, finish):
        self.ins, self.outs, self.aliases, self.scratch = list(ins), list(outs), dict(aliases), list(scratch)
        self.start, self.finish = start, finish


def _both(a, b):
    na, ma, sa = len(a.ins), len(a.outs), len(a.scratch)

    def start(ci, co, cs):
        a.start(ci[:na], co[:ma], cs[:sa])
        b.start(ci[na:], co[ma:], cs[sa:])

    def finish(ci, co, cs):
        a.finish(ci[:na], co[:ma], cs[:sa])
        b.finish(ci[na:], co[ma:], cs[sa:])

    aliases = {**a.aliases, **{na + i: ma + j for i, j in b.aliases.items()}}
    return _Exchange(a.ins + b.ins, a.outs + b.outs, aliases, a.scratch + b.scratch, start, finish)


def _call(body, args, *, name, grid, in_specs, out_specs, out_shape, scratch_shapes=(), sem=None, comm=None, vmem=None):
    single = not isinstance(out_shape, (list, tuple))
    out_shape = [out_shape] if single else list(out_shape)
    out_specs = [out_specs] if single else list(out_specs)
    if comm is None:
        res = pl.pallas_call(body, name=name, grid=grid, in_specs=list(in_specs), out_specs=out_specs,
                             out_shape=out_shape, scratch_shapes=list(scratch_shapes),
                             compiler_params=_cparams(sem, vmem))(*args)
        return (res[0] if single else res), []
    n_in, n_out, n_scr = len(args), len(out_shape), len(scratch_shapes)
    nci, nco = len(comm.ins), len(comm.outs)
    steps = 1
    for g in grid:
        steps *= g

    def hosted(*refs):
        a, ci = refs[:n_in], refs[n_in:n_in + nci]
        o, co = refs[n_in + nci:n_in + nci + n_out], refs[n_in + nci + n_out:n_in + nci + n_out + nco]
        rest = refs[n_in + nci + n_out + nco:]
        scr, cs = rest[:n_scr], rest[n_scr:]
        step = pl.program_id(0)
        for d in range(1, len(grid)):
            step = step * grid[d] + pl.program_id(d)

        @pl.when(step == 0)
        def _():
            comm.start(ci, co, cs)

        body(*a, *o, *scr)

        @pl.when(step == steps - 1)
        def _():
            comm.finish(ci, co, cs)

    res = pl.pallas_call(
        hosted, name=name, grid=grid, in_specs=list(in_specs) + [ANY] * nci, out_specs=out_specs + [ANY] * nco,
        out_shape=out_shape + comm.outs, scratch_shapes=list(scratch_shapes) + comm.scratch,
        input_output_aliases={n_in + i: n_out + j for i, j in comm.aliases.items()},
        compiler_params=_cparams(("arbitrary",) * len(grid), vmem),
    )(*args, *comm.ins)
    own = res[:n_out]
    return (own[0] if single else own), list(res[n_out:])


def _run(comm, name):
    nci = len(comm.ins)

    def body(*refs):
        ci, co, cs = refs[:nci], refs[nci:nci + len(comm.outs)], refs[nci + len(comm.outs):]
        comm.start(ci, co, cs)
        comm.finish(ci, co, cs)

    return pl.pallas_call(
        body, name=name, in_specs=[ANY] * nci, out_specs=[ANY] * len(comm.outs), out_shape=comm.outs,
        scratch_shapes=comm.scratch, input_output_aliases=comm.aliases,
        compiler_params=pltpu.CompilerParams(vmem_limit_bytes=VMEM_LIMIT),
    )(*comm.ins)


def _gelu(x):
    x2 = x * x
    t = jnp.tanh(x * (_GELU_C0 + (_GELU_C0 * _GELU_C1) * x2))
    hx = 0.5 * x
    return hx + hx * t, (t, x2, hx)


def _gelu_grad(parts):
    t, x2, hx = parts
    return (0.5 + 0.5 * t) + hx * (1.0 - t * t) * (_GELU_C0 + (3.0 * _GELU_C0 * _GELU_C1) * x2)


def _sigmoid(x):
    return 1.0 / (1.0 + jnp.exp(-x))


def _rms_hat(x):
    r = lax.rsqrt(jnp.mean(x * x, axis=-1, keepdims=True) + EPS)
    return x * r, r


def _rms_bwd(xhat, r, g, dout):
    dg = jnp.sum(dout * xhat, axis=0, keepdims=True)
    dy = dout * g
    dx = r * (dy - xhat * jnp.mean(dy * xhat, axis=-1, keepdims=True))
    return dx, dg


def _dot(a, b):
    return jnp.dot(a, b, preferred_element_type=F32)


def _dot_nt(a, b):
    return lax.dot_general(a, b, (((1,), (1,)), ((), ())), preferred_element_type=F32)


def _dot_tn(a, b):
    return lax.dot_general(a, b, (((0,), (0,)), ((), ())), preferred_element_type=F32)


def _rope(blk, c, s1, s2):
    return blk * c + pltpu.roll(blk, CH - ROPE_HALF, 1) * s1 + pltpu.roll(blk, ROPE_HALF, 1) * s2


def _rope_t(blk, c, s1, s2):
    return blk * c + pltpu.roll(blk * s1, ROPE_HALF, 1) + pltpu.roll(blk * s2, CH - ROPE_HALF, 1)


def _inproj(x, g1, w_in, rc, rs1, rs2, tm, comm=None):
    T = x.shape[0]

    def body(x_ref, g_ref, w_ref, c_ref, s1_ref, s2_ref,
             h_ref, u_ref, v_ref, q_ref, k_ref, va_ref, ga_ref, gb_ref):
        xhat, _ = _rms_hat(x_ref[...])
        h = (xhat * g_ref[...]).astype(BF16)
        h_ref[...] = h
        uv = _dot(h, w_ref[:, SEG[0]:SEG[2]])
        u_ref[...] = uv[:, :D]
        v_ref[...] = uv[:, D:]
        c, s1, s2 = c_ref[...], s1_ref[...], s2_ref[...]
        qkv = _dot(h, w_ref[:, SEG[2]:SEG[5]])
        for p in range(D // CH):
            blk = _rope(qkv[:, CH * p:CH * (p + 1)], c, s1, s2) * Q_SCALE
            q_ref[:, CH * p:CH * (p + 1)] = blk.astype(BF16)
        for p in range(KVW // CH):
            k_ref[:, CH * p:CH * (p + 1)] = _rope(qkv[:, D + CH * p:D + CH * (p + 1)], c, s1, s2).astype(BF16)
        va_ref[...] = qkv[:, D + KVW:].astype(BF16)
        gates = _dot(h, w_ref[:, SEG[5]:SEG[7]]).astype(BF16)
        ga_ref[...] = gates[:, :D]
        gb_ref[...] = gates[:, D:]

    sd = jax.ShapeDtypeStruct
    return _call(
        body, (x, g1, w_in, rc, rs1, rs2), name="inproj_fwd", grid=(T // tm,),
        in_specs=[_rows(tm, D), _const((1, D)), _resident((D, IN_W)), _rows(tm, CH), _rows(tm, CH), _rows(tm, CH)],
        out_specs=[_rows(tm, D), _rows(tm, D), _rows(tm, D), _rows(tm, D), _rows(tm, KVW), _rows(tm, KVW),
                   _rows(tm, D), _rows(tm, D)],
        out_shape=[sd((T, D), BF16), sd((T, D), F32), sd((T, D), F32), sd((T, D), BF16), sd((T, KVW), BF16),
                   sd((T, KVW), BF16), sd((T, D), BF16), sd((T, D), BF16)],
        sem=("parallel",), comm=comm)


def _sgu_common(u, vs, lng, lnb, ws_ref, bfull):
    nc = u.shape[0] // CH
    ug, tu = _gelu(u)
    vg, tv = _gelu(vs)
    mu = jnp.mean(vg, axis=-1, keepdims=True)
    xc = vg - mu
    rstd = lax.rsqrt(jnp.mean(xc * xc, axis=-1, keepdims=True) + EPS)
    vhat = xc * rstd
    vnb = (vhat * lng + lnb).astype(BF16)
    tri = lax.broadcasted_iota(jnp.int32, (CH, CH), 0) >= lax.broadcasted_iota(jnp.int32, (CH, CH), 1)
    wts, rhss, mixed = [], [], []
    for g in range(NG):
        wt = jnp.where(tri, ws_ref[g], 0.0).astype(BF16)
        rhs = jnp.concatenate([vnb[CH * c:CH * (c + 1), CH * g:CH * (g + 1)] for c in range(nc)], axis=1)
        mix = _dot(wt, rhs)
        wts.append(wt)
        rhss.append(rhs)
        mixed.append([mix[:, CH * c:CH * (c + 1)] + bfull[:, CH * g:CH * (g + 1)] for c in range(nc)])
    return nc, ug, tu, tv, rstd, vhat, tri, wts, rhss, mixed


def _pair_layout(prev, cur, grp):
    j, half = grp // 2, grp % 2
    blk = jnp.concatenate([prev[:, CH * j:CH * (j + 1)], cur[:, CH * j:CH * (j + 1)]], axis=0).astype(F32)
    lo = lax.broadcasted_iota(jnp.int32, blk.shape, 1) < HD
    rolled = pltpu.roll(blk, HD, 1)
    even = jnp.where(lo, blk if half == 0 else rolled, 0.0)
    odd = jnp.where(lo, 0.0, rolled if half == 0 else blk)
    return jnp.concatenate([even, odd], axis=0).astype(BF16)


def _attn_mask(n):
    qi = lax.broadcasted_iota(jnp.int32, (CH, 2 * CH), 0)
    kc = lax.broadcasted_iota(jnp.int32, (CH, 2 * CH), 1)
    ok = (kc > qi) & (kc <= qi + CH) & ((kc >= CH) | (n > 0))
    return jnp.concatenate([ok, ok], axis=1)


def _softmax_sink(s, sink):
    m = jnp.maximum(jnp.max(s, axis=-1, keepdims=True), sink)
    p = jnp.exp(s - m)
    ps = jnp.exp(sink - m)
    inv = 1.0 / (jnp.sum(p, axis=-1, keepdims=True) + ps)
    return p * inv, ps * inv


QUERY_BLOCKS_PER_STEP = 2


def _attn_fwd(q, k, va, sinks, comm=None):
    T = q.shape[0]
    nblk = QUERY_BLOCKS_PER_STEP
    nsteps = T // (nblk * CH)
    npairs = D // CH

    def body(sk_ref, q_ref, kp_ref, kc_ref, vp_ref, vc_ref, o_ref):
        n = pl.program_id(0)
        even_lanes = lax.broadcasted_iota(jnp.int32, (CH, CH), 1) < HD
        ks = [kp_ref[...]] + [kc_ref[CH * b:CH * (b + 1)] for b in range(nblk)]
        vs = [vp_ref[...]] + [vc_ref[CH * b:CH * (b + 1)] for b in range(nblk)]
        masks = [_attn_mask(nblk * n)] + [_attn_mask(1)] * (nblk - 1)
        kks = [[_pair_layout(ks[b], ks[b + 1], grp) for grp in range(NKV)] for b in range(nblk)]
        vvs = [[_pair_layout(vs[b], vs[b + 1], grp) for grp in range(NKV)] for b in range(nblk)]
        work = [(b, p) for b in range(nblk) for p in range(npairs)]

        def scores(i):
            b, p = work[i]
            return _dot_nt(q_ref[CH * b:CH * (b + 1), CH * p:CH * (p + 1)], kks[b][p // 2])

        def unnormalised(s, sink):
            m = jnp.maximum(jnp.max(s, axis=-1, keepdims=True), sink)
            p = jnp.exp(s - m)
            return p, 1.0 / (jnp.sum(p, axis=-1, keepdims=True) + jnp.exp(sink - m))

        def value_product(i):
            b, p = work[i]
            pr, ie, io = probs[i]
            return _dot(pr, vvs[b][p // 2]) * jnp.where(even_lanes, ie, io)

        ahead = 3
        outs, probs = [], []
        pending = [scores(i) for i in range(ahead)]
        for i, (b, p) in enumerate(work):
            s = jnp.where(masks[b], pending.pop(0), -1e30)
            if i + ahead < len(work):
                pending.append(scores(i + ahead))
            pe, ie = unnormalised(s[:, :2 * CH], sk_ref[2 * p])
            po, io = unnormalised(s[:, 2 * CH:], sk_ref[2 * p + 1])
            probs.append((jnp.concatenate([pe, po], axis=1).astype(BF16), ie, io))
            if i >= 1:
                outs.append(value_product(i - 1))
        outs.append(value_product(len(work) - 1))
        for b in range(nblk):
            o_ref[CH * b:CH * (b + 1), :] = jnp.concatenate(outs[npairs * b:npairs * (b + 1)], axis=1).astype(BF16)

    prev = lambda n: (jnp.maximum(nblk * n - 1, 0), 0)
    cur = lambda n: (n, 0)
    return _call(
        body, (sinks, q, k, k, va, va), name="attn_fwd", grid=(nsteps,),
        in_specs=[pl.BlockSpec(memory_space=pltpu.SMEM), pl.BlockSpec((nblk * CH, D), cur),
                  pl.BlockSpec((CH, KVW), prev), pl.BlockSpec((nblk * CH, KVW), cur),
                  pl.BlockSpec((CH, KVW), prev), pl.BlockSpec((nblk * CH, KVW), cur)],
        out_specs=pl.BlockSpec((nblk * CH, D), cur), out_shape=jax.ShapeDtypeStruct((T, D), BF16),
        sem=("parallel",), comm=comm)


def _attn_bwd(q, k, va, datt, sinks, rc, rs1, rs2, comm=None):
    T = q.shape[0]
    nb = T // CH

    def body(sk_ref, q_ref, kp_ref, kc_ref, vp_ref, vc_ref, do_ref, cq_ref, s1q_ref, s2q_ref, ck_ref, s1k_ref, s2k_ref,
             dq_ref, dk_ref, dv_ref, dsk_ref, kcar, vcar):
        n = pl.program_id(0)

        @pl.when(n == 0)
        def _():
            kcar[...] = jnp.zeros_like(kcar)
            vcar[...] = jnp.zeros_like(vcar)
            dsk_ref[...] = jnp.zeros_like(dsk_ref)

        def flush(kprev, vprev):
            ck, s1k, s2k = ck_ref[...], s1k_ref[...], s2k_ref[...]
            for j in range(KVW // CH):
                sl = slice(CH * j, CH * (j + 1))
                dk_ref[:, sl] = _rope_t(kcar[:, sl] + kprev[:, sl], ck, s1k, s2k).astype(BF16)
                dv_ref[:, sl] = (vcar[:, sl] + vprev[:, sl]).astype(BF16)

        @pl.when(n < nb)
        def _():
            mask = _attn_mask(n)
            kp, kc, vp, vc = kp_ref[...], kc_ref[...], vp_ref[...], vc_ref[...]
            cq, s1q, s2q = cq_ref[...], s1q_ref[...], s2q_ref[...]
            lane = lax.broadcasted_iota(jnp.int32, (1, CH), 1)
            dsk = jnp.zeros((1, CH), F32)
            npairs = D // CH
            kks = [_pair_layout(kp, kc, grp) for grp in range(NKV)]
            vvs = [_pair_layout(vp, vc, grp) for grp in range(NKV)]
            qs = [q_ref[:, CH * p:CH * (p + 1)] for p in range(npairs)]
            dos = [do_ref[:, CH * p:CH * (p + 1)].astype(BF16) for p in range(npairs)]

            def first(p):
                return _dot_nt(qs[p], kks[p // 2]), _dot_nt(dos[p], vvs[p // 2])

            def last(p, ds, pb):
                return (_rope_t(_dot(ds, kks[p // 2]), cq, s1q, s2q) * Q_SCALE, _dot_tn(qs[p], ds), _dot_tn(dos[p], pb))

            ahead = 2
            pending = [first(p) for p in range(ahead)]
            mids, ends = [], []
            for p in range(npairs):
                s, dp = pending.pop(0)
                s = jnp.where(mask, s, -1e30)
                if p + ahead < npairs:
                    pending.append(first(p + ahead))
                ds_parts, p_parts = [], []
                for par in range(2):
                    sl = slice(2 * CH * par, 2 * CH * (par + 1))
                    pr, psink = _softmax_sink(s[:, sl], sk_ref[2 * p + par])
                    delta = jnp.sum(pr * dp[:, sl], axis=-1, keepdims=True)
                    ds_parts.append(pr * (dp[:, sl] - delta))
                    p_parts.append(pr)
                    tot = -jnp.sum(psink * delta, axis=0, keepdims=True)
                    dsk = dsk + jnp.where(lane == 2 * p + par, tot, 0.0)
                mids.append((jnp.concatenate(ds_parts, axis=1).astype(BF16), jnp.concatenate(p_parts, axis=1).astype(BF16)))
                if p >= 1:
                    ends.append(last(p - 1, *mids[p - 1]))
            ends.append(last(npairs - 1, *mids[-1]))
            dq_cols = [e[0] for e in ends]
            def fold(i):
                rows = []
                for grp in range(NKV):
                    acc = ends[2 * grp][i] + ends[2 * grp + 1][i]
                    rows.append(acc[:HD, :2 * CH] + acc[HD:, 2 * CH:])
                return jnp.concatenate(rows, axis=0).T

            dkf, dvf = fold(1), fold(2)
            dq_ref[...] = jnp.concatenate(dq_cols, axis=1).astype(BF16)
            dsk_ref[...] += dsk
            flush(dkf[:CH], dvf[:CH])
            kcar[...] = dkf[CH:]
            vcar[...] = dvf[CH:]

        @pl.when(n == nb)
        def _():
            z = jnp.zeros((CH, KVW), F32)
            flush(z, z)

    last = nb - 1
    cur = lambda n: (jnp.minimum(n, last), 0)
    prev = lambda n: (jnp.clip(n - 1, 0, last), 0)
    sd = jax.ShapeDtypeStruct
    return _call(
        body, (sinks, q, k, k, va, va, datt, rc, rs1, rs2, rc, rs1, rs2), name="attn_bwd", grid=(nb + 1,),
        in_specs=[pl.BlockSpec(memory_space=pltpu.SMEM), pl.BlockSpec((CH, D), cur),
                  pl.BlockSpec((CH, KVW), prev), pl.BlockSpec((CH, KVW), cur),
                  pl.BlockSpec((CH, KVW), prev), pl.BlockSpec((CH, KVW), cur),
                  pl.BlockSpec((CH, D), cur),
                  pl.BlockSpec((CH, CH), cur), pl.BlockSpec((CH, CH), cur), pl.BlockSpec((CH, CH), cur),
                  pl.BlockSpec((CH, CH), prev), pl.BlockSpec((CH, CH), prev), pl.BlockSpec((CH, CH), prev)],
        out_specs=[pl.BlockSpec((CH, D), cur), pl.BlockSpec((CH, KVW), prev), pl.BlockSpec((CH, KVW), prev),
                   _const((1, CH))],
        out_shape=[sd((T, D), BF16), sd((T, KVW), BF16), sd((T, KVW), BF16), sd((1, CH), F32)],
        scratch_shapes=[pltpu.VMEM((CH, KVW), F32), pltpu.VMEM((CH, KVW), F32)], sem=("arbitrary",), comm=comm)


def _sgu_merge_fwd(u, vs, lng, lnb, ws, bfull, att, ga, gb, x, w_a, w_b, w_o, g2, tm, comm=None):
    T = x.shape[0]

    def body(u_ref, v_ref, lng_ref, lnb_ref, ws_ref, bf_ref, att_ref, ga_ref, gb_ref, x_ref, wa_ref, wb_ref, wo_ref, g_ref,
             a_ref, pa_ref, pb_ref, mg_ref, mix_ref, x1_ref):
        pb = _dot(att_ref[...], wb_ref[...])
        nc, ug, _, _, _, _, _, _, _, mixed = _sgu_common(
            u_ref[...], v_ref[...], lng_ref[...], lnb_ref[...], ws_ref, bf_ref[...])
        mixed_all = jnp.concatenate(
            [jnp.concatenate([mixed[g][c] for g in range(NG)], axis=1) for c in range(nc)], axis=0)
        a = (ug * mixed_all).astype(BF16)
        a_ref[...] = a
        pa = _dot(a, wa_ref[...])
        pa_ref[...] = pa.astype(BF16)
        pb_ref[...] = pb.astype(BF16)
        merged = (_sigmoid(ga_ref[...].astype(F32)) * pa + _sigmoid(gb_ref[...].astype(F32)) * pb).astype(BF16)
        mg_ref[...] = merged
        mix = _dot(merged, wo_ref[...])
        mix_ref[...] = mix
        mhat, _ = _rms_hat(mix)
        x1_ref[...] = x_ref[...] + mhat * g_ref[...]

    sd = jax.ShapeDtypeStruct
    return _call(
        body, (u, vs, lng, lnb, ws, bfull, att, ga, gb, x, w_a, w_b, w_o, g2), name="sgu_merge_fwd", grid=(T // tm,),
        in_specs=[_rows(tm, D), _rows(tm, D), _const((1, D)), _const((1, D)), _const((NG, CH, CH)), _const((CH, D))]
        + [_rows(tm, D)] * 4 + [_resident((D, D))] * 3 + [_const((1, D))],
        out_specs=[_rows(tm, D)] * 6,
        out_shape=[sd((T, D), BF16)] * 4 + [sd((T, D), F32)] * 2,
        sem=("parallel",), comm=comm, vmem=VMEM_PHYSICAL)


def _merge_bwd(dx1, mix, ga, gb, pa, pb, a, att, merged, w_a, w_b, w_o, g2, tm, comm=None):
    T = dx1.shape[0]
    nsteps = T // tm

    def body(dx1_ref, mix_ref, ga_ref, gb_ref, pa_ref, pb_ref, a_ref, att_ref, mg_ref, wa_ref, wb_ref, wo_ref, g_ref,
             dga_ref, dgb_ref, da_ref, datt_ref, dg_ref, dwa_ref, dwb_ref, dwo_ref, acc, sem):
        i = pl.program_id(0)

        @pl.when(i == 0)
        def _():
            dg_ref[...] = jnp.zeros_like(dg_ref)
            acc[...] = jnp.zeros_like(acc)

        mhat, r = _rms_hat(mix_ref[...])
        dmix, dg = _rms_bwd(mhat, r, g_ref[...], dx1_ref[...])
        dg_ref[...] += dg
        dmix = dmix.astype(BF16)
        dmerged = _dot_nt(dmix, wo_ref[...])
        sa = _sigmoid(ga_ref[...].astype(F32))
        sb = _sigmoid(gb_ref[...].astype(F32))
        dao = (dmerged * sa).astype(BF16)
        dbo = (dmerged * sb).astype(BF16)
        dga_ref[...] = (dmerged * pa_ref[...].astype(F32) * (sa * (1.0 - sa))).astype(BF16)
        dgb_ref[...] = (dmerged * pb_ref[...].astype(F32) * (sb * (1.0 - sb))).astype(BF16)
        da_ref[...] = _dot_nt(dao, wa_ref[...])
        datt_ref[...] = _dot_nt(dbo, wb_ref[...]).astype(BF16)
        acc[0] += _dot_tn(a_ref[...], dao)
        acc[1] += _dot_tn(att_ref[...], dbo)
        acc[2] += _dot_tn(mg_ref[...], dmix)

        @pl.when(i == nsteps - 1)
        def _():
            outs = [pltpu.make_async_copy(acc.at[j], ref, sem.at[j]) for j, ref in enumerate((dwa_ref, dwb_ref, dwo_ref))]
            for cp in outs:
                cp.start()
            for cp in outs:
                cp.wait()

    sd = jax.ShapeDtypeStruct
    return _call(
        body, (dx1, mix, ga, gb, pa, pb, a, att, merged, w_a, w_b, w_o, g2), name="merge_bwd", grid=(nsteps,),
        in_specs=[_rows(tm, D)] * 9 + [_resident((D, D))] * 3 + [_const((1, D))],
        out_specs=[_rows(tm, D)] * 4 + [_const((1, D))] + [ANY] * 3,
        out_shape=[sd((T, D), BF16), sd((T, D), BF16), sd((T, D), F32), sd((T, D), BF16), sd((1, D), F32)]
        + [sd((D, D), F32)] * 3,
        scratch_shapes=[pltpu.VMEM((3, D, D), F32), _dma_sems(3)], sem=("arbitrary",), comm=comm)


def _ffn(x1, target, w1, w2, g3, g4, tm):
    T = x1.shape[0]

    def body(x_ref, t_ref, w1_ref, w2_ref, g3_ref, g4_ref,
             hf_ref, f2_ref, dff_ref, df1_ref, dx_ref, ls_ref, dg3_ref, dg4_ref):
        @pl.when(pl.program_id(0) == 0)
        def _():
            ls_ref[...] = jnp.zeros_like(ls_ref)
            dg3_ref[...] = jnp.zeros_like(dg3_ref)
            dg4_ref[...] = jnp.zeros_like(dg4_ref)

        x = x_ref[...]
        g3, g4 = g3_ref[...], g4_ref[...]
        xhat, r3 = _rms_hat(x)
        hf = (xhat * g3).astype(BF16)
        hf_ref[...] = hf
        rl = jnp.maximum(_dot(hf, w1_ref[...]), 0.0)
        f2 = (rl * rl).astype(BF16)
        f2_ref[...] = f2
        fhat, r4 = _rms_hat(_dot(f2, w2_ref[...]))
        err = x + fhat * g4 - t_ref[...]
        ls_ref[...] += jnp.sum(err * err, axis=0, keepdims=True)
        dy = err * (1.0 / D)
        dff, dg4 = _rms_bwd(fhat, r4, g4, dy)
        dg4_ref[...] += dg4
        dff = dff.astype(BF16)
        dff_ref[...] = dff
        df1 = (_dot_nt(dff, w2_ref[...]) * (2.0 * rl)).astype(BF16)
        df1_ref[...] = df1
        dxn, dg3 = _rms_bwd(xhat, r3, g3, _dot_nt(df1, w1_ref[...]))
        dg3_ref[...] += dg3
        dx_ref[...] = dy + dxn

    sd = jax.ShapeDtypeStruct
    return pl.pallas_call(
        body, name="ffn_fwd_bwd", grid=(T // tm,),
        in_specs=[_rows(tm, D), _rows(tm, D), _resident((D, DFF)), _resident((DFF, D)), _const((1, D)), _const((1, D))],
        out_specs=[_rows(tm, D), _rows(tm, DFF), _rows(tm, D), _rows(tm, DFF), _rows(tm, D), _const((1, D)),
                   _const((1, D)), _const((1, D))],
        out_shape=[sd((T, D), BF16), sd((T, DFF), BF16), sd((T, D), BF16), sd((T, DFF), BF16), sd((T, D), F32),
                   sd((1, D), F32), sd((1, D), F32), sd((1, D), F32)],
        compiler_params=pltpu.CompilerParams(vmem_limit_bytes=VMEM_PHYSICAL, dimension_semantics=("arbitrary",)),
    )(x1, target, w1, w2, g3, g4)


def _sgu_inproj_bwd(u, vs, da, lng, lnb, ws, bfull, parts, x, dx1, g1, w_in, tm):
    T = x.shape[0]
    nsteps = T // tm
    widths = [p.shape[1] for p in parts]
    offs = [2 * D + sum(widths[:i]) for i in range(len(widths) + 1)]
    assert offs[-1] == IN_W
    n = len(parts)

    def body(*refs):
        u_ref, v_ref, da_ref, lng_ref, lnb_ref, ws_ref, bf_ref = refs[:7]
        prefs = refs[7:7 + n]
        x_ref, dx1_ref, g_ref, w_ref = refs[7 + n:11 + n]
        dx_ref, dp_ref, dg_ref, dws_ref, dbs_ref, dlg_ref, dlb_ref, db_ref = refs[11 + n:]
        i = pl.program_id(0)

        @pl.when(i == 0)
        def _():
            for ref in (dg_ref, dws_ref, db_ref, dlg_ref, dlb_ref):
                ref[...] = jnp.zeros_like(ref)

        for j in range(n):
            dp_ref[:, offs[j]:offs[j + 1]] = prefs[j][...]
        cut = 2 * D + (IN_W - 2 * D) // 2
        dh_a = _dot_nt(dp_ref[:, 2 * D:cut], w_ref[:, 2 * D:cut])
        u, vs, da, lng = u_ref[...], v_ref[...], da_ref[...], lng_ref[...]
        nc, ug, tu, tv, rstd, vhat, tri, wts, rhss, mixed = _sgu_common(u, vs, lng, lnb_ref[...], ws_ref, bf_ref[...])
        mixed_all = jnp.concatenate(
            [jnp.concatenate([mixed[g][c] for g in range(NG)], axis=1) for c in range(nc)], axis=0)
        dp_ref[:, :D] = (da * mixed_all * _gelu_grad(tu)).astype(BF16)
        dh_u = _dot_nt(dp_ref[:, :D], w_ref[:, :D])
        dmixed = da * ug
        dvn_cols = []
        for g in range(NG):
            dmix = [dmixed[CH * c:CH * (c + 1), CH * g:CH * (g + 1)] for c in range(nc)]
            db_ref[:, CH * g:CH * (g + 1)] += functools.reduce(lambda a, b: a + b, dmix)
            dm = jnp.concatenate(dmix, axis=1).astype(BF16)
            dws_ref[g] += _dot_nt(dm, rhss[g])
            dvn_cols.append(_dot_tn(wts[g], dm))
        dh_b = _dot_nt(dp_ref[:, cut:], w_ref[:, cut:])
        dvn = jnp.concatenate(
            [jnp.concatenate([dvn_cols[g][:, CH * c:CH * (c + 1)] for g in range(NG)], axis=1) for c in range(nc)],
            axis=0)
        dlg_ref[...] += jnp.sum(dvn * vhat, axis=0, keepdims=True)
        dlb_ref[...] += jnp.sum(dvn, axis=0, keepdims=True)
        dvh = dvn * lng
        dvg = rstd * (dvh - jnp.mean(dvh, axis=-1, keepdims=True)
                      - vhat * jnp.mean(dvh * vhat, axis=-1, keepdims=True))
        dp_ref[:, D:2 * D] = (dvg * _gelu_grad(tv)).astype(BF16)

        dh = (dh_a + dh_u) + (dh_b + _dot_nt(dp_ref[:, D:2 * D], w_ref[:, D:2 * D]))
        xhat, r = _rms_hat(x_ref[...])
        dxn, dg = _rms_bwd(xhat, r, g_ref[...], dh)
        dg_ref[...] += dg
        dx_ref[...] = dx1_ref[...] + dxn

        @pl.when(i == nsteps - 1)
        def _():
            for g in range(NG):
                dws_ref[g] = jnp.where(tri, dws_ref[g], 0.0)
                dbs_ref[g:g + 1, :] = jnp.sum(db_ref[:, CH * g:CH * (g + 1)].T, axis=0, keepdims=True)

    sd = jax.ShapeDtypeStruct
    outs, _ = _call(
        body, (u, vs, da, lng, lnb, ws, bfull, *parts, x, dx1, g1, w_in), name="sgu_inproj_bwd", grid=(nsteps,),
        in_specs=[_rows(tm, D), _rows(tm, D), _rows(tm, D), _const((1, D)), _const((1, D)), _const((NG, CH, CH)),
                  _const((CH, D))] + [_rows(tm, w) for w in widths]
        + [_rows(tm, D), _rows(tm, D), _const((1, D)), _resident((D, IN_W))],
        out_specs=[_rows(tm, D), _rows(tm, IN_W), _const((1, D)), _const((NG, CH, CH)), _const((NG, CH)), _const((1, D)),
                   _const((1, D))],
        out_shape=[sd((T, D), F32), sd((T, IN_W), BF16), sd((1, D), F32), sd((NG, CH, CH), F32), sd((NG, CH), F32),
                   sd((1, D), F32), sd((1, D), F32)],
        scratch_shapes=[pltpu.VMEM((CH, D), F32)], sem=("arbitrary",), vmem=VMEM_PHYSICAL)
    return outs


def _wgrad(a, g, tn, tm, name, comm=None, vmem=None):
    T, K = a.shape
    N = g.shape[1]

    def body(a_ref, g_ref, o_ref):
        @pl.when(pl.program_id(1) == 0)
        def _():
            o_ref[...] = jnp.zeros_like(o_ref)

        o_ref[...] += _dot_tn(a_ref[...], g_ref[...])

    return _call(
        body, (a, g), name=name, grid=(N // tn, T // tm),
        in_specs=[pl.BlockSpec((tm, K), lambda j, t: (t, 0)), pl.BlockSpec((tm, tn), lambda j, t: (t, j))],
        out_specs=pl.BlockSpec((K, tn), lambda j, t: (0, j)),
        out_shape=jax.ShapeDtypeStruct((K, N), F32), sem=("parallel", "arbitrary"), comm=comm, vmem=vmem)


def _adamw(ws, gs, ms, vs, trs, name):
    n = len(ws)
    walk = _Walk(w.shape[0] // tr for w, tr in zip(ws, trs))
    bc1 = 1.0 / (1.0 - B1 ** STEP)
    bc2 = 1.0 / (1.0 - B2 ** STEP)

    def body(*refs):
        i = pl.program_id(0)
        for k in range(n):
            mine = tuple(refs[j * n + k] for j in range(8))

            @pl.when(walk.mine(k, i))
            def _(mine=mine):
                w_ref, g_ref, m_ref, v_ref, go_ref, d_ref, nm_ref, nv_ref = mine
                g = g_ref[...]
                go_ref[...] = g
                m = B1 * m_ref[...] + (1.0 - B1) * g
                v = B2 * v_ref[...] + (1.0 - B2) * (g * g)
                nm_ref[...] = m
                nv_ref[...] = v
                d_ref[...] = -LR * ((m * bc1) / (jnp.sqrt(v * bc2) + AEPS) + WD * w_ref[...])

    def spec(k):
        return pl.BlockSpec((trs[k], ws[k].shape[1]), lambda i: (walk.tile(k, i), 0))

    specs = [spec(k) for k in range(n)]
    res = pl.pallas_call(
        body, name=name, grid=(walk.steps,), in_specs=specs * 4, out_specs=specs * 4,
        out_shape=[jax.ShapeDtypeStruct(w.shape, F32) for w in ws] * 4,
        compiler_params=_cparams(("arbitrary",)),
    )(*ws, *gs, *ms, *vs)
    return [tuple(res[j * n + k] for j in range(4)) for k in range(n)]


BIG = (("col", (D, IN_W)), ("row", (D, D)), ("row", (D, D)), ("row", (D, D)), ("col", (D, DFF)), ("row", (DFF, D)))
NBIG = len(BIG)
ANY = pl.BlockSpec(memory_space=pl.ANY)


def _shard_shape(kind, shape):
    R, C = shape
    return (R, C // 4) if kind == "col" else (R // 4, C)


def _half_shape(kind, shape):
    R, C = shape
    return (R // 2, C) if kind == "col" else (R, C // 2)


def _piece_shape(kind, shape):
    R, C = shape
    return (R // 2, C // 4) if kind == "col" else (R // 4, C // 2)


def _own_region(ref, kind, shape, s):
    R, C = shape
    return ref.at[:, pl.ds(s * (C // 4), C // 4)] if kind == "col" else ref.at[pl.ds(s * (R // 4), R // 4), :]


def _ag_region(ref, kind, shape, s, hc):
    R, C = shape
    if kind == "col":
        return ref.at[pl.ds(hc * (R // 2), R // 2), pl.ds(s * (C // 4), C // 4)]
    return ref.at[pl.ds(s * (R // 4) + hc * (R // 8), R // 8), :]


def _ag_shard_half(ref, kind, shape, hc):
    R, C = shape
    return ref.at[pl.ds(hc * (R // 2), R // 2), :] if kind == "col" else ref.at[pl.ds(hc * (R // 8), R // 8), :]


def _grad_half(ref, kind, shape, hc):
    R, C = shape
    return ref.at[pl.ds(hc * (R // 2), R // 2), :] if kind == "col" else ref.at[:, pl.ds(hc * (C // 2), C // 2)]


def _half_piece(ref, kind, shape, s):
    R, C = shape
    return ref.at[:, pl.ds(s * (C // 4), C // 4)] if kind == "col" else ref.at[pl.ds(s * (R // 4), R // 4), :]


def _place():
    x, y, c = lax.axis_index("x"), lax.axis_index("y"), lax.axis_index("c")
    chips = [(1 - x, y), (x, 1 - y), (1 - x, 1 - y)]
    return x, y, c, chips


def _rcopy(src, dst, ssem, rsem, dev):
    return pltpu.make_async_remote_copy(src_ref=src, dst_ref=dst, send_sem=ssem, recv_sem=rsem,
                                        device_id=dev, device_id_type=MESH)


def _dma_sems(n):
    return pltpu.SemaphoreType.DMA((n,))


def _x_gather_ici(shards, ws):
    n = len(ws)
    specs = [BIG[w] for w in ws]

    def place():
        x, y, c, chips = _place()
        return c, chips, 2 * x + y

    def sends(sh, full, sc):
        c, chips, me_s = place()
        return [_rcopy(_ag_shard_half(sh[i], kind, shape, c), _ag_region(full[i], kind, shape, me_s, c),
                       sc[0].at[3 * i + j], sc[1].at[3 * i + j], (cx, cy, c))
                for i, (kind, shape) in enumerate(specs) for j, (cx, cy) in enumerate(chips)]

    def start(sh, full, sc):
        for i in range(n):
            pltpu.make_async_copy(sh[i], sc[4 + i], sc[2].at[i]).start()
        for cp in sends(sh, full, sc):
            cp.start()

    def finish(sh, full, sc):
        c, chips, me_s = place()
        stores = []
        for i, (kind, shape) in enumerate(specs):
            pltpu.make_async_copy(sh[i], sc[4 + i], sc[2].at[i]).wait()
            st = pltpu.make_async_copy(sc[4 + i], _own_region(full[i], kind, shape, me_s), sc[3].at[i])
            st.start()
            stores.append(st)
        for i, (kind, shape) in enumerate(specs):
            for j, (cx, cy) in enumerate(chips):
                reg = _ag_region(full[i], kind, shape, 2 * cx + cy, c)
                _rcopy(reg, reg, sc[0].at[3 * i + j], sc[1].at[3 * i + j], (cx, cy, c)).wait_recv()
        for cp in sends(sh, full, sc):
            cp.wait_send()
        for st in stores:
            st.wait()

    return _Exchange(
        shards, [jax.ShapeDtypeStruct(shape, BF16) for _, shape in specs], {},
        [_dma_sems(3 * n), _dma_sems(3 * n), _dma_sems(n), _dma_sems(n)]
        + [pltpu.VMEM(_shard_shape(k, s), BF16) for k, s in specs], start, finish)


def _x_gather_d2d(wholes, ws):
    specs = [BIG[w] for w in ws]
    n = len(ws)

    def copies(full, sc, mine):
        x, y, c, chips = _place()
        hc = c if mine else 1 - c
        return [_rcopy(reg, reg, sc[0].at[3 * i + j], sc[1].at[3 * i + j], (x, y, 1 - c))
                for i, (kind, shape) in enumerate(specs) for j, (cx, cy) in enumerate(chips)
                for reg in [_ag_region(full[i], kind, shape, 2 * cx + cy, hc)]]

    def start(_, full, sc):
        for cp in copies(full, sc, True):
            cp.start()

    def finish(_, full, sc):
        for cp in copies(full, sc, False):
            cp.wait_recv()
        for cp in copies(full, sc, True):
            cp.wait_send()

    return _Exchange(wholes, [jax.ShapeDtypeStruct(shape, BF16) for _, shape in specs], {i: i for i in range(n)},
                     [_dma_sems(3 * n), _dma_sems(3 * n)], start, finish)


def _x_grads_sibling(grads, ws):
    specs = [BIG[w] for w in ws]
    n = len(ws)

    def copies(g, got, sc):
        x, y, c, _ = _place()
        return [_rcopy(_grad_half(g[i], kind, shape, 1 - c), got[i], sc[0].at[i], sc[1].at[i], (x, y, 1 - c))
                for i, (kind, shape) in enumerate(specs)]

    def start(g, got, sc):
        for cp in copies(g, got, sc):
            cp.start()

    def finish(g, got, sc):
        for cp in copies(g, got, sc):
            cp.wait_recv()
        for cp in copies(g, got, sc):
            cp.wait_send()

    return _Exchange(grads, [jax.ShapeDtypeStruct(_half_shape(k, s), F32) for k, s in specs], {},
                     [_dma_sems(n), _dma_sems(n)], start, finish)


def _x_grads_chips(sums_bf, ws):
    specs = [BIG[w] for w in ws]
    n = len(ws)

    def copies(s16, got, sc):
        x, y, c, chips = _place()
        return [_rcopy(_half_piece(s16[i], kind, shape, 2 * cx + cy), got[i].at[j],
                       sc[0].at[3 * i + j], sc[1].at[3 * i + j], (cx, cy, c))
                for i, (kind, shape) in enumerate(specs) for j, (cx, cy) in enumerate(chips)]

    def start(s16, got, sc):
        for cp in copies(s16, got, sc):
            cp.start()

    def finish(s16, got, sc):
        for cp in copies(s16, got, sc):
            cp.wait_recv()
        for cp in copies(s16, got, sc):
            cp.wait_send()

    return _Exchange(sums_bf, [jax.ShapeDtypeStruct((3,) + _piece_shape(k, s), BF16) for k, s in specs], {},
                     [_dma_sems(3 * n), _dma_sems(3 * n)], start, finish)


def _shard_half(ref, kind, shape, hc):
    sr, sc = _shard_shape(kind, shape)
    return ref.at[pl.ds(hc * (sr // 2), sr // 2), :] if kind == "col" else ref.at[:, pl.ds(hc * (sc // 2), sc // 2)]


def _x_grads_share(shard_grads, ws):
    specs = [BIG[w] for w in ws]
    n = len(ws)

    def copies(g, sc, mine):
        x, y, c, _ = _place()
        hc = c if mine else 1 - c
        return [_rcopy(part, part, sc[0].at[i], sc[1].at[i], (x, y, 1 - c))
                for i, (kind, shape) in enumerate(specs) for part in [_shard_half(g[i], kind, shape, hc)]]

    def start(_, g, sc):
        for cp in copies(g, sc, True):
            cp.start()

    def finish(_, g, sc):
        for cp in copies(g, sc, False):
            cp.wait_recv()
        for cp in copies(g, sc, True):
            cp.wait_send()

    return _Exchange(shard_grads, [jax.ShapeDtypeStruct(_shard_shape(k, s), F32) for k, s in specs],
                     {i: i for i in range(n)}, [_dma_sems(n), _dma_sems(n)], start, finish)


ADD_BLOCK_BYTES = 4 * 1024 * 1024


def _add_rows(rows, cols, n_arrays):
    limit = ADD_BLOCK_BYTES // (1 if n_arrays == 1 else 4)
    r = rows
    while r > 64 and r * cols * 4 > limit:
        r //= 2
    return r


class _Walk:
    def __init__(self, tiles):
        self.tiles = list(tiles)
        self.starts = [sum(self.tiles[:k]) for k in range(len(self.tiles))]
        self.steps = sum(self.tiles)

    def tile(self, k, i):
        return jnp.clip(i - self.starts[k], 0, self.tiles[k] - 1)

    def mine(self, k, i):
        return (i >= self.starts[k]) & (i < self.starts[k] + self.tiles[k])


def _add_halves(place, gs, gots, kinds, name):
    n = len(gs)
    halves = [_half_shape(kind, g.shape) for g, kind in zip(gs, kinds)]
    rows = [_add_rows(hr, hc, n) for hr, hc in halves]
    walk = _Walk(hr // r for (hr, _), r in zip(halves, rows))

    def body(p_ref, *refs):
        i = pl.program_id(0)
        for k in range(n):
            g_ref, b_ref, s_ref, sb_ref = (refs[j * n + k] for j in range(4))

            @pl.when(walk.mine(k, i))
            def _(g_ref=g_ref, b_ref=b_ref, s_ref=s_ref, sb_ref=sb_ref):
                s = g_ref[...] + b_ref[...]
                s_ref[...] = s
                sb_ref[...] = s.astype(BF16)

    def g_spec(k):
        if kinds[k] == "col":
            return pl.BlockSpec((rows[k], gs[k].shape[1]), lambda i, p: (p[0] * walk.tiles[k] + walk.tile(k, i), 0))
        return pl.BlockSpec((rows[k], halves[k][1]), lambda i, p: (walk.tile(k, i), p[0]))

    def spec(k):
        return pl.BlockSpec((rows[k], halves[k][1]), lambda i, p: (walk.tile(k, i), 0))

    specs = [spec(k) for k in range(n)]
    res = pl.pallas_call(
        body, name=name,
        grid_spec=pltpu.PrefetchScalarGridSpec(num_scalar_prefetch=1, grid=(walk.steps,),
                                               in_specs=[g_spec(k) for k in range(n)] + specs, out_specs=specs + specs),
        out_shape=[jax.ShapeDtypeStruct(h, F32) for h in halves] + [jax.ShapeDtypeStruct(h, BF16) for h in halves],
        compiler_params=_cparams(("arbitrary",)),
    )(place, *gs, *gots)
    return [(res[k], res[n + k]) for k in range(n)]


def _add_pieces(place, halves, gots, specs_big, name):
    n = len(halves)
    pieces = [_piece_shape(kind, shape) for kind, shape in specs_big]
    rows = [_add_rows(pr, pc, n) for pr, pc in pieces]
    walk = _Walk(pr // r for (pr, _), r in zip(pieces, rows))

    def body(p_ref, *refs):
        i = pl.program_id(0)
        for k in range(n):
            m_ref, g_ref, o_ref = (refs[j * n + k] for j in range(3))

            @pl.when(walk.mine(k, i))
            def _(m_ref=m_ref, g_ref=g_ref, o_ref=o_ref):
                acc = m_ref[...]
                for j in range(3):
                    acc = acc + g_ref[j].astype(F32)
                o_ref[...] = acc

    def m_spec(k):
        if specs_big[k][0] == "col":
            return pl.BlockSpec((rows[k], pieces[k][1]), lambda i, p: (walk.tile(k, i), p[1]))
        return pl.BlockSpec((rows[k], pieces[k][1]), lambda i, p: (p[1] * walk.tiles[k] + walk.tile(k, i), 0))

    def got_spec(k):
        return pl.BlockSpec((3, rows[k], pieces[k][1]), lambda i, p: (0, walk.tile(k, i), 0))

    def o_spec(k):
        if specs_big[k][0] == "col":
            return pl.BlockSpec((rows[k], pieces[k][1]), lambda i, p: (p[0] * walk.tiles[k] + walk.tile(k, i), 0))
        return pl.BlockSpec((rows[k], pieces[k][1]), lambda i, p: (walk.tile(k, i), p[0]))

    return pl.pallas_call(
        body, name=name,
        grid_spec=pltpu.PrefetchScalarGridSpec(
            num_scalar_prefetch=1, grid=(walk.steps,),
            in_specs=[m_spec(k) for k in range(n)] + [got_spec(k) for k in range(n)],
            out_specs=[o_spec(k) for k in range(n)]),
        out_shape=[jax.ShapeDtypeStruct(_shard_shape(kind, shape), F32) for kind, shape in specs_big],
        compiler_params=_cparams(("arbitrary",)),
    )(place, *halves, *gots)


SMALL_ROWS = 1024 + 8 * 8 + 8


def _x_small_all_reduce(p):
    def parts(p_ref, sc):
        slots, ssem, rsem = sc[0], sc[2], sc[3]
        x, y, c = lax.axis_index("x"), lax.axis_index("y"), lax.axis_index("c")
        me = 4 * x + 2 * y + c
        out = []
        for r in range(1, 8):
            bx, by, bc = (r >> 2) & 1, (r >> 1) & 1, r & 1
            tgt = (1 - x if bx else x, 1 - y if by else y, 1 - c if bc else c)
            send = _rcopy(p_ref, slots.at[me], ssem.at[r - 1], rsem.at[r - 1], tgt)
            src = 4 * tgt[0] + 2 * tgt[1] + tgt[2]
            recv = _rcopy(p_ref, slots.at[src], ssem.at[r - 1], rsem.at[r - 1], tgt)
            out.append((send, recv))
        return me, out

    def start(ins, outs, sc):
        me, cps = parts(ins[0], sc)
        pltpu.make_async_copy(ins[0], sc[0].at[me], sc[4].at[0]).start()
        for send, _ in cps:
            send.start()

    def finish(ins, outs, sc):
        me, cps = parts(ins[0], sc)
        pltpu.make_async_copy(ins[0], sc[0].at[me], sc[4].at[0]).wait()
        for _, recv in cps:
            recv.wait_recv()
        acc = sc[0][0]
        for d in range(1, 8):
            acc = acc + sc[0][d]
        sc[1][...] = acc
        back = pltpu.make_async_copy(sc[1], outs[0], sc[4].at[1])
        back.start()
        for send, _ in cps:
            send.wait_send()
        back.wait()

    return _Exchange([p], [jax.ShapeDtypeStruct((SMALL_ROWS, CH), F32)], {},
                     [pltpu.VMEM((8, SMALL_ROWS, CH), F32), pltpu.VMEM((SMALL_ROWS, CH), F32), _dma_sems(7), _dma_sems(7),
                      _dma_sems(2)], start, finish)


def _rope_tables(positions, comm=None):
    T = positions.shape[0]
    inv_freq = 500000.0 ** (-jnp.arange(0, 2 * ROPE_HALF, 2, dtype=F32) / (2 * ROPE_HALF))
    head = jnp.concatenate([inv_freq, inv_freq, jnp.zeros((HD - 2 * ROPE_HALF,), F32)])
    lane_freq = jnp.concatenate([head, head])[None, :]
    pos = jnp.broadcast_to(positions.astype(F32)[:, None], (T, CH))
    tm = min(1024, T)

    def body(p_ref, f_ref, c_ref, s1_ref, s2_ref):
        ang = p_ref[...] * f_ref[...]
        sin = jnp.sin(ang)
        first = (lax.broadcasted_iota(jnp.int32, ang.shape, 1) % HD) < ROPE_HALF
        c_ref[...] = jnp.cos(ang)
        s1_ref[...] = jnp.where(first, -sin, 0.0)
        s2_ref[...] = jnp.where(first, 0.0, sin)

    return _call(body, (pos, lane_freq), name="rope_tables", grid=(T // tm,),
                 in_specs=[_rows(tm, CH), _const((1, CH))], out_specs=[_rows(tm, CH)] * 3,
                 out_shape=[jax.ShapeDtypeStruct((T, CH), F32)] * 3, sem=("parallel",), comm=comm)


BIG_NAMES = ("w_in", "w_a", "w_b", "w_o", "w_ff_in", "w_ff_out")
SMALL_NAMES = ("w_spatial", "ln_v_gain", "ln_v_bias", "b_spatial", "sinks", "norm_mix_pre", "norm_mix_post",
               "norm_ff_pre", "norm_ff_post")
WEIGHT_ORDER = ("w_in", "ln_v_gain", "ln_v_bias", "w_spatial", "b_spatial", "sinks", "w_a", "w_b", "w_o",
                "norm_mix_pre", "norm_mix_post", "w_ff_in", "w_ff_out", "norm_ff_pre", "norm_ff_post")


def _pack_small(d, loss_sums=None):
    parts = []
    for n in SMALL_NAMES:
        flat = d[n].reshape(-1)
        pad = (-flat.shape[0]) % (8 * CH)
        parts.append(jnp.pad(flat, (0, pad)).reshape(-1, CH))
    parts.append(jnp.zeros((8, CH), F32) if loss_sums is None else loss_sums.reshape(8, CH))
    return jnp.concatenate(parts, axis=0)


def _unpack_small(p, like):
    out, row = {}, 0
    for n in SMALL_NAMES:
        size = like[n].size
        rows = -(-size // (8 * CH)) * 8
        out[n] = p[row:row + rows].reshape(-1)[:size].reshape(like[n].shape)
        row += rows
    return out


def kernel(x, positions, w_in, ln_v_gain, ln_v_bias, w_spatial, b_spatial, sinks, w_a, w_b, w_o, norm_mix_pre, norm_mix_post, w_ff_in, w_ff_out, norm_ff_pre, norm_ff_post, loss_target, m_w_in, m_ln_v_gain, m_ln_v_bias, m_w_spatial, m_b_spatial, m_sinks, m_w_a, m_w_b, m_w_o, m_norm_mix_pre, m_norm_mix_post, m_w_ff_in, m_w_ff_out, m_norm_ff_pre, m_norm_ff_post, v_w_in, v_ln_v_gain, v_ln_v_bias, v_w_spatial, v_b_spatial, v_sinks, v_w_a, v_w_b, v_w_o, v_norm_mix_pre, v_norm_mix_post, v_w_ff_in, v_w_ff_out, v_norm_ff_pre, v_norm_ff_post):
    w = dict(w_in=w_in, ln_v_gain=ln_v_gain, ln_v_bias=ln_v_bias, w_spatial=w_spatial, b_spatial=b_spatial, sinks=sinks,
             w_a=w_a, w_b=w_b, w_o=w_o, norm_mix_pre=norm_mix_pre, norm_mix_post=norm_mix_post, w_ff_in=w_ff_in,
             w_ff_out=w_ff_out, norm_ff_pre=norm_ff_pre, norm_ff_post=norm_ff_post)
    m = dict(w_in=m_w_in, ln_v_gain=m_ln_v_gain, ln_v_bias=m_ln_v_bias, w_spatial=m_w_spatial, b_spatial=m_b_spatial,
             sinks=m_sinks, w_a=m_w_a, w_b=m_w_b, w_o=m_w_o, norm_mix_pre=m_norm_mix_pre, norm_mix_post=m_norm_mix_post,
             w_ff_in=m_w_ff_in, w_ff_out=m_w_ff_out, norm_ff_pre=m_norm_ff_pre, norm_ff_post=m_norm_ff_post)
    v = dict(w_in=v_w_in, ln_v_gain=v_ln_v_gain, ln_v_bias=v_ln_v_bias, w_spatial=v_w_spatial, b_spatial=v_b_spatial,
             sinks=v_sinks, w_a=v_w_a, w_b=v_w_b, w_o=v_w_o, norm_mix_pre=v_norm_mix_pre, norm_mix_post=v_norm_mix_post,
             w_ff_in=v_w_ff_in, w_ff_out=v_w_ff_out, norm_ff_pre=v_norm_ff_pre, norm_ff_post=v_norm_ff_post)

    FIRST, REST = (0,), tuple(range(1, NBIG))
    shards = [w[n][0].astype(BF16) for n in BIG_NAMES]
    place = jnp.stack([lax.axis_index("c"), 2 * lax.axis_index("x") + lax.axis_index("y")]).astype(jnp.int32)
    xs, target = x[0], loss_target[0]
    T = xs.shape[0]
    tile = min(TOKEN_TILE, T)
    wtiles = {n: dict(tm=min(tm, T), tn=tn) for n, (tm, tn) in WGRAD_TILES.items()}
    g1, g2, g3, g4 = norm_mix_pre, norm_mix_post, norm_ff_pre, norm_ff_post
    w_sp, snk = w_spatial[0], sinks[0]
    MIX, FF = (1, 2, 3), (4, 5)
    bfull = jnp.repeat(b_spatial[0].T, CH, axis=1)

    def reduce_tail(ws, grads, got):
        tag = "_".join(BIG_NAMES[k] for k in ws)
        sums = _add_halves(place, grads, got, [BIG[k][0] for k in ws], name="grad_add_sibling_" + tag)
        return sums, _x_grads_chips([s[1] for s in sums], ws)

    def reduce_end(ws, sums, pieces):
        tag = "_".join(BIG_NAMES[k] for k in ws)
        return _add_pieces(place, [s[0] for s in sums], pieces, [BIG[k] for k in ws], name="grad_add_chips_" + tag)

    (rc, rs1, rs2), w_in_part = _rope_tables(positions[0], comm=_x_gather_ici(shards[:1], FIRST))
    w_in_b = _run(_x_gather_d2d(w_in_part, FIRST), "gather_w_in_d2d")[0]
    EARLY, FF_OUT = (1, 2, 3, 4), (5,)
    (h, u, vs, q, k, va, ga, gb), early_part = _inproj(xs, g1, w_in_b, rc, rs1, rs2, tm=tile,
                                                      comm=_x_gather_ici(shards[1:5], EARLY))
    att, (w_a_b, w_b_b, w_o_b, w_ff_in_b, ffo_part) = _attn_fwd(
        q, k, va, snk, comm=_both(_x_gather_d2d(early_part, EARLY), _x_gather_ici(shards[5:], FF_OUT)))
    (a, pa, pb, merged, mix, x1), (w_ff_out_b,) = _sgu_merge_fwd(
        u, vs, ln_v_gain, ln_v_bias, w_sp, bfull, att, ga, gb, xs, w_a_b, w_b_b, w_o_b, g2, tm=tile,
        comm=_x_gather_d2d([ffo_part], FF_OUT))
    hf, f2, dff, df1, dx1, lsum, dg3, dg4 = _ffn(x1, target, w_ff_in_b, w_ff_out_b, g3, g4, tm=tile)

    dw_ff_out, _ = _wgrad(f2, dff, name="wgrad_ff_out", **wtiles["w_ff_out"])
    dw_ff_in, _ = _wgrad(hf, df1, name="wgrad_ff_in", **wtiles["w_ff_in"])
    grads_ff = [dw_ff_in, dw_ff_out]
    (dga, dgb, da, datt, dg2, dw_a, dw_b, dw_o), got_ff = _merge_bwd(
        dx1, mix, ga, gb, pa, pb, a, att, merged, w_a_b, w_b_b, w_o_b, g2, tm=tile, comm=_x_grads_sibling(grads_ff, FF))
    grads_mix = [dw_a, dw_b, dw_o]
    sums_ff, ff_to_chips = reduce_tail(FF, grads_ff, got_ff)
    (dq, dk, dva, dsk), (*pieces_ff, got_a, got_b, got_o) = _attn_bwd(
        q, k, va, datt, snk, rc, rs1, rs2, comm=_both(ff_to_chips, _x_grads_sibling(grads_mix, MIX)))
    partial_ff = reduce_end(FF, sums_ff, pieces_ff)
    sums_mix, mix_to_chips = reduce_tail(MIX, grads_mix, [got_a, got_b, got_o])
    dx, dproj, dg1, dws, dbs, dlg, dlb = _sgu_inproj_bwd(
        u, vs, da, ln_v_gain, ln_v_bias, w_sp, bfull, [dq, dk, dva, dga, dgb], xs, dx1, g1, w_in_b, tm=tile // 2)
    small = dict(ln_v_gain=dlg, ln_v_bias=dlb, w_spatial=dws, b_spatial=dbs, sinks=dsk[:, :NQ],
                 norm_mix_pre=dg1, norm_mix_post=dg2, norm_ff_pre=dg3, norm_ff_post=dg4)
    dw_in, (gs, shard_ff_in, shard_ff_out, *pieces_mix) = _wgrad(
        h, dproj, name="wgrad_in", vmem=VMEM_PHYSICAL, **wtiles["w_in"],
        comm=_both(_both(_x_small_all_reduce(_pack_small(small, lsum)), _x_grads_share(partial_ff, FF)), mix_to_chips))
    partial_mix = reduce_end(MIX, sums_mix, pieces_mix)
    got_in = _run(_x_grads_sibling([dw_in], FIRST), "grads_in_to_sibling")
    sums_in, to_chips = reduce_tail(FIRST, [dw_in], got_in)
    partial_in = reduce_end(FIRST, sums_in, _run(to_chips, "grads_in_to_chips"))
    g_in, *shard_mix = _run(_x_grads_share(list(partial_in) + list(partial_mix), FIRST + MIX), "grads_in_mix_share")
    shard_rest = list(shard_mix) + [shard_ff_in, shard_ff_out]

    loss = 0.5 * jnp.sum(gs[SMALL_ROWS - 8:]) / D
    grad, delta, new_m, new_v = {}, {}, {}, {}
    for n, g in zip(BIG_NAMES, [g_in] + list(shard_rest)):
        (g_, d_, m_, v_), = _adamw([w[n][0]], [g], [m[n][0]], [v[n][0]], [256], name="adamw_" + n)
        grad[n], delta[n], new_m[n], new_v[n] = g_[None], d_[None], m_[None], v_[None]
    (gs, ds, ms, vs), = _adamw([_pack_small(w)], [gs], [_pack_small(m)], [_pack_small(v)], [SMALL_ROWS], name="adamw_small")
    for packed, dst in ((gs, grad), (ds, delta), (ms, new_m), (vs, new_v)):
        dst.update(_unpack_small(packed, w))

    outs = [loss, dx[None]]
    for group in (grad, delta, new_m, new_v):
        outs.extend(group[n] for n in WEIGHT_ORDER)
    return tuple(outs)
```

```python
import functools

import jax
import jax.numpy as jnp
from jax import lax
from jax.experimental import pallas as pl
from jax.experimental.pallas import tpu as pltpu

F32 = jnp.float32
BF16 = jnp.bfloat16

D = 1024
CH = 128
NG = 8
HD = 64
NQ = 16
NKV = 4
KVW = NKV * HD
DFF = 4 * D
EPS = 1e-6
IN_W = 5632
SEG = (0, 1024, 2048, 3072, 3328, 3584, 4608, 5632)
ROPE_HALF = 8
Q_SCALE = HD ** -0.5

LR, B1, B2, AEPS, WD, STEP = 0.001, 0.9, 0.999, 1e-08, 0.01, 10

VMEM_PHYSICAL = 64 * 1024 * 1024
VMEM_LIMIT = 60 * 1024 * 1024
MESH = pl.DeviceIdType.MESH

TOKEN_TILE = 512
WGRAD_TILES = {"w_ff_out": (512, 1024), "w_ff_in": (2048, 2048), "w_in": (2048, IN_W // 2)}

_GELU_C0 = 0.7978845608028654
_GELU_C1 = 0.044715


def _cparams(sem=None, vmem=None):
    kw = dict(vmem_limit_bytes=VMEM_LIMIT if vmem is None else vmem)
    if sem is not None:
        kw["dimension_semantics"] = sem
    return pltpu.CompilerParams(**kw)


def _resident(shape):
    nd = len(shape)
    return pl.BlockSpec(shape, lambda *_: (0,) * nd, pipeline_mode=pl.Buffered(1))


def _const(shape):
    nd = len(shape)
    return pl.BlockSpec(shape, lambda *_: (0,) * nd)


def _rows(tm, w):
    return pl.BlockSpec((tm, w), lambda i: (i, 0))


class _Exchange:
    def __init__(self, ins, outs, aliases, scratch, start, finish):
        self.ins, self.outs, self.aliases, self.scratch = list(ins), list(outs), dict(aliases), list(scratch)
        self.start, self.finish = start, finish


def _both(a, b):
    na, ma, sa = len(a.ins), len(a.outs), len(a.scratch)

    def start(ci, co, cs):
        a.start(ci[:na], co[:ma], cs[:sa])
        b.start(ci[na:], co[ma:], cs[sa:])

    def finish(ci, co, cs):
        a.finish(ci[:na], co[:ma], cs[:sa])
        b.finish(ci[na:], co[ma:], cs[sa:])

    aliases = {**a.aliases, **{na + i: ma + j for i, j in b.aliases.items()}}
    return _Exchange(a.ins + b.ins, a.outs + b.outs, aliases, a.scratch + b.scratch, start, finish)


def _call(body, args, *, name, grid, in_specs, out_specs, out_shape, scratch_shapes=(), sem=None, comm=None, vmem=None):
    single = not isinstance(out_shape, (list, tuple))
    out_shape = [out_shape] if single else list(out_shape)
    out_specs = [out_specs] if single else list(out_specs)
    if comm is None:
        res = pl.pallas_call(body, name=name, grid=grid, in_specs=list(in_specs), out_specs=out_specs,
                             out_shape=out_shape, scratch_shapes=list(scratch_shapes),
                             compiler_params=_cparams(sem, vmem))(*args)
        return (res[0] if single else res), []
    n_in, n_out, n_scr = len(args), len(out_shape), len(scratch_shapes)
    nci, nco = len(comm.ins), len(comm.outs)
    steps = 1
    for g in grid:
        steps *= g

    def hosted(*refs):
        a, ci = refs[:n_in], refs[n_in:n_in + nci]
        o, co = refs[n_in + nci:n_in + nci + n_out], refs[n_in + nci + n_out:n_in + nci + n_out + nco]
        rest = refs[n_in + nci + n_out + nco:]
        scr, cs = rest[:n_scr], rest[n_scr:]
        step = pl.program_id(0)
        for d in range(1, len(grid)):
            step = step * grid[d] + pl.program_id(d)

        @pl.when(step == 0)
        def _():
            comm.start(ci, co, cs)

        body(*a, *o, *scr)

        @pl.when(step == steps - 1)
        def _():
            comm.finish(ci, co, cs)

    res = pl.pallas_call(
        hosted, name=name, grid=grid, in_specs=list(in_specs) + [ANY] * nci, out_specs=out_specs + [ANY] * nco,
        out_shape=out_shape + comm.outs, scratch_shapes=list(scratch_shapes) + comm.scratch,
        input_output_aliases={n_in + i: n_out + j for i, j in comm.aliases.items()},
        compiler_params=_cparams(("arbitrary",) * len(grid), vmem),
    )(*args, *comm.ins)
    own = res[:n_out]
    return (own[0] if single else own), list(res[n_out:])


def _run(comm, name):
    nci = len(comm.ins)

    def body(*refs):
        ci, co, cs = refs[:nci], refs[nci:nci + len(comm.outs)], refs[nci + len(comm.outs):]
        comm.start(ci, co, cs)
        comm.finish(ci, co, cs)

    return pl.pallas_call(
        body, name=name, in_specs=[ANY] * nci, out_specs=[ANY] * len(comm.outs), out_shape=comm.outs,
        scratch_shapes=comm.scratch, input_output_aliases=comm.aliases,
        compiler_params=pltpu.CompilerParams(vmem_limit_bytes=VMEM_LIMIT),
    )(*comm.ins)


def _gelu(x):
    x2 = x * x
    t = jnp.tanh(x * (_GELU_C0 + (_GELU_C0 * _GELU_C1) * x2))
    hx = 0.5 * x
    return hx + hx * t, (t, x2, hx)


def _gelu_grad(parts):
    t, x2, hx = parts
    return (0.5 + 0.5 * t) + hx * (1.0 - t * t) * (_GELU_C0 + (3.0 * _GELU_C0 * _GELU_C1) * x2)


def _sigmoid(x):
    return 1.0 / (1.0 + jnp.exp(-x))


def _rms_hat(x):
    r = lax.rsqrt(jnp.mean(x * x, axis=-1, keepdims=True) + EPS)
    return x * r, r


def _rms_bwd(xhat, r, g, dout):
    dg = jnp.sum(dout * xhat, axis=0, keepdims=True)
    dy = dout * g
    dx = r * (dy - xhat * jnp.mean(dy * xhat, axis=-1, keepdims=True))
    return dx, dg


def _dot(a, b):
    return jnp.dot(a, b, preferred_element_type=F32)


def _dot_nt(a, b):
    return lax.dot_general(a, b, (((1,), (1,)), ((), ())), preferred_element_type=F32)


def _dot_tn(a, b):
    return lax.dot_general(a, b, (((0,), (0,)), ((), ())), preferred_element_type=F32)


def _rope(blk, c, s1, s2):
    return blk * c + pltpu.roll(blk, CH - ROPE_HALF, 1) * s1 + pltpu.roll(blk, ROPE_HALF, 1) * s2


def _rope_t(blk, c, s1, s2):
    return blk * c + pltpu.roll(blk * s1, ROPE_HALF, 1) + pltpu.roll(blk * s2, CH - ROPE_HALF, 1)


def _inproj(x, g1, w_in, rc, rs1, rs2, tm, comm=None):
    T = x.shape[0]

    def body(x_ref, g_ref, w_ref, c_ref, s1_ref, s2_ref,
             h_ref, u_ref, v_ref, q_ref, k_ref, va_ref, ga_ref, gb_ref):
        xhat, _ = _rms_hat(x_ref[...])
        h = (xhat * g_ref[...]).astype(BF16)
        h_ref[...] = h
        uv = _dot(h, w_ref[:, SEG[0]:SEG[2]])
        uv = uv.astype(BF16)
        u_ref[...] = uv[:, :D]
        v_ref[...] = uv[:, D:]
        c, s1, s2 = c_ref[...], s1_ref[...], s2_ref[...]
        qkv = _dot(h, w_ref[:, SEG[2]:SEG[5]])
        for p in range(D // CH):
            blk = _rope(qkv[:, CH * p:CH * (p + 1)], c, s1, s2) * Q_SCALE
            q_ref[:, CH * p:CH * (p + 1)] = blk.astype(BF16)
        for p in range(KVW // CH):
            k_ref[:, CH * p:CH * (p + 1)] = _rope(qkv[:, D + CH * p:D + CH * (p + 1)], c, s1, s2).astype(BF16)
        va_ref[...] = qkv[:, D + KVW:].astype(BF16)
        gates = _dot(h, w_ref[:, SEG[5]:SEG[7]]).astype(BF16)
        ga_ref[...] = gates[:, :D]
        gb_ref[...] = gates[:, D:]

    sd = jax.ShapeDtypeStruct
    return _call(
        body, (x, g1, w_in, rc, rs1, rs2), name="inproj_fwd", grid=(T // tm,),
        in_specs=[_rows(tm, D), _const((1, D)), _resident((D, IN_W)), _rows(tm, CH), _rows(tm, CH), _rows(tm, CH)],
        out_specs=[_rows(tm, D), _rows(tm, D), _rows(tm, D), _rows(tm, D), _rows(tm, KVW), _rows(tm, KVW),
                   _rows(tm, D), _rows(tm, D)],
        out_shape=[sd((T, D), BF16), sd((T, D), BF16), sd((T, D), BF16), sd((T, D), BF16), sd((T, KVW), BF16),
                   sd((T, KVW), BF16), sd((T, D), BF16), sd((T, D), BF16)],
        sem=("parallel",), comm=comm)


def _sgu_common(u, vs, lng, lnb, ws_ref, bfull):
    nc = u.shape[0] // CH
    ug, tu = _gelu(u)
    vg, tv = _gelu(vs)
    mu = jnp.mean(vg, axis=-1, keepdims=True)
    xc = vg - mu
    rstd = lax.rsqrt(jnp.mean(xc * xc, axis=-1, keepdims=True) + EPS)
    vhat = xc * rstd
    vnb = (vhat * lng + lnb).astype(BF16)
    tri = lax.broadcasted_iota(jnp.int32, (CH, CH), 0) >= lax.broadcasted_iota(jnp.int32, (CH, CH), 1)
    wts, rhss, mixed = [], [], []
    for g in range(NG):
        wt = jnp.where(tri, ws_ref[g], 0.0).astype(BF16)
        rhs = jnp.concatenate([vnb[CH * c:CH * (c + 1), CH * g:CH * (g + 1)] for c in range(nc)], axis=1)
        mix = _dot(wt, rhs)
        wts.append(wt)
        rhss.append(rhs)
        mixed.append([mix[:, CH * c:CH * (c + 1)] + bfull[:, CH * g:CH * (g + 1)] for c in range(nc)])
    return nc, ug, tu, tv, rstd, vhat, tri, wts, rhss, mixed


def _pair_layout(prev, cur, grp):
    j, half = grp // 2, grp % 2
    blk = jnp.concatenate([prev[:, CH * j:CH * (j + 1)], cur[:, CH * j:CH * (j + 1)]], axis=0).astype(F32)
    lo = lax.broadcasted_iota(jnp.int32, blk.shape, 1) < HD
    rolled = pltpu.roll(blk, HD, 1)
    even = jnp.where(lo, blk if half == 0 else rolled, 0.0)
    odd = jnp.where(lo, 0.0, rolled if half == 0 else blk)
    return jnp.concatenate([even, odd], axis=0).astype(BF16)


def _attn_mask(n):
    qi = lax.broadcasted_iota(jnp.int32, (CH, 2 * CH), 0)
    kc = lax.broadcasted_iota(jnp.int32, (CH, 2 * CH), 1)
    ok = (kc > qi) & (kc <= qi + CH) & ((kc >= CH) | (n > 0))
    return jnp.concatenate([ok, ok], axis=1)


def _softmax_sink(s, sink):
    m = jnp.maximum(jnp.max(s, axis=-1, keepdims=True), sink)
    p = jnp.exp(s - m)
    ps = jnp.exp(sink - m)
    inv = 1.0 / (jnp.sum(p, axis=-1, keepdims=True) + ps)
    return p * inv, ps * inv


QUERY_BLOCKS_PER_STEP = 2


def _attn_fwd(q, k, va, sinks, comm=None):
    T = q.shape[0]
    nblk = QUERY_BLOCKS_PER_STEP
    nsteps = T // (nblk * CH)
    npairs = D // CH

    def body(sk_ref, q_ref, kp_ref, kc_ref, vp_ref, vc_ref, o_ref):
        n = pl.program_id(0)
        even_lanes = lax.broadcasted_iota(jnp.int32, (CH, CH), 1) < HD
        ks = [kp_ref[...]] + [kc_ref[CH * b:CH * (b + 1)] for b in range(nblk)]
        vs = [vp_ref[...]] + [vc_ref[CH * b:CH * (b + 1)] for b in range(nblk)]
        masks = [_attn_mask(nblk * n)] + [_attn_mask(1)] * (nblk - 1)
        kks = [[_pair_layout(ks[b], ks[b + 1], grp) for grp in range(NKV)] for b in range(nblk)]
        vvs = [[_pair_layout(vs[b], vs[b + 1], grp) for grp in range(NKV)] for b in range(nblk)]
        work = [(b, p) for b in range(nblk) for p in range(npairs)]

        def scores(i):
            b, p = work[i]
            return _dot_nt(q_ref[CH * b:CH * (b + 1), CH * p:CH * (p + 1)], kks[b][p // 2])

        def unnormalised(s, sink):
            m = jnp.maximum(jnp.max(s, axis=-1, keepdims=True), sink)
            p = jnp.exp(s - m)
            return p, 1.0 / (jnp.sum(p, axis=-1, keepdims=True) + jnp.exp(sink - m))

        def value_product(i):
            b, p = work[i]
            pr, ie, io = probs[i]
            return _dot(pr, vvs[b][p // 2]) * jnp.where(even_lanes, ie, io)

        ahead = 3
        outs, probs = [], []
        pending = [scores(i) for i in range(ahead)]
        for i, (b, p) in enumerate(work):
            s = jnp.where(masks[b], pending.pop(0), -1e30)
            if i + ahead < len(work):
                pending.append(scores(i + ahead))
            pe, ie = unnormalised(s[:, :2 * CH], sk_ref[2 * p])
            po, io = unnormalised(s[:, 2 * CH:], sk_ref[2 * p + 1])
            probs.append((jnp.concatenate([pe, po], axis=1).astype(BF16), ie, io))
            if i >= 1:
                outs.append(value_product(i - 1))
        outs.append(value_product(len(work) - 1))
        for b in range(nblk):
            o_ref[CH * b:CH * (b + 1), :] = jnp.concatenate(outs[npairs * b:npairs * (b + 1)], axis=1).astype(BF16)

    prev = lambda n: (jnp.maximum(nblk * n - 1, 0), 0)
    cur = lambda n: (n, 0)
    return _call(
        body, (sinks, q, k, k, va, va), name="attn_fwd", grid=(nsteps,),
        in_specs=[pl.BlockSpec(memory_space=pltpu.SMEM), pl.BlockSpec((nblk * CH, D), cur),
                  pl.BlockSpec((CH, KVW), prev), pl.BlockSpec((nblk * CH, KVW), cur),
                  pl.BlockSpec((CH, KVW), prev), pl.BlockSpec((nblk * CH, KVW), cur)],
        out_specs=pl.BlockSpec((nblk * CH, D), cur), out_shape=jax.ShapeDtypeStruct((T, D), BF16),
        sem=("parallel",), comm=comm)


def _attn_bwd(q, k, va, datt, sinks, rc, rs1, rs2, comm=None):
    T = q.shape[0]
    nb = T // CH

    def body(sk_ref, q_ref, kp_ref, kc_ref, vp_ref, vc_ref, do_ref, cq_ref, s1q_ref, s2q_ref, ck_ref, s1k_ref, s2k_ref,
             dq_ref, dk_ref, dv_ref, dsk_ref, kcar, vcar):
        n = pl.program_id(0)

        @pl.when(n == 0)
        def _():
            kcar[...] = jnp.zeros_like(kcar)
            vcar[...] = jnp.zeros_like(vcar)
            dsk_ref[...] = jnp.zeros_like(dsk_ref)

        def flush(kprev, vprev):
            ck, s1k, s2k = ck_ref[...], s1k_ref[...], s2k_ref[...]
            for j in range(KVW // CH):
                sl = slice(CH * j, CH * (j + 1))
                dk_ref[:, sl] = _rope_t(kcar[:, sl] + kprev[:, sl], ck, s1k, s2k).astype(BF16)
                dv_ref[:, sl] = (vcar[:, sl] + vprev[:, sl]).astype(BF16)

        @pl.when(n < nb)
        def _():
            mask = _attn_mask(n)
            kp, kc, vp, vc = kp_ref[...], kc_ref[...], vp_ref[...], vc_ref[...]
            cq, s1q, s2q = cq_ref[...], s1q_ref[...], s2q_ref[...]
            lane = lax.broadcasted_iota(jnp.int32, (1, CH), 1)
            dsk = jnp.zeros((1, CH), F32)
            npairs = D // CH
            kks = [_pair_layout(kp, kc, grp) for grp in range(NKV)]
            vvs = [_pair_layout(vp, vc, grp) for grp in range(NKV)]
            qs = [q_ref[:, CH * p:CH * (p + 1)] for p in range(npairs)]
            dos = [do_ref[:, CH * p:CH * (p + 1)].astype(BF16) for p in range(npairs)]

            def first(p):
                return _dot_nt(qs[p], kks[p // 2]), _dot_nt(dos[p], vvs[p // 2])

            def last(p, ds, pb):
                return (_rope_t(_dot(ds, kks[p // 2]), cq, s1q, s2q) * Q_SCALE, _dot_tn(qs[p], ds), _dot_tn(dos[p], pb))

            ahead = 2
            pending = [first(p) for p in range(ahead)]
            mids, ends = [], []
            for p in range(npairs):
                s, dp = pending.pop(0)
                s = jnp.where(mask, s, -1e30)
                if p + ahead < npairs:
                    pending.append(first(p + ahead))
                ds_parts, p_parts = [], []
                for par in range(2):
                    sl = slice(2 * CH * par, 2 * CH * (par + 1))
                    pr, psink = _softmax_sink(s[:, sl], sk_ref[2 * p + par])
                    delta = jnp.sum(pr * dp[:, sl], axis=-1, keepdims=True)
                    ds_parts.append(pr * (dp[:, sl] - delta))
                    p_parts.append(pr)
                    tot = -jnp.sum(psink * delta, axis=0, keepdims=True)
                    dsk = dsk + jnp.where(lane == 2 * p + par, tot, 0.0)
                mids.append((jnp.concatenate(ds_parts, axis=1).astype(BF16), jnp.concatenate(p_parts, axis=1).astype(BF16)))
                if p >= 1:
                    ends.append(last(p - 1, *mids[p - 1]))
            ends.append(last(npairs - 1, *mids[-1]))
            dq_cols = [e[0] for e in ends]
            def fold(i):
                rows = []
                for grp in range(NKV):
                    acc = ends[2 * grp][i] + ends[2 * grp + 1][i]
                    rows.append(acc[:HD, :2 * CH] + acc[HD:, 2 * CH:])
                return jnp.concatenate(rows, axis=0).T

            dkf, dvf = fold(1), fold(2)
            dq_ref[...] = jnp.concatenate(dq_cols, axis=1).astype(BF16)
            dsk_ref[...] += dsk
            flush(dkf[:CH], dvf[:CH])
            kcar[...] = dkf[CH:]
            vcar[...] = dvf[CH:]

        @pl.when(n == nb)
        def _():
            z = jnp.zeros((CH, KVW), F32)
            flush(z, z)

    last = nb - 1
    cur = lambda n: (jnp.minimum(n, last), 0)
    prev = lambda n: (jnp.clip(n - 1, 0, last), 0)
    sd = jax.ShapeDtypeStruct
    return _call(
        body, (sinks, q, k, k, va, va, datt, rc, rs1, rs2, rc, rs1, rs2), name="attn_bwd", grid=(nb + 1,),
        in_specs=[pl.BlockSpec(memory_space=pltpu.SMEM), pl.BlockSpec((CH, D), cur),
                  pl.BlockSpec((CH, KVW), prev), pl.BlockSpec((CH, KVW), cur),
                  pl.BlockSpec((CH, KVW), prev), pl.BlockSpec((CH, KVW), cur),
                  pl.BlockSpec((CH, D), cur),
                  pl.BlockSpec((CH, CH), cur), pl.BlockSpec((CH, CH), cur), pl.BlockSpec((CH, CH), cur),
                  pl.BlockSpec((CH, CH), prev), pl.BlockSpec((CH, CH), prev), pl.BlockSpec((CH, CH), prev)],
        out_specs=[pl.BlockSpec((CH, D), cur), pl.BlockSpec((CH, KVW), prev), pl.BlockSpec((CH, KVW), prev),
                   _const((1, CH))],
        out_shape=[sd((T, D), BF16), sd((T, KVW), BF16), sd((T, KVW), BF16), sd((1, CH), F32)],
        scratch_shapes=[pltpu.VMEM((CH, KVW), F32), pltpu.VMEM((CH, KVW), F32)], sem=("arbitrary",), comm=comm)


def _sgu_merge_fwd(u, vs, lng, lnb, ws, bfull, att, ga, gb, x, w_a, w_b, w_o, g2, tm, comm=None):
    T = x.shape[0]

    def body(u_ref, v_ref, lng_ref, lnb_ref, ws_ref, bf_ref, att_ref, ga_ref, gb_ref, x_ref, wa_ref, wb_ref, wo_ref, g_ref,
             a_ref, pa_ref, pb_ref, mg_ref, mix_ref, x1_ref):
        pb = _dot(att_ref[...], wb_ref[...])
        nc, ug, _, _, _, _, _, _, _, mixed = _sgu_common(
            u_ref[...].astype(F32), v_ref[...].astype(F32), lng_ref[...], lnb_ref[...], ws_ref, bf_ref[...])
        mixed_all = jnp.concatenate(
            [jnp.concatenate([mixed[g][c] for g in range(NG)], axis=1) for c in range(nc)], axis=0)
        a = (ug * mixed_all).astype(BF16)
        a_ref[...] = a
        pa = _dot(a, wa_ref[...])
        pa_ref[...] = pa.astype(BF16)
        pb_ref[...] = pb.astype(BF16)
        merged = (_sigmoid(ga_ref[...].astype(F32)) * pa + _sigmoid(gb_ref[...].astype(F32)) * pb).astype(BF16)
        mg_ref[...] = merged
        mix = _dot(merged, wo_ref[...])
        mix_ref[...] = mix
        mhat, _ = _rms_hat(mix)
        x1_ref[...] = x_ref[...] + mhat * g_ref[...]

    sd = jax.ShapeDtypeStruct
    return _call(
        body, (u, vs, lng, lnb, ws, bfull, att, ga, gb, x, w_a, w_b, w_o, g2), name="sgu_merge_fwd", grid=(T // tm,),
        in_specs=[_rows(tm, D), _rows(tm, D), _const((1, D)), _const((1, D)), _const((NG, CH, CH)), _const((CH, D))]
        + [_rows(tm, D)] * 4 + [_resident((D, D))] * 3 + [_const((1, D))],
        out_specs=[_rows(tm, D)] * 6,
        out_shape=[sd((T, D), BF16)] * 4 + [sd((T, D), F32)] * 2,
        sem=("parallel",), comm=comm, vmem=VMEM_PHYSICAL)


def _merge_bwd(dx1, mix, ga, gb, pa, pb, a, att, merged, w_a, w_b, w_o, g2, tm, comm=None):
    T = dx1.shape[0]
    nsteps = T // tm

    def body(dx1_ref, mix_ref, ga_ref, gb_ref, pa_ref, pb_ref, a_ref, att_ref, mg_ref, wa_ref, wb_ref, wo_ref, g_ref,
             dga_ref, dgb_ref, da_ref, datt_ref, dg_ref, dwa_ref, dwb_ref, dwo_ref, acc, sem):
        i = pl.program_id(0)

        @pl.when(i == 0)
        def _():
            dg_ref[...] = jnp.zeros_like(dg_ref)
            acc[...] = jnp.zeros_like(acc)

        mhat, r = _rms_hat(mix_ref[...])
        dmix, dg = _rms_bwd(mhat, r, g_ref[...], dx1_ref[...])
        dg_ref[...] += dg
        dmix = dmix.astype(BF16)
        dmerged = _dot_nt(dmix, wo_ref[...])
        sa = _sigmoid(ga_ref[...].astype(F32))
        sb = _sigmoid(gb_ref[...].astype(F32))
        dao = (dmerged * sa).astype(BF16)
        dbo = (dmerged * sb).astype(BF16)
        dga_ref[...] = (dmerged * pa_ref[...].astype(F32) * (sa * (1.0 - sa))).astype(BF16)
        dgb_ref[...] = (dmerged * pb_ref[...].astype(F32) * (sb * (1.0 - sb))).astype(BF16)
        da_ref[...] = _dot_nt(dao, wa_ref[...])
        datt_ref[...] = _dot_nt(dbo, wb_ref[...]).astype(BF16)
        acc[0] += _dot_tn(a_ref[...], dao)
        acc[1] += _dot_tn(att_ref[...], dbo)
        acc[2] += _dot_tn(mg_ref[...], dmix)

        @pl.when(i == nsteps - 1)
        def _():
            outs = [pltpu.make_async_copy(acc.at[j], ref, sem.at[j]) for j, ref in enumerate((dwa_ref, dwb_ref, dwo_ref))]
            for cp in outs:
                cp.start()
            for cp in outs:
                cp.wait()

    sd = jax.ShapeDtypeStruct
    return _call(
        body, (dx1, mix, ga, gb, pa, pb, a, att, merged, w_a, w_b, w_o, g2), name="merge_bwd", grid=(nsteps,),
        in_specs=[_rows(tm, D)] * 9 + [_resident((D, D))] * 3 + [_const((1, D))],
        out_specs=[_rows(tm, D)] * 4 + [_const((1, D))] + [ANY] * 3,
        out_shape=[sd((T, D), BF16), sd((T, D), BF16), sd((T, D), F32), sd((T, D), BF16), sd((1, D), F32)]
        + [sd((D, D), F32)] * 3,
        scratch_shapes=[pltpu.VMEM((3, D, D), F32), _dma_sems(3)], sem=("arbitrary",), comm=comm)


def _ffn(x1, target, w1, w2, g3, g4, tm):
    T = x1.shape[0]

    def body(x_ref, t_ref, w1_ref, w2_ref, g3_ref, g4_ref,
             hf_ref, f2_ref, dff_ref, df1_ref, dx_ref, ls_ref, dg3_ref, dg4_ref):
        @pl.when(pl.program_id(0) == 0)
        def _():
            ls_ref[...] = jnp.zeros_like(ls_ref)
            dg3_ref[...] = jnp.zeros_like(dg3_ref)
            dg4_ref[...] = jnp.zeros_like(dg4_ref)

        x = x_ref[...]
        g3, g4 = g3_ref[...], g4_ref[...]
        xhat, r3 = _rms_hat(x)
        hf = (xhat * g3).astype(BF16)
        hf_ref[...] = hf
        rl = jnp.maximum(_dot(hf, w1_ref[...]), 0.0)
        f2 = (rl * rl).astype(BF16)
        f2_ref[...] = f2
        fhat, r4 = _rms_hat(_dot(f2, w2_ref[...]))
        err = x + fhat * g4 - t_ref[...]
        ls_ref[...] += jnp.sum(err * err, axis=0, keepdims=True)
        dy = err * (1.0 / D)
        dff, dg4 = _rms_bwd(fhat, r4, g4, dy)
        dg4_ref[...] += dg4
        dff = dff.astype(BF16)
        dff_ref[...] = dff
        df1 = (_dot_nt(dff, w2_ref[...]) * (2.0 * rl)).astype(BF16)
        df1_ref[...] = df1
        dxn, dg3 = _rms_bwd(xhat, r3, g3, _dot_nt(df1, w1_ref[...]))
        dg3_ref[...] += dg3
        dx_ref[...] = dy + dxn

    sd = jax.ShapeDtypeStruct
    return pl.pallas_call(
        body, name="ffn_fwd_bwd", grid=(T // tm,),
        in_specs=[_rows(tm, D), _rows(tm, D), _resident((D, DFF)), _resident((DFF, D)), _const((1, D)), _const((1, D))],
        out_specs=[_rows(tm, D), _rows(tm, DFF), _rows(tm, D), _rows(tm, DFF), _rows(tm, D), _const((1, D)),
                   _const((1, D)), _const((1, D))],
        out_shape=[sd((T, D), BF16), sd((T, DFF), BF16), sd((T, D), BF16), sd((T, DFF), BF16), sd((T, D), F32),
                   sd((1, D), F32), sd((1, D), F32), sd((1, D), F32)],
        compiler_params=pltpu.CompilerParams(vmem_limit_bytes=VMEM_PHYSICAL, dimension_semantics=("arbitrary",)),
    )(x1, target, w1, w2, g3, g4)


def _sgu_inproj_bwd(u, vs, da, lng, lnb, ws, bfull, parts, x, dx1, g1, w_in, tm):
    T = x.shape[0]
    nsteps = T // tm
    widths = [p.shape[1] for p in parts]
    offs = [2 * D + sum(widths[:i]) for i in range(len(widths) + 1)]
    assert offs[-1] == IN_W
    n = len(parts)

    def body(*refs):
        u_ref, v_ref, da_ref, lng_ref, lnb_ref, ws_ref, bf_ref = refs[:7]
        prefs = refs[7:7 + n]
        x_ref, dx1_ref, g_ref, w_ref = refs[7 + n:11 + n]
        dx_ref, dp_ref, dg_ref, dws_ref, dbs_ref, dlg_ref, dlb_ref, db_ref = refs[11 + n:]
        i = pl.program_id(0)

        @pl.when(i == 0)
        def _():
            for ref in (dg_ref, dws_ref, db_ref, dlg_ref, dlb_ref):
                ref[...] = jnp.zeros_like(ref)

        for j in range(n):
            dp_ref[:, offs[j]:offs[j + 1]] = prefs[j][...]
        cut = 2 * D + (IN_W - 2 * D) // 2
        dh_a = _dot_nt(dp_ref[:, 2 * D:cut], w_ref[:, 2 * D:cut])
        u, vs, da, lng = u_ref[...].astype(F32), v_ref[...].astype(F32), da_ref[...], lng_ref[...]
        nc, ug, tu, tv, rstd, vhat, tri, wts, rhss, mixed = _sgu_common(u, vs, lng, lnb_ref[...], ws_ref, bf_ref[...])
        mixed_all = jnp.concatenate(
            [jnp.concatenate([mixed[g][c] for g in range(NG)], axis=1) for c in range(nc)], axis=0)
        dp_ref[:, :D] = (da * mixed_all * _gelu_grad(tu)).astype(BF16)
        dh_u = _dot_nt(dp_ref[:, :D], w_ref[:, :D])
        dmixed = da * ug
        dvn_cols = []
        for g in range(NG):
            dmix = [dmixed[CH * c:CH * (c + 1), CH * g:CH * (g + 1)] for c in range(nc)]
            db_ref[:, CH * g:CH * (g + 1)] += functools.reduce(lambda a, b: a + b, dmix)
            dm = jnp.concatenate(dmix, axis=1).astype(BF16)
            dws_ref[g] += _dot_nt(dm, rhss[g])
            dvn_cols.append(_dot_tn(wts[g], dm))
        dh_b = _dot_nt(dp_ref[:, cut:], w_ref[:, cut:])
        dvn = jnp.concatenate(
            [jnp.concatenate([dvn_cols[g][:, CH * c:CH * (c + 1)] for g in range(NG)], axis=1) for c in range(nc)],
            axis=0)
        dlg_ref[...] += jnp.sum(dvn * vhat, axis=0, keepdims=True)
        dlb_ref[...] += jnp.sum(dvn, axis=0, keepdims=True)
        dvh = dvn * lng
        dvg = rstd * (dvh - jnp.mean(dvh, axis=-1, keepdims=True)
                      - vhat * jnp.mean(dvh * vhat, axis=-1, keepdims=True))
        dp_ref[:, D:2 * D] = (dvg * _gelu_grad(tv)).astype(BF16)

        dh = (dh_a + dh_u) + (dh_b + _dot_nt(dp_ref[:, D:2 * D], w_ref[:, D:2 * D]))
        xhat, r = _rms_hat(x_ref[...])
        dxn, dg = _rms_bwd(xhat, r, g_ref[...], dh)
        dg_ref[...] += dg
        dx_ref[...] = dx1_ref[...] + dxn

        @pl.when(i == nsteps - 1)
        def _():
            for g in range(NG):
                dws_ref[g] = jnp.where(tri, dws_ref[g], 0.0)
                dbs_ref[g:g + 1, :] = jnp.sum(db_ref[:, CH * g:CH * (g + 1)].T, axis=0, keepdims=True)

    sd = jax.ShapeDtypeStruct
    outs, _ = _call(
        body, (u, vs, da, lng, lnb, ws, bfull, *parts, x, dx1, g1, w_in), name="sgu_inproj_bwd", grid=(nsteps,),
        in_specs=[_rows(tm, D), _rows(tm, D), _rows(tm, D), _const((1, D)), _const((1, D)), _const((NG, CH, CH)),
                  _const((CH, D))] + [_rows(tm, w) for w in widths]
        + [_rows(tm, D), _rows(tm, D), _const((1, D)), _resident((D, IN_W))],
        out_specs=[_rows(tm, D), _rows(tm, IN_W), _const((1, D)), _const((NG, CH, CH)), _const((NG, CH)), _const((1, D)),
                   _const((1, D))],
        out_shape=[sd((T, D), F32), sd((T, IN_W), BF16), sd((1, D), F32), sd((NG, CH, CH), F32), sd((NG, CH), F32),
                   sd((1, D), F32), sd((1, D), F32)],
        scratch_shapes=[pltpu.VMEM((CH, D), F32)], sem=("arbitrary",), vmem=VMEM_PHYSICAL)
    return outs


def _wgrad(a, g, tn, tm, name, comm=None, vmem=None):
    T, K = a.shape
    N = g.shape[1]

    def body(a_ref, g_ref, o_ref):
        @pl.when(pl.program_id(1) == 0)
        def _():
            o_ref[...] = jnp.zeros_like(o_ref)

        o_ref[...] += _dot_tn(a_ref[...], g_ref[...])

    return _call(
        body, (a, g), name=name, grid=(N // tn, T // tm),
        in_specs=[pl.BlockSpec((tm, K), lambda j, t: (t, 0)), pl.BlockSpec((tm, tn), lambda j, t: (t, j))],
        out_specs=pl.BlockSpec((K, tn), lambda j, t: (0, j)),
        out_shape=jax.ShapeDtypeStruct((K, N), F32), sem=("parallel", "arbitrary"), comm=comm, vmem=vmem)


def _adamw(ws, gs, ms, vs, trs, name):
    n = len(ws)
    walk = _Walk(w.shape[0] // tr for w, tr in zip(ws, trs))
    bc1 = 1.0 / (1.0 - B1 ** STEP)
    bc2 = 1.0 / (1.0 - B2 ** STEP)

    def body(*refs):
        i = pl.program_id(0)
        for k in range(n):
            mine = tuple(refs[j * n + k] for j in range(8))

            @pl.when(walk.mine(k, i))
            def _(mine=mine):
                w_ref, g_ref, m_ref, v_ref, go_ref, d_ref, nm_ref, nv_ref = mine
                g = g_ref[...]
                go_ref[...] = g
                m = B1 * m_ref[...] + (1.0 - B1) * g
                v = B2 * v_ref[...] + (1.0 - B2) * (g * g)
                nm_ref[...] = m
                nv_ref[...] = v
                d_ref[...] = -LR * ((m * bc1) / (jnp.sqrt(v * bc2) + AEPS) + WD * w_ref[...])

    def spec(k):
        return pl.BlockSpec((trs[k], ws[k].shape[1]), lambda i: (walk.tile(k, i), 0))

    specs = [spec(k) for k in range(n)]
    res = pl.pallas_call(
        body, name=name, grid=(walk.steps,), in_specs=specs * 4, out_specs=specs * 4,
        out_shape=[jax.ShapeDtypeStruct(w.shape, F32) for w in ws] * 4,
        compiler_params=_cparams(("arbitrary",)),
    )(*ws, *gs, *ms, *vs)
    return [tuple(res[j * n + k] for j in range(4)) for k in range(n)]


BIG = (("col", (D, IN_W)), ("row", (D, D)), ("row", (D, D)), ("row", (D, D)), ("col", (D, DFF)), ("row", (DFF, D)))
NBIG = len(BIG)
ANY = pl.BlockSpec(memory_space=pl.ANY)


def _shard_shape(kind, shape):
    R, C = shape
    return (R, C // 4) if kind == "col" else (R // 4, C)


def _half_shape(kind, shape):
    R, C = shape
    return (R // 2, C) if kind == "col" else (R, C // 2)


def _piece_shape(kind, shape):
    R, C = shape
    return (R // 2, C // 4) if kind == "col" else (R // 4, C // 2)


def _own_region(ref, kind, shape, s):
    R, C = shape
    return ref.at[:, pl.ds(s * (C // 4), C // 4)] if kind == "col" else ref.at[pl.ds(s * (R // 4), R // 4), :]


def _ag_region(ref, kind, shape, s, hc):
    R, C = shape
    if kind == "col":
        return ref.at[pl.ds(hc * (R // 2), R // 2), pl.ds(s * (C // 4), C // 4)]
    return ref.at[pl.ds(s * (R // 4) + hc * (R // 8), R // 8), :]


def _ag_shard_half(ref, kind, shape, hc):
    R, C = shape
    return ref.at[pl.ds(hc * (R // 2), R // 2), :] if kind == "col" else ref.at[pl.ds(hc * (R // 8), R // 8), :]


def _grad_half(ref, kind, shape, hc):
    R, C = shape
    return ref.at[pl.ds(hc * (R // 2), R // 2), :] if kind == "col" else ref.at[:, pl.ds(hc * (C // 2), C // 2)]


def _half_piece(ref, kind, shape, s):
    R, C = shape
    return ref.at[:, pl.ds(s * (C // 4), C // 4)] if kind == "col" else ref.at[pl.ds(s * (R // 4), R // 4), :]


def _place():
    x, y, c = lax.axis_index("x"), lax.axis_index("y"), lax.axis_index("c")
    chips = [(1 - x, y), (x, 1 - y), (1 - x, 1 - y)]
    return x, y, c, chips


def _rcopy(src, dst, ssem, rsem, dev):
    return pltpu.make_async_remote_copy(src_ref=src, dst_ref=dst, send_sem=ssem, recv_sem=rsem,
                                        device_id=dev, device_id_type=MESH)


def _dma_sems(n):
    return pltpu.SemaphoreType.DMA((n,))


def _x_gather_ici(shards, ws):
    n = len(ws)
    specs = [BIG[w] for w in ws]

    def place():
        x, y, c, chips = _place()
        return c, chips, 2 * x + y

    def sends(sh, full, sc):
        c, chips, me_s = place()
        return [_rcopy(_ag_shard_half(sh[i], kind, shape, c), _ag_region(full[i], kind, shape, me_s, c),
                       sc[0].at[3 * i + j], sc[1].at[3 * i + j], (cx, cy, c))
                for i, (kind, shape) in enumerate(specs) for j, (cx, cy) in enumerate(chips)]

    def start(sh, full, sc):
        for i in range(n):
            pltpu.make_async_copy(sh[i], sc[4 + i], sc[2].at[i]).start()
        for cp in sends(sh, full, sc):
            cp.start()

    def finish(sh, full, sc):
        c, chips, me_s = place()
        stores = []
        for i, (kind, shape) in enumerate(specs):
            pltpu.make_async_copy(sh[i], sc[4 + i], sc[2].at[i]).wait()
            st = pltpu.make_async_copy(sc[4 + i], _own_region(full[i], kind, shape, me_s), sc[3].at[i])
            st.start()
            stores.append(st)
        for i, (kind, shape) in enumerate(specs):
            for j, (cx, cy) in enumerate(chips):
                reg = _ag_region(full[i], kind, shape, 2 * cx + cy, c)
                _rcopy(reg, reg, sc[0].at[3 * i + j], sc[1].at[3 * i + j], (cx, cy, c)).wait_recv()
        for cp in sends(sh, full, sc):
            cp.wait_send()
        for st in stores:
            st.wait()

    return _Exchange(
        shards, [jax.ShapeDtypeStruct(shape, BF16) for _, shape in specs], {},
        [_dma_sems(3 * n), _dma_sems(3 * n), _dma_sems(n), _dma_sems(n)]
        + [pltpu.VMEM(_shard_shape(k, s), BF16) for k, s in specs], start, finish)


def _x_gather_d2d(wholes, ws):
    specs = [BIG[w] for w in ws]
    n = len(ws)

    def copies(full, sc, mine):
        x, y, c, chips = _place()
        hc = c if mine else 1 - c
        return [_rcopy(reg, reg, sc[0].at[3 * i + j], sc[1].at[3 * i + j], (x, y, 1 - c))
                for i, (kind, shape) in enumerate(specs) for j, (cx, cy) in enumerate(chips)
                for reg in [_ag_region(full[i], kind, shape, 2 * cx + cy, hc)]]

    def start(_, full, sc):
        for cp in copies(full, sc, True):
            cp.start()

    def finish(_, full, sc):
        for cp in copies(full, sc, False):
            cp.wait_recv()
        for cp in copies(full, sc, True):
            cp.wait_send()

    return _Exchange(wholes, [jax.ShapeDtypeStruct(shape, BF16) for _, shape in specs], {i: i for i in range(n)},
                     [_dma_sems(3 * n), _dma_sems(3 * n)], start, finish)


def _x_grads_sibling(grads, ws):
    specs = [BIG[w] for w in ws]
    n = len(ws)

    def copies(g, got, sc):
        x, y, c, _ = _place()
        return [_rcopy(_grad_half(g[i], kind, shape, 1 - c), got[i], sc[0].at[i], sc[1].at[i], (x, y, 1 - c))
                for i, (kind, shape) in enumerate(specs)]

    def start(g, got, sc):
        for cp in copies(g, got, sc):
            cp.start()

    def finish(g, got, sc):
        for cp in copies(g, got, sc):
            cp.wait_recv()
        for cp in copies(g, got, sc):
            cp.wait_send()

    return _Exchange(grads, [jax.ShapeDtypeStruct(_half_shape(k, s), F32) for k, s in specs], {},
                     [_dma_sems(n), _dma_sems(n)], start, finish)


def _x_grads_chips(sums_bf, ws):
    specs = [BIG[w] for w in ws]
    n = len(ws)

    def copies(s16, got, sc):
        x, y, c, chips = _place()
        return [_rcopy(_half_piece(s16[i], kind, shape, 2 * cx + cy), got[i].at[j],
                       sc[0].at[3 * i + j], sc[1].at[3 * i + j], (cx, cy, c))
                for i, (kind, shape) in enumerate(specs) for j, (cx, cy) in enumerate(chips)]

    def start(s16, got, sc):
        for cp in copies(s16, got, sc):
            cp.start()

    def finish(s16, got, sc):
        for cp in copies(s16, got, sc):
            cp.wait_recv()
        for cp in copies(s16, got, sc):
            cp.wait_send()

    return _Exchange(sums_bf, [jax.ShapeDtypeStruct((3,) + _piece_shape(k, s), BF16) for k, s in specs], {},
                     [_dma_sems(3 * n), _dma_sems(3 * n)], start, finish)


def _shard_half(ref, kind, shape, hc):
    sr, sc = _shard_shape(kind, shape)
    return ref.at[pl.ds(hc * (sr // 2), sr // 2), :] if kind == "col" else ref.at[:, pl.ds(hc * (sc // 2), sc // 2)]


def _x_grads_share(shard_grads, ws):
    specs = [BIG[w] for w in ws]
    n = len(ws)

    def copies(g, sc, mine):
        x, y, c, _ = _place()
        hc = c if mine else 1 - c
        return [_rcopy(part, part, sc[0].at[i], sc[1].at[i], (x, y, 1 - c))
                for i, (kind, shape) in enumerate(specs) for part in [_shard_half(g[i], kind, shape, hc)]]

    def start(_, g, sc):
        for cp in copies(g, sc, True):
            cp.start()

    def finish(_, g, sc):
        for cp in copies(g, sc, False):
            cp.wait_recv()
        for cp in copies(g, sc, True):
            cp.wait_send()

    return _Exchange(shard_grads, [jax.ShapeDtypeStruct(_shard_shape(k, s), F32) for k, s in specs],
                     {i: i for i in range(n)}, [_dma_sems(n), _dma_sems(n)], start, finish)


ADD_BLOCK_BYTES = 4 * 1024 * 1024


def _add_rows(rows, cols, n_arrays):
    limit = ADD_BLOCK_BYTES // (1 if n_arrays == 1 else 4)
    r = rows
    while r > 64 and r * cols * 4 > limit:
        r //= 2
    return r


class _Walk:
    def __init__(self, tiles):
        self.tiles = list(tiles)
        self.starts = [sum(self.tiles[:k]) for k in range(len(self.tiles))]
        self.steps = sum(self.tiles)

    def tile(self, k, i):
        return jnp.clip(i - self.starts[k], 0, self.tiles[k] - 1)

    def mine(self, k, i):
        return (i >= self.starts[k]) & (i < self.starts[k] + self.tiles[k])


def _add_halves(place, gs, gots, kinds, name):
    n = len(gs)
    halves = [_half_shape(kind, g.shape) for g, kind in zip(gs, kinds)]
    rows = [_add_rows(hr, hc, n) for hr, hc in halves]
    walk = _Walk(hr // r for (hr, _), r in zip(halves, rows))

    def body(p_ref, *refs):
        i = pl.program_id(0)
        for k in range(n):
            g_ref, b_ref, s_ref, sb_ref = (refs[j * n + k] for j in range(4))

            @pl.when(walk.mine(k, i))
            def _(g_ref=g_ref, b_ref=b_ref, s_ref=s_ref, sb_ref=sb_ref):
                s = g_ref[...] + b_ref[...]
                s_ref[...] = s
                sb_ref[...] = s.astype(BF16)

    def g_spec(k):
        if kinds[k] == "col":
            return pl.BlockSpec((rows[k], gs[k].shape[1]), lambda i, p: (p[0] * walk.tiles[k] + walk.tile(k, i), 0))
        return pl.BlockSpec((rows[k], halves[k][1]), lambda i, p: (walk.tile(k, i), p[0]))

    def spec(k):
        return pl.BlockSpec((rows[k], halves[k][1]), lambda i, p: (walk.tile(k, i), 0))

    specs = [spec(k) for k in range(n)]
    res = pl.pallas_call(
        body, name=name,
        grid_spec=pltpu.PrefetchScalarGridSpec(num_scalar_prefetch=1, grid=(walk.steps,),
                                               in_specs=[g_spec(k) for k in range(n)] + specs, out_specs=specs + specs),
        out_shape=[jax.ShapeDtypeStruct(h, F32) for h in halves] + [jax.ShapeDtypeStruct(h, BF16) for h in halves],
        compiler_params=_cparams(("arbitrary",)),
    )(place, *gs, *gots)
    return [(res[k], res[n + k]) for k in range(n)]


def _add_pieces(place, halves, gots, specs_big, name):
    n = len(halves)
    pieces = [_piece_shape(kind, shape) for kind, shape in specs_big]
    rows = [_add_rows(pr, pc, n) for pr, pc in pieces]
    walk = _Walk(pr // r for (pr, _), r in zip(pieces, rows))

    def body(p_ref, *refs):
        i = pl.program_id(0)
        for k in range(n):
            m_ref, g_ref, o_ref = (refs[j * n + k] for j in range(3))

            @pl.when(walk.mine(k, i))
            def _(m_ref=m_ref, g_ref=g_ref, o_ref=o_ref):
                acc = m_ref[...]
                for j in range(3):
                    acc = acc + g_ref[j].astype(F32)
                o_ref[...] = acc

    def m_spec(k):
        if specs_big[k][0] == "col":
            return pl.BlockSpec((rows[k], pieces[k][1]), lambda i, p: (walk.tile(k, i), p[1]))
        return pl.BlockSpec((rows[k], pieces[k][1]), lambda i, p: (p[1] * walk.tiles[k] + walk.tile(k, i), 0))

    def got_spec(k):
        return pl.BlockSpec((3, rows[k], pieces[k][1]), lambda i, p: (0, walk.tile(k, i), 0))

    def o_spec(k):
        if specs_big[k][0] == "col":
            return pl.BlockSpec((rows[k], pieces[k][1]), lambda i, p: (p[0] * walk.tiles[k] + walk.tile(k, i), 0))
        return pl.BlockSpec((rows[k], pieces[k][1]), lambda i, p: (walk.tile(k, i), p[0]))

    return pl.pallas_call(
        body, name=name,
        grid_spec=pltpu.PrefetchScalarGridSpec(
            num_scalar_prefetch=1, grid=(walk.steps,),
            in_specs=[m_spec(k) for k in range(n)] + [got_spec(k) for k in range(n)],
            out_specs=[o_spec(k) for k in range(n)]),
        out_shape=[jax.ShapeDtypeStruct(_shard_shape(kind, shape), F32) for kind, shape in specs_big],
        compiler_params=_cparams(("arbitrary",)),
    )(place, *halves, *gots)


SMALL_ROWS = 1024 + 8 * 8 + 8


def _x_small_all_reduce(p):
    def parts(p_ref, sc):
        slots, ssem, rsem = sc[0], sc[2], sc[3]
        x, y, c = lax.axis_index("x"), lax.axis_index("y"), lax.axis_index("c")
        me = 4 * x + 2 * y + c
        out = []
        for r in range(1, 8):
            bx, by, bc = (r >> 2) & 1, (r >> 1) & 1, r & 1
            tgt = (1 - x if bx else x, 1 - y if by else y, 1 - c if bc else c)
            send = _rcopy(p_ref, slots.at[me], ssem.at[r - 1], rsem.at[r - 1], tgt)
            src = 4 * tgt[0] + 2 * tgt[1] + tgt[2]
            recv = _rcopy(p_ref, slots.at[src], ssem.at[r - 1], rsem.at[r - 1], tgt)
            out.append((send, recv))
        return me, out

    def start(ins, outs, sc):
        me, cps = parts(ins[0], sc)
        pltpu.make_async_copy(ins[0], sc[0].at[me], sc[4].at[0]).start()
        for send, _ in cps:
            send.start()

    def finish(ins, outs, sc):
        me, cps = parts(ins[0], sc)
        pltpu.make_async_copy(ins[0], sc[0].at[me], sc[4].at[0]).wait()
        for _, recv in cps:
            recv.wait_recv()
        acc = sc[0][0]
        for d in range(1, 8):
            acc = acc + sc[0][d]
        sc[1][...] = acc
        back = pltpu.make_async_copy(sc[1], outs[0], sc[4].at[1])
        back.start()
        for send, _ in cps:
            send.wait_send()
        back.wait()

    return _Exchange([p], [jax.ShapeDtypeStruct((SMALL_ROWS, CH), F32)], {},
                     [pltpu.VMEM((8, SMALL_ROWS, CH), F32), pltpu.VMEM((SMALL_ROWS, CH), F32), _dma_sems(7), _dma_sems(7),
                      _dma_sems(2)], start, finish)


def _rope_tables(positions, comm=None):
    T = positions.shape[0]
    inv_freq = 500000.0 ** (-jnp.arange(0, 2 * ROPE_HALF, 2, dtype=F32) / (2 * ROPE_HALF))
    head = jnp.concatenate([inv_freq, inv_freq, jnp.zeros((HD - 2 * ROPE_HALF,), F32)])
    lane_freq = jnp.concatenate([head, head])[None, :]
    pos = jnp.broadcast_to(positions.astype(F32)[:, None], (T, CH))
    tm = min(1024, T)

    def body(p_ref, f_ref, c_ref, s1_ref, s2_ref):
        ang = p_ref[...] * f_ref[...]
        sin = jnp.sin(ang)
        first = (lax.broadcasted_iota(jnp.int32, ang.shape, 1) % HD) < ROPE_HALF
        c_ref[...] = jnp.cos(ang)
        s1_ref[...] = jnp.where(first, -sin, 0.0)
        s2_ref[...] = jnp.where(first, 0.0, sin)

    return _call(body, (pos, lane_freq), name="rope_tables", grid=(T // tm,),
                 in_specs=[_rows(tm, CH), _const((1, CH))], out_specs=[_rows(tm, CH)] * 3,
                 out_shape=[jax.ShapeDtypeStruct((T, CH), F32)] * 3, sem=("parallel",), comm=comm)


BIG_NAMES = ("w_in", "w_a", "w_b", "w_o", "w_ff_in", "w_ff_out")
SMALL_NAMES = ("w_spatial", "ln_v_gain", "ln_v_bias", "b_spatial", "sinks", "norm_mix_pre", "norm_mix_post",
               "norm_ff_pre", "norm_ff_post")
WEIGHT_ORDER = ("w_in", "ln_v_gain", "ln_v_bias", "w_spatial", "b_spatial", "sinks", "w_a", "w_b", "w_o",
                "norm_mix_pre", "norm_mix_post", "w_ff_in", "w_ff_out", "norm_ff_pre", "norm_ff_post")


def _pack_small(d, loss_sums=None):
    parts = []
    for n in SMALL_NAMES:
        flat = d[n].reshape(-1)
        pad = (-flat.shape[0]) % (8 * CH)
        parts.append(jnp.pad(flat, (0, pad)).reshape(-1, CH))
    parts.append(jnp.zeros((8, CH), F32) if loss_sums is None else loss_sums.reshape(8, CH))
    return jnp.concatenate(parts, axis=0)


def _unpack_small(p, like):
    out, row = {}, 0
    for n in SMALL_NAMES:
        size = like[n].size
        rows = -(-size // (8 * CH)) * 8
        out[n] = p[row:row + rows].reshape(-1)[:size].reshape(like[n].shape)
        row += rows
    return out


def kernel(x, positions, w_in, ln_v_gain, ln_v_bias, w_spatial, b_spatial, sinks, w_a, w_b, w_o, norm_mix_pre, norm_mix_post, w_ff_in, w_ff_out, norm_ff_pre, norm_ff_post, loss_target, m_w_in, m_ln_v_gain, m_ln_v_bias, m_w_spatial, m_b_spatial, m_sinks, m_w_a, m_w_b, m_w_o, m_norm_mix_pre, m_norm_mix_post, m_w_ff_in, m_w_ff_out, m_norm_ff_pre, m_norm_ff_post, v_w_in, v_ln_v_gain, v_ln_v_bias, v_w_spatial, v_b_spatial, v_sinks, v_w_a, v_w_b, v_w_o, v_norm_mix_pre, v_norm_mix_post, v_w_ff_in, v_w_ff_out, v_norm_ff_pre, v_norm_ff_post):
    w = dict(w_in=w_in, ln_v_gain=ln_v_gain, ln_v_bias=ln_v_bias, w_spatial=w_spatial, b_spatial=b_spatial, sinks=sinks,
             w_a=w_a, w_b=w_b, w_o=w_o, norm_mix_pre=norm_mix_pre, norm_mix_post=norm_mix_post, w_ff_in=w_ff_in,
             w_ff_out=w_ff_out, norm_ff_pre=norm_ff_pre, norm_ff_post=norm_ff_post)
    m = dict(w_in=m_w_in, ln_v_gain=m_ln_v_gain, ln_v_bias=m_ln_v_bias, w_spatial=m_w_spatial, b_spatial=m_b_spatial,
             sinks=m_sinks, w_a=m_w_a, w_b=m_w_b, w_o=m_w_o, norm_mix_pre=m_norm_mix_pre, norm_mix_post=m_norm_mix_post,
             w_ff_in=m_w_ff_in, w_ff_out=m_w_ff_out, norm_ff_pre=m_norm_ff_pre, norm_ff_post=m_norm_ff_post)
    v = dict(w_in=v_w_in, ln_v_gain=v_ln_v_gain, ln_v_bias=v_ln_v_bias, w_spatial=v_w_spatial, b_spatial=v_b_spatial,
             sinks=v_sinks, w_a=v_w_a, w_b=v_w_b, w_o=v_w_o, norm_mix_pre=v_norm_mix_pre, norm_mix_post=v_norm_mix_post,
             w_ff_in=v_w_ff_in, w_ff_out=v_w_ff_out, norm_ff_pre=v_norm_ff_pre, norm_ff_post=v_norm_ff_post)

    FIRST, REST = (0,), tuple(range(1, NBIG))
    shards = [w[n][0].astype(BF16) for n in BIG_NAMES]
    place = jnp.stack([lax.axis_index("c"), 2 * lax.axis_index("x") + lax.axis_index("y")]).astype(jnp.int32)
    xs, target = x[0], loss_target[0]
    T = xs.shape[0]
    tile = min(TOKEN_TILE, T)
    wtiles = {n: dict(tm=min(tm, T), tn=tn) for n, (tm, tn) in WGRAD_TILES.items()}
    g1, g2, g3, g4 = norm_mix_pre, norm_mix_post, norm_ff_pre, norm_ff_post
    w_sp, snk = w_spatial[0], sinks[0]
    MIX, FF = (1, 2, 3), (4, 5)
    bfull = jnp.repeat(b_spatial[0].T, CH, axis=1)

    def reduce_tail(ws, grads, got):
        tag = "_".join(BIG_NAMES[k] for k in ws)
        sums = _add_halves(place, grads, got, [BIG[k][0] for k in ws], name="grad_add_sibling_" + tag)
        return sums, _x_grads_chips([s[1] for s in sums], ws)

    def reduce_end(ws, sums, pieces):
        tag = "_".join(BIG_NAMES[k] for k in ws)
        return _add_pieces(place, [s[0] for s in sums], pieces, [BIG[k] for k in ws], name="grad_add_chips_" + tag)

    (rc, rs1, rs2), w_in_part = _rope_tables(positions[0], comm=_x_gather_ici(shards[:1], FIRST))
    w_in_b = _run(_x_gather_d2d(w_in_part, FIRST), "gather_w_in_d2d")[0]
    EARLY, FF_OUT = (1, 2, 3, 4), (5,)
    (h, u, vs, q, k, va, ga, gb), early_part = _inproj(xs, g1, w_in_b, rc, rs1, rs2, tm=tile,
                                                      comm=_x_gather_ici(shards[1:5], EARLY))
    att, (w_a_b, w_b_b, w_o_b, w_ff_in_b, ffo_part) = _attn_fwd(
        q, k, va, snk, comm=_both(_x_gather_d2d(early_part, EARLY), _x_gather_ici(shards[5:], FF_OUT)))
    (a, pa, pb, merged, mix, x1), (w_ff_out_b,) = _sgu_merge_fwd(
        u, vs, ln_v_gain, ln_v_bias, w_sp, bfull, att, ga, gb, xs, w_a_b, w_b_b, w_o_b, g2, tm=tile,
        comm=_x_gather_d2d([ffo_part], FF_OUT))
    hf, f2, dff, df1, dx1, lsum, dg3, dg4 = _ffn(x1, target, w_ff_in_b, w_ff_out_b, g3, g4, tm=tile)

    dw_ff_out, _ = _wgrad(f2, dff, name="wgrad_ff_out", **wtiles["w_ff_out"])
    dw_ff_in, _ = _wgrad(hf, df1, name="wgrad_ff_in", **wtiles["w_ff_in"])
    grads_ff = [dw_ff_in, dw_ff_out]
    (dga, dgb, da, datt, dg2, dw_a, dw_b, dw_o), got_ff = _merge_bwd(
        dx1, mix, ga, gb, pa, pb, a, att, merged, w_a_b, w_b_b, w_o_b, g2, tm=tile, comm=_x_grads_sibling(grads_ff, FF))
    grads_mix = [dw_a, dw_b, dw_o]
    sums_ff, ff_to_chips = reduce_tail(FF, grads_ff, got_ff)
    (dq, dk, dva, dsk), (*pieces_ff, got_a, got_b, got_o) = _attn_bwd(
        q, k, va, datt, snk, rc, rs1, rs2, comm=_both(ff_to_chips, _x_grads_sibling(grads_mix, MIX)))
    partial_ff = reduce_end(FF, sums_ff, pieces_ff)
    sums_mix, mix_to_chips = reduce_tail(MIX, grads_mix, [got_a, got_b, got_o])
    dx, dproj, dg1, dws, dbs, dlg, dlb = _sgu_inproj_bwd(
        u, vs, da, ln_v_gain, ln_v_bias, w_sp, bfull, [dq, dk, dva, dga, dgb], xs, dx1, g1, w_in_b, tm=tile)
    small = dict(ln_v_gain=dlg, ln_v_bias=dlb, w_spatial=dws, b_spatial=dbs, sinks=dsk[:, :NQ],
                 norm_mix_pre=dg1, norm_mix_post=dg2, norm_ff_pre=dg3, norm_ff_post=dg4)
    dw_in, (gs, shard_ff_in, shard_ff_out, *pieces_mix) = _wgrad(
        h, dproj, name="wgrad_in", vmem=VMEM_PHYSICAL, **wtiles["w_in"],
        comm=_both(_both(_x_small_all_reduce(_pack_small(small, lsum)), _x_grads_share(partial_ff, FF)), mix_to_chips))
    partial_mix = reduce_end(MIX, sums_mix, pieces_mix)
    got_in = _run(_x_grads_sibling([dw_in], FIRST), "grads_in_to_sibling")
    sums_in, to_chips = reduce_tail(FIRST, [dw_in], got_in)
    partial_in = reduce_end(FIRST, sums_in, _run(to_chips, "grads_in_to_chips"))
    g_in, *shard_mix = _run(_x_grads_share(list(partial_in) + list(partial_mix), FIRST + MIX), "grads_in_mix_share")
    shard_rest = list(shard_mix) + [shard_ff_in, shard_ff_out]

    loss = 0.5 * jnp.sum(gs[SMALL_ROWS - 8:]) / D
    grad, delta, new_m, new_v = {}, {}, {}, {}
    for n, g in zip(BIG_NAMES, [g_in] + list(shard_rest)):
        (g_, d_, m_, v_), = _adamw([w[n][0]], [g], [m[n][0]], [v[n][0]], [256], name="adamw_" + n)
        grad[n], delta[n], new_m[n], new_v[n] = g_[None], d_[None], m_[None], v_[None]
    (gs, ds, ms, vs), = _adamw([_pack_small(w)], [gs], [_pack_small(m)], [_pack_small(v)], [SMALL_ROWS], name="adamw_small")
    for packed, dst in ((gs, grad), (ds, delta), (ms, new_m), (vs, new_v)):
        dst.update(_unpack_small(packed, w))

    outs = [loss, dx[None]]
    for group in (grad, delta, new_m, new_v):
        outs.extend(group[n] for n in WEIGHT_ORDER)
    return tuple(outs)
```

```python
import functools

import jax
import jax.numpy as jnp
from jax import lax
from jax.experimental import pallas as pl
from jax.experimental.pallas import tpu as pltpu

F32 = jnp.float32
BF16 = jnp.bfloat16

D = 1024
CH = 128
NG = 8
HD = 64
NQ = 16
NKV = 4
KVW = NKV * HD
DFF = 4 * D
EPS = 1e-6
IN_W = 5632
SEG = (0, 1024, 2048, 3072, 3328, 3584, 4608, 5632)
ROPE_HALF = 8
Q_SCALE = HD ** -0.5

LR, B1, B2, AEPS, WD, STEP = 0.001, 0.9, 0.999, 1e-08, 0.01, 10

VMEM_PHYSICAL = 64 * 1024 * 1024
VMEM_LIMIT = 60 * 1024 * 1024
MESH = pl.DeviceIdType.MESH

TOKEN_TILE = 512
WGRAD_TILES = {"w_ff_out": (512, 1024), "w_ff_in": (2048, 2048), "w_in": (2048, IN_W // 2)}

_GELU_C0 = 0.7978845608028654
_GELU_C1 = 0.044715


def _cparams(sem=None, vmem=None):
    kw = dict(vmem_limit_bytes=VMEM_LIMIT if vmem is None else vmem)
    if sem is not None:
        kw["dimension_semantics"] = sem
    return pltpu.CompilerParams(**kw)


def _resident(shape):
    nd = len(shape)
    return pl.BlockSpec(shape, lambda *_: (0,) * nd, pipeline_mode=pl.Buffered(1))


def _const(shape):
    nd = len(shape)
    return pl.BlockSpec(shape, lambda *_: (0,) * nd)


def _rows(tm, w):
    return pl.BlockSpec((tm, w), lambda i: (i, 0))


class _Exchange:
    def __init__(self, ins, outs, aliases, scratch, start, finish):
        self.ins, self.outs, self.aliases, self.scratch = list(ins), list(outs), dict(aliases), list(scratch)
        self.start, self.finish = start, finish


def _both(a, b):
    na, ma, sa = len(a.ins), len(a.outs), len(a.scratch)

    def start(ci, co, cs):
        a.start(ci[:na], co[:ma], cs[:sa])
        b.start(ci[na:], co[ma:], cs[sa:])

    def finish(ci, co, cs):
        a.finish(ci[:na], co[:ma], cs[:sa])
        b.finish(ci[na:], co[ma:], cs[sa:])

    aliases = {**a.aliases, **{na + i: ma + j for i, j in b.aliases.items()}}
    return _Exchange(a.ins + b.ins, a.outs + b.outs, aliases, a.scratch + b.scratch, start, finish)


def _call(body, args, *, name, grid, in_specs, out_specs, out_shape, scratch_shapes=(), sem=None, comm=None, vmem=None):
    single = not isinstance(out_shape, (list, tuple))
    out_shape = [out_shape] if single else list(out_shape)
    out_specs = [out_specs] if single else list(out_specs)
    if comm is None:
        res = pl.pallas_call(body, name=name, grid=grid, in_specs=list(in_specs), out_specs=out_specs,
                             out_shape=out_shape, scratch_shapes=list(scratch_shapes),
                             compiler_params=_cparams(sem, vmem))(*args)
        return (res[0] if single else res), []
    n_in, n_out, n_scr = len(args), len(out_shape), len(scratch_shapes)
    nci, nco = len(comm.ins), len(comm.outs)
    steps = 1
    for g in grid:
        steps *= g

    def hosted(*refs):
        a, ci = refs[:n_in], refs[n_in:n_in + nci]
        o, co = refs[n_in + nci:n_in + nci + n_out], refs[n_in + nci + n_out:n_in + nci + n_out + nco]
        rest = refs[n_in + nci + n_out + nco:]
        scr, cs = rest[:n_scr], rest[n_scr:]
        step = pl.program_id(0)
        for d in range(1, len(grid)):
            step = step * grid[d] + pl.program_id(d)

        @pl.when(step == 0)
        def _():
            comm.start(ci, co, cs)

        body(*a, *o, *scr)

        @pl.when(step == steps - 1)
        def _():
            comm.finish(ci, co, cs)

    res = pl.pallas_call(
        hosted, name=name, grid=grid, in_specs=list(in_specs) + [ANY] * nci, out_specs=out_specs + [ANY] * nco,
        out_shape=out_shape + comm.outs, scratch_shapes=list(scratch_shapes) + comm.scratch,
        input_output_aliases={n_in + i: n_out + j for i, j in comm.aliases.items()},
        compiler_params=_cparams(("arbitrary",) * len(grid), vmem),
    )(*args, *comm.ins)
    own = res[:n_out]
    return (own[0] if single else own), list(res[n_out:])


def _run(comm, name):
    nci = len(comm.ins)

    def body(*refs):
        ci, co, cs = refs[:nci], refs[nci:nci + len(comm.outs)], refs[nci + len(comm.outs):]
        comm.start(ci, co, cs)
        comm.finish(ci, co, cs)

    return pl.pallas_call(
        body, name=name, in_specs=[ANY] * nci, out_specs=[ANY] * len(comm.outs), out_shape=comm.outs,
        scratch_shapes=comm.scratch, input_output_aliases=comm.aliases,
        compiler_params=pltpu.CompilerParams(vmem_limit_bytes=VMEM_LIMIT),
    )(*comm.ins)


def _gelu(x):
    x2 = x * x
    t = jnp.tanh(x * (_GELU_C0 + (_GELU_C0 * _GELU_C1) * x2))
    hx = 0.5 * x
    return hx + hx * t, (t, x2, hx)


def _gelu_grad(parts):
    t, x2, hx = parts
    return (0.5 + 0.5 * t) + hx * (1.0 - t * t) * (_GELU_C0 + (3.0 * _GELU_C0 * _GELU_C1) * x2)


def _sigmoid(x):
    return 1.0 / (1.0 + jnp.exp(-x))


def _rms_hat(x):
    r = lax.rsqrt(jnp.mean(x * x, axis=-1, keepdims=True) + EPS)
    return x * r, r


def _rms_bwd(xhat, r, g, dout):
    dg = jnp.sum(dout * xhat, axis=0, keepdims=True)
    dy = dout * g
    dx = r * (dy - xhat * jnp.mean(dy * xhat, axis=-1, keepdims=True))
    return dx, dg


def _dot(a, b):
    return jnp.dot(a, b, preferred_element_type=F32)


def _dot_nt(a, b):
    return lax.dot_general(a, b, (((1,), (1,)), ((), ())), preferred_element_type=F32)


def _dot_tn(a, b):
    return lax.dot_general(a, b, (((0,), (0,)), ((), ())), preferred_element_type=F32)


def _rope(blk, c, s1, s2):
    return blk * c + pltpu.roll(blk, CH - ROPE_HALF, 1) * s1 + pltpu.roll(blk, ROPE_HALF, 1) * s2


def _rope_t(blk, c, s1, s2):
    return blk * c + pltpu.roll(blk * s1, ROPE_HALF, 1) + pltpu.roll(blk * s2, CH - ROPE_HALF, 1)


def _inproj(x, g1, w_in, rc, rs1, rs2, tm, comm=None):
    T = x.shape[0]

    def body(x_ref, g_ref, w_ref, c_ref, s1_ref, s2_ref,
             h_ref, u_ref, v_ref, q_ref, k_ref, va_ref, ga_ref, gb_ref):
        xhat, _ = _rms_hat(x_ref[...])
        h = (xhat * g_ref[...]).astype(BF16)
        h_ref[...] = h
        uv = _dot(h, w_ref[:, SEG[0]:SEG[2]])
        uv = uv.astype(BF16)
        u_ref[...] = uv[:, :D]
        v_ref[...] = uv[:, D:]
        c, s1, s2 = c_ref[...], s1_ref[...], s2_ref[...]
        qkv = _dot(h, w_ref[:, SEG[2]:SEG[5]])
        for p in range(D // CH):
            blk = _rope(qkv[:, CH * p:CH * (p + 1)], c, s1, s2) * Q_SCALE
            q_ref[:, CH * p:CH * (p + 1)] = blk.astype(BF16)
        for p in range(KVW // CH):
            k_ref[:, CH * p:CH * (p + 1)] = _rope(qkv[:, D + CH * p:D + CH * (p + 1)], c, s1, s2).astype(BF16)
        va_ref[...] = qkv[:, D + KVW:].astype(BF16)
        gates = _dot(h, w_ref[:, SEG[5]:SEG[7]]).astype(BF16)
        ga_ref[...] = gates[:, :D]
        gb_ref[...] = gates[:, D:]

    sd = jax.ShapeDtypeStruct
    return _call(
        body, (x, g1, w_in, rc, rs1, rs2), name="inproj_fwd", grid=(T // tm,),
        in_specs=[_rows(tm, D), _const((1, D)), _resident((D, IN_W)), _rows(tm, CH), _rows(tm, CH), _rows(tm, CH)],
        out_specs=[_rows(tm, D), _rows(tm, D), _rows(tm, D), _rows(tm, D), _rows(tm, KVW), _rows(tm, KVW),
                   _rows(tm, D), _rows(tm, D)],
        out_shape=[sd((T, D), BF16), sd((T, D), BF16), sd((T, D), BF16), sd((T, D), BF16), sd((T, KVW), BF16),
                   sd((T, KVW), BF16), sd((T, D), BF16), sd((T, D), BF16)],
        sem=("parallel",), comm=comm)


def _sgu_common(u, vs, lng, lnb, ws_ref, bfull):
    nc = u.shape[0] // CH
    ug, tu = _gelu(u)
    vg, tv = _gelu(vs)
    mu = jnp.mean(vg, axis=-1, keepdims=True)
    xc = vg - mu
    rstd = lax.rsqrt(jnp.mean(xc * xc, axis=-1, keepdims=True) + EPS)
    vhat = xc * rstd
    vnb = (vhat * lng + lnb).astype(BF16)
    tri = lax.broadcasted_iota(jnp.int32, (CH, CH), 0) >= lax.broadcasted_iota(jnp.int32, (CH, CH), 1)
    wts, rhss, mixed = [], [], []
    for g in range(NG):
        wt = jnp.where(tri, ws_ref[g], 0.0).astype(BF16)
        rhs = jnp.concatenate([vnb[CH * c:CH * (c + 1), CH * g:CH * (g + 1)] for c in range(nc)], axis=1)
        mix = _dot(wt, rhs)
        wts.append(wt)
        rhss.append(rhs)
        mixed.append([mix[:, CH * c:CH * (c + 1)] + bfull[:, CH * g:CH * (g + 1)] for c in range(nc)])
    return nc, ug, tu, tv, rstd, vhat, tri, wts, rhss, mixed


def _pair_layout(prev, cur, grp):
    j, half = grp // 2, grp % 2
    blk = jnp.concatenate([prev[:, CH * j:CH * (j + 1)], cur[:, CH * j:CH * (j + 1)]], axis=0).astype(F32)
    lo = lax.broadcasted_iota(jnp.int32, blk.shape, 1) < HD
    rolled = pltpu.roll(blk, HD, 1)
    even = jnp.where(lo, blk if half == 0 else rolled, 0.0)
    odd = jnp.where(lo, 0.0, rolled if half == 0 else blk)
    return jnp.concatenate([even, odd], axis=0).astype(BF16)


def _attn_mask(n):
    qi = lax.broadcasted_iota(jnp.int32, (CH, 2 * CH), 0)
    kc = lax.broadcasted_iota(jnp.int32, (CH, 2 * CH), 1)
    ok = (kc > qi) & (kc <= qi + CH) & ((kc >= CH) | (n > 0))
    return jnp.concatenate([ok, ok], axis=1)


def _softmax_sink(s, sink):
    m = jnp.maximum(jnp.max(s, axis=-1, keepdims=True), sink)
    p = jnp.exp(s - m)
    ps = jnp.exp(sink - m)
    inv = 1.0 / (jnp.sum(p, axis=-1, keepdims=True) + ps)
    return p * inv, ps * inv


QUERY_BLOCKS_PER_STEP = 2
RING_SLOTS = 3


def _attn_fwd(q, k, va, sinks, comm=None):
    T = q.shape[0]
    nblk = QUERY_BLOCKS_PER_STEP
    nsteps = T // (nblk * CH)
    npairs = D // CH

    def body(sk_ref, q_ref, kp_ref, kc_ref, vp_ref, vc_ref, o_ref):
        n = pl.program_id(0)
        even_lanes = lax.broadcasted_iota(jnp.int32, (CH, CH), 1) < HD
        ks = [kp_ref[...]] + [kc_ref[CH * b:CH * (b + 1)] for b in range(nblk)]
        vs = [vp_ref[...]] + [vc_ref[CH * b:CH * (b + 1)] for b in range(nblk)]
        masks = [_attn_mask(nblk * n)] + [_attn_mask(1)] * (nblk - 1)
        kks = [[_pair_layout(ks[b], ks[b + 1], grp) for grp in range(NKV)] for b in range(nblk)]
        vvs = [[_pair_layout(vs[b], vs[b + 1], grp) for grp in range(NKV)] for b in range(nblk)]
        work = [(b, p) for b in range(nblk) for p in range(npairs)]

        def scores(i):
            b, p = work[i]
            return _dot_nt(q_ref[CH * b:CH * (b + 1), CH * p:CH * (p + 1)], kks[b][p // 2])

        def unnormalised(s, sink):
            m = jnp.maximum(jnp.max(s, axis=-1, keepdims=True), sink)
            p = jnp.exp(s - m)
            return p, 1.0 / (jnp.sum(p, axis=-1, keepdims=True) + jnp.exp(sink - m))

        def value_product(i):
            b, p = work[i]
            pr, ie, io = probs[i]
            return _dot(pr, vvs[b][p // 2]) * jnp.where(even_lanes, ie, io)

        ahead = 3
        outs, probs = [], []
        pending = [scores(i) for i in range(ahead)]
        for i, (b, p) in enumerate(work):
            s = jnp.where(masks[b], pending.pop(0), -1e30)
            if i + ahead < len(work):
                pending.append(scores(i + ahead))
            pe, ie = unnormalised(s[:, :2 * CH], sk_ref[2 * p])
            po, io = unnormalised(s[:, 2 * CH:], sk_ref[2 * p + 1])
            probs.append((jnp.concatenate([pe, po], axis=1).astype(BF16), ie, io))
            if i >= 1:
                outs.append(value_product(i - 1))
        outs.append(value_product(len(work) - 1))
        for b in range(nblk):
            o_ref[CH * b:CH * (b + 1), :] = jnp.concatenate(outs[npairs * b:npairs * (b + 1)], axis=1).astype(BF16)

    prev = lambda n: (jnp.maximum(nblk * n - 1, 0), 0)
    cur = lambda n: (n, 0)
    return _call(
        body, (sinks, q, k, k, va, va), name="attn_fwd", grid=(nsteps,),
        in_specs=[pl.BlockSpec(memory_space=pltpu.SMEM), pl.BlockSpec((nblk * CH, D), cur),
                  pl.BlockSpec((CH, KVW), prev), pl.BlockSpec((nblk * CH, KVW), cur),
                  pl.BlockSpec((CH, KVW), prev), pl.BlockSpec((nblk * CH, KVW), cur)],
        out_specs=pl.BlockSpec((nblk * CH, D), cur), out_shape=jax.ShapeDtypeStruct((T, D), BF16),
        sem=("parallel",), comm=comm)


def _attn_bwd(q, k, va, datt, sinks, rc, rs1, rs2, comm=None):
    T = q.shape[0]
    nb = T // CH

    def body(sk_ref, q_ref, kp_ref, kc_ref, vp_ref, vc_ref, do_ref, cq_ref, s1q_ref, s2q_ref, ck_ref, s1k_ref, s2k_ref,
             dq_ref, dk_ref, dv_ref, dsk_ref, kcar, vcar):
        n = pl.program_id(0)

        @pl.when(n == 0)
        def _():
            kcar[...] = jnp.zeros_like(kcar)
            vcar[...] = jnp.zeros_like(vcar)
            dsk_ref[...] = jnp.zeros_like(dsk_ref)

        def flush(kprev, vprev):
            ck, s1k, s2k = ck_ref[...], s1k_ref[...], s2k_ref[...]
            for j in range(KVW // CH):
                sl = slice(CH * j, CH * (j + 1))
                dk_ref[:, sl] = _rope_t(kcar[:, sl] + kprev[:, sl], ck, s1k, s2k).astype(BF16)
                dv_ref[:, sl] = (vcar[:, sl] + vprev[:, sl]).astype(BF16)

        @pl.when(n < nb)
        def _():
            mask = _attn_mask(n)
            kp, kc, vp, vc = kp_ref[...], kc_ref[...], vp_ref[...], vc_ref[...]
            cq, s1q, s2q = cq_ref[...], s1q_ref[...], s2q_ref[...]
            lane = lax.broadcasted_iota(jnp.int32, (1, CH), 1)
            dsk = jnp.zeros((1, CH), F32)
            npairs = D // CH
            kks = [_pair_layout(kp, kc, grp) for grp in range(NKV)]
            vvs = [_pair_layout(vp, vc, grp) for grp in range(NKV)]
            qs = [q_ref[:, CH * p:CH * (p + 1)] for p in range(npairs)]
            dos = [do_ref[:, CH * p:CH * (p + 1)].astype(BF16) for p in range(npairs)]

            def first(p):
                return _dot_nt(qs[p], kks[p // 2]), _dot_nt(dos[p], vvs[p // 2])

            def last(p, ds, pb):
                return (_rope_t(_dot(ds, kks[p // 2]), cq, s1q, s2q) * Q_SCALE, _dot_tn(qs[p], ds), _dot_tn(dos[p], pb))

            ahead = 2
            pending = [first(p) for p in range(ahead)]
            mids, ends = [], []
            for p in range(npairs):
                s, dp = pending.pop(0)
                s = jnp.where(mask, s, -1e30)
                if p + ahead < npairs:
                    pending.append(first(p + ahead))
                ds_parts, p_parts = [], []
                for par in range(2):
                    sl = slice(2 * CH * par, 2 * CH * (par + 1))
                    pr, psink = _softmax_sink(s[:, sl], sk_ref[2 * p + par])
                    delta = jnp.sum(pr * dp[:, sl], axis=-1, keepdims=True)
                    ds_parts.append(pr * (dp[:, sl] - delta))
                    p_parts.append(pr)
                    tot = -jnp.sum(psink * delta, axis=0, keepdims=True)
                    dsk = dsk + jnp.where(lane == 2 * p + par, tot, 0.0)
                mids.append((jnp.concatenate(ds_parts, axis=1).astype(BF16), jnp.concatenate(p_parts, axis=1).astype(BF16)))
                if p >= 1:
                    ends.append(last(p - 1, *mids[p - 1]))
            ends.append(last(npairs - 1, *mids[-1]))
            dq_cols = [e[0] for e in ends]
            def fold(i):
                rows = []
                for grp in range(NKV):
                    acc = ends[2 * grp][i] + ends[2 * grp + 1][i]
                    rows.append(acc[:HD, :2 * CH] + acc[HD:, 2 * CH:])
                return jnp.concatenate(rows, axis=0).T

            dkf, dvf = fold(1), fold(2)
            dq_ref[...] = jnp.concatenate(dq_cols, axis=1).astype(BF16)
            dsk_ref[...] += dsk
            flush(dkf[:CH], dvf[:CH])
            kcar[...] = dkf[CH:]
            vcar[...] = dvf[CH:]

        @pl.when(n == nb)
        def _():
            z = jnp.zeros((CH, KVW), F32)
            flush(z, z)

    last = nb - 1
    cur = lambda n: (jnp.minimum(n, last), 0)
    prev = lambda n: (jnp.clip(n - 1, 0, last), 0)
    sd = jax.ShapeDtypeStruct
    return _call(
        body, (sinks, q, k, k, va, va, datt, rc, rs1, rs2, rc, rs1, rs2), name="attn_bwd", grid=(nb + 1,),
        in_specs=[pl.BlockSpec(memory_space=pltpu.SMEM), pl.BlockSpec((CH, D), cur),
                  pl.BlockSpec((CH, KVW), prev), pl.BlockSpec((CH, KVW), cur),
                  pl.BlockSpec((CH, KVW), prev), pl.BlockSpec((CH, KVW), cur),
                  pl.BlockSpec((CH, D), cur),
                  pl.BlockSpec((CH, CH), cur), pl.BlockSpec((CH, CH), cur), pl.BlockSpec((CH, CH), cur),
                  pl.BlockSpec((CH, CH), prev), pl.BlockSpec((CH, CH), prev), pl.BlockSpec((CH, CH), prev)],
        out_specs=[pl.BlockSpec((CH, D), cur), pl.BlockSpec((CH, KVW), prev), pl.BlockSpec((CH, KVW), prev),
                   _const((1, CH))],
        out_shape=[sd((T, D), BF16), sd((T, KVW), BF16), sd((T, KVW), BF16), sd((1, CH), F32)],
        scratch_shapes=[pltpu.VMEM((CH, KVW), F32), pltpu.VMEM((CH, KVW), F32)], sem=("arbitrary",), comm=comm)


def _sgu_merge_fwd(u, vs, lng, lnb, ws, bfull, att, ga, gb, x, w_a, w_b, w_o, g2, tm, comm=None):
    T = x.shape[0]
    nsteps = T // tm
    streamed = (u, vs, att, ga, gb, x)

    def body(u_hbm, v_hbm, lng_ref, lnb_ref, ws_ref, bf_ref, att_hbm, ga_hbm, gb_hbm, x_hbm, wa_ref, wb_ref, wo_ref, g_ref,
             a_ref, pa_ref, pb_ref, mg_ref, mix_ref, x1_ref, *ring):
        bufs, sem = ring[:-1], ring[-1]
        i = pl.program_id(0)

        def fetch(step):
            slot = step % RING_SLOTS
            return [pltpu.make_async_copy(h.at[pl.ds(step * tm, tm)], b.at[slot], sem.at[k, slot])
                    for k, (h, b) in enumerate(zip((u_hbm, v_hbm, att_hbm, ga_hbm, gb_hbm, x_hbm), bufs))]

        @pl.when(i == 0)
        def _():
            for s in range(min(RING_SLOTS - 1, nsteps)):
                for cp in fetch(s):
                    cp.start()

        @pl.when(i + (RING_SLOTS - 1) < nsteps)
        def _():
            for cp in fetch(i + (RING_SLOTS - 1)):
                cp.start()

        for cp in fetch(i):
            cp.wait()
        slot = i % RING_SLOTS
        u_ref, v_ref, att_ref, ga_ref, gb_ref, x_ref = (b.at[slot] for b in bufs)
        pb = _dot(att_ref[...], wb_ref[...])
        nc, ug, _, _, _, _, _, _, _, mixed = _sgu_common(
            u_ref[...].astype(F32), v_ref[...].astype(F32), lng_ref[...], lnb_ref[...], ws_ref, bf_ref[...])
        mixed_all = jnp.concatenate(
            [jnp.concatenate([mixed[g][c] for g in range(NG)], axis=1) for c in range(nc)], axis=0)
        a = (ug * mixed_all).astype(BF16)
        a_ref[...] = a
        pa = _dot(a, wa_ref[...])
        pa_ref[...] = pa.astype(BF16)
        pb_ref[...] = pb.astype(BF16)
        merged = (_sigmoid(ga_ref[...].astype(F32)) * pa + _sigmoid(gb_ref[...].astype(F32)) * pb).astype(BF16)
        mg_ref[...] = merged
        mix = _dot(merged, wo_ref[...])
        mix_ref[...] = mix
        mhat, _ = _rms_hat(mix)
        x1_ref[...] = x_ref[...] + mhat * g_ref[...]

    sd = jax.ShapeDtypeStruct
    return _call(
        body, (u, vs, lng, lnb, ws, bfull, att, ga, gb, x, w_a, w_b, w_o, g2), name="sgu_merge_fwd", grid=(nsteps,),
        in_specs=[ANY, ANY, _const((1, D)), _const((1, D)), _const((NG, CH, CH)), _const((CH, D))]
        + [ANY] * 4 + [_resident((D, D))] * 3 + [_const((1, D))],
        out_specs=[_rows(tm, D)] * 6,
        out_shape=[sd((T, D), BF16)] * 4 + [sd((T, D), F32)] * 2,
        scratch_shapes=[pltpu.VMEM((RING_SLOTS, tm, D), s.dtype) for s in streamed]
        + [pltpu.SemaphoreType.DMA((len(streamed), RING_SLOTS))],
        sem=("arbitrary",), comm=comm, vmem=VMEM_PHYSICAL)


def _merge_bwd(dx1, mix, ga, gb, pa, pb, a, att, merged, w_a, w_b, w_o, g2, tm, comm=None):
    T = dx1.shape[0]
    nsteps = T // tm

    def body(dx1_ref, mix_ref, ga_ref, gb_ref, pa_ref, pb_ref, a_ref, att_ref, mg_ref, wa_ref, wb_ref, wo_ref, g_ref,
             dga_ref, dgb_ref, da_ref, datt_ref, dg_ref, dwa_ref, dwb_ref, dwo_ref, acc, sem):
        i = pl.program_id(0)

        @pl.when(i == 0)
        def _():
            dg_ref[...] = jnp.zeros_like(dg_ref)
            acc[...] = jnp.zeros_like(acc)

        mhat, r = _rms_hat(mix_ref[...])
        dmix, dg = _rms_bwd(mhat, r, g_ref[...], dx1_ref[...])
        dg_ref[...] += dg
        dmix = dmix.astype(BF16)
        dmerged = _dot_nt(dmix, wo_ref[...])
        sa = _sigmoid(ga_ref[...].astype(F32))
        sb = _sigmoid(gb_ref[...].astype(F32))
        dao = (dmerged * sa).astype(BF16)
        dbo = (dmerged * sb).astype(BF16)
        dga_ref[...] = (dmerged * pa_ref[...].astype(F32) * (sa * (1.0 - sa))).astype(BF16)
        dgb_ref[...] = (dmerged * pb_ref[...].astype(F32) * (sb * (1.0 - sb))).astype(BF16)
        da_ref[...] = _dot_nt(dao, wa_ref[...])
        datt_ref[...] = _dot_nt(dbo, wb_ref[...]).astype(BF16)
        acc[0] += _dot_tn(a_ref[...], dao)
        acc[1] += _dot_tn(att_ref[...], dbo)
        acc[2] += _dot_tn(mg_ref[...], dmix)

        @pl.when(i == nsteps - 1)
        def _():
            outs = [pltpu.make_async_copy(acc.at[j], ref, sem.at[j]) for j, ref in enumerate((dwa_ref, dwb_ref, dwo_ref))]
            for cp in outs:
                cp.start()
            for cp in outs:
                cp.wait()

    sd = jax.ShapeDtypeStruct
    return _call(
        body, (dx1, mix, ga, gb, pa, pb, a, att, merged, w_a, w_b, w_o, g2), name="merge_bwd", grid=(nsteps,),
        in_specs=[_rows(tm, D)] * 9 + [_resident((D, D))] * 3 + [_const((1, D))],
        out_specs=[_rows(tm, D)] * 4 + [_const((1, D))] + [ANY] * 3,
        out_shape=[sd((T, D), BF16), sd((T, D), BF16), sd((T, D), F32), sd((T, D), BF16), sd((1, D), F32)]
        + [sd((D, D), F32)] * 3,
        scratch_shapes=[pltpu.VMEM((3, D, D), F32), _dma_sems(3)], sem=("arbitrary",), comm=comm)


def _ffn(x1, target, w1, w2, g3, g4, tm):
    T = x1.shape[0]

    def body(x_ref, t_ref, w1_ref, w2_ref, g3_ref, g4_ref,
             hf_ref, f2_ref, dff_ref, df1_ref, dx_ref, ls_ref, dg3_ref, dg4_ref):
        @pl.when(pl.program_id(0) == 0)
        def _():
            ls_ref[...] = jnp.zeros_like(ls_ref)
            dg3_ref[...] = jnp.zeros_like(dg3_ref)
            dg4_ref[...] = jnp.zeros_like(dg4_ref)

        x = x_ref[...]
        g3, g4 = g3_ref[...], g4_ref[...]
        xhat, r3 = _rms_hat(x)
        hf = (xhat * g3).astype(BF16)
        hf_ref[...] = hf
        rl = jnp.maximum(_dot(hf, w1_ref[...]), 0.0)
        f2 = (rl * rl).astype(BF16)
        f2_ref[...] = f2
        fhat, r4 = _rms_hat(_dot(f2, w2_ref[...]))
        err = x + fhat * g4 - t_ref[...]
        ls_ref[...] += jnp.sum(err * err, axis=0, keepdims=True)
        dy = err * (1.0 / D)
        dff, dg4 = _rms_bwd(fhat, r4, g4, dy)
        dg4_ref[...] += dg4
        dff = dff.astype(BF16)
        dff_ref[...] = dff
        df1 = (_dot_nt(dff, w2_ref[...]) * (2.0 * rl)).astype(BF16)
        df1_ref[...] = df1
        dxn, dg3 = _rms_bwd(xhat, r3, g3, _dot_nt(df1, w1_ref[...]))
        dg3_ref[...] += dg3
        dx_ref[...] = dy + dxn

    sd = jax.ShapeDtypeStruct
    return pl.pallas_call(
        body, name="ffn_fwd_bwd", grid=(T // tm,),
        in_specs=[_rows(tm, D), _rows(tm, D), _resident((D, DFF)), _resident((DFF, D)), _const((1, D)), _const((1, D))],
        out_specs=[_rows(tm, D), _rows(tm, DFF), _rows(tm, D), _rows(tm, DFF), _rows(tm, D), _const((1, D)),
                   _const((1, D)), _const((1, D))],
        out_shape=[sd((T, D), BF16), sd((T, DFF), BF16), sd((T, D), BF16), sd((T, DFF), BF16), sd((T, D), F32),
                   sd((1, D), F32), sd((1, D), F32), sd((1, D), F32)],
        compiler_params=pltpu.CompilerParams(vmem_limit_bytes=VMEM_PHYSICAL, dimension_semantics=("arbitrary",)),
    )(x1, target, w1, w2, g3, g4)


def _sgu_inproj_bwd(u, vs, da, lng, lnb, ws, bfull, parts, x, dx1, g1, w_in, tm):
    T = x.shape[0]
    nsteps = T // tm
    widths = [p.shape[1] for p in parts]
    offs = [2 * D + sum(widths[:i]) for i in range(len(widths) + 1)]
    assert offs[-1] == IN_W
    n = len(parts)

    def body(*refs):
        u_ref, v_ref, da_ref, lng_ref, lnb_ref, ws_ref, bf_ref = refs[:7]
        prefs = refs[7:7 + n]
        x_ref, dx1_ref, g_ref, w_ref = refs[7 + n:11 + n]
        dx_ref, dp_ref, dg_ref, dws_ref, dbs_ref, dlg_ref, dlb_ref, db_ref = refs[11 + n:]
        i = pl.program_id(0)

        @pl.when(i == 0)
        def _():
            for ref in (dg_ref, dws_ref, db_ref, dlg_ref, dlb_ref):
                ref[...] = jnp.zeros_like(ref)

        for j in range(n):
            dp_ref[:, offs[j]:offs[j + 1]] = prefs[j][...]
        cut = 2 * D + (IN_W - 2 * D) // 2
        dh_a = _dot_nt(dp_ref[:, 2 * D:cut], w_ref[:, 2 * D:cut])
        u, vs, da, lng = u_ref[...].astype(F32), v_ref[...].astype(F32), da_ref[...], lng_ref[...]
        nc, ug, tu, tv, rstd, vhat, tri, wts, rhss, mixed = _sgu_common(u, vs, lng, lnb_ref[...], ws_ref, bf_ref[...])
        mixed_all = jnp.concatenate(
            [jnp.concatenate([mixed[g][c] for g in range(NG)], axis=1) for c in range(nc)], axis=0)
        dp_ref[:, :D] = (da * mixed_all * _gelu_grad(tu)).astype(BF16)
        dh_u = _dot_nt(dp_ref[:, :D], w_ref[:, :D])
        dmixed = da * ug
        dvn_cols = []
        for g in range(NG):
            dmix = [dmixed[CH * c:CH * (c + 1), CH * g:CH * (g + 1)] for c in range(nc)]
            db_ref[:, CH * g:CH * (g + 1)] += functools.reduce(lambda a, b: a + b, dmix)
            dm = jnp.concatenate(dmix, axis=1).astype(BF16)
            dws_ref[g] += _dot_nt(dm, rhss[g])
            dvn_cols.append(_dot_tn(wts[g], dm))
        dh_b = _dot_nt(dp_ref[:, cut:], w_ref[:, cut:])
        dvn = jnp.concatenate(
            [jnp.concatenate([dvn_cols[g][:, CH * c:CH * (c + 1)] for g in range(NG)], axis=1) for c in range(nc)],
            axis=0)
        dlg_ref[...] += jnp.sum(dvn * vhat, axis=0, keepdims=True)
        dlb_ref[...] += jnp.sum(dvn, axis=0, keepdims=True)
        dvh = dvn * lng
        dvg = rstd * (dvh - jnp.mean(dvh, axis=-1, keepdims=True)
                      - vhat * jnp.mean(dvh * vhat, axis=-1, keepdims=True))
        dp_ref[:, D:2 * D] = (dvg * _gelu_grad(tv)).astype(BF16)

        dh = (dh_a + dh_u) + (dh_b + _dot_nt(dp_ref[:, D:2 * D], w_ref[:, D:2 * D]))
        xhat, r = _rms_hat(x_ref[...])
        dxn, dg = _rms_bwd(xhat, r, g_ref[...], dh)
        dg_ref[...] += dg
        dx_ref[...] = dx1_ref[...] + dxn

        @pl.when(i == nsteps - 1)
        def _():
            for g in range(NG):
                dws_ref[g] = jnp.where(tri, dws_ref[g], 0.0)
                dbs_ref[g:g + 1, :] = jnp.sum(db_ref[:, CH * g:CH * (g + 1)].T, axis=0, keepdims=True)

    sd = jax.ShapeDtypeStruct
    outs, _ = _call(
        body, (u, vs, da, lng, lnb, ws, bfull, *parts, x, dx1, g1, w_in), name="sgu_inproj_bwd", grid=(nsteps,),
        in_specs=[_rows(tm, D), _rows(tm, D), _rows(tm, D), _const((1, D)), _const((1, D)), _const((NG, CH, CH)),
                  _const((CH, D))] + [_rows(tm, w) for w in widths]
        + [_rows(tm, D), _rows(tm, D), _const((1, D)), _resident((D, IN_W))],
        out_specs=[_rows(tm, D), _rows(tm, IN_W), _const((1, D)), _const((NG, CH, CH)), _const((NG, CH)), _const((1, D)),
                   _const((1, D))],
        out_shape=[sd((T, D), F32), sd((T, IN_W), BF16), sd((1, D), F32), sd((NG, CH, CH), F32), sd((NG, CH), F32),
                   sd((1, D), F32), sd((1, D), F32)],
        scratch_shapes=[pltpu.VMEM((CH, D), F32)], sem=("arbitrary",), vmem=VMEM_PHYSICAL)
    return outs


def _wgrad(a, g, tn, tm, name, comm=None, vmem=None):
    T, K = a.shape
    N = g.shape[1]

    def body(a_ref, g_ref, o_ref):
        @pl.when(pl.program_id(1) == 0)
        def _():
            o_ref[...] = jnp.zeros_like(o_ref)

        o_ref[...] += _dot_tn(a_ref[...], g_ref[...])

    return _call(
        body, (a, g), name=name, grid=(N // tn, T // tm),
        in_specs=[pl.BlockSpec((tm, K), lambda j, t: (t, 0)), pl.BlockSpec((tm, tn), lambda j, t: (t, j))],
        out_specs=pl.BlockSpec((K, tn), lambda j, t: (0, j)),
        out_shape=jax.ShapeDtypeStruct((K, N), F32), sem=("parallel", "arbitrary"), comm=comm, vmem=vmem)


def _adamw(ws, gs, ms, vs, trs, name):
    n = len(ws)
    walk = _Walk(w.shape[0] // tr for w, tr in zip(ws, trs))
    bc1 = 1.0 / (1.0 - B1 ** STEP)
    bc2 = 1.0 / (1.0 - B2 ** STEP)

    def body(*refs):
        i = pl.program_id(0)
        for k in range(n):
            mine = tuple(refs[j * n + k] for j in range(8))

            @pl.when(walk.mine(k, i))
            def _(mine=mine):
                w_ref, g_ref, m_ref, v_ref, go_ref, d_ref, nm_ref, nv_ref = mine
                g = g_ref[...]
                go_ref[...] = g
                m = B1 * m_ref[...] + (1.0 - B1) * g
                v = B2 * v_ref[...] + (1.0 - B2) * (g * g)
                nm_ref[...] = m
                nv_ref[...] = v
                d_ref[...] = -LR * ((m * bc1) / (jnp.sqrt(v * bc2) + AEPS) + WD * w_ref[...])

    def spec(k):
        return pl.BlockSpec((trs[k], ws[k].shape[1]), lambda i: (walk.tile(k, i), 0))

    specs = [spec(k) for k in range(n)]
    res = pl.pallas_call(
        body, name=name, grid=(walk.steps,), in_specs=specs * 4, out_specs=specs * 4,
        out_shape=[jax.ShapeDtypeStruct(w.shape, F32) for w in ws] * 4,
        compiler_params=_cparams(("arbitrary",)),
    )(*ws, *gs, *ms, *vs)
    return [tuple(res[j * n + k] for j in range(4)) for k in range(n)]


BIG = (("col", (D, IN_W)), ("row", (D, D)), ("row", (D, D)), ("row", (D, D)), ("col", (D, DFF)), ("row", (DFF, D)))
NBIG = len(BIG)
ANY = pl.BlockSpec(memory_space=pl.ANY)


def _shard_shape(kind, shape):
    R, C = shape
    return (R, C // 4) if kind == "col" else (R // 4, C)


def _half_shape(kind, shape):
    R, C = shape
    return (R // 2, C) if kind == "col" else (R, C // 2)


def _piece_shape(kind, shape):
    R, C = shape
    return (R // 2, C // 4) if kind == "col" else (R // 4, C // 2)


def _own_region(ref, kind, shape, s):
    R, C = shape
    return ref.at[:, pl.ds(s * (C // 4), C // 4)] if kind == "col" else ref.at[pl.ds(s * (R // 4), R // 4), :]


def _ag_region(ref, kind, shape, s, hc):
    R, C = shape
    if kind == "col":
        return ref.at[pl.ds(hc * (R // 2), R // 2), pl.ds(s * (C // 4), C // 4)]
    return ref.at[pl.ds(s * (R // 4) + hc * (R // 8), R // 8), :]


def _ag_shard_half(ref, kind, shape, hc):
    R, C = shape
    return ref.at[pl.ds(hc * (R // 2), R // 2), :] if kind == "col" else ref.at[pl.ds(hc * (R // 8), R // 8), :]


def _grad_half(ref, kind, shape, hc):
    R, C = shape
    return ref.at[pl.ds(hc * (R // 2), R // 2), :] if kind == "col" else ref.at[:, pl.ds(hc * (C // 2), C // 2)]


def _half_piece(ref, kind, shape, s):
    R, C = shape
    return ref.at[:, pl.ds(s * (C // 4), C // 4)] if kind == "col" else ref.at[pl.ds(s * (R // 4), R // 4), :]


def _place():
    x, y, c = lax.axis_index("x"), lax.axis_index("y"), lax.axis_index("c")
    chips = [(1 - x, y), (x, 1 - y), (1 - x, 1 - y)]
    return x, y, c, chips


def _rcopy(src, dst, ssem, rsem, dev):
    return pltpu.make_async_remote_copy(src_ref=src, dst_ref=dst, send_sem=ssem, recv_sem=rsem,
                                        device_id=dev, device_id_type=MESH)


def _dma_sems(n):
    return pltpu.SemaphoreType.DMA((n,))


def _x_gather_ici(shards, ws):
    n = len(ws)
    specs = [BIG[w] for w in ws]

    def place():
        x, y, c, chips = _place()
        return c, chips, 2 * x + y

    def sends(sh, full, sc):
        c, chips, me_s = place()
        return [_rcopy(_ag_shard_half(sh[i], kind, shape, c), _ag_region(full[i], kind, shape, me_s, c),
                       sc[0].at[3 * i + j], sc[1].at[3 * i + j], (cx, cy, c))
                for i, (kind, shape) in enumerate(specs) for j, (cx, cy) in enumerate(chips)]

    def start(sh, full, sc):
        for i in range(n):
            pltpu.make_async_copy(sh[i], sc[4 + i], sc[2].at[i]).start()
        for cp in sends(sh, full, sc):
            cp.start()

    def finish(sh, full, sc):
        c, chips, me_s = place()
        stores = []
        for i, (kind, shape) in enumerate(specs):
            pltpu.make_async_copy(sh[i], sc[4 + i], sc[2].at[i]).wait()
            st = pltpu.make_async_copy(sc[4 + i], _own_region(full[i], kind, shape, me_s), sc[3].at[i])
            st.start()
            stores.append(st)
        for i, (kind, shape) in enumerate(specs):
            for j, (cx, cy) in enumerate(chips):
                reg = _ag_region(full[i], kind, shape, 2 * cx + cy, c)
                _rcopy(reg, reg, sc[0].at[3 * i + j], sc[1].at[3 * i + j], (cx, cy, c)).wait_recv()
        for cp in sends(sh, full, sc):
            cp.wait_send()
        for st in stores:
            st.wait()

    return _Exchange(
        shards, [jax.ShapeDtypeStruct(shape, BF16) for _, shape in specs], {},
        [_dma_sems(3 * n), _dma_sems(3 * n), _dma_sems(n), _dma_sems(n)]
        + [pltpu.VMEM(_shard_shape(k, s), BF16) for k, s in specs], start, finish)


def _x_gather_d2d(wholes, ws):
    specs = [BIG[w] for w in ws]
    n = len(ws)

    def copies(full, sc, mine):
        x, y, c, chips = _place()
        hc = c if mine else 1 - c
        return [_rcopy(reg, reg, sc[0].at[3 * i + j], sc[1].at[3 * i + j], (x, y, 1 - c))
                for i, (kind, shape) in enumerate(specs) for j, (cx, cy) in enumerate(chips)
                for reg in [_ag_region(full[i], kind, shape, 2 * cx + cy, hc)]]

    def start(_, full, sc):
        for cp in copies(full, sc, True):
            cp.start()

    def finish(_, full, sc):
        for cp in copies(full, sc, False):
            cp.wait_recv()
        for cp in copies(full, sc, True):
            cp.wait_send()

    return _Exchange(wholes, [jax.ShapeDtypeStruct(shape, BF16) for _, shape in specs], {i: i for i in range(n)},
                     [_dma_sems(3 * n), _dma_sems(3 * n)], start, finish)


def _x_grads_sibling(grads, ws):
    specs = [BIG[w] for w in ws]
    n = len(ws)

    def copies(g, got, sc):
        x, y, c, _ = _place()
        return [_rcopy(_grad_half(g[i], kind, shape, 1 - c), got[i], sc[0].at[i], sc[1].at[i], (x, y, 1 - c))
                for i, (kind, shape) in enumerate(specs)]

    def start(g, got, sc):
        for cp in copies(g, got, sc):
            cp.start()

    def finish(g, got, sc):
        for cp in copies(g, got, sc):
            cp.wait_recv()
        for cp in copies(g, got, sc):
            cp.wait_send()

    return _Exchange(grads, [jax.ShapeDtypeStruct(_half_shape(k, s), F32) for k, s in specs], {},
                     [_dma_sems(n), _dma_sems(n)], start, finish)


def _x_grads_chips(sums_bf, ws):
    specs = [BIG[w] for w in ws]
    n = len(ws)

    def copies(s16, got, sc):
        x, y, c, chips = _place()
        return [_rcopy(_half_piece(s16[i], kind, shape, 2 * cx + cy), got[i].at[j],
                       sc[0].at[3 * i + j], sc[1].at[3 * i + j], (cx, cy, c))
                for i, (kind, shape) in enumerate(specs) for j, (cx, cy) in enumerate(chips)]

    def start(s16, got, sc):
        for cp in copies(s16, got, sc):
            cp.start()

    def finish(s16, got, sc):
        for cp in copies(s16, got, sc):
            cp.wait_recv()
        for cp in copies(s16, got, sc):
            cp.wait_send()

    return _Exchange(sums_bf, [jax.ShapeDtypeStruct((3,) + _piece_shape(k, s), BF16) for k, s in specs], {},
                     [_dma_sems(3 * n), _dma_sems(3 * n)], start, finish)


def _shard_half(ref, kind, shape, hc):
    sr, sc = _shard_shape(kind, shape)
    return ref.at[pl.ds(hc * (sr // 2), sr // 2), :] if kind == "col" else ref.at[:, pl.ds(hc * (sc // 2), sc // 2)]


def _x_grads_share(shard_grads, ws):
    specs = [BIG[w] for w in ws]
    n = len(ws)

    def copies(g, sc, mine):
        x, y, c, _ = _place()
        hc = c if mine else 1 - c
        return [_rcopy(part, part, sc[0].at[i], sc[1].at[i], (x, y, 1 - c))
                for i, (kind, shape) in enumerate(specs) for part in [_shard_half(g[i], kind, shape, hc)]]

    def start(_, g, sc):
        for cp in copies(g, sc, True):
            cp.start()

    def finish(_, g, sc):
        for cp in copies(g, sc, False):
            cp.wait_recv()
        for cp in copies(g, sc, True):
            cp.wait_send()

    return _Exchange(shard_grads, [jax.ShapeDtypeStruct(_shard_shape(k, s), F32) for k, s in specs],
                     {i: i for i in range(n)}, [_dma_sems(n), _dma_sems(n)], start, finish)


ADD_BLOCK_BYTES = 4 * 1024 * 1024


def _add_rows(rows, cols, n_arrays):
    limit = ADD_BLOCK_BYTES // (1 if n_arrays == 1 else 4)
    r = rows
    while r > 64 and r * cols * 4 > limit:
        r //= 2
    return r


class _Walk:
    def __init__(self, tiles):
        self.tiles = list(tiles)
        self.starts = [sum(self.tiles[:k]) for k in range(len(self.tiles))]
        self.steps = sum(self.tiles)

    def tile(self, k, i):
        return jnp.clip(i - self.starts[k], 0, self.tiles[k] - 1)

    def mine(self, k, i):
        return (i >= self.starts[k]) & (i < self.starts[k] + self.tiles[k])


def _add_halves(place, gs, gots, kinds, name):
    n = len(gs)
    halves = [_half_shape(kind, g.shape) for g, kind in zip(gs, kinds)]
    rows = [_add_rows(hr, hc, n) for hr, hc in halves]
    walk = _Walk(hr // r for (hr, _), r in zip(halves, rows))

    def body(p_ref, *refs):
        i = pl.program_id(0)
        for k in range(n):
            g_ref, b_ref, s_ref, sb_ref = (refs[j * n + k] for j in range(4))

            @pl.when(walk.mine(k, i))
            def _(g_ref=g_ref, b_ref=b_ref, s_ref=s_ref, sb_ref=sb_ref):
                s = g_ref[...] + b_ref[...]
                s_ref[...] = s
                sb_ref[...] = s.astype(BF16)

    def g_spec(k):
        if kinds[k] == "col":
            return pl.BlockSpec((rows[k], gs[k].shape[1]), lambda i, p: (p[0] * walk.tiles[k] + walk.tile(k, i), 0))
        return pl.BlockSpec((rows[k], halves[k][1]), lambda i, p: (walk.tile(k, i), p[0]))

    def spec(k):
        return pl.BlockSpec((rows[k], halves[k][1]), lambda i, p: (walk.tile(k, i), 0))

    specs = [spec(k) for k in range(n)]
    res = pl.pallas_call(
        body, name=name,
        grid_spec=pltpu.PrefetchScalarGridSpec(num_scalar_prefetch=1, grid=(walk.steps,),
                                               in_specs=[g_spec(k) for k in range(n)] + specs, out_specs=specs + specs),
        out_shape=[jax.ShapeDtypeStruct(h, F32) for h in halves] + [jax.ShapeDtypeStruct(h, BF16) for h in halves],
        compiler_params=_cparams(("arbitrary",)),
    )(place, *gs, *gots)
    return [(res[k], res[n + k]) for k in range(n)]


def _add_pieces(place, halves, gots, specs_big, name):
    n = len(halves)
    pieces = [_piece_shape(kind, shape) for kind, shape in specs_big]
    rows = [_add_rows(pr, pc, n) for pr, pc in pieces]
    walk = _Walk(pr // r for (pr, _), r in zip(pieces, rows))

    def body(p_ref, *refs):
        i = pl.program_id(0)
        for k in range(n):
            m_ref, g_ref, o_ref = (refs[j * n + k] for j in range(3))

            @pl.when(walk.mine(k, i))
            def _(m_ref=m_ref, g_ref=g_ref, o_ref=o_ref):
                acc = m_ref[...]
                for j in range(3):
                    acc = acc + g_ref[j].astype(F32)
                o_ref[...] = acc

    def m_spec(k):
        if specs_big[k][0] == "col":
            return pl.BlockSpec((rows[k], pieces[k][1]), lambda i, p: (walk.tile(k, i), p[1]))
        return pl.BlockSpec((rows[k], pieces[k][1]), lambda i, p: (p[1] * walk.tiles[k] + walk.tile(k, i), 0))

    def got_spec(k):
        return pl.BlockSpec((3, rows[k], pieces[k][1]), lambda i, p: (0, walk.tile(k, i), 0))

    def o_spec(k):
        if specs_big[k][0] == "col":
            return pl.BlockSpec((rows[k], pieces[k][1]), lambda i, p: (p[0] * walk.tiles[k] + walk.tile(k, i), 0))
        return pl.BlockSpec((rows[k], pieces[k][1]), lambda i, p: (walk.tile(k, i), p[0]))

    return pl.pallas_call(
        body, name=name,
        grid_spec=pltpu.PrefetchScalarGridSpec(
            num_scalar_prefetch=1, grid=(walk.steps,),
            in_specs=[m_spec(k) for k in range(n)] + [got_spec(k) for k in range(n)],
            out_specs=[o_spec(k) for k in range(n)]),
        out_shape=[jax.ShapeDtypeStruct(_shard_shape(kind, shape), F32) for kind, shape in specs_big],
        compiler_params=_cparams(("arbitrary",)),
    )(place, *halves, *gots)


SMALL_ROWS = 1024 + 8 * 8 + 8


def _x_small_all_reduce(p):
    def parts(p_ref, sc):
        slots, ssem, rsem = sc[0], sc[2], sc[3]
        x, y, c = lax.axis_index("x"), lax.axis_index("y"), lax.axis_index("c")
        me = 4 * x + 2 * y + c
        out = []
        for r in range(1, 8):
            bx, by, bc = (r >> 2) & 1, (r >> 1) & 1, r & 1
            tgt = (1 - x if bx else x, 1 - y if by else y, 1 - c if bc else c)
            send = _rcopy(p_ref, slots.at[me], ssem.at[r - 1], rsem.at[r - 1], tgt)
            src = 4 * tgt[0] + 2 * tgt[1] + tgt[2]
            recv = _rcopy(p_ref, slots.at[src], ssem.at[r - 1], rsem.at[r - 1], tgt)
            out.append((send, recv))
        return me, out

    def start(ins, outs, sc):
        me, cps = parts(ins[0], sc)
        pltpu.make_async_copy(ins[0], sc[0].at[me], sc[4].at[0]).start()
        for send, _ in cps:
            send.start()

    def finish(ins, outs, sc):
        me, cps = parts(ins[0], sc)
        pltpu.make_async_copy(ins[0], sc[0].at[me], sc[4].at[0]).wait()
        for _, recv in cps:
            recv.wait_recv()
        acc = sc[0][0]
        for d in range(1, 8):
            acc = acc + sc[0][d]
        sc[1][...] = acc
        back = pltpu.make_async_copy(sc[1], outs[0], sc[4].at[1])
        back.start()
        for send, _ in cps:
            send.wait_send()
        back.wait()

    return _Exchange([p], [jax.ShapeDtypeStruct((SMALL_ROWS, CH), F32)], {},
                     [pltpu.VMEM((8, SMALL_ROWS, CH), F32), pltpu.VMEM((SMALL_ROWS, CH), F32), _dma_sems(7), _dma_sems(7),
                      _dma_sems(2)], start, finish)


def _rope_tables(positions, comm=None):
    T = positions.shape[0]
    inv_freq = 500000.0 ** (-jnp.arange(0, 2 * ROPE_HALF, 2, dtype=F32) / (2 * ROPE_HALF))
    head = jnp.concatenate([inv_freq, inv_freq, jnp.zeros((HD - 2 * ROPE_HALF,), F32)])
    lane_freq = jnp.concatenate([head, head])[None, :]
    pos = jnp.broadcast_to(positions.astype(F32)[:, None], (T, CH))
    tm = min(1024, T)

    def body(p_ref, f_ref, c_ref, s1_ref, s2_ref):
        ang = p_ref[...] * f_ref[...]
        sin = jnp.sin(ang)
        first = (lax.broadcasted_iota(jnp.int32, ang.shape, 1) % HD) < ROPE_HALF
        c_ref[...] = jnp.cos(ang)
        s1_ref[...] = jnp.where(first, -sin, 0.0)
        s2_ref[...] = jnp.where(first, 0.0, sin)

    return _call(body, (pos, lane_freq), name="rope_tables", grid=(T // tm,),
                 in_specs=[_rows(tm, CH), _const((1, CH))], out_specs=[_rows(tm, CH)] * 3,
                 out_shape=[jax.ShapeDtypeStruct((T, CH), F32)] * 3, sem=("parallel",), comm=comm)


BIG_NAMES = ("w_in", "w_a", "w_b", "w_o", "w_ff_in", "w_ff_out")
SMALL_NAMES = ("w_spatial", "ln_v_gain", "ln_v_bias", "b_spatial", "sinks", "norm_mix_pre", "norm_mix_post",
               "norm_ff_pre", "norm_ff_post")
WEIGHT_ORDER = ("w_in", "ln_v_gain", "ln_v_bias", "w_spatial", "b_spatial", "sinks", "w_a", "w_b", "w_o",
                "norm_mix_pre", "norm_mix_post", "w_ff_in", "w_ff_out", "norm_ff_pre", "norm_ff_post")


def _pack_small(d, loss_sums=None):
    parts = []
    for n in SMALL_NAMES:
        flat = d[n].reshape(-1)
        pad = (-flat.shape[0]) % (8 * CH)
        parts.append(jnp.pad(flat, (0, pad)).reshape(-1, CH))
    parts.append(jnp.zeros((8, CH), F32) if loss_sums is None else loss_sums.reshape(8, CH))
    return jnp.concatenate(parts, axis=0)


def _unpack_small(p, like):
    out, row = {}, 0
    for n in SMALL_NAMES:
        size = like[n].size
        rows = -(-size // (8 * CH)) * 8
        out[n] = p[row:row + rows].reshape(-1)[:size].reshape(like[n].shape)
        row += rows
    return out


def kernel(x, positions, w_in, ln_v_gain, ln_v_bias, w_spatial, b_spatial, sinks, w_a, w_b, w_o, norm_mix_pre, norm_mix_post, w_ff_in, w_ff_out, norm_ff_pre, norm_ff_post, loss_target, m_w_in, m_ln_v_gain, m_ln_v_bias, m_w_spatial, m_b_spatial, m_sinks, m_w_a, m_w_b, m_w_o, m_norm_mix_pre, m_norm_mix_post, m_w_ff_in, m_w_ff_out, m_norm_ff_pre, m_norm_ff_post, v_w_in, v_ln_v_gain, v_ln_v_bias, v_w_spatial, v_b_spatial, v_sinks, v_w_a, v_w_b, v_w_o, v_norm_mix_pre, v_norm_mix_post, v_w_ff_in, v_w_ff_out, v_norm_ff_pre, v_norm_ff_post):
    w = dict(w_in=w_in, ln_v_gain=ln_v_gain, ln_v_bias=ln_v_bias, w_spatial=w_spatial, b_spatial=b_spatial, sinks=sinks,
             w_a=w_a, w_b=w_b, w_o=w_o, norm_mix_pre=norm_mix_pre, norm_mix_post=norm_mix_post, w_ff_in=w_ff_in,
             w_ff_out=w_ff_out, norm_ff_pre=norm_ff_pre, norm_ff_post=norm_ff_post)
    m = dict(w_in=m_w_in, ln_v_gain=m_ln_v_gain, ln_v_bias=m_ln_v_bias, w_spatial=m_w_spatial, b_spatial=m_b_spatial,
             sinks=m_sinks, w_a=m_w_a, w_b=m_w_b, w_o=m_w_o, norm_mix_pre=m_norm_mix_pre, norm_mix_post=m_norm_mix_post,
             w_ff_in=m_w_ff_in, w_ff_out=m_w_ff_out, norm_ff_pre=m_norm_ff_pre, norm_ff_post=m_norm_ff_post)
    v = dict(w_in=v_w_in, ln_v_gain=v_ln_v_gain, ln_v_bias=v_ln_v_bias, w_spatial=v_w_spatial, b_spatial=v_b_spatial,
             sinks=v_sinks, w_a=v_w_a, w_b=v_w_b, w_o=v_w_o, norm_mix_pre=v_norm_mix_pre, norm_mix_post=v_norm_mix_post,
             w_ff_in=v_w_ff_in, w_ff_out=v_w_ff_out, norm_ff_pre=v_norm_ff_pre, norm_ff_post=v_norm_ff_post)

    FIRST, REST = (0,), tuple(range(1, NBIG))
    shards = [w[n][0].astype(BF16) for n in BIG_NAMES]
    place = jnp.stack([lax.axis_index("c"), 2 * lax.axis_index("x") + lax.axis_index("y")]).astype(jnp.int32)
    xs, target = x[0], loss_target[0]
    T = xs.shape[0]
    tile = min(TOKEN_TILE, T)
    wtiles = {n: dict(tm=min(tm, T), tn=tn) for n, (tm, tn) in WGRAD_TILES.items()}
    g1, g2, g3, g4 = norm_mix_pre, norm_mix_post, norm_ff_pre, norm_ff_post
    w_sp, snk = w_spatial[0], sinks[0]
    MIX, FF = (1, 2, 3), (4, 5)
    bfull = jnp.repeat(b_spatial[0].T, CH, axis=1)

    def reduce_tail(ws, grads, got):
        tag = "_".join(BIG_NAMES[k] for k in ws)
        sums = _add_halves(place, grads, got, [BIG[k][0] for k in ws], name="grad_add_sibling_" + tag)
        return sums, _x_grads_chips([s[1] for s in sums], ws)

    def reduce_end(ws, sums, pieces):
        tag = "_".join(BIG_NAMES[k] for k in ws)
        return _add_pieces(place, [s[0] for s in sums], pieces, [BIG[k] for k in ws], name="grad_add_chips_" + tag)

    (rc, rs1, rs2), w_in_part = _rope_tables(positions[0], comm=_x_gather_ici(shards[:1], FIRST))
    w_in_b = _run(_x_gather_d2d(w_in_part, FIRST), "gather_w_in_d2d")[0]
    EARLY, FF_OUT = (1, 2, 3, 4), (5,)
    (h, u, vs, q, k, va, ga, gb), early_part = _inproj(xs, g1, w_in_b, rc, rs1, rs2, tm=tile,
                                                      comm=_x_gather_ici(shards[1:5], EARLY))
    att, (w_a_b, w_b_b, w_o_b, w_ff_in_b, ffo_part) = _attn_fwd(
        q, k, va, snk, comm=_both(_x_gather_d2d(early_part, EARLY), _x_gather_ici(shards[5:], FF_OUT)))
    (a, pa, pb, merged, mix, x1), (w_ff_out_b,) = _sgu_merge_fwd(
        u, vs, ln_v_gain, ln_v_bias, w_sp, bfull, att, ga, gb, xs, w_a_b, w_b_b, w_o_b, g2, tm=tile,
        comm=_x_gather_d2d([ffo_part], FF_OUT))
    hf, f2, dff, df1, dx1, lsum, dg3, dg4 = _ffn(x1, target, w_ff_in_b, w_ff_out_b, g3, g4, tm=tile)

    dw_ff_out, _ = _wgrad(f2, dff, name="wgrad_ff_out", **wtiles["w_ff_out"])
    dw_ff_in, _ = _wgrad(hf, df1, name="wgrad_ff_in", **wtiles["w_ff_in"])
    grads_ff = [dw_ff_in, dw_ff_out]
    (dga, dgb, da, datt, dg2, dw_a, dw_b, dw_o), got_ff = _merge_bwd(
        dx1, mix, ga, gb, pa, pb, a, att, merged, w_a_b, w_b_b, w_o_b, g2, tm=tile, comm=_x_grads_sibling(grads_ff, FF))
    grads_mix = [dw_a, dw_b, dw_o]
    sums_ff, ff_to_chips = reduce_tail(FF, grads_ff, got_ff)
    (dq, dk, dva, dsk), (*pieces_ff, got_a, got_b, got_o) = _attn_bwd(
        q, k, va, datt, snk, rc, rs1, rs2, comm=_both(ff_to_chips, _x_grads_sibling(grads_mix, MIX)))
    partial_ff = reduce_end(FF, sums_ff, pieces_ff)
    sums_mix, mix_to_chips = reduce_tail(MIX, grads_mix, [got_a, got_b, got_o])
    dx, dproj, dg1, dws, dbs, dlg, dlb = _sgu_inproj_bwd(
        u, vs, da, ln_v_gain, ln_v_bias, w_sp, bfull, [dq, dk, dva, dga, dgb], xs, dx1, g1, w_in_b, tm=tile)
    small = dict(ln_v_gain=dlg, ln_v_bias=dlb, w_spatial=dws, b_spatial=dbs, sinks=dsk[:, :NQ],
                 norm_mix_pre=dg1, norm_mix_post=dg2, norm_ff_pre=dg3, norm_ff_post=dg4)
    dw_in, (gs, shard_ff_in, shard_ff_out, *pieces_mix) = _wgrad(
        h, dproj, name="wgrad_in", vmem=VMEM_PHYSICAL, **wtiles["w_in"],
        comm=_both(_both(_x_small_all_reduce(_pack_small(small, lsum)), _x_grads_share(partial_ff, FF)), mix_to_chips))
    partial_mix = reduce_end(MIX, sums_mix, pieces_mix)
    got_in = _run(_x_grads_sibling([dw_in], FIRST), "grads_in_to_sibling")
    sums_in, to_chips = reduce_tail(FIRST, [dw_in], got_in)
    partial_in = reduce_end(FIRST, sums_in, _run(to_chips, "grads_in_to_chips"))
    g_in, *shard_mix = _run(_x_grads_share(list(partial_in) + list(partial_mix), FIRST + MIX), "grads_in_mix_share")
    shard_rest = list(shard_mix) + [shard_ff_in, shard_ff_out]

    loss = 0.5 * jnp.sum(gs[SMALL_ROWS - 8:]) / D
    grad, delta, new_m, new_v = {}, {}, {}, {}
    for n, g in zip(BIG_NAMES, [g_in] + list(shard_rest)):
        (g_, d_, m_, v_), = _adamw([w[n][0]], [g], [m[n][0]], [v[n][0]], [256], name="adamw_" + n)
        grad[n], delta[n], new_m[n], new_v[n] = g_[None], d_[None], m_[None], v_[None]
    (gs, ds, ms, vs), = _adamw([_pack_small(w)], [gs], [_pack_small(m)], [_pack_small(v)], [SMALL_ROWS], name="adamw_small")
    for packed, dst in ((gs, grad), (ds, delta), (ms, new_m), (vs, new_v)):
        dst.update(_unpack_small(packed, w))

    outs = [loss, dx[None]]
    for group in (grad, delta, new_m, new_v):
        outs.extend(group[n] for n in WEIGHT_ORDER)
    return tuple(outs)
```

```python
import functools

import jax
import jax.numpy as jnp
from jax import lax
from jax.experimental import pallas as pl
from jax.experimental.pallas import tpu as pltpu

F32 = jnp.float32
BF16 = jnp.bfloat16

D = 1024
CH = 128
NG = 8
HD = 64
NQ = 16
NKV = 4
KVW = NKV * HD
DFF = 4 * D
EPS = 1e-6
IN_W = 5632
SEG = (0, 1024, 2048, 3072, 3328, 3584, 4608, 5632)
ROPE_HALF = 8
Q_SCALE = HD ** -0.5

LR, B1, B2, AEPS, WD, STEP = 0.001, 0.9, 0.999, 1e-08, 0.01, 10

VMEM_PHYSICAL = 64 * 1024 * 1024
VMEM_LIMIT = 60 * 1024 * 1024
MESH = pl.DeviceIdType.MESH

TOKEN_TILE = 512
WGRAD_TILES = {"w_ff_out": (512, 1024), "w_ff_in": (2048, 2048), "w_in": (2048, IN_W // 2)}

_GELU_C0 = 0.7978845608028654
_GELU_C1 = 0.044715


def _cparams(sem=None, vmem=None):
    kw = dict(vmem_limit_bytes=VMEM_LIMIT if vmem is None else vmem)
    if sem is not None:
        kw["dimension_semantics"] = sem
    return pltpu.CompilerParams(**kw)


def _resident(shape):
    nd = len(shape)
    return pl.BlockSpec(shape, lambda *_: (0,) * nd, pipeline_mode=pl.Buffered(1))


def _const(shape):
    nd = len(shape)
    return pl.BlockSpec(shape, lambda *_: (0,) * nd)


def _rows(tm, w):
    return pl.BlockSpec((tm, w), lambda i: (i, 0))


class _Exchange:
    def __init__(self, ins, outs, aliases, scratch, start, finish):
        self.ins, self.outs, self.aliases, self.scratch = list(ins), list(outs), dict(aliases), list(scratch)
        self.start, self.finish = start, finish


def _both(a, b):
    na, ma, sa = len(a.ins), len(a.outs), len(a.scratch)

    def start(ci, co, cs):
        a.start(ci[:na], co[:ma], cs[:sa])
        b.start(ci[na:], co[ma:], cs[sa:])

    def finish(ci, co, cs):
        a.finish(ci[:na], co[:ma], cs[:sa])
        b.finish(ci[na:], co[ma:], cs[sa:])

    aliases = {**a.aliases, **{na + i: ma + j for i, j in b.aliases.items()}}
    return _Exchange(a.ins + b.ins, a.outs + b.outs, aliases, a.scratch + b.scratch, start, finish)


def _call(body, args, *, name, grid, in_specs, out_specs, out_shape, scratch_shapes=(), sem=None, comm=None, vmem=None):
    single = not isinstance(out_shape, (list, tuple))
    out_shape = [out_shape] if single else list(out_shape)
    out_specs = [out_specs] if single else list(out_specs)
    if comm is None:
        res = pl.pallas_call(body, name=name, grid=grid, in_specs=list(in_specs), out_specs=out_specs,
                             out_shape=out_shape, scratch_shapes=list(scratch_shapes),
                             compiler_params=_cparams(sem, vmem))(*args)
        return (res[0] if single else res), []
    n_in, n_out, n_scr = len(args), len(out_shape), len(scratch_shapes)
    nci, nco = len(comm.ins), len(comm.outs)
    steps = 1
    for g in grid:
        steps *= g

    def hosted(*refs):
        a, ci = refs[:n_in], refs[n_in:n_in + nci]
        o, co = refs[n_in + nci:n_in + nci + n_out], refs[n_in + nci + n_out:n_in + nci + n_out + nco]
        rest = refs[n_in + nci + n_out + nco:]
        scr, cs = rest[:n_scr], rest[n_scr:]
        step = pl.program_id(0)
        for d in range(1, len(grid)):
            step = step * grid[d] + pl.program_id(d)

        @pl.when(step == 0)
        def _():
            comm.start(ci, co, cs)

        body(*a, *o, *scr)

        @pl.when(step == steps - 1)
        def _():
            comm.finish(ci, co, cs)

    res = pl.pallas_call(
        hosted, name=name, grid=grid, in_specs=list(in_specs) + [ANY] * nci, out_specs=out_specs + [ANY] * nco,
        out_shape=out_shape + comm.outs, scratch_shapes=list(scratch_shapes) + comm.scratch,
        input_output_aliases={n_in + i: n_out + j for i, j in comm.aliases.items()},
        compiler_params=_cparams(("arbitrary",) * len(grid), vmem),
    )(*args, *comm.ins)
    own = res[:n_out]
    return (own[0] if single else own), list(res[n_out:])


def _run(comm, name):
    nci = len(comm.ins)

    def body(*refs):
        ci, co, cs = refs[:nci], refs[nci:nci + len(comm.outs)], refs[nci + len(comm.outs):]
        comm.start(ci, co, cs)
        comm.finish(ci, co, cs)

    return pl.pallas_call(
        body, name=name, in_specs=[ANY] * nci, out_specs=[ANY] * len(comm.outs), out_shape=comm.outs,
        scratch_shapes=comm.scratch, input_output_aliases=comm.aliases,
        compiler_params=pltpu.CompilerParams(vmem_limit_bytes=VMEM_LIMIT),
    )(*comm.ins)


def _gelu(x):
    x2 = x * x
    t = jnp.tanh(x * (_GELU_C0 + (_GELU_C0 * _GELU_C1) * x2))
    hx = 0.5 * x
    return hx + hx * t, (t, x2, hx)


def _gelu_grad(parts):
    t, x2, hx = parts
    return (0.5 + 0.5 * t) + hx * (1.0 - t * t) * (_GELU_C0 + (3.0 * _GELU_C0 * _GELU_C1) * x2)


def _sigmoid(x):
    return 1.0 / (1.0 + jnp.exp(-x))


def _rms_hat(x):
    r = lax.rsqrt(jnp.mean(x * x, axis=-1, keepdims=True) + EPS)
    return x * r, r


def _rms_bwd(xhat, r, g, dout):
    dg = jnp.sum(dout * xhat, axis=0, keepdims=True)
    dy = dout * g
    dx = r * (dy - xhat * jnp.mean(dy * xhat, axis=-1, keepdims=True))
    return dx, dg


def _dot(a, b):
    return jnp.dot(a, b, preferred_element_type=F32)


def _dot_nt(a, b):
    return lax.dot_general(a, b, (((1,), (1,)), ((), ())), preferred_element_type=F32)


def _dot_tn(a, b):
    return lax.dot_general(a, b, (((0,), (0,)), ((), ())), preferred_element_type=F32)


def _rope(blk, c, s1, s2):
    return blk * c + pltpu.roll(blk, CH - ROPE_HALF, 1) * s1 + pltpu.roll(blk, ROPE_HALF, 1) * s2


def _rope_t(blk, c, s1, s2):
    return blk * c + pltpu.roll(blk * s1, ROPE_HALF, 1) + pltpu.roll(blk * s2, CH - ROPE_HALF, 1)


def _inproj(x, g1, w_in, rc, rs1, rs2, tm, comm=None):
    T = x.shape[0]

    def body(x_ref, g_ref, w_ref, c_ref, s1_ref, s2_ref,
             h_ref, u_ref, v_ref, q_ref, k_ref, va_ref, ga_ref, gb_ref):
        xhat, _ = _rms_hat(x_ref[...])
        h = (xhat * g_ref[...]).astype(BF16)
        h_ref[...] = h
        uv = _dot(h, w_ref[:, SEG[0]:SEG[2]])
        uv = uv.astype(BF16)
        u_ref[...] = uv[:, :D]
        v_ref[...] = uv[:, D:]
        c, s1, s2 = c_ref[...], s1_ref[...], s2_ref[...]
        qkv = _dot(h, w_ref[:, SEG[2]:SEG[5]])
        for p in range(D // CH):
            blk = _rope(qkv[:, CH * p:CH * (p + 1)], c, s1, s2) * Q_SCALE
            q_ref[:, CH * p:CH * (p + 1)] = blk.astype(BF16)
        for p in range(KVW // CH):
            k_ref[:, CH * p:CH * (p + 1)] = _rope(qkv[:, D + CH * p:D + CH * (p + 1)], c, s1, s2).astype(BF16)
        va_ref[...] = qkv[:, D + KVW:].astype(BF16)
        gates = _dot(h, w_ref[:, SEG[5]:SEG[7]]).astype(BF16)
        ga_ref[...] = gates[:, :D]
        gb_ref[...] = gates[:, D:]

    sd = jax.ShapeDtypeStruct
    return _call(
        body, (x, g1, w_in, rc, rs1, rs2), name="inproj_fwd", grid=(T // tm,),
        in_specs=[_rows(tm, D), _const((1, D)), _resident((D, IN_W)), _rows(tm, CH), _rows(tm, CH), _rows(tm, CH)],
        out_specs=[_rows(tm, D), _rows(tm, D), _rows(tm, D), _rows(tm, D), _rows(tm, KVW), _rows(tm, KVW),
                   _rows(tm, D), _rows(tm, D)],
        out_shape=[sd((T, D), BF16), sd((T, D), BF16), sd((T, D), BF16), sd((T, D), BF16), sd((T, KVW), BF16),
                   sd((T, KVW), BF16), sd((T, D), BF16), sd((T, D), BF16)],
        sem=("parallel",), comm=comm)


def _sgu_common(u, vs, lng, lnb, ws_ref, bfull):
    nc = u.shape[0] // CH
    ug, tu = _gelu(u)
    vg, tv = _gelu(vs)
    mu = jnp.mean(vg, axis=-1, keepdims=True)
    xc = vg - mu
    rstd = lax.rsqrt(jnp.mean(xc * xc, axis=-1, keepdims=True) + EPS)
    vhat = xc * rstd
    vnb = (vhat * lng + lnb).astype(BF16)
    tri = lax.broadcasted_iota(jnp.int32, (CH, CH), 0) >= lax.broadcasted_iota(jnp.int32, (CH, CH), 1)
    wts, rhss, mixed = [], [], []
    for g in range(NG):
        wt = jnp.where(tri, ws_ref[g], 0.0).astype(BF16)
        rhs = jnp.concatenate([vnb[CH * c:CH * (c + 1), CH * g:CH * (g + 1)] for c in range(nc)], axis=1)
        mix = _dot(wt, rhs)
        wts.append(wt)
        rhss.append(rhs)
        mixed.append([mix[:, CH * c:CH * (c + 1)] + bfull[:, CH * g:CH * (g + 1)] for c in range(nc)])
    return nc, ug, tu, tv, rstd, vhat, tri, wts, rhss, mixed


def _pair_layout(prev, cur, grp):
    j, half = grp // 2, grp % 2
    blk = jnp.concatenate([prev[:, CH * j:CH * (j + 1)], cur[:, CH * j:CH * (j + 1)]], axis=0).astype(F32)
    lo = lax.broadcasted_iota(jnp.int32, blk.shape, 1) < HD
    rolled = pltpu.roll(blk, HD, 1)
    even = jnp.where(lo, blk if half == 0 else rolled, 0.0)
    odd = jnp.where(lo, 0.0, rolled if half == 0 else blk)
    return jnp.concatenate([even, odd], axis=0).astype(BF16)


def _attn_mask(n):
    qi = lax.broadcasted_iota(jnp.int32, (CH, 2 * CH), 0)
    kc = lax.broadcasted_iota(jnp.int32, (CH, 2 * CH), 1)
    ok = (kc > qi) & (kc <= qi + CH) & ((kc >= CH) | (n > 0))
    return jnp.concatenate([ok, ok], axis=1)


def _softmax_sink(s, sink):
    m = jnp.maximum(jnp.max(s, axis=-1, keepdims=True), sink)
    p = jnp.exp(s - m)
    ps = jnp.exp(sink - m)
    inv = 1.0 / (jnp.sum(p, axis=-1, keepdims=True) + ps)
    return p * inv, ps * inv


QUERY_BLOCKS_PER_STEP = 2
RING_SLOTS = 3


def _attn_fwd(q, k, va, sinks, comm=None):
    T = q.shape[0]
    nblk = QUERY_BLOCKS_PER_STEP
    nsteps = T // (nblk * CH)
    npairs = D // CH

    def body(sk_ref, q_ref, kp_ref, kc_ref, vp_ref, vc_ref, o_ref):
        n = pl.program_id(0)
        even_lanes = lax.broadcasted_iota(jnp.int32, (CH, CH), 1) < HD
        ks = [kp_ref[...]] + [kc_ref[CH * b:CH * (b + 1)] for b in range(nblk)]
        vs = [vp_ref[...]] + [vc_ref[CH * b:CH * (b + 1)] for b in range(nblk)]
        masks = [_attn_mask(nblk * n)] + [_attn_mask(1)] * (nblk - 1)
        kks = [[_pair_layout(ks[b], ks[b + 1], grp) for grp in range(NKV)] for b in range(nblk)]
        vvs = [[_pair_layout(vs[b], vs[b + 1], grp) for grp in range(NKV)] for b in range(nblk)]
        work = [(b, p) for b in range(nblk) for p in range(npairs)]

        def scores(i):
            b, p = work[i]
            return _dot_nt(q_ref[CH * b:CH * (b + 1), CH * p:CH * (p + 1)], kks[b][p // 2])

        def unnormalised(s, sink):
            m = jnp.maximum(jnp.max(s, axis=-1, keepdims=True), sink)
            p = jnp.exp(s - m)
            return p, 1.0 / (jnp.sum(p, axis=-1, keepdims=True) + jnp.exp(sink - m))

        def value_product(i):
            b, p = work[i]
            pr, ie, io = probs[i]
            return _dot(pr, vvs[b][p // 2]) * jnp.where(even_lanes, ie, io)

        ahead = 3
        outs, probs = [], []
        pending = [scores(i) for i in range(ahead)]
        for i, (b, p) in enumerate(work):
            s = jnp.where(masks[b], pending.pop(0), -1e30)
            if i + ahead < len(work):
                pending.append(scores(i + ahead))
            pe, ie = unnormalised(s[:, :2 * CH], sk_ref[2 * p])
            po, io = unnormalised(s[:, 2 * CH:], sk_ref[2 * p + 1])
            probs.append((jnp.concatenate([pe, po], axis=1).astype(BF16), ie, io))
            if i >= 1:
                outs.append(value_product(i - 1))
        outs.append(value_product(len(work) - 1))
        for b in range(nblk):
            o_ref[CH * b:CH * (b + 1), :] = jnp.concatenate(outs[npairs * b:npairs * (b + 1)], axis=1).astype(BF16)

    prev = lambda n: (jnp.maximum(nblk * n - 1, 0), 0)
    cur = lambda n: (n, 0)
    return _call(
        body, (sinks, q, k, k, va, va), name="attn_fwd", grid=(nsteps,),
        in_specs=[pl.BlockSpec(memory_space=pltpu.SMEM), pl.BlockSpec((nblk * CH, D), cur),
                  pl.BlockSpec((CH, KVW), prev), pl.BlockSpec((nblk * CH, KVW), cur),
                  pl.BlockSpec((CH, KVW), prev), pl.BlockSpec((nblk * CH, KVW), cur)],
        out_specs=pl.BlockSpec((nblk * CH, D), cur), out_shape=jax.ShapeDtypeStruct((T, D), BF16),
        sem=("parallel",), comm=comm)


def _attn_bwd(q, k, va, datt, sinks, rc, rs1, rs2, comm=None):
    T = q.shape[0]
    nb = T // CH

    def body(sk_ref, q_ref, kp_ref, kc_ref, vp_ref, vc_ref, do_ref, cq_ref, s1q_ref, s2q_ref, ck_ref, s1k_ref, s2k_ref,
             dq_ref, dk_ref, dv_ref, dsk_ref, kcar, vcar):
        n = pl.program_id(0)

        @pl.when(n == 0)
        def _():
            kcar[...] = jnp.zeros_like(kcar)
            vcar[...] = jnp.zeros_like(vcar)
            dsk_ref[...] = jnp.zeros_like(dsk_ref)

        def flush(kprev, vprev):
            ck, s1k, s2k = ck_ref[...], s1k_ref[...], s2k_ref[...]
            for j in range(KVW // CH):
                sl = slice(CH * j, CH * (j + 1))
                dk_ref[:, sl] = _rope_t(kcar[:, sl] + kprev[:, sl], ck, s1k, s2k).astype(BF16)
                dv_ref[:, sl] = (vcar[:, sl] + vprev[:, sl]).astype(BF16)

        @pl.when(n < nb)
        def _():
            mask = _attn_mask(n)
            kp, kc, vp, vc = kp_ref[...], kc_ref[...], vp_ref[...], vc_ref[...]
            cq, s1q, s2q = cq_ref[...], s1q_ref[...], s2q_ref[...]
            lane = lax.broadcasted_iota(jnp.int32, (1, CH), 1)
            dsk = jnp.zeros((1, CH), F32)
            npairs = D // CH
            kks = [_pair_layout(kp, kc, grp) for grp in range(NKV)]
            vvs = [_pair_layout(vp, vc, grp) for grp in range(NKV)]
            qs = [q_ref[:, CH * p:CH * (p + 1)] for p in range(npairs)]
            dos = [do_ref[:, CH * p:CH * (p + 1)].astype(BF16) for p in range(npairs)]

            def first(p):
                return _dot_nt(qs[p], kks[p // 2]), _dot_nt(dos[p], vvs[p // 2])

            def last(p, ds, pb):
                return (_rope_t(_dot(ds, kks[p // 2]), cq, s1q, s2q) * Q_SCALE, _dot_tn(qs[p], ds), _dot_tn(dos[p], pb))

            ahead = 2
            pending = [first(p) for p in range(ahead)]
            mids, ends = [], []
            for p in range(npairs):
                s, dp = pending.pop(0)
                s = jnp.where(mask, s, -1e30)
                if p + ahead < npairs:
                    pending.append(first(p + ahead))
                ds_parts, p_parts = [], []
                for par in range(2):
                    sl = slice(2 * CH * par, 2 * CH * (par + 1))
                    pr, psink = _softmax_sink(s[:, sl], sk_ref[2 * p + par])
                    delta = jnp.sum(pr * dp[:, sl], axis=-1, keepdims=True)
                    ds_parts.append(pr * (dp[:, sl] - delta))
                    p_parts.append(pr)
                    tot = -jnp.sum(psink * delta, axis=0, keepdims=True)
                    dsk = dsk + jnp.where(lane == 2 * p + par, tot, 0.0)
                mids.append((jnp.concatenate(ds_parts, axis=1).astype(BF16), jnp.concatenate(p_parts, axis=1).astype(BF16)))
                if p >= 1:
                    ends.append(last(p - 1, *mids[p - 1]))
            ends.append(last(npairs - 1, *mids[-1]))
            dq_cols = [e[0] for e in ends]
            def fold(i):
                rows = []
                for grp in range(NKV):
                    acc = ends[2 * grp][i] + ends[2 * grp + 1][i]
                    rows.append(acc[:HD, :2 * CH] + acc[HD:, 2 * CH:])
                return jnp.concatenate(rows, axis=0).T

            dkf, dvf = fold(1), fold(2)
            dq_ref[...] = jnp.concatenate(dq_cols, axis=1).astype(BF16)
            dsk_ref[...] += dsk
            flush(dkf[:CH], dvf[:CH])
            kcar[...] = dkf[CH:]
            vcar[...] = dvf[CH:]

        @pl.when(n == nb)
        def _():
            z = jnp.zeros((CH, KVW), F32)
            flush(z, z)

    last = nb - 1
    cur = lambda n: (jnp.minimum(n, last), 0)
    prev = lambda n: (jnp.clip(n - 1, 0, last), 0)
    sd = jax.ShapeDtypeStruct
    return _call(
        body, (sinks, q, k, k, va, va, datt, rc, rs1, rs2, rc, rs1, rs2), name="attn_bwd", grid=(nb + 1,),
        in_specs=[pl.BlockSpec(memory_space=pltpu.SMEM), pl.BlockSpec((CH, D), cur),
                  pl.BlockSpec((CH, KVW), prev), pl.BlockSpec((CH, KVW), cur),
                  pl.BlockSpec((CH, KVW), prev), pl.BlockSpec((CH, KVW), cur),
                  pl.BlockSpec((CH, D), cur),
                  pl.BlockSpec((CH, CH), cur), pl.BlockSpec((CH, CH), cur), pl.BlockSpec((CH, CH), cur),
                  pl.BlockSpec((CH, CH), prev), pl.BlockSpec((CH, CH), prev), pl.BlockSpec((CH, CH), prev)],
        out_specs=[pl.BlockSpec((CH, D), cur), pl.BlockSpec((CH, KVW), prev), pl.BlockSpec((CH, KVW), prev),
                   _const((1, CH))],
        out_shape=[sd((T, D), BF16), sd((T, KVW), BF16), sd((T, KVW), BF16), sd((1, CH), F32)],
        scratch_shapes=[pltpu.VMEM((CH, KVW), F32), pltpu.VMEM((CH, KVW), F32)], sem=("arbitrary",), comm=comm)


def _sgu_merge_fwd(u, vs, lng, lnb, ws, bfull, att, ga, gb, x, w_a, w_b, w_o, g2, tm, comm=None):
    T = x.shape[0]
    nsteps = T // tm
    streamed = (u, vs, att, ga, gb, x)

    def body(u_hbm, v_hbm, lng_ref, lnb_ref, ws_ref, bf_ref, att_hbm, ga_hbm, gb_hbm, x_hbm, wa_ref, wb_ref, wo_ref, g_ref,
             a_ref, pa_ref, pb_ref, mix_ref, x1_ref, *ring):
        bufs, sem = ring[:-1], ring[-1]
        i = pl.program_id(0)

        def fetch(step):
            slot = step % RING_SLOTS
            return [pltpu.make_async_copy(h.at[pl.ds(step * tm, tm)], b.at[slot], sem.at[k, slot])
                    for k, (h, b) in enumerate(zip((u_hbm, v_hbm, att_hbm, ga_hbm, gb_hbm, x_hbm), bufs))]

        @pl.when(i == 0)
        def _():
            for s in range(min(RING_SLOTS - 1, nsteps)):
                for cp in fetch(s):
                    cp.start()

        @pl.when(i + (RING_SLOTS - 1) < nsteps)
        def _():
            for cp in fetch(i + (RING_SLOTS - 1)):
                cp.start()

        for cp in fetch(i):
            cp.wait()
        slot = i % RING_SLOTS
        u_ref, v_ref, att_ref, ga_ref, gb_ref, x_ref = (b.at[slot] for b in bufs)
        pb = _dot(att_ref[...], wb_ref[...])
        nc, ug, _, _, _, _, _, _, _, mixed = _sgu_common(
            u_ref[...].astype(F32), v_ref[...].astype(F32), lng_ref[...], lnb_ref[...], ws_ref, bf_ref[...])
        mixed_all = jnp.concatenate(
            [jnp.concatenate([mixed[g][c] for g in range(NG)], axis=1) for c in range(nc)], axis=0)
        a = (ug * mixed_all).astype(BF16)
        a_ref[...] = a
        pa = _dot(a, wa_ref[...])
        pa_ref[...] = pa.astype(BF16)
        pb_ref[...] = pb.astype(BF16)
        merged = (_sigmoid(ga_ref[...].astype(F32)) * pa + _sigmoid(gb_ref[...].astype(F32)) * pb).astype(BF16)
        mix = _dot(merged, wo_ref[...])
        mix_ref[...] = mix.astype(BF16)
        mhat, _ = _rms_hat(mix)
        x1_ref[...] = x_ref[...] + mhat * g_ref[...]

    sd = jax.ShapeDtypeStruct
    return _call(
        body, (u, vs, lng, lnb, ws, bfull, att, ga, gb, x, w_a, w_b, w_o, g2), name="sgu_merge_fwd", grid=(nsteps,),
        in_specs=[ANY, ANY, _const((1, D)), _const((1, D)), _const((NG, CH, CH)), _const((CH, D))]
        + [ANY] * 4 + [_resident((D, D))] * 3 + [_const((1, D))],
        out_specs=[_rows(tm, D)] * 5,
        out_shape=[sd((T, D), BF16)] * 4 + [sd((T, D), F32)],
        scratch_shapes=[pltpu.VMEM((RING_SLOTS, tm, D), s.dtype) for s in streamed]
        + [pltpu.SemaphoreType.DMA((len(streamed), RING_SLOTS))],
        sem=("arbitrary",), comm=comm, vmem=VMEM_PHYSICAL)


def _merge_bwd(dx1, mix, ga, gb, pa, pb, a, att, w_a, w_b, w_o, g2, tm, comm=None):
    T = dx1.shape[0]
    nsteps = T // tm

    def body(dx1_ref, mix_ref, ga_ref, gb_ref, pa_ref, pb_ref, a_ref, att_ref, wa_ref, wb_ref, wo_ref, g_ref,
             dga_ref, dgb_ref, da_ref, datt_ref, dg_ref, dwa_ref, dwb_ref, dwo_ref, acc, sem):
        i = pl.program_id(0)

        @pl.when(i == 0)
        def _():
            dg_ref[...] = jnp.zeros_like(dg_ref)
            acc[...] = jnp.zeros_like(acc)

        mhat, r = _rms_hat(mix_ref[...].astype(F32))
        dmix, dg = _rms_bwd(mhat, r, g_ref[...], dx1_ref[...])
        dg_ref[...] += dg
        dmix = dmix.astype(BF16)
        dmerged = _dot_nt(dmix, wo_ref[...])
        sa = _sigmoid(ga_ref[...].astype(F32))
        sb = _sigmoid(gb_ref[...].astype(F32))
        pa = pa_ref[...].astype(F32)
        pb = pb_ref[...].astype(F32)
        merged = (sa * pa + sb * pb).astype(BF16)
        dao = (dmerged * sa).astype(BF16)
        dbo = (dmerged * sb).astype(BF16)
        dga_ref[...] = (dmerged * pa * (sa * (1.0 - sa))).astype(BF16)
        dgb_ref[...] = (dmerged * pb * (sb * (1.0 - sb))).astype(BF16)
        da_ref[...] = _dot_nt(dao, wa_ref[...])
        datt_ref[...] = _dot_nt(dbo, wb_ref[...]).astype(BF16)
        acc[0] += _dot_tn(a_ref[...], dao)
        acc[1] += _dot_tn(att_ref[...], dbo)
        acc[2] += _dot_tn(merged, dmix)

        @pl.when(i == nsteps - 1)
        def _():
            outs = [pltpu.make_async_copy(acc.at[j], ref, sem.at[j]) for j, ref in enumerate((dwa_ref, dwb_ref, dwo_ref))]
            for cp in outs:
                cp.start()
            for cp in outs:
                cp.wait()

    sd = jax.ShapeDtypeStruct
    return _call(
        body, (dx1, mix, ga, gb, pa, pb, a, att, w_a, w_b, w_o, g2), name="merge_bwd", grid=(nsteps,),
        in_specs=[_rows(tm, D)] * 8 + [_resident((D, D))] * 3 + [_const((1, D))],
        out_specs=[_rows(tm, D)] * 4 + [_const((1, D))] + [ANY] * 3,
        out_shape=[sd((T, D), BF16), sd((T, D), BF16), sd((T, D), F32), sd((T, D), BF16), sd((1, D), F32)]
        + [sd((D, D), F32)] * 3,
        scratch_shapes=[pltpu.VMEM((3, D, D), F32), _dma_sems(3)], sem=("arbitrary",), comm=comm)


def _ffn(x1, target, w1, w2, g3, g4, tm):
    T = x1.shape[0]

    def body(x_ref, t_ref, w1_ref, w2_ref, g3_ref, g4_ref,
             hf_ref, f2_ref, dff_ref, df1_ref, dx_ref, ls_ref, dg3_ref, dg4_ref):
        @pl.when(pl.program_id(0) == 0)
        def _():
            ls_ref[...] = jnp.zeros_like(ls_ref)
            dg3_ref[...] = jnp.zeros_like(dg3_ref)
            dg4_ref[...] = jnp.zeros_like(dg4_ref)

        x = x_ref[...]
        g3, g4 = g3_ref[...], g4_ref[...]
        xhat, r3 = _rms_hat(x)
        hf = (xhat * g3).astype(BF16)
        hf_ref[...] = hf
        rl = jnp.maximum(_dot(hf, w1_ref[...]), 0.0)
        f2 = (rl * rl).astype(BF16)
        f2_ref[...] = f2
        fhat, r4 = _rms_hat(_dot(f2, w2_ref[...]))
        err = x + fhat * g4 - t_ref[...]
        ls_ref[...] += jnp.sum(err * err, axis=0, keepdims=True)
        dy = err * (1.0 / D)
        dff, dg4 = _rms_bwd(fhat, r4, g4, dy)
        dg4_ref[...] += dg4
        dff = dff.astype(BF16)
        dff_ref[...] = dff
        df1 = (_dot_nt(dff, w2_ref[...]) * (2.0 * rl)).astype(BF16)
        df1_ref[...] = df1
        dxn, dg3 = _rms_bwd(xhat, r3, g3, _dot_nt(df1, w1_ref[...]))
        dg3_ref[...] += dg3
        dx_ref[...] = dy + dxn

    sd = jax.ShapeDtypeStruct
    return pl.pallas_call(
        body, name="ffn_fwd_bwd", grid=(T // tm,),
        in_specs=[_rows(tm, D), _rows(tm, D), _resident((D, DFF)), _resident((DFF, D)), _const((1, D)), _const((1, D))],
        out_specs=[_rows(tm, D), _rows(tm, DFF), _rows(tm, D), _rows(tm, DFF), _rows(tm, D), _const((1, D)),
                   _const((1, D)), _const((1, D))],
        out_shape=[sd((T, D), BF16), sd((T, DFF), BF16), sd((T, D), BF16), sd((T, DFF), BF16), sd((T, D), F32),
                   sd((1, D), F32), sd((1, D), F32), sd((1, D), F32)],
        compiler_params=pltpu.CompilerParams(vmem_limit_bytes=VMEM_PHYSICAL, dimension_semantics=("arbitrary",)),
    )(x1, target, w1, w2, g3, g4)


def _sgu_inproj_bwd(u, vs, da, lng, lnb, ws, bfull, parts, x, dx1, g1, w_in, tm):
    T = x.shape[0]
    nsteps = T // tm
    widths = [p.shape[1] for p in parts]
    offs = [2 * D + sum(widths[:i]) for i in range(len(widths) + 1)]
    assert offs[-1] == IN_W
    n = len(parts)

    def body(*refs):
        u_ref, v_ref, da_ref, lng_ref, lnb_ref, ws_ref, bf_ref = refs[:7]
        prefs = refs[7:7 + n]
        x_ref, dx1_ref, g_ref, w_ref = refs[7 + n:11 + n]
        dx_ref, dp_ref, dg_ref, dws_ref, dbs_ref, dlg_ref, dlb_ref, db_ref = refs[11 + n:]
        i = pl.program_id(0)

        @pl.when(i == 0)
        def _():
            for ref in (dg_ref, dws_ref, db_ref, dlg_ref, dlb_ref):
                ref[...] = jnp.zeros_like(ref)

        for j in range(n):
            dp_ref[:, offs[j]:offs[j + 1]] = prefs[j][...]
        cut = 2 * D + (IN_W - 2 * D) // 2
        dh_a = _dot_nt(dp_ref[:, 2 * D:cut], w_ref[:, 2 * D:cut])
        u, vs, da, lng = u_ref[...].astype(F32), v_ref[...].astype(F32), da_ref[...], lng_ref[...]
        nc, ug, tu, tv, rstd, vhat, tri, wts, rhss, mixed = _sgu_common(u, vs, lng, lnb_ref[...], ws_ref, bf_ref[...])
        mixed_all = jnp.concatenate(
            [jnp.concatenate([mixed[g][c] for g in range(NG)], axis=1) for c in range(nc)], axis=0)
        dp_ref[:, :D] = (da * mixed_all * _gelu_grad(tu)).astype(BF16)
        dh_u = _dot_nt(dp_ref[:, :D], w_ref[:, :D])
        dmixed = da * ug
        dvn_cols = []
        for g in range(NG):
            dmix = [dmixed[CH * c:CH * (c + 1), CH * g:CH * (g + 1)] for c in range(nc)]
            db_ref[:, CH * g:CH * (g + 1)] += functools.reduce(lambda a, b: a + b, dmix)
            dm = jnp.concatenate(dmix, axis=1).astype(BF16)
            dws_ref[g] += _dot_nt(dm, rhss[g])
            dvn_cols.append(_dot_tn(wts[g], dm))
        dh_b = _dot_nt(dp_ref[:, cut:], w_ref[:, cut:])
        dvn = jnp.concatenate(
            [jnp.concatenate([dvn_cols[g][:, CH * c:CH * (c + 1)] for g in range(NG)], axis=1) for c in range(nc)],
            axis=0)
        dlg_ref[...] += jnp.sum(dvn * vhat, axis=0, keepdims=True)
        dlb_ref[...] += jnp.sum(dvn, axis=0, keepdims=True)
        dvh = dvn * lng
        dvg = rstd * (dvh - jnp.mean(dvh, axis=-1, keepdims=True)
                      - vhat * jnp.mean(dvh * vhat, axis=-1, keepdims=True))
        dp_ref[:, D:2 * D] = (dvg * _gelu_grad(tv)).astype(BF16)

        dh = (dh_a + dh_u) + (dh_b + _dot_nt(dp_ref[:, D:2 * D], w_ref[:, D:2 * D]))
        xhat, r = _rms_hat(x_ref[...])
        dxn, dg = _rms_bwd(xhat, r, g_ref[...], dh)
        dg_ref[...] += dg
        dx_ref[...] = dx1_ref[...] + dxn

        @pl.when(i == nsteps - 1)
        def _():
            for g in range(NG):
                dws_ref[g] = jnp.where(tri, dws_ref[g], 0.0)
                dbs_ref[g:g + 1, :] = jnp.sum(db_ref[:, CH * g:CH * (g + 1)].T, axis=0, keepdims=True)

    sd = jax.ShapeDtypeStruct
    outs, _ = _call(
        body, (u, vs, da, lng, lnb, ws, bfull, *parts, x, dx1, g1, w_in), name="sgu_inproj_bwd", grid=(nsteps,),
        in_specs=[_rows(tm, D), _rows(tm, D), _rows(tm, D), _const((1, D)), _const((1, D)), _const((NG, CH, CH)),
                  _const((CH, D))] + [_rows(tm, w) for w in widths]
        + [_rows(tm, D), _rows(tm, D), _const((1, D)), _resident((D, IN_W))],
        out_specs=[_rows(tm, D), _rows(tm, IN_W), _const((1, D)), _const((NG, CH, CH)), _const((NG, CH)), _const((1, D)),
                   _const((1, D))],
        out_shape=[sd((T, D), F32), sd((T, IN_W), BF16), sd((1, D), F32), sd((NG, CH, CH), F32), sd((NG, CH), F32),
                   sd((1, D), F32), sd((1, D), F32)],
        scratch_shapes=[pltpu.VMEM((CH, D), F32)], sem=("arbitrary",), vmem=VMEM_PHYSICAL)
    return outs


def _wgrad(a, g, tn, tm, name, comm=None, vmem=None):
    T, K = a.shape
    N = g.shape[1]

    def body(a_ref, g_ref, o_ref):
        @pl.when(pl.program_id(1) == 0)
        def _():
            o_ref[...] = jnp.zeros_like(o_ref)

        o_ref[...] += _dot_tn(a_ref[...], g_ref[...])

    return _call(
        body, (a, g), name=name, grid=(N // tn, T // tm),
        in_specs=[pl.BlockSpec((tm, K), lambda j, t: (t, 0)), pl.BlockSpec((tm, tn), lambda j, t: (t, j))],
        out_specs=pl.BlockSpec((K, tn), lambda j, t: (0, j)),
        out_shape=jax.ShapeDtypeStruct((K, N), F32), sem=("parallel", "arbitrary"), comm=comm, vmem=vmem)


def _adamw(ws, gs, ms, vs, trs, name):
    n = len(ws)
    walk = _Walk(w.shape[0] // tr for w, tr in zip(ws, trs))
    bc1 = 1.0 / (1.0 - B1 ** STEP)
    bc2 = 1.0 / (1.0 - B2 ** STEP)

    def body(*refs):
        i = pl.program_id(0)
        for k in range(n):
            mine = tuple(refs[j * n + k] for j in range(8))

            @pl.when(walk.mine(k, i))
            def _(mine=mine):
                w_ref, g_ref, m_ref, v_ref, go_ref, d_ref, nm_ref, nv_ref = mine
                g = g_ref[...]
                go_ref[...] = g
                m = B1 * m_ref[...] + (1.0 - B1) * g
                v = B2 * v_ref[...] + (1.0 - B2) * (g * g)
                nm_ref[...] = m
                nv_ref[...] = v
                d_ref[...] = -LR * ((m * bc1) / (jnp.sqrt(v * bc2) + AEPS) + WD * w_ref[...])

    def spec(k):
        return pl.BlockSpec((trs[k], ws[k].shape[1]), lambda i: (walk.tile(k, i), 0))

    specs = [spec(k) for k in range(n)]
    res = pl.pallas_call(
        body, name=name, grid=(walk.steps,), in_specs=specs * 4, out_specs=specs * 4,
        out_shape=[jax.ShapeDtypeStruct(w.shape, F32) for w in ws] * 4,
        compiler_params=_cparams(("arbitrary",)),
    )(*ws, *gs, *ms, *vs)
    return [tuple(res[j * n + k] for j in range(4)) for k in range(n)]


BIG = (("col", (D, IN_W)), ("row", (D, D)), ("row", (D, D)), ("row", (D, D)), ("col", (D, DFF)), ("row", (DFF, D)))
NBIG = len(BIG)
ANY = pl.BlockSpec(memory_space=pl.ANY)


def _shard_shape(kind, shape):
    R, C = shape
    return (R, C // 4) if kind == "col" else (R // 4, C)


def _half_shape(kind, shape):
    R, C = shape
    return (R // 2, C) if kind == "col" else (R, C // 2)


def _piece_shape(kind, shape):
    R, C = shape
    return (R // 2, C // 4) if kind == "col" else (R // 4, C // 2)


def _own_region(ref, kind, shape, s):
    R, C = shape
    return ref.at[:, pl.ds(s * (C // 4), C // 4)] if kind == "col" else ref.at[pl.ds(s * (R // 4), R // 4), :]


def _ag_region(ref, kind, shape, s, hc):
    R, C = shape
    if kind == "col":
        return ref.at[pl.ds(hc * (R // 2), R // 2), pl.ds(s * (C // 4), C // 4)]
    return ref.at[pl.ds(s * (R // 4) + hc * (R // 8), R // 8), :]


def _ag_shard_half(ref, kind, shape, hc):
    R, C = shape
    return ref.at[pl.ds(hc * (R // 2), R // 2), :] if kind == "col" else ref.at[pl.ds(hc * (R // 8), R // 8), :]


def _grad_half(ref, kind, shape, hc):
    R, C = shape
    return ref.at[pl.ds(hc * (R // 2), R // 2), :] if kind == "col" else ref.at[:, pl.ds(hc * (C // 2), C // 2)]


def _half_piece(ref, kind, shape, s):
    R, C = shape
    return ref.at[:, pl.ds(s * (C // 4), C // 4)] if kind == "col" else ref.at[pl.ds(s * (R // 4), R // 4), :]


def _place():
    x, y, c = lax.axis_index("x"), lax.axis_index("y"), lax.axis_index("c")
    chips = [(1 - x, y), (x, 1 - y), (1 - x, 1 - y)]
    return x, y, c, chips


def _rcopy(src, dst, ssem, rsem, dev):
    return pltpu.make_async_remote_copy(src_ref=src, dst_ref=dst, send_sem=ssem, recv_sem=rsem,
                                        device_id=dev, device_id_type=MESH)


def _dma_sems(n):
    return pltpu.SemaphoreType.DMA((n,))


def _x_gather_ici(shards, ws):
    n = len(ws)
    specs = [BIG[w] for w in ws]

    def place():
        x, y, c, chips = _place()
        return c, chips, 2 * x + y

    def sends(sh, full, sc):
        c, chips, me_s = place()
        return [_rcopy(_ag_shard_half(sh[i], kind, shape, c), _ag_region(full[i], kind, shape, me_s, c),
                       sc[0].at[3 * i + j], sc[1].at[3 * i + j], (cx, cy, c))
                for i, (kind, shape) in enumerate(specs) for j, (cx, cy) in enumerate(chips)]

    def start(sh, full, sc):
        for i in range(n):
            pltpu.make_async_copy(sh[i], sc[4 + i], sc[2].at[i]).start()
        for cp in sends(sh, full, sc):
            cp.start()

    def finish(sh, full, sc):
        c, chips, me_s = place()
        stores = []
        for i, (kind, shape) in enumerate(specs):
            pltpu.make_async_copy(sh[i], sc[4 + i], sc[2].at[i]).wait()
            st = pltpu.make_async_copy(sc[4 + i], _own_region(full[i], kind, shape, me_s), sc[3].at[i])
            st.start()
            stores.append(st)
        for i, (kind, shape) in enumerate(specs):
            for j, (cx, cy) in enumerate(chips):
                reg = _ag_region(full[i], kind, shape, 2 * cx + cy, c)
                _rcopy(reg, reg, sc[0].at[3 * i + j], sc[1].at[3 * i + j], (cx, cy, c)).wait_recv()
        for cp in sends(sh, full, sc):
            cp.wait_send()
        for st in stores:
            st.wait()

    return _Exchange(
        shards, [jax.ShapeDtypeStruct(shape, BF16) for _, shape in specs], {},
        [_dma_sems(3 * n), _dma_sems(3 * n), _dma_sems(n), _dma_sems(n)]
        + [pltpu.VMEM(_shard_shape(k, s), BF16) for k, s in specs], start, finish)


def _x_gather_d2d(wholes, ws):
    specs = [BIG[w] for w in ws]
    n = len(ws)

    def copies(full, sc, mine):
        x, y, c, chips = _place()
        hc = c if mine else 1 - c
        return [_rcopy(reg, reg, sc[0].at[3 * i + j], sc[1].at[3 * i + j], (x, y, 1 - c))
                for i, (kind, shape) in enumerate(specs) for j, (cx, cy) in enumerate(chips)
                for reg in [_ag_region(full[i], kind, shape, 2 * cx + cy, hc)]]

    def start(_, full, sc):
        for cp in copies(full, sc, True):
            cp.start()

    def finish(_, full, sc):
        for cp in copies(full, sc, False):
            cp.wait_recv()
        for cp in copies(full, sc, True):
            cp.wait_send()

    return _Exchange(wholes, [jax.ShapeDtypeStruct(shape, BF16) for _, shape in specs], {i: i for i in range(n)},
                     [_dma_sems(3 * n), _dma_sems(3 * n)], start, finish)


def _x_grads_sibling(grads, ws):
    specs = [BIG[w] for w in ws]
    n = len(ws)

    def copies(g, got, sc):
        x, y, c, _ = _place()
        return [_rcopy(_grad_half(g[i], kind, shape, 1 - c), got[i], sc[0].at[i], sc[1].at[i], (x, y, 1 - c))
                for i, (kind, shape) in enumerate(specs)]

    def start(g, got, sc):
        for cp in copies(g, got, sc):
            cp.start()

    def finish(g, got, sc):
        for cp in copies(g, got, sc):
            cp.wait_recv()
        for cp in copies(g, got, sc):
            cp.wait_send()

    return _Exchange(grads, [jax.ShapeDtypeStruct(_half_shape(k, s), F32) for k, s in specs], {},
                     [_dma_sems(n), _dma_sems(n)], start, finish)


def _x_grads_chips(sums_bf, ws):
    specs = [BIG[w] for w in ws]
    n = len(ws)

    def copies(s16, got, sc):
        x, y, c, chips = _place()
        return [_rcopy(_half_piece(s16[i], kind, shape, 2 * cx + cy), got[i].at[j],
                       sc[0].at[3 * i + j], sc[1].at[3 * i + j], (cx, cy, c))
                for i, (kind, shape) in enumerate(specs) for j, (cx, cy) in enumerate(chips)]

    def start(s16, got, sc):
        for cp in copies(s16, got, sc):
            cp.start()

    def finish(s16, got, sc):
        for cp in copies(s16, got, sc):
            cp.wait_recv()
        for cp in copies(s16, got, sc):
            cp.wait_send()

    return _Exchange(sums_bf, [jax.ShapeDtypeStruct((3,) + _piece_shape(k, s), BF16) for k, s in specs], {},
                     [_dma_sems(3 * n), _dma_sems(3 * n)], start, finish)


def _shard_half(ref, kind, shape, hc):
    sr, sc = _shard_shape(kind, shape)
    return ref.at[pl.ds(hc * (sr // 2), sr // 2), :] if kind == "col" else ref.at[:, pl.ds(hc * (sc // 2), sc // 2)]


def _x_grads_share(shard_grads, ws):
    specs = [BIG[w] for w in ws]
    n = len(ws)

    def copies(g, sc, mine):
        x, y, c, _ = _place()
        hc = c if mine else 1 - c
        return [_rcopy(part, part, sc[0].at[i], sc[1].at[i], (x, y, 1 - c))
                for i, (kind, shape) in enumerate(specs) for part in [_shard_half(g[i], kind, shape, hc)]]

    def start(_, g, sc):
        for cp in copies(g, sc, True):
            cp.start()

    def finish(_, g, sc):
        for cp in copies(g, sc, False):
            cp.wait_recv()
        for cp in copies(g, sc, True):
            cp.wait_send()

    return _Exchange(shard_grads, [jax.ShapeDtypeStruct(_shard_shape(k, s), F32) for k, s in specs],
                     {i: i for i in range(n)}, [_dma_sems(n), _dma_sems(n)], start, finish)


ADD_BLOCK_BYTES = 4 * 1024 * 1024


def _add_rows(rows, cols, n_arrays):
    limit = ADD_BLOCK_BYTES // (1 if n_arrays == 1 else 4)
    r = rows
    while r > 64 and r * cols * 4 > limit:
        r //= 2
    return r


class _Walk:
    def __init__(self, tiles):
        self.tiles = list(tiles)
        self.starts = [sum(self.tiles[:k]) for k in range(len(self.tiles))]
        self.steps = sum(self.tiles)

    def tile(self, k, i):
        return jnp.clip(i - self.starts[k], 0, self.tiles[k] - 1)

    def mine(self, k, i):
        return (i >= self.starts[k]) & (i < self.starts[k] + self.tiles[k])


def _add_halves(place, gs, gots, kinds, name):
    n = len(gs)
    halves = [_half_shape(kind, g.shape) for g, kind in zip(gs, kinds)]
    rows = [_add_rows(hr, hc, n) for hr, hc in halves]
    walk = _Walk(hr // r for (hr, _), r in zip(halves, rows))

    def body(p_ref, *refs):
        i = pl.program_id(0)
        for k in range(n):
            g_ref, b_ref, s_ref, sb_ref = (refs[j * n + k] for j in range(4))

            @pl.when(walk.mine(k, i))
            def _(g_ref=g_ref, b_ref=b_ref, s_ref=s_ref, sb_ref=sb_ref):
                s = g_ref[...] + b_ref[...]
                s_ref[...] = s
                sb_ref[...] = s.astype(BF16)

    def g_spec(k):
        if kinds[k] == "col":
            return pl.BlockSpec((rows[k], gs[k].shape[1]), lambda i, p: (p[0] * walk.tiles[k] + walk.tile(k, i), 0))
        return pl.BlockSpec((rows[k], halves[k][1]), lambda i, p: (walk.tile(k, i), p[0]))

    def spec(k):
        return pl.BlockSpec((rows[k], halves[k][1]), lambda i, p: (walk.tile(k, i), 0))

    specs = [spec(k) for k in range(n)]
    res = pl.pallas_call(
        body, name=name,
        grid_spec=pltpu.PrefetchScalarGridSpec(num_scalar_prefetch=1, grid=(walk.steps,),
                                               in_specs=[g_spec(k) for k in range(n)] + specs, out_specs=specs + specs),
        out_shape=[jax.ShapeDtypeStruct(h, F32) for h in halves] + [jax.ShapeDtypeStruct(h, BF16) for h in halves],
        compiler_params=_cparams(("arbitrary",)),
    )(place, *gs, *gots)
    return [(res[k], res[n + k]) for k in range(n)]


def _add_pieces(place, halves, gots, specs_big, name):
    n = len(halves)
    pieces = [_piece_shape(kind, shape) for kind, shape in specs_big]
    rows = [_add_rows(pr, pc, n) for pr, pc in pieces]
    walk = _Walk(pr // r for (pr, _), r in zip(pieces, rows))

    def body(p_ref, *refs):
        i = pl.program_id(0)
        for k in range(n):
            m_ref, g_ref, o_ref = (refs[j * n + k] for j in range(3))

            @pl.when(walk.mine(k, i))
            def _(m_ref=m_ref, g_ref=g_ref, o_ref=o_ref):
                acc = m_ref[...]
                for j in range(3):
                    acc = acc + g_ref[j].astype(F32)
                o_ref[...] = acc

    def m_spec(k):
        if specs_big[k][0] == "col":
            return pl.BlockSpec((rows[k], pieces[k][1]), lambda i, p: (walk.tile(k, i), p[1]))
        return pl.BlockSpec((rows[k], pieces[k][1]), lambda i, p: (p[1] * walk.tiles[k] + walk.tile(k, i), 0))

    def got_spec(k):
        return pl.BlockSpec((3, rows[k], pieces[k][1]), lambda i, p: (0, walk.tile(k, i), 0))

    def o_spec(k):
        if specs_big[k][0] == "col":
            return pl.BlockSpec((rows[k], pieces[k][1]), lambda i, p: (p[0] * walk.tiles[k] + walk.tile(k, i), 0))
        return pl.BlockSpec((rows[k], pieces[k][1]), lambda i, p: (walk.tile(k, i), p[0]))

    return pl.pallas_call(
        body, name=name,
        grid_spec=pltpu.PrefetchScalarGridSpec(
            num_scalar_prefetch=1, grid=(walk.steps,),
            in_specs=[m_spec(k) for k in range(n)] + [got_spec(k) for k in range(n)],
            out_specs=[o_spec(k) for k in range(n)]),
        out_shape=[jax.ShapeDtypeStruct(_shard_shape(kind, shape), F32) for kind, shape in specs_big],
        compiler_params=_cparams(("arbitrary",)),
    )(place, *halves, *gots)


SMALL_ROWS = 1024 + 8 * 8 + 8


def _x_small_all_reduce(p):
    def parts(p_ref, sc):
        slots, ssem, rsem = sc[0], sc[2], sc[3]
        x, y, c = lax.axis_index("x"), lax.axis_index("y"), lax.axis_index("c")
        me = 4 * x + 2 * y + c
        out = []
        for r in range(1, 8):
            bx, by, bc = (r >> 2) & 1, (r >> 1) & 1, r & 1
            tgt = (1 - x if bx else x, 1 - y if by else y, 1 - c if bc else c)
            send = _rcopy(p_ref, slots.at[me], ssem.at[r - 1], rsem.at[r - 1], tgt)
            src = 4 * tgt[0] + 2 * tgt[1] + tgt[2]
            recv = _rcopy(p_ref, slots.at[src], ssem.at[r - 1], rsem.at[r - 1], tgt)
            out.append((send, recv))
        return me, out

    def start(ins, outs, sc):
        me, cps = parts(ins[0], sc)
        pltpu.make_async_copy(ins[0], sc[0].at[me], sc[4].at[0]).start()
        for send, _ in cps:
            send.start()

    def finish(ins, outs, sc):
        me, cps = parts(ins[0], sc)
        pltpu.make_async_copy(ins[0], sc[0].at[me], sc[4].at[0]).wait()
        for _, recv in cps:
            recv.wait_recv()
        acc = sc[0][0]
        for d in range(1, 8):
            acc = acc + sc[0][d]
        sc[1][...] = acc
        back = pltpu.make_async_copy(sc[1], outs[0], sc[4].at[1])
        back.start()
        for send, _ in cps:
            send.wait_send()
        back.wait()

    return _Exchange([p], [jax.ShapeDtypeStruct((SMALL_ROWS, CH), F32)], {},
                     [pltpu.VMEM((8, SMALL_ROWS, CH), F32), pltpu.VMEM((SMALL_ROWS, CH), F32), _dma_sems(7), _dma_sems(7),
                      _dma_sems(2)], start, finish)


def _rope_tables(positions, comm=None):
    T = positions.shape[0]
    inv_freq = 500000.0 ** (-jnp.arange(0, 2 * ROPE_HALF, 2, dtype=F32) / (2 * ROPE_HALF))
    head = jnp.concatenate([inv_freq, inv_freq, jnp.zeros((HD - 2 * ROPE_HALF,), F32)])
    lane_freq = jnp.concatenate([head, head])[None, :]
    pos = jnp.broadcast_to(positions.astype(F32)[:, None], (T, CH))
    tm = min(1024, T)

    def body(p_ref, f_ref, c_ref, s1_ref, s2_ref):
        ang = p_ref[...] * f_ref[...]
        sin = jnp.sin(ang)
        first = (lax.broadcasted_iota(jnp.int32, ang.shape, 1) % HD) < ROPE_HALF
        c_ref[...] = jnp.cos(ang)
        s1_ref[...] = jnp.where(first, -sin, 0.0)
        s2_ref[...] = jnp.where(first, 0.0, sin)

    return _call(body, (pos, lane_freq), name="rope_tables", grid=(T // tm,),
                 in_specs=[_rows(tm, CH), _const((1, CH))], out_specs=[_rows(tm, CH)] * 3,
                 out_shape=[jax.ShapeDtypeStruct((T, CH), F32)] * 3, sem=("parallel",), comm=comm)


BIG_NAMES = ("w_in", "w_a", "w_b", "w_o", "w_ff_in", "w_ff_out")
SMALL_NAMES = ("w_spatial", "ln_v_gain", "ln_v_bias", "b_spatial", "sinks", "norm_mix_pre", "norm_mix_post",
               "norm_ff_pre", "norm_ff_post")
WEIGHT_ORDER = ("w_in", "ln_v_gain", "ln_v_bias", "w_spatial", "b_spatial", "sinks", "w_a", "w_b", "w_o",
                "norm_mix_pre", "norm_mix_post", "w_ff_in", "w_ff_out", "norm_ff_pre", "norm_ff_post")


def _pack_small(d, loss_sums=None):
    parts = []
    for n in SMALL_NAMES:
        flat = d[n].reshape(-1)
        pad = (-flat.shape[0]) % (8 * CH)
        parts.append(jnp.pad(flat, (0, pad)).reshape(-1, CH))
    parts.append(jnp.zeros((8, CH), F32) if loss_sums is None else loss_sums.reshape(8, CH))
    return jnp.concatenate(parts, axis=0)


def _unpack_small(p, like):
    out, row = {}, 0
    for n in SMALL_NAMES:
        size = like[n].size
        rows = -(-size // (8 * CH)) * 8
        out[n] = p[row:row + rows].reshape(-1)[:size].reshape(like[n].shape)
        row += rows
    return out


def kernel(x, positions, w_in, ln_v_gain, ln_v_bias, w_spatial, b_spatial, sinks, w_a, w_b, w_o, norm_mix_pre, norm_mix_post, w_ff_in, w_ff_out, norm_ff_pre, norm_ff_post, loss_target, m_w_in, m_ln_v_gain, m_ln_v_bias, m_w_spatial, m_b_spatial, m_sinks, m_w_a, m_w_b, m_w_o, m_norm_mix_pre, m_norm_mix_post, m_w_ff_in, m_w_ff_out, m_norm_ff_pre, m_norm_ff_post, v_w_in, v_ln_v_gain, v_ln_v_bias, v_w_spatial, v_b_spatial, v_sinks, v_w_a, v_w_b, v_w_o, v_norm_mix_pre, v_norm_mix_post, v_w_ff_in, v_w_ff_out, v_norm_ff_pre, v_norm_ff_post):
    w = dict(w_in=w_in, ln_v_gain=ln_v_gain, ln_v_bias=ln_v_bias, w_spatial=w_spatial, b_spatial=b_spatial, sinks=sinks,
             w_a=w_a, w_b=w_b, w_o=w_o, norm_mix_pre=norm_mix_pre, norm_mix_post=norm_mix_post, w_ff_in=w_ff_in,
             w_ff_out=w_ff_out, norm_ff_pre=norm_ff_pre, norm_ff_post=norm_ff_post)
    m = dict(w_in=m_w_in, ln_v_gain=m_ln_v_gain, ln_v_bias=m_ln_v_bias, w_spatial=m_w_spatial, b_spatial=m_b_spatial,
             sinks=m_sinks, w_a=m_w_a, w_b=m_w_b, w_o=m_w_o, norm_mix_pre=m_norm_mix_pre, norm_mix_post=m_norm_mix_post,
             w_ff_in=m_w_ff_in, w_ff_out=m_w_ff_out, norm_ff_pre=m_norm_ff_pre, norm_ff_post=m_norm_ff_post)
    v = dict(w_in=v_w_in, ln_v_gain=v_ln_v_gain, ln_v_bias=v_ln_v_bias, w_spatial=v_w_spatial, b_spatial=v_b_spatial,
             sinks=v_sinks, w_a=v_w_a, w_b=v_w_b, w_o=v_w_o, norm_mix_pre=v_norm_mix_pre, norm_mix_post=v_norm_mix_post,
             w_ff_in=v_w_ff_in, w_ff_out=v_w_ff_out, norm_ff_pre=v_norm_ff_pre, norm_ff_post=v_norm_ff_post)

    FIRST, REST = (0,), tuple(range(1, NBIG))
    shards = [w[n][0].astype(BF16) for n in BIG_NAMES]
    place = jnp.stack([lax.axis_index("c"), 2 * lax.axis_index("x") + lax.axis_index("y")]).astype(jnp.int32)
    xs, target = x[0], loss_target[0]
    T = xs.shape[0]
    tile = min(TOKEN_TILE, T)
    wtiles = {n: dict(tm=min(tm, T), tn=tn) for n, (tm, tn) in WGRAD_TILES.items()}
    g1, g2, g3, g4 = norm_mix_pre, norm_mix_post, norm_ff_pre, norm_ff_post
    w_sp, snk = w_spatial[0], sinks[0]
    MIX, FF = (1, 2, 3), (4, 5)
    bfull = jnp.repeat(b_spatial[0].T, CH, axis=1)

    def reduce_tail(ws, grads, got):
        tag = "_".join(BIG_NAMES[k] for k in ws)
        sums = _add_halves(place, grads, got, [BIG[k][0] for k in ws], name="grad_add_sibling_" + tag)
        return sums, _x_grads_chips([s[1] for s in sums], ws)

    def reduce_end(ws, sums, pieces):
        tag = "_".join(BIG_NAMES[k] for k in ws)
        return _add_pieces(place, [s[0] for s in sums], pieces, [BIG[k] for k in ws], name="grad_add_chips_" + tag)

    (rc, rs1, rs2), w_in_part = _rope_tables(positions[0], comm=_x_gather_ici(shards[:1], FIRST))
    w_in_b = _run(_x_gather_d2d(w_in_part, FIRST), "gather_w_in_d2d")[0]
    EARLY, FF_OUT = (1, 2, 3, 4), (5,)
    (h, u, vs, q, k, va, ga, gb), early_part = _inproj(xs, g1, w_in_b, rc, rs1, rs2, tm=tile,
                                                      comm=_x_gather_ici(shards[1:5], EARLY))
    att, (w_a_b, w_b_b, w_o_b, w_ff_in_b, ffo_part) = _attn_fwd(
        q, k, va, snk, comm=_both(_x_gather_d2d(early_part, EARLY), _x_gather_ici(shards[5:], FF_OUT)))
    (a, pa, pb, mix, x1), (w_ff_out_b,) = _sgu_merge_fwd(
        u, vs, ln_v_gain, ln_v_bias, w_sp, bfull, att, ga, gb, xs, w_a_b, w_b_b, w_o_b, g2, tm=tile,
        comm=_x_gather_d2d([ffo_part], FF_OUT))
    hf, f2, dff, df1, dx1, lsum, dg3, dg4 = _ffn(x1, target, w_ff_in_b, w_ff_out_b, g3, g4, tm=tile)

    dw_ff_out, _ = _wgrad(f2, dff, name="wgrad_ff_out", **wtiles["w_ff_out"])
    dw_ff_in, _ = _wgrad(hf, df1, name="wgrad_ff_in", **wtiles["w_ff_in"])
    grads_ff = [dw_ff_in, dw_ff_out]
    (dga, dgb, da, datt, dg2, dw_a, dw_b, dw_o), got_ff = _merge_bwd(
        dx1, mix, ga, gb, pa, pb, a, att, w_a_b, w_b_b, w_o_b, g2, tm=tile, comm=_x_grads_sibling(grads_ff, FF))
    grads_mix = [dw_a, dw_b, dw_o]
    sums_ff, ff_to_chips = reduce_tail(FF, grads_ff, got_ff)
    (dq, dk, dva, dsk), (*pieces_ff, got_a, got_b, got_o) = _attn_bwd(
        q, k, va, datt, snk, rc, rs1, rs2, comm=_both(ff_to_chips, _x_grads_sibling(grads_mix, MIX)))
    partial_ff = reduce_end(FF, sums_ff, pieces_ff)
    sums_mix, mix_to_chips = reduce_tail(MIX, grads_mix, [got_a, got_b, got_o])
    dx, dproj, dg1, dws, dbs, dlg, dlb = _sgu_inproj_bwd(
        u, vs, da, ln_v_gain, ln_v_bias, w_sp, bfull, [dq, dk, dva, dga, dgb], xs, dx1, g1, w_in_b, tm=tile)
    small = dict(ln_v_gain=dlg, ln_v_bias=dlb, w_spatial=dws, b_spatial=dbs, sinks=dsk[:, :NQ],
                 norm_mix_pre=dg1, norm_mix_post=dg2, norm_ff_pre=dg3, norm_ff_post=dg4)
    dw_in, (gs, shard_ff_in, shard_ff_out, *pieces_mix) = _wgrad(
        h, dproj, name="wgrad_in", vmem=VMEM_PHYSICAL, **wtiles["w_in"],
        comm=_both(_both(_x_small_all_reduce(_pack_small(small, lsum)), _x_grads_share(partial_ff, FF)), mix_to_chips))
    partial_mix = reduce_end(MIX, sums_mix, pieces_mix)
    got_in = _run(_x_grads_sibling([dw_in], FIRST), "grads_in_to_sibling")
    sums_in, to_chips = reduce_tail(FIRST, [dw_in], got_in)
    partial_in = reduce_end(FIRST, sums_in, _run(to_chips, "grads_in_to_chips"))
    g_in, *shard_mix = _run(_x_grads_share(list(partial_in) + list(partial_mix), FIRST + MIX), "grads_in_mix_share")
    shard_rest = list(shard_mix) + [shard_ff_in, shard_ff_out]

    loss = 0.5 * jnp.sum(gs[SMALL_ROWS - 8:]) / D
    grad, delta, new_m, new_v = {}, {}, {}, {}
    for n, g in zip(BIG_NAMES, [g_in] + list(shard_rest)):
        (g_, d_, m_, v_), = _adamw([w[n][0]], [g], [m[n][0]], [v[n][0]], [256], name="adamw_" + n)
        grad[n], delta[n], new_m[n], new_v[n] = g_[None], d_[None], m_[None], v_[None]
    (gs, ds, ms, vs), = _adamw([_pack_small(w)], [gs], [_pack_small(m)], [_pack_small(v)], [SMALL_ROWS], name="adamw_small")
    for packed, dst in ((gs, grad), (ds, delta), (ms, new_m), (vs, new_v)):
        dst.update(_unpack_small(packed, w))

    outs = [loss, dx[None]]
    for group in (grad, delta, new_m, new_v):
        outs.extend(group[n] for n in WEIGHT_ORDER)
    return tuple(outs)
```

```python
import functools

import jax
import jax.numpy as jnp
from jax import lax
from jax.experimental import pallas as pl
from jax.experimental.pallas import tpu as pltpu

F32 = jnp.float32
BF16 = jnp.bfloat16

D = 1024
CH = 128
NG = 8
HD = 64
NQ = 16
NKV = 4
KVW = NKV * HD
DFF = 4 * D
EPS = 1e-6
IN_W = 5632
SEG = (0, 1024, 2048, 3072, 3328, 3584, 4608, 5632)
ROPE_HALF = 8
Q_SCALE = HD ** -0.5

LR, B1, B2, AEPS, WD, STEP = 0.001, 0.9, 0.999, 1e-08, 0.01, 10

VMEM_PHYSICAL = 64 * 1024 * 1024
VMEM_LIMIT = 60 * 1024 * 1024
MESH = pl.DeviceIdType.MESH

TOKEN_TILE = 512
WGRAD_TILES = {"w_ff_out": (512, 1024), "w_ff_in": (2048, 2048), "w_in": (2048, IN_W // 2)}

_GELU_C0 = 0.7978845608028654
_GELU_C1 = 0.044715


def _cparams(sem=None, vmem=None):
    kw = dict(vmem_limit_bytes=VMEM_LIMIT if vmem is None else vmem)
    if sem is not None:
        kw["dimension_semantics"] = sem
    return pltpu.CompilerParams(**kw)


def _resident(shape):
    nd = len(shape)
    return pl.BlockSpec(shape, lambda *_: (0,) * nd, pipeline_mode=pl.Buffered(1))


def _const(shape):
    nd = len(shape)
    return pl.BlockSpec(shape, lambda *_: (0,) * nd)


def _rows(tm, w):
    return pl.BlockSpec((tm, w), lambda i: (i, 0))


class _Exchange:
    def __init__(self, ins, outs, aliases, scratch, start, finish):
        self.ins, self.outs, self.aliases, self.scratch = list(ins), list(outs), dict(aliases), list(scratch)
        self.start, self.finish = start, finish


def _both(a, b):
    na, ma, sa = len(a.ins), len(a.outs), len(a.scratch)

    def start(ci, co, cs):
        a.start(ci[:na], co[:ma], cs[:sa])
        b.start(ci[na:], co[ma:], cs[sa:])

    def finish(ci, co, cs):
        a.finish(ci[:na], co[:ma], cs[:sa])
        b.finish(ci[na:], co[ma:], cs[sa:])

    aliases = {**a.aliases, **{na + i: ma + j for i, j in b.aliases.items()}}
    return _Exchange(a.ins + b.ins, a.outs + b.outs, aliases, a.scratch + b.scratch, start, finish)


def _call(body, args, *, name, grid, in_specs, out_specs, out_shape, scratch_shapes=(), sem=None, comm=None, vmem=None):
    single = not isinstance(out_shape, (list, tuple))
    out_shape = [out_shape] if single else list(out_shape)
    out_specs = [out_specs] if single else list(out_specs)
    if comm is None:
        res = pl.pallas_call(body, name=name, grid=grid, in_specs=list(in_specs), out_specs=out_specs,
                             out_shape=out_shape, scratch_shapes=list(scratch_shapes),
                             compiler_params=_cparams(sem, vmem))(*args)
        return (res[0] if single else res), []
    n_in, n_out, n_scr = len(args), len(out_shape), len(scratch_shapes)
    nci, nco = len(comm.ins), len(comm.outs)
    steps = 1
    for g in grid:
        steps *= g

    def hosted(*refs):
        a, ci = refs[:n_in], refs[n_in:n_in + nci]
        o, co = refs[n_in + nci:n_in + nci + n_out], refs[n_in + nci + n_out:n_in + nci + n_out + nco]
        rest = refs[n_in + nci + n_out + nco:]
        scr, cs = rest[:n_scr], rest[n_scr:]
        step = pl.program_id(0)
        for d in range(1, len(grid)):
            step = step * grid[d] + pl.program_id(d)

        @pl.when(step == 0)
        def _():
            comm.start(ci, co, cs)

        body(*a, *o, *scr)

        @pl.when(step == steps - 1)
        def _():
            comm.finish(ci, co, cs)

    res = pl.pallas_call(
        hosted, name=name, grid=grid, in_specs=list(in_specs) + [ANY] * nci, out_specs=out_specs + [ANY] * nco,
        out_shape=out_shape + comm.outs, scratch_shapes=list(scratch_shapes) + comm.scratch,
        input_output_aliases={n_in + i: n_out + j for i, j in comm.aliases.items()},
        compiler_params=_cparams(("arbitrary",) * len(grid), vmem),
    )(*args, *comm.ins)
    own = res[:n_out]
    return (own[0] if single else own), list(res[n_out:])


def _run(comm, name):
    nci = len(comm.ins)

    def body(*refs):
        ci, co, cs = refs[:nci], refs[nci:nci + len(comm.outs)], refs[nci + len(comm.outs):]
        comm.start(ci, co, cs)
        comm.finish(ci, co, cs)

    return pl.pallas_call(
        body, name=name, in_specs=[ANY] * nci, out_specs=[ANY] * len(comm.outs), out_shape=comm.outs,
        scratch_shapes=comm.scratch, input_output_aliases=comm.aliases,
        compiler_params=pltpu.CompilerParams(vmem_limit_bytes=VMEM_LIMIT),
    )(*comm.ins)


def _gelu(x):
    x2 = x * x
    t = jnp.tanh(x * (_GELU_C0 + (_GELU_C0 * _GELU_C1) * x2))
    hx = 0.5 * x
    return hx + hx * t, (t, x2, hx)


def _gelu_grad(parts):
    t, x2, hx = parts
    return (0.5 + 0.5 * t) + hx * (1.0 - t * t) * (_GELU_C0 + (3.0 * _GELU_C0 * _GELU_C1) * x2)


def _sigmoid(x):
    return 1.0 / (1.0 + jnp.exp(-x))


def _rms_hat(x):
    r = lax.rsqrt(jnp.mean(x * x, axis=-1, keepdims=True) + EPS)
    return x * r, r


def _rms_bwd(xhat, r, g, dout):
    dg = jnp.sum(dout * xhat, axis=0, keepdims=True)
    dy = dout * g
    dx = r * (dy - xhat * jnp.mean(dy * xhat, axis=-1, keepdims=True))
    return dx, dg


def _dot(a, b):
    return jnp.dot(a, b, preferred_element_type=F32)


def _dot_nt(a, b):
    return lax.dot_general(a, b, (((1,), (1,)), ((), ())), preferred_element_type=F32)


def _dot_tn(a, b):
    return lax.dot_general(a, b, (((0,), (0,)), ((), ())), preferred_element_type=F32)


def _rope(blk, c, s1, s2):
    return blk * c + pltpu.roll(blk, CH - ROPE_HALF, 1) * s1 + pltpu.roll(blk, ROPE_HALF, 1) * s2


def _rope_t(blk, c, s1, s2):
    return blk * c + pltpu.roll(blk * s1, ROPE_HALF, 1) + pltpu.roll(blk * s2, CH - ROPE_HALF, 1)


def _inproj(x, g1, w_in, rc, rs1, rs2, tm, comm=None):
    T = x.shape[0]

    def body(x_ref, g_ref, w_ref, c_ref, s1_ref, s2_ref,
             h_ref, u_ref, v_ref, q_ref, k_ref, va_ref, ga_ref, gb_ref):
        xhat, _ = _rms_hat(x_ref[...])
        h = (xhat * g_ref[...]).astype(BF16)
        h_ref[...] = h
        uv = _dot(h, w_ref[:, SEG[0]:SEG[2]])
        uv = uv.astype(BF16)
        u_ref[...] = uv[:, :D]
        v_ref[...] = uv[:, D:]
        c, s1, s2 = c_ref[...], s1_ref[...], s2_ref[...]
        qkv = _dot(h, w_ref[:, SEG[2]:SEG[5]])
        for p in range(D // CH):
            blk = _rope(qkv[:, CH * p:CH * (p + 1)], c, s1, s2) * Q_SCALE
            q_ref[:, CH * p:CH * (p + 1)] = blk.astype(BF16)
        for p in range(KVW // CH):
            k_ref[:, CH * p:CH * (p + 1)] = _rope(qkv[:, D + CH * p:D + CH * (p + 1)], c, s1, s2).astype(BF16)
        va_ref[...] = qkv[:, D + KVW:].astype(BF16)
        gates = _dot(h, w_ref[:, SEG[5]:SEG[7]]).astype(BF16)
        ga_ref[...] = gates[:, :D]
        gb_ref[...] = gates[:, D:]

    sd = jax.ShapeDtypeStruct
    return _call(
        body, (x, g1, w_in, rc, rs1, rs2), name="inproj_fwd", grid=(T // tm,),
        in_specs=[_rows(tm, D), _const((1, D)), _resident((D, IN_W)), _rows(tm, CH), _rows(tm, CH), _rows(tm, CH)],
        out_specs=[_rows(tm, D), _rows(tm, D), _rows(tm, D), _rows(tm, D), _rows(tm, KVW), _rows(tm, KVW),
                   _rows(tm, D), _rows(tm, D)],
        out_shape=[sd((T, D), BF16), sd((T, D), BF16), sd((T, D), BF16), sd((T, D), BF16), sd((T, KVW), BF16),
                   sd((T, KVW), BF16), sd((T, D), BF16), sd((T, D), BF16)],
        sem=("parallel",), comm=comm)


def _sgu_common(u, vs, lng, lnb, ws_ref, bfull):
    nc = u.shape[0] // CH
    ug, tu = _gelu(u)
    vg, tv = _gelu(vs)
    mu = jnp.mean(vg, axis=-1, keepdims=True)
    xc = vg - mu
    rstd = lax.rsqrt(jnp.mean(xc * xc, axis=-1, keepdims=True) + EPS)
    vhat = xc * rstd
    vnb = (vhat * lng + lnb).astype(BF16)
    tri = lax.broadcasted_iota(jnp.int32, (CH, CH), 0) >= lax.broadcasted_iota(jnp.int32, (CH, CH), 1)
    wts, rhss, mixed = [], [], []
    for g in range(NG):
        wt = jnp.where(tri, ws_ref[g], 0.0).astype(BF16)
        rhs = jnp.concatenate([vnb[CH * c:CH * (c + 1), CH * g:CH * (g + 1)] for c in range(nc)], axis=1)
        mix = _dot(wt, rhs)
        wts.append(wt)
        rhss.append(rhs)
        mixed.append([mix[:, CH * c:CH * (c + 1)] + bfull[:, CH * g:CH * (g + 1)] for c in range(nc)])
    return nc, ug, tu, tv, rstd, vhat, tri, wts, rhss, mixed


def _pair_layout(prev, cur, grp):
    j, half = grp // 2, grp % 2
    blk = jnp.concatenate([prev[:, CH * j:CH * (j + 1)], cur[:, CH * j:CH * (j + 1)]], axis=0).astype(F32)
    lo = lax.broadcasted_iota(jnp.int32, blk.shape, 1) < HD
    rolled = pltpu.roll(blk, HD, 1)
    even = jnp.where(lo, blk if half == 0 else rolled, 0.0)
    odd = jnp.where(lo, 0.0, rolled if half == 0 else blk)
    return jnp.concatenate([even, odd], axis=0).astype(BF16)


def _attn_mask(n):
    qi = lax.broadcasted_iota(jnp.int32, (CH, 2 * CH), 0)
    kc = lax.broadcasted_iota(jnp.int32, (CH, 2 * CH), 1)
    ok = (kc > qi) & (kc <= qi + CH) & ((kc >= CH) | (n > 0))
    return jnp.concatenate([ok, ok], axis=1)


def _softmax_sink(s, sink):
    m = jnp.maximum(jnp.max(s, axis=-1, keepdims=True), sink)
    p = jnp.exp(s - m)
    ps = jnp.exp(sink - m)
    inv = 1.0 / (jnp.sum(p, axis=-1, keepdims=True) + ps)
    return p * inv, ps * inv


QUERY_BLOCKS_PER_STEP = 2
RING_SLOTS = 3


def _attn_fwd(q, k, va, sinks, comm=None):
    T = q.shape[0]
    nblk = QUERY_BLOCKS_PER_STEP
    nsteps = T // (nblk * CH)
    npairs = D // CH

    def body(sk_ref, q_ref, kp_ref, kc_ref, vp_ref, vc_ref, o_ref):
        n = pl.program_id(0)
        even_lanes = lax.broadcasted_iota(jnp.int32, (CH, CH), 1) < HD
        ks = [kp_ref[...]] + [kc_ref[CH * b:CH * (b + 1)] for b in range(nblk)]
        vs = [vp_ref[...]] + [vc_ref[CH * b:CH * (b + 1)] for b in range(nblk)]
        masks = [_attn_mask(nblk * n)] + [_attn_mask(1)] * (nblk - 1)
        kks = [[_pair_layout(ks[b], ks[b + 1], grp) for grp in range(NKV)] for b in range(nblk)]
        vvs = [[_pair_layout(vs[b], vs[b + 1], grp) for grp in range(NKV)] for b in range(nblk)]
        work = [(b, p) for b in range(nblk) for p in range(npairs)]

        def scores(i):
            b, p = work[i]
            return _dot_nt(q_ref[CH * b:CH * (b + 1), CH * p:CH * (p + 1)], kks[b][p // 2])

        def unnormalised(s, sink):
            m = jnp.maximum(jnp.max(s, axis=-1, keepdims=True), sink)
            p = jnp.exp(s - m)
            return p, 1.0 / (jnp.sum(p, axis=-1, keepdims=True) + jnp.exp(sink - m))

        def value_product(i):
            b, p = work[i]
            pr, ie, io = probs[i]
            return _dot(pr, vvs[b][p // 2]) * jnp.where(even_lanes, ie, io)

        ahead = 3
        outs, probs = [], []
        pending = [scores(i) for i in range(ahead)]
        for i, (b, p) in enumerate(work):
            s = jnp.where(masks[b], pending.pop(0), -1e30)
            if i + ahead < len(work):
                pending.append(scores(i + ahead))
            pe, ie = unnormalised(s[:, :2 * CH], sk_ref[2 * p])
            po, io = unnormalised(s[:, 2 * CH:], sk_ref[2 * p + 1])
            probs.append((jnp.concatenate([pe, po], axis=1).astype(BF16), ie, io))
            if i >= 1:
                outs.append(value_product(i - 1))
        outs.append(value_product(len(work) - 1))
        for b in range(nblk):
            o_ref[CH * b:CH * (b + 1), :] = jnp.concatenate(outs[npairs * b:npairs * (b + 1)], axis=1).astype(BF16)

    prev = lambda n: (jnp.maximum(nblk * n - 1, 0), 0)
    cur = lambda n: (n, 0)
    return _call(
        body, (sinks, q, k, k, va, va), name="attn_fwd", grid=(nsteps,),
        in_specs=[pl.BlockSpec(memory_space=pltpu.SMEM), pl.BlockSpec((nblk * CH, D), cur),
                  pl.BlockSpec((CH, KVW), prev), pl.BlockSpec((nblk * CH, KVW), cur),
                  pl.BlockSpec((CH, KVW), prev), pl.BlockSpec((nblk * CH, KVW), cur)],
        out_specs=pl.BlockSpec((nblk * CH, D), cur), out_shape=jax.ShapeDtypeStruct((T, D), BF16),
        sem=("parallel",), comm=comm)


def _attn_bwd(q, k, va, datt, sinks, rc, rs1, rs2, comm=None):
    T = q.shape[0]
    nb = T // CH

    def body(sk_ref, q_ref, kp_ref, kc_ref, vp_ref, vc_ref, do_ref, cq_ref, s1q_ref, s2q_ref, ck_ref, s1k_ref, s2k_ref,
             dq_ref, dk_ref, dv_ref, dsk_ref, kcar, vcar):
        n = pl.program_id(0)

        @pl.when(n == 0)
        def _():
            kcar[...] = jnp.zeros_like(kcar)
            vcar[...] = jnp.zeros_like(vcar)
            dsk_ref[...] = jnp.zeros_like(dsk_ref)

        def flush(kprev, vprev):
            ck, s1k, s2k = ck_ref[...], s1k_ref[...], s2k_ref[...]
            for j in range(KVW // CH):
                sl = slice(CH * j, CH * (j + 1))
                dk_ref[:, sl] = _rope_t(kcar[:, sl] + kprev[:, sl], ck, s1k, s2k).astype(BF16)
                dv_ref[:, sl] = (vcar[:, sl] + vprev[:, sl]).astype(BF16)

        @pl.when(n < nb)
        def _():
            mask = _attn_mask(n)
            kp, kc, vp, vc = kp_ref[...], kc_ref[...], vp_ref[...], vc_ref[...]
            cq, s1q, s2q = cq_ref[...], s1q_ref[...], s2q_ref[...]
            lane = lax.broadcasted_iota(jnp.int32, (1, CH), 1)
            dsk = jnp.zeros((1, CH), F32)
            npairs = D // CH
            kks = [_pair_layout(kp, kc, grp) for grp in range(NKV)]
            vvs = [_pair_layout(vp, vc, grp) for grp in range(NKV)]
            qs = [q_ref[:, CH * p:CH * (p + 1)] for p in range(npairs)]
            dos = [do_ref[:, CH * p:CH * (p + 1)].astype(BF16) for p in range(npairs)]

            def first(p):
                return _dot_nt(qs[p], kks[p // 2]), _dot_nt(dos[p], vvs[p // 2])

            def last(p, ds, pb):
                return (_rope_t(_dot(ds, kks[p // 2]), cq, s1q, s2q) * Q_SCALE, _dot_tn(qs[p], ds), _dot_tn(dos[p], pb))

            ahead = 2
            pending = [first(p) for p in range(ahead)]
            mids, ends = [], []
            for p in range(npairs):
                s, dp = pending.pop(0)
                s = jnp.where(mask, s, -1e30)
                if p + ahead < npairs:
                    pending.append(first(p + ahead))
                ds_parts, p_parts = [], []
                for par in range(2):
                    sl = slice(2 * CH * par, 2 * CH * (par + 1))
                    pr, psink = _softmax_sink(s[:, sl], sk_ref[2 * p + par])
                    delta = jnp.sum(pr * dp[:, sl], axis=-1, keepdims=True)
                    ds_parts.append(pr * (dp[:, sl] - delta))
                    p_parts.append(pr)
                    tot = -jnp.sum(psink * delta, axis=0, keepdims=True)
                    dsk = dsk + jnp.where(lane == 2 * p + par, tot, 0.0)
                mids.append((jnp.concatenate(ds_parts, axis=1).astype(BF16), jnp.concatenate(p_parts, axis=1).astype(BF16)))
                if p >= 1:
                    ends.append(last(p - 1, *mids[p - 1]))
            ends.append(last(npairs - 1, *mids[-1]))
            dq_cols = [e[0] for e in ends]
            def fold(i):
                rows = []
                for grp in range(NKV):
                    acc = ends[2 * grp][i] + ends[2 * grp + 1][i]
                    rows.append(acc[:HD, :2 * CH] + acc[HD:, 2 * CH:])
                return jnp.concatenate(rows, axis=0).T

            dkf, dvf = fold(1), fold(2)
            dq_ref[...] = jnp.concatenate(dq_cols, axis=1).astype(BF16)
            dsk_ref[...] += dsk
            flush(dkf[:CH], dvf[:CH])
            kcar[...] = dkf[CH:]
            vcar[...] = dvf[CH:]

        @pl.when(n == nb)
        def _():
            z = jnp.zeros((CH, KVW), F32)
            flush(z, z)

    last = nb - 1
    cur = lambda n: (jnp.minimum(n, last), 0)
    prev = lambda n: (jnp.clip(n - 1, 0, last), 0)
    sd = jax.ShapeDtypeStruct
    return _call(
        body, (sinks, q, k, k, va, va, datt, rc, rs1, rs2, rc, rs1, rs2), name="attn_bwd", grid=(nb + 1,),
        in_specs=[pl.BlockSpec(memory_space=pltpu.SMEM), pl.BlockSpec((CH, D), cur),
                  pl.BlockSpec((CH, KVW), prev), pl.BlockSpec((CH, KVW), cur),
                  pl.BlockSpec((CH, KVW), prev), pl.BlockSpec((CH, KVW), cur),
                  pl.BlockSpec((CH, D), cur),
                  pl.BlockSpec((CH, CH), cur), pl.BlockSpec((CH, CH), cur), pl.BlockSpec((CH, CH), cur),
                  pl.BlockSpec((CH, CH), prev), pl.BlockSpec((CH, CH), prev), pl.BlockSpec((CH, CH), prev)],
        out_specs=[pl.BlockSpec((CH, D), cur), pl.BlockSpec((CH, KVW), prev), pl.BlockSpec((CH, KVW), prev),
                   _const((1, CH))],
        out_shape=[sd((T, D), BF16), sd((T, KVW), BF16), sd((T, KVW), BF16), sd((1, CH), F32)],
        scratch_shapes=[pltpu.VMEM((CH, KVW), F32), pltpu.VMEM((CH, KVW), F32)], sem=("arbitrary",), comm=comm)


def _sgu_merge_fwd(u, vs, lng, lnb, ws, bfull, att, ga, gb, x, w_a, w_b, w_o, g2, tm, comm=None):
    T = x.shape[0]
    nsteps = T // tm
    streamed = (u, vs, att, ga, gb, x)

    def body(u_hbm, v_hbm, lng_ref, lnb_ref, ws_ref, bf_ref, att_hbm, ga_hbm, gb_hbm, x_hbm, wa_ref, wb_ref, wo_ref, g_ref,
             a_ref, pa_ref, pb_ref, mix_ref, x1_ref, *ring):
        bufs, sem = ring[:-1], ring[-1]
        i = pl.program_id(0)

        def fetch(step):
            slot = step % RING_SLOTS
            return [pltpu.make_async_copy(h.at[pl.ds(step * tm, tm)], b.at[slot], sem.at[k, slot])
                    for k, (h, b) in enumerate(zip((u_hbm, v_hbm, att_hbm, ga_hbm, gb_hbm, x_hbm), bufs))]

        @pl.when(i == 0)
        def _():
            for s in range(min(RING_SLOTS - 1, nsteps)):
                for cp in fetch(s):
                    cp.start()

        @pl.when(i + (RING_SLOTS - 1) < nsteps)
        def _():
            for cp in fetch(i + (RING_SLOTS - 1)):
                cp.start()

        for cp in fetch(i):
            cp.wait()
        slot = i % RING_SLOTS
        u_ref, v_ref, att_ref, ga_ref, gb_ref, x_ref = (b.at[slot] for b in bufs)
        pb = _dot(att_ref[...], wb_ref[...])
        nc, ug, _, _, _, _, _, _, _, mixed = _sgu_common(
            u_ref[...].astype(F32), v_ref[...].astype(F32), lng_ref[...], lnb_ref[...], ws_ref, bf_ref[...])
        mixed_all = jnp.concatenate(
            [jnp.concatenate([mixed[g][c] for g in range(NG)], axis=1) for c in range(nc)], axis=0)
        a = (ug * mixed_all).astype(BF16)
        a_ref[...] = a
        pa = _dot(a, wa_ref[...])
        pa_ref[...] = pa.astype(BF16)
        pb_ref[...] = pb.astype(BF16)
        merged = (_sigmoid(ga_ref[...].astype(F32)) * pa + _sigmoid(gb_ref[...].astype(F32)) * pb).astype(BF16)
        mix = _dot(merged, wo_ref[...])
        mix_ref[...] = mix.astype(BF16)
        mhat, _ = _rms_hat(mix)
        x1_ref[...] = x_ref[...] + mhat * g_ref[...]

    sd = jax.ShapeDtypeStruct
    return _call(
        body, (u, vs, lng, lnb, ws, bfull, att, ga, gb, x, w_a, w_b, w_o, g2), name="sgu_merge_fwd", grid=(nsteps,),
        in_specs=[ANY, ANY, _const((1, D)), _const((1, D)), _const((NG, CH, CH)), _const((CH, D))]
        + [ANY] * 4 + [_resident((D, D))] * 3 + [_const((1, D))],
        out_specs=[_rows(tm, D)] * 5,
        out_shape=[sd((T, D), BF16)] * 4 + [sd((T, D), F32)],
        scratch_shapes=[pltpu.VMEM((RING_SLOTS, tm, D), s.dtype) for s in streamed]
        + [pltpu.SemaphoreType.DMA((len(streamed), RING_SLOTS))],
        sem=("arbitrary",), comm=comm, vmem=VMEM_PHYSICAL)


def _merge_bwd(dx1, mix, ga, gb, pa, pb, a, att, w_a, w_b, w_o, g2, tm, comm=None):
    T = dx1.shape[0]
    nsteps = T // tm

    def body(dx1_ref, mix_ref, ga_ref, gb_ref, pa_ref, pb_ref, a_ref, att_ref, wa_ref, wb_ref, wo_ref, g_ref,
             dga_ref, dgb_ref, da_ref, datt_ref, dg_ref, dwa_ref, dwb_ref, dwo_ref, acc, sem):
        i = pl.program_id(0)

        @pl.when(i == 0)
        def _():
            dg_ref[...] = jnp.zeros_like(dg_ref)
            acc[...] = jnp.zeros_like(acc)

        mhat, r = _rms_hat(mix_ref[...].astype(F32))
        dmix, dg = _rms_bwd(mhat, r, g_ref[...], dx1_ref[...])
        dg_ref[...] += dg
        dmix = dmix.astype(BF16)
        dmerged = _dot_nt(dmix, wo_ref[...])
        sa = _sigmoid(ga_ref[...].astype(F32))
        sb = _sigmoid(gb_ref[...].astype(F32))
        pa = pa_ref[...].astype(F32)
        pb = pb_ref[...].astype(F32)
        merged = (sa * pa + sb * pb).astype(BF16)
        dao = (dmerged * sa).astype(BF16)
        dbo = (dmerged * sb).astype(BF16)
        dga_ref[...] = (dmerged * pa * (sa * (1.0 - sa))).astype(BF16)
        dgb_ref[...] = (dmerged * pb * (sb * (1.0 - sb))).astype(BF16)
        da_ref[...] = _dot_nt(dao, wa_ref[...]).astype(BF16)
        datt_ref[...] = _dot_nt(dbo, wb_ref[...]).astype(BF16)
        acc[0] += _dot_tn(a_ref[...], dao)
        acc[1] += _dot_tn(att_ref[...], dbo)
        acc[2] += _dot_tn(merged, dmix)

        @pl.when(i == nsteps - 1)
        def _():
            outs = [pltpu.make_async_copy(acc.at[j], ref, sem.at[j]) for j, ref in enumerate((dwa_ref, dwb_ref, dwo_ref))]
            for cp in outs:
                cp.start()
            for cp in outs:
                cp.wait()

    sd = jax.ShapeDtypeStruct
    return _call(
        body, (dx1, mix, ga, gb, pa, pb, a, att, w_a, w_b, w_o, g2), name="merge_bwd", grid=(nsteps,),
        in_specs=[_rows(tm, D)] * 8 + [_resident((D, D))] * 3 + [_const((1, D))],
        out_specs=[_rows(tm, D)] * 4 + [_const((1, D))] + [ANY] * 3,
        out_shape=[sd((T, D), BF16)] * 4 + [sd((1, D), F32)]
        + [sd((D, D), F32)] * 3,
        scratch_shapes=[pltpu.VMEM((3, D, D), F32), _dma_sems(3)], sem=("arbitrary",), comm=comm)


def _ffn(x1, target, w1, w2, g3, g4, tm):
    T = x1.shape[0]

    def body(x_ref, t_ref, w1_ref, w2_ref, g3_ref, g4_ref,
             hf_ref, f2_ref, dff_ref, df1_ref, dx_ref, ls_ref, dg3_ref, dg4_ref):
        @pl.when(pl.program_id(0) == 0)
        def _():
            ls_ref[...] = jnp.zeros_like(ls_ref)
            dg3_ref[...] = jnp.zeros_like(dg3_ref)
            dg4_ref[...] = jnp.zeros_like(dg4_ref)

        x = x_ref[...]
        g3, g4 = g3_ref[...], g4_ref[...]
        xhat, r3 = _rms_hat(x)
        hf = (xhat * g3).astype(BF16)
        hf_ref[...] = hf
        rl = jnp.maximum(_dot(hf, w1_ref[...]), 0.0)
        f2 = (rl * rl).astype(BF16)
        f2_ref[...] = f2
        fhat, r4 = _rms_hat(_dot(f2, w2_ref[...]))
        err = x + fhat * g4 - t_ref[...]
        ls_ref[...] += jnp.sum(err * err, axis=0, keepdims=True)
        dy = err * (1.0 / D)
        dff, dg4 = _rms_bwd(fhat, r4, g4, dy)
        dg4_ref[...] += dg4
        dff = dff.astype(BF16)
        dff_ref[...] = dff
        df1 = (_dot_nt(dff, w2_ref[...]) * (2.0 * rl)).astype(BF16)
        df1_ref[...] = df1
        dxn, dg3 = _rms_bwd(xhat, r3, g3, _dot_nt(df1, w1_ref[...]))
        dg3_ref[...] += dg3
        dx_ref[...] = dy + dxn

    sd = jax.ShapeDtypeStruct
    return pl.pallas_call(
        body, name="ffn_fwd_bwd", grid=(T // tm,),
        in_specs=[_rows(tm, D), _rows(tm, D), _resident((D, DFF)), _resident((DFF, D)), _const((1, D)), _const((1, D))],
        out_specs=[_rows(tm, D), _rows(tm, DFF), _rows(tm, D), _rows(tm, DFF), _rows(tm, D), _const((1, D)),
                   _const((1, D)), _const((1, D))],
        out_shape=[sd((T, D), BF16), sd((T, DFF), BF16), sd((T, D), BF16), sd((T, DFF), BF16), sd((T, D), F32),
                   sd((1, D), F32), sd((1, D), F32), sd((1, D), F32)],
        compiler_params=pltpu.CompilerParams(vmem_limit_bytes=VMEM_PHYSICAL, dimension_semantics=("arbitrary",)),
    )(x1, target, w1, w2, g3, g4)


def _sgu_inproj_bwd(u, vs, da, lng, lnb, ws, bfull, parts, x, dx1, g1, w_in, tm):
    T = x.shape[0]
    nsteps = T // tm
    widths = [p.shape[1] for p in parts]
    offs = [2 * D + sum(widths[:i]) for i in range(len(widths) + 1)]
    assert offs[-1] == IN_W
    n = len(parts)

    def body(*refs):
        u_ref, v_ref, da_ref, lng_ref, lnb_ref, ws_ref, bf_ref = refs[:7]
        prefs = refs[7:7 + n]
        x_ref, dx1_ref, g_ref, w_ref = refs[7 + n:11 + n]
        dx_ref, dp_ref, dg_ref, dws_ref, dbs_ref, dlg_ref, dlb_ref, db_ref = refs[11 + n:]
        i = pl.program_id(0)

        @pl.when(i == 0)
        def _():
            for ref in (dg_ref, dws_ref, db_ref, dlg_ref, dlb_ref):
                ref[...] = jnp.zeros_like(ref)

        for j in range(n):
            dp_ref[:, offs[j]:offs[j + 1]] = prefs[j][...]
        cut = 2 * D + (IN_W - 2 * D) // 2
        dh_a = _dot_nt(dp_ref[:, 2 * D:cut], w_ref[:, 2 * D:cut])
        u, vs, da, lng = u_ref[...].astype(F32), v_ref[...].astype(F32), da_ref[...].astype(F32), lng_ref[...]
        nc, ug, tu, tv, rstd, vhat, tri, wts, rhss, mixed = _sgu_common(u, vs, lng, lnb_ref[...], ws_ref, bf_ref[...])
        mixed_all = jnp.concatenate(
            [jnp.concatenate([mixed[g][c] for g in range(NG)], axis=1) for c in range(nc)], axis=0)
        dp_ref[:, :D] = (da * mixed_all * _gelu_grad(tu)).astype(BF16)
        dh_u = _dot_nt(dp_ref[:, :D], w_ref[:, :D])
        dmixed = da * ug
        dvn_cols = []
        for g in range(NG):
            dmix = [dmixed[CH * c:CH * (c + 1), CH * g:CH * (g + 1)] for c in range(nc)]
            db_ref[:, CH * g:CH * (g + 1)] += functools.reduce(lambda a, b: a + b, dmix)
            dm = jnp.concatenate(dmix, axis=1).astype(BF16)
            dws_ref[g] += _dot_nt(dm, rhss[g])
            dvn_cols.append(_dot_tn(wts[g], dm))
        dh_b = _dot_nt(dp_ref[:, cut:], w_ref[:, cut:])
        dvn = jnp.concatenate(
            [jnp.concatenate([dvn_cols[g][:, CH * c:CH * (c + 1)] for g in range(NG)], axis=1) for c in range(nc)],
            axis=0)
        dlg_ref[...] += jnp.sum(dvn * vhat, axis=0, keepdims=True)
        dlb_ref[...] += jnp.sum(dvn, axis=0, keepdims=True)
        dvh = dvn * lng
        dvg = rstd * (dvh - jnp.mean(dvh, axis=-1, keepdims=True)
                      - vhat * jnp.mean(dvh * vhat, axis=-1, keepdims=True))
        dp_ref[:, D:2 * D] = (dvg * _gelu_grad(tv)).astype(BF16)

        dh = (dh_a + dh_u) + (dh_b + _dot_nt(dp_ref[:, D:2 * D], w_ref[:, D:2 * D]))
        xhat, r = _rms_hat(x_ref[...])
        dxn, dg = _rms_bwd(xhat, r, g_ref[...], dh)
        dg_ref[...] += dg
        dx_ref[...] = dx1_ref[...] + dxn

        @pl.when(i == nsteps - 1)
        def _():
            for g in range(NG):
                dws_ref[g] = jnp.where(tri, dws_ref[g], 0.0)
                dbs_ref[g:g + 1, :] = jnp.sum(db_ref[:, CH * g:CH * (g + 1)].T, axis=0, keepdims=True)

    sd = jax.ShapeDtypeStruct
    outs, _ = _call(
        body, (u, vs, da, lng, lnb, ws, bfull, *parts, x, dx1, g1, w_in), name="sgu_inproj_bwd", grid=(nsteps,),
        in_specs=[_rows(tm, D), _rows(tm, D), _rows(tm, D), _const((1, D)), _const((1, D)), _const((NG, CH, CH)),
                  _const((CH, D))] + [_rows(tm, w) for w in widths]
        + [_rows(tm, D), _rows(tm, D), _const((1, D)), _resident((D, IN_W))],
        out_specs=[_rows(tm, D), _rows(tm, IN_W), _const((1, D)), _const((NG, CH, CH)), _const((NG, CH)), _const((1, D)),
                   _const((1, D))],
        out_shape=[sd((T, D), F32), sd((T, IN_W), BF16), sd((1, D), F32), sd((NG, CH, CH), F32), sd((NG, CH), F32),
                   sd((1, D), F32), sd((1, D), F32)],
        scratch_shapes=[pltpu.VMEM((CH, D), F32)], sem=("arbitrary",), vmem=VMEM_PHYSICAL)
    return outs


def _wgrad(a, g, tn, tm, name, comm=None, vmem=None):
    T, K = a.shape
    N = g.shape[1]

    def body(a_ref, g_ref, o_ref):
        @pl.when(pl.program_id(1) == 0)
        def _():
            o_ref[...] = jnp.zeros_like(o_ref)

        o_ref[...] += _dot_tn(a_ref[...], g_ref[...])

    return _call(
        body, (a, g), name=name, grid=(N // tn, T // tm),
        in_specs=[pl.BlockSpec((tm, K), lambda j, t: (t, 0)), pl.BlockSpec((tm, tn), lambda j, t: (t, j))],
        out_specs=pl.BlockSpec((K, tn), lambda j, t: (0, j)),
        out_shape=jax.ShapeDtypeStruct((K, N), F32), sem=("parallel", "arbitrary"), comm=comm, vmem=vmem)


def _adamw(ws, gs, ms, vs, trs, name):
    n = len(ws)
    walk = _Walk(w.shape[0] // tr for w, tr in zip(ws, trs))
    bc1 = 1.0 / (1.0 - B1 ** STEP)
    bc2 = 1.0 / (1.0 - B2 ** STEP)

    def body(*refs):
        i = pl.program_id(0)
        for k in range(n):
            mine = tuple(refs[j * n + k] for j in range(8))

            @pl.when(walk.mine(k, i))
            def _(mine=mine):
                w_ref, g_ref, m_ref, v_ref, go_ref, d_ref, nm_ref, nv_ref = mine
                g = g_ref[...]
                go_ref[...] = g
                m = B1 * m_ref[...] + (1.0 - B1) * g
                v = B2 * v_ref[...] + (1.0 - B2) * (g * g)
                nm_ref[...] = m
                nv_ref[...] = v
                d_ref[...] = -LR * ((m * bc1) / (jnp.sqrt(v * bc2) + AEPS) + WD * w_ref[...])

    def spec(k):
        return pl.BlockSpec((trs[k], ws[k].shape[1]), lambda i: (walk.tile(k, i), 0))

    specs = [spec(k) for k in range(n)]
    res = pl.pallas_call(
        body, name=name, grid=(walk.steps,), in_specs=specs * 4, out_specs=specs * 4,
        out_shape=[jax.ShapeDtypeStruct(w.shape, F32) for w in ws] * 4,
        compiler_params=_cparams(("arbitrary",)),
    )(*ws, *gs, *ms, *vs)
    return [tuple(res[j * n + k] for j in range(4)) for k in range(n)]


BIG = (("col", (D, IN_W)), ("row", (D, D)), ("row", (D, D)), ("row", (D, D)), ("col", (D, DFF)), ("row", (DFF, D)))
NBIG = len(BIG)
ANY = pl.BlockSpec(memory_space=pl.ANY)


def _shard_shape(kind, shape):
    R, C = shape
    return (R, C // 4) if kind == "col" else (R // 4, C)


def _half_shape(kind, shape):
    R, C = shape
    return (R // 2, C) if kind == "col" else (R, C // 2)


def _piece_shape(kind, shape):
    R, C = shape
    return (R // 2, C // 4) if kind == "col" else (R // 4, C // 2)


def _own_region(ref, kind, shape, s):
    R, C = shape
    return ref.at[:, pl.ds(s * (C // 4), C // 4)] if kind == "col" else ref.at[pl.ds(s * (R // 4), R // 4), :]


def _ag_region(ref, kind, shape, s, hc):
    R, C = shape
    if kind == "col":
        return ref.at[pl.ds(hc * (R // 2), R // 2), pl.ds(s * (C // 4), C // 4)]
    return ref.at[pl.ds(s * (R // 4) + hc * (R // 8), R // 8), :]


def _ag_shard_half(ref, kind, shape, hc):
    R, C = shape
    return ref.at[pl.ds(hc * (R // 2), R // 2), :] if kind == "col" else ref.at[pl.ds(hc * (R // 8), R // 8), :]


def _grad_half(ref, kind, shape, hc):
    R, C = shape
    return ref.at[pl.ds(hc * (R // 2), R // 2), :] if kind == "col" else ref.at[:, pl.ds(hc * (C // 2), C // 2)]


def _half_piece(ref, kind, shape, s):
    R, C = shape
    return ref.at[:, pl.ds(s * (C // 4), C // 4)] if kind == "col" else ref.at[pl.ds(s * (R // 4), R // 4), :]


def _place():
    x, y, c = lax.axis_index("x"), lax.axis_index("y"), lax.axis_index("c")
    chips = [(1 - x, y), (x, 1 - y), (1 - x, 1 - y)]
    return x, y, c, chips


def _rcopy(src, dst, ssem, rsem, dev):
    return pltpu.make_async_remote_copy(src_ref=src, dst_ref=dst, send_sem=ssem, recv_sem=rsem,
                                        device_id=dev, device_id_type=MESH)


def _dma_sems(n):
    return pltpu.SemaphoreType.DMA((n,))


def _x_gather_ici(shards, ws):
    n = len(ws)
    specs = [BIG[w] for w in ws]

    def place():
        x, y, c, chips = _place()
        return c, chips, 2 * x + y

    def sends(sh, full, sc):
        c, chips, me_s = place()
        return [_rcopy(_ag_shard_half(sh[i], kind, shape, c), _ag_region(full[i], kind, shape, me_s, c),
                       sc[0].at[3 * i + j], sc[1].at[3 * i + j], (cx, cy, c))
                for i, (kind, shape) in enumerate(specs) for j, (cx, cy) in enumerate(chips)]

    def start(sh, full, sc):
        for i in range(n):
            pltpu.make_async_copy(sh[i], sc[4 + i], sc[2].at[i]).start()
        for cp in sends(sh, full, sc):
            cp.start()

    def finish(sh, full, sc):
        c, chips, me_s = place()
        stores = []
        for i, (kind, shape) in enumerate(specs):
            pltpu.make_async_copy(sh[i], sc[4 + i], sc[2].at[i]).wait()
            st = pltpu.make_async_copy(sc[4 + i], _own_region(full[i], kind, shape, me_s), sc[3].at[i])
            st.start()
            stores.append(st)
        for i, (kind, shape) in enumerate(specs):
            for j, (cx, cy) in enumerate(chips):
                reg = _ag_region(full[i], kind, shape, 2 * cx + cy, c)
                _rcopy(reg, reg, sc[0].at[3 * i + j], sc[1].at[3 * i + j], (cx, cy, c)).wait_recv()
        for cp in sends(sh, full, sc):
            cp.wait_send()
        for st in stores:
            st.wait()

    return _Exchange(
        shards, [jax.ShapeDtypeStruct(shape, BF16) for _, shape in specs], {},
        [_dma_sems(3 * n), _dma_sems(3 * n), _dma_sems(n), _dma_sems(n)]
        + [pltpu.VMEM(_shard_shape(k, s), BF16) for k, s in specs], start, finish)


def _x_gather_d2d(wholes, ws):
    specs = [BIG[w] for w in ws]
    n = len(ws)

    def copies(full, sc, mine):
        x, y, c, chips = _place()
        hc = c if mine else 1 - c
        return [_rcopy(reg, reg, sc[0].at[3 * i + j], sc[1].at[3 * i + j], (x, y, 1 - c))
                for i, (kind, shape) in enumerate(specs) for j, (cx, cy) in enumerate(chips)
                for reg in [_ag_region(full[i], kind, shape, 2 * cx + cy, hc)]]

    def start(_, full, sc):
        for cp in copies(full, sc, True):
            cp.start()

    def finish(_, full, sc):
        for cp in copies(full, sc, False):
            cp.wait_recv()
        for cp in copies(full, sc, True):
            cp.wait_send()

    return _Exchange(wholes, [jax.ShapeDtypeStruct(shape, BF16) for _, shape in specs], {i: i for i in range(n)},
                     [_dma_sems(3 * n), _dma_sems(3 * n)], start, finish)


def _x_grads_sibling(grads, ws):
    specs = [BIG[w] for w in ws]
    n = len(ws)

    def copies(g, got, sc):
        x, y, c, _ = _place()
        return [_rcopy(_grad_half(g[i], kind, shape, 1 - c), got[i], sc[0].at[i], sc[1].at[i], (x, y, 1 - c))
                for i, (kind, shape) in enumerate(specs)]

    def start(g, got, sc):
        for cp in copies(g, got, sc):
            cp.start()

    def finish(g, got, sc):
        for cp in copies(g, got, sc):
            cp.wait_recv()
        for cp in copies(g, got, sc):
            cp.wait_send()

    return _Exchange(grads, [jax.ShapeDtypeStruct(_half_shape(k, s), F32) for k, s in specs], {},
                     [_dma_sems(n), _dma_sems(n)], start, finish)


def _x_grads_chips(sums_bf, ws):
    specs = [BIG[w] for w in ws]
    n = len(ws)

    def copies(s16, got, sc):
        x, y, c, chips = _place()
        return [_rcopy(_half_piece(s16[i], kind, shape, 2 * cx + cy), got[i].at[j],
                       sc[0].at[3 * i + j], sc[1].at[3 * i + j], (cx, cy, c))
                for i, (kind, shape) in enumerate(specs) for j, (cx, cy) in enumerate(chips)]

    def start(s16, got, sc):
        for cp in copies(s16, got, sc):
            cp.start()

    def finish(s16, got, sc):
        for cp in copies(s16, got, sc):
            cp.wait_recv()
        for cp in copies(s16, got, sc):
            cp.wait_send()

    return _Exchange(sums_bf, [jax.ShapeDtypeStruct((3,) + _piece_shape(k, s), BF16) for k, s in specs], {},
                     [_dma_sems(3 * n), _dma_sems(3 * n)], start, finish)


def _shard_half(ref, kind, shape, hc):
    sr, sc = _shard_shape(kind, shape)
    return ref.at[pl.ds(hc * (sr // 2), sr // 2), :] if kind == "col" else ref.at[:, pl.ds(hc * (sc // 2), sc // 2)]


def _x_grads_share(shard_grads, ws):
    specs = [BIG[w] for w in ws]
    n = len(ws)

    def copies(g, sc, mine):
        x, y, c, _ = _place()
        hc = c if mine else 1 - c
        return [_rcopy(part, part, sc[0].at[i], sc[1].at[i], (x, y, 1 - c))
                for i, (kind, shape) in enumerate(specs) for part in [_shard_half(g[i], kind, shape, hc)]]

    def start(_, g, sc):
        for cp in copies(g, sc, True):
            cp.start()

    def finish(_, g, sc):
        for cp in copies(g, sc, False):
            cp.wait_recv()
        for cp in copies(g, sc, True):
            cp.wait_send()

    return _Exchange(shard_grads, [jax.ShapeDtypeStruct(_shard_shape(k, s), F32) for k, s in specs],
                     {i: i for i in range(n)}, [_dma_sems(n), _dma_sems(n)], start, finish)


ADD_BLOCK_BYTES = 4 * 1024 * 1024


def _add_rows(rows, cols, n_arrays):
    limit = ADD_BLOCK_BYTES // (1 if n_arrays == 1 else 4)
    r = rows
    while r > 64 and r * cols * 4 > limit:
        r //= 2
    return r


class _Walk:
    def __init__(self, tiles):
        self.tiles = list(tiles)
        self.starts = [sum(self.tiles[:k]) for k in range(len(self.tiles))]
        self.steps = sum(self.tiles)

    def tile(self, k, i):
        return jnp.clip(i - self.starts[k], 0, self.tiles[k] - 1)

    def mine(self, k, i):
        return (i >= self.starts[k]) & (i < self.starts[k] + self.tiles[k])


def _add_halves(place, gs, gots, kinds, name):
    n = len(gs)
    halves = [_half_shape(kind, g.shape) for g, kind in zip(gs, kinds)]
    rows = [_add_rows(hr, hc, n) for hr, hc in halves]
    walk = _Walk(hr // r for (hr, _), r in zip(halves, rows))

    def body(p_ref, *refs):
        i = pl.program_id(0)
        for k in range(n):
            g_ref, b_ref, s_ref, sb_ref = (refs[j * n + k] for j in range(4))

            @pl.when(walk.mine(k, i))
            def _(g_ref=g_ref, b_ref=b_ref, s_ref=s_ref, sb_ref=sb_ref):
                s = g_ref[...] + b_ref[...]
                s_ref[...] = s
                sb_ref[...] = s.astype(BF16)

    def g_spec(k):
        if kinds[k] == "col":
            return pl.BlockSpec((rows[k], gs[k].shape[1]), lambda i, p: (p[0] * walk.tiles[k] + walk.tile(k, i), 0))
        return pl.BlockSpec((rows[k], halves[k][1]), lambda i, p: (walk.tile(k, i), p[0]))

    def spec(k):
        return pl.BlockSpec((rows[k], halves[k][1]), lambda i, p: (walk.tile(k, i), 0))

    specs = [spec(k) for k in range(n)]
    res = pl.pallas_call(
        body, name=name,
        grid_spec=pltpu.PrefetchScalarGridSpec(num_scalar_prefetch=1, grid=(walk.steps,),
                                               in_specs=[g_spec(k) for k in range(n)] + specs, out_specs=specs + specs),
        out_shape=[jax.ShapeDtypeStruct(h, F32) for h in halves] + [jax.ShapeDtypeStruct(h, BF16) for h in halves],
        compiler_params=_cparams(("arbitrary",)),
    )(place, *gs, *gots)
    return [(res[k], res[n + k]) for k in range(n)]


def _add_pieces(place, halves, gots, specs_big, name):
    n = len(halves)
    pieces = [_piece_shape(kind, shape) for kind, shape in specs_big]
    rows = [_add_rows(pr, pc, n) for pr, pc in pieces]
    walk = _Walk(pr // r for (pr, _), r in zip(pieces, rows))

    def body(p_ref, *refs):
        i = pl.program_id(0)
        for k in range(n):
            m_ref, g_ref, o_ref = (refs[j * n + k] for j in range(3))

            @pl.when(walk.mine(k, i))
            def _(m_ref=m_ref, g_ref=g_ref, o_ref=o_ref):
                acc = m_ref[...]
                for j in range(3):
                    acc = acc + g_ref[j].astype(F32)
                o_ref[...] = acc

    def m_spec(k):
        if specs_big[k][0] == "col":
            return pl.BlockSpec((rows[k], pieces[k][1]), lambda i, p: (walk.tile(k, i), p[1]))
        return pl.BlockSpec((rows[k], pieces[k][1]), lambda i, p: (p[1] * walk.tiles[k] + walk.tile(k, i), 0))

    def got_spec(k):
        return pl.BlockSpec((3, rows[k], pieces[k][1]), lambda i, p: (0, walk.tile(k, i), 0))

    def o_spec(k):
        if specs_big[k][0] == "col":
            return pl.BlockSpec((rows[k], pieces[k][1]), lambda i, p: (p[0] * walk.tiles[k] + walk.tile(k, i), 0))
        return pl.BlockSpec((rows[k], pieces[k][1]), lambda i, p: (walk.tile(k, i), p[0]))

    return pl.pallas_call(
        body, name=name,
        grid_spec=pltpu.PrefetchScalarGridSpec(
            num_scalar_prefetch=1, grid=(walk.steps,),
            in_specs=[m_spec(k) for k in range(n)] + [got_spec(k) for k in range(n)],
            out_specs=[o_spec(k) for k in range(n)]),
        out_shape=[jax.ShapeDtypeStruct(_shard_shape(kind, shape), F32) for kind, shape in specs_big],
        compiler_params=_cparams(("arbitrary",)),
    )(place, *halves, *gots)


SMALL_ROWS = 1024 + 8 * 8 + 8


def _x_small_all_reduce(p):
    def parts(p_ref, sc):
        slots, ssem, rsem = sc[0], sc[2], sc[3]
        x, y, c = lax.axis_index("x"), lax.axis_index("y"), lax.axis_index("c")
        me = 4 * x + 2 * y + c
        out = []
        for r in range(1, 8):
            bx, by, bc = (r >> 2) & 1, (r >> 1) & 1, r & 1
            tgt = (1 - x if bx else x, 1 - y if by else y, 1 - c if bc else c)
            send = _rcopy(p_ref, slots.at[me], ssem.at[r - 1], rsem.at[r - 1], tgt)
            src = 4 * tgt[0] + 2 * tgt[1] + tgt[2]
            recv = _rcopy(p_ref, slots.at[src], ssem.at[r - 1], rsem.at[r - 1], tgt)
            out.append((send, recv))
        return me, out

    def start(ins, outs, sc):
        me, cps = parts(ins[0], sc)
        pltpu.make_async_copy(ins[0], sc[0].at[me], sc[4].at[0]).start()
        for send, _ in cps:
            send.start()

    def finish(ins, outs, sc):
        me, cps = parts(ins[0], sc)
        pltpu.make_async_copy(ins[0], sc[0].at[me], sc[4].at[0]).wait()
        for _, recv in cps:
            recv.wait_recv()
        acc = sc[0][0]
        for d in range(1, 8):
            acc = acc + sc[0][d]
        sc[1][...] = acc
        back = pltpu.make_async_copy(sc[1], outs[0], sc[4].at[1])
        back.start()
        for send, _ in cps:
            send.wait_send()
        back.wait()

    return _Exchange([p], [jax.ShapeDtypeStruct((SMALL_ROWS, CH), F32)], {},
                     [pltpu.VMEM((8, SMALL_ROWS, CH), F32), pltpu.VMEM((SMALL_ROWS, CH), F32), _dma_sems(7), _dma_sems(7),
                      _dma_sems(2)], start, finish)


def _rope_tables(positions, comm=None):
    T = positions.shape[0]
    inv_freq = 500000.0 ** (-jnp.arange(0, 2 * ROPE_HALF, 2, dtype=F32) / (2 * ROPE_HALF))
    head = jnp.concatenate([inv_freq, inv_freq, jnp.zeros((HD - 2 * ROPE_HALF,), F32)])
    lane_freq = jnp.concatenate([head, head])[None, :]
    pos = jnp.broadcast_to(positions.astype(F32)[:, None], (T, CH))
    tm = min(1024, T)

    def body(p_ref, f_ref, c_ref, s1_ref, s2_ref):
        ang = p_ref[...] * f_ref[...]
        sin = jnp.sin(ang)
        first = (lax.broadcasted_iota(jnp.int32, ang.shape, 1) % HD) < ROPE_HALF
        c_ref[...] = jnp.cos(ang)
        s1_ref[...] = jnp.where(first, -sin, 0.0)
        s2_ref[...] = jnp.where(first, 0.0, sin)

    return _call(body, (pos, lane_freq), name="rope_tables", grid=(T // tm,),
                 in_specs=[_rows(tm, CH), _const((1, CH))], out_specs=[_rows(tm, CH)] * 3,
                 out_shape=[jax.ShapeDtypeStruct((T, CH), F32)] * 3, sem=("parallel",), comm=comm)


BIG_NAMES = ("w_in", "w_a", "w_b", "w_o", "w_ff_in", "w_ff_out")
SMALL_NAMES = ("w_spatial", "ln_v_gain", "ln_v_bias", "b_spatial", "sinks", "norm_mix_pre", "norm_mix_post",
               "norm_ff_pre", "norm_ff_post")
WEIGHT_ORDER = ("w_in", "ln_v_gain", "ln_v_bias", "w_spatial", "b_spatial", "sinks", "w_a", "w_b", "w_o",
                "norm_mix_pre", "norm_mix_post", "w_ff_in", "w_ff_out", "norm_ff_pre", "norm_ff_post")


def _pack_small(d, loss_sums=None):
    parts = []
    for n in SMALL_NAMES:
        flat = d[n].reshape(-1)
        pad = (-flat.shape[0]) % (8 * CH)
        parts.append(jnp.pad(flat, (0, pad)).reshape(-1, CH))
    parts.append(jnp.zeros((8, CH), F32) if loss_sums is None else loss_sums.reshape(8, CH))
    return jnp.concatenate(parts, axis=0)


def _unpack_small(p, like):
    out, row = {}, 0
    for n in SMALL_NAMES:
        size = like[n].size
        rows = -(-size // (8 * CH)) * 8
        out[n] = p[row:row + rows].reshape(-1)[:size].reshape(like[n].shape)
        row += rows
    return out


def kernel(x, positions, w_in, ln_v_gain, ln_v_bias, w_spatial, b_spatial, sinks, w_a, w_b, w_o, norm_mix_pre, norm_mix_post, w_ff_in, w_ff_out, norm_ff_pre, norm_ff_post, loss_target, m_w_in, m_ln_v_gain, m_ln_v_bias, m_w_spatial, m_b_spatial, m_sinks, m_w_a, m_w_b, m_w_o, m_norm_mix_pre, m_norm_mix_post, m_w_ff_in, m_w_ff_out, m_norm_ff_pre, m_norm_ff_post, v_w_in, v_ln_v_gain, v_ln_v_bias, v_w_spatial, v_b_spatial, v_sinks, v_w_a, v_w_b, v_w_o, v_norm_mix_pre, v_norm_mix_post, v_w_ff_in, v_w_ff_out, v_norm_ff_pre, v_norm_ff_post):
    w = dict(w_in=w_in, ln_v_gain=ln_v_gain, ln_v_bias=ln_v_bias, w_spatial=w_spatial, b_spatial=b_spatial, sinks=sinks,
             w_a=w_a, w_b=w_b, w_o=w_o, norm_mix_pre=norm_mix_pre, norm_mix_post=norm_mix_post, w_ff_in=w_ff_in,
             w_ff_out=w_ff_out, norm_ff_pre=norm_ff_pre, norm_ff_post=norm_ff_post)
    m = dict(w_in=m_w_in, ln_v_gain=m_ln_v_gain, ln_v_bias=m_ln_v_bias, w_spatial=m_w_spatial, b_spatial=m_b_spatial,
             sinks=m_sinks, w_a=m_w_a, w_b=m_w_b, w_o=m_w_o, norm_mix_pre=m_norm_mix_pre, norm_mix_post=m_norm_mix_post,
             w_ff_in=m_w_ff_in, w_ff_out=m_w_ff_out, norm_ff_pre=m_norm_ff_pre, norm_ff_post=m_norm_ff_post)
    v = dict(w_in=v_w_in, ln_v_gain=v_ln_v_gain, ln_v_bias=v_ln_v_bias, w_spatial=v_w_spatial, b_spatial=v_b_spatial,
             sinks=v_sinks, w_a=v_w_a, w_b=v_w_b, w_o=v_w_o, norm_mix_pre=v_norm_mix_pre, norm_mix_post=v_norm_mix_post,
             w_ff_in=v_w_ff_in, w_ff_out=v_w_ff_out, norm_ff_pre=v_norm_ff_pre, norm_ff_post=v_norm_ff_post)

    FIRST, REST = (0,), tuple(range(1, NBIG))
    shards = [w[n][0].astype(BF16) for n in BIG_NAMES]
    place = jnp.stack([lax.axis_index("c"), 2 * lax.axis_index("x") + lax.axis_index("y")]).astype(jnp.int32)
    xs, target = x[0], loss_target[0]
    T = xs.shape[0]
    tile = min(TOKEN_TILE, T)
    wtiles = {n: dict(tm=min(tm, T), tn=tn) for n, (tm, tn) in WGRAD_TILES.items()}
    g1, g2, g3, g4 = norm_mix_pre, norm_mix_post, norm_ff_pre, norm_ff_post
    w_sp, snk = w_spatial[0], sinks[0]
    MIX, FF = (1, 2, 3), (4, 5)
    bfull = jnp.repeat(b_spatial[0].T, CH, axis=1)

    def reduce_tail(ws, grads, got):
        tag = "_".join(BIG_NAMES[k] for k in ws)
        sums = _add_halves(place, grads, got, [BIG[k][0] for k in ws], name="grad_add_sibling_" + tag)
        return sums, _x_grads_chips([s[1] for s in sums], ws)

    def reduce_end(ws, sums, pieces):
        tag = "_".join(BIG_NAMES[k] for k in ws)
        return _add_pieces(place, [s[0] for s in sums], pieces, [BIG[k] for k in ws], name="grad_add_chips_" + tag)

    (rc, rs1, rs2), w_in_part = _rope_tables(positions[0], comm=_x_gather_ici(shards[:1], FIRST))
    w_in_b = _run(_x_gather_d2d(w_in_part, FIRST), "gather_w_in_d2d")[0]
    EARLY, FF_OUT = (1, 2, 3, 4), (5,)
    (h, u, vs, q, k, va, ga, gb), early_part = _inproj(xs, g1, w_in_b, rc, rs1, rs2, tm=tile,
                                                      comm=_x_gather_ici(shards[1:5], EARLY))
    att, (w_a_b, w_b_b, w_o_b, w_ff_in_b, ffo_part) = _attn_fwd(
        q, k, va, snk, comm=_both(_x_gather_d2d(early_part, EARLY), _x_gather_ici(shards[5:], FF_OUT)))
    (a, pa, pb, mix, x1), (w_ff_out_b,) = _sgu_merge_fwd(
        u, vs, ln_v_gain, ln_v_bias, w_sp, bfull, att, ga, gb, xs, w_a_b, w_b_b, w_o_b, g2, tm=tile,
        comm=_x_gather_d2d([ffo_part], FF_OUT))
    hf, f2, dff, df1, dx1, lsum, dg3, dg4 = _ffn(x1, target, w_ff_in_b, w_ff_out_b, g3, g4, tm=tile)

    dw_ff_out, _ = _wgrad(f2, dff, name="wgrad_ff_out", **wtiles["w_ff_out"])
    dw_ff_in, _ = _wgrad(hf, df1, name="wgrad_ff_in", **wtiles["w_ff_in"])
    grads_ff = [dw_ff_in, dw_ff_out]
    (dga, dgb, da, datt, dg2, dw_a, dw_b, dw_o), got_ff = _merge_bwd(
        dx1, mix, ga, gb, pa, pb, a, att, w_a_b, w_b_b, w_o_b, g2, tm=tile, comm=_x_grads_sibling(grads_ff, FF))
    grads_mix = [dw_a, dw_b, dw_o]
    sums_ff, ff_to_chips = reduce_tail(FF, grads_ff, got_ff)
    (dq, dk, dva, dsk), (*pieces_ff, got_a, got_b, got_o) = _attn_bwd(
        q, k, va, datt, snk, rc, rs1, rs2, comm=_both(ff_to_chips, _x_grads_sibling(grads_mix, MIX)))
    partial_ff = reduce_end(FF, sums_ff, pieces_ff)
    sums_mix, mix_to_chips = reduce_tail(MIX, grads_mix, [got_a, got_b, got_o])
    dx, dproj, dg1, dws, dbs, dlg, dlb = _sgu_inproj_bwd(
        u, vs, da, ln_v_gain, ln_v_bias, w_sp, bfull, [dq, dk, dva, dga, dgb], xs, dx1, g1, w_in_b, tm=tile)
    small = dict(ln_v_gain=dlg, ln_v_bias=dlb, w_spatial=dws, b_spatial=dbs, sinks=dsk[:, :NQ],
                 norm_mix_pre=dg1, norm_mix_post=dg2, norm_ff_pre=dg3, norm_ff_post=dg4)
    dw_in, (gs, shard_ff_in, shard_ff_out, *pieces_mix) = _wgrad(
        h, dproj, name="wgrad_in", vmem=VMEM_PHYSICAL, **wtiles["w_in"],
        comm=_both(_both(_x_small_all_reduce(_pack_small(small, lsum)), _x_grads_share(partial_ff, FF)), mix_to_chips))
    partial_mix = reduce_end(MIX, sums_mix, pieces_mix)
    got_in = _run(_x_grads_sibling([dw_in], FIRST), "grads_in_to_sibling")
    sums_in, to_chips = reduce_tail(FIRST, [dw_in], got_in)
    partial_in = reduce_end(FIRST, sums_in, _run(to_chips, "grads_in_to_chips"))
    g_in, *shard_mix = _run(_x_grads_share(list(partial_in) + list(partial_mix), FIRST + MIX), "grads_in_mix_share")
    shard_rest = list(shard_mix) + [shard_ff_in, shard_ff_out]

    loss = 0.5 * jnp.sum(gs[SMALL_ROWS - 8:]) / D
    grad, delta, new_m, new_v = {}, {}, {}, {}
    for n, g in zip(BIG_NAMES, [g_in] + list(shard_rest)):
        (g_, d_, m_, v_), = _adamw([w[n][0]], [g], [m[n][0]], [v[n][0]], [256], name="adamw_" + n)
        grad[n], delta[n], new_m[n], new_v[n] = g_[None], d_[None], m_[None], v_[None]
    (gs, ds, ms, vs), = _adamw([_pack_small(w)], [gs], [_pack_small(m)], [_pack_small(v)], [SMALL_ROWS], name="adamw_small")
    for packed, dst in ((gs, grad), (ds, delta), (ms, new_m), (vs, new_v)):
        dst.update(_unpack_small(packed, w))

    outs = [loss, dx[None]]
    for group in (grad, delta, new_m, new_v):
        outs.extend(group[n] for n in WEIGHT_ORDER)
    return tuple(outs)
```

```python
import functools

import jax
import jax.numpy as jnp
from jax import lax
from jax.experimental import pallas as pl
from jax.experimental.pallas import tpu as pltpu

F32 = jnp.float32
BF16 = jnp.bfloat16

D = 1024
CH = 128
NG = 8
HD = 64
NQ = 16
NKV = 4
KVW = NKV * HD
DFF = 4 * D
EPS = 1e-6
IN_W = 5632
SEG = (0, 1024, 2048, 3072, 3328, 3584, 4608, 5632)
ROPE_HALF = 8
Q_SCALE = HD ** -0.5

LR, B1, B2, AEPS, WD, STEP = 0.001, 0.9, 0.999, 1e-08, 0.01, 10

VMEM_PHYSICAL = 64 * 1024 * 1024
VMEM_LIMIT = 60 * 1024 * 1024
MESH = pl.DeviceIdType.MESH

TOKEN_TILE = 512
WGRAD_TILES = {"w_ff_out": (512, 1024), "w_ff_in": (2048, 2048), "w_in": (2048, IN_W // 2)}

_GELU_C0 = 0.7978845608028654
_GELU_C1 = 0.044715


def _cparams(sem=None, vmem=None):
    kw = dict(vmem_limit_bytes=VMEM_LIMIT if vmem is None else vmem)
    if sem is not None:
        kw["dimension_semantics"] = sem
    return pltpu.CompilerParams(**kw)


def _resident(shape):
    nd = len(shape)
    return pl.BlockSpec(shape, lambda *_: (0,) * nd, pipeline_mode=pl.Buffered(1))


def _const(shape):
    nd = len(shape)
    return pl.BlockSpec(shape, lambda *_: (0,) * nd)


def _rows(tm, w):
    return pl.BlockSpec((tm, w), lambda i: (i, 0))


class _Exchange:
    def __init__(self, ins, outs, aliases, scratch, start, finish):
        self.ins, self.outs, self.aliases, self.scratch = list(ins), list(outs), dict(aliases), list(scratch)
        self.start, self.finish = start, finish


def _both(a, b):
    na, ma, sa = len(a.ins), len(a.outs), len(a.scratch)

    def start(ci, co, cs):
        a.start(ci[:na], co[:ma], cs[:sa])
        b.start(ci[na:], co[ma:], cs[sa:])

    def finish(ci, co, cs):
        a.finish(ci[:na], co[:ma], cs[:sa])
        b.finish(ci[na:], co[ma:], cs[sa:])

    aliases = {**a.aliases, **{na + i: ma + j for i, j in b.aliases.items()}}
    return _Exchange(a.ins + b.ins, a.outs + b.outs, aliases, a.scratch + b.scratch, start, finish)


def _call(body, args, *, name, grid, in_specs, out_specs, out_shape, scratch_shapes=(), sem=None, comm=None, vmem=None):
    single = not isinstance(out_shape, (list, tuple))
    out_shape = [out_shape] if single else list(out_shape)
    out_specs = [out_specs] if single else list(out_specs)
    if comm is None:
        res = pl.pallas_call(body, name=name, grid=grid, in_specs=list(in_specs), out_specs=out_specs,
                             out_shape=out_shape, scratch_shapes=list(scratch_shapes),
                             compiler_params=_cparams(sem, vmem))(*args)
        return (res[0] if single else res), []
    n_in, n_out, n_scr = len(args), len(out_shape), len(scratch_shapes)
    nci, nco = len(comm.ins), len(comm.outs)
    steps = 1
    for g in grid:
        steps *= g

    def hosted(*refs):
        a, ci = refs[:n_in], refs[n_in:n_in + nci]
        o, co = refs[n_in + nci:n_in + nci + n_out], refs[n_in + nci + n_out:n_in + nci + n_out + nco]
        rest = refs[n_in + nci + n_out + nco:]
        scr, cs = rest[:n_scr], rest[n_scr:]
        step = pl.program_id(0)
        for d in range(1, len(grid)):
            step = step * grid[d] + pl.program_id(d)

        @pl.when(step == 0)
        def _():
            comm.start(ci, co, cs)

        body(*a, *o, *scr)

        @pl.when(step == steps - 1)
        def _():
            comm.finish(ci, co, cs)

    res = pl.pallas_call(
        hosted, name=name, grid=grid, in_specs=list(in_specs) + [ANY] * nci, out_specs=out_specs + [ANY] * nco,
        out_shape=out_shape + comm.outs, scratch_shapes=list(scratch_shapes) + comm.scratch,
        input_output_aliases={n_in + i: n_out + j for i, j in comm.aliases.items()},
        compiler_params=_cparams(("arbitrary",) * len(grid), vmem),
    )(*args, *comm.ins)
    own = res[:n_out]
    return (own[0] if single else own), list(res[n_out:])


def _run(comm, name):
    nci = len(comm.ins)

    def body(*refs):
        ci, co, cs = refs[:nci], refs[nci:nci + len(comm.outs)], refs[nci + len(comm.outs):]
        comm.start(ci, co, cs)
        comm.finish(ci, co, cs)

    return pl.pallas_call(
        body, name=name, in_specs=[ANY] * nci, out_specs=[ANY] * len(comm.outs), out_shape=comm.outs,
        scratch_shapes=comm.scratch, input_output_aliases=comm.aliases,
        compiler_params=pltpu.CompilerParams(vmem_limit_bytes=VMEM_LIMIT),
    )(*comm.ins)


def _gelu(x):
    x2 = x * x
    t = jnp.tanh(x * (_GELU_C0 + (_GELU_C0 * _GELU_C1) * x2))
    hx = 0.5 * x
    return hx + hx * t, (t, x2, hx)


def _gelu_grad(parts):
    t, x2, hx = parts
    return (0.5 + 0.5 * t) + hx * (1.0 - t * t) * (_GELU_C0 + (3.0 * _GELU_C0 * _GELU_C1) * x2)


def _sigmoid(x):
    return 1.0 / (1.0 + jnp.exp(-x))


def _rms_hat(x):
    r = lax.rsqrt(jnp.mean(x * x, axis=-1, keepdims=True) + EPS)
    return x * r, r


def _rms_bwd(xhat, r, g, dout):
    dg = jnp.sum(dout * xhat, axis=0, keepdims=True)
    dy = dout * g
    dx = r * (dy - xhat * jnp.mean(dy * xhat, axis=-1, keepdims=True))
    return dx, dg


def _dot(a, b):
    return jnp.dot(a, b, preferred_element_type=F32)


def _dot_nt(a, b):
    return lax.dot_general(a, b, (((1,), (1,)), ((), ())), preferred_element_type=F32)


def _dot_tn(a, b):
    return lax.dot_general(a, b, (((0,), (0,)), ((), ())), preferred_element_type=F32)


def _rope(blk, c, s1, s2):
    return blk * c + pltpu.roll(blk, CH - ROPE_HALF, 1) * s1 + pltpu.roll(blk, ROPE_HALF, 1) * s2


def _rope_t(blk, c, s1, s2):
    return blk * c + pltpu.roll(blk * s1, ROPE_HALF, 1) + pltpu.roll(blk * s2, CH - ROPE_HALF, 1)


def _inproj(x, g1, w_in, rc, rs1, rs2, tm, comm=None):
    T = x.shape[0]

    def body(x_ref, g_ref, w_ref, c_ref, s1_ref, s2_ref,
             h_ref, u_ref, v_ref, q_ref, k_ref, va_ref, ga_ref, gb_ref):
        xhat, _ = _rms_hat(x_ref[...])
        h = (xhat * g_ref[...]).astype(BF16)
        h_ref[...] = h
        uv = _dot(h, w_ref[:, SEG[0]:SEG[2]])
        uv = uv.astype(BF16)
        u_ref[...] = uv[:, :D]
        v_ref[...] = uv[:, D:]
        c, s1, s2 = c_ref[...], s1_ref[...], s2_ref[...]
        qkv = _dot(h, w_ref[:, SEG[2]:SEG[5]])
        for p in range(D // CH):
            blk = _rope(qkv[:, CH * p:CH * (p + 1)], c, s1, s2) * Q_SCALE
            q_ref[:, CH * p:CH * (p + 1)] = blk.astype(BF16)
        for p in range(KVW // CH):
            k_ref[:, CH * p:CH * (p + 1)] = _rope(qkv[:, D + CH * p:D + CH * (p + 1)], c, s1, s2).astype(BF16)
        va_ref[...] = qkv[:, D + KVW:].astype(BF16)
        gates = _dot(h, w_ref[:, SEG[5]:SEG[7]]).astype(BF16)
        ga_ref[...] = gates[:, :D]
        gb_ref[...] = gates[:, D:]

    sd = jax.ShapeDtypeStruct
    return _call(
        body, (x, g1, w_in, rc, rs1, rs2), name="inproj_fwd", grid=(T // tm,),
        in_specs=[_rows(tm, D), _const((1, D)), _resident((D, IN_W)), _rows(tm, CH), _rows(tm, CH), _rows(tm, CH)],
        out_specs=[_rows(tm, D), _rows(tm, D), _rows(tm, D), _rows(tm, D), _rows(tm, KVW), _rows(tm, KVW),
                   _rows(tm, D), _rows(tm, D)],
        out_shape=[sd((T, D), BF16), sd((T, D), BF16), sd((T, D), BF16), sd((T, D), BF16), sd((T, KVW), BF16),
                   sd((T, KVW), BF16), sd((T, D), BF16), sd((T, D), BF16)],
        sem=("parallel",), comm=comm)


def _sgu_common(u, vs, lng, lnb, ws_ref, bfull):
    nc = u.shape[0] // CH
    ug, tu = _gelu(u)
    vg, tv = _gelu(vs)
    mu = jnp.mean(vg, axis=-1, keepdims=True)
    xc = vg - mu
    rstd = lax.rsqrt(jnp.mean(xc * xc, axis=-1, keepdims=True) + EPS)
    vhat = xc * rstd
    vnb = (vhat * lng + lnb).astype(BF16)
    tri = lax.broadcasted_iota(jnp.int32, (CH, CH), 0) >= lax.broadcasted_iota(jnp.int32, (CH, CH), 1)
    wts, rhss, mixed = [], [], []
    for g in range(NG):
        wt = jnp.where(tri, ws_ref[g], 0.0).astype(BF16)
        rhs = jnp.concatenate([vnb[CH * c:CH * (c + 1), CH * g:CH * (g + 1)] for c in range(nc)], axis=1)
        mix = _dot(wt, rhs)
        wts.append(wt)
        rhss.append(rhs)
        mixed.append([mix[:, CH * c:CH * (c + 1)] + bfull[:, CH * g:CH * (g + 1)] for c in range(nc)])
    return nc, ug, tu, tv, rstd, vhat, tri, wts, rhss, mixed


def _pair_layout(prev, cur, grp):
    j, half = grp // 2, grp % 2
    blk = jnp.concatenate([prev[:, CH * j:CH * (j + 1)], cur[:, CH * j:CH * (j + 1)]], axis=0).astype(F32)
    lo = lax.broadcasted_iota(jnp.int32, blk.shape, 1) < HD
    rolled = pltpu.roll(blk, HD, 1)
    even = jnp.where(lo, blk if half == 0 else rolled, 0.0)
    odd = jnp.where(lo, 0.0, rolled if half == 0 else blk)
    return jnp.concatenate([even, odd], axis=0).astype(BF16)


def _attn_mask(n):
    qi = lax.broadcasted_iota(jnp.int32, (CH, 2 * CH), 0)
    kc = lax.broadcasted_iota(jnp.int32, (CH, 2 * CH), 1)
    ok = (kc > qi) & (kc <= qi + CH) & ((kc >= CH) | (n > 0))
    return jnp.concatenate([ok, ok], axis=1)


def _softmax_sink(s, sink):
    m = jnp.maximum(jnp.max(s, axis=-1, keepdims=True), sink)
    p = jnp.exp(s - m)
    ps = jnp.exp(sink - m)
    inv = 1.0 / (jnp.sum(p, axis=-1, keepdims=True) + ps)
    return p * inv, ps * inv


QUERY_BLOCKS_PER_STEP = 2
RING_SLOTS = 3


def _attn_fwd(q, k, va, sinks, comm=None):
    T = q.shape[0]
    nblk = QUERY_BLOCKS_PER_STEP
    nsteps = T // (nblk * CH)
    npairs = D // CH

    def body(sk_ref, q_ref, kp_ref, kc_ref, vp_ref, vc_ref, o_ref):
        n = pl.program_id(0)
        even_lanes = lax.broadcasted_iota(jnp.int32, (CH, CH), 1) < HD
        ks = [kp_ref[...]] + [kc_ref[CH * b:CH * (b + 1)] for b in range(nblk)]
        vs = [vp_ref[...]] + [vc_ref[CH * b:CH * (b + 1)] for b in range(nblk)]
        masks = [_attn_mask(nblk * n)] + [_attn_mask(1)] * (nblk - 1)
        kks = [[_pair_layout(ks[b], ks[b + 1], grp) for grp in range(NKV)] for b in range(nblk)]
        vvs = [[_pair_layout(vs[b], vs[b + 1], grp) for grp in range(NKV)] for b in range(nblk)]
        work = [(b, p) for b in range(nblk) for p in range(npairs)]

        def scores(i):
            b, p = work[i]
            return _dot_nt(q_ref[CH * b:CH * (b + 1), CH * p:CH * (p + 1)], kks[b][p // 2])

        def unnormalised(s, sink):
            m = jnp.maximum(jnp.max(s, axis=-1, keepdims=True), sink)
            p = jnp.exp(s - m)
            return p, 1.0 / (jnp.sum(p, axis=-1, keepdims=True) + jnp.exp(sink - m))

        def value_product(i):
            b, p = work[i]
            pr, ie, io = probs[i]
            return _dot(pr, vvs[b][p // 2]) * jnp.where(even_lanes, ie, io)

        ahead = 3
        outs, probs = [], []
        pending = [scores(i) for i in range(ahead)]
        for i, (b, p) in enumerate(work):
            s = jnp.where(masks[b], pending.pop(0), -1e30)
            if i + ahead < len(work):
                pending.append(scores(i + ahead))
            pe, ie = unnormalised(s[:, :2 * CH], sk_ref[2 * p])
            po, io = unnormalised(s[:, 2 * CH:], sk_ref[2 * p + 1])
            probs.append((jnp.concatenate([pe, po], axis=1).astype(BF16), ie, io))
            if i >= 1:
                outs.append(value_product(i - 1))
        outs.append(value_product(len(work) - 1))
        for b in range(nblk):
            o_ref[CH * b:CH * (b + 1), :] = jnp.concatenate(outs[npairs * b:npairs * (b + 1)], axis=1).astype(BF16)

    prev = lambda n: (jnp.maximum(nblk * n - 1, 0), 0)
    cur = lambda n: (n, 0)
    return _call(
        body, (sinks, q, k, k, va, va), name="attn_fwd", grid=(nsteps,),
        in_specs=[pl.BlockSpec(memory_space=pltpu.SMEM), pl.BlockSpec((nblk * CH, D), cur),
                  pl.BlockSpec((CH, KVW), prev), pl.BlockSpec((nblk * CH, KVW), cur),
                  pl.BlockSpec((CH, KVW), prev), pl.BlockSpec((nblk * CH, KVW), cur)],
        out_specs=pl.BlockSpec((nblk * CH, D), cur), out_shape=jax.ShapeDtypeStruct((T, D), BF16),
        sem=("parallel",), comm=comm)


def _attn_bwd(q, k, va, datt, sinks, rc, rs1, rs2, comm=None):
    T = q.shape[0]
    nb = T // CH

    def body(sk_ref, q_ref, kp_ref, kc_ref, vp_ref, vc_ref, do_ref, cq_ref, s1q_ref, s2q_ref, ck_ref, s1k_ref, s2k_ref,
             dq_ref, dk_ref, dv_ref, dsk_ref, kcar, vcar):
        n = pl.program_id(0)

        @pl.when(n == 0)
        def _():
            kcar[...] = jnp.zeros_like(kcar)
            vcar[...] = jnp.zeros_like(vcar)
            dsk_ref[...] = jnp.zeros_like(dsk_ref)

        def flush(kprev, vprev):
            ck, s1k, s2k = ck_ref[...], s1k_ref[...], s2k_ref[...]
            for j in range(KVW // CH):
                sl = slice(CH * j, CH * (j + 1))
                dk_ref[:, sl] = _rope_t(kcar[:, sl] + kprev[:, sl], ck, s1k, s2k).astype(BF16)
                dv_ref[:, sl] = (vcar[:, sl] + vprev[:, sl]).astype(BF16)

        @pl.when(n < nb)
        def _():
            mask = _attn_mask(n)
            kp, kc, vp, vc = kp_ref[...], kc_ref[...], vp_ref[...], vc_ref[...]
            cq, s1q, s2q = cq_ref[...], s1q_ref[...], s2q_ref[...]
            lane = lax.broadcasted_iota(jnp.int32, (1, CH), 1)
            dsk = jnp.zeros((1, CH), F32)
            npairs = D // CH
            kks = [_pair_layout(kp, kc, grp) for grp in range(NKV)]
            vvs = [_pair_layout(vp, vc, grp) for grp in range(NKV)]
            qs = [q_ref[:, CH * p:CH * (p + 1)] for p in range(npairs)]
            dos = [do_ref[:, CH * p:CH * (p + 1)].astype(BF16) for p in range(npairs)]

            def first(p):
                return _dot_nt(qs[p], kks[p // 2]), _dot_nt(dos[p], vvs[p // 2])

            def last(p, ds, pb):
                return (_rope_t(_dot(ds, kks[p // 2]), cq, s1q, s2q) * Q_SCALE, _dot_tn(qs[p], ds), _dot_tn(dos[p], pb))

            ahead = 2
            pending = [first(p) for p in range(ahead)]
            mids, ends = [], []
            for p in range(npairs):
                s, dp = pending.pop(0)
                s = jnp.where(mask, s, -1e30)
                if p + ahead < npairs:
                    pending.append(first(p + ahead))
                ds_parts, p_parts = [], []
                for par in range(2):
                    sl = slice(2 * CH * par, 2 * CH * (par + 1))
                    pr, psink = _softmax_sink(s[:, sl], sk_ref[2 * p + par])
                    delta = jnp.sum(pr * dp[:, sl], axis=-1, keepdims=True)
                    ds_parts.append(pr * (dp[:, sl] - delta))
                    p_parts.append(pr)
                    tot = -jnp.sum(psink * delta, axis=0, keepdims=True)
                    dsk = dsk + jnp.where(lane == 2 * p + par, tot, 0.0)
                mids.append((jnp.concatenate(ds_parts, axis=1).astype(BF16), jnp.concatenate(p_parts, axis=1).astype(BF16)))
                if p >= 1:
                    ends.append(last(p - 1, *mids[p - 1]))
            ends.append(last(npairs - 1, *mids[-1]))
            dq_cols = [e[0] for e in ends]
            def fold(i):
                rows = []
                for grp in range(NKV):
                    acc = ends[2 * grp][i] + ends[2 * grp + 1][i]
                    rows.append(acc[:HD, :2 * CH] + acc[HD:, 2 * CH:])
                return jnp.concatenate(rows, axis=0).T

            dkf, dvf = fold(1), fold(2)
            dq_ref[...] = jnp.concatenate(dq_cols, axis=1).astype(BF16)
            dsk_ref[...] += dsk
            flush(dkf[:CH], dvf[:CH])
            kcar[...] = dkf[CH:]
            vcar[...] = dvf[CH:]

        @pl.when(n == nb)
        def _():
            z = jnp.zeros((CH, KVW), F32)
            flush(z, z)

    last = nb - 1
    cur = lambda n: (jnp.minimum(n, last), 0)
    prev = lambda n: (jnp.clip(n - 1, 0, last), 0)
    sd = jax.ShapeDtypeStruct
    return _call(
        body, (sinks, q, k, k, va, va, datt, rc, rs1, rs2, rc, rs1, rs2), name="attn_bwd", grid=(nb + 1,),
        in_specs=[pl.BlockSpec(memory_space=pltpu.SMEM), pl.BlockSpec((CH, D), cur),
                  pl.BlockSpec((CH, KVW), prev), pl.BlockSpec((CH, KVW), cur),
                  pl.BlockSpec((CH, KVW), prev), pl.BlockSpec((CH, KVW), cur),
                  pl.BlockSpec((CH, D), cur),
                  pl.BlockSpec((CH, CH), cur), pl.BlockSpec((CH, CH), cur), pl.BlockSpec((CH, CH), cur),
                  pl.BlockSpec((CH, CH), prev), pl.BlockSpec((CH, CH), prev), pl.BlockSpec((CH, CH), prev)],
        out_specs=[pl.BlockSpec((CH, D), cur), pl.BlockSpec((CH, KVW), prev), pl.BlockSpec((CH, KVW), prev),
                   _const((1, CH))],
        out_shape=[sd((T, D), BF16), sd((T, KVW), BF16), sd((T, KVW), BF16), sd((1, CH), F32)],
        scratch_shapes=[pltpu.VMEM((CH, KVW), F32), pltpu.VMEM((CH, KVW), F32)], sem=("arbitrary",), comm=comm)


def _sgu_merge_fwd(u, vs, lng, lnb, ws, bfull, att, ga, gb, x, w_a, w_b, w_o, g2, tm, comm=None):
    T = x.shape[0]
    nsteps = T // tm
    streamed = (u, vs, att, ga, gb, x)

    def body(u_hbm, v_hbm, lng_ref, lnb_ref, ws_ref, bf_ref, att_hbm, ga_hbm, gb_hbm, x_hbm, wa_ref, wb_ref, wo_ref, g_ref,
             a_ref, pab_ref, mix_ref, x1_ref, *ring):
        bufs, sem = ring[:-1], ring[-1]
        i = pl.program_id(0)

        def fetch(step):
            slot = step % RING_SLOTS
            return [pltpu.make_async_copy(h.at[pl.ds(step * tm, tm)], b.at[slot], sem.at[k, slot])
                    for k, (h, b) in enumerate(zip((u_hbm, v_hbm, att_hbm, ga_hbm, gb_hbm, x_hbm), bufs))]

        @pl.when(i == 0)
        def _():
            for s in range(min(RING_SLOTS - 1, nsteps)):
                for cp in fetch(s):
                    cp.start()

        @pl.when(i + (RING_SLOTS - 1) < nsteps)
        def _():
            for cp in fetch(i + (RING_SLOTS - 1)):
                cp.start()

        for cp in fetch(i):
            cp.wait()
        slot = i % RING_SLOTS
        u_ref, v_ref, att_ref, ga_ref, gb_ref, x_ref = (b.at[slot] for b in bufs)
        pb = _dot(att_ref[...], wb_ref[...])
        nc, ug, _, _, _, _, _, _, _, mixed = _sgu_common(
            u_ref[...].astype(F32), v_ref[...].astype(F32), lng_ref[...], lnb_ref[...], ws_ref, bf_ref[...])
        mixed_all = jnp.concatenate(
            [jnp.concatenate([mixed[g][c] for g in range(NG)], axis=1) for c in range(nc)], axis=0)
        a = (ug * mixed_all).astype(BF16)
        a_ref[...] = a
        pa = _dot(a, wa_ref[...])
        pab_ref[:, :D] = pa.astype(BF16)
        pab_ref[:, D:] = pb.astype(BF16)
        merged = (_sigmoid(ga_ref[...].astype(F32)) * pa + _sigmoid(gb_ref[...].astype(F32)) * pb).astype(BF16)
        mix = _dot(merged, wo_ref[...])
        mix_ref[...] = mix.astype(BF16)
        mhat, _ = _rms_hat(mix)
        x1_ref[...] = x_ref[...] + mhat * g_ref[...]

    sd = jax.ShapeDtypeStruct
    return _call(
        body, (u, vs, lng, lnb, ws, bfull, att, ga, gb, x, w_a, w_b, w_o, g2), name="sgu_merge_fwd", grid=(nsteps,),
        in_specs=[ANY, ANY, _const((1, D)), _const((1, D)), _const((NG, CH, CH)), _const((CH, D))]
        + [ANY] * 4 + [_resident((D, D))] * 3 + [_const((1, D))],
        out_specs=[_rows(tm, D), _rows(tm, 2 * D), _rows(tm, D), _rows(tm, D)],
        out_shape=[sd((T, D), BF16), sd((T, 2 * D), BF16), sd((T, D), BF16), sd((T, D), F32)],
        scratch_shapes=[pltpu.VMEM((RING_SLOTS, tm, D), s.dtype) for s in streamed]
        + [pltpu.SemaphoreType.DMA((len(streamed), RING_SLOTS))],
        sem=("arbitrary",), comm=comm, vmem=VMEM_PHYSICAL)


def _merge_bwd(dx1, mix, ga, gb, pab, a, att, w_a, w_b, w_o, g2, tm, comm=None):
    T = dx1.shape[0]
    nsteps = T // tm

    def body(dx1_ref, mix_ref, ga_ref, gb_ref, pab_ref, a_ref, att_ref, wa_ref, wb_ref, wo_ref, g_ref,
             dga_ref, dgb_ref, da_ref, datt_ref, dg_ref, dwa_ref, dwb_ref, dwo_ref, acc, sem):
        i = pl.program_id(0)

        @pl.when(i == 0)
        def _():
            dg_ref[...] = jnp.zeros_like(dg_ref)
            acc[...] = jnp.zeros_like(acc)

        mhat, r = _rms_hat(mix_ref[...].astype(F32))
        dmix, dg = _rms_bwd(mhat, r, g_ref[...], dx1_ref[...])
        dg_ref[...] += dg
        dmix = dmix.astype(BF16)
        dmerged = _dot_nt(dmix, wo_ref[...])
        sa = _sigmoid(ga_ref[...].astype(F32))
        sb = _sigmoid(gb_ref[...].astype(F32))
        pa = pab_ref[:, :D].astype(F32)
        pb = pab_ref[:, D:].astype(F32)
        merged = (sa * pa + sb * pb).astype(BF16)
        dao = (dmerged * sa).astype(BF16)
        dbo = (dmerged * sb).astype(BF16)
        dga_ref[...] = (dmerged * pa * (sa * (1.0 - sa))).astype(BF16)
        dgb_ref[...] = (dmerged * pb * (sb * (1.0 - sb))).astype(BF16)
        da_ref[...] = _dot_nt(dao, wa_ref[...])
        datt_ref[...] = _dot_nt(dbo, wb_ref[...]).astype(BF16)
        acc[0] += _dot_tn(a_ref[...], dao)
        acc[1] += _dot_tn(att_ref[...], dbo)
        acc[2] += _dot_tn(merged, dmix)

        @pl.when(i == nsteps - 1)
        def _():
            outs = [pltpu.make_async_copy(acc.at[j], ref, sem.at[j]) for j, ref in enumerate((dwa_ref, dwb_ref, dwo_ref))]
            for cp in outs:
                cp.start()
            for cp in outs:
                cp.wait()

    sd = jax.ShapeDtypeStruct
    return _call(
        body, (dx1, mix, ga, gb, pab, a, att, w_a, w_b, w_o, g2), name="merge_bwd", grid=(nsteps,),
        in_specs=[_rows(tm, D)] * 4 + [_rows(tm, 2 * D)] + [_rows(tm, D)] * 2 + [_resident((D, D))] * 3 + [_const((1, D))],
        out_specs=[_rows(tm, D)] * 4 + [_const((1, D))] + [ANY] * 3,
        out_shape=[sd((T, D), BF16), sd((T, D), BF16), sd((T, D), F32), sd((T, D), BF16), sd((1, D), F32)]
        + [sd((D, D), F32)] * 3,
        scratch_shapes=[pltpu.VMEM((3, D, D), F32), _dma_sems(3)], sem=("arbitrary",), comm=comm)


def _ffn(x1, target, w1, w2, g3, g4, tm):
    T = x1.shape[0]

    def body(x_ref, t_ref, w1_ref, w2_ref, g3_ref, g4_ref,
             hf_ref, f2_ref, dff_ref, df1_ref, dx_ref, ls_ref, dg3_ref, dg4_ref):
        @pl.when(pl.program_id(0) == 0)
        def _():
            ls_ref[...] = jnp.zeros_like(ls_ref)
            dg3_ref[...] = jnp.zeros_like(dg3_ref)
            dg4_ref[...] = jnp.zeros_like(dg4_ref)

        x = x_ref[...]
        g3, g4 = g3_ref[...], g4_ref[...]
        xhat, r3 = _rms_hat(x)
        hf = (xhat * g3).astype(BF16)
        hf_ref[...] = hf
        rl = jnp.maximum(_dot(hf, w1_ref[...]), 0.0)
        f2 = (rl * rl).astype(BF16)
        f2_ref[...] = f2
        fhat, r4 = _rms_hat(_dot(f2, w2_ref[...]))
        err = x + fhat * g4 - t_ref[...]
        ls_ref[...] += jnp.sum(err * err, axis=0, keepdims=True)
        dy = err * (1.0 / D)
        dff, dg4 = _rms_bwd(fhat, r4, g4, dy)
        dg4_ref[...] += dg4
        dff = dff.astype(BF16)
        dff_ref[...] = dff
        df1 = (_dot_nt(dff, w2_ref[...]) * (2.0 * rl)).astype(BF16)
        df1_ref[...] = df1
        dxn, dg3 = _rms_bwd(xhat, r3, g3, _dot_nt(df1, w1_ref[...]))
        dg3_ref[...] += dg3
        dx_ref[...] = dy + dxn

    sd = jax.ShapeDtypeStruct
    return pl.pallas_call(
        body, name="ffn_fwd_bwd", grid=(T // tm,),
        in_specs=[_rows(tm, D), _rows(tm, D), _resident((D, DFF)), _resident((DFF, D)), _const((1, D)), _const((1, D))],
        out_specs=[_rows(tm, D), _rows(tm, DFF), _rows(tm, D), _rows(tm, DFF), _rows(tm, D), _const((1, D)),
                   _const((1, D)), _const((1, D))],
        out_shape=[sd((T, D), BF16), sd((T, DFF), BF16), sd((T, D), BF16), sd((T, DFF), BF16), sd((T, D), F32),
                   sd((1, D), F32), sd((1, D), F32), sd((1, D), F32)],
        compiler_params=pltpu.CompilerParams(vmem_limit_bytes=VMEM_PHYSICAL, dimension_semantics=("arbitrary",)),
    )(x1, target, w1, w2, g3, g4)


def _sgu_inproj_bwd(u, vs, da, lng, lnb, ws, bfull, parts, x, dx1, g1, w_in, tm):
    T = x.shape[0]
    nsteps = T // tm
    widths = [p.shape[1] for p in parts]
    offs = [2 * D + sum(widths[:i]) for i in range(len(widths) + 1)]
    assert offs[-1] == IN_W
    n = len(parts)

    def body(*refs):
        u_ref, v_ref, da_ref, lng_ref, lnb_ref, ws_ref, bf_ref = refs[:7]
        prefs = refs[7:7 + n]
        x_ref, dx1_ref, g_ref, w_ref = refs[7 + n:11 + n]
        dx_ref, dp_ref, dg_ref, dws_ref, dbs_ref, dlg_ref, dlb_ref, db_ref = refs[11 + n:]
        i = pl.program_id(0)

        @pl.when(i == 0)
        def _():
            for ref in (dg_ref, dws_ref, db_ref, dlg_ref, dlb_ref):
                ref[...] = jnp.zeros_like(ref)

        for j in range(n):
            dp_ref[:, offs[j]:offs[j + 1]] = prefs[j][...]
        cut = 2 * D + (IN_W - 2 * D) // 2
        dh_a = _dot_nt(dp_ref[:, 2 * D:cut], w_ref[:, 2 * D:cut])
        u, vs, da, lng = u_ref[...].astype(F32), v_ref[...].astype(F32), da_ref[...], lng_ref[...]
        nc, ug, tu, tv, rstd, vhat, tri, wts, rhss, mixed = _sgu_common(u, vs, lng, lnb_ref[...], ws_ref, bf_ref[...])
        mixed_all = jnp.concatenate(
            [jnp.concatenate([mixed[g][c] for g in range(NG)], axis=1) for c in range(nc)], axis=0)
        dp_ref[:, :D] = (da * mixed_all * _gelu_grad(tu)).astype(BF16)
        dh_u = _dot_nt(dp_ref[:, :D], w_ref[:, :D])
        dmixed = da * ug
        dvn_cols = []
        for g in range(NG):
            dmix = [dmixed[CH * c:CH * (c + 1), CH * g:CH * (g + 1)] for c in range(nc)]
            db_ref[:, CH * g:CH * (g + 1)] += functools.reduce(lambda a, b: a + b, dmix)
            dm = jnp.concatenate(dmix, axis=1).astype(BF16)
            dws_ref[g] += _dot_nt(dm, rhss[g])
            dvn_cols.append(_dot_tn(wts[g], dm))
        dh_b = _dot_nt(dp_ref[:, cut:], w_ref[:, cut:])
        dvn = jnp.concatenate(
            [jnp.concatenate([dvn_cols[g][:, CH * c:CH * (c + 1)] for g in range(NG)], axis=1) for c in range(nc)],
            axis=0)
        dlg_ref[...] += jnp.sum(dvn * vhat, axis=0, keepdims=True)
        dlb_ref[...] += jnp.sum(dvn, axis=0, keepdims=True)
        dvh = dvn * lng
        dvg = rstd * (dvh - jnp.mean(dvh, axis=-1, keepdims=True)
                      - vhat * jnp.mean(dvh * vhat, axis=-1, keepdims=True))
        dp_ref[:, D:2 * D] = (dvg * _gelu_grad(tv)).astype(BF16)

        dh = (dh_a + dh_u) + (dh_b + _dot_nt(dp_ref[:, D:2 * D], w_ref[:, D:2 * D]))
        xhat, r = _rms_hat(x_ref[...])
        dxn, dg = _rms_bwd(xhat, r, g_ref[...], dh)
        dg_ref[...] += dg
        dx_ref[...] = dx1_ref[...] + dxn

        @pl.when(i == nsteps - 1)
        def _():
            for g in range(NG):
                dws_ref[g] = jnp.where(tri, dws_ref[g], 0.0)
                dbs_ref[g:g + 1, :] = jnp.sum(db_ref[:, CH * g:CH * (g + 1)].T, axis=0, keepdims=True)

    sd = jax.ShapeDtypeStruct
    outs, _ = _call(
        body, (u, vs, da, lng, lnb, ws, bfull, *parts, x, dx1, g1, w_in), name="sgu_inproj_bwd", grid=(nsteps,),
        in_specs=[_rows(tm, D), _rows(tm, D), _rows(tm, D), _const((1, D)), _const((1, D)), _const((NG, CH, CH)),
                  _const((CH, D))] + [_rows(tm, w) for w in widths]
        + [_rows(tm, D), _rows(tm, D), _const((1, D)), _resident((D, IN_W))],
        out_specs=[_rows(tm, D), _rows(tm, IN_W), _const((1, D)), _const((NG, CH, CH)), _const((NG, CH)), _const((1, D)),
                   _const((1, D))],
        out_shape=[sd((T, D), F32), sd((T, IN_W), BF16), sd((1, D), F32), sd((NG, CH, CH), F32), sd((NG, CH), F32),
                   sd((1, D), F32), sd((1, D), F32)],
        scratch_shapes=[pltpu.VMEM((CH, D), F32)], sem=("arbitrary",), vmem=VMEM_PHYSICAL)
    return outs


def _wgrad(a, g, tn, tm, name, comm=None, vmem=None):
    T, K = a.shape
    N = g.shape[1]

    def body(a_ref, g_ref, o_ref):
        @pl.when(pl.program_id(1) == 0)
        def _():
            o_ref[...] = jnp.zeros_like(o_ref)

        o_ref[...] += _dot_tn(a_ref[...], g_ref[...])

    return _call(
        body, (a, g), name=name, grid=(N // tn, T // tm),
        in_specs=[pl.BlockSpec((tm, K), lambda j, t: (t, 0)), pl.BlockSpec((tm, tn), lambda j, t: (t, j))],
        out_specs=pl.BlockSpec((K, tn), lambda j, t: (0, j)),
        out_shape=jax.ShapeDtypeStruct((K, N), F32), sem=("parallel", "arbitrary"), comm=comm, vmem=vmem)


def _adamw(ws, gs, ms, vs, trs, name):
    n = len(ws)
    walk = _Walk(w.shape[0] // tr for w, tr in zip(ws, trs))
    bc1 = 1.0 / (1.0 - B1 ** STEP)
    bc2 = 1.0 / (1.0 - B2 ** STEP)

    def body(*refs):
        i = pl.program_id(0)
        for k in range(n):
            mine = tuple(refs[j * n + k] for j in range(8))

            @pl.when(walk.mine(k, i))
            def _(mine=mine):
                w_ref, g_ref, m_ref, v_ref, go_ref, d_ref, nm_ref, nv_ref = mine
                g = g_ref[...]
                go_ref[...] = g
                m = B1 * m_ref[...] + (1.0 - B1) * g
                v = B2 * v_ref[...] + (1.0 - B2) * (g * g)
                nm_ref[...] = m
                nv_ref[...] = v
                d_ref[...] = -LR * ((m * bc1) / (jnp.sqrt(v * bc2) + AEPS) + WD * w_ref[...])

    def spec(k):
        return pl.BlockSpec((trs[k], ws[k].shape[1]), lambda i: (walk.tile(k, i), 0))

    specs = [spec(k) for k in range(n)]
    res = pl.pallas_call(
        body, name=name, grid=(walk.steps,), in_specs=specs * 4, out_specs=specs * 4,
        out_shape=[jax.ShapeDtypeStruct(w.shape, F32) for w in ws] * 4,
        compiler_params=_cparams(("arbitrary",)),
    )(*ws, *gs, *ms, *vs)
    return [tuple(res[j * n + k] for j in range(4)) for k in range(n)]


BIG = (("col", (D, IN_W)), ("row", (D, D)), ("row", (D, D)), ("row", (D, D)), ("col", (D, DFF)), ("row", (DFF, D)))
NBIG = len(BIG)
ANY = pl.BlockSpec(memory_space=pl.ANY)


def _shard_shape(kind, shape):
    R, C = shape
    return (R, C // 4) if kind == "col" else (R // 4, C)


def _half_shape(kind, shape):
    R, C = shape
    return (R // 2, C) if kind == "col" else (R, C // 2)


def _piece_shape(kind, shape):
    R, C = shape
    return (R // 2, C // 4) if kind == "col" else (R // 4, C // 2)


def _own_region(ref, kind, shape, s):
    R, C = shape
    return ref.at[:, pl.ds(s * (C // 4), C // 4)] if kind == "col" else ref.at[pl.ds(s * (R // 4), R // 4), :]


def _ag_region(ref, kind, shape, s, hc):
    R, C = shape
    if kind == "col":
        return ref.at[pl.ds(hc * (R // 2), R // 2), pl.ds(s * (C // 4), C // 4)]
    return ref.at[pl.ds(s * (R // 4) + hc * (R // 8), R // 8), :]


def _ag_shard_half(ref, kind, shape, hc):
    R, C = shape
    return ref.at[pl.ds(hc * (R // 2), R // 2), :] if kind == "col" else ref.at[pl.ds(hc * (R // 8), R // 8), :]


def _grad_half(ref, kind, shape, hc):
    R, C = shape
    return ref.at[pl.ds(hc * (R // 2), R // 2), :] if kind == "col" else ref.at[:, pl.ds(hc * (C // 2), C // 2)]


def _half_piece(ref, kind, shape, s):
    R, C = shape
    return ref.at[:, pl.ds(s * (C // 4), C // 4)] if kind == "col" else ref.at[pl.ds(s * (R // 4), R // 4), :]


def _place():
    x, y, c = lax.axis_index("x"), lax.axis_index("y"), lax.axis_index("c")
    chips = [(1 - x, y), (x, 1 - y), (1 - x, 1 - y)]
    return x, y, c, chips


def _rcopy(src, dst, ssem, rsem, dev):
    return pltpu.make_async_remote_copy(src_ref=src, dst_ref=dst, send_sem=ssem, recv_sem=rsem,
                                        device_id=dev, device_id_type=MESH)


def _dma_sems(n):
    return pltpu.SemaphoreType.DMA((n,))


def _x_gather_ici(shards, ws):
    n = len(ws)
    specs = [BIG[w] for w in ws]

    def place():
        x, y, c, chips = _place()
        return c, chips, 2 * x + y

    def sends(sh, full, sc):
        c, chips, me_s = place()
        return [_rcopy(_ag_shard_half(sh[i], kind, shape, c), _ag_region(full[i], kind, shape, me_s, c),
                       sc[0].at[3 * i + j], sc[1].at[3 * i + j], (cx, cy, c))
                for i, (kind, shape) in enumerate(specs) for j, (cx, cy) in enumerate(chips)]

    def start(sh, full, sc):
        for i in range(n):
            pltpu.make_async_copy(sh[i], sc[4 + i], sc[2].at[i]).start()
        for cp in sends(sh, full, sc):
            cp.start()

    def finish(sh, full, sc):
        c, chips, me_s = place()
        stores = []
        for i, (kind, shape) in enumerate(specs):
            pltpu.make_async_copy(sh[i], sc[4 + i], sc[2].at[i]).wait()
            st = pltpu.make_async_copy(sc[4 + i], _own_region(full[i], kind, shape, me_s), sc[3].at[i])
            st.start()
            stores.append(st)
        for i, (kind, shape) in enumerate(specs):
            for j, (cx, cy) in enumerate(chips):
                reg = _ag_region(full[i], kind, shape, 2 * cx + cy, c)
                _rcopy(reg, reg, sc[0].at[3 * i + j], sc[1].at[3 * i + j], (cx, cy, c)).wait_recv()
        for cp in sends(sh, full, sc):
            cp.wait_send()
        for st in stores:
            st.wait()

    return _Exchange(
        shards, [jax.ShapeDtypeStruct(shape, BF16) for _, shape in specs], {},
        [_dma_sems(3 * n), _dma_sems(3 * n), _dma_sems(n), _dma_sems(n)]
        + [pltpu.VMEM(_shard_shape(k, s), BF16) for k, s in specs], start, finish)


def _x_gather_d2d(wholes, ws):
    specs = [BIG[w] for w in ws]
    n = len(ws)

    def copies(full, sc, mine):
        x, y, c, chips = _place()
        hc = c if mine else 1 - c
        return [_rcopy(reg, reg, sc[0].at[3 * i + j], sc[1].at[3 * i + j], (x, y, 1 - c))
                for i, (kind, shape) in enumerate(specs) for j, (cx, cy) in enumerate(chips)
                for reg in [_ag_region(full[i], kind, shape, 2 * cx + cy, hc)]]

    def start(_, full, sc):
        for cp in copies(full, sc, True):
            cp.start()

    def finish(_, full, sc):
        for cp in copies(full, sc, False):
            cp.wait_recv()
        for cp in copies(full, sc, True):
            cp.wait_send()

    return _Exchange(wholes, [jax.ShapeDtypeStruct(shape, BF16) for _, shape in specs], {i: i for i in range(n)},
                     [_dma_sems(3 * n), _dma_sems(3 * n)], start, finish)


def _x_grads_sibling(grads, ws):
    specs = [BIG[w] for w in ws]
    n = len(ws)

    def copies(g, got, sc):
        x, y, c, _ = _place()
        return [_rcopy(_grad_half(g[i], kind, shape, 1 - c), got[i], sc[0].at[i], sc[1].at[i], (x, y, 1 - c))
                for i, (kind, shape) in enumerate(specs)]

    def start(g, got, sc):
        for cp in copies(g, got, sc):
            cp.start()

    def finish(g, got, sc):
        for cp in copies(g, got, sc):
            cp.wait_recv()
        for cp in copies(g, got, sc):
            cp.wait_send()

    return _Exchange(grads, [jax.ShapeDtypeStruct(_half_shape(k, s), F32) for k, s in specs], {},
                     [_dma_sems(n), _dma_sems(n)], start, finish)


def _x_grads_chips(sums_bf, ws):
    specs = [BIG[w] for w in ws]
    n = len(ws)

    def copies(s16, got, sc):
        x, y, c, chips = _place()
        return [_rcopy(_half_piece(s16[i], kind, shape, 2 * cx + cy), got[i].at[j],
                       sc[0].at[3 * i + j], sc[1].at[3 * i + j], (cx, cy, c))
                for i, (kind, shape) in enumerate(specs) for j, (cx, cy) in enumerate(chips)]

    def start(s16, got, sc):
        for cp in copies(s16, got, sc):
            cp.start()

    def finish(s16, got, sc):
        for cp in copies(s16, got, sc):
            cp.wait_recv()
        for cp in copies(s16, got, sc):
            cp.wait_send()

    return _Exchange(sums_bf, [jax.ShapeDtypeStruct((3,) + _piece_shape(k, s), BF16) for k, s in specs], {},
                     [_dma_sems(3 * n), _dma_sems(3 * n)], start, finish)


def _shard_half(ref, kind, shape, hc):
    sr, sc = _shard_shape(kind, shape)
    return ref.at[pl.ds(hc * (sr // 2), sr // 2), :] if kind == "col" else ref.at[:, pl.ds(hc * (sc // 2), sc // 2)]


def _x_grads_share(shard_grads, ws):
    specs = [BIG[w] for w in ws]
    n = len(ws)

    def copies(g, sc, mine):
        x, y, c, _ = _place()
        hc = c if mine else 1 - c
        return [_rcopy(part, part, sc[0].at[i], sc[1].at[i], (x, y, 1 - c))
                for i, (kind, shape) in enumerate(specs) for part in [_shard_half(g[i], kind, shape, hc)]]

    def start(_, g, sc):
        for cp in copies(g, sc, True):
            cp.start()

    def finish(_, g, sc):
        for cp in copies(g, sc, False):
            cp.wait_recv()
        for cp in copies(g, sc, True):
            cp.wait_send()

    return _Exchange(shard_grads, [jax.ShapeDtypeStruct(_shard_shape(k, s), F32) for k, s in specs],
                     {i: i for i in range(n)}, [_dma_sems(n), _dma_sems(n)], start, finish)


ADD_BLOCK_BYTES = 4 * 1024 * 1024


def _add_rows(rows, cols, n_arrays):
    limit = ADD_BLOCK_BYTES // (1 if n_arrays == 1 else 4)
    r = rows
    while r > 64 and r * cols * 4 > limit:
        r //= 2
    return r


class _Walk:
    def __init__(self, tiles):
        self.tiles = list(tiles)
        self.starts = [sum(self.tiles[:k]) for k in range(len(self.tiles))]
        self.steps = sum(self.tiles)

    def tile(self, k, i):
        return jnp.clip(i - self.starts[k], 0, self.tiles[k] - 1)

    def mine(self, k, i):
        return (i >= self.starts[k]) & (i < self.starts[k] + self.tiles[k])


def _add_halves(place, gs, gots, kinds, name):
    n = len(gs)
    halves = [_half_shape(kind, g.shape) for g, kind in zip(gs, kinds)]
    rows = [_add_rows(hr, hc, n) for hr, hc in halves]
    walk = _Walk(hr // r for (hr, _), r in zip(halves, rows))

    def body(p_ref, *refs):
        i = pl.program_id(0)
        for k in range(n):
            g_ref, b_ref, s_ref, sb_ref = (refs[j * n + k] for j in range(4))

            @pl.when(walk.mine(k, i))
            def _(g_ref=g_ref, b_ref=b_ref, s_ref=s_ref, sb_ref=sb_ref):
                s = g_ref[...] + b_ref[...]
                s_ref[...] = s
                sb_ref[...] = s.astype(BF16)

    def g_spec(k):
        if kinds[k] == "col":
            return pl.BlockSpec((rows[k], gs[k].shape[1]), lambda i, p: (p[0] * walk.tiles[k] + walk.tile(k, i), 0))
        return pl.BlockSpec((rows[k], halves[k][1]), lambda i, p: (walk.tile(k, i), p[0]))

    def spec(k):
        return pl.BlockSpec((rows[k], halves[k][1]), lambda i, p: (walk.tile(k, i), 0))

    specs = [spec(k) for k in range(n)]
    res = pl.pallas_call(
        body, name=name,
        grid_spec=pltpu.PrefetchScalarGridSpec(num_scalar_prefetch=1, grid=(walk.steps,),
                                               in_specs=[g_spec(k) for k in range(n)] + specs, out_specs=specs + specs),
        out_shape=[jax.ShapeDtypeStruct(h, F32) for h in halves] + [jax.ShapeDtypeStruct(h, BF16) for h in halves],
        compiler_params=_cparams(("arbitrary",)),
    )(place, *gs, *gots)
    return [(res[k], res[n + k]) for k in range(n)]


def _add_pieces(place, halves, gots, specs_big, name):
    n = len(halves)
    pieces = [_piece_shape(kind, shape) for kind, shape in specs_big]
    rows = [_add_rows(pr, pc, n) for pr, pc in pieces]
    walk = _Walk(pr // r for (pr, _), r in zip(pieces, rows))

    def body(p_ref, *refs):
        i = pl.program_id(0)
        for k in range(n):
            m_ref, g_ref, o_ref = (refs[j * n + k] for j in range(3))

            @pl.when(walk.mine(k, i))
            def _(m_ref=m_ref, g_ref=g_ref, o_ref=o_ref):
                acc = m_ref[...]
                for j in range(3):
                    acc = acc + g_ref[j].astype(F32)
                o_ref[...] = acc

    def m_spec(k):
        if specs_big[k][0] == "col":
            return pl.BlockSpec((rows[k], pieces[k][1]), lambda i, p: (walk.tile(k, i), p[1]))
        return pl.BlockSpec((rows[k], pieces[k][1]), lambda i, p: (p[1] * walk.tiles[k] + walk.tile(k, i), 0))

    def got_spec(k):
        return pl.BlockSpec((3, rows[k], pieces[k][1]), lambda i, p: (0, walk.tile(k, i), 0))

    def o_spec(k):
        if specs_big[k][0] == "col":
            return pl.BlockSpec((rows[k], pieces[k][1]), lambda i, p: (p[0] * walk.tiles[k] + walk.tile(k, i), 0))
        return pl.BlockSpec((rows[k], pieces[k][1]), lambda i, p: (walk.tile(k, i), p[0]))

    return pl.pallas_call(
        body, name=name,
        grid_spec=pltpu.PrefetchScalarGridSpec(
            num_scalar_prefetch=1, grid=(walk.steps,),
            in_specs=[m_spec(k) for k in range(n)] + [got_spec(k) for k in range(n)],
            out_specs=[o_spec(k) for k in range(n)]),
        out_shape=[jax.ShapeDtypeStruct(_shard_shape(kind, shape), F32) for kind, shape in specs_big],
        compiler_params=_cparams(("arbitrary",)),
    )(place, *halves, *gots)


SMALL_ROWS = 1024 + 8 * 8 + 8


def _x_small_all_reduce(p):
    def parts(p_ref, sc):
        slots, ssem, rsem = sc[0], sc[2], sc[3]
        x, y, c = lax.axis_index("x"), lax.axis_index("y"), lax.axis_index("c")
        me = 4 * x + 2 * y + c
        out = []
        for r in range(1, 8):
            bx, by, bc = (r >> 2) & 1, (r >> 1) & 1, r & 1
            tgt = (1 - x if bx else x, 1 - y if by else y, 1 - c if bc else c)
            send = _rcopy(p_ref, slots.at[me], ssem.at[r - 1], rsem.at[r - 1], tgt)
            src = 4 * tgt[0] + 2 * tgt[1] + tgt[2]
            recv = _rcopy(p_ref, slots.at[src], ssem.at[r - 1], rsem.at[r - 1], tgt)
            out.append((send, recv))
        return me, out

    def start(ins, outs, sc):
        me, cps = parts(ins[0], sc)
        pltpu.make_async_copy(ins[0], sc[0].at[me], sc[4].at[0]).start()
        for send, _ in cps:
            send.start()

    def finish(ins, outs, sc):
        me, cps = parts(ins[0], sc)
        pltpu.make_async_copy(ins[0], sc[0].at[me], sc[4].at[0]).wait()
        for _, recv in cps:
            recv.wait_recv()
        acc = sc[0][0]
        for d in range(1, 8):
            acc = acc + sc[0][d]
        sc[1][...] = acc
        back = pltpu.make_async_copy(sc[1], outs[0], sc[4].at[1])
        back.start()
        for send, _ in cps:
            send.wait_send()
        back.wait()

    return _Exchange([p], [jax.ShapeDtypeStruct((SMALL_ROWS, CH), F32)], {},
                     [pltpu.VMEM((8, SMALL_ROWS, CH), F32), pltpu.VMEM((SMALL_ROWS, CH), F32), _dma_sems(7), _dma_sems(7),
                      _dma_sems(2)], start, finish)


def _rope_tables(positions, comm=None):
    T = positions.shape[0]
    inv_freq = 500000.0 ** (-jnp.arange(0, 2 * ROPE_HALF, 2, dtype=F32) / (2 * ROPE_HALF))
    head = jnp.concatenate([inv_freq, inv_freq, jnp.zeros((HD - 2 * ROPE_HALF,), F32)])
    lane_freq = jnp.concatenate([head, head])[None, :]
    pos = jnp.broadcast_to(positions.astype(F32)[:, None], (T, CH))
    tm = min(1024, T)

    def body(p_ref, f_ref, c_ref, s1_ref, s2_ref):
        ang = p_ref[...] * f_ref[...]
        sin = jnp.sin(ang)
        first = (lax.broadcasted_iota(jnp.int32, ang.shape, 1) % HD) < ROPE_HALF
        c_ref[...] = jnp.cos(ang)
        s1_ref[...] = jnp.where(first, -sin, 0.0)
        s2_ref[...] = jnp.where(first, 0.0, sin)

    return _call(body, (pos, lane_freq), name="rope_tables", grid=(T // tm,),
                 in_specs=[_rows(tm, CH), _const((1, CH))], out_specs=[_rows(tm, CH)] * 3,
                 out_shape=[jax.ShapeDtypeStruct((T, CH), F32)] * 3, sem=("parallel",), comm=comm)


BIG_NAMES = ("w_in", "w_a", "w_b", "w_o", "w_ff_in", "w_ff_out")
SMALL_NAMES = ("w_spatial", "ln_v_gain", "ln_v_bias", "b_spatial", "sinks", "norm_mix_pre", "norm_mix_post",
               "norm_ff_pre", "norm_ff_post")
WEIGHT_ORDER = ("w_in", "ln_v_gain", "ln_v_bias", "w_spatial", "b_spatial", "sinks", "w_a", "w_b", "w_o",
                "norm_mix_pre", "norm_mix_post", "w_ff_in", "w_ff_out", "norm_ff_pre", "norm_ff_post")


def _pack_small(d, loss_sums=None):
    parts = []
    for n in SMALL_NAMES:
        flat = d[n].reshape(-1)
        pad = (-flat.shape[0]) % (8 * CH)
        parts.append(jnp.pad(flat, (0, pad)).reshape(-1, CH))
    parts.append(jnp.zeros((8, CH), F32) if loss_sums is None else loss_sums.reshape(8, CH))
    return jnp.concatenate(parts, axis=0)


def _unpack_small(p, like):
    out, row = {}, 0
    for n in SMALL_NAMES:
        size = like[n].size
        rows = -(-size // (8 * CH)) * 8
        out[n] = p[row:row + rows].reshape(-1)[:size].reshape(like[n].shape)
        row += rows
    return out


def kernel(x, positions, w_in, ln_v_gain, ln_v_bias, w_spatial, b_spatial, sinks, w_a, w_b, w_o, norm_mix_pre, norm_mix_post, w_ff_in, w_ff_out, norm_ff_pre, norm_ff_post, loss_target, m_w_in, m_ln_v_gain, m_ln_v_bias, m_w_spatial, m_b_spatial, m_sinks, m_w_a, m_w_b, m_w_o, m_norm_mix_pre, m_norm_mix_post, m_w_ff_in, m_w_ff_out, m_norm_ff_pre, m_norm_ff_post, v_w_in, v_ln_v_gain, v_ln_v_bias, v_w_spatial, v_b_spatial, v_sinks, v_w_a, v_w_b, v_w_o, v_norm_mix_pre, v_norm_mix_post, v_w_ff_in, v_w_ff_out, v_norm_ff_pre, v_norm_ff_post):
    w = dict(w_in=w_in, ln_v_gain=ln_v_gain, ln_v_bias=ln_v_bias, w_spatial=w_spatial, b_spatial=b_spatial, sinks=sinks,
             w_a=w_a, w_b=w_b, w_o=w_o, norm_mix_pre=norm_mix_pre, norm_mix_post=norm_mix_post, w_ff_in=w_ff_in,
             w_ff_out=w_ff_out, norm_ff_pre=norm_ff_pre, norm_ff_post=norm_ff_post)
    m = dict(w_in=m_w_in, ln_v_gain=m_ln_v_gain, ln_v_bias=m_ln_v_bias, w_spatial=m_w_spatial, b_spatial=m_b_spatial,
             sinks=m_sinks, w_a=m_w_a, w_b=m_w_b, w_o=m_w_o, norm_mix_pre=m_norm_mix_pre, norm_mix_post=m_norm_mix_post,
             w_ff_in=m_w_ff_in, w_ff_out=m_w_ff_out, norm_ff_pre=m_norm_ff_pre, norm_ff_post=m_norm_ff_post)
    v = dict(w_in=v_w_in, ln_v_gain=v_ln_v_gain, ln_v_bias=v_ln_v_bias, w_spatial=v_w_spatial, b_spatial=v_b_spatial,
             sinks=v_sinks, w_a=v_w_a, w_b=v_w_b, w_o=v_w_o, norm_mix_pre=v_norm_mix_pre, norm_mix_post=v_norm_mix_post,
             w_ff_in=v_w_ff_in, w_ff_out=v_w_ff_out, norm_ff_pre=v_norm_ff_pre, norm_ff_post=v_norm_ff_post)

    FIRST, REST = (0,), tuple(range(1, NBIG))
    shards = [w[n][0].astype(BF16) for n in BIG_NAMES]
    place = jnp.stack([lax.axis_index("c"), 2 * lax.axis_index("x") + lax.axis_index("y")]).astype(jnp.int32)
    xs, target = x[0], loss_target[0]
    T = xs.shape[0]
    tile = min(TOKEN_TILE, T)
    wtiles = {n: dict(tm=min(tm, T), tn=tn) for n, (tm, tn) in WGRAD_TILES.items()}
    g1, g2, g3, g4 = norm_mix_pre, norm_mix_post, norm_ff_pre, norm_ff_post
    w_sp, snk = w_spatial[0], sinks[0]
    MIX, FF = (1, 2, 3), (4, 5)
    bfull = jnp.repeat(b_spatial[0].T, CH, axis=1)

    def reduce_tail(ws, grads, got):
        tag = "_".join(BIG_NAMES[k] for k in ws)
        sums = _add_halves(place, grads, got, [BIG[k][0] for k in ws], name="grad_add_sibling_" + tag)
        return sums, _x_grads_chips([s[1] for s in sums], ws)

    def reduce_end(ws, sums, pieces):
        tag = "_".join(BIG_NAMES[k] for k in ws)
        return _add_pieces(place, [s[0] for s in sums], pieces, [BIG[k] for k in ws], name="grad_add_chips_" + tag)

    (rc, rs1, rs2), w_in_part = _rope_tables(positions[0], comm=_x_gather_ici(shards[:1], FIRST))
    w_in_b = _run(_x_gather_d2d(w_in_part, FIRST), "gather_w_in_d2d")[0]
    EARLY, FF_OUT = (1, 2, 3, 4), (5,)
    (h, u, vs, q, k, va, ga, gb), early_part = _inproj(xs, g1, w_in_b, rc, rs1, rs2, tm=tile,
                                                      comm=_x_gather_ici(shards[1:5], EARLY))
    att, (w_a_b, w_b_b, w_o_b, w_ff_in_b, ffo_part) = _attn_fwd(
        q, k, va, snk, comm=_both(_x_gather_d2d(early_part, EARLY), _x_gather_ici(shards[5:], FF_OUT)))
    (a, pab, mix, x1), (w_ff_out_b,) = _sgu_merge_fwd(
        u, vs, ln_v_gain, ln_v_bias, w_sp, bfull, att, ga, gb, xs, w_a_b, w_b_b, w_o_b, g2, tm=tile,
        comm=_x_gather_d2d([ffo_part], FF_OUT))
    hf, f2, dff, df1, dx1, lsum, dg3, dg4 = _ffn(x1, target, w_ff_in_b, w_ff_out_b, g3, g4, tm=tile)

    dw_ff_out, _ = _wgrad(f2, dff, name="wgrad_ff_out", **wtiles["w_ff_out"])
    dw_ff_in, _ = _wgrad(hf, df1, name="wgrad_ff_in", **wtiles["w_ff_in"])
    grads_ff = [dw_ff_in, dw_ff_out]
    (dga, dgb, da, datt, dg2, dw_a, dw_b, dw_o), got_ff = _merge_bwd(
        dx1, mix, ga, gb, pab, a, att, w_a_b, w_b_b, w_o_b, g2, tm=tile, comm=_x_grads_sibling(grads_ff, FF))
    grads_mix = [dw_a, dw_b, dw_o]
    sums_ff, ff_to_chips = reduce_tail(FF, grads_ff, got_ff)
    (dq, dk, dva, dsk), (*pieces_ff, got_a, got_b, got_o) = _attn_bwd(
        q, k, va, datt, snk, rc, rs1, rs2, comm=_both(ff_to_chips, _x_grads_sibling(grads_mix, MIX)))
    partial_ff = reduce_end(FF, sums_ff, pieces_ff)
    sums_mix, mix_to_chips = reduce_tail(MIX, grads_mix, [got_a, got_b, got_o])
    dx, dproj, dg1, dws, dbs, dlg, dlb = _sgu_inproj_bwd(
        u, vs, da, ln_v_gain, ln_v_bias, w_sp, bfull, [dq, dk, dva, dga, dgb], xs, dx1, g1, w_in_b, tm=tile)
    small = dict(ln_v_gain=dlg, ln_v_bias=dlb, w_spatial=dws, b_spatial=dbs, sinks=dsk[:, :NQ],
                 norm_mix_pre=dg1, norm_mix_post=dg2, norm_ff_pre=dg3, norm_ff_post=dg4)
    dw_in, (gs, shard_ff_in, shard_ff_out, *pieces_mix) = _wgrad(
        h, dproj, name="wgrad_in", vmem=VMEM_PHYSICAL, **wtiles["w_in"],
        comm=_both(_both(_x_small_all_reduce(_pack_small(small, lsum)), _x_grads_share(partial_ff, FF)), mix_to_chips))
    partial_mix = reduce_end(MIX, sums_mix, pieces_mix)
    got_in = _run(_x_grads_sibling([dw_in], FIRST), "grads_in_to_sibling")
    sums_in, to_chips = reduce_tail(FIRST, [dw_in], got_in)
    partial_in = reduce_end(FIRST, sums_in, _run(to_chips, "grads_in_to_chips"))
    g_in, *shard_mix = _run(_x_grads_share(list(partial_in) + list(partial_mix), FIRST + MIX), "grads_in_mix_share")
    shard_rest = list(shard_mix) + [shard_ff_in, shard_ff_out]

    loss = 0.5 * jnp.sum(gs[SMALL_ROWS - 8:]) / D
    grad, delta, new_m, new_v = {}, {}, {}, {}
    for n, g in zip(BIG_NAMES, [g_in] + list(shard_rest)):
        (g_, d_, m_, v_), = _adamw([w[n][0]], [g], [m[n][0]], [v[n][0]], [256], name="adamw_" + n)
        grad[n], delta[n], new_m[n], new_v[n] = g_[None], d_[None], m_[None], v_[None]
    (gs, ds, ms, vs), = _adamw([_pack_small(w)], [gs], [_pack_small(m)], [_pack_small(v)], [SMALL_ROWS], name="adamw_small")
    for packed, dst in ((gs, grad), (ds, delta), (ms, new_m), (vs, new_v)):
        dst.update(_unpack_small(packed, w))

    outs = [loss, dx[None]]
    for group in (grad, delta, new_m, new_v):
        outs.extend(group[n] for n in WEIGHT_ORDER)
    return tuple(outs)
```
